```python
import math
import jax, jax.numpy as jnp
from jax import lax
import numpy as np

D_MODEL = 1024
BATCH = 8
SEQ = 4096
DEPTH = 4

N_MIXERS = 3
RMS_EPS = 1e-6

DN_HEADS = 8
DN_DK = 128
DN_DV = 256
DN_QK_W = DN_HEADS * DN_DK
DN_V_W = DN_HEADS * DN_DV
DN_CONV = 4
DN_CHUNK = 64
DN_CONV_W = 2 * DN_QK_W + DN_V_W
DN_IN = DN_CONV_W + DN_V_W + 2 * DN_HEADS

SB_HEADS = 16
SB_DH = 64
SB_W = SB_HEADS * SB_DH
SB_BLOCK = 128
SB_IN = 4 * SB_W

SC_W = 2 * D_MODEL
SC_CONV = 3
SC_IN = 4 * SC_W

N_DN = (DEPTH + 2) // 3
N_SB = (DEPTH + 1) // 3
N_SC = DEPTH // 3

kernel_name = "interleaved_deltanet_stickbreak_shortconv"


def rms_norm(x, g, eps=RMS_EPS):
    xf = x.astype(jnp.float32)
    y = xf * lax.rsqrt(jnp.mean(xf * xf, axis=-1, keepdims=True) + eps)
    return (y * g.astype(jnp.float32)).astype(x.dtype)


def l2_norm(x, eps=1e-6):
    xf = x.astype(jnp.float32)
    return xf * lax.rsqrt(jnp.sum(xf * xf, axis=-1, keepdims=True) + eps)


def causal_dwconv(x, w):
    K, C = w.shape
    return lax.conv_general_dilated(
        x, w[:, None, :].astype(x.dtype), window_strides=(1,), padding=[(K - 1, 0)],
        dimension_numbers=('NWC', 'WIO', 'NWC'), feature_group_count=C)


def gated_delta_rule(q, k, v, log_a, beta):
    f32 = jnp.float32
    Bn, T, H, dk = q.shape
    dv = v.shape[-1]
    C = DN_CHUNK
    N = T // C

    def chunks(t):
        t = t.astype(f32).reshape((Bn, N, C, H) + t.shape[3:])
        return jnp.moveaxis(t, 3, 1)

    q = chunks(q) * (dk ** -0.5)
    k = chunks(k)
    v = chunks(v)
    beta = chunks(beta)
    g = jnp.cumsum(chunks(log_a), axis=-1)
    causal = jnp.tril(jnp.ones((C, C), bool))
    strict = jnp.tril(jnp.ones((C, C), bool), -1)
    gdiff = g[..., :, None] - g[..., None, :]
    decay = jnp.where(causal, jnp.exp(jnp.where(causal, gdiff, 0.0)), 0.0)

    k_beta = k * beta[..., None]
    L = jnp.where(strict, jnp.einsum('bhncd,bhnsd->bhncs', k_beta, k) * decay, 0.0)
    eye = jnp.eye(C, dtype=f32)
    rhs = jnp.concatenate([v * beta[..., None], k_beta * jnp.exp(g)[..., None]], axis=-1)
    sol = lax.linalg.triangular_solve(L + eye, rhs, left_side=True, lower=True, unit_diagonal=True)
    u, w = sol[..., :dv], sol[..., dv:]

    intra = jnp.where(causal, jnp.einsum('bhncd,bhnsd->bhncs', q, k) * decay, 0.0)
    q_dec = q * jnp.exp(g)[..., None]
    g_last = g[..., -1]
    k_dec = k * jnp.exp(g_last[..., None] - g)[..., None]

    def step(S, xs):
        q_c, k_c, u_c, w_c, a_c, gl = xs
        v_new = u_c - jnp.einsum('bhcd,bhde->bhce', w_c, S)
        o = jnp.einsum('bhcd,bhde->bhce', q_c, S) + jnp.einsum('bhcs,bhse->bhce', a_c, v_new)
        S = S * jnp.exp(gl)[..., None, None] + jnp.einsum('bhcd,bhce->bhde', k_c, v_new)
        return S, o

    xs = tuple(jnp.moveaxis(t, 2, 0) for t in (q_dec, k_dec, u, w, intra, g_last))
    S0 = jnp.zeros((Bn, H, dk, dv), f32)
    _, o = lax.scan(step, S0, xs)
    return jnp.transpose(o, (1, 0, 3, 2, 4)).reshape(Bn, T, H, dv)


def deltanet_mixer(h, w_in, conv_w, a_log, dt_bias, o_norm_g, w_out):
    f32 = jnp.float32
    Bn, T, _ = h.shape
    proj = h @ w_in
    qkv, gate, a_in, b_in = jnp.split(proj, [DN_CONV_W, DN_CONV_W + DN_V_W, DN_CONV_W + DN_V_W + DN_HEADS], axis=-1)
    qkv = jax.nn.silu(causal_dwconv(qkv, conv_w))
    q, k, v = jnp.split(qkv, [DN_QK_W, 2 * DN_QK_W], axis=-1)
    q = l2_norm(q.reshape(Bn, T, DN_HEADS, DN_DK))
    k = l2_norm(k.reshape(Bn, T, DN_HEADS, DN_DK))
    v = v.reshape(Bn, T, DN_HEADS, DN_DV)
    beta = jax.nn.sigmoid(b_in.astype(f32))
    log_a = -jnp.exp(a_log.astype(f32)) * jax.nn.softplus(a_in.astype(f32) + dt_bias.astype(f32))
    o = gated_delta_rule(q, k, v, log_a, beta)
    o = rms_norm(o, o_norm_g) * jax.nn.silu(gate.astype(f32).reshape(Bn, T, DN_HEADS, DN_DV))
    return o.reshape(Bn, T, DN_V_W).astype(h.dtype) @ w_out


def stick_breaking_mixer(h, w_in, q_norm_g, k_norm_g, w_out):
    f32 = jnp.float32
    Bn, T, _ = h.shape
    q, k, v, gate = jnp.split(h @ w_in, 4, axis=-1)
    q = rms_norm(q.reshape(Bn, T, SB_HEADS, SB_DH), q_norm_g)
    k = rms_norm(k.reshape(Bn, T, SB_HEADS, SB_DH), k_norm_g)
    v = v.reshape(Bn, T, SB_HEADS, SB_DH)
    nb = T // SB_BLOCK
    qb = jnp.moveaxis(q.reshape(Bn, nb, SB_BLOCK, SB_HEADS, SB_DH), 1, 0)
    key_pos = jnp.arange(T)
    scale = SB_DH ** -0.5

    def block(args):
        i, q_blk = args
        q_pos = i * SB_BLOCK + jnp.arange(SB_BLOCK)
        z = jnp.einsum('bqhd,bshd->bhqs', q_blk, k, preferred_element_type=f32) * scale
        mask = key_pos[None, :] < q_pos[:, None]
        log1m = jnp.where(mask, -jax.nn.softplus(z), 0.0)
        after = lax.cumsum(log1m, axis=3, reverse=True) - log1m
        wts = jnp.where(mask, jnp.exp(jax.nn.log_sigmoid(z) + after), 0.0)
        return jnp.einsum('bhqs,bshd->bqhd', wts.astype(v.dtype), v)

    o = lax.map(block, (jnp.arange(nb), qb))
    o = jnp.moveaxis(o, 0, 1).reshape(Bn, T, SB_W)
    return (o * jax.nn.silu(gate)) @ w_out


def short_conv_mixer(h, w_in, conv_w, w_out):
    b_gate, c_gate, u, gate = jnp.split(h @ w_in, 4, axis=-1)
    y = b_gate * causal_dwconv(c_gate * u, conv_w)
    return (y * jax.nn.silu(gate)) @ w_out


def _fwd_setup_inputs(seed: int = 0) -> dict:
    key = jax.random.key(seed)
    ks = jax.random.split(key, 16)
    f32 = jnp.float32

    def dense(k, shape, fan_in):
        return jax.random.normal(k, shape, f32) * fan_in ** -0.5

    def gain(k, shape):
        return 1.0 + 0.05 * jax.random.normal(k, shape, f32)

    x = jax.random.normal(ks[0], (BATCH, SEQ, D_MODEL), f32)
    norm_g = gain(ks[1], (DEPTH, D_MODEL))
    dn_w_in = dense(ks[2], (N_DN, D_MODEL, DN_IN), D_MODEL)
    dn_conv_w = dense(ks[3], (N_DN, DN_CONV, DN_CONV_W), DN_CONV)
    dn_a_log = jnp.log(jax.random.uniform(ks[4], (N_DN, DN_HEADS), f32, 1.0, 16.0))
    dt = jnp.exp(jax.random.uniform(ks[5], (N_DN, DN_HEADS), f32, math.log(1e-3), math.log(1e-1)))
    dn_dt_bias = dt + jnp.log(-jnp.expm1(-dt))
    dn_o_norm_g = gain(ks[6], (N_DN, DN_DV))
    dn_w_out = dense(ks[7], (N_DN, DN_V_W, D_MODEL), DN_V_W)
    sb_w_in = dense(ks[8], (N_SB, D_MODEL, SB_IN), D_MODEL)
    sb_q_norm_g = gain(ks[9], (N_SB, SB_DH))
    sb_k_norm_g = gain(ks[10], (N_SB, SB_DH))
    sb_w_out = dense(ks[11], (N_SB, SB_W, D_MODEL), SB_W)
    sc_w_in = dense(ks[12], (N_SC, D_MODEL, SC_IN), D_MODEL)
    sc_conv_w = dense(ks[13], (N_SC, SC_CONV, SC_W), SC_CONV)
    sc_w_out = dense(ks[14], (N_SC, SC_W, D_MODEL), SC_W)
    return {"x": x, "norm_g": norm_g,
            "dn_w_in": dn_w_in, "dn_conv_w": dn_conv_w, "dn_a_log": dn_a_log, "dn_dt_bias": dn_dt_bias,
            "dn_o_norm_g": dn_o_norm_g, "dn_w_out": dn_w_out,
            "sb_w_in": sb_w_in, "sb_q_norm_g": sb_q_norm_g, "sb_k_norm_g": sb_k_norm_g, "sb_w_out": sb_w_out,
            "sc_w_in": sc_w_in, "sc_conv_w": sc_conv_w, "sc_w_out": sc_w_out}


def _fwd_reference(x, norm_g, dn_w_in, dn_conv_w, dn_a_log, dn_dt_bias, dn_o_norm_g, dn_w_out,
              sb_w_in, sb_q_norm_g, sb_k_norm_g, sb_w_out, sc_w_in, sc_conv_w, sc_w_out):
    for i in range(DEPTH):
        h = rms_norm(x, norm_g[i])
        j = i // N_MIXERS
        kind = i % N_MIXERS
        if kind == 0:
            y = deltanet_mixer(h, dn_w_in[j], dn_conv_w[j], dn_a_log[j], dn_dt_bias[j], dn_o_norm_g[j], dn_w_out[j])
        elif kind == 1:
            y = stick_breaking_mixer(h, sb_w_in[j], sb_q_norm_g[j], sb_k_norm_g[j], sb_w_out[j])
        else:
            y = short_conv_mixer(h, sc_w_in[j], sc_conv_w[j], sc_w_out[j])
        x = x + y
    return x


import jax as _jax
import jax.numpy as _jnp

TWIN_FORMAT = 'train_step'
FWD_PARAMS = ['x', 'norm_g', 'dn_w_in', 'dn_conv_w', 'dn_a_log', 'dn_dt_bias', 'dn_o_norm_g', 'dn_w_out', 'sb_w_in', 'sb_q_norm_g', 'sb_k_norm_g', 'sb_w_out', 'sc_w_in', 'sc_conv_w', 'sc_w_out']
TWIN_WEIGHTS = ['norm_g', 'dn_w_in', 'dn_conv_w', 'dn_a_log', 'dn_dt_bias', 'dn_o_norm_g', 'dn_w_out', 'sb_w_in', 'sb_q_norm_g', 'sb_k_norm_g', 'sb_w_out', 'sc_w_in', 'sc_conv_w', 'sc_w_out']
TWIN_DIFF_INPUT = 'x'
TWIN_INPUTS = ['x', 'norm_g', 'dn_w_in', 'dn_conv_w', 'dn_a_log', 'dn_dt_bias', 'dn_o_norm_g', 'dn_w_out', 'sb_w_in', 'sb_q_norm_g', 'sb_k_norm_g', 'sb_w_out', 'sc_w_in', 'sc_conv_w', 'sc_w_out', 'loss_target', 'm_norm_g', 'm_dn_w_in', 'm_dn_conv_w', 'm_dn_a_log', 'm_dn_dt_bias', 'm_dn_o_norm_g', 'm_dn_w_out', 'm_sb_w_in', 'm_sb_q_norm_g', 'm_sb_k_norm_g', 'm_sb_w_out', 'm_sc_w_in', 'm_sc_conv_w', 'm_sc_w_out', 'v_norm_g', 'v_dn_w_in', 'v_dn_conv_w', 'v_dn_a_log', 'v_dn_dt_bias', 'v_dn_o_norm_g', 'v_dn_w_out', 'v_sb_w_in', 'v_sb_q_norm_g', 'v_sb_k_norm_g', 'v_sb_w_out', 'v_sc_w_in', 'v_sc_conv_w', 'v_sc_w_out']
TWIN_OUTPUTS = ['loss', 'grad_x', 'grad_norm_g', 'grad_dn_w_in', 'grad_dn_conv_w', 'grad_dn_a_log', 'grad_dn_dt_bias', 'grad_dn_o_norm_g', 'grad_dn_w_out', 'grad_sb_w_in', 'grad_sb_q_norm_g', 'grad_sb_k_norm_g', 'grad_sb_w_out', 'grad_sc_w_in', 'grad_sc_conv_w', 'grad_sc_w_out', 'delta_norm_g', 'delta_dn_w_in', 'delta_dn_conv_w', 'delta_dn_a_log', 'delta_dn_dt_bias', 'delta_dn_o_norm_g', 'delta_dn_w_out', 'delta_sb_w_in', 'delta_sb_q_norm_g', 'delta_sb_k_norm_g', 'delta_sb_w_out', 'delta_sc_w_in', 'delta_sc_conv_w', 'delta_sc_w_out', 'new_m_norm_g', 'new_m_dn_w_in', 'new_m_dn_conv_w', 'new_m_dn_a_log', 'new_m_dn_dt_bias', 'new_m_dn_o_norm_g', 'new_m_dn_w_out', 'new_m_sb_w_in', 'new_m_sb_q_norm_g', 'new_m_sb_k_norm_g', 'new_m_sb_w_out', 'new_m_sc_w_in', 'new_m_sc_conv_w', 'new_m_sc_w_out', 'new_v_norm_g', 'new_v_dn_w_in', 'new_v_dn_conv_w', 'new_v_dn_a_log', 'new_v_dn_dt_bias', 'new_v_dn_o_norm_g', 'new_v_dn_w_out', 'new_v_sb_w_in', 'new_v_sb_q_norm_g', 'new_v_sb_k_norm_g', 'new_v_sb_w_out', 'new_v_sc_w_in', 'new_v_sc_conv_w', 'new_v_sc_w_out']
TWIN_LEAF_KINDS = {'loss': 'loss', 'grad_x': 'grad_x', 'grad_norm_g': 'grad_w', 'grad_dn_w_in': 'grad_w', 'grad_dn_conv_w': 'grad_w', 'grad_dn_a_log': 'grad_w', 'grad_dn_dt_bias': 'grad_w', 'grad_dn_o_norm_g': 'grad_w', 'grad_dn_w_out': 'grad_w', 'grad_sb_w_in': 'grad_w', 'grad_sb_q_norm_g': 'grad_w', 'grad_sb_k_norm_g': 'grad_w', 'grad_sb_w_out': 'grad_w', 'grad_sc_w_in': 'grad_w', 'grad_sc_conv_w': 'grad_w', 'grad_sc_w_out': 'grad_w', 'delta_norm_g': 'delta_w', 'delta_dn_w_in': 'delta_w', 'delta_dn_conv_w': 'delta_w', 'delta_dn_a_log': 'delta_w', 'delta_dn_dt_bias': 'delta_w', 'delta_dn_o_norm_g': 'delta_w', 'delta_dn_w_out': 'delta_w', 'delta_sb_w_in': 'delta_w', 'delta_sb_q_norm_g': 'delta_w', 'delta_sb_k_norm_g': 'delta_w', 'delta_sb_w_out': 'delta_w', 'delta_sc_w_in': 'delta_w', 'delta_sc_conv_w': 'delta_w', 'delta_sc_w_out': 'delta_w', 'new_m_norm_g': 'new_m', 'new_m_dn_w_in': 'new_m', 'new_m_dn_conv_w': 'new_m', 'new_m_dn_a_log': 'new_m', 'new_m_dn_dt_bias': 'new_m', 'new_m_dn_o_norm_g': 'new_m', 'new_m_dn_w_out': 'new_m', 'new_m_sb_w_in': 'new_m', 'new_m_sb_q_norm_g': 'new_m', 'new_m_sb_k_norm_g': 'new_m', 'new_m_sb_w_out': 'new_m', 'new_m_sc_w_in': 'new_m', 'new_m_sc_conv_w': 'new_m', 'new_m_sc_w_out': 'new_m', 'new_v_norm_g': 'new_v', 'new_v_dn_w_in': 'new_v', 'new_v_dn_conv_w': 'new_v', 'new_v_dn_a_log': 'new_v', 'new_v_dn_dt_bias': 'new_v', 'new_v_dn_o_norm_g': 'new_v', 'new_v_dn_w_out': 'new_v', 'new_v_sb_w_in': 'new_v', 'new_v_sb_q_norm_g': 'new_v', 'new_v_sb_k_norm_g': 'new_v', 'new_v_sb_w_out': 'new_v', 'new_v_sc_w_in': 'new_v', 'new_v_sc_conv_w': 'new_v', 'new_v_sc_w_out': 'new_v'}


def _forward(args):
    return _fwd_reference(*[args[k] for k in FWD_PARAMS])


def _output_shape():
    def fwd():
        inp = _fwd_setup_inputs(0)
        return _fwd_reference(*[inp[k] for k in FWD_PARAMS])
    out = _jax.eval_shape(fwd)
    return out.shape, out.dtype

N_MICROBATCH = 1
ADAM_LR = 0.001
ADAM_B1 = 0.9
ADAM_B2 = 0.999
ADAM_EPS = 1e-08
ADAM_WD = 0.01
ADAM_STEP = 10
PER_EXAMPLE_BATCH_AXIS = {'x': 0, 'loss_target': 0}
SHARED_INPUTS = []
_WEIGHT_DTYPES = {'norm_g': _jnp.float32, 'dn_w_in': _jnp.float32, 'dn_conv_w': _jnp.float32, 'dn_a_log': _jnp.float32, 'dn_dt_bias': _jnp.float32, 'dn_o_norm_g': _jnp.float32, 'dn_w_out': _jnp.float32, 'sb_w_in': _jnp.float32, 'sb_q_norm_g': _jnp.float32, 'sb_k_norm_g': _jnp.float32, 'sb_w_out': _jnp.float32, 'sc_w_in': _jnp.float32, 'sc_conv_w': _jnp.float32, 'sc_w_out': _jnp.float32}
MOMENT_SCALE = {'norm_g': 2.679403e+01, 'dn_w_in': 4.350131e-01, 'dn_conv_w': 6.121656e-01, 'dn_a_log': 4.327514e+01, 'dn_dt_bias': 4.137922e+01, 'dn_o_norm_g': 4.540380e+01, 'dn_w_out': 1.149451e+00, 'sb_w_in': 3.708603e-01, 'sb_q_norm_g': 1.100161e+01, 'sb_k_norm_g': 1.101602e+01, 'sb_w_out': 4.391632e-01, 'sc_w_in': 4.701312e-01, 'sc_conv_w': 3.364541e+00, 'sc_w_out': 3.779647e-01}


def _to_microbatches(a, axis):
    t = _jnp.moveaxis(a, axis, 0)
    t = t.reshape((N_MICROBATCH, t.shape[0] // N_MICROBATCH) + t.shape[1:])
    return _jnp.moveaxis(t, 1, axis + 1)


def setup_inputs(seed: int = 0) -> dict:
    inp = _fwd_setup_inputs(seed)
    key = _jax.random.fold_in(_jax.random.key(seed), 7919)
    shape, _ = _output_shape()
    out = dict(inp)
    out["loss_target"] = _jax.random.normal(_jax.random.fold_in(key, 0), shape, _jnp.float32)
    for i, name in enumerate(TWIN_WEIGHTS):
        w = inp[name].astype(_jnp.float32)
        if MOMENT_SCALE is None:
            s = _jnp.sqrt(_jnp.mean(_jnp.square(w)) + 1e-30)
        else:
            s = MOMENT_SCALE[name]
        km, kv = _jax.random.split(_jax.random.fold_in(key, i + 1))
        out[name] = w
        out["m_" + name] = s * _jax.random.normal(km, w.shape, _jnp.float32)
        out["v_" + name] = (s * s) * _jax.random.uniform(kv, w.shape, _jnp.float32, 0.5, 1.5)
    if N_MICROBATCH > 1:
        for name, axis in PER_EXAMPLE_BATCH_AXIS.items():
            out[name] = _to_microbatches(out[name], axis)
    return {'x': out['x'], 'norm_g': out['norm_g'], 'dn_w_in': out['dn_w_in'], 'dn_conv_w': out['dn_conv_w'], 'dn_a_log': out['dn_a_log'], 'dn_dt_bias': out['dn_dt_bias'], 'dn_o_norm_g': out['dn_o_norm_g'], 'dn_w_out': out['dn_w_out'], 'sb_w_in': out['sb_w_in'], 'sb_q_norm_g': out['sb_q_norm_g'], 'sb_k_norm_g': out['sb_k_norm_g'], 'sb_w_out': out['sb_w_out'], 'sc_w_in': out['sc_w_in'], 'sc_conv_w': out['sc_conv_w'], 'sc_w_out': out['sc_w_out'], 'loss_target': out['loss_target'], 'm_norm_g': out['m_norm_g'], 'm_dn_w_in': out['m_dn_w_in'], 'm_dn_conv_w': out['m_dn_conv_w'], 'm_dn_a_log': out['m_dn_a_log'], 'm_dn_dt_bias': out['m_dn_dt_bias'], 'm_dn_o_norm_g': out['m_dn_o_norm_g'], 'm_dn_w_out': out['m_dn_w_out'], 'm_sb_w_in': out['m_sb_w_in'], 'm_sb_q_norm_g': out['m_sb_q_norm_g'], 'm_sb_k_norm_g': out['m_sb_k_norm_g'], 'm_sb_w_out': out['m_sb_w_out'], 'm_sc_w_in': out['m_sc_w_in'], 'm_sc_conv_w': out['m_sc_conv_w'], 'm_sc_w_out': out['m_sc_w_out'], 'v_norm_g': out['v_norm_g'], 'v_dn_w_in': out['v_dn_w_in'], 'v_dn_conv_w': out['v_dn_conv_w'], 'v_dn_a_log': out['v_dn_a_log'], 'v_dn_dt_bias': out['v_dn_dt_bias'], 'v_dn_o_norm_g': out['v_dn_o_norm_g'], 'v_dn_w_out': out['v_dn_w_out'], 'v_sb_w_in': out['v_sb_w_in'], 'v_sb_q_norm_g': out['v_sb_q_norm_g'], 'v_sb_k_norm_g': out['v_sb_k_norm_g'], 'v_sb_w_out': out['v_sb_w_out'], 'v_sc_w_in': out['v_sc_w_in'], 'v_sc_conv_w': out['v_sc_conv_w'], 'v_sc_w_out': out['v_sc_w_out']}


def _loss(weights, diff, rest, loss_target):
    with _jax.named_scope("forward"):
        args = {**rest, TWIN_DIFF_INPUT: diff, **{k: w.astype(_WEIGHT_DTYPES[k]) for k, w in weights.items()}}
        y = _forward(args)
    with _jax.named_scope("loss_head"):
        err = _jnp.square(y.astype(_jnp.float32) - loss_target)
        return 0.5 * _jnp.sum(_jnp.mean(err, axis=-1)) if err.ndim else 0.5 * err


def _adamw(w, g, m, v):
    m = ADAM_B1 * m + (1.0 - ADAM_B1) * g
    v = ADAM_B2 * v + (1.0 - ADAM_B2) * _jnp.square(g)
    m_hat = m / (1.0 - ADAM_B1 ** ADAM_STEP)
    v_hat = v / (1.0 - ADAM_B2 ** ADAM_STEP)
    delta = -ADAM_LR * (m_hat / (_jnp.sqrt(v_hat) + ADAM_EPS) + ADAM_WD * w)
    return delta, m, v


def reference(x, norm_g, dn_w_in, dn_conv_w, dn_a_log, dn_dt_bias, dn_o_norm_g, dn_w_out, sb_w_in, sb_q_norm_g, sb_k_norm_g, sb_w_out, sc_w_in, sc_conv_w, sc_w_out, loss_target, m_norm_g, m_dn_w_in, m_dn_conv_w, m_dn_a_log, m_dn_dt_bias, m_dn_o_norm_g, m_dn_w_out, m_sb_w_in, m_sb_q_norm_g, m_sb_k_norm_g, m_sb_w_out, m_sc_w_in, m_sc_conv_w, m_sc_w_out, v_norm_g, v_dn_w_in, v_dn_conv_w, v_dn_a_log, v_dn_dt_bias, v_dn_o_norm_g, v_dn_w_out, v_sb_w_in, v_sb_q_norm_g, v_sb_k_norm_g, v_sb_w_out, v_sc_w_in, v_sc_conv_w, v_sc_w_out):
    given = dict(x=x, norm_g=norm_g, dn_w_in=dn_w_in, dn_conv_w=dn_conv_w, dn_a_log=dn_a_log, dn_dt_bias=dn_dt_bias, dn_o_norm_g=dn_o_norm_g, dn_w_out=dn_w_out, sb_w_in=sb_w_in, sb_q_norm_g=sb_q_norm_g, sb_k_norm_g=sb_k_norm_g, sb_w_out=sb_w_out, sc_w_in=sc_w_in, sc_conv_w=sc_conv_w, sc_w_out=sc_w_out, loss_target=loss_target, m_norm_g=m_norm_g, m_dn_w_in=m_dn_w_in, m_dn_conv_w=m_dn_conv_w, m_dn_a_log=m_dn_a_log, m_dn_dt_bias=m_dn_dt_bias, m_dn_o_norm_g=m_dn_o_norm_g, m_dn_w_out=m_dn_w_out, m_sb_w_in=m_sb_w_in, m_sb_q_norm_g=m_sb_q_norm_g, m_sb_k_norm_g=m_sb_k_norm_g, m_sb_w_out=m_sb_w_out, m_sc_w_in=m_sc_w_in, m_sc_conv_w=m_sc_conv_w, m_sc_w_out=m_sc_w_out, v_norm_g=v_norm_g, v_dn_w_in=v_dn_w_in, v_dn_conv_w=v_dn_conv_w, v_dn_a_log=v_dn_a_log, v_dn_dt_bias=v_dn_dt_bias, v_dn_o_norm_g=v_dn_o_norm_g, v_dn_w_out=v_dn_w_out, v_sb_w_in=v_sb_w_in, v_sb_q_norm_g=v_sb_q_norm_g, v_sb_k_norm_g=v_sb_k_norm_g, v_sb_w_out=v_sb_w_out, v_sc_w_in=v_sc_w_in, v_sc_conv_w=v_sc_conv_w, v_sc_w_out=v_sc_w_out)
    weights = {n: given[n] for n in TWIN_WEIGHTS}
    shared = {n: given[n] for n in SHARED_INPUTS}
    per_example = {n: given[n] for n in ['x']}
    grad_fn = _jax.value_and_grad(_loss, argnums=(0, 1))

    def one_microbatch(ex, loss_target):
        ex = dict(ex)
        diff = ex.pop(TWIN_DIFF_INPUT)
        return grad_fn(weights, diff, {**shared, **ex}, loss_target)

    if N_MICROBATCH == 1:
        loss, (grad_w, grad_x) = one_microbatch(per_example, given["loss_target"])
    else:
        def body(carry, xs):
            loss_sum, grad_sum = carry
            l_k, (gw_k, gx_k) = one_microbatch(xs[0], xs[1])
            with _jax.named_scope("update"):
                return (loss_sum + l_k, _jax.tree.map(_jnp.add, grad_sum, gw_k)), gx_k

        init = (_jnp.zeros((), _jnp.float32), _jax.tree.map(_jnp.zeros_like, weights))
        (loss, grad_w), grad_x = _jax.lax.scan(body, init, (per_example, given["loss_target"]))
    with _jax.named_scope("update"):
        delta_w, new_m, new_v = {}, {}, {}
        for n in TWIN_WEIGHTS:
            delta_w[n], new_m[n], new_v[n] = _adamw(weights[n], grad_w[n], given["m_" + n], given["v_" + n])
    return (loss, grad_x, *[grad_w[n] for n in TWIN_WEIGHTS], *[delta_w[n] for n in TWIN_WEIGHTS],
            *[new_m[n] for n in TWIN_WEIGHTS], *[new_v[n] for n in TWIN_WEIGHTS])
```

```python
import functools
import math

import jax
import jax.numpy as jnp
from jax import lax
from jax.experimental import pallas as pl
from jax.experimental.pallas import tpu as pltpu

F32 = jnp.float32
BF16 = jnp.bfloat16
HIGHEST = lax.Precision.HIGHEST

N_DEV = 8
D_MODEL = 1024
RMS_EPS = 1e-6
L2_EPS = 1e-6

DN_HEADS = 8
DN_DK = 128
DN_DV = 256
DN_QK_W = DN_HEADS * DN_DK
DN_V_W = DN_HEADS * DN_DV
DN_CONV = 4
DN_CHUNK = 64
DN_CONV_W = 2 * DN_QK_W + DN_V_W
DN_IN = DN_CONV_W + DN_V_W + 2 * DN_HEADS
DN_AB_PAD = 128
DN_HEADS_PER_STEP = 2

SB_HEADS = 16
SB_DH = 64
SB_W = SB_HEADS * SB_DH
SB_PAIRS = SB_HEADS // 2
SB_TQ = 256
SB_TK = 128

SC_W = 2 * D_MODEL
SC_CONV = 3
SC_BLK = 512
SC_NBLK = SC_W // SC_BLK

ADAM_LR = 0.001
ADAM_B1 = 0.9
ADAM_B2 = 0.999
ADAM_EPS = 1e-08
ADAM_WD = 0.01
ADAM_STEP = 10

LANE = 128
SUBLANE = 8
HALO = SUBLANE
ROW_TILE = 256
VMEM_LIMIT = 48 * 2 ** 20

NN = ((1,), (0,))
NT = ((1,), (1,))
TN = ((0,), (0,))


def _dot(a, b, dims=NN, precision=None):
    return lax.dot_general(a, b, (dims, ((), ())), precision=precision, preferred_element_type=F32)


def _bdot(a, b, dims=NN):
    return _dot(a.astype(BF16), b.astype(BF16), dims)


def _hdot(a, b, dims=NN):
    return _dot(a, b, dims, precision=HIGHEST)


def _tile(dim, pref, align=LANE):
    t = (min(pref, dim) // align) * align
    while t >= align:
        if dim % t == 0:
            return t
        t -= align
    return dim


def _params(*sem):
    return pltpu.CompilerParams(dimension_semantics=sem, vmem_limit_bytes=VMEM_LIMIT)


def _sigmoid(x):
    return 1.0 / (1.0 + jnp.exp(-x))


def _softplus(x):
    return jnp.maximum(x, 0.0) + jnp.log(1.0 + jnp.exp(-jnp.abs(x)))


def _silu_and_grad(x):
    s = _sigmoid(x)
    return x * s, s * (1.0 + x * (1.0 - s))


def _iota2(shape, dim):
    return lax.broadcasted_iota(jnp.int32, shape, dim)


def _matmul(a, b, mode, name, out_dtype=F32, add=None, tm=512, tn=1024, tk=1024):
    if mode == "nn":
        (M, K), (K2, N) = a.shape, b.shape
    elif mode == "nt":
        (M, K), (N, K2) = a.shape, b.shape
    else:
        (K, M), (K2, N) = a.shape, b.shape
    assert K == K2, (a.shape, b.shape, mode)
    tm, tn, tk = _tile(M, tm), _tile(N, tn), _tile(K, tk)
    nk = K // tk
    dims = {"nn": NN, "nt": NT, "tn": TN}[mode]
    a_spec = pl.BlockSpec((tk, tm), lambda i, j, k: (k, i)) if mode == "tn" else pl.BlockSpec((tm, tk), lambda i, j, k: (i, k))
    b_spec = pl.BlockSpec((tn, tk), lambda i, j, k: (j, k)) if mode == "nt" else pl.BlockSpec((tk, tn), lambda i, j, k: (k, j))
    o_spec = pl.BlockSpec((tm, tn), lambda i, j, k: (i, j))
    has_add = add is not None

    def body(*refs):
        a_ref, b_ref = refs[0], refs[1]
        add_ref = refs[2] if has_add else None
        o_ref = refs[3] if has_add else refs[2]
        p = _bdot(a_ref[...], b_ref[...], dims)

        def finish(acc):
            if has_add:
                acc = acc + add_ref[...]
            o_ref[...] = acc.astype(out_dtype)

        if nk == 1:
            finish(p)
        else:
            acc_ref = refs[-1]
            k = pl.program_id(2)

            @pl.when(k == 0)
            def _():
                acc_ref[...] = p

            @pl.when(k > 0)
            def _():
                acc_ref[...] += p

            @pl.when(k == nk - 1)
            def _():
                finish(acc_ref[...])

    in_specs = [a_spec, b_spec] + ([o_spec] if has_add else [])
    args = (a, b) + ((add,) if has_add else ())
    return pl.pallas_call(
        body, name=name, grid=(M // tm, N // tn, nk),
        in_specs=in_specs, out_specs=o_spec,
        out_shape=jax.ShapeDtypeStruct((M, N), out_dtype),
        scratch_shapes=[pltpu.VMEM((tm, tn), F32)] if nk > 1 else [],
        compiler_params=_params("parallel", "parallel", "arbitrary"),
    )(*args)


def _rmsnorm_fwd(x, g, name):
    T, D = x.shape
    tt = _tile(T, 512, SUBLANE)

    def body(x_ref, g_ref, o_ref):
        xv = x_ref[...]
        r = lax.rsqrt(jnp.mean(xv * xv, axis=-1, keepdims=True) + RMS_EPS)
        o_ref[...] = (xv * r * g_ref[...]).astype(BF16)

    return pl.pallas_call(
        body, name=name, grid=(T // tt,),
        in_specs=[pl.BlockSpec((tt, D), lambda i: (i, 0)), pl.BlockSpec((1, D), lambda i: (0, 0))],
        out_specs=pl.BlockSpec((tt, D), lambda i: (i, 0)),
        out_shape=jax.ShapeDtypeStruct((T, D), BF16),
        compiler_params=_params("parallel"),
    )(x, g)


def _rmsnorm_bwd(dh, x, g, dx_res, name):
    T, D = x.shape
    tt = _tile(T, 256, SUBLANE)

    def body(dh_ref, x_ref, g_ref, res_ref, dx_ref, dg_ref):
        xv, dhv = x_ref[...], dh_ref[...]
        r = lax.rsqrt(jnp.mean(xv * xv, axis=-1, keepdims=True) + RMS_EPS)
        xh = xv * r
        dxh = dhv * g_ref[...]
        m = jnp.mean(dxh * xh, axis=-1, keepdims=True)
        dx_ref[...] = res_ref[...] + r * (dxh - xh * m)
        part = jnp.sum(dhv * xh, axis=0, keepdims=True)

        @pl.when(pl.program_id(0) == 0)
        def _():
            dg_ref[...] = part

        @pl.when(pl.program_id(0) > 0)
        def _():
            dg_ref[...] += part

    row = pl.BlockSpec((tt, D), lambda i: (i, 0))
    vec = pl.BlockSpec((1, D), lambda i: (0, 0))
    return pl.pallas_call(
        body, name=name, grid=(T // tt,),
        in_specs=[row, row, vec, row], out_specs=[row, vec],
        out_shape=[jax.ShapeDtypeStruct((T, D), F32), jax.ShapeDtypeStruct((1, D), F32)],
        compiler_params=_params("arbitrary"),
    )(dh, x, g, dx_res)


def _loss_head(y, target, name="loss_head"):
    T, D = y.shape
    tt = _tile(T, 512, SUBLANE)

    def body(y_ref, t_ref, dy_ref, l_ref):
        e = y_ref[...] - t_ref[...]
        dy_ref[...] = e * (1.0 / D)
        s = jnp.sum(jnp.sum(e * e, axis=1, keepdims=True), axis=0, keepdims=True) * (0.5 / D)
        s = jnp.broadcast_to(s, (1, LANE))

        @pl.when(pl.program_id(0) == 0)
        def _():
            l_ref[...] = s

        @pl.when(pl.program_id(0) > 0)
        def _():
            l_ref[...] += s

    row = pl.BlockSpec((tt, D), lambda i: (i, 0))
    return pl.pallas_call(
        body, name=name, grid=(T // tt,),
        in_specs=[row, row], out_specs=[row, pl.BlockSpec((1, LANE), lambda i: (0, 0))],
        out_shape=[jax.ShapeDtypeStruct((T, D), F32), jax.ShapeDtypeStruct((1, LANE), F32)],
        compiler_params=_params("arbitrary"),
    )(y, target)


def _down(x, k):
    return pltpu.roll(x, k, 0) if k else x


def _up(x, k):
    return pltpu.roll(x, x.shape[0] - k, 0) if k else x


def _sc_fwd(proj, conv_w, name):
    T = proj.shape[0]
    tt = _tile(T, ROW_TILE, SUBLANE)
    B = SC_BLK

    def body(p_ref, ph_ref, w_ref, o_ref):
        i = pl.program_id(0)
        keep = (i > 0).astype(F32)
        c = jnp.concatenate([ph_ref[:, B:2 * B] * keep, p_ref[:, B:2 * B]], axis=0)
        u = jnp.concatenate([ph_ref[:, 2 * B:3 * B] * keep, p_ref[:, 2 * B:3 * B]], axis=0)
        z = c * u
        cz = (w_ref[2:3, :] * z + w_ref[1:2, :] * _down(z, 1) + w_ref[0:1, :] * _down(z, 2))[HALO:]
        gate = p_ref[:, 3 * B:4 * B]
        o_ref[...] = (p_ref[:, 0:B] * cz * (gate * _sigmoid(gate))).astype(BF16)

    return pl.pallas_call(
        body, name=name, grid=(T // tt, SC_NBLK),
        in_specs=[pl.BlockSpec((tt, 4 * B), lambda i, j: (i, j)),
                  pl.BlockSpec((HALO, 4 * B), lambda i, j: (jnp.maximum(i * (tt // HALO) - 1, 0), j)),
                  pl.BlockSpec((SC_CONV, B), lambda i, j: (0, j))],
        out_specs=pl.BlockSpec((tt, B), lambda i, j: (i, j)),
        out_shape=jax.ShapeDtypeStruct((T, SC_W), BF16),
        compiler_params=_params("parallel", "parallel"),
    )(proj, proj, conv_w)


def _sc_bwd(dyg, proj, conv_w, name):
    T = proj.shape[0]
    tt = _tile(T, ROW_TILE, SUBLANE)
    nt = T // tt
    B = SC_BLK
    hb = tt // HALO

    def body(d_ref, dn_ref, p_ref, pp_ref, pn_ref, w_ref, o_ref, dw_ref):
        i = pl.program_id(1)
        keep_p = (i > 0).astype(F32)
        keep_n = (i < nt - 1).astype(F32)

        def ext(k):
            s = slice(k * B, (k + 1) * B)
            return jnp.concatenate([pp_ref[:, s] * keep_p, p_ref[:, s], pn_ref[:, s]], axis=0)

        b, c, u, gate = ext(0), ext(1), ext(2), ext(3)
        dyg_e = jnp.concatenate([jnp.zeros((HALO, B), F32), d_ref[...], dn_ref[...] * keep_n], axis=0)
        w0, w1, w2 = w_ref[0:1, :], w_ref[1:2, :], w_ref[2:3, :]
        z = c * u
        z1, z2 = _down(z, 1), _down(z, 2)
        cz = w2 * z + w1 * z1 + w0 * z2
        sg, dsg = _silu_and_grad(gate)
        dy = dyg_e * sg
        dgate = dyg_e * (b * cz) * dsg
        db = dy * cz
        dcz = dy * b
        dz = w2 * dcz + w1 * _up(dcz, 1) + w0 * _up(dcz, 2)
        main = slice(HALO, HALO + tt)
        o_ref[:, 0:B] = db[main]
        o_ref[:, B:2 * B] = (dz * u)[main]
        o_ref[:, 2 * B:3 * B] = (dz * c)[main]
        o_ref[:, 3 * B:4 * B] = dgate[main]
        dcm = dcz[main]
        part = jnp.concatenate([jnp.sum(dcm * z2[main], axis=0, keepdims=True),
                                jnp.sum(dcm * z1[main], axis=0, keepdims=True),
                                jnp.sum(dcm * z[main], axis=0, keepdims=True)], axis=0)

        @pl.when(i == 0)
        def _():
            dw_ref[...] = part

        @pl.when(i > 0)
        def _():
            dw_ref[...] += part

    return pl.pallas_call(
        body, name=name, grid=(SC_NBLK, nt),
        in_specs=[pl.BlockSpec((tt, B), lambda j, i: (i, j)),
                  pl.BlockSpec((HALO, B), lambda j, i: (jnp.minimum((i + 1) * hb, nt * hb - 1), j)),
                  pl.BlockSpec((tt, 4 * B), lambda j, i: (i, j)),
                  pl.BlockSpec((HALO, 4 * B), lambda j, i: (jnp.maximum(i * hb - 1, 0), j)),
                  pl.BlockSpec((HALO, 4 * B), lambda j, i: (jnp.minimum((i + 1) * hb, nt * hb - 1), j)),
                  pl.BlockSpec((SC_CONV, B), lambda j, i: (0, j))],
        out_specs=[pl.BlockSpec((tt, 4 * B), lambda j, i: (i, j)), pl.BlockSpec((SC_CONV, B), lambda j, i: (0, j))],
        out_shape=[jax.ShapeDtypeStruct((T, 4 * SC_W), F32), jax.ShapeDtypeStruct((SC_CONV, SC_W), F32)],
        compiler_params=_params("parallel", "arbitrary"),
    )(dyg, dyg, proj, proj, proj, conv_w)


def _sc_perm(w_in):
    d = w_in.shape[0]
    return w_in.reshape(d, 4, SC_NBLK, SC_BLK).transpose(0, 2, 1, 3).reshape(d, 4 * SC_W)


def _sc_unperm(w):
    d = w.shape[0]
    return w.reshape(d, SC_NBLK, 4, SC_BLK).transpose(0, 2, 1, 3).reshape(d, 4 * SC_W)


def _sb_perm(w_in):
    d = w_in.shape[0]
    return w_in.reshape(d, 4, SB_PAIRS, LANE).transpose(0, 2, 1, 3).reshape(d, 4 * SB_W)


def _sb_unperm(w):
    d = w.shape[0]
    return w.reshape(d, SB_PAIRS, 4, LANE).transpose(0, 2, 1, 3).reshape(d, 4 * SB_W)


def _split3_dot(x, m):
    hi = x.astype(BF16)
    r1 = x - hi.astype(F32)
    mid = r1.astype(BF16)
    lo = (r1 - mid.astype(F32)).astype(BF16)
    return _dot(hi, m) + _dot(mid, m) + _dot(lo, m)


def _split2_dot(x, m):
    hi = x.astype(BF16)
    lo = (x - hi.astype(F32)).astype(BF16)
    return _dot(hi, m) + _dot(lo, m)


def _head_mean_matrix():
    r, c = _iota2((LANE, LANE), 0), _iota2((LANE, LANE), 1)
    return jnp.where((r // SB_DH) == (c // SB_DH), 1.0 / SB_DH, 0.0).astype(BF16)


def _sb_prep(proj, qg2, kg2, name):
    T = proj.shape[0]
    tt = _tile(T, ROW_TILE, SUBLANE)

    def body(p_ref, qg_ref, kg_ref, q_ref, k_ref, v_ref):
        bd = _head_mean_matrix()

        def norm(x, g):
            r = lax.rsqrt(_split3_dot(x * x, bd) + RMS_EPS)
            return (x * r * g).astype(BF16)

        q_ref[...] = norm(p_ref[:, 0:LANE], qg_ref[...])
        k_ref[...] = norm(p_ref[:, LANE:2 * LANE], kg_ref[...])
        v_ref[...] = p_ref[:, 2 * LANE:3 * LANE].astype(BF16)

    blk = pl.BlockSpec((tt, LANE), lambda i, p: (i, p))
    vec = pl.BlockSpec((1, LANE), lambda i, p: (0, 0))
    return pl.pallas_call(
        body, name=name, grid=(T // tt, SB_PAIRS),
        in_specs=[pl.BlockSpec((tt, 4 * LANE), lambda i, p: (i, p)), vec, vec],
        out_specs=[blk, blk, blk],
        out_shape=[jax.ShapeDtypeStruct((T, SB_W), BF16)] * 3,
        compiler_params=_params("parallel", "parallel"),
    )(proj, qg2, kg2)


def _sb_prep_bwd(proj, dqn, dkn, dv, dgate, qg2, kg2, name):
    T = proj.shape[0]
    tt = _tile(T, ROW_TILE, SUBLANE)

    def body(p_ref, dq_ref, dk_ref, dv_ref, dg_ref, qg_ref, kg_ref, o_ref, dqg_ref, dkg_ref):
        i = pl.program_id(1)
        bd = _head_mean_matrix()

        def norm_bwd(x, g, dy):
            r = lax.rsqrt(_split3_dot(x * x, bd) + RMS_EPS)
            xh = x * r
            dxh = dy * g
            m = _split3_dot(dxh * xh, bd)
            return r * (dxh - xh * m), jnp.sum(dy * xh, axis=0, keepdims=True)

        dxq, pq = norm_bwd(p_ref[:, 0:LANE], qg_ref[...], dq_ref[...])
        dxk, pk = norm_bwd(p_ref[:, LANE:2 * LANE], kg_ref[...], dk_ref[...])
        o_ref[:, 0:LANE] = dxq
        o_ref[:, LANE:2 * LANE] = dxk
        o_ref[:, 2 * LANE:3 * LANE] = dv_ref[...]
        o_ref[:, 3 * LANE:4 * LANE] = dg_ref[...]

        @pl.when(i == 0)
        def _():
            dqg_ref[...] = pq
            dkg_ref[...] = pk

        @pl.when(i > 0)
        def _():
            dqg_ref[...] += pq
            dkg_ref[...] += pk

    blk = pl.BlockSpec((tt, LANE), lambda p, i: (i, p))
    vec = pl.BlockSpec((1, LANE), lambda p, i: (0, 0))
    acc = pl.BlockSpec((None, 1, LANE), lambda p, i: (p, 0, 0))
    wide = pl.BlockSpec((tt, 4 * LANE), lambda p, i: (i, p))
    return pl.pallas_call(
        body, name=name, grid=(SB_PAIRS, T // tt),
        in_specs=[wide, blk, blk, blk, blk, vec, vec],
        out_specs=[wide, acc, acc],
        out_shape=[jax.ShapeDtypeStruct((T, 4 * SB_W), F32)] + [jax.ShapeDtypeStruct((SB_PAIRS, 1, LANE), F32)] * 2,
        compiler_params=_params("parallel", "arbitrary"),
    )(proj, dqn, dkn, dv, dgate, qg2, kg2)


def _fold_heads(part, name):
    def body(p_ref, o_ref):
        r, c = _iota2((LANE, SB_DH), 0), _iota2((LANE, SB_DH), 1)
        fold = jnp.where((r % SB_DH) == c, 1.0, 0.0).astype(F32)
        o_ref[...] = jnp.sum(_hdot(p_ref[...], fold), axis=0, keepdims=True)

    return pl.pallas_call(body, name=name, out_shape=jax.ShapeDtypeStruct((1, SB_DH), F32))(part)


def _sb_masks():
    lane = _iota2((1, LANE), 1)
    return lane < SB_DH


def _sb_scores(qh, k2, mask, accl, upper):
    z = _dot(qh, k2, NT) * (SB_DH ** -0.5)
    sp = _softplus(z)
    l = jnp.where(mask, -sp, 0.0)
    a = _split2_dot(l, upper) + accl
    return z, sp, l, a


def _sb_attn_fwd(qn, kn, vb, proj, name):
    T = qn.shape[0]
    tq, tk = _tile(T, SB_TQ, SUBLANE), SB_TK
    assert tq % tk == 0

    def body(q_ref, k_ref, v_ref, g_ref, o_ref, og_ref, lt_ref):
        i = pl.program_id(1)
        ma = _sb_masks()
        q2 = q_ref[...]
        zero = jnp.zeros_like(q2)
        qs = (jnp.where(ma, q2, zero), jnp.where(ma, zero, q2))
        upper = (_iota2((tk, tk), 0) > _iota2((tk, tk), 1)).astype(BF16)
        qpos = i * tq + _iota2((tq, tk), 0)
        nk = (i + 1) * (tq // tk)

        def step(kk, carry):
            acc, la, lb = carry
            kb = nk - 1 - kk
            rows = pl.ds(pl.multiple_of(kb * tk, tk), tk)
            k2, v2 = k_ref[rows, :], v_ref[rows, :]
            zv = jnp.zeros_like(v2)
            vs = (jnp.where(ma, v2, zv), jnp.where(ma, zv, v2))
            mask = (kb * tk + _iota2((tq, tk), 1)) < qpos
            ls = []
            for h, accl in enumerate((la, lb)):
                z, sp, l, a = _sb_scores(qs[h], k2, mask, accl, upper)
                w = jnp.where(mask, jnp.exp(z - sp + a), 0.0)
                acc = acc + _dot(w.astype(BF16), vs[h])
                ls.append(accl + jnp.sum(l, axis=1, keepdims=True))
            return acc, ls[0], ls[1]

        z1 = jnp.zeros((tq, 1), F32)
        acc, la, lb = lax.fori_loop(0, nk, step, (jnp.zeros((tq, LANE), F32), z1, z1))
        gate = g_ref[...]
        o_ref[...] = acc
        og_ref[...] = (acc * (gate * _sigmoid(gate))).astype(BF16)
        lt_ref[...] = jnp.where(_iota2((tq, 2), 1) == 0, la, lb)

    qblk = pl.BlockSpec((tq, LANE), lambda p, i: (i, p))
    full = pl.BlockSpec((T, LANE), lambda p, i: (0, p))
    return pl.pallas_call(
        body, name=name, grid=(SB_PAIRS, T // tq),
        in_specs=[qblk, full, full, pl.BlockSpec((tq, LANE), lambda p, i: (i, 4 * p + 3))],
        out_specs=[qblk, qblk, pl.BlockSpec((None, tq, 2), lambda p, i: (p, i, 0))],
        out_shape=[jax.ShapeDtypeStruct((T, SB_W), F32), jax.ShapeDtypeStruct((T, SB_W), BF16),
                   jax.ShapeDtypeStruct((SB_PAIRS, T, 2), F32)],
        compiler_params=_params("parallel", "parallel"),
    )(qn, kn, vb, proj)


def _sb_attn_bwd(qn, kn, vb, dog, o, ltot, proj, name):
    T = qn.shape[0]
    tq, tk = _tile(T, SB_TQ, SUBLANE), SB_TK

    def body(q_ref, k_ref, v_ref, dog_ref, o_ref, lt_ref, g_ref, dq_ref, dk_ref, dv_ref, dgate_ref):
        i = pl.program_id(1)

        @pl.when(i == 0)
        def _():
            dk_ref[...] = jnp.zeros_like(dk_ref)
            dv_ref[...] = jnp.zeros_like(dv_ref)

        ma = _sb_masks()
        gate, o2, dog2 = g_ref[...], o_ref[...], dog_ref[...]
        sg, dsg = _silu_and_grad(gate)
        do2 = dog2 * sg
        dgate_ref[...] = dog2 * o2 * dsg
        lt = lt_ref[...]
        first = _iota2((tq, 2), 1) == 0
        ltots = (jnp.sum(jnp.where(first, lt, 0.0), axis=1, keepdims=True),
                 jnp.sum(jnp.where(first, 0.0, lt), axis=1, keepdims=True))
        q2 = q_ref[...]
        zq = jnp.zeros_like(q2)
        qs = (jnp.where(ma, q2, zq), jnp.where(ma, zq, q2))
        dob = do2.astype(BF16)
        dos = (jnp.where(ma, dob, zq), jnp.where(ma, zq, dob))
        upto = (_iota2((tk, tk), 0) <= _iota2((tk, tk), 1)).astype(BF16)
        before = (_iota2((tk, tk), 0) < _iota2((tk, tk), 1)).astype(BF16)
        qpos = i * tq + _iota2((tq, tk), 0)
        nk = (i + 1) * (tq // tk)
        scale = SB_DH ** -0.5

        def step(kb, carry):
            dq, la, lb, ea, eb = carry
            rows = pl.ds(pl.multiple_of(kb * tk, tk), tk)
            k2, v2 = k_ref[rows, :], v_ref[rows, :]
            zk = jnp.zeros_like(k2)
            ks = (jnp.where(ma, k2, zk), jnp.where(ma, zk, k2))
            mask = (kb * tk + _iota2((tq, tk), 1)) < qpos
            dk_blk = jnp.zeros((tk, LANE), F32)
            dv_blk = jnp.zeros((tk, LANE), F32)
            ls, es = [], []
            for h, (accl, acce) in enumerate(((la, ea), (lb, eb))):
                z = _dot(qs[h], k2, NT) * scale
                sp = _softplus(z)
                l = jnp.where(mask, -sp, 0.0)
                a = ltots[h] - (accl + _split3_dot(l, upto))
                p = jnp.exp(z - sp)
                w = jnp.where(mask, jnp.exp(z - sp + a), 0.0)
                dw = _dot(dos[h], v2, NT)
                e = dw * w
                big_e = acce + _split2_dot(e, before)
                dz = (jnp.where(mask, e * (1.0 - p) - big_e * p, 0.0) * scale).astype(BF16)
                dq = dq + _dot(dz, ks[h])
                dk_blk = dk_blk + _dot(dz, qs[h], TN)
                dv_blk = dv_blk + _dot(w.astype(BF16), dos[h], TN)
                ls.append(accl + jnp.sum(l, axis=1, keepdims=True))
                es.append(acce + jnp.sum(e, axis=1, keepdims=True))
            dk_ref[rows, :] += dk_blk
            dv_ref[rows, :] += dv_blk
            return dq, ls[0], ls[1], es[0], es[1]

        z1 = jnp.zeros((tq, 1), F32)
        dq, _, _, _, _ = lax.fori_loop(0, nk, step, (jnp.zeros((tq, LANE), F32), z1, z1, z1, z1))
        dq_ref[...] = dq

    qblk = pl.BlockSpec((tq, LANE), lambda p, i: (i, p))
    full = pl.BlockSpec((T, LANE), lambda p, i: (0, p))
    return pl.pallas_call(
        body, name=name, grid=(SB_PAIRS, T // tq),
        in_specs=[qblk, full, full, qblk, qblk, pl.BlockSpec((None, tq, 2), lambda p, i: (p, i, 0)),
                  pl.BlockSpec((tq, LANE), lambda p, i: (i, 4 * p + 3))],
        out_specs=[qblk, full, full, qblk],
        out_shape=[jax.ShapeDtypeStruct((T, SB_W), F32)] * 4,
        compiler_params=_params("parallel", "arbitrary"),
    )(qn, kn, vb, dog, o, ltot, proj)


def _dn_conv(ext, w_ref):
    return (w_ref[3:4, :] * ext + w_ref[2:3, :] * _down(ext, 1) + w_ref[1:2, :] * _down(ext, 2)
            + w_ref[0:1, :] * _down(ext, 3))


def _dn_prep(pqkv, conv_w, name):
    T, W = pqkv.shape
    tt = _tile(T, ROW_TILE, SUBLANE)
    nq = DN_QK_W // LANE

    def body(p_ref, ph_ref, w_ref, o_ref):
        i, cb = pl.program_id(0), pl.program_id(1)
        keep = (i > 0).astype(F32)
        ext = jnp.concatenate([ph_ref[...] * keep, p_ref[...]], axis=0)
        c = _dn_conv(ext, w_ref)[HALO:]
        a = c * _sigmoid(c)
        r = lax.rsqrt(jnp.sum(a * a, axis=-1, keepdims=True) + L2_EPS)
        scale = jnp.where(cb < nq, DN_DK ** -0.5, 1.0)
        o_ref[...] = jnp.where(cb < 2 * nq, a * r * scale, a)

    return pl.pallas_call(
        body, name=name, grid=(T // tt, W // LANE),
        in_specs=[pl.BlockSpec((tt, LANE), lambda i, c: (i, c)),
                  pl.BlockSpec((HALO, LANE), lambda i, c: (jnp.maximum(i * (tt // HALO) - 1, 0), c)),
                  pl.BlockSpec((DN_CONV, LANE), lambda i, c: (0, c))],
        out_specs=pl.BlockSpec((tt, LANE), lambda i, c: (i, c)),
        out_shape=jax.ShapeDtypeStruct((T, W), F32),
        compiler_params=_params("parallel", "parallel"),
    )(pqkv, pqkv, conv_w)


def _dn_prep_bwd(pqkv, conv_w, dact, name):
    T, W = pqkv.shape
    tt = _tile(T, ROW_TILE, SUBLANE)
    nt = T // tt
    hb = tt // HALO
    nq = DN_QK_W // LANE

    def body(p_ref, pp_ref, pn_ref, w_ref, d_ref, dn_ref, o_ref, dw_ref):
        cb, i = pl.program_id(0), pl.program_id(1)
        keep_p = (i > 0).astype(F32)
        keep_n = (i < nt - 1).astype(F32)
        ext = jnp.concatenate([pp_ref[...] * keep_p, p_ref[...], pn_ref[...]], axis=0)
        c = _dn_conv(ext, w_ref)
        s = _sigmoid(c)
        a = c * s
        da_dc = s * (1.0 + c * (1.0 - s))
        d_up = jnp.concatenate([jnp.zeros((HALO, LANE), F32), d_ref[...], dn_ref[...] * keep_n], axis=0)
        r = lax.rsqrt(jnp.sum(a * a, axis=-1, keepdims=True) + L2_EPS)
        y = a * r
        dy = d_up * jnp.where(cb < nq, DN_DK ** -0.5, 1.0)
        da_norm = r * (dy - y * jnp.sum(dy * y, axis=-1, keepdims=True))
        dc = jnp.where(cb < 2 * nq, da_norm, d_up) * da_dc
        dp = (w_ref[3:4, :] * dc + w_ref[2:3, :] * _up(dc, 1) + w_ref[1:2, :] * _up(dc, 2) + w_ref[0:1, :] * _up(dc, 3))
        main = slice(HALO, HALO + tt)
        o_ref[...] = dp[main]
        dcm = dc[main]
        part = jnp.concatenate([jnp.sum(dcm * _down(ext, 3 - k)[main], axis=0, keepdims=True) for k in range(DN_CONV)], axis=0)

        @pl.when(i == 0)
        def _():
            dw_ref[...] = part

        @pl.when(i > 0)
        def _():
            dw_ref[...] += part

    main_spec = pl.BlockSpec((tt, LANE), lambda c, i: (i, c))
    prev_spec = pl.BlockSpec((HALO, LANE), lambda c, i: (jnp.maximum(i * hb - 1, 0), c))
    next_spec = pl.BlockSpec((HALO, LANE), lambda c, i: (jnp.minimum((i + 1) * hb, nt * hb - 1), c))
    w_spec = pl.BlockSpec((DN_CONV, LANE), lambda c, i: (0, c))
    return pl.pallas_call(
        body, name=name, grid=(W // LANE, nt),
        in_specs=[main_spec, prev_spec, next_spec, w_spec, main_spec, next_spec],
        out_specs=[main_spec, w_spec],
        out_shape=[jax.ShapeDtypeStruct((T, W), F32), jax.ShapeDtypeStruct((DN_CONV, W), F32)],
        compiler_params=_params("parallel", "arbitrary"),
    )(pqkv, pqkv, pqkv, conv_w, dact, dact)


def _dn_gates(a_in, b_in, a_log, dt_bias, name):
    T, H = a_in.shape
    C = DN_CHUNK

    def body(a_ref, b_ref, al_ref, dt_ref, g_ref, beta_ref):
        beta_ref[...] = _sigmoid(b_ref[...])
        g_ref[...] = -jnp.exp(al_ref[...]) * _softplus(a_ref[...] + dt_ref[...])
        tri = (_iota2((C, C), 0) >= _iota2((C, C), 1)).astype(F32)

        def chunk(n, carry):
            rows = pl.ds(pl.multiple_of(n * C, C), C)
            g_ref[rows, :] = _hdot(tri, g_ref[rows, :])
            return carry

        lax.fori_loop(0, T // C, chunk, 0)

    return pl.pallas_call(body, name=name, out_shape=[jax.ShapeDtypeStruct((T, H), F32)] * 2)(a_in, b_in, a_log, dt_bias)


def _dn_gates_bwd(dg, dbeta, a_in, b_in, a_log, dt_bias, name):
    T, H = a_in.shape
    C = DN_CHUNK

    def body(dg_ref, db_ref, a_ref, b_ref, al_ref, dt_ref, da_ref, dbi_ref, dal_ref, ddt_ref):
        tri_t = (_iota2((C, C), 0) <= _iota2((C, C), 1)).astype(F32)

        def chunk(n, carry):
            rows = pl.ds(pl.multiple_of(n * C, C), C)
            da_ref[rows, :] = _hdot(tri_t, dg_ref[rows, :])
            return carry

        lax.fori_loop(0, T // C, chunk, 0)
        dla = da_ref[...]
        x = a_ref[...] + dt_ref[...]
        ea = jnp.exp(al_ref[...])
        da = dla * (-ea) * _sigmoid(x)
        da_ref[...] = da
        dal_ref[...] = jnp.sum(dla * (-ea * _softplus(x)), axis=0, keepdims=True)
        ddt_ref[...] = jnp.sum(da, axis=0, keepdims=True)
        beta = _sigmoid(b_ref[...])
        dbi_ref[...] = db_ref[...] * beta * (1.0 - beta)

    return pl.pallas_call(
        body, name=name,
        out_shape=[jax.ShapeDtypeStruct((T, H), F32)] * 2 + [jax.ShapeDtypeStruct((1, H), F32)] * 2,
    )(dg, dbeta, a_in, b_in, a_log, dt_bias)


def _dn_post(o_raw, pgate, gn, name):
    T = o_raw.shape[0]
    tt = _tile(T, ROW_TILE, SUBLANE)

    def body(o_ref, g_ref, gn_ref, out_ref):
        o, gate = o_ref[...], g_ref[...]
        r = lax.rsqrt(jnp.mean(o * o, axis=-1, keepdims=True) + RMS_EPS)
        out_ref[...] = (o * r * gn_ref[...] * (gate * _sigmoid(gate))).astype(BF16)

    blk = pl.BlockSpec((tt, DN_DV), lambda i, h: (i, h))
    return pl.pallas_call(
        body, name=name, grid=(T // tt, DN_HEADS),
        in_specs=[blk, blk, pl.BlockSpec((1, DN_DV), lambda i, h: (0, 0))], out_specs=blk,
        out_shape=jax.ShapeDtypeStruct((T, DN_V_W), BF16),
        compiler_params=_params("parallel", "parallel"),
    )(o_raw, pgate, gn)


def _dn_post_bwd(dog, o_raw, pgate, gn, name):
    T = o_raw.shape[0]
    tt = _tile(T, ROW_TILE, SUBLANE)

    def body(d_ref, o_ref, g_ref, gn_ref, do_ref, dgate_ref, dgn_ref):
        d, o, gate, gn_v = d_ref[...], o_ref[...], g_ref[...], gn_ref[...]
        sg, dsg = _silu_and_grad(gate)
        r = lax.rsqrt(jnp.mean(o * o, axis=-1, keepdims=True) + RMS_EPS)
        n = o * r
        dy = d * sg
        dgate_ref[...] = d * (n * gn_v) * dsg
        dn = dy * gn_v
        do_ref[...] = r * (dn - n * jnp.mean(dn * n, axis=-1, keepdims=True))
        part = jnp.sum(dy * n, axis=0, keepdims=True)
        first = (pl.program_id(0) == 0) & (pl.program_id(1) == 0)

        @pl.when(first)
        def _():
            dgn_ref[...] = part

        @pl.when(jnp.logical_not(first))
        def _():
            dgn_ref[...] += part

    blk = pl.BlockSpec((tt, DN_DV), lambda i, h: (i, h))
    vec = pl.BlockSpec((1, DN_DV), lambda i, h: (0, 0))
    return pl.pallas_call(
        body, name=name, grid=(T // tt, DN_HEADS),
        in_specs=[blk, blk, blk, vec], out_specs=[blk, blk, vec],
        out_shape=[jax.ShapeDtypeStruct((T, DN_V_W), F32)] * 2 + [jax.ShapeDtypeStruct((1, DN_DV), F32)],
        compiler_params=_params("arbitrary", "arbitrary"),
    )(dog, o_raw, pgate, gn)


def _dn_chunk_terms(q, k, gc, bc):
    C = DN_CHUNK
    r, c = _iota2((C, C), 0), _iota2((C, C), 1)
    lower, strict, eye = r >= c, r > c, r == c
    grow = jnp.sum(jnp.where(eye, gc, 0.0), axis=0, keepdims=True)
    decay = jnp.where(lower, jnp.exp(jnp.where(lower, gc - grow, 0.0)), 0.0)
    last = _iota2((C, 1), 0) == C - 1
    gl = jnp.sum(jnp.where(last, gc, 0.0), axis=0, keepdims=True)
    eg = jnp.exp(gc)
    egl = jnp.exp(gl - gc)
    kb = k * bc
    lmat = jnp.where(strict, _bdot(kb, k, NT) * decay, 0.0)
    aqk = jnp.where(lower, _bdot(q, k, NT) * decay, 0.0)
    return dict(lower=lower, strict=strict, eye=eye, last=last, decay=decay, gl=gl, eg=eg, egl=egl, kb=kb,
                lmat=lmat, aqk=aqk, qd=q * eg, kd=k * egl)


def _unit_lower_inverse(lmat, eye):
    ident = jnp.where(eye, 1.0, 0.0).astype(F32)
    m = -lmat
    inv = ident + m
    steps = int(math.log2(DN_CHUNK)) - 1
    for _ in range(steps):
        m = _hdot(m, m)
        inv = inv + _hdot(inv, m)
    return inv


def _dn_chunk_fwd(act, g, beta, name):
    T = act.shape[0]
    C, H, hb = DN_CHUNK, DN_HEADS, DN_HEADS_PER_STEP
    N = T // C
    kblk0 = DN_QK_W // (hb * DN_DK)
    vblk0 = 2 * DN_QK_W // (hb * DN_DV)

    def body(q_ref, k_ref, v_ref, g_ref, b_ref, o_ref, s_out, t_out, vn_out, u_out, w_out, s_scr):
        n = pl.program_id(1)

        @pl.when(n == 0)
        def _():
            s_scr[...] = jnp.zeros_like(s_scr)

        for hh in range(hb):
            qs, vs = slice(hh * DN_DK, (hh + 1) * DN_DK), slice(hh * DN_DV, (hh + 1) * DN_DV)
            q, k, v = q_ref[:, qs], k_ref[:, qs], v_ref[:, vs]
            gc, bc = g_ref[hh], b_ref[hh]
            t = _dn_chunk_terms(q, k, gc, bc)
            tm = _unit_lower_inverse(t["lmat"], t["eye"])
            u = _hdot(tm, v * bc)
            w = _hdot(tm, t["kb"] * t["eg"])
            s = s_scr[hh]
            s_out[hh] = s
            t_out[hh] = tm
            vn = u - _bdot(w, s)
            o_ref[:, vs] = _bdot(t["qd"], s) + _bdot(t["aqk"], vn)
            s_scr[hh] = s * jnp.exp(t["gl"]) + _bdot(t["kd"], vn, TN)
            vn_out[:, vs] = vn
            u_out[:, vs] = u
            w_out[:, qs] = w

    qk_w, v_w = hb * DN_DK, hb * DN_DV
    return pl.pallas_call(
        body, name=name, grid=(H // hb, N),
        in_specs=[pl.BlockSpec((C, qk_w), lambda p, n: (n, p)),
                  pl.BlockSpec((C, qk_w), lambda p, n: (n, kblk0 + p)),
                  pl.BlockSpec((C, v_w), lambda p, n: (n, vblk0 + p)),
                  pl.BlockSpec((hb, C, 1), lambda p, n: (p, n, 0)),
                  pl.BlockSpec((hb, C, 1), lambda p, n: (p, n, 0))],
        out_specs=[pl.BlockSpec((C, v_w), lambda p, n: (n, p)),
                   pl.BlockSpec((hb, None, DN_DK, DN_DV), lambda p, n: (p, n, 0, 0)),
                   pl.BlockSpec((hb, None, C, C), lambda p, n: (p, n, 0, 0)),
                   pl.BlockSpec((C, v_w), lambda p, n: (n, p)),
                   pl.BlockSpec((C, v_w), lambda p, n: (n, p)),
                   pl.BlockSpec((C, qk_w), lambda p, n: (n, p))],
        out_shape=[jax.ShapeDtypeStruct((T, DN_V_W), F32),
                   jax.ShapeDtypeStruct((H, N, DN_DK, DN_DV), F32),
                   jax.ShapeDtypeStruct((H, N, C, C), F32),
                   jax.ShapeDtypeStruct((T, DN_V_W), F32),
                   jax.ShapeDtypeStruct((T, DN_V_W), F32),
                   jax.ShapeDtypeStruct((T, DN_QK_W), F32)],
        scratch_shapes=[pltpu.VMEM((hb, DN_DK, DN_DV), F32)],
        compiler_params=_params("parallel", "arbitrary"),
    )(act, act, act, g, beta)


def _dn_chunk_bwd(act, g, beta, s_saved, tm_saved, vn_saved, u_saved, w_saved, do, name):
    T = act.shape[0]
    C, H, hb = DN_CHUNK, DN_HEADS, DN_HEADS_PER_STEP
    N = T // C
    kblk0 = DN_QK_W // (hb * DN_DK)
    vblk0 = 2 * DN_QK_W // (hb * DN_DV)

    def body(q_ref, k_ref, v_ref, g_ref, b_ref, s_ref, t_ref, vn_ref, u_ref, w_ref, do_ref,
             dq_ref, dk_ref, dv_ref, dg_ref, db_ref, ds_scr):
        @pl.when(pl.program_id(1) == 0)
        def _():
            ds_scr[...] = jnp.zeros_like(ds_scr)

        for hh in range(hb):
            qs, vs = slice(hh * DN_DK, (hh + 1) * DN_DK), slice(hh * DN_DV, (hh + 1) * DN_DV)
            q, k, v = q_ref[:, qs], k_ref[:, qs], v_ref[:, vs]
            gc, bc = g_ref[hh], b_ref[hh]
            t = _dn_chunk_terms(q, k, gc, bc)
            lower, strict, eye = t["lower"], t["strict"], t["eye"]
            decay, eg, egl, kb, qd, kd = t["decay"], t["eg"], t["egl"], t["kb"], t["qd"], t["kd"]
            s, tm, vn, u, w, d_o = s_ref[hh], t_ref[hh], vn_ref[:, vs], u_ref[:, vs], w_ref[:, qs], do_ref[:, vs]
            ds_next = ds_scr[hh]
            egl_tot = jnp.exp(t["gl"])

            dvn = _bdot(t["aqk"], d_o, TN) + _bdot(kd, ds_next)
            daqk = jnp.where(lower, _bdot(d_o, vn, NT), 0.0)
            dqd = _bdot(d_o, s, NT)
            dkd = _bdot(vn, ds_next, NT)
            ds_scr[hh] = _bdot(qd, d_o, TN) + egl_tot * ds_next - _bdot(w, dvn, TN)
            dgl = egl_tot * jnp.sum(jnp.sum(s * ds_next, axis=1, keepdims=True), axis=0, keepdims=True)
            dw = -_bdot(dvn, s, NT)
            dru = _hdot(tm, dvn, TN)
            drw = _hdot(tm, dw, TN)
            dl = -jnp.where(strict, _hdot(dru, u, NT) + _hdot(drw, w, NT), 0.0)
            dkk = dl * decay
            dqk = daqk * decay
            dkb = _bdot(dkk, k) + drw * eg
            dk_ref[:, qs] = _bdot(dkk, kb, TN) + _bdot(dqk, q, TN) + dkd * egl + dkb * bc
            dq_ref[:, qs] = _bdot(dqk, k) + dqd * eg
            dv_ref[:, vs] = dru * bc
            db_ref[hh] = jnp.sum(dru * v, axis=1, keepdims=True) + jnp.sum(dkb * k, axis=1, keepdims=True)
            pm = dl * t["lmat"] + daqk * t["aqk"]
            col_as_col = jnp.sum(jnp.where(eye, jnp.sum(pm, axis=0, keepdims=True), 0.0), axis=1, keepdims=True)
            kdsum = jnp.sum(dkd * kd, axis=1, keepdims=True)
            dgc = (jnp.sum(pm, axis=1, keepdims=True) - col_as_col + jnp.sum(dqd * qd, axis=1, keepdims=True)
                   - kdsum + jnp.sum(drw * (kb * eg), axis=1, keepdims=True))
            dgl = dgl + jnp.sum(kdsum, axis=0, keepdims=True)
            dg_ref[hh] = dgc + jnp.where(t["last"], dgl, 0.0)

    qk_w, v_w = hb * DN_DK, hb * DN_DV
    rn = lambda n: N - 1 - n
    qk_out = pl.BlockSpec((C, qk_w), lambda p, n: (rn(n), p))
    v_out = pl.BlockSpec((C, v_w), lambda p, n: (rn(n), p))
    col = pl.BlockSpec((hb, C, 1), lambda p, n: (p, rn(n), 0))
    return pl.pallas_call(
        body, name=name, grid=(H // hb, N),
        in_specs=[pl.BlockSpec((C, qk_w), lambda p, n: (rn(n), p)),
                  pl.BlockSpec((C, qk_w), lambda p, n: (rn(n), kblk0 + p)),
                  pl.BlockSpec((C, v_w), lambda p, n: (rn(n), vblk0 + p)),
                  col, col,
                  pl.BlockSpec((hb, None, DN_DK, DN_DV), lambda p, n: (p, rn(n), 0, 0)),
                  pl.BlockSpec((hb, None, C, C), lambda p, n: (p, rn(n), 0, 0)),
                  v_out, v_out, qk_out, v_out],
        out_specs=[qk_out, qk_out, v_out, col, col],
        out_shape=[jax.ShapeDtypeStruct((T, DN_QK_W), F32), jax.ShapeDtypeStruct((T, DN_QK_W), F32),
                   jax.ShapeDtypeStruct((T, DN_V_W), F32),
                   jax.ShapeDtypeStruct((H, T, 1), F32), jax.ShapeDtypeStruct((H, T, 1), F32)],
        scratch_shapes=[pltpu.VMEM((hb, DN_DK, DN_DV), F32)],
        compiler_params=_params("parallel", "arbitrary"),
    )(act, act, act, g, beta, s_saved, tm_saved, vn_saved, u_saved, w_saved, do)


def _dn_split_w_in(w):
    wab = jnp.pad(w[:, DN_CONV_W + DN_V_W:], ((0, 0), (0, DN_AB_PAD - 2 * DN_HEADS)))
    return w[:, :DN_CONV_W], w[:, DN_CONV_W:DN_CONV_W + DN_V_W], wab


def _cols_to_heads(x):
    return x.T[:, :, None]


def _heads_to_cols(x):
    return x[:, :, 0].T


def _dn_layer_fwd(h, wts, conv_w, a_log, dt_bias, gn, w_out, x_res, tag):
    wqkv, wgate, wab = wts
    H = DN_HEADS
    pqkv = _matmul(h, wqkv, "nn", tag + "_pqkv")
    pgate = _matmul(h, wgate, "nn", tag + "_pgate")
    pab = _matmul(h, wab, "nn", tag + "_pab")
    a_in, b_in = pab[:, :H], pab[:, H:2 * H]
    g, beta = _dn_gates(a_in, b_in, a_log, dt_bias, tag + "_gates")
    gh, bh = _cols_to_heads(g), _cols_to_heads(beta)
    act = _dn_prep(pqkv, conv_w, tag + "_prep")
    o_raw, s_sv, tm_sv, vn_sv, u_sv, w_sv = _dn_chunk_fwd(act, gh, bh, tag + "_chunk_fwd")
    og = _dn_post(o_raw, pgate, gn, tag + "_post")
    y = _matmul(og, w_out, "nn", tag + "_out", add=x_res)
    saved = dict(h=h, wts=wts, conv_w=conv_w, a_log=a_log, dt_bias=dt_bias, gn=gn, w_out=w_out, pqkv=pqkv, pgate=pgate,
                 a_in=a_in, b_in=b_in, gh=gh, bh=bh, act=act, o_raw=o_raw, chunk=(s_sv, tm_sv, vn_sv, u_sv, w_sv), og=og)
    return y, saved


def _dn_layer_bwd(dout, sv, tag):
    wqkv, wgate, wab = sv["wts"]
    h = sv["h"]
    dog = _matmul(dout, sv["w_out"], "nt", tag + "_dog")
    dw_out = _matmul(sv["og"], dout, "tn", tag + "_dwout")
    do_raw, dgate, dgn = _dn_post_bwd(dog, sv["o_raw"], sv["pgate"], sv["gn"], tag + "_post_bwd")
    dq, dk, dv, dgh, dbh = _dn_chunk_bwd(sv["act"], sv["gh"], sv["bh"], *sv["chunk"], do_raw, tag + "_chunk_bwd")
    da_in, db_in, da_log, ddt = _dn_gates_bwd(_heads_to_cols(dgh), _heads_to_cols(dbh), sv["a_in"], sv["b_in"],
                                              sv["a_log"], sv["dt_bias"], tag + "_gates_bwd")
    dact = jnp.concatenate([dq, dk, dv], axis=1)
    dpqkv, dconv = _dn_prep_bwd(sv["pqkv"], sv["conv_w"], dact, tag + "_prep_bwd")
    dpab = jnp.pad(jnp.concatenate([da_in, db_in], axis=1), ((0, 0), (0, DN_AB_PAD - 2 * DN_HEADS)))
    dwqkv = _matmul(h, dpqkv, "tn", tag + "_dwqkv")
    dwgate = _matmul(h, dgate, "tn", tag + "_dwgate")
    dwab = _matmul(h, dpab, "tn", tag + "_dwab")
    dh = _matmul(dpqkv, wqkv, "nt", tag + "_dh0")
    dh = _matmul(dgate, wgate, "nt", tag + "_dh1", add=dh)
    dh = _matmul(dpab, wab, "nt", tag + "_dh2", add=dh)
    dw_in = jnp.concatenate([dwqkv, dwgate, dwab[:, :2 * DN_HEADS]], axis=1)
    return dh, (dw_in, dconv, da_log, ddt, dgn, dw_out)


def _sb_layer_fwd(h, w_in_perm, qg, kg, w_out, x_res, tag):
    qg2, kg2 = jnp.tile(qg, (1, 2)), jnp.tile(kg, (1, 2))
    proj = _matmul(h, w_in_perm, "nn", tag + "_proj")
    qn, kn, vb = _sb_prep(proj, qg2, kg2, tag + "_prep")
    o, og, ltot = _sb_attn_fwd(qn, kn, vb, proj, tag + "_attn_fwd")
    y = _matmul(og, w_out, "nn", tag + "_out", add=x_res)
    saved = dict(h=h, w_in=w_in_perm, qg2=qg2, kg2=kg2, w_out=w_out, proj=proj, qn=qn, kn=kn, vb=vb, o=o, og=og, ltot=ltot)
    return y, saved


def _sb_layer_bwd(dout, sv, tag):
    dog = _matmul(dout, sv["w_out"], "nt", tag + "_dog")
    dw_out = _matmul(sv["og"], dout, "tn", tag + "_dwout")
    dqn, dkn, dv, dgate = _sb_attn_bwd(sv["qn"], sv["kn"], sv["vb"], dog, sv["o"], sv["ltot"], sv["proj"], tag + "_attn_bwd")
    dproj, dqgp, dkgp = _sb_prep_bwd(sv["proj"], dqn, dkn, dv, dgate, sv["qg2"], sv["kg2"], tag + "_prep_bwd")
    dw_in = _sb_unperm(_matmul(sv["h"], dproj, "tn", tag + "_dwin"))
    dh = _matmul(dproj, sv["w_in"], "nt", tag + "_dh")
    dqg = _fold_heads(dqgp.reshape(SB_PAIRS, LANE), tag + "_dqg")
    dkg = _fold_heads(dkgp.reshape(SB_PAIRS, LANE), tag + "_dkg")
    return dh, (dw_in, dqg, dkg, dw_out)


def _sc_layer_fwd(h, w_in_perm, conv_w, w_out, x_res, tag):
    proj = _matmul(h, w_in_perm, "nn", tag + "_proj")
    yg = _sc_fwd(proj, conv_w, tag + "_fwd")
    y = _matmul(yg, w_out, "nn", tag + "_out", add=x_res)
    return y, dict(h=h, w_in=w_in_perm, conv_w=conv_w, w_out=w_out, proj=proj, yg=yg)


def _sc_layer_bwd(dout, sv, tag):
    dyg = _matmul(dout, sv["w_out"], "nt", tag + "_dyg")
    dw_out = _matmul(sv["yg"], dout, "tn", tag + "_dwout")
    dproj, dconv = _sc_bwd(dyg, sv["proj"], sv["conv_w"], tag + "_bwd")
    dw_in = _sc_unperm(_matmul(sv["h"], dproj, "tn", tag + "_dwin"))
    dh = _matmul(dproj, sv["w_in"], "nt", tag + "_dh")
    return dh, (dw_in, dconv, dw_out)


def _local_step(x, target, W):
    norm_g = W["norm_g"]
    xs, saves = [x], []
    for i in range(4):
        h = _rmsnorm_fwd(xs[i], norm_g[i:i + 1], f"norm{i}")
        if i in (0, 3):
            j = i // 3
            y, sv = _dn_layer_fwd(h, W["dn_w_in"][j], W["dn_conv_w"][j], W["dn_a_log"][j:j + 1], W["dn_dt_bias"][j:j + 1],
                                  W["dn_o_norm_g"][j:j + 1], W["dn_w_out"][j], xs[i], f"dn{j}")
        elif i == 1:
            y, sv = _sb_layer_fwd(h, W["sb_w_in"], W["sb_q_norm_g"], W["sb_k_norm_g"], W["sb_w_out"], xs[i], "sb")
        else:
            y, sv = _sc_layer_fwd(h, W["sc_w_in"], W["sc_conv_w"], W["sc_w_out"], xs[i], "sc")
        xs.append(y)
        saves.append(sv)
    dx, loss = _loss_head(xs[4], target)
    G = {}
    dnorm = [None] * 4
    dn_parts = [None, None]
    for i in (3, 2, 1, 0):
        if i in (0, 3):
            dh, dn_parts[i // 3] = _dn_layer_bwd(dx, saves[i], f"dn{i // 3}")
        elif i == 1:
            dh, (G["sb_w_in"], G["sb_q_norm_g"], G["sb_k_norm_g"], G["sb_w_out"]) = _sb_layer_bwd(dx, saves[i], "sb")
        else:
            dh, (G["sc_w_in"], G["sc_conv_w"], G["sc_w_out"]) = _sc_layer_bwd(dx, saves[i], "sc")
        dx, dnorm[i] = _rmsnorm_bwd(dh, xs[i], norm_g[i:i + 1], dx, f"norm{i}_bwd")
    G["norm_g"] = jnp.concatenate(dnorm, axis=0)
    for k, name in enumerate(("dn_w_in", "dn_conv_w", "dn_a_log", "dn_dt_bias", "dn_o_norm_g", "dn_w_out")):
        G[name] = jnp.stack([dn_parts[0][k], dn_parts[1][k]], axis=0)
    return loss, dx, G


def _adamw(w, m, v, parts, name):
    R, C = w.shape
    tr = _tile(R, 128, SUBLANE)

    def body(w_ref, m_ref, v_ref, p_ref, g_ref, d_ref, nm_ref, nv_ref):
        g = p_ref[0]
        for s in range(1, N_DEV):
            g = g + p_ref[s]
        m2 = ADAM_B1 * m_ref[...] + (1.0 - ADAM_B1) * g
        v2 = ADAM_B2 * v_ref[...] + (1.0 - ADAM_B2) * (g * g)
        m_hat = m2 / (1.0 - ADAM_B1 ** ADAM_STEP)
        v_hat = v2 / (1.0 - ADAM_B2 ** ADAM_STEP)
        g_ref[...] = g
        d_ref[...] = -ADAM_LR * (m_hat / (jnp.sqrt(v_hat) + ADAM_EPS) + ADAM_WD * w_ref[...])
        nm_ref[...] = m2
        nv_ref[...] = v2

    blk = pl.BlockSpec((tr, C), lambda i: (i, 0))
    return pl.pallas_call(
        body, name=name, grid=(R // tr,),
        in_specs=[blk, blk, blk, pl.BlockSpec((N_DEV, tr, C), lambda i: (0, i, 0))],
        out_specs=[blk] * 4, out_shape=[jax.ShapeDtypeStruct((R, C), F32)] * 4,
        compiler_params=_params("parallel"),
    )(w, m, v, parts)


_HBM = pl.BlockSpec(memory_space=pltpu.HBM)
_MESH = pl.DeviceIdType.MESH


def _slot(x, y, c):
    return 4 * x + 2 * y + c


def _all_gather(shards, name):
    n = len(shards)

    def body(*refs):
        ins, outs = refs[:n], refs[n:2 * n]
        send_sems, recv_sems, local_sems = refs[2 * n:]
        x, y, c = lax.axis_index("x"), lax.axis_index("y"), lax.axis_index("c")
        me, sibling = (x, y, c), (x, y, 1 - c)
        chips = [(1 - x, y), (x, 1 - y), (1 - x, 1 - y)]

        def copy(a, k, block, to, src=None):
            dst = outs[a].at[_slot(*block)]
            return pltpu.make_async_remote_copy(src_ref=dst if src is None else src, dst_ref=dst,
                                                send_sem=send_sems.at[a, k], recv_sem=recv_sems.at[a, k],
                                                device_id=to, device_id_type=_MESH)

        mine = [pltpu.make_async_copy(ins[a], outs[a].at[_slot(*me)], local_sems.at[a]) for a in range(n)]
        for cp in mine:
            cp.start()
        first = []
        for a in range(n):
            first.append(copy(a, 0, me, sibling, src=ins[a]))
            first += [copy(a, 1 + j, me, (*chip, c), src=ins[a]) for j, chip in enumerate(chips)]
        for cp in first:
            cp.start()
        passed = []
        for j, chip in enumerate(chips):
            for a in range(n):
                copy(a, 1 + j, (*chip, c), me).wait_recv()
                fwd = copy(a, 4 + j, (*chip, c), sibling)
                fwd.start()
                passed.append(fwd)
        for a in range(n):
            copy(a, 0, sibling, me).wait_recv()
            for j, chip in enumerate(chips):
                copy(a, 4 + j, (*chip, 1 - c), me).wait_recv()
        for cp in first + passed:
            cp.wait_send()
        for cp in mine:
            cp.wait()

    return pl.pallas_call(
        body, name=name,
        in_specs=[_HBM] * n, out_specs=[_HBM] * n,
        out_shape=[jax.ShapeDtypeStruct((N_DEV,) + s.shape, s.dtype) for s in shards],
        scratch_shapes=[pltpu.SemaphoreType.DMA((n, N_DEV - 1)), pltpu.SemaphoreType.DMA((n, N_DEV - 1)),
                        pltpu.SemaphoreType.DMA((n,))],
    )(*shards)


def _exchange(arrays, scatter, name):
    n = len(arrays)

    def body(*refs):
        ins, outs = refs[:n], refs[n:2 * n]
        send_sems, recv_sems, local_sems = refs[2 * n:]
        x, y, c = lax.axis_index("x"), lax.axis_index("y"), lax.axis_index("c")
        me = _slot(x, y, c)
        copies = []
        for a in range(n):
            cp = pltpu.make_async_copy(ins[a].at[me] if scatter[a] else ins[a], outs[a].at[me], local_sems.at[a])
            cp.start()
            copies.append(cp)
        for r in range(1, N_DEV):
            px = 1 - x if r & 4 else x
            py = 1 - y if r & 2 else y
            pc = 1 - c if r & 1 else c
            for a in range(n):
                cp = pltpu.make_async_remote_copy(
                    src_ref=ins[a].at[_slot(px, py, pc)] if scatter[a] else ins[a], dst_ref=outs[a].at[me],
                    send_sem=send_sems.at[a, r - 1], recv_sem=recv_sems.at[a, r - 1],
                    device_id=(px, py, pc), device_id_type=_MESH)
                cp.start()
                copies.append(cp)
        for cp in copies:
            cp.wait()

    shapes = [a.shape[1:] if s else a.shape for a, s in zip(arrays, scatter)]
    return pl.pallas_call(
        body, name=name,
        in_specs=[_HBM] * n, out_specs=[_HBM] * n,
        out_shape=[jax.ShapeDtypeStruct((N_DEV,) + tuple(s), F32) for s in shapes],
        scratch_shapes=[pltpu.SemaphoreType.DMA((n, N_DEV - 1)), pltpu.SemaphoreType.DMA((n, N_DEV - 1)),
                        pltpu.SemaphoreType.DMA((n,))],
    )(*arrays)


_SHARDED = ("dn_w_in", "dn_conv_w", "dn_o_norm_g", "dn_w_out", "sb_w_in", "sb_w_out", "sc_w_in", "sc_conv_w", "sc_w_out")
_MATMUL_WEIGHTS = ("dn_w_in", "dn_w_out", "sb_w_in", "sb_w_out", "sc_w_in", "sc_w_out")
_COLUMN_SHARDED = ("dn_w_in", "dn_conv_w", "dn_o_norm_g", "sb_w_in", "sc_w_in", "sc_conv_w")
_REPLICATED = ("norm_g", "dn_a_log", "dn_dt_bias", "sb_q_norm_g", "sb_k_norm_g")
_ORDER = ("norm_g", "dn_w_in", "dn_conv_w", "dn_a_log", "dn_dt_bias", "dn_o_norm_g", "dn_w_out", "sb_w_in", "sb_q_norm_g",
          "sb_k_norm_g", "sb_w_out", "sc_w_in", "sc_conv_w", "sc_w_out")
_PACK_COLS = D_MODEL


def _assemble(name, gathered):
    if name in _COLUMN_SHARDED:
        g = jnp.moveaxis(gathered, 0, -2)
        return g.reshape(g.shape[:-2] + (g.shape[-2] * g.shape[-1],))
    g = jnp.moveaxis(gathered, 0, 1)
    return g.reshape((g.shape[0], g.shape[1] * g.shape[2]) + g.shape[3:])


def _disassemble(name, full):
    if name in _COLUMN_SHARDED:
        g = full.reshape(full.shape[:-1] + (N_DEV, full.shape[-1] // N_DEV))
        return jnp.moveaxis(g, -2, 0)
    g = full.reshape((full.shape[0], N_DEV, full.shape[1] // N_DEV) + full.shape[2:])
    return jnp.moveaxis(g, 1, 0)


def _pack_replicated(d):
    rows = [d["norm_g"]]
    for name in _REPLICATED[1:]:
        flat = d[name].reshape(1, -1)
        rows.append(jnp.pad(flat, ((0, 0), (0, _PACK_COLS - flat.shape[1]))))
    return jnp.concatenate(rows, axis=0)


def _unpack_replicated(p, like):
    out = {"norm_g": p[:4]}
    for r, name in enumerate(_REPLICATED[1:]):
        shape = like[name].shape
        out[name] = p[4 + r, :math.prod(shape)].reshape(shape)
    return out


def kernel(x, norm_g, dn_w_in, dn_conv_w, dn_a_log, dn_dt_bias, dn_o_norm_g, dn_w_out, sb_w_in, sb_q_norm_g, sb_k_norm_g, sb_w_out, sc_w_in, sc_conv_w, sc_w_out, loss_target, m_norm_g, m_dn_w_in, m_dn_conv_w, m_dn_a_log, m_dn_dt_bias, m_dn_o_norm_g, m_dn_w_out, m_sb_w_in, m_sb_q_norm_g, m_sb_k_norm_g, m_sb_w_out, m_sc_w_in, m_sc_conv_w, m_sc_w_out, v_norm_g, v_dn_w_in, v_dn_conv_w, v_dn_a_log, v_dn_dt_bias, v_dn_o_norm_g, v_dn_w_out, v_sb_w_in, v_sb_q_norm_g, v_sb_k_norm_g, v_sb_w_out, v_sc_w_in, v_sc_conv_w, v_sc_w_out):
    w = dict(norm_g=norm_g, dn_w_in=dn_w_in, dn_conv_w=dn_conv_w, dn_a_log=dn_a_log, dn_dt_bias=dn_dt_bias,
             dn_o_norm_g=dn_o_norm_g, dn_w_out=dn_w_out, sb_w_in=sb_w_in, sb_q_norm_g=sb_q_norm_g, sb_k_norm_g=sb_k_norm_g,
             sb_w_out=sb_w_out, sc_w_in=sc_w_in, sc_conv_w=sc_conv_w, sc_w_out=sc_w_out)
    m = dict(norm_g=m_norm_g, dn_w_in=m_dn_w_in, dn_conv_w=m_dn_conv_w, dn_a_log=m_dn_a_log, dn_dt_bias=m_dn_dt_bias,
             dn_o_norm_g=m_dn_o_norm_g, dn_w_out=m_dn_w_out, sb_w_in=m_sb_w_in, sb_q_norm_g=m_sb_q_norm_g,
             sb_k_norm_g=m_sb_k_norm_g, sb_w_out=m_sb_w_out, sc_w_in=m_sc_w_in, sc_conv_w=m_sc_conv_w, sc_w_out=m_sc_w_out)
    v = dict(norm_g=v_norm_g, dn_w_in=v_dn_w_in, dn_conv_w=v_dn_conv_w, dn_a_log=v_dn_a_log, dn_dt_bias=v_dn_dt_bias,
             dn_o_norm_g=v_dn_o_norm_g, dn_w_out=v_dn_w_out, sb_w_in=v_sb_w_in, sb_q_norm_g=v_sb_q_norm_g,
             sb_k_norm_g=v_sb_k_norm_g, sb_w_out=v_sb_w_out, sc_w_in=v_sc_w_in, sc_conv_w=v_sc_conv_w, sc_w_out=v_sc_w_out)

    shards = [w[k].astype(BF16) if k in _MATMUL_WEIGHTS else w[k] for k in _SHARDED]
    gathered = _all_gather(shards, "gather_weights")
    W = {k: _assemble(k, g) for k, g in zip(_SHARDED, gathered)}
    for k in _REPLICATED:
        W[k] = w[k]
    W["dn_w_in"] = [_dn_split_w_in(W["dn_w_in"][j]) for j in range(2)]
    W["sb_w_in"] = _sb_perm(W["sb_w_in"][0])
    W["sb_w_out"] = W["sb_w_out"][0]
    W["sc_w_in"] = _sc_perm(W["sc_w_in"][0])
    W["sc_conv_w"] = W["sc_conv_w"][0]
    W["sc_w_out"] = W["sc_w_out"][0]

    loss_part, dx, G = _local_step(x[0], loss_target[0], W)
    for k in ("sb_w_in", "sb_w_out", "sc_w_in", "sc_conv_w", "sc_w_out"):
        G[k] = G[k][None]
    G["dn_a_log"] = G["dn_a_log"].reshape(2, DN_HEADS)
    G["dn_dt_bias"] = G["dn_dt_bias"].reshape(2, DN_HEADS)
    G["dn_o_norm_g"] = G["dn_o_norm_g"].reshape(2, DN_DV)

    outgoing = [_disassemble(k, G[k]) for k in _SHARDED] + [_pack_replicated(G)]
    landed = _exchange(outgoing, [True] * len(_SHARDED) + [False], "exchange_grads")

    res = {}
    for k, parts in zip(_SHARDED, landed[:-1]):
        shape = w[k].shape
        rows = math.prod(shape[:-1])
        flat = lambda a: a.reshape(rows, shape[-1])
        outs = _adamw(flat(w[k]), flat(m[k]), flat(v[k]), parts.reshape(N_DEV, rows, shape[-1]), "adamw_" + k)
        res[k] = [o.reshape(shape) for o in outs]
    outs = _adamw(_pack_replicated(w), _pack_replicated(m), _pack_replicated(v), landed[-1], "adamw_replicated")
    unpacked = [_unpack_replicated(o, w) for o in outs]
    for k in _REPLICATED:
        res[k] = [u[k] for u in unpacked]

    loss = lax.psum(loss_part[0, 0], ("x", "y", "c"))
    return (loss, dx[None]) + tuple(res[k][0] for k in _ORDER) + tuple(res[k][1] for k in _ORDER) \
        + tuple(res[k][2] for k in _ORDER) + tuple(res[k][3] for k in _ORDER)
```

```python
import itertools
import math

import jax
import jax.numpy as jnp
from jax import lax
from jax.experimental import pallas as pl
from jax.experimental.pallas import tpu as pltpu

F32 = jnp.float32
BF16 = jnp.bfloat16
HIGHEST = lax.Precision.HIGHEST

N_DEV = 8
D_MODEL = 1024
RMS_EPS = 1e-6
L2_EPS = 1e-6

DN_HEADS = 8
DN_DK = 128
DN_DV = 256
DN_QK_W = DN_HEADS * DN_DK
DN_V_W = DN_HEADS * DN_DV
DN_CONV = 4
DN_CHUNK = 64
DN_CONV_W = 2 * DN_QK_W + DN_V_W
DN_IN = DN_CONV_W + DN_V_W + 2 * DN_HEADS
DN_AB_PAD = 128
DN_HEADS_PER_STEP = 8
DN_PREP_BLK = 512

SB_HEADS = 16
SB_DH = 64
SB_W = SB_HEADS * SB_DH
SB_PAIRS = SB_HEADS // 2
SB_TQ = 256
SB_TK = 128

SC_W = 2 * D_MODEL
SC_CONV = 3
SC_BLK = 512
SC_NBLK = SC_W // SC_BLK

ADAM_LR = 0.001
ADAM_B1 = 0.9
ADAM_B2 = 0.999
ADAM_EPS = 1e-08
ADAM_WD = 0.01
ADAM_STEP = 10

LANE = 128
SUBLANE = 8
HALO = SUBLANE
ROW_TILE = 256
VMEM_LIMIT = 48 * 2 ** 20

NN = ((1,), (0,))
NT = ((1,), (1,))
TN = ((0,), (0,))


def _dot(a, b, dims=NN, precision=None):
    return lax.dot_general(a, b, (dims, ((), ())), precision=precision, preferred_element_type=F32)


def _bdot(a, b, dims=NN):
    return _dot(a.astype(BF16), b.astype(BF16), dims)


def _hdot(a, b, dims=NN):
    return _dot(a, b, dims, precision=HIGHEST)


def _tile(dim, pref, align=LANE):
    t = (min(pref, dim) // align) * align
    while t >= align:
        if dim % t == 0:
            return t
        t -= align
    return dim


def _params(*sem):
    return pltpu.CompilerParams(dimension_semantics=sem, vmem_limit_bytes=VMEM_LIMIT)


def _sigmoid(x):
    return 1.0 / (1.0 + jnp.exp(-x))


def _softplus(x):
    return jnp.maximum(x, 0.0) + jnp.log(1.0 + jnp.exp(-jnp.abs(x)))


def _silu_and_grad(x):
    s = _sigmoid(x)
    return x * s, s * (1.0 + x * (1.0 - s))


def _iota2(shape, dim):
    return lax.broadcasted_iota(jnp.int32, shape, dim)


def _matmul(a, b, mode, name, out_dtype=F32, add=None, tm=512, tn=1024, tk=1024):
    if mode == "nn":
        (M, K), (K2, N) = a.shape, b.shape
    elif mode == "nt":
        (M, K), (N, K2) = a.shape, b.shape
    else:
        (K, M), (K2, N) = a.shape, b.shape
    assert K == K2, (a.shape, b.shape, mode)
    tm, tn, tk = _tile(M, tm), _tile(N, tn), _tile(K, tk)
    nk = K // tk
    dims = {"nn": NN, "nt": NT, "tn": TN}[mode]
    a_spec = pl.BlockSpec((tk, tm), lambda i, j, k: (k, i)) if mode == "tn" else pl.BlockSpec((tm, tk), lambda i, j, k: (i, k))
    b_spec = pl.BlockSpec((tn, tk), lambda i, j, k: (j, k)) if mode == "nt" else pl.BlockSpec((tk, tn), lambda i, j, k: (k, j))
    o_spec = pl.BlockSpec((tm, tn), lambda i, j, k: (i, j))
    has_add = add is not None

    def body(*refs):
        a_ref, b_ref = refs[0], refs[1]
        add_ref = refs[2] if has_add else None
        o_ref = refs[3] if has_add else refs[2]
        p = _bdot(a_ref[...], b_ref[...], dims)

        def finish(acc):
            if has_add:
                acc = acc + add_ref[...]
            o_ref[...] = acc.astype(out_dtype)

        if nk == 1:
            finish(p)
        else:
            acc_ref = refs[-1]
            k = pl.program_id(2)

            @pl.when(k == 0)
            def _():
                acc_ref[...] = p

            @pl.when(k > 0)
            def _():
                acc_ref[...] += p

            @pl.when(k == nk - 1)
            def _():
                finish(acc_ref[...])

    in_specs = [a_spec, b_spec] + ([o_spec] if has_add else [])
    args = (a, b) + ((add,) if has_add else ())
    return pl.pallas_call(
        body, name=name, grid=(M // tm, N // tn, nk),
        in_specs=in_specs, out_specs=o_spec,
        out_shape=jax.ShapeDtypeStruct((M, N), out_dtype),
        scratch_shapes=[pltpu.VMEM((tm, tn), F32)] if nk > 1 else [],
        compiler_params=_params("parallel", "parallel", "arbitrary"),
    )(*args)


def _rmsnorm_fwd(x, g, name):
    T, D = x.shape
    tt = _tile(T, 512, SUBLANE)

    def body(x_ref, g_ref, o_ref):
        xv = x_ref[...]
        r = lax.rsqrt(jnp.mean(xv * xv, axis=-1, keepdims=True) + RMS_EPS)
        o_ref[...] = (xv * r * g_ref[...]).astype(BF16)

    return pl.pallas_call(
        body, name=name, grid=(T // tt,),
        in_specs=[pl.BlockSpec((tt, D), lambda i: (i, 0)), pl.BlockSpec((1, D), lambda i: (0, 0))],
        out_specs=pl.BlockSpec((tt, D), lambda i: (i, 0)),
        out_shape=jax.ShapeDtypeStruct((T, D), BF16),
        compiler_params=_params("parallel"),
    )(x, g)


def _rmsnorm_bwd(dh, x, g, dx_res, name):
    T, D = x.shape
    tt = _tile(T, 256, SUBLANE)

    def body(dh_ref, x_ref, g_ref, res_ref, dx_ref, dg_ref):
        xv, dhv = x_ref[...], dh_ref[...]
        r = lax.rsqrt(jnp.mean(xv * xv, axis=-1, keepdims=True) + RMS_EPS)
        xh = xv * r
        dxh = dhv * g_ref[...]
        m = jnp.mean(dxh * xh, axis=-1, keepdims=True)
        dx_ref[...] = res_ref[...] + r * (dxh - xh * m)
        part = jnp.sum(dhv * xh, axis=0, keepdims=True)

        @pl.when(pl.program_id(0) == 0)
        def _():
            dg_ref[...] = part

        @pl.when(pl.program_id(0) > 0)
        def _():
            dg_ref[...] += part

    row = pl.BlockSpec((tt, D), lambda i: (i, 0))
    vec = pl.BlockSpec((1, D), lambda i: (0, 0))
    return pl.pallas_call(
        body, name=name, grid=(T // tt,),
        in_specs=[row, row, vec, row], out_specs=[row, vec],
        out_shape=[jax.ShapeDtypeStruct((T, D), F32), jax.ShapeDtypeStruct((1, D), F32)],
        compiler_params=_params("arbitrary"),
    )(dh, x, g, dx_res)


def _loss_head(y, target, name="loss_head"):
    T, D = y.shape
    tt = _tile(T, 512, SUBLANE)

    def body(y_ref, t_ref, dy_ref, l_ref):
        e = y_ref[...] - t_ref[...]
        dy_ref[...] = e * (1.0 / D)
        s = jnp.sum(jnp.sum(e * e, axis=1, keepdims=True), axis=0, keepdims=True) * (0.5 / D)
        s = jnp.broadcast_to(s, (1, LANE))

        @pl.when(pl.program_id(0) == 0)
        def _():
            l_ref[...] = s

        @pl.when(pl.program_id(0) > 0)
        def _():
            l_ref[...] += s

    row = pl.BlockSpec((tt, D), lambda i: (i, 0))
    return pl.pallas_call(
        body, name=name, grid=(T // tt,),
        in_specs=[row, row], out_specs=[row, pl.BlockSpec((1, LANE), lambda i: (0, 0))],
        out_shape=[jax.ShapeDtypeStruct((T, D), F32), jax.ShapeDtypeStruct((1, LANE), F32)],
        compiler_params=_params("arbitrary"),
    )(y, target)


def _down(x, k):
    return pltpu.roll(x, k, 0) if k else x


def _up(x, k):
    return pltpu.roll(x, x.shape[0] - k, 0) if k else x


def _sc_fwd(proj, conv_w, name):
    T = proj.shape[0]
    tt = _tile(T, ROW_TILE, SUBLANE)
    B = SC_BLK

    def body(p_ref, ph_ref, w_ref, o_ref):
        i = pl.program_id(0)
        keep = (i > 0).astype(F32)
        c = jnp.concatenate([ph_ref[:, B:2 * B] * keep, p_ref[:, B:2 * B]], axis=0)
        u = jnp.concatenate([ph_ref[:, 2 * B:3 * B] * keep, p_ref[:, 2 * B:3 * B]], axis=0)
        z = c * u
        cz = (w_ref[2:3, :] * z + w_ref[1:2, :] * _down(z, 1) + w_ref[0:1, :] * _down(z, 2))[HALO:]
        gate = p_ref[:, 3 * B:4 * B]
        o_ref[...] = (p_ref[:, 0:B] * cz * (gate * _sigmoid(gate))).astype(BF16)

    return pl.pallas_call(
        body, name=name, grid=(T // tt, SC_NBLK),
        in_specs=[pl.BlockSpec((tt, 4 * B), lambda i, j: (i, j)),
                  pl.BlockSpec((HALO, 4 * B), lambda i, j: (jnp.maximum(i * (tt // HALO) - 1, 0), j)),
                  pl.BlockSpec((SC_CONV, B), lambda i, j: (0, j))],
        out_specs=pl.BlockSpec((tt, B), lambda i, j: (i, j)),
        out_shape=jax.ShapeDtypeStruct((T, SC_W), BF16),
        compiler_params=_params("parallel", "parallel"),
    )(proj, proj, conv_w)


def _sc_bwd(dyg, proj, conv_w, name):
    T = proj.shape[0]
    tt = _tile(T, ROW_TILE, SUBLANE)
    nt = T // tt
    B = SC_BLK
    hb = tt // HALO

    def body(d_ref, dn_ref, p_ref, pp_ref, pn_ref, w_ref, o_ref, dw_ref):
        i = pl.program_id(1)
        keep_p = (i > 0).astype(F32)
        keep_n = (i < nt - 1).astype(F32)

        def ext(k):
            s = slice(k * B, (k + 1) * B)
            return jnp.concatenate([pp_ref[:, s] * keep_p, p_ref[:, s], pn_ref[:, s]], axis=0)

        b, c, u, gate = ext(0), ext(1), ext(2), ext(3)
        dyg_e = jnp.concatenate([jnp.zeros((HALO, B), F32), d_ref[...], dn_ref[...] * keep_n], axis=0)
        w0, w1, w2 = w_ref[0:1, :], w_ref[1:2, :], w_ref[2:3, :]
        z = c * u
        z1, z2 = _down(z, 1), _down(z, 2)
        cz = w2 * z + w1 * z1 + w0 * z2
        sg, dsg = _silu_and_grad(gate)
        dy = dyg_e * sg
        dgate = dyg_e * (b * cz) * dsg
        db = dy * cz
        dcz = dy * b
        dz = w2 * dcz + w1 * _up(dcz, 1) + w0 * _up(dcz, 2)
        main = slice(HALO, HALO + tt)
        o_ref[:, 0:B] = db[main]
        o_ref[:, B:2 * B] = (dz * u)[main]
        o_ref[:, 2 * B:3 * B] = (dz * c)[main]
        o_ref[:, 3 * B:4 * B] = dgate[main]
        dcm = dcz[main]
        part = jnp.concatenate([jnp.sum(dcm * z2[main], axis=0, keepdims=True),
                                jnp.sum(dcm * z1[main], axis=0, keepdims=True),
                                jnp.sum(dcm * z[main], axis=0, keepdims=True)], axis=0)

        @pl.when(i == 0)
        def _():
            dw_ref[...] = part

        @pl.when(i > 0)
        def _():
            dw_ref[...] += part

    return pl.pallas_call(
        body, name=name, grid=(SC_NBLK, nt),
        in_specs=[pl.BlockSpec((tt, B), lambda j, i: (i, j)),
                  pl.BlockSpec((HALO, B), lambda j, i: (jnp.minimum((i + 1) * hb, nt * hb - 1), j)),
                  pl.BlockSpec((tt, 4 * B), lambda j, i: (i, j)),
                  pl.BlockSpec((HALO, 4 * B), lambda j, i: (jnp.maximum(i * hb - 1, 0), j)),
                  pl.BlockSpec((HALO, 4 * B), lambda j, i: (jnp.minimum((i + 1) * hb, nt * hb - 1), j)),
                  pl.BlockSpec((SC_CONV, B), lambda j, i: (0, j))],
        out_specs=[pl.BlockSpec((tt, 4 * B), lambda j, i: (i, j)), pl.BlockSpec((SC_CONV, B), lambda j, i: (0, j))],
        out_shape=[jax.ShapeDtypeStruct((T, 4 * SC_W), F32), jax.ShapeDtypeStruct((SC_CONV, SC_W), F32)],
        compiler_params=_params("parallel", "arbitrary"),
    )(dyg, dyg, proj, proj, proj, conv_w)


def _sc_perm(w_in):
    d = w_in.shape[0]
    return w_in.reshape(d, 4, SC_NBLK, SC_BLK).transpose(0, 2, 1, 3).reshape(d, 4 * SC_W)


def _sc_unperm(w):
    d = w.shape[0]
    return w.reshape(d, SC_NBLK, 4, SC_BLK).transpose(0, 2, 1, 3).reshape(d, 4 * SC_W)


def _sb_perm(w_in):
    d = w_in.shape[0]
    return w_in.reshape(d, 4, SB_PAIRS, LANE).transpose(0, 2, 1, 3).reshape(d, 4 * SB_W)


def _sb_unperm(w):
    d = w.shape[0]
    return w.reshape(d, SB_PAIRS, 4, LANE).transpose(0, 2, 1, 3).reshape(d, 4 * SB_W)


def _split3_dot(x, m):
    hi = x.astype(BF16)
    r1 = x - hi.astype(F32)
    mid = r1.astype(BF16)
    lo = (r1 - mid.astype(F32)).astype(BF16)
    return _dot(hi, m) + _dot(mid, m) + _dot(lo, m)


def _split2_dot(x, m):
    hi = x.astype(BF16)
    lo = (x - hi.astype(F32)).astype(BF16)
    return _dot(hi, m) + _dot(lo, m)


def _head_mean_matrix():
    r, c = _iota2((LANE, LANE), 0), _iota2((LANE, LANE), 1)
    return jnp.where((r // SB_DH) == (c // SB_DH), 1.0 / SB_DH, 0.0).astype(BF16)


def _sb_prep(proj, qg2, kg2, name):
    T = proj.shape[0]
    tt = _tile(T, ROW_TILE, SUBLANE)

    def body(p_ref, qg_ref, kg_ref, q_ref, k_ref, v_ref):
        bd = _head_mean_matrix()

        def norm(x, g, scale):
            r = lax.rsqrt(_split3_dot(x * x, bd) + RMS_EPS)
            return (x * r * g * scale).astype(BF16)

        q_ref[...] = norm(p_ref[:, 0:LANE], qg_ref[...], SB_DH ** -0.5)
        k_ref[...] = norm(p_ref[:, LANE:2 * LANE], kg_ref[...], 1.0)
        v_ref[...] = p_ref[:, 2 * LANE:3 * LANE].astype(BF16)

    blk = pl.BlockSpec((tt, LANE), lambda i, p: (i, p))
    vec = pl.BlockSpec((1, LANE), lambda i, p: (0, 0))
    return pl.pallas_call(
        body, name=name, grid=(T // tt, SB_PAIRS),
        in_specs=[pl.BlockSpec((tt, 4 * LANE), lambda i, p: (i, p)), vec, vec],
        out_specs=[blk, blk, blk],
        out_shape=[jax.ShapeDtypeStruct((T, SB_W), BF16)] * 3,
        compiler_params=_params("parallel", "parallel"),
    )(proj, qg2, kg2)


def _sb_prep_bwd(proj, dqn, dkn, dv, dgate, qg2, kg2, name):
    T = proj.shape[0]
    tt = _tile(T, ROW_TILE, SUBLANE)

    def body(p_ref, dq_ref, dk_ref, dv_ref, dg_ref, qg_ref, kg_ref, o_ref, dqg_ref, dkg_ref):
        i = pl.program_id(1)
        bd = _head_mean_matrix()

        def norm_bwd(x, g, dy):
            r = lax.rsqrt(_split3_dot(x * x, bd) + RMS_EPS)
            xh = x * r
            dxh = dy * g
            m = _split3_dot(dxh * xh, bd)
            return r * (dxh - xh * m), jnp.sum(dy * xh, axis=0, keepdims=True)

        dxq, pq = norm_bwd(p_ref[:, 0:LANE], qg_ref[...], dq_ref[...])
        dxk, pk = norm_bwd(p_ref[:, LANE:2 * LANE], kg_ref[...], dk_ref[...])
        o_ref[:, 0:LANE] = dxq
        o_ref[:, LANE:2 * LANE] = dxk
        o_ref[:, 2 * LANE:3 * LANE] = dv_ref[...]
        o_ref[:, 3 * LANE:4 * LANE] = dg_ref[...]

        @pl.when(i == 0)
        def _():
            dqg_ref[...] = pq
            dkg_ref[...] = pk

        @pl.when(i > 0)
        def _():
            dqg_ref[...] += pq
            dkg_ref[...] += pk

    blk = pl.BlockSpec((tt, LANE), lambda p, i: (i, p))
    vec = pl.BlockSpec((1, LANE), lambda p, i: (0, 0))
    acc = pl.BlockSpec((None, 1, LANE), lambda p, i: (p, 0, 0))
    wide = pl.BlockSpec((tt, 4 * LANE), lambda p, i: (i, p))
    return pl.pallas_call(
        body, name=name, grid=(SB_PAIRS, T // tt),
        in_specs=[wide, blk, blk, blk, blk, vec, vec],
        out_specs=[wide, acc, acc],
        out_shape=[jax.ShapeDtypeStruct((T, 4 * SB_W), F32)] + [jax.ShapeDtypeStruct((SB_PAIRS, 1, LANE), F32)] * 2,
        compiler_params=_params("parallel", "arbitrary"),
    )(proj, dqn, dkn, dv, dgate, qg2, kg2)


def _fold_heads(part, name):
    def body(p_ref, o_ref):
        r, c = _iota2((LANE, SB_DH), 0), _iota2((LANE, SB_DH), 1)
        fold = jnp.where((r % SB_DH) == c, 1.0, 0.0).astype(F32)
        o_ref[...] = jnp.sum(_hdot(p_ref[...], fold), axis=0, keepdims=True)

    return pl.pallas_call(body, name=name, out_shape=jax.ShapeDtypeStruct((1, SB_DH), F32))(part)


def _sb_masks():
    lane = _iota2((1, LANE), 1)
    return lane < SB_DH


def _sb_attn_fwd(qn, kn, vb, proj, name):
    T = qn.shape[0]
    tq, tk = _tile(T, SB_TQ, SUBLANE), SB_TK
    assert tq % tk == 0

    def body(q_ref, k_ref, v_ref, g_ref, o_ref, og_ref, lt_ref):
        i = pl.program_id(1)
        ma = _sb_masks()
        q2 = q_ref[...]
        zero = jnp.zeros_like(q2)
        qs = (jnp.where(ma, q2, zero), jnp.where(ma, zero, q2))
        upper = (_iota2((tk, tk), 0) > _iota2((tk, tk), 1)).astype(BF16)
        qpos = i * tq + _iota2((tq, tk), 0)
        nb = tq // tk

        def trip(kb_top, masked, carry):
            acc, la, lb = carry
            chains = [(b, h) for b in range(nb) for h in range(2)]
            k2s, vss, masks = [], [], []
            for b in range(nb):
                kb = kb_top - b
                rows = pl.ds(pl.multiple_of(kb * tk, tk), tk)
                k2s.append(k_ref[rows, :])
                v2 = v_ref[rows, :]
                zv = jnp.zeros_like(v2)
                vss.append((jnp.where(ma, v2, zv), jnp.where(ma, zv, v2)))
                masks.append((kb * tk + _iota2((tq, tk), 1)) < qpos if masked else None)
            zs = [_dot(qs[h], k2s[b], NT) for b, h in chains]
            ts = [jnp.log(1.0 + jnp.exp(-jnp.abs(z))) for z in zs]
            ls = [-(jnp.maximum(z, 0.0) + t) for z, t in zip(zs, ts)]
            if masked:
                ls = [jnp.where(masks[b], l, 0.0) for (b, h), l in zip(chains, ls)]
            cums = [_split2_dot(l, upper) for l in ls]
            sums = [jnp.sum(l, axis=1, keepdims=True) for l in ls]
            offs, tot = {}, [la, lb]
            for b in range(nb):
                for h in range(2):
                    offs[(b, h)] = tot[h]
                    tot[h] = tot[h] + sums[chains.index((b, h))]
            ws = [jnp.exp(jnp.minimum(z, 0.0) - t + c + offs[ch]) for ch, z, t, c in zip(chains, zs, ts, cums)]
            if masked:
                ws = [jnp.where(masks[b], w, 0.0) for (b, h), w in zip(chains, ws)]
            for (b, h), w in zip(chains, ws):
                acc = acc + _dot(w.astype(BF16), vss[b][h])
            return acc, tot[0], tot[1]

        z1 = jnp.zeros((tq, 1), F32)
        carry = trip((i + 1) * nb - 1, True, (jnp.zeros((tq, LANE), F32), z1, z1))
        acc, la, lb = lax.fori_loop(0, i, lambda j, c: trip((i - j) * nb - 1, False, c), carry)
        gate = g_ref[...]
        o_ref[...] = acc
        og_ref[...] = (acc * (gate * _sigmoid(gate))).astype(BF16)
        lt_ref[...] = jnp.where(_iota2((tq, 2), 1) == 0, la, lb)

    qblk = pl.BlockSpec((tq, LANE), lambda p, i: (i, p))
    full = pl.BlockSpec((T, LANE), lambda p, i: (0, p))
    return pl.pallas_call(
        body, name=name, grid=(SB_PAIRS, T // tq),
        in_specs=[qblk, full, full, pl.BlockSpec((tq, LANE), lambda p, i: (i, 4 * p + 3))],
        out_specs=[qblk, qblk, pl.BlockSpec((None, tq, 2), lambda p, i: (p, i, 0))],
        out_shape=[jax.ShapeDtypeStruct((T, SB_W), F32), jax.ShapeDtypeStruct((T, SB_W), BF16),
                   jax.ShapeDtypeStruct((SB_PAIRS, T, 2), F32)],
        compiler_params=_params("parallel", "parallel"),
    )(qn, kn, vb, proj)


def _sb_attn_bwd(qn, kn, vb, dog, o, ltot, proj, name):
    T = qn.shape[0]
    tq, tk = _tile(T, SB_TQ, SUBLANE), SB_TK

    def body(q_ref, k_ref, v_ref, dog_ref, o_ref, lt_ref, g_ref, dq_ref, dk_ref, dv_ref, dgate_ref):
        i = pl.program_id(1)

        @pl.when(i == 0)
        def _():
            dk_ref[...] = jnp.zeros_like(dk_ref)
            dv_ref[...] = jnp.zeros_like(dv_ref)

        ma = _sb_masks()
        gate, o2, dog2 = g_ref[...], o_ref[...], dog_ref[...]
        sg, dsg = _silu_and_grad(gate)
        do2 = dog2 * sg
        dgate_ref[...] = dog2 * o2 * dsg
        lt = lt_ref[...]
        first = _iota2((tq, 2), 1) == 0
        ltots = (jnp.sum(jnp.where(first, lt, 0.0), axis=1, keepdims=True),
                 jnp.sum(jnp.where(first, 0.0, lt), axis=1, keepdims=True))
        q2 = q_ref[...]
        zq = jnp.zeros_like(q2)
        qs = (jnp.where(ma, q2, zq), jnp.where(ma, zq, q2))
        dob = do2.astype(BF16)
        dos = (jnp.where(ma, dob, zq), jnp.where(ma, zq, dob))
        upto = (_iota2((tk, tk), 0) <= _iota2((tk, tk), 1)).astype(BF16)
        before = (_iota2((tk, tk), 0) < _iota2((tk, tk), 1)).astype(BF16)
        qpos = i * tq + _iota2((tq, tk), 0)
        nb = tq // tk

        def trip(kb_bot, masked, carry):
            dq, la, lb, ea, eb = carry
            chains = [(b, h) for b in range(nb) for h in range(2)]
            rows, k2s, v2s, kss, masks = [], [], [], [], []
            for b in range(nb):
                kb = kb_bot + b
                rows.append(pl.ds(pl.multiple_of(kb * tk, tk), tk))
                k2 = k_ref[rows[b], :]
                zk = jnp.zeros_like(k2)
                k2s.append(k2)
                v2s.append(v_ref[rows[b], :])
                kss.append((jnp.where(ma, k2, zk), jnp.where(ma, zk, k2)))
                masks.append((kb * tk + _iota2((tq, tk), 1)) < qpos if masked else None)

            def keep(vals):
                return [jnp.where(masks[b], x, 0.0) for (b, h), x in zip(chains, vals)] if masked else vals

            zs = [_dot(qs[h], k2s[b], NT) for b, h in chains]
            dws = [_dot(dos[h], v2s[b], NT) for b, h in chains]
            ts = [jnp.log(1.0 + jnp.exp(-jnp.abs(z))) for z in zs]
            ls = keep([-(jnp.maximum(z, 0.0) + t) for z, t in zip(zs, ts)])
            lps = [jnp.minimum(z, 0.0) - t for z, t in zip(zs, ts)]
            cums = [_split3_dot(l, upto) for l in ls]
            lsums = [jnp.sum(l, axis=1, keepdims=True) for l in ls]
            offs, tot = {}, [la, lb]
            for b in range(nb):
                for h in range(2):
                    offs[(b, h)] = tot[h]
                    tot[h] = tot[h] + lsums[chains.index((b, h))]
            ws = keep([jnp.exp(lp + (ltots[h] - (offs[(b, h)] + c))) for (b, h), lp, c in zip(chains, lps, cums)])
            es = [dw * w for dw, w in zip(dws, ws)]
            ecums = [_split2_dot(e, before) for e in es]
            esums = [jnp.sum(e, axis=1, keepdims=True) for e in es]
            eoffs, etot = {}, [ea, eb]
            for b in range(nb):
                for h in range(2):
                    eoffs[(b, h)] = etot[h]
                    etot[h] = etot[h] + esums[chains.index((b, h))]
            dzs = keep([e - jnp.exp(lp) * (e + eoffs[ch] + ec) for ch, e, lp, ec in zip(chains, es, lps, ecums)])
            dzs = [dz.astype(BF16) for dz in dzs]
            wbs = [w.astype(BF16) for w in ws]
            for (b, h), dz in zip(chains, dzs):
                dq = dq + _dot(dz, kss[b][h])
            for b in range(nb):
                ia, ib = chains.index((b, 0)), chains.index((b, 1))
                dk_ref[rows[b], :] += _dot(dzs[ia], qs[0], TN) + _dot(dzs[ib], qs[1], TN)
                dv_ref[rows[b], :] += _dot(wbs[ia], dos[0], TN) + _dot(wbs[ib], dos[1], TN)
            return dq, tot[0], tot[1], etot[0], etot[1]

        z1 = jnp.zeros((tq, 1), F32)
        carry = lax.fori_loop(0, i, lambda j, c: trip(j * nb, False, c), (jnp.zeros((tq, LANE), F32), z1, z1, z1, z1))
        dq = trip(i * nb, True, carry)[0]
        dq_ref[...] = dq * (SB_DH ** -0.5)

    qblk = pl.BlockSpec((tq, LANE), lambda p, i: (i, p))
    full = pl.BlockSpec((T, LANE), lambda p, i: (0, p))
    return pl.pallas_call(
        body, name=name, grid=(SB_PAIRS, T // tq),
        in_specs=[qblk, full, full, qblk, qblk, pl.BlockSpec((None, tq, 2), lambda p, i: (p, i, 0)),
                  pl.BlockSpec((tq, LANE), lambda p, i: (i, 4 * p + 3))],
        out_specs=[qblk, full, full, qblk],
        out_shape=[jax.ShapeDtypeStruct((T, SB_W), F32)] * 4,
        compiler_params=_params("parallel", "arbitrary"),
    )(qn, kn, vb, dog, o, ltot, proj)


def _dn_conv(ext, w_ref):
    return (w_ref[3:4, :] * ext + w_ref[2:3, :] * _down(ext, 1) + w_ref[1:2, :] * _down(ext, 2)
            + w_ref[0:1, :] * _down(ext, 3))


def _dn_prep(pqkv, conv_w, name):
    T, W = pqkv.shape
    tt = _tile(T, ROW_TILE, SUBLANE)
    B = DN_PREP_BLK
    nq, nqk = DN_QK_W // B, 2 * DN_QK_W // B

    def body(p_ref, ph_ref, w_ref, o_ref):
        i, cb = pl.program_id(0), pl.program_id(1)
        keep = (i > 0).astype(F32)
        ext = jnp.concatenate([ph_ref[...] * keep, p_ref[...]], axis=0)
        c = _dn_conv(ext, w_ref)[HALO:]
        a = c * _sigmoid(c)

        @pl.when(cb >= nqk)
        def _():
            o_ref[...] = a

        @pl.when(cb < nqk)
        def _():
            scale = jnp.where(cb < nq, DN_DK ** -0.5, 1.0)
            for hh in range(B // DN_DK):
                cols = slice(hh * DN_DK, (hh + 1) * DN_DK)
                ah = a[:, cols]
                o_ref[:, cols] = ah * (lax.rsqrt(jnp.sum(ah * ah, axis=-1, keepdims=True) + L2_EPS) * scale)

    return pl.pallas_call(
        body, name=name, grid=(T // tt, W // B),
        in_specs=[pl.BlockSpec((tt, B), lambda i, c: (i, c)),
                  pl.BlockSpec((HALO, B), lambda i, c: (jnp.maximum(i * (tt // HALO) - 1, 0), c)),
                  pl.BlockSpec((DN_CONV, B), lambda i, c: (0, c))],
        out_specs=pl.BlockSpec((tt, B), lambda i, c: (i, c)),
        out_shape=jax.ShapeDtypeStruct((T, W), F32),
        compiler_params=_params("parallel", "parallel"),
    )(pqkv, pqkv, conv_w)


def _dn_prep_bwd(pqkv, conv_w, dact, name):
    T, W = pqkv.shape
    tt = _tile(T, ROW_TILE, SUBLANE)
    nt = T // tt
    hb = tt // HALO
    B = DN_PREP_BLK
    nq, nqk = DN_QK_W // B, 2 * DN_QK_W // B

    def body(p_ref, pp_ref, pn_ref, w_ref, d_ref, dn_ref, o_ref, dw_ref):
        cb, i = pl.program_id(0), pl.program_id(1)
        keep_p = (i > 0).astype(F32)
        keep_n = (i < nt - 1).astype(F32)
        ext = jnp.concatenate([pp_ref[...] * keep_p, p_ref[...], pn_ref[...]], axis=0)
        c = _dn_conv(ext, w_ref)
        s = _sigmoid(c)
        a = c * s
        da_dc = s * (1.0 + c * (1.0 - s))
        d_up = jnp.concatenate([jnp.zeros((HALO, B), F32), d_ref[...], dn_ref[...] * keep_n], axis=0)
        scale = jnp.where(cb < nq, DN_DK ** -0.5, 1.0)
        normed = []
        for hh in range(B // DN_DK):
            cols = slice(hh * DN_DK, (hh + 1) * DN_DK)
            ah = a[:, cols]
            r = lax.rsqrt(jnp.sum(ah * ah, axis=-1, keepdims=True) + L2_EPS)
            y = ah * r
            dy = d_up[:, cols] * scale
            normed.append(r * (dy - y * jnp.sum(dy * y, axis=-1, keepdims=True)))
        dc = jnp.where(cb < nqk, jnp.concatenate(normed, axis=1), d_up) * da_dc
        dp = (w_ref[3:4, :] * dc + w_ref[2:3, :] * _up(dc, 1) + w_ref[1:2, :] * _up(dc, 2) + w_ref[0:1, :] * _up(dc, 3))
        main = slice(HALO, HALO + tt)
        o_ref[...] = dp[main]
        dcm = dc[main]
        part = jnp.concatenate([jnp.sum(dcm * _down(ext, 3 - k)[main], axis=0, keepdims=True) for k in range(DN_CONV)], axis=0)

        @pl.when(i == 0)
        def _():
            dw_ref[...] = part

        @pl.when(i > 0)
        def _():
            dw_ref[...] += part

    main_spec = pl.BlockSpec((tt, B), lambda c, i: (i, c))
    prev_spec = pl.BlockSpec((HALO, B), lambda c, i: (jnp.maximum(i * hb - 1, 0), c))
    next_spec = pl.BlockSpec((HALO, B), lambda c, i: (jnp.minimum((i + 1) * hb, nt * hb - 1), c))
    w_spec = pl.BlockSpec((DN_CONV, B), lambda c, i: (0, c))
    return pl.pallas_call(
        body, name=name, grid=(W // B, nt),
        in_specs=[main_spec, prev_spec, next_spec, w_spec, main_spec, next_spec],
        out_specs=[main_spec, w_spec],
        out_shape=[jax.ShapeDtypeStruct((T, W), F32), jax.ShapeDtypeStruct((DN_CONV, W), F32)],
        compiler_params=_params("parallel", "arbitrary"),
    )(pqkv, pqkv, pqkv, conv_w, dact, dact)


def _dn_gates(a_in, b_in, a_log, dt_bias, name):
    T, H = a_in.shape
    C = DN_CHUNK

    def body(a_ref, b_ref, al_ref, dt_ref, g_ref, beta_ref):
        beta_ref[...] = _sigmoid(b_ref[...])
        g_ref[...] = -jnp.exp(al_ref[...]) * _softplus(a_ref[...] + dt_ref[...])
        tri = (_iota2((C, C), 0) >= _iota2((C, C), 1)).astype(F32)

        def chunk(n, carry):
            rows = pl.ds(pl.multiple_of(n * C, C), C)
            g_ref[rows, :] = _hdot(tri, g_ref[rows, :])
            return carry

        lax.fori_loop(0, T // C, chunk, 0)

    return pl.pallas_call(body, name=name, out_shape=[jax.ShapeDtypeStruct((T, H), F32)] * 2)(a_in, b_in, a_log, dt_bias)


def _dn_gates_bwd(dg, dbeta, a_in, b_in, a_log, dt_bias, name):
    T, H = a_in.shape
    C = DN_CHUNK

    def body(dg_ref, db_ref, a_ref, b_ref, al_ref, dt_ref, da_ref, dbi_ref, dal_ref, ddt_ref):
        tri_t = (_iota2((C, C), 0) <= _iota2((C, C), 1)).astype(F32)

        def chunk(n, carry):
            rows = pl.ds(pl.multiple_of(n * C, C), C)
            da_ref[rows, :] = _hdot(tri_t, dg_ref[rows, :])
            return carry

        lax.fori_loop(0, T // C, chunk, 0)
        dla = da_ref[...]
        x = a_ref[...] + dt_ref[...]
        ea = jnp.exp(al_ref[...])
        da = dla * (-ea) * _sigmoid(x)
        da_ref[...] = da
        dal_ref[...] = jnp.sum(dla * (-ea * _softplus(x)), axis=0, keepdims=True)
        ddt_ref[...] = jnp.sum(da, axis=0, keepdims=True)
        beta = _sigmoid(b_ref[...])
        dbi_ref[...] = db_ref[...] * beta * (1.0 - beta)

    return pl.pallas_call(
        body, name=name,
        out_shape=[jax.ShapeDtypeStruct((T, H), F32)] * 2 + [jax.ShapeDtypeStruct((1, H), F32)] * 2,
    )(dg, dbeta, a_in, b_in, a_log, dt_bias)


def _dn_post(o_raw, pgate, gn, name):
    T = o_raw.shape[0]
    tt = _tile(T, ROW_TILE, SUBLANE)

    def body(o_ref, g_ref, gn_ref, out_ref):
        o, gate = o_ref[...], g_ref[...]
        r = lax.rsqrt(jnp.mean(o * o, axis=-1, keepdims=True) + RMS_EPS)
        out_ref[...] = (o * r * gn_ref[...] * (gate * _sigmoid(gate))).astype(BF16)

    blk = pl.BlockSpec((tt, DN_DV), lambda i, h: (i, h))
    return pl.pallas_call(
        body, name=name, grid=(T // tt, DN_HEADS),
        in_specs=[blk, blk, pl.BlockSpec((1, DN_DV), lambda i, h: (0, 0))], out_specs=blk,
        out_shape=jax.ShapeDtypeStruct((T, DN_V_W), BF16),
        compiler_params=_params("parallel", "parallel"),
    )(o_raw, pgate, gn)


def _dn_post_bwd(dog, o_raw, pgate, gn, name):
    T = o_raw.shape[0]
    tt = _tile(T, ROW_TILE, SUBLANE)

    def body(d_ref, o_ref, g_ref, gn_ref, do_ref, dgate_ref, dgn_ref):
        d, o, gate, gn_v = d_ref[...], o_ref[...], g_ref[...], gn_ref[...]
        sg, dsg = _silu_and_grad(gate)
        r = lax.rsqrt(jnp.mean(o * o, axis=-1, keepdims=True) + RMS_EPS)
        n = o * r
        dy = d * sg
        dgate_ref[...] = d * (n * gn_v) * dsg
        dn = dy * gn_v
        do_ref[...] = r * (dn - n * jnp.mean(dn * n, axis=-1, keepdims=True))
        part = jnp.sum(dy * n, axis=0, keepdims=True)
        first = (pl.program_id(0) == 0) & (pl.program_id(1) == 0)

        @pl.when(first)
        def _():
            dgn_ref[...] = part

        @pl.when(jnp.logical_not(first))
        def _():
            dgn_ref[...] += part

    blk = pl.BlockSpec((tt, DN_DV), lambda i, h: (i, h))
    vec = pl.BlockSpec((1, DN_DV), lambda i, h: (0, 0))
    return pl.pallas_call(
        body, name=name, grid=(T // tt, DN_HEADS),
        in_specs=[blk, blk, blk, vec], out_specs=[blk, blk, vec],
        out_shape=[jax.ShapeDtypeStruct((T, DN_V_W), F32)] * 2 + [jax.ShapeDtypeStruct((1, DN_DV), F32)],
        compiler_params=_params("arbitrary", "arbitrary"),
    )(dog, o_raw, pgate, gn)


def _dn_chunk_terms(q, k, gc, bc):
    C = DN_CHUNK
    r, c = _iota2((C, C), 0), _iota2((C, C), 1)
    lower, strict, eye = r >= c, r > c, r == c
    grow = jnp.sum(jnp.where(eye, gc, 0.0), axis=0, keepdims=True)
    decay = jnp.where(lower, jnp.exp(jnp.where(lower, gc - grow, 0.0)), 0.0)
    last = _iota2((C, 1), 0) == C - 1
    gl = jnp.sum(jnp.where(last, gc, 0.0), axis=0, keepdims=True)
    eg = jnp.exp(gc)
    egl = jnp.exp(gl - gc)
    kb = k * bc
    lmat = jnp.where(strict, _bdot(kb, k, NT) * decay, 0.0)
    aqk = jnp.where(lower, _bdot(q, k, NT) * decay, 0.0)
    return dict(lower=lower, strict=strict, eye=eye, last=last, decay=decay, gl=gl, eg=eg, egl=egl, kb=kb,
                lmat=lmat, aqk=aqk, qd=q * eg, kd=k * egl)


def _split(x):
    hi = x.astype(BF16)
    return hi, (x - hi.astype(F32)).astype(BF16)


def _x3dot(a, b, dims=NN):
    ah, al = a if isinstance(a, tuple) else _split(a)
    bh, bl = b if isinstance(b, tuple) else _split(b)
    return _dot(ah, bh, dims) + (_dot(ah, bl, dims) + _dot(al, bh, dims))


def _interleave(gens):
    for _ in itertools.zip_longest(*gens):
        pass


def _unit_lower_inverse_steps(lmat, eye, out):
    ident = jnp.where(eye, 1.0, 0.0).astype(F32)
    m = -lmat
    inv = ident + m
    for _ in range(int(math.log2(DN_CHUNK)) - 1):
        ms = _split(m)
        m = _x3dot(ms, ms)
        yield
        inv = inv + _x3dot(inv, m)
        yield
    out["tm"] = inv


def _dn_chunk_fwd(act, g, beta, name):
    T = act.shape[0]
    C, H, hb = DN_CHUNK, DN_HEADS, DN_HEADS_PER_STEP
    N = T // C
    kblk0 = DN_QK_W // (hb * DN_DK)
    vblk0 = 2 * DN_QK_W // (hb * DN_DV)

    def body(q_ref, k_ref, v_ref, g_ref, b_ref, o_ref, s_out, t_out, vn_out, u_out, w_out, s_scr):
        n = pl.program_id(1)

        @pl.when(n == 0)
        def _():
            s_scr[...] = jnp.zeros_like(s_scr)

        def head(hh):
            qs, vs = slice(hh * DN_DK, (hh + 1) * DN_DK), slice(hh * DN_DV, (hh + 1) * DN_DV)
            q, k, v = q_ref[:, qs], k_ref[:, qs], v_ref[:, vs]
            gc, bc = g_ref[hh], b_ref[hh]
            t = _dn_chunk_terms(q, k, gc, bc)
            yield
            res = {}
            yield from _unit_lower_inverse_steps(t["lmat"], t["eye"], res)
            tms = _split(res["tm"])
            u = _x3dot(tms, v * bc)
            yield
            w = _x3dot(tms, t["kb"] * t["eg"])
            yield
            s = s_scr[hh]
            s_out[hh] = s
            t_out[hh] = res["tm"]
            sb = s.astype(BF16)
            vn = u - _dot(w.astype(BF16), sb)
            yield
            o_ref[:, vs] = _dot(t["qd"].astype(BF16), sb) + _bdot(t["aqk"], vn)
            yield
            s_scr[hh] = s * jnp.exp(t["gl"]) + _bdot(t["kd"], vn, TN)
            vn_out[:, vs] = vn
            u_out[:, vs] = u
            w_out[:, qs] = w

        _interleave([head(hh) for hh in range(hb)])

    qk_w, v_w = hb * DN_DK, hb * DN_DV
    return pl.pallas_call(
        body, name=name, grid=(H // hb, N),
        in_specs=[pl.BlockSpec((C, qk_w), lambda p, n: (n, p)),
                  pl.BlockSpec((C, qk_w), lambda p, n: (n, kblk0 + p)),
                  pl.BlockSpec((C, v_w), lambda p, n: (n, vblk0 + p)),
                  pl.BlockSpec((hb, C, 1), lambda p, n: (p, n, 0)),
                  pl.BlockSpec((hb, C, 1), lambda p, n: (p, n, 0))],
        out_specs=[pl.BlockSpec((C, v_w), lambda p, n: (n, p)),
                   pl.BlockSpec((hb, None, DN_DK, DN_DV), lambda p, n: (p, n, 0, 0)),
                   pl.BlockSpec((hb, None, C, C), lambda p, n: (p, n, 0, 0)),
                   pl.BlockSpec((C, v_w), lambda p, n: (n, p)),
                   pl.BlockSpec((C, v_w), lambda p, n: (n, p)),
                   pl.BlockSpec((C, qk_w), lambda p, n: (n, p))],
        out_shape=[jax.ShapeDtypeStruct((T, DN_V_W), F32),
                   jax.ShapeDtypeStruct((H, N, DN_DK, DN_DV), F32),
                   jax.ShapeDtypeStruct((H, N, C, C), F32),
                   jax.ShapeDtypeStruct((T, DN_V_W), F32),
                   jax.ShapeDtypeStruct((T, DN_V_W), F32),
                   jax.ShapeDtypeStruct((T, DN_QK_W), F32)],
        scratch_shapes=[pltpu.VMEM((hb, DN_DK, DN_DV), F32)],
        compiler_params=_params("parallel", "arbitrary"),
    )(act, act, act, g, beta)


def _dn_chunk_bwd(act, g, beta, s_saved, tm_saved, vn_saved, u_saved, w_saved, do, name):
    T = act.shape[0]
    C, H, hb = DN_CHUNK, DN_HEADS, DN_HEADS_PER_STEP
    N = T // C
    kblk0 = DN_QK_W // (hb * DN_DK)
    vblk0 = 2 * DN_QK_W // (hb * DN_DV)

    def body(q_ref, k_ref, v_ref, g_ref, b_ref, s_ref, t_ref, vn_ref, u_ref, w_ref, do_ref,
             dq_ref, dk_ref, dv_ref, dg_ref, db_ref, ds_scr):
        @pl.when(pl.program_id(1) == 0)
        def _():
            ds_scr[...] = jnp.zeros_like(ds_scr)

        def head(hh):
            qs, vs = slice(hh * DN_DK, (hh + 1) * DN_DK), slice(hh * DN_DV, (hh + 1) * DN_DV)
            q, k, v = q_ref[:, qs], k_ref[:, qs], v_ref[:, vs]
            gc, bc = g_ref[hh], b_ref[hh]
            t = _dn_chunk_terms(q, k, gc, bc)
            yield
            lower, strict, eye = t["lower"], t["strict"], t["eye"]
            decay, eg, egl, kb, qd, kd = t["decay"], t["eg"], t["egl"], t["kb"], t["qd"], t["kd"]
            s, tm, vn, u, w, d_o = s_ref[hh], t_ref[hh], vn_ref[:, vs], u_ref[:, vs], w_ref[:, qs], do_ref[:, vs]
            ds_next = ds_scr[hh]
            egl_tot = jnp.exp(t["gl"])
            dob, sb, dsb, vnb = d_o.astype(BF16), s.astype(BF16), ds_next.astype(BF16), vn.astype(BF16)

            dvn = _bdot(t["aqk"], dob, TN) + _bdot(kd, dsb)
            yield
            daqk = jnp.where(lower, _dot(dob, vnb, NT), 0.0)
            dqd = _dot(dob, sb, NT)
            dkd = _dot(vnb, dsb, NT)
            yield
            dvnb = dvn.astype(BF16)
            ds_scr[hh] = _bdot(qd, dob, TN) + egl_tot * ds_next - _bdot(w, dvnb, TN)
            dgl = egl_tot * jnp.sum(jnp.sum(s * ds_next, axis=1, keepdims=True), axis=0, keepdims=True)
            dw = -_dot(dvnb, sb, NT)
            yield
            tms = _split(tm)
            dru = _x3dot(tms, dvn, TN)
            drw = _x3dot(tms, dw, TN)
            yield
            dl = -jnp.where(strict, _x3dot(dru, u, NT) + _x3dot(drw, w, NT), 0.0)
            yield
            dkk = (dl * decay).astype(BF16)
            dqk = (daqk * decay).astype(BF16)
            dkb = _bdot(dkk, k) + drw * eg
            yield
            dk_ref[:, qs] = _bdot(dkk, kb, TN) + _bdot(dqk, q, TN) + dkd * egl + dkb * bc
            dq_ref[:, qs] = _bdot(dqk, k) + dqd * eg
            dv_ref[:, vs] = dru * bc
            yield
            db_ref[hh] = jnp.sum(dru * v, axis=1, keepdims=True) + jnp.sum(dkb * k, axis=1, keepdims=True)
            pm = dl * t["lmat"] + daqk * t["aqk"]
            col_as_col = jnp.sum(jnp.where(eye, jnp.sum(pm, axis=0, keepdims=True), 0.0), axis=1, keepdims=True)
            kdsum = jnp.sum(dkd * kd, axis=1, keepdims=True)
            dgc = (jnp.sum(pm, axis=1, keepdims=True) - col_as_col + jnp.sum(dqd * qd, axis=1, keepdims=True)
                   - kdsum + jnp.sum(drw * (kb * eg), axis=1, keepdims=True))
            dgl = dgl + jnp.sum(kdsum, axis=0, keepdims=True)
            dg_ref[hh] = dgc + jnp.where(t["last"], dgl, 0.0)

        _interleave([head(hh) for hh in range(hb)])

    qk_w, v_w = hb * DN_DK, hb * DN_DV
    rn = lambda n: N - 1 - n
    qk_out = pl.BlockSpec((C, qk_w), lambda p, n: (rn(n), p))
    v_out = pl.BlockSpec((C, v_w), lambda p, n: (rn(n), p))
    col = pl.BlockSpec((hb, C, 1), lambda p, n: (p, rn(n), 0))
    return pl.pallas_call(
        body, name=name, grid=(H // hb, N),
        in_specs=[pl.BlockSpec((C, qk_w), lambda p, n: (rn(n), p)),
                  pl.BlockSpec((C, qk_w), lambda p, n: (rn(n), kblk0 + p)),
                  pl.BlockSpec((C, v_w), lambda p, n: (rn(n), vblk0 + p)),
                  col, col,
                  pl.BlockSpec((hb, None, DN_DK, DN_DV), lambda p, n: (p, rn(n), 0, 0)),
                  pl.BlockSpec((hb, None, C, C), lambda p, n: (p, rn(n), 0, 0)),
                  v_out, v_out, qk_out, v_out],
        out_specs=[qk_out, qk_out, v_out, col, col],
        out_shape=[jax.ShapeDtypeStruct((T, DN_QK_W), F32), jax.ShapeDtypeStruct((T, DN_QK_W), F32),
                   jax.ShapeDtypeStruct((T, DN_V_W), F32),
                   jax.ShapeDtypeStruct((H, T, 1), F32), jax.ShapeDtypeStruct((H, T, 1), F32)],
        scratch_shapes=[pltpu.VMEM((hb, DN_DK, DN_DV), F32)],
        compiler_params=_params("parallel", "arbitrary"),
    )(act, act, act, g, beta, s_saved, tm_saved, vn_saved, u_saved, w_saved, do)


def _dn_split_w_in(w):
    wab = jnp.pad(w[:, DN_CONV_W + DN_V_W:], ((0, 0), (0, DN_AB_PAD - 2 * DN_HEADS)))
    return w[:, :DN_CONV_W], w[:, DN_CONV_W:DN_CONV_W + DN_V_W], wab


def _cols_to_heads(x):
    return x.T[:, :, None]


def _heads_to_cols(x):
    return x[:, :, 0].T


def _dn_layer_fwd(h, wts, conv_w, a_log, dt_bias, gn, w_out, x_res, tag):
    wqkv, wgate, wab = wts
    H = DN_HEADS
    pqkv = _matmul(h, wqkv, "nn", tag + "_pqkv")
    pgate = _matmul(h, wgate, "nn", tag + "_pgate")
    pab = _matmul(h, wab, "nn", tag + "_pab")
    a_in, b_in = pab[:, :H], pab[:, H:2 * H]
    g, beta = _dn_gates(a_in, b_in, a_log, dt_bias, tag + "_gates")
    gh, bh = _cols_to_heads(g), _cols_to_heads(beta)
    act = _dn_prep(pqkv, conv_w, tag + "_prep")
    o_raw, s_sv, tm_sv, vn_sv, u_sv, w_sv = _dn_chunk_fwd(act, gh, bh, tag + "_chunk_fwd")
    og = _dn_post(o_raw, pgate, gn, tag + "_post")
    y = _matmul(og, w_out, "nn", tag + "_out", add=x_res)
    saved = dict(h=h, wts=wts, conv_w=conv_w, a_log=a_log, dt_bias=dt_bias, gn=gn, w_out=w_out, pqkv=pqkv, pgate=pgate,
                 a_in=a_in, b_in=b_in, gh=gh, bh=bh, act=act, o_raw=o_raw, chunk=(s_sv, tm_sv, vn_sv, u_sv, w_sv), og=og)
    return y, saved


def _dn_layer_bwd(dout, sv, tag):
    wqkv, wgate, wab = sv["wts"]
    h = sv["h"]
    dog = _matmul(dout, sv["w_out"], "nt", tag + "_dog")
    dw_out = _matmul(sv["og"], dout, "tn", tag + "_dwout")
    do_raw, dgate, dgn = _dn_post_bwd(dog, sv["o_raw"], sv["pgate"], sv["gn"], tag + "_post_bwd")
    dq, dk, dv, dgh, dbh = _dn_chunk_bwd(sv["act"], sv["gh"], sv["bh"], *sv["chunk"], do_raw, tag + "_chunk_bwd")
    da_in, db_in, da_log, ddt = _dn_gates_bwd(_heads_to_cols(dgh), _heads_to_cols(dbh), sv["a_in"], sv["b_in"],
                                              sv["a_log"], sv["dt_bias"], tag + "_gates_bwd")
    dact = jnp.concatenate([dq, dk, dv], axis=1)
    dpqkv, dconv = _dn_prep_bwd(sv["pqkv"], sv["conv_w"], dact, tag + "_prep_bwd")
    dpab = jnp.pad(jnp.concatenate([da_in, db_in], axis=1), ((0, 0), (0, DN_AB_PAD - 2 * DN_HEADS)))
    dwqkv = _matmul(h, dpqkv, "tn", tag + "_dwqkv")
    dwgate = _matmul(h, dgate, "tn", tag + "_dwgate")
    dwab = _matmul(h, dpab, "tn", tag + "_dwab")
    dh = _matmul(dpqkv, wqkv, "nt", tag + "_dh0")
    dh = _matmul(dgate, wgate, "nt", tag + "_dh1", add=dh)
    dh = _matmul(dpab, wab, "nt", tag + "_dh2", add=dh)
    dw_in = jnp.concatenate([dwqkv, dwgate, dwab[:, :2 * DN_HEADS]], axis=1)
    return dh, (dw_in, dconv, da_log, ddt, dgn, dw_out)


def _sb_layer_fwd(h, w_in_perm, qg, kg, w_out, x_res, tag):
    qg2, kg2 = jnp.tile(qg, (1, 2)), jnp.tile(kg, (1, 2))
    proj = _matmul(h, w_in_perm, "nn", tag + "_proj")
    qn, kn, vb = _sb_prep(proj, qg2, kg2, tag + "_prep")
    o, og, ltot = _sb_attn_fwd(qn, kn, vb, proj, tag + "_attn_fwd")
    y = _matmul(og, w_out, "nn", tag + "_out", add=x_res)
    saved = dict(h=h, w_in=w_in_perm, qg2=qg2, kg2=kg2, w_out=w_out, proj=proj, qn=qn, kn=kn, vb=vb, o=o, og=og, ltot=ltot)
    return y, saved


def _sb_layer_bwd(dout, sv, tag):
    dog = _matmul(dout, sv["w_out"], "nt", tag + "_dog")
    dw_out = _matmul(sv["og"], dout, "tn", tag + "_dwout")
    dqn, dkn, dv, dgate = _sb_attn_bwd(sv["qn"], sv["kn"], sv["vb"], dog, sv["o"], sv["ltot"], sv["proj"], tag + "_attn_bwd")
    dproj, dqgp, dkgp = _sb_prep_bwd(sv["proj"], dqn, dkn, dv, dgate, sv["qg2"], sv["kg2"], tag + "_prep_bwd")
    dw_in = _sb_unperm(_matmul(sv["h"], dproj, "tn", tag + "_dwin"))
    dh = _matmul(dproj, sv["w_in"], "nt", tag + "_dh")
    dqg = _fold_heads(dqgp.reshape(SB_PAIRS, LANE), tag + "_dqg")
    dkg = _fold_heads(dkgp.reshape(SB_PAIRS, LANE), tag + "_dkg")
    return dh, (dw_in, dqg, dkg, dw_out)


def _sc_layer_fwd(h, w_in_perm, conv_w, w_out, x_res, tag):
    proj = _matmul(h, w_in_perm, "nn", tag + "_proj")
    yg = _sc_fwd(proj, conv_w, tag + "_fwd")
    y = _matmul(yg, w_out, "nn", tag + "_out", add=x_res)
    return y, dict(h=h, w_in=w_in_perm, conv_w=conv_w, w_out=w_out, proj=proj, yg=yg)


def _sc_layer_bwd(dout, sv, tag):
    dyg = _matmul(dout, sv["w_out"], "nt", tag + "_dyg")
    dw_out = _matmul(sv["yg"], dout, "tn", tag + "_dwout")
    dproj, dconv = _sc_bwd(dyg, sv["proj"], sv["conv_w"], tag + "_bwd")
    dw_in = _sc_unperm(_matmul(sv["h"], dproj, "tn", tag + "_dwin"))
    dh = _matmul(dproj, sv["w_in"], "nt", tag + "_dh")
    return dh, (dw_in, dconv, dw_out)


def _local_step(x, target, W):
    norm_g = W["norm_g"]
    xs, saves = [x], []
    for i in range(4):
        h = _rmsnorm_fwd(xs[i], norm_g[i:i + 1], f"norm{i}")
        if i in (0, 3):
            j = i // 3
            y, sv = _dn_layer_fwd(h, W["dn_w_in"][j], W["dn_conv_w"][j], W["dn_a_log"][j:j + 1], W["dn_dt_bias"][j:j + 1],
                                  W["dn_o_norm_g"][j:j + 1], W["dn_w_out"][j], xs[i], f"dn{j}")
        elif i == 1:
            y, sv = _sb_layer_fwd(h, W["sb_w_in"], W["sb_q_norm_g"], W["sb_k_norm_g"], W["sb_w_out"], xs[i], "sb")
        else:
            y, sv = _sc_layer_fwd(h, W["sc_w_in"], W["sc_conv_w"], W["sc_w_out"], xs[i], "sc")
        xs.append(y)
        saves.append(sv)
    dx, loss = _loss_head(xs[4], target)
    G = {}
    dnorm = [None] * 4
    dn_parts = [None, None]
    for i in (3, 2, 1, 0):
        if i in (0, 3):
            dh, dn_parts[i // 3] = _dn_layer_bwd(dx, saves[i], f"dn{i // 3}")
        elif i == 1:
            dh, (G["sb_w_in"], G["sb_q_norm_g"], G["sb_k_norm_g"], G["sb_w_out"]) = _sb_layer_bwd(dx, saves[i], "sb")
        else:
            dh, (G["sc_w_in"], G["sc_conv_w"], G["sc_w_out"]) = _sc_layer_bwd(dx, saves[i], "sc")
        dx, dnorm[i] = _rmsnorm_bwd(dh, xs[i], norm_g[i:i + 1], dx, f"norm{i}_bwd")
    G["norm_g"] = jnp.concatenate(dnorm, axis=0)
    for k, name in enumerate(("dn_w_in", "dn_conv_w", "dn_a_log", "dn_dt_bias", "dn_o_norm_g", "dn_w_out")):
        G[name] = jnp.stack([dn_parts[0][k], dn_parts[1][k]], axis=0)
    return loss, dx, G


def _adamw(w, m, v, parts, name):
    R, C = w.shape
    tr = _tile(R, 128, SUBLANE)

    def body(w_ref, m_ref, v_ref, p_ref, g_ref, d_ref, nm_ref, nv_ref):
        g = p_ref[0].astype(F32)
        for s in range(1, N_DEV):
            g = g + p_ref[s].astype(F32)
        m2 = ADAM_B1 * m_ref[...] + (1.0 - ADAM_B1) * g
        v2 = ADAM_B2 * v_ref[...] + (1.0 - ADAM_B2) * (g * g)
        m_hat = m2 / (1.0 - ADAM_B1 ** ADAM_STEP)
        v_hat = v2 / (1.0 - ADAM_B2 ** ADAM_STEP)
        g_ref[...] = g
        d_ref[...] = -ADAM_LR * (m_hat / (jnp.sqrt(v_hat) + ADAM_EPS) + ADAM_WD * w_ref[...])
        nm_ref[...] = m2
        nv_ref[...] = v2

    blk = pl.BlockSpec((tr, C), lambda i: (i, 0))
    return pl.pallas_call(
        body, name=name, grid=(R // tr,),
        in_specs=[blk, blk, blk, pl.BlockSpec((N_DEV, tr, C), lambda i: (0, i, 0))],
        out_specs=[blk] * 4, out_shape=[jax.ShapeDtypeStruct((R, C), F32)] * 4,
        compiler_params=_params("parallel"),
    )(w, m, v, parts)


_HBM = pl.BlockSpec(memory_space=pltpu.HBM)
_MESH = pl.DeviceIdType.MESH


def _slot(x, y, c):
    return 4 * x + 2 * y + c


def _all_gather(shards, name):
    n = len(shards)

    def body(*refs):
        ins, outs = refs[:n], refs[n:2 * n]
        send_sems, recv_sems, local_sems = refs[2 * n:]
        x, y, c = lax.axis_index("x"), lax.axis_index("y"), lax.axis_index("c")
        me, sibling = (x, y, c), (x, y, 1 - c)
        chips = [(1 - x, y), (x, 1 - y), (1 - x, 1 - y)]

        def copy(a, k, block, to, src=None):
            dst = outs[a].at[_slot(*block)]
            return pltpu.make_async_remote_copy(src_ref=dst if src is None else src, dst_ref=dst,
                                                send_sem=send_sems.at[a, k], recv_sem=recv_sems.at[a, k],
                                                device_id=to, device_id_type=_MESH)

        mine = [pltpu.make_async_copy(ins[a], outs[a].at[_slot(*me)], local_sems.at[a]) for a in range(n)]
        for cp in mine:
            cp.start()
        first = []
        for a in range(n):
            first.append(copy(a, 0, me, sibling, src=ins[a]))
            first += [copy(a, 1 + j, me, (*chip, c), src=ins[a]) for j, chip in enumerate(chips)]
        for cp in first:
            cp.start()
        passed = []
        for j, chip in enumerate(chips):
            for a in range(n):
                copy(a, 1 + j, (*chip, c), me).wait_recv()
                fwd = copy(a, 4 + j, (*chip, c), sibling)
                fwd.start()
                passed.append(fwd)
        for a in range(n):
            copy(a, 0, sibling, me).wait_recv()
            for j, chip in enumerate(chips):
                copy(a, 4 + j, (*chip, 1 - c), me).wait_recv()
        for cp in first + passed:
            cp.wait_send()
        for cp in mine:
            cp.wait()

    return pl.pallas_call(
        body, name=name,
        in_specs=[_HBM] * n, out_specs=[_HBM] * n,
        out_shape=[jax.ShapeDtypeStruct((N_DEV,) + s.shape, s.dtype) for s in shards],
        scratch_shapes=[pltpu.SemaphoreType.DMA((n, N_DEV - 1)), pltpu.SemaphoreType.DMA((n, N_DEV - 1)),
                        pltpu.SemaphoreType.DMA((n,))],
    )(*shards)


def _exchange(arrays, scatter, name):
    n = len(arrays)

    def body(*refs):
        ins, outs = refs[:n], refs[n:2 * n]
        send_sems, recv_sems, local_sems = refs[2 * n:]
        x, y, c = lax.axis_index("x"), lax.axis_index("y"), lax.axis_index("c")
        me = _slot(x, y, c)
        copies = []
        for a in range(n):
            cp = pltpu.make_async_copy(ins[a].at[me] if scatter[a] else ins[a], outs[a].at[me], local_sems.at[a])
            cp.start()
            copies.append(cp)
        for r in range(1, N_DEV):
            px = 1 - x if r & 4 else x
            py = 1 - y if r & 2 else y
            pc = 1 - c if r & 1 else c
            for a in range(n):
                cp = pltpu.make_async_remote_copy(
                    src_ref=ins[a].at[_slot(px, py, pc)] if scatter[a] else ins[a], dst_ref=outs[a].at[me],
                    send_sem=send_sems.at[a, r - 1], recv_sem=recv_sems.at[a, r - 1],
                    device_id=(px, py, pc), device_id_type=_MESH)
                cp.start()
                copies.append(cp)
        for cp in copies:
            cp.wait()

    shapes = [a.shape[1:] if s else a.shape for a, s in zip(arrays, scatter)]
    return pl.pallas_call(
        body, name=name,
        in_specs=[_HBM] * n, out_specs=[_HBM] * n,
        out_shape=[jax.ShapeDtypeStruct((N_DEV,) + tuple(s), a.dtype) for s, a in zip(shapes, arrays)],
        scratch_shapes=[pltpu.SemaphoreType.DMA((n, N_DEV - 1)), pltpu.SemaphoreType.DMA((n, N_DEV - 1)),
                        pltpu.SemaphoreType.DMA((n,))],
    )(*arrays)


_SHARDED = ("dn_w_in", "dn_conv_w", "dn_o_norm_g", "dn_w_out", "sb_w_in", "sb_w_out", "sc_w_in", "sc_conv_w", "sc_w_out")
_MATMUL_WEIGHTS = ("dn_w_in", "dn_w_out", "sb_w_in", "sb_w_out", "sc_w_in", "sc_w_out")
_COLUMN_SHARDED = ("dn_w_in", "dn_conv_w", "dn_o_norm_g", "sb_w_in", "sc_w_in", "sc_conv_w")
_REPLICATED = ("norm_g", "dn_a_log", "dn_dt_bias", "sb_q_norm_g", "sb_k_norm_g")
_ORDER = ("norm_g", "dn_w_in", "dn_conv_w", "dn_a_log", "dn_dt_bias", "dn_o_norm_g", "dn_w_out", "sb_w_in", "sb_q_norm_g",
          "sb_k_norm_g", "sb_w_out", "sc_w_in", "sc_conv_w", "sc_w_out")
_PACK_COLS = D_MODEL


def _assemble(name, gathered):
    if name in _COLUMN_SHARDED:
        g = jnp.moveaxis(gathered, 0, -2)
        return g.reshape(g.shape[:-2] + (g.shape[-2] * g.shape[-1],))
    g = jnp.moveaxis(gathered, 0, 1)
    return g.reshape((g.shape[0], g.shape[1] * g.shape[2]) + g.shape[3:])


def _disassemble(name, full):
    if name in _COLUMN_SHARDED:
        g = full.reshape(full.shape[:-1] + (N_DEV, full.shape[-1] // N_DEV))
        return jnp.moveaxis(g, -2, 0)
    g = full.reshape((full.shape[0], N_DEV, full.shape[1] // N_DEV) + full.shape[2:])
    return jnp.moveaxis(g, 1, 0)


def _pack_replicated(d):
    rows = [d["norm_g"]]
    for name in _REPLICATED[1:]:
        flat = d[name].reshape(1, -1)
        rows.append(jnp.pad(flat, ((0, 0), (0, _PACK_COLS - flat.shape[1]))))
    return jnp.concatenate(rows, axis=0)


def _unpack_replicated(p, like):
    out = {"norm_g": p[:4]}
    for r, name in enumerate(_REPLICATED[1:]):
        shape = like[name].shape
        out[name] = p[4 + r, :math.prod(shape)].reshape(shape)
    return out


def kernel(x, norm_g, dn_w_in, dn_conv_w, dn_a_log, dn_dt_bias, dn_o_norm_g, dn_w_out, sb_w_in, sb_q_norm_g, sb_k_norm_g, sb_w_out, sc_w_in, sc_conv_w, sc_w_out, loss_target, m_norm_g, m_dn_w_in, m_dn_conv_w, m_dn_a_log, m_dn_dt_bias, m_dn_o_norm_g, m_dn_w_out, m_sb_w_in, m_sb_q_norm_g, m_sb_k_norm_g, m_sb_w_out, m_sc_w_in, m_sc_conv_w, m_sc_w_out, v_norm_g, v_dn_w_in, v_dn_conv_w, v_dn_a_log, v_dn_dt_bias, v_dn_o_norm_g, v_dn_w_out, v_sb_w_in, v_sb_q_norm_g, v_sb_k_norm_g, v_sb_w_out, v_sc_w_in, v_sc_conv_w, v_sc_w_out):
    w = dict(norm_g=norm_g, dn_w_in=dn_w_in, dn_conv_w=dn_conv_w, dn_a_log=dn_a_log, dn_dt_bias=dn_dt_bias,
             dn_o_norm_g=dn_o_norm_g, dn_w_out=dn_w_out, sb_w_in=sb_w_in, sb_q_norm_g=sb_q_norm_g, sb_k_norm_g=sb_k_norm_g,
             sb_w_out=sb_w_out, sc_w_in=sc_w_in, sc_conv_w=sc_conv_w, sc_w_out=sc_w_out)
    m = dict(norm_g=m_norm_g, dn_w_in=m_dn_w_in, dn_conv_w=m_dn_conv_w, dn_a_log=m_dn_a_log, dn_dt_bias=m_dn_dt_bias,
             dn_o_norm_g=m_dn_o_norm_g, dn_w_out=m_dn_w_out, sb_w_in=m_sb_w_in, sb_q_norm_g=m_sb_q_norm_g,
             sb_k_norm_g=m_sb_k_norm_g, sb_w_out=m_sb_w_out, sc_w_in=m_sc_w_in, sc_conv_w=m_sc_conv_w, sc_w_out=m_sc_w_out)
    v = dict(norm_g=v_norm_g, dn_w_in=v_dn_w_in, dn_conv_w=v_dn_conv_w, dn_a_log=v_dn_a_log, dn_dt_bias=v_dn_dt_bias,
             dn_o_norm_g=v_dn_o_norm_g, dn_w_out=v_dn_w_out, sb_w_in=v_sb_w_in, sb_q_norm_g=v_sb_q_norm_g,
             sb_k_norm_g=v_sb_k_norm_g, sb_w_out=v_sb_w_out, sc_w_in=v_sc_w_in, sc_conv_w=v_sc_conv_w, sc_w_out=v_sc_w_out)

    shards = [w[k].astype(BF16) if k in _MATMUL_WEIGHTS else w[k] for k in _SHARDED]
    gathered = _all_gather(shards, "gather_weights")
    W = {k: _assemble(k, g) for k, g in zip(_SHARDED, gathered)}
    for k in _REPLICATED:
        W[k] = w[k]
    W["dn_w_in"] = [_dn_split_w_in(W["dn_w_in"][j]) for j in range(2)]
    W["sb_w_in"] = _sb_perm(W["sb_w_in"][0])
    W["sb_w_out"] = W["sb_w_out"][0]
    W["sc_w_in"] = _sc_perm(W["sc_w_in"][0])
    W["sc_conv_w"] = W["sc_conv_w"][0]
    W["sc_w_out"] = W["sc_w_out"][0]

    loss_part, dx, G = _local_step(x[0], loss_target[0], W)
    for k in ("sb_w_in", "sb_w_out", "sc_w_in", "sc_conv_w", "sc_w_out"):
        G[k] = G[k][None]
    G["dn_a_log"] = G["dn_a_log"].reshape(2, DN_HEADS)
    G["dn_dt_bias"] = G["dn_dt_bias"].reshape(2, DN_HEADS)
    G["dn_o_norm_g"] = G["dn_o_norm_g"].reshape(2, DN_DV)

    outgoing = [_disassemble(k, G[k].astype(BF16) if k in _MATMUL_WEIGHTS else G[k]) for k in _SHARDED] + [_pack_replicated(G)]
    landed = _exchange(outgoing, [True] * len(_SHARDED) + [False], "exchange_grads")

    res = {}
    for k, parts in zip(_SHARDED, landed[:-1]):
        shape = w[k].shape
        rows = math.prod(shape[:-1])
        flat = lambda a: a.reshape(rows, shape[-1])
        outs = _adamw(flat(w[k]), flat(m[k]), flat(v[k]), parts.reshape(N_DEV, rows, shape[-1]), "adamw_" + k)
        res[k] = [o.reshape(shape) for o in outs]
    outs = _adamw(_pack_replicated(w), _pack_replicated(m), _pack_replicated(v), landed[-1], "adamw_replicated")
    unpacked = [_unpack_replicated(o, w) for o in outs]
    for k in _REPLICATED:
        res[k] = [u[k] for u in unpacked]

    loss = lax.psum(loss_part[0, 0], ("x", "y", "c"))
    return (loss, dx[None]) + tuple(res[k][0] for k in _ORDER) + tuple(res[k][1] for k in _ORDER) \
        + tuple(res[k][2] for k in _ORDER) + tuple(res[k][3] for k in _ORDER)
```

```python
import itertools
import math

import jax
import jax.numpy as jnp
from jax import lax
from jax.experimental import pallas as pl
from jax.experimental.pallas import tpu as pltpu

F32 = jnp.float32
BF16 = jnp.bfloat16
HIGHEST = lax.Precision.HIGHEST

N_DEV = 8
D_MODEL = 1024
RMS_EPS = 1e-6
L2_EPS = 1e-6

DN_HEADS = 8
DN_DK = 128
DN_DV = 256
DN_QK_W = DN_HEADS * DN_DK
DN_V_W = DN_HEADS * DN_DV
DN_CONV = 4
DN_CHUNK = 64
DN_CONV_W = 2 * DN_QK_W + DN_V_W
DN_IN = DN_CONV_W + DN_V_W + 2 * DN_HEADS
DN_AB_PAD = 128
DN_PREP_BLK = 512

SB_HEADS = 16
SB_DH = 64
SB_W = SB_HEADS * SB_DH
SB_PAIRS = SB_HEADS // 2
SB_TQ = 256
SB_TK = 128
SB_DEAD = -106.0

SC_W = 2 * D_MODEL
SC_CONV = 3
SC_BLK = 512
SC_NBLK = SC_W // SC_BLK

ADAM_LR = 0.001
ADAM_B1 = 0.9
ADAM_B2 = 0.999
ADAM_EPS = 1e-08
ADAM_WD = 0.01
ADAM_STEP = 10

LANE = 128
SUBLANE = 8
HALO = SUBLANE
ROW_TILE = 256
VMEM_LIMIT = 48 * 2 ** 20

NN = ((1,), (0,))
NT = ((1,), (1,))
TN = ((0,), (0,))


def _dot(a, b, dims=NN, precision=None):
    return lax.dot_general(a, b, (dims, ((), ())), precision=precision, preferred_element_type=F32)


def _bdot(a, b, dims=NN):
    return _dot(a.astype(BF16), b.astype(BF16), dims)


def _hdot(a, b, dims=NN):
    return _dot(a, b, dims, precision=HIGHEST)


def _tile(dim, pref, align=LANE):
    t = (min(pref, dim) // align) * align
    while t >= align:
        if dim % t == 0:
            return t
        t -= align
    return dim


def _params(*sem):
    return pltpu.CompilerParams(dimension_semantics=sem, vmem_limit_bytes=VMEM_LIMIT)


def _sigmoid(x):
    return 1.0 / (1.0 + jnp.exp(-x))


def _softplus(x):
    return jnp.maximum(x, 0.0) + jnp.log(1.0 + jnp.exp(-jnp.abs(x)))


def _silu_and_grad(x):
    s = _sigmoid(x)
    return x * s, s * (1.0 + x * (1.0 - s))


def _iota2(shape, dim):
    return lax.broadcasted_iota(jnp.int32, shape, dim)


def _matmul(a, b, mode, name, out_dtype=F32, add=None, tm=512, tn=1024, tk=1024):
    if mode == "nn":
        (M, K), (K2, N) = a.shape, b.shape
    elif mode == "nt":
        (M, K), (N, K2) = a.shape, b.shape
    else:
        (K, M), (K2, N) = a.shape, b.shape
    assert K == K2, (a.shape, b.shape, mode)
    tm, tn, tk = _tile(M, tm), _tile(N, tn), _tile(K, tk)
    nk = K // tk
    dims = {"nn": NN, "nt": NT, "tn": TN}[mode]
    a_spec = pl.BlockSpec((tk, tm), lambda i, j, k: (k, i)) if mode == "tn" else pl.BlockSpec((tm, tk), lambda i, j, k: (i, k))
    b_spec = pl.BlockSpec((tn, tk), lambda i, j, k: (j, k)) if mode == "nt" else pl.BlockSpec((tk, tn), lambda i, j, k: (k, j))
    o_spec = pl.BlockSpec((tm, tn), lambda i, j, k: (i, j))
    has_add = add is not None

    def body(*refs):
        a_ref, b_ref = refs[0], refs[1]
        add_ref = refs[2] if has_add else None
        o_ref = refs[3] if has_add else refs[2]
        p = _bdot(a_ref[...], b_ref[...], dims)

        def finish(acc):
            if has_add:
                acc = acc + add_ref[...]
            o_ref[...] = acc.astype(out_dtype)

        if nk == 1:
            finish(p)
        else:
            acc_ref = refs[-1]
            k = pl.program_id(2)

            @pl.when(k == 0)
            def _():
                acc_ref[...] = p

            @pl.when(k > 0)
            def _():
                acc_ref[...] += p

            @pl.when(k == nk - 1)
            def _():
                finish(acc_ref[...])

    in_specs = [a_spec, b_spec] + ([o_spec] if has_add else [])
    args = (a, b) + ((add,) if has_add else ())
    return pl.pallas_call(
        body, name=name, grid=(M // tm, N // tn, nk),
        in_specs=in_specs, out_specs=o_spec,
        out_shape=jax.ShapeDtypeStruct((M, N), out_dtype),
        scratch_shapes=[pltpu.VMEM((tm, tn), F32)] if nk > 1 else [],
        compiler_params=_params("parallel", "parallel", "arbitrary"),
    )(*args)


def _rmsnorm_fwd(x, g, name):
    T, D = x.shape
    tt = _tile(T, 512, SUBLANE)

    def body(x_ref, g_ref, o_ref):
        xv = x_ref[...]
        r = lax.rsqrt(jnp.mean(xv * xv, axis=-1, keepdims=True) + RMS_EPS)
        o_ref[...] = (xv * r * g_ref[...]).astype(BF16)

    return pl.pallas_call(
        body, name=name, grid=(T // tt,),
        in_specs=[pl.BlockSpec((tt, D), lambda i: (i, 0)), pl.BlockSpec((1, D), lambda i: (0, 0))],
        out_specs=pl.BlockSpec((tt, D), lambda i: (i, 0)),
        out_shape=jax.ShapeDtypeStruct((T, D), BF16),
        compiler_params=_params("parallel"),
    )(x, g)


def _rmsnorm_bwd(dh, x, g, dx_res, name):
    T, D = x.shape
    tt = _tile(T, 256, SUBLANE)

    def body(dh_ref, x_ref, g_ref, res_ref, dx_ref, dg_ref):
        xv, dhv = x_ref[...], dh_ref[...]
        r = lax.rsqrt(jnp.mean(xv * xv, axis=-1, keepdims=True) + RMS_EPS)
        xh = xv * r
        dxh = dhv * g_ref[...]
        m = jnp.mean(dxh * xh, axis=-1, keepdims=True)
        dx_ref[...] = res_ref[...] + r * (dxh - xh * m)
        part = jnp.sum(dhv * xh, axis=0, keepdims=True)

        @pl.when(pl.program_id(0) == 0)
        def _():
            dg_ref[...] = part

        @pl.when(pl.program_id(0) > 0)
        def _():
            dg_ref[...] += part

    row = pl.BlockSpec((tt, D), lambda i: (i, 0))
    vec = pl.BlockSpec((1, D), lambda i: (0, 0))
    return pl.pallas_call(
        body, name=name, grid=(T // tt,),
        in_specs=[row, row, vec, row], out_specs=[row, vec],
        out_shape=[jax.ShapeDtypeStruct((T, D), F32), jax.ShapeDtypeStruct((1, D), F32)],
        compiler_params=_params("arbitrary"),
    )(dh, x, g, dx_res)


def _loss_head(y, target, name="loss_head"):
    T, D = y.shape
    tt = _tile(T, 512, SUBLANE)

    def body(y_ref, t_ref, dy_ref, l_ref):
        e = y_ref[...] - t_ref[...]
        dy_ref[...] = e * (1.0 / D)
        s = jnp.sum(jnp.sum(e * e, axis=1, keepdims=True), axis=0, keepdims=True) * (0.5 / D)
        s = jnp.broadcast_to(s, (1, LANE))

        @pl.when(pl.program_id(0) == 0)
        def _():
            l_ref[...] = s

        @pl.when(pl.program_id(0) > 0)
        def _():
            l_ref[...] += s

    row = pl.BlockSpec((tt, D), lambda i: (i, 0))
    return pl.pallas_call(
        body, name=name, grid=(T // tt,),
        in_specs=[row, row], out_specs=[row, pl.BlockSpec((1, LANE), lambda i: (0, 0))],
        out_shape=[jax.ShapeDtypeStruct((T, D), F32), jax.ShapeDtypeStruct((1, LANE), F32)],
        compiler_params=_params("arbitrary"),
    )(y, target)


def _down(x, k):
    return pltpu.roll(x, k, 0) if k else x


def _up(x, k):
    return pltpu.roll(x, x.shape[0] - k, 0) if k else x


def _sc_fwd(proj, conv_w, name):
    T = proj.shape[0]
    tt = _tile(T, ROW_TILE, SUBLANE)
    B = SC_BLK

    def body(p_ref, ph_ref, w_ref, o_ref):
        i = pl.program_id(0)
        keep = (i > 0).astype(F32)
        c = jnp.concatenate([ph_ref[:, B:2 * B] * keep, p_ref[:, B:2 * B]], axis=0)
        u = jnp.concatenate([ph_ref[:, 2 * B:3 * B] * keep, p_ref[:, 2 * B:3 * B]], axis=0)
        z = c * u
        cz = (w_ref[2:3, :] * z + w_ref[1:2, :] * _down(z, 1) + w_ref[0:1, :] * _down(z, 2))[HALO:]
        gate = p_ref[:, 3 * B:4 * B]
        o_ref[...] = (p_ref[:, 0:B] * cz * (gate * _sigmoid(gate))).astype(BF16)

    return pl.pallas_call(
        body, name=name, grid=(T // tt, SC_NBLK),
        in_specs=[pl.BlockSpec((tt, 4 * B), lambda i, j: (i, j)),
                  pl.BlockSpec((HALO, 4 * B), lambda i, j: (jnp.maximum(i * (tt // HALO) - 1, 0), j)),
                  pl.BlockSpec((SC_CONV, B), lambda i, j: (0, j))],
        out_specs=pl.BlockSpec((tt, B), lambda i, j: (i, j)),
        out_shape=jax.ShapeDtypeStruct((T, SC_W), BF16),
        compiler_params=_params("parallel", "parallel"),
    )(proj, proj, conv_w)


def _sc_bwd(dyg, proj, conv_w, name):
    T = proj.shape[0]
    tt = _tile(T, ROW_TILE, SUBLANE)
    nt = T // tt
    B = SC_BLK
    hb = tt // HALO

    def body(d_ref, dn_ref, p_ref, pp_ref, pn_ref, w_ref, o_ref, dw_ref):
        i = pl.program_id(1)
        keep_p = (i > 0).astype(F32)
        keep_n = (i < nt - 1).astype(F32)

        def ext(k):
            s = slice(k * B, (k + 1) * B)
            return jnp.concatenate([pp_ref[:, s] * keep_p, p_ref[:, s], pn_ref[:, s]], axis=0)

        b, c, u, gate = ext(0), ext(1), ext(2), ext(3)
        dyg_e = jnp.concatenate([jnp.zeros((HALO, B), F32), d_ref[...], dn_ref[...] * keep_n], axis=0)
        w0, w1, w2 = w_ref[0:1, :], w_ref[1:2, :], w_ref[2:3, :]
        z = c * u
        z1, z2 = _down(z, 1), _down(z, 2)
        cz = w2 * z + w1 * z1 + w0 * z2
        sg, dsg = _silu_and_grad(gate)
        dy = dyg_e * sg
        dgate = dyg_e * (b * cz) * dsg
        db = dy * cz
        dcz = dy * b
        dz = w2 * dcz + w1 * _up(dcz, 1) + w0 * _up(dcz, 2)
        main = slice(HALO, HALO + tt)
        o_ref[:, 0:B] = db[main]
        o_ref[:, B:2 * B] = (dz * u)[main]
        o_ref[:, 2 * B:3 * B] = (dz * c)[main]
        o_ref[:, 3 * B:4 * B] = dgate[main]
        dcm = dcz[main]
        part = jnp.concatenate([jnp.sum(dcm * z2[main], axis=0, keepdims=True),
                                jnp.sum(dcm * z1[main], axis=0, keepdims=True),
                                jnp.sum(dcm * z[main], axis=0, keepdims=True)], axis=0)

        @pl.when(i == 0)
        def _():
            dw_ref[...] = part

        @pl.when(i > 0)
        def _():
            dw_ref[...] += part

    return pl.pallas_call(
        body, name=name, grid=(SC_NBLK, nt),
        in_specs=[pl.BlockSpec((tt, B), lambda j, i: (i, j)),
                  pl.BlockSpec((HALO, B), lambda j, i: (jnp.minimum((i + 1) * hb, nt * hb - 1), j)),
                  pl.BlockSpec((tt, 4 * B), lambda j, i: (i, j)),
                  pl.BlockSpec((HALO, 4 * B), lambda j, i: (jnp.maximum(i * hb - 1, 0), j)),
                  pl.BlockSpec((HALO, 4 * B), lambda j, i: (jnp.minimum((i + 1) * hb, nt * hb - 1), j)),
                  pl.BlockSpec((SC_CONV, B), lambda j, i: (0, j))],
        out_specs=[pl.BlockSpec((tt, 4 * B), lambda j, i: (i, j)), pl.BlockSpec((SC_CONV, B), lambda j, i: (0, j))],
        out_shape=[jax.ShapeDtypeStruct((T, 4 * SC_W), F32), jax.ShapeDtypeStruct((SC_CONV, SC_W), F32)],
        compiler_params=_params("parallel", "arbitrary"),
    )(dyg, dyg, proj, proj, proj, conv_w)


def _sc_perm(w_in):
    d = w_in.shape[0]
    return w_in.reshape(d, 4, SC_NBLK, SC_BLK).transpose(0, 2, 1, 3).reshape(d, 4 * SC_W)


def _sc_unperm(w):
    d = w.shape[0]
    return w.reshape(d, SC_NBLK, 4, SC_BLK).transpose(0, 2, 1, 3).reshape(d, 4 * SC_W)


def _sb_perm(w_in):
    d = w_in.shape[0]
    return w_in.reshape(d, 4, SB_PAIRS, LANE).transpose(0, 2, 1, 3).reshape(d, 4 * SB_W)


def _sb_unperm(w):
    d = w.shape[0]
    return w.reshape(d, SB_PAIRS, 4, LANE).transpose(0, 2, 1, 3).reshape(d, 4 * SB_W)


def _split3_dot(x, m):
    hi = x.astype(BF16)
    r1 = x - hi.astype(F32)
    mid = r1.astype(BF16)
    lo = (r1 - mid.astype(F32)).astype(BF16)
    return _dot(hi, m) + _dot(mid, m) + _dot(lo, m)


def _split2_dot(x, m):
    hi = x.astype(BF16)
    lo = (x - hi.astype(F32)).astype(BF16)
    return _dot(hi, m) + _dot(lo, m)


def _head_mean_matrix():
    r, c = _iota2((LANE, LANE), 0), _iota2((LANE, LANE), 1)
    return jnp.where((r // SB_DH) == (c // SB_DH), 1.0 / SB_DH, 0.0).astype(BF16)


def _sb_prep(proj, qg2, kg2, name):
    T = proj.shape[0]
    tt = _tile(T, ROW_TILE, SUBLANE)

    def body(p_ref, qg_ref, kg_ref, q_ref, k_ref, v_ref):
        bd = _head_mean_matrix()

        def norm(x, g, scale):
            r = lax.rsqrt(_split3_dot(x * x, bd) + RMS_EPS)
            return (x * r * g * scale).astype(BF16)

        q_ref[...] = norm(p_ref[:, 0:LANE], qg_ref[...], SB_DH ** -0.5)
        k_ref[...] = norm(p_ref[:, LANE:2 * LANE], kg_ref[...], 1.0)
        v_ref[...] = p_ref[:, 2 * LANE:3 * LANE].astype(BF16)

    blk = pl.BlockSpec((tt, LANE), lambda i, p: (i, p))
    vec = pl.BlockSpec((1, LANE), lambda i, p: (0, 0))
    return pl.pallas_call(
        body, name=name, grid=(T // tt, SB_PAIRS),
        in_specs=[pl.BlockSpec((tt, 4 * LANE), lambda i, p: (i, p)), vec, vec],
        out_specs=[blk, blk, blk],
        out_shape=[jax.ShapeDtypeStruct((T, SB_W), BF16)] * 3,
        compiler_params=_params("parallel", "parallel"),
    )(proj, qg2, kg2)


def _sb_prep_bwd(proj, dqn, dkn, dv, dgate, qg2, kg2, name):
    T = proj.shape[0]
    tt = _tile(T, ROW_TILE, SUBLANE)

    def body(p_ref, dq_ref, dk_ref, dv_ref, dg_ref, qg_ref, kg_ref, o_ref, dqg_ref, dkg_ref):
        i = pl.program_id(1)
        bd = _head_mean_matrix()

        def norm_bwd(x, g, dy):
            r = lax.rsqrt(_split3_dot(x * x, bd) + RMS_EPS)
            xh = x * r
            dxh = dy * g
            m = _split3_dot(dxh * xh, bd)
            return r * (dxh - xh * m), jnp.sum(dy * xh, axis=0, keepdims=True)

        dxq, pq = norm_bwd(p_ref[:, 0:LANE], qg_ref[...], dq_ref[...])
        dxk, pk = norm_bwd(p_ref[:, LANE:2 * LANE], kg_ref[...], dk_ref[...])
        o_ref[:, 0:LANE] = dxq
        o_ref[:, LANE:2 * LANE] = dxk
        o_ref[:, 2 * LANE:3 * LANE] = dv_ref[...]
        o_ref[:, 3 * LANE:4 * LANE] = dg_ref[...]

        @pl.when(i == 0)
        def _():
            dqg_ref[...] = pq
            dkg_ref[...] = pk

        @pl.when(i > 0)
        def _():
            dqg_ref[...] += pq
            dkg_ref[...] += pk

    blk = pl.BlockSpec((tt, LANE), lambda p, i: (i, p))
    vec = pl.BlockSpec((1, LANE), lambda p, i: (0, 0))
    acc = pl.BlockSpec((None, 1, LANE), lambda p, i: (p, 0, 0))
    wide = pl.BlockSpec((tt, 4 * LANE), lambda p, i: (i, p))
    return pl.pallas_call(
        body, name=name, grid=(SB_PAIRS, T // tt),
        in_specs=[wide, blk, blk, blk, blk, vec, vec],
        out_specs=[wide, acc, acc],
        out_shape=[jax.ShapeDtypeStruct((T, 4 * SB_W), F32)] + [jax.ShapeDtypeStruct((SB_PAIRS, 1, LANE), F32)] * 2,
        compiler_params=_params("parallel", "arbitrary"),
    )(proj, dqn, dkn, dv, dgate, qg2, kg2)


def _fold_heads(part, name):
    def body(p_ref, o_ref):
        r, c = _iota2((LANE, SB_DH), 0), _iota2((LANE, SB_DH), 1)
        fold = jnp.where((r % SB_DH) == c, 1.0, 0.0).astype(F32)
        o_ref[...] = jnp.sum(_hdot(p_ref[...], fold), axis=0, keepdims=True)

    return pl.pallas_call(body, name=name, out_shape=jax.ShapeDtypeStruct((1, SB_DH), F32))(part)


def _sb_masks():
    lane = _iota2((1, LANE), 1)
    return lane < SB_DH


def _sb_attn_fwd(qn, kn, vb, proj, name):
    T = qn.shape[0]
    tq, tk = _tile(T, SB_TQ, SUBLANE), SB_TK
    assert tq % tk == 0

    def body(q_ref, k_ref, v_ref, g_ref, o_ref, og_ref, lt_ref, done_ref):
        i = pl.program_id(1)
        ma = _sb_masks()
        q2 = q_ref[...]
        zero = jnp.zeros_like(q2)
        qs = (jnp.where(ma, q2, zero), jnp.where(ma, zero, q2))
        upper = (_iota2((tk, tk), 0) > _iota2((tk, tk), 1)).astype(BF16)
        qpos = i * tq + _iota2((tq, tk), 0)
        nb = tq // tk

        def trip(kb_top, masked, carry):
            acc, la, lb = carry
            chains = [(b, h) for b in range(nb) for h in range(2)]
            k2s, vss, masks = [], [], []
            for b in range(nb):
                kb = kb_top - b
                rows = pl.ds(pl.multiple_of(kb * tk, tk), tk)
                k2s.append(k_ref[rows, :])
                v2 = v_ref[rows, :]
                zv = jnp.zeros_like(v2)
                vss.append((jnp.where(ma, v2, zv), jnp.where(ma, zv, v2)))
                masks.append((kb * tk + _iota2((tq, tk), 1)) < qpos if masked else None)
            zs = [_dot(qs[h], k2s[b], NT) for b, h in chains]
            ts = [jnp.log(1.0 + jnp.exp(-jnp.abs(z))) for z in zs]
            ls = [-(jnp.maximum(z, 0.0) + t) for z, t in zip(zs, ts)]
            if masked:
                ls = [jnp.where(masks[b], l, 0.0) for (b, h), l in zip(chains, ls)]
            cums = [_split2_dot(l, upper) for l in ls]
            sums = [jnp.sum(l, axis=1, keepdims=True) for l in ls]
            offs, tot = {}, [la, lb]
            for b in range(nb):
                for h in range(2):
                    offs[(b, h)] = tot[h]
                    tot[h] = tot[h] + sums[chains.index((b, h))]
            ws = [jnp.exp(jnp.minimum(z, 0.0) - t + c + offs[ch]) for ch, z, t, c in zip(chains, zs, ts, cums)]
            if masked:
                ws = [jnp.where(masks[b], w, 0.0) for (b, h), w in zip(chains, ws)]
            for (b, h), w in zip(chains, ws):
                acc = acc + _dot(w.astype(BF16), vss[b][h])
            return acc, tot[0], tot[1]

        def largest(la, lb):
            return jnp.max(jnp.maximum(la, lb))

        z1 = jnp.zeros((tq, 1), F32)
        acc, la, lb = trip((i + 1) * nb - 1, True, (jnp.zeros((tq, LANE), F32), z1, z1))

        def live(c):
            return (c[0] < i) & (c[4] > SB_DEAD)

        def more(c):
            j, acc, la, lb, _ = c
            acc, la, lb = trip((i - j) * nb - 1, False, (acc, la, lb))
            return j + 1, acc, la, lb, largest(la, lb)

        done, acc, la, lb, _ = lax.while_loop(live, more, (jnp.int32(0), acc, la, lb, largest(la, lb)))
        gate = g_ref[...]
        o_ref[...] = acc
        og_ref[...] = (acc * (gate * _sigmoid(gate))).astype(BF16)
        lt_ref[...] = jnp.where(_iota2((tq, 2), 1) == 0, la, lb)
        done_ref[...] = jnp.full((SUBLANE, LANE), done, F32)

    nq = T // tq
    qblk = pl.BlockSpec((tq, LANE), lambda p, i: (i, p))
    full = pl.BlockSpec((T, LANE), lambda p, i: (0, p))
    return pl.pallas_call(
        body, name=name, grid=(SB_PAIRS, nq),
        in_specs=[qblk, full, full, pl.BlockSpec((tq, LANE), lambda p, i: (i, 4 * p + 3))],
        out_specs=[qblk, qblk, pl.BlockSpec((None, tq, 2), lambda p, i: (p, i, 0)),
                   pl.BlockSpec((None, None, SUBLANE, LANE), lambda p, i: (p, i, 0, 0))],
        out_shape=[jax.ShapeDtypeStruct((T, SB_W), F32), jax.ShapeDtypeStruct((T, SB_W), BF16),
                   jax.ShapeDtypeStruct((SB_PAIRS, T, 2), F32), jax.ShapeDtypeStruct((SB_PAIRS, nq, SUBLANE, LANE), F32)],
        compiler_params=_params("parallel", "parallel"),
    )(qn, kn, vb, proj)


def _sb_attn_bwd(qn, kn, vb, dog, o, ltot, done, proj, name):
    T = qn.shape[0]
    tq, tk = _tile(T, SB_TQ, SUBLANE), SB_TK

    def body(q_ref, k_ref, v_ref, dog_ref, o_ref, lt_ref, done_ref, g_ref, dq_ref, dk_ref, dv_ref, dgate_ref):
        i = pl.program_id(1)
        first_trip = i - jnp.max(done_ref[...]).astype(jnp.int32)

        @pl.when(i == 0)
        def _():
            dk_ref[...] = jnp.zeros_like(dk_ref)
            dv_ref[...] = jnp.zeros_like(dv_ref)

        ma = _sb_masks()
        gate, o2, dog2 = g_ref[...], o_ref[...], dog_ref[...]
        sg, dsg = _silu_and_grad(gate)
        do2 = dog2 * sg
        dgate_ref[...] = dog2 * o2 * dsg
        lt = lt_ref[...]
        first = _iota2((tq, 2), 1) == 0
        ltots = (jnp.sum(jnp.where(first, lt, 0.0), axis=1, keepdims=True),
                 jnp.sum(jnp.where(first, 0.0, lt), axis=1, keepdims=True))
        q2 = q_ref[...]
        zq = jnp.zeros_like(q2)
        qs = (jnp.where(ma, q2, zq), jnp.where(ma, zq, q2))
        dob = do2.astype(BF16)
        dos = (jnp.where(ma, dob, zq), jnp.where(ma, zq, dob))
        upto = (_iota2((tk, tk), 0) <= _iota2((tk, tk), 1)).astype(BF16)
        before = (_iota2((tk, tk), 0) < _iota2((tk, tk), 1)).astype(BF16)
        qpos = i * tq + _iota2((tq, tk), 0)
        nb = tq // tk

        def trip(kb_bot, masked, carry):
            dq, la, lb, ea, eb = carry
            chains = [(b, h) for b in range(nb) for h in range(2)]
            rows, k2s, v2s, kss, masks = [], [], [], [], []
            for b in range(nb):
                kb = kb_bot + b
                rows.append(pl.ds(pl.multiple_of(kb * tk, tk), tk))
                k2 = k_ref[rows[b], :]
                zk = jnp.zeros_like(k2)
                k2s.append(k2)
                v2s.append(v_ref[rows[b], :])
                kss.append((jnp.where(ma, k2, zk), jnp.where(ma, zk, k2)))
                masks.append((kb * tk + _iota2((tq, tk), 1)) < qpos if masked else None)

            def keep(vals):
                return [jnp.where(masks[b], x, 0.0) for (b, h), x in zip(chains, vals)] if masked else vals

            zs = [_dot(qs[h], k2s[b], NT) for b, h in chains]
            dws = [_dot(dos[h], v2s[b], NT) for b, h in chains]
            ts = [jnp.log(1.0 + jnp.exp(-jnp.abs(z))) for z in zs]
            ls = keep([-(jnp.maximum(z, 0.0) + t) for z, t in zip(zs, ts)])
            lps = [jnp.minimum(z, 0.0) - t for z, t in zip(zs, ts)]
            cums = [_split3_dot(l, upto) for l in ls]
            lsums = [jnp.sum(l, axis=1, keepdims=True) for l in ls]
            offs, tot = {}, [la, lb]
            for b in range(nb):
                for h in range(2):
                    offs[(b, h)] = tot[h]
                    tot[h] = tot[h] + lsums[chains.index((b, h))]
            ws = keep([jnp.exp(lp + (ltots[h] - (offs[(b, h)] + c))) for (b, h), lp, c in zip(chains, lps, cums)])
            es = [dw * w for dw, w in zip(dws, ws)]
            ecums = [_split2_dot(e, before) for e in es]
            esums = [jnp.sum(e, axis=1, keepdims=True) for e in es]
            eoffs, etot = {}, [ea, eb]
            for b in range(nb):
                for h in range(2):
                    eoffs[(b, h)] = etot[h]
                    etot[h] = etot[h] + esums[chains.index((b, h))]
            dzs = keep([e - jnp.exp(lp) * (e + eoffs[ch] + ec) for ch, e, lp, ec in zip(chains, es, lps, ecums)])
            dzs = [dz.astype(BF16) for dz in dzs]
            wbs = [w.astype(BF16) for w in ws]
            for (b, h), dz in zip(chains, dzs):
                dq = dq + _dot(dz, kss[b][h])
            for b in range(nb):
                ia, ib = chains.index((b, 0)), chains.index((b, 1))
                dk_ref[rows[b], :] += _dot(dzs[ia], qs[0], TN) + _dot(dzs[ib], qs[1], TN)
                dv_ref[rows[b], :] += _dot(wbs[ia], dos[0], TN) + _dot(wbs[ib], dos[1], TN)
            return dq, tot[0], tot[1], etot[0], etot[1]

        z1 = jnp.zeros((tq, 1), F32)
        carry = lax.fori_loop(first_trip, i, lambda j, c: trip(j * nb, False, c),
                              (jnp.zeros((tq, LANE), F32), z1, z1, z1, z1))
        dq = trip(i * nb, True, carry)[0]
        dq_ref[...] = dq * (SB_DH ** -0.5)

    qblk = pl.BlockSpec((tq, LANE), lambda p, i: (i, p))
    full = pl.BlockSpec((T, LANE), lambda p, i: (0, p))
    return pl.pallas_call(
        body, name=name, grid=(SB_PAIRS, T // tq),
        in_specs=[qblk, full, full, qblk, qblk, pl.BlockSpec((None, tq, 2), lambda p, i: (p, i, 0)),
                  pl.BlockSpec((None, None, SUBLANE, LANE), lambda p, i: (p, i, 0, 0)),
                  pl.BlockSpec((tq, LANE), lambda p, i: (i, 4 * p + 3))],
        out_specs=[qblk, full, full, qblk],
        out_shape=[jax.ShapeDtypeStruct((T, SB_W), F32)] * 4,
        compiler_params=_params("parallel", "arbitrary"),
    )(qn, kn, vb, dog, o, ltot, done, proj)


def _dn_conv(ext, w_ref):
    return (w_ref[3:4, :] * ext + w_ref[2:3, :] * _down(ext, 1) + w_ref[1:2, :] * _down(ext, 2)
            + w_ref[0:1, :] * _down(ext, 3))


def _dn_prep(pqkv, conv_w, name):
    T, W = pqkv.shape
    tt = _tile(T, ROW_TILE, SUBLANE)
    B = DN_PREP_BLK
    nq, nqk = DN_QK_W // B, 2 * DN_QK_W // B

    def body(p_ref, ph_ref, w_ref, o_ref):
        i, cb = pl.program_id(0), pl.program_id(1)
        keep = (i > 0).astype(F32)
        ext = jnp.concatenate([ph_ref[...] * keep, p_ref[...]], axis=0)
        c = _dn_conv(ext, w_ref)[HALO:]
        a = c * _sigmoid(c)

        @pl.when(cb >= nqk)
        def _():
            o_ref[...] = a

        @pl.when(cb < nqk)
        def _():
            scale = jnp.where(cb < nq, DN_DK ** -0.5, 1.0)
            for hh in range(B // DN_DK):
                cols = slice(hh * DN_DK, (hh + 1) * DN_DK)
                ah = a[:, cols]
                o_ref[:, cols] = ah * (lax.rsqrt(jnp.sum(ah * ah, axis=-1, keepdims=True) + L2_EPS) * scale)

    return pl.pallas_call(
        body, name=name, grid=(T // tt, W // B),
        in_specs=[pl.BlockSpec((tt, B), lambda i, c: (i, c)),
                  pl.BlockSpec((HALO, B), lambda i, c: (jnp.maximum(i * (tt // HALO) - 1, 0), c)),
                  pl.BlockSpec((DN_CONV, B), lambda i, c: (0, c))],
        out_specs=pl.BlockSpec((tt, B), lambda i, c: (i, c)),
        out_shape=jax.ShapeDtypeStruct((T, W), F32),
        compiler_params=_params("parallel", "parallel"),
    )(pqkv, pqkv, conv_w)


def _dn_prep_bwd(pqkv, conv_w, dact, name):
    T, W = pqkv.shape
    tt = _tile(T, ROW_TILE, SUBLANE)
    nt = T // tt
    hb = tt // HALO
    B = DN_PREP_BLK
    nq, nqk = DN_QK_W // B, 2 * DN_QK_W // B

    def body(p_ref, pp_ref, pn_ref, w_ref, d_ref, dn_ref, o_ref, dw_ref):
        cb, i = pl.program_id(0), pl.program_id(1)
        keep_p = (i > 0).astype(F32)
        keep_n = (i < nt - 1).astype(F32)
        ext = jnp.concatenate([pp_ref[...] * keep_p, p_ref[...], pn_ref[...]], axis=0)
        c = _dn_conv(ext, w_ref)
        s = _sigmoid(c)
        a = c * s
        da_dc = s * (1.0 + c * (1.0 - s))
        d_up = jnp.concatenate([jnp.zeros((HALO, B), F32), d_ref[...], dn_ref[...] * keep_n], axis=0)
        scale = jnp.where(cb < nq, DN_DK ** -0.5, 1.0)
        normed = []
        for hh in range(B // DN_DK):
            cols = slice(hh * DN_DK, (hh + 1) * DN_DK)
            ah = a[:, cols]
            r = lax.rsqrt(jnp.sum(ah * ah, axis=-1, keepdims=True) + L2_EPS)
            y = ah * r
            dy = d_up[:, cols] * scale
            normed.append(r * (dy - y * jnp.sum(dy * y, axis=-1, keepdims=True)))
        dc = jnp.where(cb < nqk, jnp.concatenate(normed, axis=1), d_up) * da_dc
        dp = (w_ref[3:4, :] * dc + w_ref[2:3, :] * _up(dc, 1) + w_ref[1:2, :] * _up(dc, 2) + w_ref[0:1, :] * _up(dc, 3))
        main = slice(HALO, HALO + tt)
        o_ref[...] = dp[main]
        dcm = dc[main]
        part = jnp.concatenate([jnp.sum(dcm * _down(ext, 3 - k)[main], axis=0, keepdims=True) for k in range(DN_CONV)], axis=0)

        @pl.when(i == 0)
        def _():
            dw_ref[...] = part

        @pl.when(i > 0)
        def _():
            dw_ref[...] += part

    main_spec = pl.BlockSpec((tt, B), lambda c, i: (i, c))
    prev_spec = pl.BlockSpec((HALO, B), lambda c, i: (jnp.maximum(i * hb - 1, 0), c))
    next_spec = pl.BlockSpec((HALO, B), lambda c, i: (jnp.minimum((i + 1) * hb, nt * hb - 1), c))
    w_spec = pl.BlockSpec((DN_CONV, B), lambda c, i: (0, c))
    return pl.pallas_call(
        body, name=name, grid=(W // B, nt),
        in_specs=[main_spec, prev_spec, next_spec, w_spec, main_spec, next_spec],
        out_specs=[main_spec, w_spec],
        out_shape=[jax.ShapeDtypeStruct((T, W), F32), jax.ShapeDtypeStruct((DN_CONV, W), F32)],
        compiler_params=_params("parallel", "arbitrary"),
    )(pqkv, pqkv, pqkv, conv_w, dact, dact)


def _dn_gates(a_in, b_in, a_log, dt_bias, name):
    T, H = a_in.shape
    C = DN_CHUNK

    def body(a_ref, b_ref, al_ref, dt_ref, g_ref, beta_ref):
        beta_ref[...] = _sigmoid(b_ref[...])
        g_ref[...] = -jnp.exp(al_ref[...]) * _softplus(a_ref[...] + dt_ref[...])
        tri = (_iota2((C, C), 0) >= _iota2((C, C), 1)).astype(F32)

        def chunk(n, carry):
            rows = pl.ds(pl.multiple_of(n * C, C), C)
            g_ref[rows, :] = _hdot(tri, g_ref[rows, :])
            return carry

        lax.fori_loop(0, T // C, chunk, 0)

    return pl.pallas_call(body, name=name, out_shape=[jax.ShapeDtypeStruct((T, H), F32)] * 2)(a_in, b_in, a_log, dt_bias)


def _dn_gates_bwd(dg, dbeta, a_in, b_in, a_log, dt_bias, name):
    T, H = a_in.shape
    C = DN_CHUNK

    def body(dg_ref, db_ref, a_ref, b_ref, al_ref, dt_ref, da_ref, dbi_ref, dal_ref, ddt_ref):
        tri_t = (_iota2((C, C), 0) <= _iota2((C, C), 1)).astype(F32)

        def chunk(n, carry):
            rows = pl.ds(pl.multiple_of(n * C, C), C)
            da_ref[rows, :] = _hdot(tri_t, dg_ref[rows, :])
            return carry

        lax.fori_loop(0, T // C, chunk, 0)
        dla = da_ref[...]
        x = a_ref[...] + dt_ref[...]
        ea = jnp.exp(al_ref[...])
        da = dla * (-ea) * _sigmoid(x)
        da_ref[...] = da
        dal_ref[...] = jnp.sum(dla * (-ea * _softplus(x)), axis=0, keepdims=True)
        ddt_ref[...] = jnp.sum(da, axis=0, keepdims=True)
        beta = _sigmoid(b_ref[...])
        dbi_ref[...] = db_ref[...] * beta * (1.0 - beta)

    return pl.pallas_call(
        body, name=name,
        out_shape=[jax.ShapeDtypeStruct((T, H), F32)] * 2 + [jax.ShapeDtypeStruct((1, H), F32)] * 2,
    )(dg, dbeta, a_in, b_in, a_log, dt_bias)


def _dn_post(o_raw, pgate, gn, name):
    T = o_raw.shape[0]
    tt = _tile(T, ROW_TILE, SUBLANE)

    def body(o_ref, g_ref, gn_ref, out_ref):
        o, gate = o_ref[...], g_ref[...]
        r = lax.rsqrt(jnp.mean(o * o, axis=-1, keepdims=True) + RMS_EPS)
        out_ref[...] = (o * r * gn_ref[...] * (gate * _sigmoid(gate))).astype(BF16)

    blk = pl.BlockSpec((tt, DN_DV), lambda i, h: (i, h))
    return pl.pallas_call(
        body, name=name, grid=(T // tt, DN_HEADS),
        in_specs=[blk, blk, pl.BlockSpec((1, DN_DV), lambda i, h: (0, 0))], out_specs=blk,
        out_shape=jax.ShapeDtypeStruct((T, DN_V_W), BF16),
        compiler_params=_params("parallel", "parallel"),
    )(o_raw, pgate, gn)


def _dn_post_bwd(dog, o_raw, pgate, gn, name):
    T = o_raw.shape[0]
    tt = _tile(T, ROW_TILE, SUBLANE)

    def body(d_ref, o_ref, g_ref, gn_ref, do_ref, dgate_ref, dgn_ref):
        d, o, gate, gn_v = d_ref[...], o_ref[...], g_ref[...], gn_ref[...]
        sg, dsg = _silu_and_grad(gate)
        r = lax.rsqrt(jnp.mean(o * o, axis=-1, keepdims=True) + RMS_EPS)
        n = o * r
        dy = d * sg
        dgate_ref[...] = d * (n * gn_v) * dsg
        dn = dy * gn_v
        do_ref[...] = r * (dn - n * jnp.mean(dn * n, axis=-1, keepdims=True))
        part = jnp.sum(dy * n, axis=0, keepdims=True)
        first = (pl.program_id(0) == 0) & (pl.program_id(1) == 0)

        @pl.when(first)
        def _():
            dgn_ref[...] = part

        @pl.when(jnp.logical_not(first))
        def _():
            dgn_ref[...] += part

    blk = pl.BlockSpec((tt, DN_DV), lambda i, h: (i, h))
    vec = pl.BlockSpec((1, DN_DV), lambda i, h: (0, 0))
    return pl.pallas_call(
        body, name=name, grid=(T // tt, DN_HEADS),
        in_specs=[blk, blk, blk, vec], out_specs=[blk, blk, vec],
        out_shape=[jax.ShapeDtypeStruct((T, DN_V_W), F32)] * 2 + [jax.ShapeDtypeStruct((1, DN_DV), F32)],
        compiler_params=_params("arbitrary", "arbitrary"),
    )(dog, o_raw, pgate, gn)


def _dn_chunk_terms(q, k, gc, bc):
    C = DN_CHUNK
    r, c = _iota2((C, C), 0), _iota2((C, C), 1)
    lower, strict, eye = r >= c, r > c, r == c
    grow = jnp.sum(jnp.where(eye, gc, 0.0), axis=0, keepdims=True)
    decay = jnp.where(lower, jnp.exp(jnp.where(lower, gc - grow, 0.0)), 0.0)
    last = _iota2((C, 1), 0) == C - 1
    gl = jnp.sum(jnp.where(last, gc, 0.0), axis=0, keepdims=True)
    eg = jnp.exp(gc)
    egl = jnp.exp(gl - gc)
    kb = k * bc
    lmat = jnp.where(strict, _bdot(kb, k, NT) * decay, 0.0)
    aqk = jnp.where(lower, _bdot(q, k, NT) * decay, 0.0)
    return dict(lower=lower, strict=strict, eye=eye, last=last, decay=decay, gl=gl, eg=eg, egl=egl, kb=kb,
                lmat=lmat, aqk=aqk, qd=q * eg, kd=k * egl)


def _split(x):
    hi = x.astype(BF16)
    return hi, (x - hi.astype(F32)).astype(BF16)


def _x3dot(a, b, dims=NN):
    ah, al = a if isinstance(a, tuple) else _split(a)
    bh, bl = b if isinstance(b, tuple) else _split(b)
    return _dot(ah, bh, dims) + (_dot(ah, bl, dims) + _dot(al, bh, dims))


def _interleave(gens):
    for _ in itertools.zip_longest(*gens):
        pass


def _unit_lower_inverse_steps(lmat, eye, out):
    ident = jnp.where(eye, 1.0, 0.0).astype(F32)
    m = -lmat
    inv = ident + m
    for _ in range(int(math.log2(DN_CHUNK)) - 1):
        ms = _split(m)
        m = _x3dot(ms, ms)
        yield
        inv = inv + _x3dot(inv, m)
        yield
    out["tm"] = inv


def _dn_chunk_fwd(act, g, beta, name):
    T = act.shape[0]
    C, H = DN_CHUNK, DN_HEADS
    N = T // C

    def body(a_ref, g_ref, b_ref, o_ref, s_out, t_out, vn_out, u_out, w_out, s_scr):
        n = pl.program_id(0)

        @pl.when(n == 0)
        def _():
            s_scr[...] = jnp.zeros_like(s_scr)

        head_lane = _iota2((C, H), 1)

        def head(hh):
            qs, vs = slice(hh * DN_DK, (hh + 1) * DN_DK), slice(hh * DN_DV, (hh + 1) * DN_DV)
            q, k, v = a_ref[:, qs], a_ref[:, DN_QK_W + hh * DN_DK:DN_QK_W + (hh + 1) * DN_DK], \
                a_ref[:, 2 * DN_QK_W + hh * DN_DV:2 * DN_QK_W + (hh + 1) * DN_DV]
            gc = jnp.sum(jnp.where(head_lane == hh, g_ref[...], 0.0), axis=1, keepdims=True)
            bc = jnp.sum(jnp.where(head_lane == hh, b_ref[...], 0.0), axis=1, keepdims=True)
            t = _dn_chunk_terms(q, k, gc, bc)
            yield
            res = {}
            yield from _unit_lower_inverse_steps(t["lmat"], t["eye"], res)
            tms = _split(res["tm"])
            u = _x3dot(tms, v * bc)
            yield
            w = _x3dot(tms, t["kb"] * t["eg"])
            yield
            s = s_scr[hh]
            s_out[hh] = s
            t_out[hh] = res["tm"]
            sb = s.astype(BF16)
            vn = u - _dot(w.astype(BF16), sb)
            yield
            o_ref[:, vs] = _dot(t["qd"].astype(BF16), sb) + _bdot(t["aqk"], vn)
            yield
            s_scr[hh] = s * jnp.exp(t["gl"]) + _bdot(t["kd"], vn, TN)
            vn_out[:, vs] = vn
            u_out[:, vs] = u
            w_out[:, qs] = w

        _interleave([head(hh) for hh in range(H)])

    row = lambda w: pl.BlockSpec((C, w), lambda n: (n, 0))
    return pl.pallas_call(
        body, name=name, grid=(N,),
        in_specs=[row(DN_CONV_W), row(H), row(H)],
        out_specs=[row(DN_V_W),
                   pl.BlockSpec((H, None, DN_DK, DN_DV), lambda n: (0, n, 0, 0)),
                   pl.BlockSpec((H, None, C, C), lambda n: (0, n, 0, 0)),
                   row(DN_V_W), row(DN_V_W), row(DN_QK_W)],
        out_shape=[jax.ShapeDtypeStruct((T, DN_V_W), F32),
                   jax.ShapeDtypeStruct((H, N, DN_DK, DN_DV), F32),
                   jax.ShapeDtypeStruct((H, N, C, C), F32),
                   jax.ShapeDtypeStruct((T, DN_V_W), F32),
                   jax.ShapeDtypeStruct((T, DN_V_W), F32),
                   jax.ShapeDtypeStruct((T, DN_QK_W), F32)],
        scratch_shapes=[pltpu.VMEM((H, DN_DK, DN_DV), F32)],
        compiler_params=_params("arbitrary"),
    )(act, g, beta)


def _dn_chunk_bwd(act, g, beta, s_saved, tm_saved, vn_saved, u_saved, w_saved, do, name):
    T = act.shape[0]
    C, H = DN_CHUNK, DN_HEADS
    N = T // C

    def body(a_ref, g_ref, b_ref, s_ref, t_ref, vn_ref, u_ref, w_ref, do_ref, da_ref, dg_ref, db_ref, ds_scr):
        @pl.when(pl.program_id(0) == 0)
        def _():
            ds_scr[...] = jnp.zeros_like(ds_scr)

        head_lane = _iota2((C, H), 1)
        dg_cols, db_cols = {}, {}

        def head(hh):
            qs, vs = slice(hh * DN_DK, (hh + 1) * DN_DK), slice(hh * DN_DV, (hh + 1) * DN_DV)
            ks = slice(DN_QK_W + hh * DN_DK, DN_QK_W + (hh + 1) * DN_DK)
            vas = slice(2 * DN_QK_W + hh * DN_DV, 2 * DN_QK_W + (hh + 1) * DN_DV)
            q, k, v = a_ref[:, qs], a_ref[:, ks], a_ref[:, vas]
            gc = jnp.sum(jnp.where(head_lane == hh, g_ref[...], 0.0), axis=1, keepdims=True)
            bc = jnp.sum(jnp.where(head_lane == hh, b_ref[...], 0.0), axis=1, keepdims=True)
            t = _dn_chunk_terms(q, k, gc, bc)
            yield
            lower, strict, eye = t["lower"], t["strict"], t["eye"]
            decay, eg, egl, kb, qd, kd = t["decay"], t["eg"], t["egl"], t["kb"], t["qd"], t["kd"]
            s, tm, vn, u, w, d_o = s_ref[hh], t_ref[hh], vn_ref[:, vs], u_ref[:, vs], w_ref[:, qs], do_ref[:, vs]
            ds_next = ds_scr[hh]
            egl_tot = jnp.exp(t["gl"])
            dob, sb, dsb, vnb = d_o.astype(BF16), s.astype(BF16), ds_next.astype(BF16), vn.astype(BF16)

            dvn = _bdot(t["aqk"], dob, TN) + _bdot(kd, dsb)
            yield
            daqk = jnp.where(lower, _dot(dob, vnb, NT), 0.0)
            dqd = _dot(dob, sb, NT)
            dkd = _dot(vnb, dsb, NT)
            yield
            dvnb = dvn.astype(BF16)
            ds_scr[hh] = _bdot(qd, dob, TN) + egl_tot * ds_next - _bdot(w, dvnb, TN)
            dgl = egl_tot * jnp.sum(jnp.sum(s * ds_next, axis=1, keepdims=True), axis=0, keepdims=True)
            dw = -_dot(dvnb, sb, NT)
            yield
            tms = _split(tm)
            dru = _x3dot(tms, dvn, TN)
            drw = _x3dot(tms, dw, TN)
            yield
            dl = -jnp.where(strict, _x3dot(dru, u, NT) + _x3dot(drw, w, NT), 0.0)
            yield
            dkk = (dl * decay).astype(BF16)
            dqk = (daqk * decay).astype(BF16)
            dkb = _bdot(dkk, k) + drw * eg
            yield
            da_ref[:, ks] = _bdot(dkk, kb, TN) + _bdot(dqk, q, TN) + dkd * egl + dkb * bc
            da_ref[:, qs] = _bdot(dqk, k) + dqd * eg
            da_ref[:, vas] = dru * bc
            yield
            db_cols[hh] = jnp.sum(dru * v, axis=1, keepdims=True) + jnp.sum(dkb * k, axis=1, keepdims=True)
            pm = dl * t["lmat"] + daqk * t["aqk"]
            col_as_col = jnp.sum(jnp.where(eye, jnp.sum(pm, axis=0, keepdims=True), 0.0), axis=1, keepdims=True)
            kdsum = jnp.sum(dkd * kd, axis=1, keepdims=True)
            dgc = (jnp.sum(pm, axis=1, keepdims=True) - col_as_col + jnp.sum(dqd * qd, axis=1, keepdims=True)
                   - kdsum + jnp.sum(drw * (kb * eg), axis=1, keepdims=True))
            dgl = dgl + jnp.sum(kdsum, axis=0, keepdims=True)
            dg_cols[hh] = dgc + jnp.where(t["last"], dgl, 0.0)

        _interleave([head(hh) for hh in range(H)])
        dg_ref[...] = sum(jnp.where(head_lane == hh, dg_cols[hh], 0.0) for hh in range(H))
        db_ref[...] = sum(jnp.where(head_lane == hh, db_cols[hh], 0.0) for hh in range(H))

    row = lambda w: pl.BlockSpec((C, w), lambda n: (N - 1 - n, 0))
    return pl.pallas_call(
        body, name=name, grid=(N,),
        in_specs=[row(DN_CONV_W), row(H), row(H),
                  pl.BlockSpec((H, None, DN_DK, DN_DV), lambda n: (0, N - 1 - n, 0, 0)),
                  pl.BlockSpec((H, None, C, C), lambda n: (0, N - 1 - n, 0, 0)),
                  row(DN_V_W), row(DN_V_W), row(DN_QK_W), row(DN_V_W)],
        out_specs=[row(DN_CONV_W), row(H), row(H)],
        out_shape=[jax.ShapeDtypeStruct((T, DN_CONV_W), F32),
                   jax.ShapeDtypeStruct((T, H), F32), jax.ShapeDtypeStruct((T, H), F32)],
        scratch_shapes=[pltpu.VMEM((H, DN_DK, DN_DV), F32)],
        compiler_params=_params("arbitrary"),
    )(act, g, beta, s_saved, tm_saved, vn_saved, u_saved, w_saved, do)


def _dn_split_w_in(w):
    wab = jnp.pad(w[:, DN_CONV_W + DN_V_W:], ((0, 0), (0, DN_AB_PAD - 2 * DN_HEADS)))
    return w[:, :DN_CONV_W], w[:, DN_CONV_W:DN_CONV_W + DN_V_W], wab


def _dn_layer_fwd(h, wts, conv_w, a_log, dt_bias, gn, w_out, x_res, tag):
    wqkv, wgate, wab = wts
    H = DN_HEADS
    pqkv = _matmul(h, wqkv, "nn", tag + "_pqkv")
    pgate = _matmul(h, wgate, "nn", tag + "_pgate")
    pab = _matmul(h, wab, "nn", tag + "_pab")
    a_in, b_in = pab[:, :H], pab[:, H:2 * H]
    g, beta = _dn_gates(a_in, b_in, a_log, dt_bias, tag + "_gates")
    act = _dn_prep(pqkv, conv_w, tag + "_prep")
    o_raw, s_sv, tm_sv, vn_sv, u_sv, w_sv = _dn_chunk_fwd(act, g, beta, tag + "_chunk_fwd")
    og = _dn_post(o_raw, pgate, gn, tag + "_post")
    y = _matmul(og, w_out, "nn", tag + "_out", add=x_res)
    saved = dict(h=h, wts=wts, conv_w=conv_w, a_log=a_log, dt_bias=dt_bias, gn=gn, w_out=w_out, pqkv=pqkv, pgate=pgate,
                 a_in=a_in, b_in=b_in, g=g, beta=beta, act=act, o_raw=o_raw, chunk=(s_sv, tm_sv, vn_sv, u_sv, w_sv), og=og)
    return y, saved


def _dn_layer_bwd(dout, sv, tag):
    wqkv, wgate, wab = sv["wts"]
    h = sv["h"]
    dog = _matmul(dout, sv["w_out"], "nt", tag + "_dog")
    dw_out = _matmul(sv["og"], dout, "tn", tag + "_dwout")
    do_raw, dgate, dgn = _dn_post_bwd(dog, sv["o_raw"], sv["pgate"], sv["gn"], tag + "_post_bwd")
    dact, dg, dbeta = _dn_chunk_bwd(sv["act"], sv["g"], sv["beta"], *sv["chunk"], do_raw, tag + "_chunk_bwd")
    da_in, db_in, da_log, ddt = _dn_gates_bwd(dg, dbeta, sv["a_in"], sv["b_in"], sv["a_log"], sv["dt_bias"],
                                              tag + "_gates_bwd")
    dpqkv, dconv = _dn_prep_bwd(sv["pqkv"], sv["conv_w"], dact, tag + "_prep_bwd")
    dpab = jnp.pad(jnp.concatenate([da_in, db_in], axis=1), ((0, 0), (0, DN_AB_PAD - 2 * DN_HEADS)))
    dwqkv = _matmul(h, dpqkv, "tn", tag + "_dwqkv")
    dwgate = _matmul(h, dgate, "tn", tag + "_dwgate")
    dwab = _matmul(h, dpab, "tn", tag + "_dwab")
    dh = _matmul(dpqkv, wqkv, "nt", tag + "_dh0")
    dh = _matmul(dgate, wgate, "nt", tag + "_dh1", add=dh)
    dh = _matmul(dpab, wab, "nt", tag + "_dh2", add=dh)
    dw_in = jnp.concatenate([dwqkv, dwgate, dwab[:, :2 * DN_HEADS]], axis=1)
    return dh, (dw_in, dconv, da_log, ddt, dgn, dw_out)


def _sb_layer_fwd(h, w_in_perm, qg, kg, w_out, x_res, tag):
    qg2, kg2 = jnp.tile(qg, (1, 2)), jnp.tile(kg, (1, 2))
    proj = _matmul(h, w_in_perm, "nn", tag + "_proj")
    qn, kn, vb = _sb_prep(proj, qg2, kg2, tag + "_prep")
    o, og, ltot, done = _sb_attn_fwd(qn, kn, vb, proj, tag + "_attn_fwd")
    y = _matmul(og, w_out, "nn", tag + "_out", add=x_res)
    saved = dict(h=h, w_in=w_in_perm, qg2=qg2, kg2=kg2, w_out=w_out, proj=proj, qn=qn, kn=kn, vb=vb, o=o, og=og, ltot=ltot,
                 done=done)
    return y, saved


def _sb_layer_bwd(dout, sv, tag):
    dog = _matmul(dout, sv["w_out"], "nt", tag + "_dog")
    dw_out = _matmul(sv["og"], dout, "tn", tag + "_dwout")
    dqn, dkn, dv, dgate = _sb_attn_bwd(sv["qn"], sv["kn"], sv["vb"], dog, sv["o"], sv["ltot"], sv["done"], sv["proj"],
                                       tag + "_attn_bwd")
    dproj, dqgp, dkgp = _sb_prep_bwd(sv["proj"], dqn, dkn, dv, dgate, sv["qg2"], sv["kg2"], tag + "_prep_bwd")
    dw_in = _sb_unperm(_matmul(sv["h"], dproj, "tn", tag + "_dwin"))
    dh = _matmul(dproj, sv["w_in"], "nt", tag + "_dh")
    dqg = _fold_heads(dqgp.reshape(SB_PAIRS, LANE), tag + "_dqg")
    dkg = _fold_heads(dkgp.reshape(SB_PAIRS, LANE), tag + "_dkg")
    return dh, (dw_in, dqg, dkg, dw_out)


def _sc_layer_fwd(h, w_in_perm, conv_w, w_out, x_res, tag):
    proj = _matmul(h, w_in_perm, "nn", tag + "_proj")
    yg = _sc_fwd(proj, conv_w, tag + "_fwd")
    y = _matmul(yg, w_out, "nn", tag + "_out", add=x_res)
    return y, dict(h=h, w_in=w_in_perm, conv_w=conv_w, w_out=w_out, proj=proj, yg=yg)


def _sc_layer_bwd(dout, sv, tag):
    dyg = _matmul(dout, sv["w_out"], "nt", tag + "_dyg")
    dw_out = _matmul(sv["yg"], dout, "tn", tag + "_dwout")
    dproj, dconv = _sc_bwd(dyg, sv["proj"], sv["conv_w"], tag + "_bwd")
    dw_in = _sc_unperm(_matmul(sv["h"], dproj, "tn", tag + "_dwin"))
    dh = _matmul(dproj, sv["w_in"], "nt", tag + "_dh")
    return dh, (dw_in, dconv, dw_out)


def _local_step(x, target, W):
    norm_g = W["norm_g"]
    xs, saves = [x], []
    for i in range(4):
        h = _rmsnorm_fwd(xs[i], norm_g[i:i + 1], f"norm{i}")
        if i in (0, 3):
            j = i // 3
            y, sv = _dn_layer_fwd(h, W["dn_w_in"][j], W["dn_conv_w"][j], W["dn_a_log"][j:j + 1], W["dn_dt_bias"][j:j + 1],
                                  W["dn_o_norm_g"][j:j + 1], W["dn_w_out"][j], xs[i], f"dn{j}")
        elif i == 1:
            y, sv = _sb_layer_fwd(h, W["sb_w_in"], W["sb_q_norm_g"], W["sb_k_norm_g"], W["sb_w_out"], xs[i], "sb")
        else:
            y, sv = _sc_layer_fwd(h, W["sc_w_in"], W["sc_conv_w"], W["sc_w_out"], xs[i], "sc")
        xs.append(y)
        saves.append(sv)
    dx, loss = _loss_head(xs[4], target)
    G = {}
    dnorm = [None] * 4
    dn_parts = [None, None]
    for i in (3, 2, 1, 0):
        if i in (0, 3):
            dh, dn_parts[i // 3] = _dn_layer_bwd(dx, saves[i], f"dn{i // 3}")
        elif i == 1:
            dh, (G["sb_w_in"], G["sb_q_norm_g"], G["sb_k_norm_g"], G["sb_w_out"]) = _sb_layer_bwd(dx, saves[i], "sb")
        else:
            dh, (G["sc_w_in"], G["sc_conv_w"], G["sc_w_out"]) = _sc_layer_bwd(dx, saves[i], "sc")
        dx, dnorm[i] = _rmsnorm_bwd(dh, xs[i], norm_g[i:i + 1], dx, f"norm{i}_bwd")
    G["norm_g"] = jnp.concatenate(dnorm, axis=0)
    for k, name in enumerate(("dn_w_in", "dn_conv_w", "dn_a_log", "dn_dt_bias", "dn_o_norm_g", "dn_w_out")):
        G[name] = jnp.stack([dn_parts[0][k], dn_parts[1][k]], axis=0)
    return loss, dx, G


def _adamw(w, m, v, parts, name):
    R, C = w.shape
    tr = _tile(R, 128, SUBLANE)

    def body(w_ref, m_ref, v_ref, p_ref, g_ref, d_ref, nm_ref, nv_ref):
        g = p_ref[0].astype(F32)
        for s in range(1, N_DEV):
            g = g + p_ref[s].astype(F32)
        m2 = ADAM_B1 * m_ref[...] + (1.0 - ADAM_B1) * g
        v2 = ADAM_B2 * v_ref[...] + (1.0 - ADAM_B2) * (g * g)
        m_hat = m2 / (1.0 - ADAM_B1 ** ADAM_STEP)
        v_hat = v2 / (1.0 - ADAM_B2 ** ADAM_STEP)
        g_ref[...] = g
        d_ref[...] = -ADAM_LR * (m_hat / (jnp.sqrt(v_hat) + ADAM_EPS) + ADAM_WD * w_ref[...])
        nm_ref[...] = m2
        nv_ref[...] = v2

    blk = pl.BlockSpec((tr, C), lambda i: (i, 0))
    return pl.pallas_call(
        body, name=name, grid=(R // tr,),
        in_specs=[blk, blk, blk, pl.BlockSpec((N_DEV, tr, C), lambda i: (0, i, 0))],
        out_specs=[blk] * 4, out_shape=[jax.ShapeDtypeStruct((R, C), F32)] * 4,
        compiler_params=_params("parallel"),
    )(w, m, v, parts)


_HBM = pl.BlockSpec(memory_space=pltpu.HBM)
_MESH = pl.DeviceIdType.MESH


def _slot(x, y, c):
    return 4 * x + 2 * y + c


def _all_gather(shards, name):
    n = len(shards)

    def body(*refs):
        ins, outs = refs[:n], refs[n:2 * n]
        send_sems, recv_sems, local_sems = refs[2 * n:]
        x, y, c = lax.axis_index("x"), lax.axis_index("y"), lax.axis_index("c")
        me, sibling = (x, y, c), (x, y, 1 - c)
        chips = [(1 - x, y), (x, 1 - y), (1 - x, 1 - y)]

        def copy(a, k, block, to, src=None):
            dst = outs[a].at[_slot(*block)]
            return pltpu.make_async_remote_copy(src_ref=dst if src is None else src, dst_ref=dst,
                                                send_sem=send_sems.at[a, k], recv_sem=recv_sems.at[a, k],
                                                device_id=to, device_id_type=_MESH)

        mine = [pltpu.make_async_copy(ins[a], outs[a].at[_slot(*me)], local_sems.at[a]) for a in range(n)]
        for cp in mine:
            cp.start()
        first = []
        for a in range(n):
            first.append(copy(a, 0, me, sibling, src=ins[a]))
            first += [copy(a, 1 + j, me, (*chip, c), src=ins[a]) for j, chip in enumerate(chips)]
        for cp in first:
            cp.start()
        passed = []
        for j, chip in enumerate(chips):
            for a in range(n):
                copy(a, 1 + j, (*chip, c), me).wait_recv()
                fwd = copy(a, 4 + j, (*chip, c), sibling)
                fwd.start()
                passed.append(fwd)
        for a in range(n):
            copy(a, 0, sibling, me).wait_recv()
            for j, chip in enumerate(chips):
                copy(a, 4 + j, (*chip, 1 - c), me).wait_recv()
        for cp in first + passed:
            cp.wait_send()
        for cp in mine:
            cp.wait()

    return pl.pallas_call(
        body, name=name,
        in_specs=[_HBM] * n, out_specs=[_HBM] * n,
        out_shape=[jax.ShapeDtypeStruct((N_DEV,) + s.shape, s.dtype) for s in shards],
        scratch_shapes=[pltpu.SemaphoreType.DMA((n, N_DEV - 1)), pltpu.SemaphoreType.DMA((n, N_DEV - 1)),
                        pltpu.SemaphoreType.DMA((n,))],
    )(*shards)


def _exchange(arrays, scatter, name):
    n = len(arrays)

    def body(*refs):
        ins, outs = refs[:n], refs[n:2 * n]
        send_sems, recv_sems, local_sems = refs[2 * n:]
        x, y, c = lax.axis_index("x"), lax.axis_index("y"), lax.axis_index("c")
        me = _slot(x, y, c)
        copies = []
        for a in range(n):
            cp = pltpu.make_async_copy(ins[a].at[me] if scatter[a] else ins[a], outs[a].at[me], local_sems.at[a])
            cp.start()
            copies.append(cp)
        for r in range(1, N_DEV):
            px = 1 - x if r & 4 else x
            py = 1 - y if r & 2 else y
            pc = 1 - c if r & 1 else c
            for a in range(n):
                cp = pltpu.make_async_remote_copy(
                    src_ref=ins[a].at[_slot(px, py, pc)] if scatter[a] else ins[a], dst_ref=outs[a].at[me],
                    send_sem=send_sems.at[a, r - 1], recv_sem=recv_sems.at[a, r - 1],
                    device_id=(px, py, pc), device_id_type=_MESH)
                cp.start()
                copies.append(cp)
        for cp in copies:
            cp.wait()

    shapes = [a.shape[1:] if s else a.shape for a, s in zip(arrays, scatter)]
    return pl.pallas_call(
        body, name=name,
        in_specs=[_HBM] * n, out_specs=[_HBM] * n,
        out_shape=[jax.ShapeDtypeStruct((N_DEV,) + tuple(s), a.dtype) for s, a in zip(shapes, arrays)],
        scratch_shapes=[pltpu.SemaphoreType.DMA((n, N_DEV - 1)), pltpu.SemaphoreType.DMA((n, N_DEV - 1)),
                        pltpu.SemaphoreType.DMA((n,))],
    )(*arrays)


_SHARDED = ("dn_w_in", "dn_conv_w", "dn_o_norm_g", "dn_w_out", "sb_w_in", "sb_w_out", "sc_w_in", "sc_conv_w", "sc_w_out")
_MATMUL_WEIGHTS = ("dn_w_in", "dn_w_out", "sb_w_in", "sb_w_out", "sc_w_in", "sc_w_out")
_COLUMN_SHARDED = ("dn_w_in", "dn_conv_w", "dn_o_norm_g", "sb_w_in", "sc_w_in", "sc_conv_w")
_REPLICATED = ("norm_g", "dn_a_log", "dn_dt_bias", "sb_q_norm_g", "sb_k_norm_g")
_ORDER = ("norm_g", "dn_w_in", "dn_conv_w", "dn_a_log", "dn_dt_bias", "dn_o_norm_g", "dn_w_out", "sb_w_in", "sb_q_norm_g",
          "sb_k_norm_g", "sb_w_out", "sc_w_in", "sc_conv_w", "sc_w_out")
_PACK_COLS = D_MODEL


def _assemble(name, gathered):
    if name in _COLUMN_SHARDED:
        g = jnp.moveaxis(gathered, 0, -2)
        return g.reshape(g.shape[:-2] + (g.shape[-2] * g.shape[-1],))
    g = jnp.moveaxis(gathered, 0, 1)
    return g.reshape((g.shape[0], g.shape[1] * g.shape[2]) + g.shape[3:])


def _disassemble(name, full):
    if name in _COLUMN_SHARDED:
        g = full.reshape(full.shape[:-1] + (N_DEV, full.shape[-1] // N_DEV))
        return jnp.moveaxis(g, -2, 0)
    g = full.reshape((full.shape[0], N_DEV, full.shape[1] // N_DEV) + full.shape[2:])
    return jnp.moveaxis(g, 1, 0)


def _pack_replicated(d):
    rows = [d["norm_g"]]
    for name in _REPLICATED[1:]:
        flat = d[name].reshape(1, -1)
        rows.append(jnp.pad(flat, ((0, 0), (0, _PACK_COLS - flat.shape[1]))))
    return jnp.concatenate(rows, axis=0)


def _unpack_replicated(p, like):
    out = {"norm_g": p[:4]}
    for r, name in enumerate(_REPLICATED[1:]):
        shape = like[name].shape
        out[name] = p[4 + r, :math.prod(shape)].reshape(shape)
    return out


def kernel(x, norm_g, dn_w_in, dn_conv_w, dn_a_log, dn_dt_bias, dn_o_norm_g, dn_w_out, sb_w_in, sb_q_norm_g, sb_k_norm_g, sb_w_out, sc_w_in, sc_conv_w, sc_w_out, loss_target, m_norm_g, m_dn_w_in, m_dn_conv_w, m_dn_a_log, m_dn_dt_bias, m_dn_o_norm_g, m_dn_w_out, m_sb_w_in, m_sb_q_norm_g, m_sb_k_norm_g, m_sb_w_out, m_sc_w_in, m_sc_conv_w, m_sc_w_out, v_norm_g, v_dn_w_in, v_dn_conv_w, v_dn_a_log, v_dn_dt_bias, v_dn_o_norm_g, v_dn_w_out, v_sb_w_in, v_sb_q_norm_g, v_sb_k_norm_g, v_sb_w_out, v_sc_w_in, v_sc_conv_w, v_sc_w_out):
    w = dict(norm_g=norm_g, dn_w_in=dn_w_in, dn_conv_w=dn_conv_w, dn_a_log=dn_a_log, dn_dt_bias=dn_dt_bias,
             dn_o_norm_g=dn_o_norm_g, dn_w_out=dn_w_out, sb_w_in=sb_w_in, sb_q_norm_g=sb_q_norm_g, sb_k_norm_g=sb_k_norm_g,
             sb_w_out=sb_w_out, sc_w_in=sc_w_in, sc_conv_w=sc_conv_w, sc_w_out=sc_w_out)
    m = dict(norm_g=m_norm_g, dn_w_in=m_dn_w_in, dn_conv_w=m_dn_conv_w, dn_a_log=m_dn_a_log, dn_dt_bias=m_dn_dt_bias,
             dn_o_norm_g=m_dn_o_norm_g, dn_w_out=m_dn_w_out, sb_w_in=m_sb_w_in, sb_q_norm_g=m_sb_q_norm_g,
             sb_k_norm_g=m_sb_k_norm_g, sb_w_out=m_sb_w_out, sc_w_in=m_sc_w_in, sc_conv_w=m_sc_conv_w, sc_w_out=m_sc_w_out)
    v = dict(norm_g=v_norm_g, dn_w_in=v_dn_w_in, dn_conv_w=v_dn_conv_w, dn_a_log=v_dn_a_log, dn_dt_bias=v_dn_dt_bias,
             dn_o_norm_g=v_dn_o_norm_g, dn_w_out=v_dn_w_out, sb_w_in=v_sb_w_in, sb_q_norm_g=v_sb_q_norm_g,
             sb_k_norm_g=v_sb_k_norm_g, sb_w_out=v_sb_w_out, sc_w_in=v_sc_w_in, sc_conv_w=v_sc_conv_w, sc_w_out=v_sc_w_out)

    shards = [w[k].astype(BF16) if k in _MATMUL_WEIGHTS else w[k] for k in _SHARDED]
    gathered = _all_gather(shards, "gather_weights")
    W = {k: _assemble(k, g) for k, g in zip(_SHARDED, gathered)}
    for k in _REPLICATED:
        W[k] = w[k]
    W["dn_w_in"] = [_dn_split_w_in(W["dn_w_in"][j]) for j in range(2)]
    W["sb_w_in"] = _sb_perm(W["sb_w_in"][0])
    W["sb_w_out"] = W["sb_w_out"][0]
    W["sc_w_in"] = _sc_perm(W["sc_w_in"][0])
    W["sc_conv_w"] = W["sc_conv_w"][0]
    W["sc_w_out"] = W["sc_w_out"][0]

    loss_part, dx, G = _local_step(x[0], loss_target[0], W)
    for k in ("sb_w_in", "sb_w_out", "sc_w_in", "sc_conv_w", "sc_w_out"):
        G[k] = G[k][None]
    G["dn_a_log"] = G["dn_a_log"].reshape(2, DN_HEADS)
    G["dn_dt_bias"] = G["dn_dt_bias"].reshape(2, DN_HEADS)
    G["dn_o_norm_g"] = G["dn_o_norm_g"].reshape(2, DN_DV)

    outgoing = [_disassemble(k, G[k].astype(BF16) if k in _MATMUL_WEIGHTS else G[k]) for k in _SHARDED] + [_pack_replicated(G)]
    landed = _exchange(outgoing, [True] * len(_SHARDED) + [False], "exchange_grads")

    res = {}
    for k, parts in zip(_SHARDED, landed[:-1]):
        shape = w[k].shape
        rows = math.prod(shape[:-1])
        flat = lambda a: a.reshape(rows, shape[-1])
        outs = _adamw(flat(w[k]), flat(m[k]), flat(v[k]), parts.reshape(N_DEV, rows, shape[-1]), "adamw_" + k)
        res[k] = [o.reshape(shape) for o in outs]
    outs = _adamw(_pack_replicated(w), _pack_replicated(m), _pack_replicated(v), landed[-1], "adamw_replicated")
    unpacked = [_unpack_replicated(o, w) for o in outs]
    for k in _REPLICATED:
        res[k] = [u[k] for u in unpacked]

    loss = lax.psum(loss_part[0, 0], ("x", "y", "c"))
    return (loss, dx[None]) + tuple(res[k][0] for k in _ORDER) + tuple(res[k][1] for k in _ORDER) \
        + tuple(res[k][2] for k in _ORDER) + tuple(res[k][3] for k in _ORDER)
```

```python
import itertools
import math

import jax
import jax.numpy as jnp
from jax import lax
from jax.experimental import pallas as pl
from jax.experimental.pallas import tpu as pltpu

F32 = jnp.float32
BF16 = jnp.bfloat16
HIGHEST = lax.Precision.HIGHEST

N_DEV = 8
D_MODEL = 1024
RMS_EPS = 1e-6
L2_EPS = 1e-6

DN_HEADS = 8
DN_DK = 128
DN_DV = 256
DN_QK_W = DN_HEADS * DN_DK
DN_V_W = DN_HEADS * DN_DV
DN_CONV = 4
DN_CHUNK = 64
DN_CONV_W = 2 * DN_QK_W + DN_V_W
DN_IN = DN_CONV_W + DN_V_W + 2 * DN_HEADS
DN_AB_PAD = 128
DN_PREP_BLK = 512

SB_HEADS = 16
SB_DH = 64
SB_W = SB_HEADS * SB_DH
SB_PAIRS = SB_HEADS // 2
SB_TQ = 256
SB_TK = 128
SB_DEAD = -106.0

SC_W = 2 * D_MODEL
SC_CONV = 3
SC_BLK = 512
SC_NBLK = SC_W // SC_BLK

ADAM_LR = 0.001
ADAM_B1 = 0.9
ADAM_B2 = 0.999
ADAM_EPS = 1e-08
ADAM_WD = 0.01
ADAM_STEP = 10

LANE = 128
SUBLANE = 8
HALO = SUBLANE
ROW_TILE = 256
WIDE_ROW_TILE = 128
VMEM_LIMIT = 48 * 2 ** 20

NN = ((1,), (0,))
NT = ((1,), (1,))
TN = ((0,), (0,))


def _dot(a, b, dims=NN, precision=None):
    return lax.dot_general(a, b, (dims, ((), ())), precision=precision, preferred_element_type=F32)


def _bdot(a, b, dims=NN):
    return _dot(a.astype(BF16), b.astype(BF16), dims)


def _hdot(a, b, dims=NN):
    return _dot(a, b, dims, precision=HIGHEST)


def _tile(dim, pref, align=LANE):
    t = (min(pref, dim) // align) * align
    while t >= align:
        if dim % t == 0:
            return t
        t -= align
    return dim


def _params(*sem):
    return pltpu.CompilerParams(dimension_semantics=sem, vmem_limit_bytes=VMEM_LIMIT)


def _sigmoid(x):
    return 1.0 / (1.0 + jnp.exp(-x))


def _softplus(x):
    return jnp.maximum(x, 0.0) + jnp.log(1.0 + jnp.exp(-jnp.abs(x)))


def _silu_and_grad(x):
    s = _sigmoid(x)
    return x * s, s * (1.0 + x * (1.0 - s))


def _iota2(shape, dim):
    return lax.broadcasted_iota(jnp.int32, shape, dim)


def _matmul(a, b, mode, name, out_dtype=F32, add=None, tm=1024, tn=1024, tk=1024):
    if mode == "nn":
        (M, K), (K2, N) = a.shape, b.shape
    elif mode == "nt":
        (M, K), (N, K2) = a.shape, b.shape
    else:
        (K, M), (K2, N) = a.shape, b.shape
    assert K == K2, (a.shape, b.shape, mode)
    tm, tn, tk = _tile(M, tm), _tile(N, tn), _tile(K, tk)
    nk = K // tk
    dims = {"nn": NN, "nt": NT, "tn": TN}[mode]
    a_spec = pl.BlockSpec((tk, tm), lambda i, j, k: (k, i)) if mode == "tn" else pl.BlockSpec((tm, tk), lambda i, j, k: (i, k))
    b_spec = pl.BlockSpec((tn, tk), lambda i, j, k: (j, k)) if mode == "nt" else pl.BlockSpec((tk, tn), lambda i, j, k: (k, j))
    o_spec = pl.BlockSpec((tm, tn), lambda i, j, k: (i, j))
    has_add = add is not None

    def body(*refs):
        a_ref, b_ref = refs[0], refs[1]
        add_ref = refs[2] if has_add else None
        o_ref = refs[3] if has_add else refs[2]
        p = _bdot(a_ref[...], b_ref[...], dims)

        def finish(acc):
            if has_add:
                acc = acc + add_ref[...]
            o_ref[...] = acc.astype(out_dtype)

        if nk == 1:
            finish(p)
        else:
            acc_ref = refs[-1]
            k = pl.program_id(2)

            @pl.when(k == 0)
            def _():
                acc_ref[...] = p

            @pl.when(k > 0)
            def _():
                acc_ref[...] += p

            @pl.when(k == nk - 1)
            def _():
                finish(acc_ref[...])

    in_specs = [a_spec, b_spec] + ([o_spec] if has_add else [])
    args = (a, b) + ((add,) if has_add else ())
    return pl.pallas_call(
        body, name=name, grid=(M // tm, N // tn, nk),
        in_specs=in_specs, out_specs=o_spec,
        out_shape=jax.ShapeDtypeStruct((M, N), out_dtype),
        scratch_shapes=[pltpu.VMEM((tm, tn), F32)] if nk > 1 else [],
        compiler_params=_params("parallel", "parallel", "arbitrary"),
    )(*args)


def _rmsnorm_fwd(x, g, name):
    T, D = x.shape
    tt = _tile(T, 512, SUBLANE)

    def body(x_ref, g_ref, o_ref):
        xv = x_ref[...]
        r = lax.rsqrt(jnp.mean(xv * xv, axis=-1, keepdims=True) + RMS_EPS)
        o_ref[...] = (xv * r * g_ref[...]).astype(BF16)

    return pl.pallas_call(
        body, name=name, grid=(T // tt,),
        in_specs=[pl.BlockSpec((tt, D), lambda i: (i, 0)), pl.BlockSpec((1, D), lambda i: (0, 0))],
        out_specs=pl.BlockSpec((tt, D), lambda i: (i, 0)),
        out_shape=jax.ShapeDtypeStruct((T, D), BF16),
        compiler_params=_params("parallel"),
    )(x, g)


def _rmsnorm_bwd(dh, x, g, dx_res, name):
    T, D = x.shape
    tt = _tile(T, 256, SUBLANE)

    def body(dh_ref, x_ref, g_ref, res_ref, dx_ref, dg_ref):
        xv, dhv = x_ref[...], dh_ref[...]
        r = lax.rsqrt(jnp.mean(xv * xv, axis=-1, keepdims=True) + RMS_EPS)
        xh = xv * r
        dxh = dhv * g_ref[...]
        m = jnp.mean(dxh * xh, axis=-1, keepdims=True)
        dx_ref[...] = res_ref[...] + r * (dxh - xh * m)
        part = jnp.sum(dhv * xh, axis=0, keepdims=True)

        @pl.when(pl.program_id(0) == 0)
        def _():
            dg_ref[...] = part

        @pl.when(pl.program_id(0) > 0)
        def _():
            dg_ref[...] += part

    row = pl.BlockSpec((tt, D), lambda i: (i, 0))
    vec = pl.BlockSpec((1, D), lambda i: (0, 0))
    return pl.pallas_call(
        body, name=name, grid=(T // tt,),
        in_specs=[row, row, vec, row], out_specs=[row, vec],
        out_shape=[jax.ShapeDtypeStruct((T, D), F32), jax.ShapeDtypeStruct((1, D), F32)],
        compiler_params=_params("arbitrary"),
    )(dh, x, g, dx_res)


def _loss_head(y, target, name="loss_head"):
    T, D = y.shape
    tt = _tile(T, 512, SUBLANE)

    def body(y_ref, t_ref, dy_ref, l_ref):
        e = y_ref[...] - t_ref[...]
        dy_ref[...] = e * (1.0 / D)
        s = jnp.sum(jnp.sum(e * e, axis=1, keepdims=True), axis=0, keepdims=True) * (0.5 / D)
        s = jnp.broadcast_to(s, (1, LANE))

        @pl.when(pl.program_id(0) == 0)
        def _():
            l_ref[...] = s

        @pl.when(pl.program_id(0) > 0)
        def _():
            l_ref[...] += s

    row = pl.BlockSpec((tt, D), lambda i: (i, 0))
    return pl.pallas_call(
        body, name=name, grid=(T // tt,),
        in_specs=[row, row], out_specs=[row, pl.BlockSpec((1, LANE), lambda i: (0, 0))],
        out_shape=[jax.ShapeDtypeStruct((T, D), F32), jax.ShapeDtypeStruct((1, LANE), F32)],
        compiler_params=_params("arbitrary"),
    )(y, target)


def _down(x, k):
    return pltpu.roll(x, k, 0) if k else x


def _up(x, k):
    return pltpu.roll(x, x.shape[0] - k, 0) if k else x


def _sc_fwd(proj, conv_w, name):
    T = proj.shape[0]
    tt = _tile(T, WIDE_ROW_TILE, SUBLANE)
    B = SC_BLK

    def body(p_ref, ph_ref, w_ref, o_ref):
        keep = (pl.program_id(0) > 0).astype(F32)
        for j in range(SC_NBLK):
            cb, cc, cu, cg = (slice(k * SC_W + j * B, k * SC_W + (j + 1) * B) for k in range(4))
            cw = slice(j * B, (j + 1) * B)
            z = jnp.concatenate([ph_ref[:, cc] * ph_ref[:, cu] * keep, p_ref[:, cc] * p_ref[:, cu]], axis=0)
            cz = (w_ref[2:3, cw] * z + w_ref[1:2, cw] * _down(z, 1) + w_ref[0:1, cw] * _down(z, 2))[HALO:]
            gate = p_ref[:, cg]
            o_ref[:, cw] = (p_ref[:, cb] * cz * (gate * _sigmoid(gate))).astype(BF16)

    return pl.pallas_call(
        body, name=name, grid=(T // tt,),
        in_specs=[pl.BlockSpec((tt, 4 * SC_W), lambda i: (i, 0)),
                  pl.BlockSpec((HALO, 4 * SC_W), lambda i: (jnp.maximum(i * (tt // HALO) - 1, 0), 0)),
                  pl.BlockSpec((SC_CONV, SC_W), lambda i: (0, 0))],
        out_specs=pl.BlockSpec((tt, SC_W), lambda i: (i, 0)),
        out_shape=jax.ShapeDtypeStruct((T, SC_W), BF16),
        compiler_params=_params("parallel"),
    )(proj, proj, conv_w)


def _sc_bwd(dyg, proj, conv_w, name):
    T = proj.shape[0]
    tt = _tile(T, WIDE_ROW_TILE, SUBLANE)
    nt = T // tt
    B = SC_BLK
    hb = tt // HALO

    def body(d_ref, dn_ref, p_ref, pp_ref, pn_ref, w_ref, o_ref, dw_ref):
        i = pl.program_id(0)
        keep_p = (i > 0).astype(F32)
        keep_n = (i < nt - 1).astype(F32)
        main = slice(HALO, HALO + tt)
        parts = []
        for j in range(SC_NBLK):
            cw = slice(j * B, (j + 1) * B)

            def ext(k):
                s = slice(k * SC_W + j * B, k * SC_W + (j + 1) * B)
                return s, jnp.concatenate([pp_ref[:, s] * keep_p, p_ref[:, s], pn_ref[:, s]], axis=0)

            (sb, b), (sc, c), (su, u), (sg_, gate) = ext(0), ext(1), ext(2), ext(3)
            dyg_e = jnp.concatenate([jnp.zeros((HALO, B), F32), d_ref[:, cw], dn_ref[:, cw] * keep_n], axis=0)
            w0, w1, w2 = w_ref[0:1, cw], w_ref[1:2, cw], w_ref[2:3, cw]
            z = c * u
            z1, z2 = _down(z, 1), _down(z, 2)
            cz = w2 * z + w1 * z1 + w0 * z2
            sg, dsg = _silu_and_grad(gate)
            dy = dyg_e * sg
            dcz = dy * b
            dz = w2 * dcz + w1 * _up(dcz, 1) + w0 * _up(dcz, 2)
            o_ref[:, sb] = (dy * cz)[main].astype(BF16)
            o_ref[:, sc] = (dz * u)[main].astype(BF16)
            o_ref[:, su] = (dz * c)[main].astype(BF16)
            o_ref[:, sg_] = (dyg_e * (b * cz) * dsg)[main].astype(BF16)
            dcm = dcz[main]
            parts.append(jnp.concatenate([jnp.sum(dcm * z2[main], axis=0, keepdims=True),
                                          jnp.sum(dcm * z1[main], axis=0, keepdims=True),
                                          jnp.sum(dcm * z[main], axis=0, keepdims=True)], axis=0))
        part = jnp.concatenate(parts, axis=1)

        @pl.when(i == 0)
        def _():
            dw_ref[...] = part

        @pl.when(i > 0)
        def _():
            dw_ref[...] += part

    nxt = lambda i: (jnp.minimum((i + 1) * hb, nt * hb - 1), 0)
    return pl.pallas_call(
        body, name=name, grid=(nt,),
        in_specs=[pl.BlockSpec((tt, SC_W), lambda i: (i, 0)),
                  pl.BlockSpec((HALO, SC_W), nxt),
                  pl.BlockSpec((tt, 4 * SC_W), lambda i: (i, 0)),
                  pl.BlockSpec((HALO, 4 * SC_W), lambda i: (jnp.maximum(i * hb - 1, 0), 0)),
                  pl.BlockSpec((HALO, 4 * SC_W), nxt),
                  pl.BlockSpec((SC_CONV, SC_W), lambda i: (0, 0))],
        out_specs=[pl.BlockSpec((tt, 4 * SC_W), lambda i: (i, 0)), pl.BlockSpec((SC_CONV, SC_W), lambda i: (0, 0))],
        out_shape=[jax.ShapeDtypeStruct((T, 4 * SC_W), BF16), jax.ShapeDtypeStruct((SC_CONV, SC_W), F32)],
        compiler_params=_params("arbitrary"),
    )(dyg, dyg, proj, proj, proj, conv_w)


def _split3_dot(x, m):
    hi = x.astype(BF16)
    r1 = x - hi.astype(F32)
    mid = r1.astype(BF16)
    lo = (r1 - mid.astype(F32)).astype(BF16)
    return _dot(hi, m) + _dot(mid, m) + _dot(lo, m)


def _split2_dot(x, m):
    hi = x.astype(BF16)
    lo = (x - hi.astype(F32)).astype(BF16)
    return _dot(hi, m) + _dot(lo, m)


def _head_mean_matrix():
    r, c = _iota2((LANE, LANE), 0), _iota2((LANE, LANE), 1)
    return jnp.where((r // SB_DH) == (c // SB_DH), 1.0 / SB_DH, 0.0).astype(BF16)


def _sb_prep(proj, qg2, kg2, name):
    T = proj.shape[0]
    tt = _tile(T, WIDE_ROW_TILE, SUBLANE)

    def body(p_ref, qg_ref, kg_ref, q_ref, k_ref, v_ref):
        bd = _head_mean_matrix()

        def norm(x, g, scale):
            r = lax.rsqrt(_split3_dot(x * x, bd) + RMS_EPS)
            return (x * r * g * scale).astype(BF16)

        v_ref[...] = p_ref[:, 2 * SB_W:3 * SB_W].astype(BF16)
        for p in range(SB_PAIRS):
            cols = slice(p * LANE, (p + 1) * LANE)
            q_ref[:, cols] = norm(p_ref[:, cols], qg_ref[...], SB_DH ** -0.5)
            k_ref[:, cols] = norm(p_ref[:, SB_W + p * LANE:SB_W + (p + 1) * LANE], kg_ref[...], 1.0)

    blk = pl.BlockSpec((tt, SB_W), lambda i: (i, 0))
    vec = pl.BlockSpec((1, LANE), lambda i: (0, 0))
    return pl.pallas_call(
        body, name=name, grid=(T // tt,),
        in_specs=[pl.BlockSpec((tt, 4 * SB_W), lambda i: (i, 0)), vec, vec],
        out_specs=[blk, blk, blk],
        out_shape=[jax.ShapeDtypeStruct((T, SB_W), BF16)] * 3,
        compiler_params=_params("parallel"),
    )(proj, qg2, kg2)


def _sb_prep_bwd(proj, dqn, dkn, dv, dgate, qg2, kg2, name):
    T = proj.shape[0]
    tt = _tile(T, WIDE_ROW_TILE, SUBLANE)

    def body(p_ref, dq_ref, dk_ref, dv_ref, dg_ref, qg_ref, kg_ref, o_ref, dqg_ref, dkg_ref):
        i = pl.program_id(0)
        bd = _head_mean_matrix()

        def norm_bwd(x, g, dy):
            r = lax.rsqrt(_split3_dot(x * x, bd) + RMS_EPS)
            xh = x * r
            dxh = dy * g
            m = _split3_dot(dxh * xh, bd)
            return r * (dxh - xh * m), jnp.sum(dy * xh, axis=0, keepdims=True)

        o_ref[:, 2 * SB_W:3 * SB_W] = dv_ref[...].astype(BF16)
        o_ref[:, 3 * SB_W:4 * SB_W] = dg_ref[...].astype(BF16)
        pq = pk = jnp.zeros((1, LANE), F32)
        for p in range(SB_PAIRS):
            cols, kcols = slice(p * LANE, (p + 1) * LANE), slice(SB_W + p * LANE, SB_W + (p + 1) * LANE)
            dxq, sq = norm_bwd(p_ref[:, cols], qg_ref[...], dq_ref[:, cols])
            dxk, sk = norm_bwd(p_ref[:, kcols], kg_ref[...], dk_ref[:, cols])
            o_ref[:, cols] = dxq.astype(BF16)
            o_ref[:, kcols] = dxk.astype(BF16)
            pq, pk = pq + sq, pk + sk

        @pl.when(i == 0)
        def _():
            dqg_ref[...] = pq
            dkg_ref[...] = pk

        @pl.when(i > 0)
        def _():
            dqg_ref[...] += pq
            dkg_ref[...] += pk

    blk = pl.BlockSpec((tt, SB_W), lambda i: (i, 0))
    vec = pl.BlockSpec((1, LANE), lambda i: (0, 0))
    wide = pl.BlockSpec((tt, 4 * SB_W), lambda i: (i, 0))
    return pl.pallas_call(
        body, name=name, grid=(T // tt,),
        in_specs=[wide, blk, blk, blk, blk, vec, vec],
        out_specs=[wide, vec, vec],
        out_shape=[jax.ShapeDtypeStruct((T, 4 * SB_W), BF16)] + [jax.ShapeDtypeStruct((1, LANE), F32)] * 2,
        compiler_params=_params("arbitrary"),
    )(proj, dqn, dkn, dv, dgate, qg2, kg2)


def _fold_heads(part, name):
    def body(p_ref, o_ref):
        r, c = _iota2((LANE, SB_DH), 0), _iota2((LANE, SB_DH), 1)
        fold = jnp.where((r % SB_DH) == c, 1.0, 0.0).astype(F32)
        o_ref[...] = jnp.sum(_hdot(p_ref[...], fold), axis=0, keepdims=True)

    return pl.pallas_call(body, name=name, out_shape=jax.ShapeDtypeStruct((1, SB_DH), F32))(part)


def _sb_masks():
    lane = _iota2((1, LANE), 1)
    return lane < SB_DH


def _sb_attn_fwd(qn, kn, vb, proj, name):
    T = qn.shape[0]
    tq, tk = _tile(T, SB_TQ, SUBLANE), SB_TK
    assert tq % tk == 0

    def body(q_ref, k_ref, v_ref, g_ref, o_ref, og_ref, lt_ref, done_ref):
        i = pl.program_id(1)
        ma = _sb_masks()
        q2 = q_ref[...]
        zero = jnp.zeros_like(q2)
        qs = (jnp.where(ma, q2, zero), jnp.where(ma, zero, q2))
        upper = (_iota2((tk, tk), 0) > _iota2((tk, tk), 1)).astype(BF16)
        qpos = i * tq + _iota2((tq, tk), 0)
        nb = tq // tk

        def trip(kb_top, masked, carry):
            acc, la, lb = carry
            chains = [(b, h) for b in range(nb) for h in range(2)]
            k2s, vss, masks = [], [], []
            for b in range(nb):
                kb = kb_top - b
                rows = pl.ds(pl.multiple_of(kb * tk, tk), tk)
                k2s.append(k_ref[rows, :])
                v2 = v_ref[rows, :]
                zv = jnp.zeros_like(v2)
                vss.append((jnp.where(ma, v2, zv), jnp.where(ma, zv, v2)))
                masks.append((kb * tk + _iota2((tq, tk), 1)) < qpos if masked else None)
            zs = [_dot(qs[h], k2s[b], NT) for b, h in chains]
            ts = [jnp.log(1.0 + jnp.exp(-jnp.abs(z))) for z in zs]
            ls = [-(jnp.maximum(z, 0.0) + t) for z, t in zip(zs, ts)]
            if masked:
                ls = [jnp.where(masks[b], l, 0.0) for (b, h), l in zip(chains, ls)]
            cums = [_split2_dot(l, upper) for l in ls]
            sums = [jnp.sum(l, axis=1, keepdims=True) for l in ls]
            offs, tot = {}, [la, lb]
            for b in range(nb):
                for h in range(2):
                    offs[(b, h)] = tot[h]
                    tot[h] = tot[h] + sums[chains.index((b, h))]
            ws = [jnp.exp(jnp.minimum(z, 0.0) - t + c + offs[ch]) for ch, z, t, c in zip(chains, zs, ts, cums)]
            if masked:
                ws = [jnp.where(masks[b], w, 0.0) for (b, h), w in zip(chains, ws)]
            for (b, h), w in zip(chains, ws):
                acc = acc + _dot(w.astype(BF16), vss[b][h])
            return acc, tot[0], tot[1]

        def largest(la, lb):
            return jnp.max(jnp.maximum(la, lb))

        z1 = jnp.zeros((tq, 1), F32)
        acc, la, lb = trip((i + 1) * nb - 1, True, (jnp.zeros((tq, LANE), F32), z1, z1))

        def live(c):
            return (c[0] < i) & (c[4] > SB_DEAD)

        def more(c):
            j, acc, la, lb, _ = c
            acc, la, lb = trip((i - j) * nb - 1, False, (acc, la, lb))
            return j + 1, acc, la, lb, largest(la, lb)

        done, acc, la, lb, _ = lax.while_loop(live, more, (jnp.int32(0), acc, la, lb, largest(la, lb)))
        gate = g_ref[...]
        o_ref[...] = acc
        og_ref[...] = (acc * (gate * _sigmoid(gate))).astype(BF16)
        lt_ref[...] = jnp.where(_iota2((tq, 2), 1) == 0, la, lb)
        done_ref[...] = jnp.full((SUBLANE, LANE), done, F32)

    nq = T // tq
    qblk = pl.BlockSpec((tq, LANE), lambda p, i: (i, p))
    full = pl.BlockSpec((T, LANE), lambda p, i: (0, p))
    return pl.pallas_call(
        body, name=name, grid=(SB_PAIRS, nq),
        in_specs=[qblk, full, full, pl.BlockSpec((tq, LANE), lambda p, i: (i, 3 * SB_PAIRS + p))],
        out_specs=[qblk, qblk, pl.BlockSpec((None, tq, 2), lambda p, i: (p, i, 0)),
                   pl.BlockSpec((None, None, SUBLANE, LANE), lambda p, i: (p, i, 0, 0))],
        out_shape=[jax.ShapeDtypeStruct((T, SB_W), F32), jax.ShapeDtypeStruct((T, SB_W), BF16),
                   jax.ShapeDtypeStruct((SB_PAIRS, T, 2), F32), jax.ShapeDtypeStruct((SB_PAIRS, nq, SUBLANE, LANE), F32)],
        compiler_params=_params("parallel", "parallel"),
    )(qn, kn, vb, proj)


def _sb_attn_bwd(qn, kn, vb, dog, o, ltot, done, proj, name):
    T = qn.shape[0]
    tq, tk = _tile(T, SB_TQ, SUBLANE), SB_TK

    def body(q_ref, k_ref, v_ref, dog_ref, o_ref, lt_ref, done_ref, g_ref, dq_ref, dk_ref, dv_ref, dgate_ref):
        i = pl.program_id(1)
        first_trip = i - jnp.max(done_ref[...]).astype(jnp.int32)

        @pl.when(i == 0)
        def _():
            dk_ref[...] = jnp.zeros_like(dk_ref)
            dv_ref[...] = jnp.zeros_like(dv_ref)

        ma = _sb_masks()
        gate, o2, dog2 = g_ref[...], o_ref[...], dog_ref[...]
        sg, dsg = _silu_and_grad(gate)
        do2 = dog2 * sg
        dgate_ref[...] = dog2 * o2 * dsg
        lt = lt_ref[...]
        first = _iota2((tq, 2), 1) == 0
        ltots = (jnp.sum(jnp.where(first, lt, 0.0), axis=1, keepdims=True),
                 jnp.sum(jnp.where(first, 0.0, lt), axis=1, keepdims=True))
        q2 = q_ref[...]
        zq = jnp.zeros_like(q2)
        qs = (jnp.where(ma, q2, zq), jnp.where(ma, zq, q2))
        dob = do2.astype(BF16)
        dos = (jnp.where(ma, dob, zq), jnp.where(ma, zq, dob))
        upto = (_iota2((tk, tk), 0) <= _iota2((tk, tk), 1)).astype(BF16)
        before = (_iota2((tk, tk), 0) < _iota2((tk, tk), 1)).astype(BF16)
        qpos = i * tq + _iota2((tq, tk), 0)
        nb = tq // tk

        def trip(kb_bot, masked, carry):
            dq, la, lb, ea, eb = carry
            chains = [(b, h) for b in range(nb) for h in range(2)]
            rows, k2s, v2s, kss, masks = [], [], [], [], []
            for b in range(nb):
                kb = kb_bot + b
                rows.append(pl.ds(pl.multiple_of(kb * tk, tk), tk))
                k2 = k_ref[rows[b], :]
                zk = jnp.zeros_like(k2)
                k2s.append(k2)
                v2s.append(v_ref[rows[b], :])
                kss.append((jnp.where(ma, k2, zk), jnp.where(ma, zk, k2)))
                masks.append((kb * tk + _iota2((tq, tk), 1)) < qpos if masked else None)

            def keep(vals):
                return [jnp.where(masks[b], x, 0.0) for (b, h), x in zip(chains, vals)] if masked else vals

            zs = [_dot(qs[h], k2s[b], NT) for b, h in chains]
            dws = [_dot(dos[h], v2s[b], NT) for b, h in chains]
            ts = [jnp.log(1.0 + jnp.exp(-jnp.abs(z))) for z in zs]
            ls = keep([-(jnp.maximum(z, 0.0) + t) for z, t in zip(zs, ts)])
            lps = [jnp.minimum(z, 0.0) - t for z, t in zip(zs, ts)]
            cums = [_split3_dot(l, upto) for l in ls]
            lsums = [jnp.sum(l, axis=1, keepdims=True) for l in ls]
            offs, tot = {}, [la, lb]
            for b in range(nb):
                for h in range(2):
                    offs[(b, h)] = tot[h]
                    tot[h] = tot[h] + lsums[chains.index((b, h))]
            ws = keep([jnp.exp(lp + (ltots[h] - (offs[(b, h)] + c))) for (b, h), lp, c in zip(chains, lps, cums)])
            es = [dw * w for dw, w in zip(dws, ws)]
            ecums = [_split2_dot(e, before) for e in es]
            esums = [jnp.sum(e, axis=1, keepdims=True) for e in es]
            eoffs, etot = {}, [ea, eb]
            for b in range(nb):
                for h in range(2):
                    eoffs[(b, h)] = etot[h]
                    etot[h] = etot[h] + esums[chains.index((b, h))]
            dzs = keep([e - jnp.exp(lp) * (e + eoffs[ch] + ec) for ch, e, lp, ec in zip(chains, es, lps, ecums)])
            dzs = [dz.astype(BF16) for dz in dzs]
            wbs = [w.astype(BF16) for w in ws]
            for (b, h), dz in zip(chains, dzs):
                dq = dq + _dot(dz, kss[b][h])
            for b in range(nb):
                ia, ib = chains.index((b, 0)), chains.index((b, 1))
                dk_ref[rows[b], :] += _dot(dzs[ia], qs[0], TN) + _dot(dzs[ib], qs[1], TN)
                dv_ref[rows[b], :] += _dot(wbs[ia], dos[0], TN) + _dot(wbs[ib], dos[1], TN)
            return dq, tot[0], tot[1], etot[0], etot[1]

        z1 = jnp.zeros((tq, 1), F32)
        carry = lax.fori_loop(first_trip, i, lambda j, c: trip(j * nb, False, c),
                              (jnp.zeros((tq, LANE), F32), z1, z1, z1, z1))
        dq = trip(i * nb, True, carry)[0]
        dq_ref[...] = dq * (SB_DH ** -0.5)

    qblk = pl.BlockSpec((tq, LANE), lambda p, i: (i, p))
    full = pl.BlockSpec((T, LANE), lambda p, i: (0, p))
    return pl.pallas_call(
        body, name=name, grid=(SB_PAIRS, T // tq),
        in_specs=[qblk, full, full, qblk, qblk, pl.BlockSpec((None, tq, 2), lambda p, i: (p, i, 0)),
                  pl.BlockSpec((None, None, SUBLANE, LANE), lambda p, i: (p, i, 0, 0)),
                  pl.BlockSpec((tq, LANE), lambda p, i: (i, 3 * SB_PAIRS + p))],
        out_specs=[qblk, full, full, qblk],
        out_shape=[jax.ShapeDtypeStruct((T, SB_W), F32)] * 4,
        compiler_params=_params("parallel", "arbitrary"),
    )(qn, kn, vb, dog, o, ltot, done, proj)


def _dn_conv(ext, w_ref, cw):
    return (w_ref[3:4, cw] * ext + w_ref[2:3, cw] * _down(ext, 1) + w_ref[1:2, cw] * _down(ext, 2)
            + w_ref[0:1, cw] * _down(ext, 3))


def _dn_prep(pqkv, conv_w, name):
    T, W = pqkv.shape
    tt = _tile(T, WIDE_ROW_TILE, SUBLANE)
    B = DN_PREP_BLK
    nq, nqk = DN_QK_W // B, 2 * DN_QK_W // B

    def body(p_ref, ph_ref, w_ref, o_ref):
        keep = (pl.program_id(0) > 0).astype(F32)
        for cb in range(W // B):
            cw = slice(cb * B, (cb + 1) * B)
            ext = jnp.concatenate([ph_ref[:, cw] * keep, p_ref[:, cw]], axis=0)
            c = _dn_conv(ext, w_ref, cw)[HALO:]
            a = c * _sigmoid(c)
            if cb >= nqk:
                o_ref[:, cw] = a
                continue
            scale = DN_DK ** -0.5 if cb < nq else 1.0
            for hh in range(B // DN_DK):
                ah = a[:, hh * DN_DK:(hh + 1) * DN_DK]
                r = lax.rsqrt(jnp.sum(ah * ah, axis=-1, keepdims=True) + L2_EPS)
                o_ref[:, cb * B + hh * DN_DK:cb * B + (hh + 1) * DN_DK] = ah * (r * scale)

    return pl.pallas_call(
        body, name=name, grid=(T // tt,),
        in_specs=[pl.BlockSpec((tt, W), lambda i: (i, 0)),
                  pl.BlockSpec((HALO, W), lambda i: (jnp.maximum(i * (tt // HALO) - 1, 0), 0)),
                  pl.BlockSpec((DN_CONV, W), lambda i: (0, 0))],
        out_specs=pl.BlockSpec((tt, W), lambda i: (i, 0)),
        out_shape=jax.ShapeDtypeStruct((T, W), F32),
        compiler_params=_params("parallel"),
    )(pqkv, pqkv, conv_w)


def _dn_prep_bwd(pqkv, conv_w, dact, name):
    T, W = pqkv.shape
    tt = _tile(T, WIDE_ROW_TILE, SUBLANE)
    nt = T // tt
    hb = tt // HALO
    B = DN_PREP_BLK
    nq, nqk = DN_QK_W // B, 2 * DN_QK_W // B

    def body(p_ref, pp_ref, pn_ref, w_ref, d_ref, dn_ref, o_ref, dw_ref):
        i = pl.program_id(0)
        keep_p = (i > 0).astype(F32)
        keep_n = (i < nt - 1).astype(F32)
        main = slice(HALO, HALO + tt)
        parts = []
        for cb in range(W // B):
            cw = slice(cb * B, (cb + 1) * B)
            ext = jnp.concatenate([pp_ref[:, cw] * keep_p, p_ref[:, cw], pn_ref[:, cw]], axis=0)
            c = _dn_conv(ext, w_ref, cw)
            s = _sigmoid(c)
            da_dc = s * (1.0 + c * (1.0 - s))
            d_up = jnp.concatenate([jnp.zeros((HALO, B), F32), d_ref[:, cw], dn_ref[:, cw] * keep_n], axis=0)
            if cb < nqk:
                a = c * s
                scale = DN_DK ** -0.5 if cb < nq else 1.0
                normed = []
                for hh in range(B // DN_DK):
                    cols = slice(hh * DN_DK, (hh + 1) * DN_DK)
                    ah = a[:, cols]
                    r = lax.rsqrt(jnp.sum(ah * ah, axis=-1, keepdims=True) + L2_EPS)
                    y = ah * r
                    dy = d_up[:, cols] * scale
                    normed.append(r * (dy - y * jnp.sum(dy * y, axis=-1, keepdims=True)))
                d_up = jnp.concatenate(normed, axis=1)
            dc = d_up * da_dc
            dp = (w_ref[3:4, cw] * dc + w_ref[2:3, cw] * _up(dc, 1) + w_ref[1:2, cw] * _up(dc, 2)
                  + w_ref[0:1, cw] * _up(dc, 3))
            o_ref[:, cw] = dp[main].astype(BF16)
            dcm = dc[main]
            parts.append(jnp.concatenate([jnp.sum(dcm * _down(ext, 3 - k)[main], axis=0, keepdims=True)
                                          for k in range(DN_CONV)], axis=0))
        part = jnp.concatenate(parts, axis=1)

        @pl.when(i == 0)
        def _():
            dw_ref[...] = part

        @pl.when(i > 0)
        def _():
            dw_ref[...] += part

    main_spec = pl.BlockSpec((tt, W), lambda i: (i, 0))
    prev_spec = pl.BlockSpec((HALO, W), lambda i: (jnp.maximum(i * hb - 1, 0), 0))
    next_spec = pl.BlockSpec((HALO, W), lambda i: (jnp.minimum((i + 1) * hb, nt * hb - 1), 0))
    w_spec = pl.BlockSpec((DN_CONV, W), lambda i: (0, 0))
    return pl.pallas_call(
        body, name=name, grid=(nt,),
        in_specs=[main_spec, prev_spec, next_spec, w_spec, main_spec, next_spec],
        out_specs=[main_spec, w_spec],
        out_shape=[jax.ShapeDtypeStruct((T, W), BF16), jax.ShapeDtypeStruct((DN_CONV, W), F32)],
        compiler_params=_params("arbitrary"),
    )(pqkv, pqkv, pqkv, conv_w, dact, dact)


def _dn_gates(a_in, b_in, a_log, dt_bias, name):
    T, H = a_in.shape
    C = DN_CHUNK

    def body(a_ref, b_ref, al_ref, dt_ref, g_ref, beta_ref):
        beta_ref[...] = _sigmoid(b_ref[...])
        g_ref[...] = -jnp.exp(al_ref[...]) * _softplus(a_ref[...] + dt_ref[...])
        tri = (_iota2((C, C), 0) >= _iota2((C, C), 1)).astype(F32)

        def chunk(n, carry):
            rows = pl.ds(pl.multiple_of(n * C, C), C)
            g_ref[rows, :] = _hdot(tri, g_ref[rows, :])
            return carry

        lax.fori_loop(0, T // C, chunk, 0)

    return pl.pallas_call(body, name=name, out_shape=[jax.ShapeDtypeStruct((T, H), F32)] * 2)(a_in, b_in, a_log, dt_bias)


def _dn_gates_bwd(dg, dbeta, a_in, b_in, a_log, dt_bias, name):
    T, H = a_in.shape
    C = DN_CHUNK

    def body(dg_ref, db_ref, a_ref, b_ref, al_ref, dt_ref, da_ref, dbi_ref, dal_ref, ddt_ref):
        tri_t = (_iota2((C, C), 0) <= _iota2((C, C), 1)).astype(F32)

        def chunk(n, carry):
            rows = pl.ds(pl.multiple_of(n * C, C), C)
            da_ref[rows, :] = _hdot(tri_t, dg_ref[rows, :])
            return carry

        lax.fori_loop(0, T // C, chunk, 0)
        dla = da_ref[...]
        x = a_ref[...] + dt_ref[...]
        ea = jnp.exp(al_ref[...])
        da = dla * (-ea) * _sigmoid(x)
        da_ref[...] = da
        dal_ref[...] = jnp.sum(dla * (-ea * _softplus(x)), axis=0, keepdims=True)
        ddt_ref[...] = jnp.sum(da, axis=0, keepdims=True)
        beta = _sigmoid(b_ref[...])
        dbi_ref[...] = db_ref[...] * beta * (1.0 - beta)

    return pl.pallas_call(
        body, name=name,
        out_shape=[jax.ShapeDtypeStruct((T, H), F32)] * 2 + [jax.ShapeDtypeStruct((1, H), F32)] * 2,
    )(dg, dbeta, a_in, b_in, a_log, dt_bias)


def _dn_post(o_raw, pgate, gn, name):
    T = o_raw.shape[0]
    tt = _tile(T, WIDE_ROW_TILE, SUBLANE)

    def body(o_ref, g_ref, gn_ref, out_ref):
        for hh in range(DN_HEADS):
            cols = slice(hh * DN_DV, (hh + 1) * DN_DV)
            o, gate = o_ref[:, cols], g_ref[:, cols]
            r = lax.rsqrt(jnp.mean(o * o, axis=-1, keepdims=True) + RMS_EPS)
            out_ref[:, cols] = (o * r * gn_ref[...] * (gate * _sigmoid(gate))).astype(BF16)

    blk = pl.BlockSpec((tt, DN_V_W), lambda i: (i, 0))
    return pl.pallas_call(
        body, name=name, grid=(T // tt,),
        in_specs=[blk, blk, pl.BlockSpec((1, DN_DV), lambda i: (0, 0))], out_specs=blk,
        out_shape=jax.ShapeDtypeStruct((T, DN_V_W), BF16),
        compiler_params=_params("parallel"),
    )(o_raw, pgate, gn)


def _dn_post_bwd(dog, o_raw, pgate, gn, name):
    T = o_raw.shape[0]
    tt = _tile(T, WIDE_ROW_TILE, SUBLANE)

    def body(d_ref, o_ref, g_ref, gn_ref, do_ref, dgate_ref, dgn_ref):
        gn_v = gn_ref[...]
        part = jnp.zeros((1, DN_DV), F32)
        for hh in range(DN_HEADS):
            cols = slice(hh * DN_DV, (hh + 1) * DN_DV)
            d, o, gate = d_ref[:, cols], o_ref[:, cols], g_ref[:, cols]
            sg, dsg = _silu_and_grad(gate)
            r = lax.rsqrt(jnp.mean(o * o, axis=-1, keepdims=True) + RMS_EPS)
            n = o * r
            dy = d * sg
            dgate_ref[:, cols] = (d * (n * gn_v) * dsg).astype(BF16)
            dn = dy * gn_v
            do_ref[:, cols] = r * (dn - n * jnp.mean(dn * n, axis=-1, keepdims=True))
            part = part + jnp.sum(dy * n, axis=0, keepdims=True)

        @pl.when(pl.program_id(0) == 0)
        def _():
            dgn_ref[...] = part

        @pl.when(pl.program_id(0) > 0)
        def _():
            dgn_ref[...] += part

    blk = pl.BlockSpec((tt, DN_V_W), lambda i: (i, 0))
    vec = pl.BlockSpec((1, DN_DV), lambda i: (0, 0))
    return pl.pallas_call(
        body, name=name, grid=(T // tt,),
        in_specs=[blk, blk, blk, vec], out_specs=[blk, blk, vec],
        out_shape=[jax.ShapeDtypeStruct((T, DN_V_W), F32), jax.ShapeDtypeStruct((T, DN_V_W), BF16),
                   jax.ShapeDtypeStruct((1, DN_DV), F32)],
        compiler_params=_params("arbitrary"),
    )(dog, o_raw, pgate, gn)


def _dn_chunk_terms(q, k, gc, bc):
    C = DN_CHUNK
    r, c = _iota2((C, C), 0), _iota2((C, C), 1)
    lower, strict, eye = r >= c, r > c, r == c
    grow = jnp.sum(jnp.where(eye, gc, 0.0), axis=0, keepdims=True)
    decay = jnp.where(lower, jnp.exp(jnp.where(lower, gc - grow, 0.0)), 0.0)
    last = _iota2((C, 1), 0) == C - 1
    gl = jnp.sum(jnp.where(last, gc, 0.0), axis=0, keepdims=True)
    eg = jnp.exp(gc)
    egl = jnp.exp(gl - gc)
    kb = k * bc
    lmat = jnp.where(strict, _bdot(kb, k, NT) * decay, 0.0)
    aqk = jnp.where(lower, _bdot(q, k, NT) * decay, 0.0)
    return dict(lower=lower, strict=strict, eye=eye, last=last, decay=decay, gl=gl, eg=eg, egl=egl, kb=kb,
                lmat=lmat, aqk=aqk, qd=q * eg, kd=k * egl)


def _split(x):
    hi = x.astype(BF16)
    return hi, (x - hi.astype(F32)).astype(BF16)


def _x3dot(a, b, dims=NN):
    ah, al = a if isinstance(a, tuple) else _split(a)
    bh, bl = b if isinstance(b, tuple) else _split(b)
    return _dot(ah, bh, dims) + (_dot(ah, bl, dims) + _dot(al, bh, dims))


def _interleave(gens):
    for _ in itertools.zip_longest(*gens):
        pass


def _unit_lower_inverse_steps(lmat, eye, out):
    ident = jnp.where(eye, 1.0, 0.0).astype(F32)
    m = -lmat
    inv = ident + m
    for _ in range(int(math.log2(DN_CHUNK)) - 1):
        ms = _split(m)
        m = _x3dot(ms, ms)
        yield
        inv = inv + _x3dot(inv, m)
        yield
    out["tm"] = inv


def _dn_chunk_fwd(act, g, beta, name):
    T = act.shape[0]
    C, H = DN_CHUNK, DN_HEADS
    N = T // C

    def body(a_ref, g_ref, b_ref, o_ref, s_out, t_out, vn_out, u_out, w_out, s_scr):
        n = pl.program_id(0)

        @pl.when(n == 0)
        def _():
            s_scr[...] = jnp.zeros_like(s_scr)

        head_lane = _iota2((C, H), 1)

        def head(hh):
            qs, vs = slice(hh * DN_DK, (hh + 1) * DN_DK), slice(hh * DN_DV, (hh + 1) * DN_DV)
            q, k, v = a_ref[:, qs], a_ref[:, DN_QK_W + hh * DN_DK:DN_QK_W + (hh + 1) * DN_DK], \
                a_ref[:, 2 * DN_QK_W + hh * DN_DV:2 * DN_QK_W + (hh + 1) * DN_DV]
            gc = jnp.sum(jnp.where(head_lane == hh, g_ref[...], 0.0), axis=1, keepdims=True)
            bc = jnp.sum(jnp.where(head_lane == hh, b_ref[...], 0.0), axis=1, keepdims=True)
            t = _dn_chunk_terms(q, k, gc, bc)
            yield
            res = {}
            yield from _unit_lower_inverse_steps(t["lmat"], t["eye"], res)
            tms = _split(res["tm"])
            u = _x3dot(tms, v * bc)
            yield
            w = _x3dot(tms, t["kb"] * t["eg"])
            yield
            s = s_scr[hh]
            s_out[hh] = s
            t_out[hh] = res["tm"]
            sb = s.astype(BF16)
            vn = u - _dot(w.astype(BF16), sb)
            yield
            o_ref[:, vs] = _dot(t["qd"].astype(BF16), sb) + _bdot(t["aqk"], vn)
            yield
            s_scr[hh] = s * jnp.exp(t["gl"]) + _bdot(t["kd"], vn, TN)
            vn_out[:, vs] = vn
            u_out[:, vs] = u
            w_out[:, qs] = w

        _interleave([head(hh) for hh in range(H)])

    row = lambda w: pl.BlockSpec((C, w), lambda n: (n, 0))
    return pl.pallas_call(
        body, name=name, grid=(N,),
        in_specs=[row(DN_CONV_W), row(H), row(H)],
        out_specs=[row(DN_V_W),
                   pl.BlockSpec((H, None, DN_DK, DN_DV), lambda n: (0, n, 0, 0)),
                   pl.BlockSpec((H, None, C, C), lambda n: (0, n, 0, 0)),
                   row(DN_V_W), row(DN_V_W), row(DN_QK_W)],
        out_shape=[jax.ShapeDtypeStruct((T, DN_V_W), F32),
                   jax.ShapeDtypeStruct((H, N, DN_DK, DN_DV), F32),
                   jax.ShapeDtypeStruct((H, N, C, C), F32),
                   jax.ShapeDtypeStruct((T, DN_V_W), F32),
                   jax.ShapeDtypeStruct((T, DN_V_W), F32),
                   jax.ShapeDtypeStruct((T, DN_QK_W), F32)],
        scratch_shapes=[pltpu.VMEM((H, DN_DK, DN_DV), F32)],
        compiler_params=_params("arbitrary"),
    )(act, g, beta)


def _dn_chunk_bwd(act, g, beta, s_saved, tm_saved, vn_saved, u_saved, w_saved, do, name):
    T = act.shape[0]
    C, H = DN_CHUNK, DN_HEADS
    N = T // C

    def body(a_ref, g_ref, b_ref, s_ref, t_ref, vn_ref, u_ref, w_ref, do_ref, da_ref, dg_ref, db_ref, ds_scr):
        @pl.when(pl.program_id(0) == 0)
        def _():
            ds_scr[...] = jnp.zeros_like(ds_scr)

        head_lane = _iota2((C, H), 1)
        dg_cols, db_cols = {}, {}

        def head(hh):
            qs, vs = slice(hh * DN_DK, (hh + 1) * DN_DK), slice(hh * DN_DV, (hh + 1) * DN_DV)
            ks = slice(DN_QK_W + hh * DN_DK, DN_QK_W + (hh + 1) * DN_DK)
            vas = slice(2 * DN_QK_W + hh * DN_DV, 2 * DN_QK_W + (hh + 1) * DN_DV)
            q, k, v = a_ref[:, qs], a_ref[:, ks], a_ref[:, vas]
            gc = jnp.sum(jnp.where(head_lane == hh, g_ref[...], 0.0), axis=1, keepdims=True)
            bc = jnp.sum(jnp.where(head_lane == hh, b_ref[...], 0.0), axis=1, keepdims=True)
            t = _dn_chunk_terms(q, k, gc, bc)
            yield
            lower, strict, eye = t["lower"], t["strict"], t["eye"]
            decay, eg, egl, kb, qd, kd = t["decay"], t["eg"], t["egl"], t["kb"], t["qd"], t["kd"]
            s, tm, vn, u, w, d_o = s_ref[hh], t_ref[hh], vn_ref[:, vs], u_ref[:, vs], w_ref[:, qs], do_ref[:, vs]
            ds_next = ds_scr[hh]
            egl_tot = jnp.exp(t["gl"])
            dob, sb, dsb, vnb = d_o.astype(BF16), s.astype(BF16), ds_next.astype(BF16), vn.astype(BF16)

            dvn = _bdot(t["aqk"], dob, TN) + _bdot(kd, dsb)
            yield
            daqk = jnp.where(lower, _dot(dob, vnb, NT), 0.0)
            dqd = _dot(dob, sb, NT)
            dkd = _dot(vnb, dsb, NT)
            yield
            dvnb = dvn.astype(BF16)
            ds_scr[hh] = _bdot(qd, dob, TN) + egl_tot * ds_next - _bdot(w, dvnb, TN)
            dgl = egl_tot * jnp.sum(jnp.sum(s * ds_next, axis=1, keepdims=True), axis=0, keepdims=True)
            dw = -_dot(dvnb, sb, NT)
            yield
            tms = _split(tm)
            dru = _x3dot(tms, dvn, TN)
            drw = _x3dot(tms, dw, TN)
            yield
            dl = -jnp.where(strict, _x3dot(dru, u, NT) + _x3dot(drw, w, NT), 0.0)
            yield
            dkk = (dl * decay).astype(BF16)
            dqk = (daqk * decay).astype(BF16)
            dkb = _bdot(dkk, k) + drw * eg
            yield
            da_ref[:, ks] = _bdot(dkk, kb, TN) + _bdot(dqk, q, TN) + dkd * egl + dkb * bc
            da_ref[:, qs] = _bdot(dqk, k) + dqd * eg
            da_ref[:, vas] = dru * bc
            yield
            db_cols[hh] = jnp.sum(dru * v, axis=1, keepdims=True) + jnp.sum(dkb * k, axis=1, keepdims=True)
            pm = dl * t["lmat"] + daqk * t["aqk"]
            col_as_col = jnp.sum(jnp.where(eye, jnp.sum(pm, axis=0, keepdims=True), 0.0), axis=1, keepdims=True)
            kdsum = jnp.sum(dkd * kd, axis=1, keepdims=True)
            dgc = (jnp.sum(pm, axis=1, keepdims=True) - col_as_col + jnp.sum(dqd * qd, axis=1, keepdims=True)
                   - kdsum + jnp.sum(drw * (kb * eg), axis=1, keepdims=True))
            dgl = dgl + jnp.sum(kdsum, axis=0, keepdims=True)
            dg_cols[hh] = dgc + jnp.where(t["last"], dgl, 0.0)

        _interleave([head(hh) for hh in range(H)])
        dg_ref[...] = sum(jnp.where(head_lane == hh, dg_cols[hh], 0.0) for hh in range(H))
        db_ref[...] = sum(jnp.where(head_lane == hh, db_cols[hh], 0.0) for hh in range(H))

    row = lambda w: pl.BlockSpec((C, w), lambda n: (N - 1 - n, 0))
    return pl.pallas_call(
        body, name=name, grid=(N,),
        in_specs=[row(DN_CONV_W), row(H), row(H),
                  pl.BlockSpec((H, None, DN_DK, DN_DV), lambda n: (0, N - 1 - n, 0, 0)),
                  pl.BlockSpec((H, None, C, C), lambda n: (0, N - 1 - n, 0, 0)),
                  row(DN_V_W), row(DN_V_W), row(DN_QK_W), row(DN_V_W)],
        out_specs=[row(DN_CONV_W), row(H), row(H)],
        out_shape=[jax.ShapeDtypeStruct((T, DN_CONV_W), F32),
                   jax.ShapeDtypeStruct((T, H), F32), jax.ShapeDtypeStruct((T, H), F32)],
        scratch_shapes=[pltpu.VMEM((H, DN_DK, DN_DV), F32)],
        compiler_params=_params("arbitrary"),
    )(act, g, beta, s_saved, tm_saved, vn_saved, u_saved, w_saved, do)


def _dn_split_w_in(w):
    wab = jnp.pad(w[:, DN_CONV_W + DN_V_W:], ((0, 0), (0, DN_AB_PAD - 2 * DN_HEADS)))
    return w[:, :DN_CONV_W], w[:, DN_CONV_W:DN_CONV_W + DN_V_W], wab


def _dn_layer_fwd(h, wts, conv_w, a_log, dt_bias, gn, w_out, x_res, tag):
    wqkv, wgate, wab = wts
    H = DN_HEADS
    pqkv = _matmul(h, wqkv, "nn", tag + "_pqkv")
    pgate = _matmul(h, wgate, "nn", tag + "_pgate")
    pab = _matmul(h, wab, "nn", tag + "_pab")
    a_in, b_in = pab[:, :H], pab[:, H:2 * H]
    g, beta = _dn_gates(a_in, b_in, a_log, dt_bias, tag + "_gates")
    act = _dn_prep(pqkv, conv_w, tag + "_prep")
    o_raw, s_sv, tm_sv, vn_sv, u_sv, w_sv = _dn_chunk_fwd(act, g, beta, tag + "_chunk_fwd")
    og = _dn_post(o_raw, pgate, gn, tag + "_post")
    y = _matmul(og, w_out, "nn", tag + "_out", add=x_res)
    saved = dict(h=h, wts=wts, conv_w=conv_w, a_log=a_log, dt_bias=dt_bias, gn=gn, w_out=w_out, pqkv=pqkv, pgate=pgate,
                 a_in=a_in, b_in=b_in, g=g, beta=beta, act=act, o_raw=o_raw, chunk=(s_sv, tm_sv, vn_sv, u_sv, w_sv), og=og)
    return y, saved


def _dn_layer_bwd(dout, sv, tag):
    wqkv, wgate, wab = sv["wts"]
    h = sv["h"]
    dog = _matmul(dout, sv["w_out"], "nt", tag + "_dog")
    dw_out = _matmul(sv["og"], dout, "tn", tag + "_dwout")
    do_raw, dgate, dgn = _dn_post_bwd(dog, sv["o_raw"], sv["pgate"], sv["gn"], tag + "_post_bwd")
    dact, dg, dbeta = _dn_chunk_bwd(sv["act"], sv["g"], sv["beta"], *sv["chunk"], do_raw, tag + "_chunk_bwd")
    da_in, db_in, da_log, ddt = _dn_gates_bwd(dg, dbeta, sv["a_in"], sv["b_in"], sv["a_log"], sv["dt_bias"],
                                              tag + "_gates_bwd")
    dpqkv, dconv = _dn_prep_bwd(sv["pqkv"], sv["conv_w"], dact, tag + "_prep_bwd")
    dpab = jnp.pad(jnp.concatenate([da_in, db_in], axis=1), ((0, 0), (0, DN_AB_PAD - 2 * DN_HEADS)))
    dwqkv = _matmul(h, dpqkv, "tn", tag + "_dwqkv")
    dwgate = _matmul(h, dgate, "tn", tag + "_dwgate")
    dwab = _matmul(h, dpab, "tn", tag + "_dwab")
    dh = _matmul(dpqkv, wqkv, "nt", tag + "_dh0")
    dh = _matmul(dgate, wgate, "nt", tag + "_dh1", add=dh)
    dh = _matmul(dpab, wab, "nt", tag + "_dh2", add=dh)
    dw_in = jnp.concatenate([dwqkv, dwgate, dwab[:, :2 * DN_HEADS]], axis=1)
    return dh, (dw_in, dconv, da_log, ddt, dgn, dw_out)


def _sb_layer_fwd(h, w_in, qg, kg, w_out, x_res, tag):
    qg2, kg2 = jnp.tile(qg, (1, 2)), jnp.tile(kg, (1, 2))
    proj = _matmul(h, w_in, "nn", tag + "_proj")
    qn, kn, vb = _sb_prep(proj, qg2, kg2, tag + "_prep")
    o, og, ltot, done = _sb_attn_fwd(qn, kn, vb, proj, tag + "_attn_fwd")
    y = _matmul(og, w_out, "nn", tag + "_out", add=x_res)
    saved = dict(h=h, w_in=w_in, qg2=qg2, kg2=kg2, w_out=w_out, proj=proj, qn=qn, kn=kn, vb=vb, o=o, og=og, ltot=ltot,
                 done=done)
    return y, saved


def _sb_layer_bwd(dout, sv, tag):
    dog = _matmul(dout, sv["w_out"], "nt", tag + "_dog")
    dw_out = _matmul(sv["og"], dout, "tn", tag + "_dwout")
    dqn, dkn, dv, dgate = _sb_attn_bwd(sv["qn"], sv["kn"], sv["vb"], dog, sv["o"], sv["ltot"], sv["done"], sv["proj"],
                                       tag + "_attn_bwd")
    dproj, dqgp, dkgp = _sb_prep_bwd(sv["proj"], dqn, dkn, dv, dgate, sv["qg2"], sv["kg2"], tag + "_prep_bwd")
    dw_in = _matmul(sv["h"], dproj, "tn", tag + "_dwin")
    dh = _matmul(dproj, sv["w_in"], "nt", tag + "_dh")
    dqg = _fold_heads(dqgp, tag + "_dqg")
    dkg = _fold_heads(dkgp, tag + "_dkg")
    return dh, (dw_in, dqg, dkg, dw_out)


def _sc_layer_fwd(h, w_in, conv_w, w_out, x_res, tag):
    proj = _matmul(h, w_in, "nn", tag + "_proj")
    yg = _sc_fwd(proj, conv_w, tag + "_fwd")
    y = _matmul(yg, w_out, "nn", tag + "_out", add=x_res)
    return y, dict(h=h, w_in=w_in, conv_w=conv_w, w_out=w_out, proj=proj, yg=yg)


def _sc_layer_bwd(dout, sv, tag):
    dyg = _matmul(dout, sv["w_out"], "nt", tag + "_dyg")
    dw_out = _matmul(sv["yg"], dout, "tn", tag + "_dwout")
    dproj, dconv = _sc_bwd(dyg, sv["proj"], sv["conv_w"], tag + "_bwd")
    dw_in = _matmul(sv["h"], dproj, "tn", tag + "_dwin")
    dh = _matmul(dproj, sv["w_in"], "nt", tag + "_dh")
    return dh, (dw_in, dconv, dw_out)


def _local_step(x, target, W):
    norm_g = W["norm_g"]
    xs, saves = [x], []
    for i in range(4):
        h = _rmsnorm_fwd(xs[i], norm_g[i:i + 1], f"norm{i}")
        if i in (0, 3):
            j = i // 3
            y, sv = _dn_layer_fwd(h, W["dn_w_in"][j], W["dn_conv_w"][j], W["dn_a_log"][j:j + 1], W["dn_dt_bias"][j:j + 1],
                                  W["dn_o_norm_g"][j:j + 1], W["dn_w_out"][j], xs[i], f"dn{j}")
        elif i == 1:
            y, sv = _sb_layer_fwd(h, W["sb_w_in"], W["sb_q_norm_g"], W["sb_k_norm_g"], W["sb_w_out"], xs[i], "sb")
        else:
            y, sv = _sc_layer_fwd(h, W["sc_w_in"], W["sc_conv_w"], W["sc_w_out"], xs[i], "sc")
        xs.append(y)
        saves.append(sv)
    dx, loss = _loss_head(xs[4], target)
    G = {}
    dnorm = [None] * 4
    dn_parts = [None, None]
    for i in (3, 2, 1, 0):
        if i in (0, 3):
            dh, dn_parts[i // 3] = _dn_layer_bwd(dx, saves[i], f"dn{i // 3}")
        elif i == 1:
            dh, (G["sb_w_in"], G["sb_q_norm_g"], G["sb_k_norm_g"], G["sb_w_out"]) = _sb_layer_bwd(dx, saves[i], "sb")
        else:
            dh, (G["sc_w_in"], G["sc_conv_w"], G["sc_w_out"]) = _sc_layer_bwd(dx, saves[i], "sc")
        dx, dnorm[i] = _rmsnorm_bwd(dh, xs[i], norm_g[i:i + 1], dx, f"norm{i}_bwd")
    G["norm_g"] = jnp.concatenate(dnorm, axis=0)
    for k, name in enumerate(("dn_w_in", "dn_conv_w", "dn_a_log", "dn_dt_bias", "dn_o_norm_g", "dn_w_out")):
        G[name] = jnp.stack([dn_parts[0][k], dn_parts[1][k]], axis=0)
    return loss, dx, G


def _adamw(w, m, v, parts, name):
    R, C = w.shape
    tr = _tile(R, 128, SUBLANE)

    def body(w_ref, m_ref, v_ref, p_ref, g_ref, d_ref, nm_ref, nv_ref):
        g = p_ref[0].astype(F32)
        for s in range(1, N_DEV):
            g = g + p_ref[s].astype(F32)
        m2 = ADAM_B1 * m_ref[...] + (1.0 - ADAM_B1) * g
        v2 = ADAM_B2 * v_ref[...] + (1.0 - ADAM_B2) * (g * g)
        m_hat = m2 / (1.0 - ADAM_B1 ** ADAM_STEP)
        v_hat = v2 / (1.0 - ADAM_B2 ** ADAM_STEP)
        g_ref[...] = g
        d_ref[...] = -ADAM_LR * (m_hat / (jnp.sqrt(v_hat) + ADAM_EPS) + ADAM_WD * w_ref[...])
        nm_ref[...] = m2
        nv_ref[...] = v2

    blk = pl.BlockSpec((tr, C), lambda i: (i, 0))
    return pl.pallas_call(
        body, name=name, grid=(R // tr,),
        in_specs=[blk, blk, blk, pl.BlockSpec((N_DEV, tr, C), lambda i: (0, i, 0))],
        out_specs=[blk] * 4, out_shape=[jax.ShapeDtypeStruct((R, C), F32)] * 4,
        compiler_params=_params("parallel"),
    )(w, m, v, parts)


_HBM = pl.BlockSpec(memory_space=pltpu.HBM)
_MESH = pl.DeviceIdType.MESH


def _slot(x, y, c):
    return 4 * x + 2 * y + c


def _all_gather(shards, name):
    n = len(shards)

    def body(*refs):
        ins, outs = refs[:n], refs[n:2 * n]
        send_sems, recv_sems, local_sems = refs[2 * n:]
        x, y, c = lax.axis_index("x"), lax.axis_index("y"), lax.axis_index("c")
        me, sibling = (x, y, c), (x, y, 1 - c)
        chips = [(1 - x, y), (x, 1 - y), (1 - x, 1 - y)]

        def copy(a, k, block, to, src=None):
            dst = outs[a].at[_slot(*block)]
            return pltpu.make_async_remote_copy(src_ref=dst if src is None else src, dst_ref=dst,
                                                send_sem=send_sems.at[a, k], recv_sem=recv_sems.at[a, k],
                                                device_id=to, device_id_type=_MESH)

        mine = [pltpu.make_async_copy(ins[a], outs[a].at[_slot(*me)], local_sems.at[a]) for a in range(n)]
        for cp in mine:
            cp.start()
        first = []
        for a in range(n):
            first.append(copy(a, 0, me, sibling, src=ins[a]))
            first += [copy(a, 1 + j, me, (*chip, c), src=ins[a]) for j, chip in enumerate(chips)]
        for cp in first:
            cp.start()
        passed = []
        for j, chip in enumerate(chips):
            for a in range(n):
                copy(a, 1 + j, (*chip, c), me).wait_recv()
                fwd = copy(a, 4 + j, (*chip, c), sibling)
                fwd.start()
                passed.append(fwd)
        for a in range(n):
            copy(a, 0, sibling, me).wait_recv()
            for j, chip in enumerate(chips):
                copy(a, 4 + j, (*chip, 1 - c), me).wait_recv()
        for cp in first + passed:
            cp.wait_send()
        for cp in mine:
            cp.wait()

    return pl.pallas_call(
        body, name=name,
        in_specs=[_HBM] * n, out_specs=[_HBM] * n,
        out_shape=[jax.ShapeDtypeStruct((N_DEV,) + s.shape, s.dtype) for s in shards],
        scratch_shapes=[pltpu.SemaphoreType.DMA((n, N_DEV - 1)), pltpu.SemaphoreType.DMA((n, N_DEV - 1)),
                        pltpu.SemaphoreType.DMA((n,))],
    )(*shards)


def _exchange(arrays, scatter, name):
    n = len(arrays)

    def body(*refs):
        ins, outs = refs[:n], refs[n:2 * n]
        send_sems, recv_sems, local_sems = refs[2 * n:]
        x, y, c = lax.axis_index("x"), lax.axis_index("y"), lax.axis_index("c")
        me = _slot(x, y, c)
        copies = []
        for a in range(n):
            cp = pltpu.make_async_copy(ins[a].at[me] if scatter[a] else ins[a], outs[a].at[me], local_sems.at[a])
            cp.start()
            copies.append(cp)
        for r in range(1, N_DEV):
            px = 1 - x if r & 4 else x
            py = 1 - y if r & 2 else y
            pc = 1 - c if r & 1 else c
            for a in range(n):
                cp = pltpu.make_async_remote_copy(
                    src_ref=ins[a].at[_slot(px, py, pc)] if scatter[a] else ins[a], dst_ref=outs[a].at[me],
                    send_sem=send_sems.at[a, r - 1], recv_sem=recv_sems.at[a, r - 1],
                    device_id=(px, py, pc), device_id_type=_MESH)
                cp.start()
                copies.append(cp)
        for cp in copies:
            cp.wait()

    shapes = [a.shape[1:] if s else a.shape for a, s in zip(arrays, scatter)]
    return pl.pallas_call(
        body, name=name,
        in_specs=[_HBM] * n, out_specs=[_HBM] * n,
        out_shape=[jax.ShapeDtypeStruct((N_DEV,) + tuple(s), a.dtype) for s, a in zip(shapes, arrays)],
        scratch_shapes=[pltpu.SemaphoreType.DMA((n, N_DEV - 1)), pltpu.SemaphoreType.DMA((n, N_DEV - 1)),
                        pltpu.SemaphoreType.DMA((n,))],
    )(*arrays)


_SHARDED = ("dn_w_in", "dn_conv_w", "dn_o_norm_g", "dn_w_out", "sb_w_in", "sb_w_out", "sc_w_in", "sc_conv_w", "sc_w_out")
_MATMUL_WEIGHTS = ("dn_w_in", "dn_w_out", "sb_w_in", "sb_w_out", "sc_w_in", "sc_w_out")
_COLUMN_SHARDED = ("dn_w_in", "dn_conv_w", "dn_o_norm_g", "sb_w_in", "sc_w_in", "sc_conv_w")
_REPLICATED = ("norm_g", "dn_a_log", "dn_dt_bias", "sb_q_norm_g", "sb_k_norm_g")
_ORDER = ("norm_g", "dn_w_in", "dn_conv_w", "dn_a_log", "dn_dt_bias", "dn_o_norm_g", "dn_w_out", "sb_w_in", "sb_q_norm_g",
          "sb_k_norm_g", "sb_w_out", "sc_w_in", "sc_conv_w", "sc_w_out")
_PACK_COLS = D_MODEL


def _assemble(name, gathered):
    if name in _COLUMN_SHARDED:
        g = jnp.moveaxis(gathered, 0, -2)
        return g.reshape(g.shape[:-2] + (g.shape[-2] * g.shape[-1],))
    g = jnp.moveaxis(gathered, 0, 1)
    return g.reshape((g.shape[0], g.shape[1] * g.shape[2]) + g.shape[3:])


def _disassemble(name, full):
    if name in _COLUMN_SHARDED:
        g = full.reshape(full.shape[:-1] + (N_DEV, full.shape[-1] // N_DEV))
        return jnp.moveaxis(g, -2, 0)
    g = full.reshape((full.shape[0], N_DEV, full.shape[1] // N_DEV) + full.shape[2:])
    return jnp.moveaxis(g, 1, 0)


def _pack_replicated(d):
    rows = [d["norm_g"]]
    for name in _REPLICATED[1:]:
        flat = d[name].reshape(1, -1)
        rows.append(jnp.pad(flat, ((0, 0), (0, _PACK_COLS - flat.shape[1]))))
    return jnp.concatenate(rows, axis=0)


def _unpack_replicated(p, like):
    out = {"norm_g": p[:4]}
    for r, name in enumerate(_REPLICATED[1:]):
        shape = like[name].shape
        out[name] = p[4 + r, :math.prod(shape)].reshape(shape)
    return out


def kernel(x, norm_g, dn_w_in, dn_conv_w, dn_a_log, dn_dt_bias, dn_o_norm_g, dn_w_out, sb_w_in, sb_q_norm_g, sb_k_norm_g, sb_w_out, sc_w_in, sc_conv_w, sc_w_out, loss_target, m_norm_g, m_dn_w_in, m_dn_conv_w, m_dn_a_log, m_dn_dt_bias, m_dn_o_norm_g, m_dn_w_out, m_sb_w_in, m_sb_q_norm_g, m_sb_k_norm_g, m_sb_w_out, m_sc_w_in, m_sc_conv_w, m_sc_w_out, v_norm_g, v_dn_w_in, v_dn_conv_w, v_dn_a_log, v_dn_dt_bias, v_dn_o_norm_g, v_dn_w_out, v_sb_w_in, v_sb_q_norm_g, v_sb_k_norm_g, v_sb_w_out, v_sc_w_in, v_sc_conv_w, v_sc_w_out):
    w = dict(norm_g=norm_g, dn_w_in=dn_w_in, dn_conv_w=dn_conv_w, dn_a_log=dn_a_log, dn_dt_bias=dn_dt_bias,
             dn_o_norm_g=dn_o_norm_g, dn_w_out=dn_w_out, sb_w_in=sb_w_in, sb_q_norm_g=sb_q_norm_g, sb_k_norm_g=sb_k_norm_g,
             sb_w_out=sb_w_out, sc_w_in=sc_w_in, sc_conv_w=sc_conv_w, sc_w_out=sc_w_out)
    m = dict(norm_g=m_norm_g, dn_w_in=m_dn_w_in, dn_conv_w=m_dn_conv_w, dn_a_log=m_dn_a_log, dn_dt_bias=m_dn_dt_bias,
             dn_o_norm_g=m_dn_o_norm_g, dn_w_out=m_dn_w_out, sb_w_in=m_sb_w_in, sb_q_norm_g=m_sb_q_norm_g,
             sb_k_norm_g=m_sb_k_norm_g, sb_w_out=m_sb_w_out, sc_w_in=m_sc_w_in, sc_conv_w=m_sc_conv_w, sc_w_out=m_sc_w_out)
    v = dict(norm_g=v_norm_g, dn_w_in=v_dn_w_in, dn_conv_w=v_dn_conv_w, dn_a_log=v_dn_a_log, dn_dt_bias=v_dn_dt_bias,
             dn_o_norm_g=v_dn_o_norm_g, dn_w_out=v_dn_w_out, sb_w_in=v_sb_w_in, sb_q_norm_g=v_sb_q_norm_g,
             sb_k_norm_g=v_sb_k_norm_g, sb_w_out=v_sb_w_out, sc_w_in=v_sc_w_in, sc_conv_w=v_sc_conv_w, sc_w_out=v_sc_w_out)

    shards = [w[k].astype(BF16) if k in _MATMUL_WEIGHTS else w[k] for k in _SHARDED]
    gathered = _all_gather(shards, "gather_weights")
    W = {k: _assemble(k, g) for k, g in zip(_SHARDED, gathered)}
    for k in _REPLICATED:
        W[k] = w[k]
    W["dn_w_in"] = [_dn_split_w_in(W["dn_w_in"][j]) for j in range(2)]
    W["sb_w_in"] = W["sb_w_in"][0]
    W["sb_w_out"] = W["sb_w_out"][0]
    W["sc_w_in"] = W["sc_w_in"][0]
    W["sc_conv_w"] = W["sc_conv_w"][0]
    W["sc_w_out"] = W["sc_w_out"][0]

    loss_part, dx, G = _local_step(x[0], loss_target[0], W)
    for k in ("sb_w_in", "sb_w_out", "sc_w_in", "sc_conv_w", "sc_w_out"):
        G[k] = G[k][None]
    G["dn_a_log"] = G["dn_a_log"].reshape(2, DN_HEADS)
    G["dn_dt_bias"] = G["dn_dt_bias"].reshape(2, DN_HEADS)
    G["dn_o_norm_g"] = G["dn_o_norm_g"].reshape(2, DN_DV)

    outgoing = [_disassemble(k, G[k].astype(BF16) if k in _MATMUL_WEIGHTS else G[k]) for k in _SHARDED] + [_pack_replicated(G)]
    landed = _exchange(outgoing, [True] * len(_SHARDED) + [False], "exchange_grads")

    res = {}
    for k, parts in zip(_SHARDED, landed[:-1]):
        shape = w[k].shape
        rows = math.prod(shape[:-1])
        flat = lambda a: a.reshape(rows, shape[-1])
        outs = _adamw(flat(w[k]), flat(m[k]), flat(v[k]), parts.reshape(N_DEV, rows, shape[-1]), "adamw_" + k)
        res[k] = [o.reshape(shape) for o in outs]
    outs = _adamw(_pack_replicated(w), _pack_replicated(m), _pack_replicated(v), landed[-1], "adamw_replicated")
    unpacked = [_unpack_replicated(o, w) for o in outs]
    for k in _REPLICATED:
        res[k] = [u[k] for u in unpacked]

    loss = lax.psum(loss_part[0, 0], ("x", "y", "c"))
    return (loss, dx[None]) + tuple(res[k][0] for k in _ORDER) + tuple(res[k][1] for k in _ORDER) \
        + tuple(res[k][2] for k in _ORDER) + tuple(res[k][3] for k in _ORDER)
```

```python
import functools
import itertools
import math

import jax
import jax.numpy as jnp
from jax import lax
from jax.experimental import pallas as pl
from jax.experimental.pallas import tpu as pltpu

F32 = jnp.float32
BF16 = jnp.bfloat16
HIGHEST = lax.Precision.HIGHEST

N_DEV = 8
D_MODEL = 1024
RMS_EPS = 1e-6
L2_EPS = 1e-6

DN_HEADS = 8
DN_DK = 128
DN_DV = 256
DN_QK_W = DN_HEADS * DN_DK
DN_V_W = DN_HEADS * DN_DV
DN_CONV = 4
DN_CHUNK = 64
DN_CONV_W = 2 * DN_QK_W + DN_V_W
DN_IN = DN_CONV_W + DN_V_W + 2 * DN_HEADS
DN_AB_PAD = 128
DN_PREP_BLK = 512

SB_HEADS = 16
SB_DH = 64
SB_W = SB_HEADS * SB_DH
SB_PAIRS = SB_HEADS // 2
SB_TQ = 256
SB_TK = 128
SB_DEAD = -106.0

SC_W = 2 * D_MODEL
SC_CONV = 3
SC_BLK = 512
SC_NBLK = SC_W // SC_BLK

ADAM_LR = 0.001
ADAM_B1 = 0.9
ADAM_B2 = 0.999
ADAM_EPS = 1e-08
ADAM_WD = 0.01
ADAM_STEP = 10

LANE = 128
SUBLANE = 8
HALO = SUBLANE
ROW_TILE = 256
WIDE_ROW_TILE = 128
VMEM_LIMIT = 48 * 2 ** 20

NN = ((1,), (0,))
NT = ((1,), (1,))
TN = ((0,), (0,))


def _dot(a, b, dims=NN, precision=None):
    return lax.dot_general(a, b, (dims, ((), ())), precision=precision, preferred_element_type=F32)


def _bdot(a, b, dims=NN):
    return _dot(a.astype(BF16), b.astype(BF16), dims)


def _hdot(a, b, dims=NN):
    return _dot(a, b, dims, precision=HIGHEST)


def _tile(dim, pref, align=LANE):
    t = (min(pref, dim) // align) * align
    while t >= align:
        if dim % t == 0:
            return t
        t -= align
    return dim


def _params(*sem):
    return pltpu.CompilerParams(dimension_semantics=sem, vmem_limit_bytes=VMEM_LIMIT)


def _sigmoid(x):
    return 1.0 / (1.0 + jnp.exp(-x))


def _softplus(x):
    return jnp.maximum(x, 0.0) + jnp.log(1.0 + jnp.exp(-jnp.abs(x)))


def _silu_and_grad(x):
    s = _sigmoid(x)
    return x * s, s * (1.0 + x * (1.0 - s))


def _iota2(shape, dim):
    return lax.broadcasted_iota(jnp.int32, shape, dim)


def _matmul(a, b, mode, name, out_dtype=F32, add=None, tm=1024, tn=1024, tk=1024):
    if mode == "nn":
        (M, K), (K2, N) = a.shape, b.shape
    elif mode == "nt":
        (M, K), (N, K2) = a.shape, b.shape
    else:
        (K, M), (K2, N) = a.shape, b.shape
    assert K == K2, (a.shape, b.shape, mode)
    tm, tn, tk = _tile(M, tm), _tile(N, tn), _tile(K, tk)
    nk = K // tk
    dims = {"nn": NN, "nt": NT, "tn": TN}[mode]
    a_spec = pl.BlockSpec((tk, tm), lambda i, j, k: (k, i)) if mode == "tn" else pl.BlockSpec((tm, tk), lambda i, j, k: (i, k))
    b_spec = pl.BlockSpec((tn, tk), lambda i, j, k: (j, k)) if mode == "nt" else pl.BlockSpec((tk, tn), lambda i, j, k: (k, j))
    o_spec = pl.BlockSpec((tm, tn), lambda i, j, k: (i, j))
    has_add = add is not None

    def body(*refs):
        a_ref, b_ref = refs[0], refs[1]
        add_ref = refs[2] if has_add else None
        o_ref = refs[3] if has_add else refs[2]
        p = _bdot(a_ref[...], b_ref[...], dims)

        def finish(acc):
            if has_add:
                acc = acc + add_ref[...]
            o_ref[...] = acc.astype(out_dtype)

        if nk == 1:
            finish(p)
        else:
            acc_ref = refs[-1]
            k = pl.program_id(2)

            @pl.when(k == 0)
            def _():
                acc_ref[...] = p

            @pl.when(k > 0)
            def _():
                acc_ref[...] += p

            @pl.when(k == nk - 1)
            def _():
                finish(acc_ref[...])

    in_specs = [a_spec, b_spec] + ([o_spec] if has_add else [])
    args = (a, b) + ((add,) if has_add else ())
    return pl.pallas_call(
        body, name=name, grid=(M // tm, N // tn, nk),
        in_specs=in_specs, out_specs=o_spec,
        out_shape=jax.ShapeDtypeStruct((M, N), out_dtype),
        scratch_shapes=[pltpu.VMEM((tm, tn), F32)] if nk > 1 else [],
        compiler_params=_params("parallel", "parallel", "arbitrary"),
    )(*args)


def _rmsnorm_fwd(x, g, name):
    T, D = x.shape
    tt = _tile(T, 512, SUBLANE)

    def body(x_ref, g_ref, o_ref):
        xv = x_ref[...]
        r = lax.rsqrt(jnp.mean(xv * xv, axis=-1, keepdims=True) + RMS_EPS)
        o_ref[...] = (xv * r * g_ref[...]).astype(BF16)

    return pl.pallas_call(
        body, name=name, grid=(T // tt,),
        in_specs=[pl.BlockSpec((tt, D), lambda i: (i, 0)), pl.BlockSpec((1, D), lambda i: (0, 0))],
        out_specs=pl.BlockSpec((tt, D), lambda i: (i, 0)),
        out_shape=jax.ShapeDtypeStruct((T, D), BF16),
        compiler_params=_params("parallel"),
    )(x, g)


def _rmsnorm_bwd(dh, x, g, dx_res, name):
    T, D = x.shape
    tt = _tile(T, 256, SUBLANE)

    def body(dh_ref, x_ref, g_ref, res_ref, dx_ref, dg_ref):
        xv, dhv = x_ref[...], dh_ref[...]
        r = lax.rsqrt(jnp.mean(xv * xv, axis=-1, keepdims=True) + RMS_EPS)
        xh = xv * r
        dxh = dhv * g_ref[...]
        m = jnp.mean(dxh * xh, axis=-1, keepdims=True)
        dx_ref[...] = res_ref[...] + r * (dxh - xh * m)
        part = jnp.sum(dhv * xh, axis=0, keepdims=True)

        @pl.when(pl.program_id(0) == 0)
        def _():
            dg_ref[...] = part

        @pl.when(pl.program_id(0) > 0)
        def _():
            dg_ref[...] += part

    row = pl.BlockSpec((tt, D), lambda i: (i, 0))
    vec = pl.BlockSpec((1, D), lambda i: (0, 0))
    return pl.pallas_call(
        body, name=name, grid=(T // tt,),
        in_specs=[row, row, vec, row], out_specs=[row, vec],
        out_shape=[jax.ShapeDtypeStruct((T, D), F32), jax.ShapeDtypeStruct((1, D), F32)],
        compiler_params=_params("arbitrary"),
    )(dh, x, g, dx_res)


def _loss_head(y, target, name="loss_head"):
    T, D = y.shape
    tt = _tile(T, 512, SUBLANE)

    def body(y_ref, t_ref, dy_ref, l_ref):
        e = y_ref[...] - t_ref[...]
        dy_ref[...] = e * (1.0 / D)
        s = jnp.sum(jnp.sum(e * e, axis=1, keepdims=True), axis=0, keepdims=True) * (0.5 / D)
        s = jnp.broadcast_to(s, (1, LANE))

        @pl.when(pl.program_id(0) == 0)
        def _():
            l_ref[...] = s

        @pl.when(pl.program_id(0) > 0)
        def _():
            l_ref[...] += s

    row = pl.BlockSpec((tt, D), lambda i: (i, 0))
    return pl.pallas_call(
        body, name=name, grid=(T // tt,),
        in_specs=[row, row], out_specs=[row, pl.BlockSpec((1, LANE), lambda i: (0, 0))],
        out_shape=[jax.ShapeDtypeStruct((T, D), F32), jax.ShapeDtypeStruct((1, LANE), F32)],
        compiler_params=_params("arbitrary"),
    )(y, target)


def _down(x, k):
    return pltpu.roll(x, k, 0) if k else x


def _up(x, k):
    return pltpu.roll(x, x.shape[0] - k, 0) if k else x


def _sc_fwd(proj, conv_w, name):
    T = proj.shape[0]
    tt = _tile(T, WIDE_ROW_TILE, SUBLANE)
    B = SC_BLK

    def body(p_ref, ph_ref, w_ref, o_ref):
        keep = (pl.program_id(0) > 0).astype(F32)
        for j in range(SC_NBLK):
            cb, cc, cu, cg = (slice(k * SC_W + j * B, k * SC_W + (j + 1) * B) for k in range(4))
            cw = slice(j * B, (j + 1) * B)
            z = jnp.concatenate([ph_ref[:, cc] * ph_ref[:, cu] * keep, p_ref[:, cc] * p_ref[:, cu]], axis=0)
            cz = (w_ref[2:3, cw] * z + w_ref[1:2, cw] * _down(z, 1) + w_ref[0:1, cw] * _down(z, 2))[HALO:]
            gate = p_ref[:, cg]
            o_ref[:, cw] = (p_ref[:, cb] * cz * (gate * _sigmoid(gate))).astype(BF16)

    return pl.pallas_call(
        body, name=name, grid=(T // tt,),
        in_specs=[pl.BlockSpec((tt, 4 * SC_W), lambda i: (i, 0)),
                  pl.BlockSpec((HALO, 4 * SC_W), lambda i: (jnp.maximum(i * (tt // HALO) - 1, 0), 0)),
                  pl.BlockSpec((SC_CONV, SC_W), lambda i: (0, 0))],
        out_specs=pl.BlockSpec((tt, SC_W), lambda i: (i, 0)),
        out_shape=jax.ShapeDtypeStruct((T, SC_W), BF16),
        compiler_params=_params("parallel"),
    )(proj, proj, conv_w)


def _sc_bwd(dyg, proj, conv_w, name):
    T = proj.shape[0]
    tt = _tile(T, WIDE_ROW_TILE, SUBLANE)
    nt = T // tt
    B = SC_BLK
    hb = tt // HALO

    def body(d_ref, dn_ref, p_ref, pp_ref, pn_ref, w_ref, o_ref, dw_ref):
        i = pl.program_id(0)
        keep_p = (i > 0).astype(F32)
        keep_n = (i < nt - 1).astype(F32)
        main = slice(HALO, HALO + tt)
        parts = []
        for j in range(SC_NBLK):
            cw = slice(j * B, (j + 1) * B)

            def ext(k):
                s = slice(k * SC_W + j * B, k * SC_W + (j + 1) * B)
                return s, jnp.concatenate([pp_ref[:, s] * keep_p, p_ref[:, s], pn_ref[:, s]], axis=0)

            (sb, b), (sc, c), (su, u), (sg_, gate) = ext(0), ext(1), ext(2), ext(3)
            dyg_e = jnp.concatenate([jnp.zeros((HALO, B), F32), d_ref[:, cw], dn_ref[:, cw] * keep_n], axis=0)
            w0, w1, w2 = w_ref[0:1, cw], w_ref[1:2, cw], w_ref[2:3, cw]
            z = c * u
            z1, z2 = _down(z, 1), _down(z, 2)
            cz = w2 * z + w1 * z1 + w0 * z2
            sg, dsg = _silu_and_grad(gate)
            dy = dyg_e * sg
            dcz = dy * b
            dz = w2 * dcz + w1 * _up(dcz, 1) + w0 * _up(dcz, 2)
            o_ref[:, sb] = (dy * cz)[main].astype(BF16)
            o_ref[:, sc] = (dz * u)[main].astype(BF16)
            o_ref[:, su] = (dz * c)[main].astype(BF16)
            o_ref[:, sg_] = (dyg_e * (b * cz) * dsg)[main].astype(BF16)
            dcm = dcz[main]
            parts.append(jnp.concatenate([jnp.sum(dcm * z2[main], axis=0, keepdims=True),
                                          jnp.sum(dcm * z1[main], axis=0, keepdims=True),
                                          jnp.sum(dcm * z[main], axis=0, keepdims=True)], axis=0))
        part = jnp.concatenate(parts, axis=1)

        @pl.when(i == 0)
        def _():
            dw_ref[...] = part

        @pl.when(i > 0)
        def _():
            dw_ref[...] += part

    nxt = lambda i: (jnp.minimum((i + 1) * hb, nt * hb - 1), 0)
    return pl.pallas_call(
        body, name=name, grid=(nt,),
        in_specs=[pl.BlockSpec((tt, SC_W), lambda i: (i, 0)),
                  pl.BlockSpec((HALO, SC_W), nxt),
                  pl.BlockSpec((tt, 4 * SC_W), lambda i: (i, 0)),
                  pl.BlockSpec((HALO, 4 * SC_W), lambda i: (jnp.maximum(i * hb - 1, 0), 0)),
                  pl.BlockSpec((HALO, 4 * SC_W), nxt),
                  pl.BlockSpec((SC_CONV, SC_W), lambda i: (0, 0))],
        out_specs=[pl.BlockSpec((tt, 4 * SC_W), lambda i: (i, 0)), pl.BlockSpec((SC_CONV, SC_W), lambda i: (0, 0))],
        out_shape=[jax.ShapeDtypeStruct((T, 4 * SC_W), BF16), jax.ShapeDtypeStruct((SC_CONV, SC_W), F32)],
        compiler_params=_params("arbitrary"),
    )(dyg, dyg, proj, proj, proj, conv_w)


def _split3_dot(x, m):
    hi = x.astype(BF16)
    r1 = x - hi.astype(F32)
    mid = r1.astype(BF16)
    lo = (r1 - mid.astype(F32)).astype(BF16)
    return _dot(hi, m) + _dot(mid, m) + _dot(lo, m)


def _split2_dot(x, m):
    hi = x.astype(BF16)
    lo = (x - hi.astype(F32)).astype(BF16)
    return _dot(hi, m) + _dot(lo, m)


def _head_mean_matrix():
    r, c = _iota2((LANE, LANE), 0), _iota2((LANE, LANE), 1)
    return jnp.where((r // SB_DH) == (c // SB_DH), 1.0 / SB_DH, 0.0).astype(BF16)


def _sb_prep(proj, qg2, kg2, name):
    T = proj.shape[0]
    tt = _tile(T, WIDE_ROW_TILE, SUBLANE)

    def body(p_ref, qg_ref, kg_ref, q_ref, k_ref, v_ref):
        bd = _head_mean_matrix()

        def norm(x, g, scale):
            r = lax.rsqrt(_split3_dot(x * x, bd) + RMS_EPS)
            return (x * r * g * scale).astype(BF16)

        v_ref[...] = p_ref[:, 2 * SB_W:3 * SB_W].astype(BF16)
        for p in range(SB_PAIRS):
            cols = slice(p * LANE, (p + 1) * LANE)
            q_ref[:, cols] = norm(p_ref[:, cols], qg_ref[...], SB_DH ** -0.5)
            k_ref[:, cols] = norm(p_ref[:, SB_W + p * LANE:SB_W + (p + 1) * LANE], kg_ref[...], 1.0)

    blk = pl.BlockSpec((tt, SB_W), lambda i: (i, 0))
    vec = pl.BlockSpec((1, LANE), lambda i: (0, 0))
    return pl.pallas_call(
        body, name=name, grid=(T // tt,),
        in_specs=[pl.BlockSpec((tt, 4 * SB_W), lambda i: (i, 0)), vec, vec],
        out_specs=[blk, blk, blk],
        out_shape=[jax.ShapeDtypeStruct((T, SB_W), BF16)] * 3,
        compiler_params=_params("parallel"),
    )(proj, qg2, kg2)


def _sb_prep_bwd(proj, dqn, dkn, dv, dgate, qg2, kg2, name):
    T = proj.shape[0]
    tt = _tile(T, WIDE_ROW_TILE, SUBLANE)

    def body(p_ref, dq_ref, dk_ref, dv_ref, dg_ref, qg_ref, kg_ref, o_ref, dqg_ref, dkg_ref):
        i = pl.program_id(0)
        bd = _head_mean_matrix()

        def norm_bwd(x, g, dy):
            r = lax.rsqrt(_split3_dot(x * x, bd) + RMS_EPS)
            xh = x * r
            dxh = dy * g
            m = _split3_dot(dxh * xh, bd)
            return r * (dxh - xh * m), jnp.sum(dy * xh, axis=0, keepdims=True)

        o_ref[:, 2 * SB_W:3 * SB_W] = dv_ref[...].astype(BF16)
        o_ref[:, 3 * SB_W:4 * SB_W] = dg_ref[...].astype(BF16)
        pq = pk = jnp.zeros((1, LANE), F32)
        for p in range(SB_PAIRS):
            cols, kcols = slice(p * LANE, (p + 1) * LANE), slice(SB_W + p * LANE, SB_W + (p + 1) * LANE)
            dxq, sq = norm_bwd(p_ref[:, cols], qg_ref[...], dq_ref[:, cols])
            dxk, sk = norm_bwd(p_ref[:, kcols], kg_ref[...], dk_ref[:, cols])
            o_ref[:, cols] = dxq.astype(BF16)
            o_ref[:, kcols] = dxk.astype(BF16)
            pq, pk = pq + sq, pk + sk

        @pl.when(i == 0)
        def _():
            dqg_ref[...] = pq
            dkg_ref[...] = pk

        @pl.when(i > 0)
        def _():
            dqg_ref[...] += pq
            dkg_ref[...] += pk

    blk = pl.BlockSpec((tt, SB_W), lambda i: (i, 0))
    vec = pl.BlockSpec((1, LANE), lambda i: (0, 0))
    wide = pl.BlockSpec((tt, 4 * SB_W), lambda i: (i, 0))
    return pl.pallas_call(
        body, name=name, grid=(T // tt,),
        in_specs=[wide, blk, blk, blk, blk, vec, vec],
        out_specs=[wide, vec, vec],
        out_shape=[jax.ShapeDtypeStruct((T, 4 * SB_W), BF16)] + [jax.ShapeDtypeStruct((1, LANE), F32)] * 2,
        compiler_params=_params("arbitrary"),
    )(proj, dqn, dkn, dv, dgate, qg2, kg2)


def _fold_heads(part, name):
    def body(p_ref, o_ref):
        r, c = _iota2((LANE, SB_DH), 0), _iota2((LANE, SB_DH), 1)
        fold = jnp.where((r % SB_DH) == c, 1.0, 0.0).astype(F32)
        o_ref[...] = jnp.sum(_hdot(p_ref[...], fold), axis=0, keepdims=True)

    return pl.pallas_call(body, name=name, out_shape=jax.ShapeDtypeStruct((1, SB_DH), F32))(part)


def _sb_masks():
    lane = _iota2((1, LANE), 1)
    return lane < SB_DH


def _sb_attn_fwd(qn, kn, vb, proj, name, comm=None):
    T = qn.shape[0]
    tq, tk = _tile(T, SB_TQ, SUBLANE), SB_TK
    assert tq % tk == 0

    def body(q_ref, k_ref, v_ref, g_ref, o_ref, og_ref, lt_ref, done_ref):
        i = pl.program_id(1)
        ma = _sb_masks()
        q2 = q_ref[...]
        zero = jnp.zeros_like(q2)
        qs = (jnp.where(ma, q2, zero), jnp.where(ma, zero, q2))
        upper = (_iota2((tk, tk), 0) > _iota2((tk, tk), 1)).astype(BF16)
        qpos = i * tq + _iota2((tq, tk), 0)
        nb = tq // tk

        def trip(kb_top, masked, carry):
            acc, la, lb = carry
            chains = [(b, h) for b in range(nb) for h in range(2)]
            k2s, vss, masks = [], [], []
            for b in range(nb):
                kb = kb_top - b
                rows = pl.ds(pl.multiple_of(kb * tk, tk), tk)
                k2s.append(k_ref[rows, :])
                v2 = v_ref[rows, :]
                zv = jnp.zeros_like(v2)
                vss.append((jnp.where(ma, v2, zv), jnp.where(ma, zv, v2)))
                masks.append((kb * tk + _iota2((tq, tk), 1)) < qpos if masked else None)
            zs = [_dot(qs[h], k2s[b], NT) for b, h in chains]
            ts = [jnp.log(1.0 + jnp.exp(-jnp.abs(z))) for z in zs]
            ls = [-(jnp.maximum(z, 0.0) + t) for z, t in zip(zs, ts)]
            if masked:
                ls = [jnp.where(masks[b], l, 0.0) for (b, h), l in zip(chains, ls)]
            cums = [_split2_dot(l, upper) for l in ls]
            sums = [jnp.sum(l, axis=1, keepdims=True) for l in ls]
            offs, tot = {}, [la, lb]
            for b in range(nb):
                for h in range(2):
                    offs[(b, h)] = tot[h]
                    tot[h] = tot[h] + sums[chains.index((b, h))]
            ws = [jnp.exp(jnp.minimum(z, 0.0) - t + c + offs[ch]) for ch, z, t, c in zip(chains, zs, ts, cums)]
            if masked:
                ws = [jnp.where(masks[b], w, 0.0) for (b, h), w in zip(chains, ws)]
            for (b, h), w in zip(chains, ws):
                acc = acc + _dot(w.astype(BF16), vss[b][h])
            return acc, tot[0], tot[1]

        def largest(la, lb):
            return jnp.max(jnp.maximum(la, lb))

        z1 = jnp.zeros((tq, 1), F32)
        acc, la, lb = trip((i + 1) * nb - 1, True, (jnp.zeros((tq, LANE), F32), z1, z1))

        def live(c):
            return (c[0] < i) & (c[4] > SB_DEAD)

        def more(c):
            j, acc, la, lb, _ = c
            acc, la, lb = trip((i - j) * nb - 1, False, (acc, la, lb))
            return j + 1, acc, la, lb, largest(la, lb)

        done, acc, la, lb, _ = lax.while_loop(live, more, (jnp.int32(0), acc, la, lb, largest(la, lb)))
        gate = g_ref[...]
        o_ref[...] = acc
        og_ref[...] = (acc * (gate * _sigmoid(gate))).astype(BF16)
        lt_ref[...] = jnp.where(_iota2((tq, 2), 1) == 0, la, lb)
        done_ref[...] = jnp.full((SUBLANE, LANE), done, F32)

    nq = T // tq
    qblk = pl.BlockSpec((tq, LANE), lambda p, i: (i, p))
    full = pl.BlockSpec((T, LANE), lambda p, i: (0, p))
    return _call(
        body, comm, name=name, grid=(SB_PAIRS, nq),
        in_specs=[qblk, full, full, pl.BlockSpec((tq, LANE), lambda p, i: (i, 3 * SB_PAIRS + p))],
        out_specs=[qblk, qblk, pl.BlockSpec((None, tq, 2), lambda p, i: (p, i, 0)),
                   pl.BlockSpec((None, None, SUBLANE, LANE), lambda p, i: (p, i, 0, 0))],
        out_shape=[jax.ShapeDtypeStruct((T, SB_W), F32), jax.ShapeDtypeStruct((T, SB_W), BF16),
                   jax.ShapeDtypeStruct((SB_PAIRS, T, 2), F32), jax.ShapeDtypeStruct((SB_PAIRS, nq, SUBLANE, LANE), F32)],
        scratch_shapes=[], semantics=("parallel", "parallel"), args=(qn, kn, vb, proj))


def _sb_attn_bwd(qn, kn, vb, dog, o, ltot, done, proj, name, comm=None):
    T = qn.shape[0]
    tq, tk = _tile(T, SB_TQ, SUBLANE), SB_TK

    def body(q_ref, k_ref, v_ref, dog_ref, o_ref, lt_ref, done_ref, g_ref, dq_ref, dk_ref, dv_ref, dgate_ref):
        i = pl.program_id(1)
        first_trip = i - jnp.max(done_ref[...]).astype(jnp.int32)

        @pl.when(i == 0)
        def _():
            dk_ref[...] = jnp.zeros_like(dk_ref)
            dv_ref[...] = jnp.zeros_like(dv_ref)

        ma = _sb_masks()
        gate, o2, dog2 = g_ref[...], o_ref[...], dog_ref[...]
        sg, dsg = _silu_and_grad(gate)
        do2 = dog2 * sg
        dgate_ref[...] = dog2 * o2 * dsg
        lt = lt_ref[...]
        first = _iota2((tq, 2), 1) == 0
        ltots = (jnp.sum(jnp.where(first, lt, 0.0), axis=1, keepdims=True),
                 jnp.sum(jnp.where(first, 0.0, lt), axis=1, keepdims=True))
        q2 = q_ref[...]
        zq = jnp.zeros_like(q2)
        qs = (jnp.where(ma, q2, zq), jnp.where(ma, zq, q2))
        dob = do2.astype(BF16)
        dos = (jnp.where(ma, dob, zq), jnp.where(ma, zq, dob))
        upto = (_iota2((tk, tk), 0) <= _iota2((tk, tk), 1)).astype(BF16)
        before = (_iota2((tk, tk), 0) < _iota2((tk, tk), 1)).astype(BF16)
        qpos = i * tq + _iota2((tq, tk), 0)
        nb = tq // tk

        def trip(kb_bot, masked, carry):
            dq, la, lb, ea, eb = carry
            chains = [(b, h) for b in range(nb) for h in range(2)]
            rows, k2s, v2s, kss, masks = [], [], [], [], []
            for b in range(nb):
                kb = kb_bot + b
                rows.append(pl.ds(pl.multiple_of(kb * tk, tk), tk))
                k2 = k_ref[rows[b], :]
                zk = jnp.zeros_like(k2)
                k2s.append(k2)
                v2s.append(v_ref[rows[b], :])
                kss.append((jnp.where(ma, k2, zk), jnp.where(ma, zk, k2)))
                masks.append((kb * tk + _iota2((tq, tk), 1)) < qpos if masked else None)

            def keep(vals):
                return [jnp.where(masks[b], x, 0.0) for (b, h), x in zip(chains, vals)] if masked else vals

            zs = [_dot(qs[h], k2s[b], NT) for b, h in chains]
            dws = [_dot(dos[h], v2s[b], NT) for b, h in chains]
            ts = [jnp.log(1.0 + jnp.exp(-jnp.abs(z))) for z in zs]
            ls = keep([-(jnp.maximum(z, 0.0) + t) for z, t in zip(zs, ts)])
            lps = [jnp.minimum(z, 0.0) - t for z, t in zip(zs, ts)]
            cums = [_split3_dot(l, upto) for l in ls]
            lsums = [jnp.sum(l, axis=1, keepdims=True) for l in ls]
            offs, tot = {}, [la, lb]
            for b in range(nb):
                for h in range(2):
                    offs[(b, h)] = tot[h]
                    tot[h] = tot[h] + lsums[chains.index((b, h))]
            ws = keep([jnp.exp(lp + (ltots[h] - (offs[(b, h)] + c))) for (b, h), lp, c in zip(chains, lps, cums)])
            es = [dw * w for dw, w in zip(dws, ws)]
            ecums = [_split2_dot(e, before) for e in es]
            esums = [jnp.sum(e, axis=1, keepdims=True) for e in es]
            eoffs, etot = {}, [ea, eb]
            for b in range(nb):
                for h in range(2):
                    eoffs[(b, h)] = etot[h]
                    etot[h] = etot[h] + esums[chains.index((b, h))]
            dzs = keep([e - jnp.exp(lp) * (e + eoffs[ch] + ec) for ch, e, lp, ec in zip(chains, es, lps, ecums)])
            dzs = [dz.astype(BF16) for dz in dzs]
            wbs = [w.astype(BF16) for w in ws]
            for (b, h), dz in zip(chains, dzs):
                dq = dq + _dot(dz, kss[b][h])
            for b in range(nb):
                ia, ib = chains.index((b, 0)), chains.index((b, 1))
                dk_ref[rows[b], :] += _dot(dzs[ia], qs[0], TN) + _dot(dzs[ib], qs[1], TN)
                dv_ref[rows[b], :] += _dot(wbs[ia], dos[0], TN) + _dot(wbs[ib], dos[1], TN)
            return dq, tot[0], tot[1], etot[0], etot[1]

        z1 = jnp.zeros((tq, 1), F32)
        carry = lax.fori_loop(first_trip, i, lambda j, c: trip(j * nb, False, c),
                              (jnp.zeros((tq, LANE), F32), z1, z1, z1, z1))
        dq = trip(i * nb, True, carry)[0]
        dq_ref[...] = dq * (SB_DH ** -0.5)

    qblk = pl.BlockSpec((tq, LANE), lambda p, i: (i, p))
    full = pl.BlockSpec((T, LANE), lambda p, i: (0, p))
    return _call(
        body, comm, name=name, grid=(SB_PAIRS, T // tq),
        in_specs=[qblk, full, full, qblk, qblk, pl.BlockSpec((None, tq, 2), lambda p, i: (p, i, 0)),
                  pl.BlockSpec((None, None, SUBLANE, LANE), lambda p, i: (p, i, 0, 0)),
                  pl.BlockSpec((tq, LANE), lambda p, i: (i, 3 * SB_PAIRS + p))],
        out_specs=[qblk, full, full, qblk],
        out_shape=[jax.ShapeDtypeStruct((T, SB_W), F32)] * 4,
        scratch_shapes=[], semantics=("parallel", "arbitrary"), args=(qn, kn, vb, dog, o, ltot, done, proj))


def _dn_conv(ext, w_ref, cw):
    return (w_ref[3:4, cw] * ext + w_ref[2:3, cw] * _down(ext, 1) + w_ref[1:2, cw] * _down(ext, 2)
            + w_ref[0:1, cw] * _down(ext, 3))


def _dn_prep(pqkv, conv_w, name):
    T, W = pqkv.shape
    tt = _tile(T, WIDE_ROW_TILE, SUBLANE)
    B = DN_PREP_BLK
    nq, nqk = DN_QK_W // B, 2 * DN_QK_W // B

    def body(p_ref, ph_ref, w_ref, o_ref):
        keep = (pl.program_id(0) > 0).astype(F32)
        for cb in range(W // B):
            cw = slice(cb * B, (cb + 1) * B)
            ext = jnp.concatenate([ph_ref[:, cw] * keep, p_ref[:, cw]], axis=0)
            c = _dn_conv(ext, w_ref, cw)[HALO:]
            a = c * _sigmoid(c)
            if cb >= nqk:
                o_ref[:, cw] = a
                continue
            scale = DN_DK ** -0.5 if cb < nq else 1.0
            for hh in range(B // DN_DK):
                ah = a[:, hh * DN_DK:(hh + 1) * DN_DK]
                r = lax.rsqrt(jnp.sum(ah * ah, axis=-1, keepdims=True) + L2_EPS)
                o_ref[:, cb * B + hh * DN_DK:cb * B + (hh + 1) * DN_DK] = ah * (r * scale)

    return pl.pallas_call(
        body, name=name, grid=(T // tt,),
        in_specs=[pl.BlockSpec((tt, W), lambda i: (i, 0)),
                  pl.BlockSpec((HALO, W), lambda i: (jnp.maximum(i * (tt // HALO) - 1, 0), 0)),
                  pl.BlockSpec((DN_CONV, W), lambda i: (0, 0))],
        out_specs=pl.BlockSpec((tt, W), lambda i: (i, 0)),
        out_shape=jax.ShapeDtypeStruct((T, W), F32),
        compiler_params=_params("parallel"),
    )(pqkv, pqkv, conv_w)


def _dn_prep_bwd(pqkv, conv_w, dact, name):
    T, W = pqkv.shape
    tt = _tile(T, WIDE_ROW_TILE, SUBLANE)
    nt = T // tt
    hb = tt // HALO
    B = DN_PREP_BLK
    nq, nqk = DN_QK_W // B, 2 * DN_QK_W // B

    def body(p_ref, pp_ref, pn_ref, w_ref, d_ref, dn_ref, o_ref, dw_ref):
        i = pl.program_id(0)
        keep_p = (i > 0).astype(F32)
        keep_n = (i < nt - 1).astype(F32)
        main = slice(HALO, HALO + tt)
        parts = []
        for cb in range(W // B):
            cw = slice(cb * B, (cb + 1) * B)
            ext = jnp.concatenate([pp_ref[:, cw] * keep_p, p_ref[:, cw], pn_ref[:, cw]], axis=0)
            c = _dn_conv(ext, w_ref, cw)
            s = _sigmoid(c)
            da_dc = s * (1.0 + c * (1.0 - s))
            d_up = jnp.concatenate([jnp.zeros((HALO, B), F32), d_ref[:, cw], dn_ref[:, cw] * keep_n], axis=0)
            if cb < nqk:
                a = c * s
                scale = DN_DK ** -0.5 if cb < nq else 1.0
                normed = []
                for hh in range(B // DN_DK):
                    cols = slice(hh * DN_DK, (hh + 1) * DN_DK)
                    ah = a[:, cols]
                    r = lax.rsqrt(jnp.sum(ah * ah, axis=-1, keepdims=True) + L2_EPS)
                    y = ah * r
                    dy = d_up[:, cols] * scale
                    normed.append(r * (dy - y * jnp.sum(dy * y, axis=-1, keepdims=True)))
                d_up = jnp.concatenate(normed, axis=1)
            dc = d_up * da_dc
            dp = (w_ref[3:4, cw] * dc + w_ref[2:3, cw] * _up(dc, 1) + w_ref[1:2, cw] * _up(dc, 2)
                  + w_ref[0:1, cw] * _up(dc, 3))
            o_ref[:, cw] = dp[main].astype(BF16)
            dcm = dc[main]
            parts.append(jnp.concatenate([jnp.sum(dcm * _down(ext, 3 - k)[main], axis=0, keepdims=True)
                                          for k in range(DN_CONV)], axis=0))
        part = jnp.concatenate(parts, axis=1)

        @pl.when(i == 0)
        def _():
            dw_ref[...] = part

        @pl.when(i > 0)
        def _():
            dw_ref[...] += part

    main_spec = pl.BlockSpec((tt, W), lambda i: (i, 0))
    prev_spec = pl.BlockSpec((HALO, W), lambda i: (jnp.maximum(i * hb - 1, 0), 0))
    next_spec = pl.BlockSpec((HALO, W), lambda i: (jnp.minimum((i + 1) * hb, nt * hb - 1), 0))
    w_spec = pl.BlockSpec((DN_CONV, W), lambda i: (0, 0))
    return pl.pallas_call(
        body, name=name, grid=(nt,),
        in_specs=[main_spec, prev_spec, next_spec, w_spec, main_spec, next_spec],
        out_specs=[main_spec, w_spec],
        out_shape=[jax.ShapeDtypeStruct((T, W), BF16), jax.ShapeDtypeStruct((DN_CONV, W), F32)],
        compiler_params=_params("arbitrary"),
    )(pqkv, pqkv, pqkv, conv_w, dact, dact)


def _dn_gates(a_in, b_in, a_log, dt_bias, name):
    T, H = a_in.shape
    C = DN_CHUNK

    def body(a_ref, b_ref, al_ref, dt_ref, g_ref, beta_ref):
        beta_ref[...] = _sigmoid(b_ref[...])
        g_ref[...] = -jnp.exp(al_ref[...]) * _softplus(a_ref[...] + dt_ref[...])
        tri = (_iota2((C, C), 0) >= _iota2((C, C), 1)).astype(F32)

        def chunk(n, carry):
            rows = pl.ds(pl.multiple_of(n * C, C), C)
            g_ref[rows, :] = _hdot(tri, g_ref[rows, :])
            return carry

        lax.fori_loop(0, T // C, chunk, 0)

    return pl.pallas_call(body, name=name, out_shape=[jax.ShapeDtypeStruct((T, H), F32)] * 2)(a_in, b_in, a_log, dt_bias)


def _dn_gates_bwd(dg, dbeta, a_in, b_in, a_log, dt_bias, name):
    T, H = a_in.shape
    C = DN_CHUNK

    def body(dg_ref, db_ref, a_ref, b_ref, al_ref, dt_ref, da_ref, dbi_ref, dal_ref, ddt_ref):
        tri_t = (_iota2((C, C), 0) <= _iota2((C, C), 1)).astype(F32)

        def chunk(n, carry):
            rows = pl.ds(pl.multiple_of(n * C, C), C)
            da_ref[rows, :] = _hdot(tri_t, dg_ref[rows, :])
            return carry

        lax.fori_loop(0, T // C, chunk, 0)
        dla = da_ref[...]
        x = a_ref[...] + dt_ref[...]
        ea = jnp.exp(al_ref[...])
        da = dla * (-ea) * _sigmoid(x)
        da_ref[...] = da
        dal_ref[...] = jnp.sum(dla * (-ea * _softplus(x)), axis=0, keepdims=True)
        ddt_ref[...] = jnp.sum(da, axis=0, keepdims=True)
        beta = _sigmoid(b_ref[...])
        dbi_ref[...] = db_ref[...] * beta * (1.0 - beta)

    return pl.pallas_call(
        body, name=name,
        out_shape=[jax.ShapeDtypeStruct((T, H), F32)] * 2 + [jax.ShapeDtypeStruct((1, H), F32)] * 2,
    )(dg, dbeta, a_in, b_in, a_log, dt_bias)


def _dn_post(o_raw, pgate, gn, name):
    T = o_raw.shape[0]
    tt = _tile(T, WIDE_ROW_TILE, SUBLANE)

    def body(o_ref, g_ref, gn_ref, out_ref):
        for hh in range(DN_HEADS):
            cols = slice(hh * DN_DV, (hh + 1) * DN_DV)
            o, gate = o_ref[:, cols], g_ref[:, cols]
            r = lax.rsqrt(jnp.mean(o * o, axis=-1, keepdims=True) + RMS_EPS)
            out_ref[:, cols] = (o * r * gn_ref[...] * (gate * _sigmoid(gate))).astype(BF16)

    blk = pl.BlockSpec((tt, DN_V_W), lambda i: (i, 0))
    return pl.pallas_call(
        body, name=name, grid=(T // tt,),
        in_specs=[blk, blk, pl.BlockSpec((1, DN_DV), lambda i: (0, 0))], out_specs=blk,
        out_shape=jax.ShapeDtypeStruct((T, DN_V_W), BF16),
        compiler_params=_params("parallel"),
    )(o_raw, pgate, gn)


def _dn_post_bwd(dog, o_raw, pgate, gn, name):
    T = o_raw.shape[0]
    tt = _tile(T, WIDE_ROW_TILE, SUBLANE)

    def body(d_ref, o_ref, g_ref, gn_ref, do_ref, dgate_ref, dgn_ref):
        gn_v = gn_ref[...]
        part = jnp.zeros((1, DN_DV), F32)
        for hh in range(DN_HEADS):
            cols = slice(hh * DN_DV, (hh + 1) * DN_DV)
            d, o, gate = d_ref[:, cols], o_ref[:, cols], g_ref[:, cols]
            sg, dsg = _silu_and_grad(gate)
            r = lax.rsqrt(jnp.mean(o * o, axis=-1, keepdims=True) + RMS_EPS)
            n = o * r
            dy = d * sg
            dgate_ref[:, cols] = (d * (n * gn_v) * dsg).astype(BF16)
            dn = dy * gn_v
            do_ref[:, cols] = r * (dn - n * jnp.mean(dn * n, axis=-1, keepdims=True))
            part = part + jnp.sum(dy * n, axis=0, keepdims=True)

        @pl.when(pl.program_id(0) == 0)
        def _():
            dgn_ref[...] = part

        @pl.when(pl.program_id(0) > 0)
        def _():
            dgn_ref[...] += part

    blk = pl.BlockSpec((tt, DN_V_W), lambda i: (i, 0))
    vec = pl.BlockSpec((1, DN_DV), lambda i: (0, 0))
    return pl.pallas_call(
        body, name=name, grid=(T // tt,),
        in_specs=[blk, blk, blk, vec], out_specs=[blk, blk, vec],
        out_shape=[jax.ShapeDtypeStruct((T, DN_V_W), F32), jax.ShapeDtypeStruct((T, DN_V_W), BF16),
                   jax.ShapeDtypeStruct((1, DN_DV), F32)],
        compiler_params=_params("arbitrary"),
    )(dog, o_raw, pgate, gn)


def _dn_chunk_terms(q, k, gc, bc):
    C = DN_CHUNK
    r, c = _iota2((C, C), 0), _iota2((C, C), 1)
    lower, strict, eye = r >= c, r > c, r == c
    grow = jnp.sum(jnp.where(eye, gc, 0.0), axis=0, keepdims=True)
    decay = jnp.where(lower, jnp.exp(jnp.where(lower, gc - grow, 0.0)), 0.0)
    last = _iota2((C, 1), 0) == C - 1
    gl = jnp.sum(jnp.where(last, gc, 0.0), axis=0, keepdims=True)
    eg = jnp.exp(gc)
    egl = jnp.exp(gl - gc)
    kb = k * bc
    lmat = jnp.where(strict, _bdot(kb, k, NT) * decay, 0.0)
    aqk = jnp.where(lower, _bdot(q, k, NT) * decay, 0.0)
    return dict(lower=lower, strict=strict, eye=eye, last=last, decay=decay, gl=gl, eg=eg, egl=egl, kb=kb,
                lmat=lmat, aqk=aqk, qd=q * eg, kd=k * egl)


def _split(x):
    hi = x.astype(BF16)
    return hi, (x - hi.astype(F32)).astype(BF16)


def _x3dot(a, b, dims=NN):
    ah, al = a if isinstance(a, tuple) else _split(a)
    bh, bl = b if isinstance(b, tuple) else _split(b)
    return _dot(ah, bh, dims) + (_dot(ah, bl, dims) + _dot(al, bh, dims))


def _interleave(gens):
    for _ in itertools.zip_longest(*gens):
        pass


def _unit_lower_inverse_steps(lmat, eye, out):
    ident = jnp.where(eye, 1.0, 0.0).astype(F32)
    m = -lmat
    inv = ident + m
    for _ in range(int(math.log2(DN_CHUNK)) - 1):
        ms = _split(m)
        m = _x3dot(ms, ms)
        yield
        inv = inv + _x3dot(inv, m)
        yield
    out["tm"] = inv


def _dn_chunk_fwd(act, g, beta, name, comm=None):
    T = act.shape[0]
    C, H = DN_CHUNK, DN_HEADS
    N = T // C

    def body(a_ref, g_ref, b_ref, o_ref, s_out, t_out, vn_out, u_out, w_out, s_scr):
        n = pl.program_id(0)

        @pl.when(n == 0)
        def _():
            s_scr[...] = jnp.zeros_like(s_scr)

        head_lane = _iota2((C, H), 1)

        def head(hh):
            qs, vs = slice(hh * DN_DK, (hh + 1) * DN_DK), slice(hh * DN_DV, (hh + 1) * DN_DV)
            q, k, v = a_ref[:, qs], a_ref[:, DN_QK_W + hh * DN_DK:DN_QK_W + (hh + 1) * DN_DK], \
                a_ref[:, 2 * DN_QK_W + hh * DN_DV:2 * DN_QK_W + (hh + 1) * DN_DV]
            gc = jnp.sum(jnp.where(head_lane == hh, g_ref[...], 0.0), axis=1, keepdims=True)
            bc = jnp.sum(jnp.where(head_lane == hh, b_ref[...], 0.0), axis=1, keepdims=True)
            t = _dn_chunk_terms(q, k, gc, bc)
            yield
            res = {}
            yield from _unit_lower_inverse_steps(t["lmat"], t["eye"], res)
            tms = _split(res["tm"])
            u = _x3dot(tms, v * bc)
            yield
            w = _x3dot(tms, t["kb"] * t["eg"])
            yield
            s = s_scr[hh]
            s_out[hh] = s
            t_out[hh] = res["tm"]
            sb = s.astype(BF16)
            vn = u - _dot(w.astype(BF16), sb)
            yield
            o_ref[:, vs] = _dot(t["qd"].astype(BF16), sb) + _bdot(t["aqk"], vn)
            yield
            s_scr[hh] = s * jnp.exp(t["gl"]) + _bdot(t["kd"], vn, TN)
            vn_out[:, vs] = vn
            u_out[:, vs] = u
            w_out[:, qs] = w

        _interleave([head(hh) for hh in range(H)])

    row = lambda w: pl.BlockSpec((C, w), lambda n: (n, 0))
    return _call(
        body, comm, name=name, grid=(N,),
        in_specs=[row(DN_CONV_W), row(H), row(H)],
        out_specs=[row(DN_V_W),
                   pl.BlockSpec((H, None, DN_DK, DN_DV), lambda n: (0, n, 0, 0)),
                   pl.BlockSpec((H, None, C, C), lambda n: (0, n, 0, 0)),
                   row(DN_V_W), row(DN_V_W), row(DN_QK_W)],
        out_shape=[jax.ShapeDtypeStruct((T, DN_V_W), F32),
                   jax.ShapeDtypeStruct((H, N, DN_DK, DN_DV), F32),
                   jax.ShapeDtypeStruct((H, N, C, C), F32),
                   jax.ShapeDtypeStruct((T, DN_V_W), F32),
                   jax.ShapeDtypeStruct((T, DN_V_W), F32),
                   jax.ShapeDtypeStruct((T, DN_QK_W), F32)],
        scratch_shapes=[pltpu.VMEM((H, DN_DK, DN_DV), F32)], semantics=("arbitrary",), args=(act, g, beta))


def _dn_chunk_bwd(act, g, beta, s_saved, tm_saved, vn_saved, u_saved, w_saved, do, name, comm=None):
    T = act.shape[0]
    C, H = DN_CHUNK, DN_HEADS
    N = T // C

    def body(a_ref, g_ref, b_ref, s_ref, t_ref, vn_ref, u_ref, w_ref, do_ref, da_ref, dg_ref, db_ref, ds_scr):
        @pl.when(pl.program_id(0) == 0)
        def _():
            ds_scr[...] = jnp.zeros_like(ds_scr)

        head_lane = _iota2((C, H), 1)
        dg_cols, db_cols = {}, {}

        def head(hh):
            qs, vs = slice(hh * DN_DK, (hh + 1) * DN_DK), slice(hh * DN_DV, (hh + 1) * DN_DV)
            ks = slice(DN_QK_W + hh * DN_DK, DN_QK_W + (hh + 1) * DN_DK)
            vas = slice(2 * DN_QK_W + hh * DN_DV, 2 * DN_QK_W + (hh + 1) * DN_DV)
            q, k, v = a_ref[:, qs], a_ref[:, ks], a_ref[:, vas]
            gc = jnp.sum(jnp.where(head_lane == hh, g_ref[...], 0.0), axis=1, keepdims=True)
            bc = jnp.sum(jnp.where(head_lane == hh, b_ref[...], 0.0), axis=1, keepdims=True)
            t = _dn_chunk_terms(q, k, gc, bc)
            yield
            lower, strict, eye = t["lower"], t["strict"], t["eye"]
            decay, eg, egl, kb, qd, kd = t["decay"], t["eg"], t["egl"], t["kb"], t["qd"], t["kd"]
            s, tm, vn, u, w, d_o = s_ref[hh], t_ref[hh], vn_ref[:, vs], u_ref[:, vs], w_ref[:, qs], do_ref[:, vs]
            ds_next = ds_scr[hh]
            egl_tot = jnp.exp(t["gl"])
            dob, sb, dsb, vnb = d_o.astype(BF16), s.astype(BF16), ds_next.astype(BF16), vn.astype(BF16)

            dvn = _bdot(t["aqk"], dob, TN) + _bdot(kd, dsb)
            yield
            daqk = jnp.where(lower, _dot(dob, vnb, NT), 0.0)
            dqd = _dot(dob, sb, NT)
            dkd = _dot(vnb, dsb, NT)
            yield
            dvnb = dvn.astype(BF16)
            ds_scr[hh] = _bdot(qd, dob, TN) + egl_tot * ds_next - _bdot(w, dvnb, TN)
            dgl = egl_tot * jnp.sum(jnp.sum(s * ds_next, axis=1, keepdims=True), axis=0, keepdims=True)
            dw = -_dot(dvnb, sb, NT)
            yield
            tms = _split(tm)
            dru = _x3dot(tms, dvn, TN)
            drw = _x3dot(tms, dw, TN)
            yield
            dl = -jnp.where(strict, _x3dot(dru, u, NT) + _x3dot(drw, w, NT), 0.0)
            yield
            dkk = (dl * decay).astype(BF16)
            dqk = (daqk * decay).astype(BF16)
            dkb = _bdot(dkk, k) + drw * eg
            yield
            da_ref[:, ks] = _bdot(dkk, kb, TN) + _bdot(dqk, q, TN) + dkd * egl + dkb * bc
            da_ref[:, qs] = _bdot(dqk, k) + dqd * eg
            da_ref[:, vas] = dru * bc
            yield
            db_cols[hh] = jnp.sum(dru * v, axis=1, keepdims=True) + jnp.sum(dkb * k, axis=1, keepdims=True)
            pm = dl * t["lmat"] + daqk * t["aqk"]
            col_as_col = jnp.sum(jnp.where(eye, jnp.sum(pm, axis=0, keepdims=True), 0.0), axis=1, keepdims=True)
            kdsum = jnp.sum(dkd * kd, axis=1, keepdims=True)
            dgc = (jnp.sum(pm, axis=1, keepdims=True) - col_as_col + jnp.sum(dqd * qd, axis=1, keepdims=True)
                   - kdsum + jnp.sum(drw * (kb * eg), axis=1, keepdims=True))
            dgl = dgl + jnp.sum(kdsum, axis=0, keepdims=True)
            dg_cols[hh] = dgc + jnp.where(t["last"], dgl, 0.0)

        _interleave([head(hh) for hh in range(H)])
        dg_ref[...] = sum(jnp.where(head_lane == hh, dg_cols[hh], 0.0) for hh in range(H))
        db_ref[...] = sum(jnp.where(head_lane == hh, db_cols[hh], 0.0) for hh in range(H))

    row = lambda w: pl.BlockSpec((C, w), lambda n: (N - 1 - n, 0))
    return _call(
        body, comm, name=name, grid=(N,),
        in_specs=[row(DN_CONV_W), row(H), row(H),
                  pl.BlockSpec((H, None, DN_DK, DN_DV), lambda n: (0, N - 1 - n, 0, 0)),
                  pl.BlockSpec((H, None, C, C), lambda n: (0, N - 1 - n, 0, 0)),
                  row(DN_V_W), row(DN_V_W), row(DN_QK_W), row(DN_V_W)],
        out_specs=[row(DN_CONV_W), row(H), row(H)],
        out_shape=[jax.ShapeDtypeStruct((T, DN_CONV_W), F32),
                   jax.ShapeDtypeStruct((T, H), F32), jax.ShapeDtypeStruct((T, H), F32)],
        scratch_shapes=[pltpu.VMEM((H, DN_DK, DN_DV), F32)], semantics=("arbitrary",),
        args=(act, g, beta, s_saved, tm_saved, vn_saved, u_saved, w_saved, do))


def _dn_split_w_in(w):
    wab = jnp.pad(w[:, DN_CONV_W + DN_V_W:], ((0, 0), (0, DN_AB_PAD - 2 * DN_HEADS)))
    return w[:, :DN_CONV_W], w[:, DN_CONV_W:DN_CONV_W + DN_V_W], wab


def _dn_layer_fwd(h, wts, conv_w, a_log, dt_bias, gn, w_out, x_res, tag, comm=None):
    wqkv, wgate, wab = wts
    H = DN_HEADS
    pqkv = _matmul(h, wqkv, "nn", tag + "_pqkv")
    pgate = _matmul(h, wgate, "nn", tag + "_pgate")
    pab = _matmul(h, wab, "nn", tag + "_pab")
    a_in, b_in = pab[:, :H], pab[:, H:2 * H]
    g, beta = _dn_gates(a_in, b_in, a_log, dt_bias, tag + "_gates")
    act = _dn_prep(pqkv, conv_w, tag + "_prep")
    (o_raw, s_sv, tm_sv, vn_sv, u_sv, w_sv), landed = _dn_chunk_fwd(act, g, beta, tag + "_chunk_fwd", comm)
    og = _dn_post(o_raw, pgate, gn, tag + "_post")
    y = _matmul(og, w_out, "nn", tag + "_out", add=x_res)
    saved = dict(h=h, wts=wts, conv_w=conv_w, a_log=a_log, dt_bias=dt_bias, gn=gn, w_out=w_out, pqkv=pqkv, pgate=pgate,
                 a_in=a_in, b_in=b_in, g=g, beta=beta, act=act, o_raw=o_raw, chunk=(s_sv, tm_sv, vn_sv, u_sv, w_sv), og=og)
    return y, saved, landed


def _dn_layer_bwd(dout, sv, tag, comm=None):
    wqkv, wgate, wab = sv["wts"]
    h = sv["h"]
    dog = _matmul(dout, sv["w_out"], "nt", tag + "_dog")
    dw_out = _matmul(sv["og"], dout, "tn", tag + "_dwout")
    do_raw, dgate, dgn = _dn_post_bwd(dog, sv["o_raw"], sv["pgate"], sv["gn"], tag + "_post_bwd")
    (dact, dg, dbeta), landed = _dn_chunk_bwd(sv["act"], sv["g"], sv["beta"], *sv["chunk"], do_raw, tag + "_chunk_bwd", comm)
    da_in, db_in, da_log, ddt = _dn_gates_bwd(dg, dbeta, sv["a_in"], sv["b_in"], sv["a_log"], sv["dt_bias"],
                                              tag + "_gates_bwd")
    dpqkv, dconv = _dn_prep_bwd(sv["pqkv"], sv["conv_w"], dact, tag + "_prep_bwd")
    dpab = jnp.pad(jnp.concatenate([da_in, db_in], axis=1), ((0, 0), (0, DN_AB_PAD - 2 * DN_HEADS)))
    dwqkv = _matmul(h, dpqkv, "tn", tag + "_dwqkv")
    dwgate = _matmul(h, dgate, "tn", tag + "_dwgate")
    dwab = _matmul(h, dpab, "tn", tag + "_dwab")
    dh = _matmul(dpqkv, wqkv, "nt", tag + "_dh0")
    dh = _matmul(dgate, wgate, "nt", tag + "_dh1", add=dh)
    dh = _matmul(dpab, wab, "nt", tag + "_dh2", add=dh)
    dw_in = jnp.concatenate([dwqkv, dwgate, dwab[:, :2 * DN_HEADS]], axis=1)
    return dh, dict(dn_w_in=dw_in, dn_conv_w=dconv, dn_a_log=da_log, dn_dt_bias=ddt, dn_o_norm_g=dgn, dn_w_out=dw_out), landed


def _sb_layer_fwd(h, w_in, qg, kg, w_out, x_res, tag, comm=None):
    qg2, kg2 = jnp.tile(qg, (1, 2)), jnp.tile(kg, (1, 2))
    proj = _matmul(h, w_in, "nn", tag + "_proj")
    qn, kn, vb = _sb_prep(proj, qg2, kg2, tag + "_prep")
    (o, og, ltot, done), landed = _sb_attn_fwd(qn, kn, vb, proj, tag + "_attn_fwd", comm)
    y = _matmul(og, w_out, "nn", tag + "_out", add=x_res)
    saved = dict(h=h, w_in=w_in, qg2=qg2, kg2=kg2, w_out=w_out, proj=proj, qn=qn, kn=kn, vb=vb, o=o, og=og, ltot=ltot,
                 done=done)
    return y, saved, landed


def _sb_layer_bwd(dout, sv, tag, comm=None):
    dog = _matmul(dout, sv["w_out"], "nt", tag + "_dog")
    dw_out = _matmul(sv["og"], dout, "tn", tag + "_dwout")
    (dqn, dkn, dv, dgate), landed = _sb_attn_bwd(sv["qn"], sv["kn"], sv["vb"], dog, sv["o"], sv["ltot"], sv["done"],
                                                 sv["proj"], tag + "_attn_bwd", comm)
    dproj, dqgp, dkgp = _sb_prep_bwd(sv["proj"], dqn, dkn, dv, dgate, sv["qg2"], sv["kg2"], tag + "_prep_bwd")
    dw_in = _matmul(sv["h"], dproj, "tn", tag + "_dwin")
    dh = _matmul(dproj, sv["w_in"], "nt", tag + "_dh")
    dqg = _fold_heads(dqgp, tag + "_dqg")
    dkg = _fold_heads(dkgp, tag + "_dkg")
    return dh, dict(sb_w_in=dw_in, sb_q_norm_g=dqg, sb_k_norm_g=dkg, sb_w_out=dw_out), landed


def _sc_layer_fwd(h, w_in, conv_w, w_out, x_res, tag):
    proj = _matmul(h, w_in, "nn", tag + "_proj")
    yg = _sc_fwd(proj, conv_w, tag + "_fwd")
    y = _matmul(yg, w_out, "nn", tag + "_out", add=x_res)
    return y, dict(h=h, w_in=w_in, conv_w=conv_w, w_out=w_out, proj=proj, yg=yg)


def _sc_layer_bwd(dout, sv, tag):
    dyg = _matmul(dout, sv["w_out"], "nt", tag + "_dyg")
    dw_out = _matmul(sv["yg"], dout, "tn", tag + "_dwout")
    dproj, dconv = _sc_bwd(dyg, sv["proj"], sv["conv_w"], tag + "_bwd")
    dw_in = _matmul(sv["h"], dproj, "tn", tag + "_dwin")
    dh = _matmul(dproj, sv["w_in"], "nt", tag + "_dh")
    return dh, dict(sc_w_in=dw_in, sc_conv_w=dconv, sc_w_out=dw_out)


def _adamw(w, m, v, parts, name):
    R, C = w.shape
    tr = _tile(R, 128, SUBLANE)

    def body(w_ref, m_ref, v_ref, p_ref, g_ref, d_ref, nm_ref, nv_ref):
        g = p_ref[0].astype(F32)
        for s in range(1, N_DEV):
            g = g + p_ref[s].astype(F32)
        m2 = ADAM_B1 * m_ref[...] + (1.0 - ADAM_B1) * g
        v2 = ADAM_B2 * v_ref[...] + (1.0 - ADAM_B2) * (g * g)
        m_hat = m2 / (1.0 - ADAM_B1 ** ADAM_STEP)
        v_hat = v2 / (1.0 - ADAM_B2 ** ADAM_STEP)
        g_ref[...] = g
        d_ref[...] = -ADAM_LR * (m_hat / (jnp.sqrt(v_hat) + ADAM_EPS) + ADAM_WD * w_ref[...])
        nm_ref[...] = m2
        nv_ref[...] = v2

    blk = pl.BlockSpec((tr, C), lambda i: (i, 0))
    return pl.pallas_call(
        body, name=name, grid=(R // tr,),
        in_specs=[blk, blk, blk, pl.BlockSpec((N_DEV, tr, C), lambda i: (0, i, 0))],
        out_specs=[blk] * 4, out_shape=[jax.ShapeDtypeStruct((R, C), F32)] * 4,
        compiler_params=_params("parallel"),
    )(w, m, v, parts)


_HBM = pl.BlockSpec(memory_space=pltpu.HBM)
_MESH = pl.DeviceIdType.MESH


def _slot(x, y, c):
    return 4 * x + 2 * y + c


class _Gather:
    def __init__(self, shards):
        self.arrays = list(shards)
        n = len(self.arrays)
        self.out_shapes = [jax.ShapeDtypeStruct((N_DEV,) + s.shape, s.dtype) for s in self.arrays]
        self.scratch = [pltpu.SemaphoreType.DMA((n, N_DEV - 1)), pltpu.SemaphoreType.DMA((n, N_DEV - 1)),
                        pltpu.SemaphoreType.DMA((n,))]

    def _parts(self, ins, outs, sems):
        send_sems, recv_sems, local_sems = sems
        n = len(self.arrays)
        x, y, c = lax.axis_index("x"), lax.axis_index("y"), lax.axis_index("c")
        me, sibling = (x, y, c), (x, y, 1 - c)
        chips = [(1 - x, y), (x, 1 - y), (1 - x, 1 - y)]

        def copy(a, k, block, to, src=None):
            dst = outs[a].at[_slot(*block)]
            return pltpu.make_async_remote_copy(src_ref=dst if src is None else src, dst_ref=dst,
                                                send_sem=send_sems.at[a, k], recv_sem=recv_sems.at[a, k],
                                                device_id=to, device_id_type=_MESH)

        mine = [pltpu.make_async_copy(ins[a], outs[a].at[_slot(*me)], local_sems.at[a]) for a in range(n)]
        first = []
        for a in range(n):
            first.append(copy(a, 0, me, sibling, src=ins[a]))
            first += [copy(a, 1 + j, me, (*chip, c), src=ins[a]) for j, chip in enumerate(chips)]
        return n, c, me, sibling, chips, copy, mine, first

    def start(self, ins, outs, sems):
        _, _, _, _, _, _, mine, first = self._parts(ins, outs, sems)
        for cp in mine + first:
            cp.start()

    def finish(self, ins, outs, sems):
        n, c, me, sibling, chips, copy, mine, first = self._parts(ins, outs, sems)
        passed = []
        for j, chip in enumerate(chips):
            for a in range(n):
                copy(a, 1 + j, (*chip, c), me).wait_recv()
                fwd = copy(a, 4 + j, (*chip, c), sibling)
                fwd.start()
                passed.append(fwd)
        for a in range(n):
            copy(a, 0, sibling, me).wait_recv()
            for j, chip in enumerate(chips):
                copy(a, 4 + j, (*chip, 1 - c), me).wait_recv()
        for cp in first + passed:
            cp.wait_send()
        for cp in mine:
            cp.wait()


class _Exchange:
    def __init__(self, arrays, scatter):
        self.arrays, self.scatter = list(arrays), list(scatter)
        n = len(self.arrays)
        shapes = [a.shape[1:] if s else a.shape for a, s in zip(self.arrays, self.scatter)]
        self.out_shapes = [jax.ShapeDtypeStruct((N_DEV,) + tuple(s), a.dtype) for s, a in zip(shapes, self.arrays)]
        self.scratch = [pltpu.SemaphoreType.DMA((n, N_DEV - 1)), pltpu.SemaphoreType.DMA((n, N_DEV - 1)),
                        pltpu.SemaphoreType.DMA((n,))]

    def _copies(self, ins, outs, sems):
        send_sems, recv_sems, local_sems = sems
        n, scatter = len(self.arrays), self.scatter
        x, y, c = lax.axis_index("x"), lax.axis_index("y"), lax.axis_index("c")
        me = _slot(x, y, c)
        copies = [pltpu.make_async_copy(ins[a].at[me] if scatter[a] else ins[a], outs[a].at[me], local_sems.at[a])
                  for a in range(n)]
        for r in range(1, N_DEV):
            px = 1 - x if r & 4 else x
            py = 1 - y if r & 2 else y
            pc = 1 - c if r & 1 else c
            for a in range(n):
                copies.append(pltpu.make_async_remote_copy(
                    src_ref=ins[a].at[_slot(px, py, pc)] if scatter[a] else ins[a], dst_ref=outs[a].at[me],
                    send_sem=send_sems.at[a, r - 1], recv_sem=recv_sems.at[a, r - 1],
                    device_id=(px, py, pc), device_id_type=_MESH))
        return copies

    def start(self, ins, outs, sems):
        for cp in self._copies(ins, outs, sems):
            cp.start()

    def finish(self, ins, outs, sems):
        for cp in self._copies(ins, outs, sems):
            cp.wait()


def _comm_call(comm, name):
    n = len(comm.arrays)

    def body(*refs):
        ins, outs, sems = refs[:n], refs[n:2 * n], refs[2 * n:]
        comm.start(ins, outs, sems)
        comm.finish(ins, outs, sems)

    return pl.pallas_call(body, name=name, in_specs=[_HBM] * n, out_specs=[_HBM] * n, out_shape=comm.out_shapes,
                          scratch_shapes=comm.scratch)(*comm.arrays)


def _call(body, comm, *, name, grid, in_specs, out_specs, out_shape, scratch_shapes, semantics, args):
    if comm is None:
        outs = pl.pallas_call(body, name=name, grid=grid, in_specs=in_specs, out_specs=out_specs, out_shape=out_shape,
                              scratch_shapes=scratch_shapes, compiler_params=_params(*semantics))(*args)
        return outs, []
    n_in, n_out, n_scr, n_c = len(in_specs), len(out_specs), len(scratch_shapes), len(comm.arrays)

    def fused(*refs):
        ins, refs = refs[:n_in], refs[n_in:]
        c_ins, refs = refs[:n_c], refs[n_c:]
        outs, refs = refs[:n_out], refs[n_out:]
        c_outs, refs = refs[:n_c], refs[n_c:]
        scr, sems = refs[:n_scr], refs[n_scr:]
        ids = [pl.program_id(d) for d in range(len(grid))]
        first = functools.reduce(jnp.logical_and, [i == 0 for i in ids])
        last = functools.reduce(jnp.logical_and, [i == g - 1 for i, g in zip(ids, grid)])

        @pl.when(first)
        def _():
            comm.start(c_ins, c_outs, sems)

        body(*ins, *outs, *scr)

        @pl.when(last)
        def _():
            comm.finish(c_ins, c_outs, sems)

    outs = pl.pallas_call(
        fused, name=name, grid=grid, in_specs=list(in_specs) + [_HBM] * n_c, out_specs=list(out_specs) + [_HBM] * n_c,
        out_shape=list(out_shape) + comm.out_shapes, scratch_shapes=list(scratch_shapes) + comm.scratch,
        compiler_params=_params(*["arbitrary"] * len(grid)))(*args, *comm.arrays)
    return outs[:n_out], outs[n_out:]


_LAYER_WEIGHTS = {
    0: (("dn_w_in", 0), ("dn_conv_w", 0), ("dn_o_norm_g", 0), ("dn_w_out", 0)),
    1: (("sb_w_in", 0), ("sb_w_out", 0)),
    2: (("sc_w_in", 0), ("sc_conv_w", 0), ("sc_w_out", 0)),
    3: (("dn_w_in", 1), ("dn_conv_w", 1), ("dn_o_norm_g", 1), ("dn_w_out", 1)),
}
_MATMUL_WEIGHTS = ("dn_w_in", "dn_w_out", "sb_w_in", "sb_w_out", "sc_w_in", "sc_w_out")
_COLUMN_SHARDED = ("dn_w_in", "dn_conv_w", "dn_o_norm_g", "sb_w_in", "sc_w_in", "sc_conv_w")
_REPLICATED = ("norm_g", "dn_a_log", "dn_dt_bias", "sb_q_norm_g", "sb_k_norm_g")
_ORDER = ("norm_g", "dn_w_in", "dn_conv_w", "dn_a_log", "dn_dt_bias", "dn_o_norm_g", "dn_w_out", "sb_w_in", "sb_q_norm_g",
          "sb_k_norm_g", "sb_w_out", "sc_w_in", "sc_conv_w", "sc_w_out")
_PACK_COLS = D_MODEL


def _layer_keys(layers):
    return [key for layer in layers for key in _LAYER_WEIGHTS[layer]]


def _as_2d(a):
    return a.reshape(1, -1) if a.ndim == 1 else a


def _assemble(name, gathered):
    n, r, c = gathered.shape
    if name in _COLUMN_SHARDED:
        return jnp.moveaxis(gathered, 0, 1).reshape(r, n * c)
    return gathered.reshape(n * r, c)


def _disassemble(name, full):
    r, c = full.shape
    if name in _COLUMN_SHARDED:
        return jnp.moveaxis(full.reshape(r, N_DEV, c // N_DEV), 1, 0)
    return full.reshape(N_DEV, r // N_DEV, c)


def _pack_replicated(d):
    rows = [d["norm_g"]]
    for name in _REPLICATED[1:]:
        flat = d[name].reshape(1, -1)
        rows.append(jnp.pad(flat, ((0, 0), (0, _PACK_COLS - flat.shape[1]))))
    return jnp.concatenate(rows, axis=0)


def _unpack_replicated(p, like):
    out = {"norm_g": p[:4]}
    for r, name in enumerate(_REPLICATED[1:]):
        shape = like[name].shape
        out[name] = p[4 + r, :math.prod(shape)].reshape(shape)
    return out


def kernel(x, norm_g, dn_w_in, dn_conv_w, dn_a_log, dn_dt_bias, dn_o_norm_g, dn_w_out, sb_w_in, sb_q_norm_g, sb_k_norm_g, sb_w_out, sc_w_in, sc_conv_w, sc_w_out, loss_target, m_norm_g, m_dn_w_in, m_dn_conv_w, m_dn_a_log, m_dn_dt_bias, m_dn_o_norm_g, m_dn_w_out, m_sb_w_in, m_sb_q_norm_g, m_sb_k_norm_g, m_sb_w_out, m_sc_w_in, m_sc_conv_w, m_sc_w_out, v_norm_g, v_dn_w_in, v_dn_conv_w, v_dn_a_log, v_dn_dt_bias, v_dn_o_norm_g, v_dn_w_out, v_sb_w_in, v_sb_q_norm_g, v_sb_k_norm_g, v_sb_w_out, v_sc_w_in, v_sc_conv_w, v_sc_w_out):
    w = dict(norm_g=norm_g, dn_w_in=dn_w_in, dn_conv_w=dn_conv_w, dn_a_log=dn_a_log, dn_dt_bias=dn_dt_bias,
             dn_o_norm_g=dn_o_norm_g, dn_w_out=dn_w_out, sb_w_in=sb_w_in, sb_q_norm_g=sb_q_norm_g, sb_k_norm_g=sb_k_norm_g,
             sb_w_out=sb_w_out, sc_w_in=sc_w_in, sc_conv_w=sc_conv_w, sc_w_out=sc_w_out)
    m = dict(norm_g=m_norm_g, dn_w_in=m_dn_w_in, dn_conv_w=m_dn_conv_w, dn_a_log=m_dn_a_log, dn_dt_bias=m_dn_dt_bias,
             dn_o_norm_g=m_dn_o_norm_g, dn_w_out=m_dn_w_out, sb_w_in=m_sb_w_in, sb_q_norm_g=m_sb_q_norm_g,
             sb_k_norm_g=m_sb_k_norm_g, sb_w_out=m_sb_w_out, sc_w_in=m_sc_w_in, sc_conv_w=m_sc_conv_w, sc_w_out=m_sc_w_out)
    v = dict(norm_g=v_norm_g, dn_w_in=v_dn_w_in, dn_conv_w=v_dn_conv_w, dn_a_log=v_dn_a_log, dn_dt_bias=v_dn_dt_bias,
             dn_o_norm_g=v_dn_o_norm_g, dn_w_out=v_dn_w_out, sb_w_in=v_sb_w_in, sb_q_norm_g=v_sb_q_norm_g,
             sb_k_norm_g=v_sb_k_norm_g, sb_w_out=v_sb_w_out, sc_w_in=v_sc_w_in, sc_conv_w=v_sc_conv_w, sc_w_out=v_sc_w_out)

    def gather_of(layers):
        return _Gather([_as_2d(w[k][j]).astype(BF16) if k in _MATMUL_WEIGHTS else _as_2d(w[k][j]) for k, j in _layer_keys(layers)])

    def full_weights(layers, gathered):
        return {key: _assemble(key[0], g) for key, g in zip(_layer_keys(layers), gathered)}

    def exchange_of(layers, grads, extra=()):
        out = [_disassemble(k, grads[layer][k].astype(BF16) if k in _MATMUL_WEIGHTS else grads[layer][k])
               for layer in layers for k, _ in _LAYER_WEIGHTS[layer]]
        return _Exchange(out + list(extra), [True] * len(out) + [False] * len(extra))

    F = full_weights((0,), _comm_call(gather_of((0,)), "gather_layer0"))
    xs, saves = [x[0]], []
    h = _rmsnorm_fwd(xs[0], norm_g[0:1], "norm0")
    y, sv, got = _dn_layer_fwd(h, _dn_split_w_in(F["dn_w_in", 0]), F["dn_conv_w", 0], dn_a_log[0:1], dn_dt_bias[0:1],
                               F["dn_o_norm_g", 0], F["dn_w_out", 0], xs[0], "dn0", gather_of((1,)))
    F.update(full_weights((1,), got))
    xs.append(y)
    saves.append(sv)
    h = _rmsnorm_fwd(xs[1], norm_g[1:2], "norm1")
    y, sv, got = _sb_layer_fwd(h, F["sb_w_in", 0], sb_q_norm_g, sb_k_norm_g, F["sb_w_out", 0], xs[1], "sb", gather_of((2, 3)))
    F.update(full_weights((2, 3), got))
    xs.append(y)
    saves.append(sv)
    h = _rmsnorm_fwd(xs[2], norm_g[2:3], "norm2")
    y, sv = _sc_layer_fwd(h, F["sc_w_in", 0], F["sc_conv_w", 0], F["sc_w_out", 0], xs[2], "sc")
    xs.append(y)
    saves.append(sv)
    h = _rmsnorm_fwd(xs[3], norm_g[3:4], "norm3")
    y, sv, _ = _dn_layer_fwd(h, _dn_split_w_in(F["dn_w_in", 1]), F["dn_conv_w", 1], dn_a_log[1:2], dn_dt_bias[1:2],
                             F["dn_o_norm_g", 1], F["dn_w_out", 1], xs[3], "dn1")
    xs.append(y)
    saves.append(sv)
    dx, loss_part = _loss_head(xs[4], loss_target[0])

    G, dnorm, landed = {}, [None] * 4, {}
    dh, G[3], _ = _dn_layer_bwd(dx, saves[3], "dn1")
    dx, dnorm[3] = _rmsnorm_bwd(dh, xs[3], norm_g[3:4], dx, "norm3_bwd")
    dh, G[2] = _sc_layer_bwd(dx, saves[2], "sc")
    dx, dnorm[2] = _rmsnorm_bwd(dh, xs[2], norm_g[2:3], dx, "norm2_bwd")
    dh, G[1], got = _sb_layer_bwd(dx, saves[1], "sb", exchange_of((3, 2), G))
    landed.update(zip(_layer_keys((3, 2)), got))
    dx, dnorm[1] = _rmsnorm_bwd(dh, xs[1], norm_g[1:2], dx, "norm1_bwd")
    dh, G[0], got = _dn_layer_bwd(dx, saves[0], "dn0", exchange_of((1,), G))
    landed.update(zip(_layer_keys((1,)), got))
    dx, dnorm[0] = _rmsnorm_bwd(dh, xs[0], norm_g[0:1], dx, "norm0_bwd")
    replicated = dict(norm_g=jnp.concatenate(dnorm, axis=0),
                      dn_a_log=jnp.concatenate([G[0]["dn_a_log"], G[3]["dn_a_log"]], axis=0),
                      dn_dt_bias=jnp.concatenate([G[0]["dn_dt_bias"], G[3]["dn_dt_bias"]], axis=0),
                      sb_q_norm_g=G[1]["sb_q_norm_g"], sb_k_norm_g=G[1]["sb_k_norm_g"])
    got = _comm_call(exchange_of((0,), G, extra=[_pack_replicated(replicated)]), "exchange_layer0")
    landed.update(zip(_layer_keys((0,)), got[:-1]))

    res = {}
    for k in _ORDER:
        if k in _REPLICATED:
            continue
        per_layer = []
        for j in range(w[k].shape[0]):
            shape = w[k][j].shape
            outs = _adamw(_as_2d(w[k][j]), _as_2d(m[k][j]), _as_2d(v[k][j]), landed[k, j], f"adamw_{k}{j}")
            per_layer.append([o.reshape(shape) for o in outs])
        res[k] = [jnp.stack([layer[i] for layer in per_layer], axis=0) for i in range(4)]
    outs = _adamw(_pack_replicated(w), _pack_replicated(m), _pack_replicated(v), got[-1], "adamw_replicated")
    unpacked = [_unpack_replicated(o, w) for o in outs]
    for k in _REPLICATED:
        res[k] = [u[k] for u in unpacked]

    loss = lax.psum(loss_part[0, 0], ("x", "y", "c"))
    return (loss, dx[None]) + tuple(res[k][0] for k in _ORDER) + tuple(res[k][1] for k in _ORDER) \
        + tuple(res[k][2] for k in _ORDER) + tuple(res[k][3] for k in _ORDER)
```

```python
import functools
import itertools
import math

import jax
import jax.numpy as jnp
from jax import lax
from jax.experimental import pallas as pl
from jax.experimental.pallas import tpu as pltpu

F32 = jnp.float32
BF16 = jnp.bfloat16
HIGHEST = lax.Precision.HIGHEST

N_DEV = 8
D_MODEL = 1024
RMS_EPS = 1e-6
L2_EPS = 1e-6

DN_HEADS = 8
DN_DK = 128
DN_DV = 256
DN_QK_W = DN_HEADS * DN_DK
DN_V_W = DN_HEADS * DN_DV
DN_CONV = 4
DN_CHUNK = 64
DN_CONV_W = 2 * DN_QK_W + DN_V_W
DN_IN = DN_CONV_W + DN_V_W + 2 * DN_HEADS
DN_AB_PAD = 128
DN_PREP_BLK = 512

SB_HEADS = 16
SB_DH = 64
SB_W = SB_HEADS * SB_DH
SB_PAIRS = SB_HEADS // 2
SB_TQ = 256
SB_TK = 128
SB_DEAD = -106.0

SC_W = 2 * D_MODEL
SC_CONV = 3
SC_BLK = 512
SC_NBLK = SC_W // SC_BLK

ADAM_LR = 0.001
ADAM_B1 = 0.9
ADAM_B2 = 0.999
ADAM_EPS = 1e-08
ADAM_WD = 0.01
ADAM_STEP = 10

LANE = 128
SUBLANE = 8
HALO = SUBLANE
ROW_TILE = 256
WIDE_ROW_TILE = 128
CONV_ROW_TILE = 256
VMEM_LIMIT = 48 * 2 ** 20

NN = ((1,), (0,))
NT = ((1,), (1,))
TN = ((0,), (0,))


def _dot(a, b, dims=NN, precision=None):
    return lax.dot_general(a, b, (dims, ((), ())), precision=precision, preferred_element_type=F32)


def _bdot(a, b, dims=NN):
    return _dot(a.astype(BF16), b.astype(BF16), dims)


def _hdot(a, b, dims=NN):
    return _dot(a, b, dims, precision=HIGHEST)


def _tile(dim, pref, align=LANE):
    t = (min(pref, dim) // align) * align
    while t >= align:
        if dim % t == 0:
            return t
        t -= align
    return dim


def _params(*sem):
    return pltpu.CompilerParams(dimension_semantics=sem, vmem_limit_bytes=VMEM_LIMIT)


def _sigmoid(x):
    return 0.5 * jnp.tanh(0.5 * x) + 0.5


def _softplus(x):
    return jnp.maximum(x, 0.0) + jnp.log(1.0 + jnp.exp(-jnp.abs(x)))


def _silu_and_grad(x):
    s = _sigmoid(x)
    return x * s, s * (1.0 + x * (1.0 - s))


def _iota2(shape, dim):
    return lax.broadcasted_iota(jnp.int32, shape, dim)


def _matmul(a, b, mode, name, out_dtype=F32, add=None, tm=1024, tn=1024, tk=1024):
    if mode == "nn":
        (M, K), (K2, N) = a.shape, b.shape
    elif mode == "nt":
        (M, K), (N, K2) = a.shape, b.shape
    else:
        (K, M), (K2, N) = a.shape, b.shape
    assert K == K2, (a.shape, b.shape, mode)
    tm, tn, tk = _tile(M, tm), _tile(N, tn), _tile(K, tk)
    nk = K // tk
    dims = {"nn": NN, "nt": NT, "tn": TN}[mode]
    a_spec = pl.BlockSpec((tk, tm), lambda i, j, k: (k, i)) if mode == "tn" else pl.BlockSpec((tm, tk), lambda i, j, k: (i, k))
    b_spec = pl.BlockSpec((tn, tk), lambda i, j, k: (j, k)) if mode == "nt" else pl.BlockSpec((tk, tn), lambda i, j, k: (k, j))
    o_spec = pl.BlockSpec((tm, tn), lambda i, j, k: (i, j))
    has_add = add is not None

    def body(*refs):
        a_ref, b_ref = refs[0], refs[1]
        add_ref = refs[2] if has_add else None
        o_ref = refs[3] if has_add else refs[2]
        p = _bdot(a_ref[...], b_ref[...], dims)

        def finish(acc):
            if has_add:
                acc = acc + add_ref[...]
            o_ref[...] = acc.astype(out_dtype)

        if nk == 1:
            finish(p)
        else:
            acc_ref = refs[-1]
            k = pl.program_id(2)

            @pl.when(k == 0)
            def _():
                acc_ref[...] = p

            @pl.when(k > 0)
            def _():
                acc_ref[...] += p

            @pl.when(k == nk - 1)
            def _():
                finish(acc_ref[...])

    in_specs = [a_spec, b_spec] + ([o_spec] if has_add else [])
    args = (a, b) + ((add,) if has_add else ())
    return pl.pallas_call(
        body, name=name, grid=(M // tm, N // tn, nk),
        in_specs=in_specs, out_specs=o_spec,
        out_shape=jax.ShapeDtypeStruct((M, N), out_dtype),
        scratch_shapes=[pltpu.VMEM((tm, tn), F32)] if nk > 1 else [],
        compiler_params=_params("parallel", "parallel", "arbitrary"),
    )(*args)


def _rmsnorm_fwd(x, g, name):
    T, D = x.shape
    tt = _tile(T, 512, SUBLANE)

    def body(x_ref, g_ref, o_ref):
        xv = x_ref[...]
        r = lax.rsqrt(jnp.mean(xv * xv, axis=-1, keepdims=True) + RMS_EPS)
        o_ref[...] = (xv * r * g_ref[...]).astype(BF16)

    return pl.pallas_call(
        body, name=name, grid=(T // tt,),
        in_specs=[pl.BlockSpec((tt, D), lambda i: (i, 0)), pl.BlockSpec((1, D), lambda i: (0, 0))],
        out_specs=pl.BlockSpec((tt, D), lambda i: (i, 0)),
        out_shape=jax.ShapeDtypeStruct((T, D), BF16),
        compiler_params=_params("parallel"),
    )(x, g)


def _rmsnorm_bwd(dh, x, g, dx_res, name):
    T, D = x.shape
    tt = _tile(T, 256, SUBLANE)

    def body(dh_ref, x_ref, g_ref, res_ref, dx_ref, dg_ref):
        xv, dhv = x_ref[...], dh_ref[...]
        r = lax.rsqrt(jnp.mean(xv * xv, axis=-1, keepdims=True) + RMS_EPS)
        xh = xv * r
        dxh = dhv * g_ref[...]
        m = jnp.mean(dxh * xh, axis=-1, keepdims=True)
        dx_ref[...] = res_ref[...] + r * (dxh - xh * m)
        part = jnp.sum(dhv * xh, axis=0, keepdims=True)

        @pl.when(pl.program_id(0) == 0)
        def _():
            dg_ref[...] = part

        @pl.when(pl.program_id(0) > 0)
        def _():
            dg_ref[...] += part

    row = pl.BlockSpec((tt, D), lambda i: (i, 0))
    vec = pl.BlockSpec((1, D), lambda i: (0, 0))
    return pl.pallas_call(
        body, name=name, grid=(T // tt,),
        in_specs=[row, row, vec, row], out_specs=[row, vec],
        out_shape=[jax.ShapeDtypeStruct((T, D), F32), jax.ShapeDtypeStruct((1, D), F32)],
        compiler_params=_params("arbitrary"),
    )(dh, x, g, dx_res)


def _loss_head(y, target, name="loss_head"):
    T, D = y.shape
    tt = _tile(T, 512, SUBLANE)

    def body(y_ref, t_ref, dy_ref, l_ref):
        e = y_ref[...] - t_ref[...]
        dy_ref[...] = e * (1.0 / D)
        s = jnp.sum(jnp.sum(e * e, axis=1, keepdims=True), axis=0, keepdims=True) * (0.5 / D)
        s = jnp.broadcast_to(s, (1, LANE))

        @pl.when(pl.program_id(0) == 0)
        def _():
            l_ref[...] = s

        @pl.when(pl.program_id(0) > 0)
        def _():
            l_ref[...] += s

    row = pl.BlockSpec((tt, D), lambda i: (i, 0))
    return pl.pallas_call(
        body, name=name, grid=(T // tt,),
        in_specs=[row, row], out_specs=[row, pl.BlockSpec((1, LANE), lambda i: (0, 0))],
        out_shape=[jax.ShapeDtypeStruct((T, D), F32), jax.ShapeDtypeStruct((1, LANE), F32)],
        compiler_params=_params("arbitrary"),
    )(y, target)


def _down(x, k):
    return pltpu.roll(x, k, 0) if k else x


def _up(x, k):
    return pltpu.roll(x, x.shape[0] - k, 0) if k else x


def _sc_fwd(proj, conv_w, name):
    T = proj.shape[0]
    tt = _tile(T, WIDE_ROW_TILE, SUBLANE)
    B = SC_BLK

    def body(p_ref, ph_ref, w_ref, o_ref):
        keep = (pl.program_id(0) > 0).astype(F32)
        for j in range(SC_NBLK):
            cb, cc, cu, cg = (slice(k * SC_W + j * B, k * SC_W + (j + 1) * B) for k in range(4))
            cw = slice(j * B, (j + 1) * B)
            z = jnp.concatenate([ph_ref[:, cc] * ph_ref[:, cu] * keep, p_ref[:, cc] * p_ref[:, cu]], axis=0)
            cz = (w_ref[2:3, cw] * z + w_ref[1:2, cw] * _down(z, 1) + w_ref[0:1, cw] * _down(z, 2))[HALO:]
            gate = p_ref[:, cg]
            o_ref[:, cw] = (p_ref[:, cb] * cz * (gate * _sigmoid(gate))).astype(BF16)

    return pl.pallas_call(
        body, name=name, grid=(T // tt,),
        in_specs=[pl.BlockSpec((tt, 4 * SC_W), lambda i: (i, 0)),
                  pl.BlockSpec((HALO, 4 * SC_W), lambda i: (jnp.maximum(i * (tt // HALO) - 1, 0), 0)),
                  pl.BlockSpec((SC_CONV, SC_W), lambda i: (0, 0))],
        out_specs=pl.BlockSpec((tt, SC_W), lambda i: (i, 0)),
        out_shape=jax.ShapeDtypeStruct((T, SC_W), BF16),
        compiler_params=_params("parallel"),
    )(proj, proj, conv_w)


def _sc_bwd(dyg, proj, conv_w, name):
    T = proj.shape[0]
    tt = _tile(T, WIDE_ROW_TILE, SUBLANE)
    nt = T // tt
    B = SC_BLK
    hb = tt // HALO

    def body(d_ref, dn_ref, p_ref, pp_ref, pn_ref, w_ref, o_ref, dw_ref):
        i = pl.program_id(0)
        keep_p = (i > 0).astype(F32)
        keep_n = (i < nt - 1).astype(F32)
        main = slice(HALO, HALO + tt)
        parts = []
        for j in range(SC_NBLK):
            cw = slice(j * B, (j + 1) * B)

            def ext(k):
                s = slice(k * SC_W + j * B, k * SC_W + (j + 1) * B)
                return s, jnp.concatenate([pp_ref[:, s] * keep_p, p_ref[:, s], pn_ref[:, s]], axis=0)

            (sb, b), (sc, c), (su, u), (sg_, gate) = ext(0), ext(1), ext(2), ext(3)
            dyg_e = jnp.concatenate([jnp.zeros((HALO, B), F32), d_ref[:, cw], dn_ref[:, cw] * keep_n], axis=0)
            w0, w1, w2 = w_ref[0:1, cw], w_ref[1:2, cw], w_ref[2:3, cw]
            z = c * u
            z1, z2 = _down(z, 1), _down(z, 2)
            cz = w2 * z + w1 * z1 + w0 * z2
            sg, dsg = _silu_and_grad(gate)
            dy = dyg_e * sg
            dcz = dy * b
            dz = w2 * dcz + w1 * _up(dcz, 1) + w0 * _up(dcz, 2)
            o_ref[:, sb] = (dy * cz)[main].astype(BF16)
            o_ref[:, sc] = (dz * u)[main].astype(BF16)
            o_ref[:, su] = (dz * c)[main].astype(BF16)
            o_ref[:, sg_] = (dyg_e * (b * cz) * dsg)[main].astype(BF16)
            dcm = dcz[main]
            parts.append(jnp.concatenate([jnp.sum(dcm * z2[main], axis=0, keepdims=True),
                                          jnp.sum(dcm * z1[main], axis=0, keepdims=True),
                                          jnp.sum(dcm * z[main], axis=0, keepdims=True)], axis=0))
        part = jnp.concatenate(parts, axis=1)

        @pl.when(i == 0)
        def _():
            dw_ref[...] = part

        @pl.when(i > 0)
        def _():
            dw_ref[...] += part

    nxt = lambda i: (jnp.minimum((i + 1) * hb, nt * hb - 1), 0)
    return pl.pallas_call(
        body, name=name, grid=(nt,),
        in_specs=[pl.BlockSpec((tt, SC_W), lambda i: (i, 0)),
                  pl.BlockSpec((HALO, SC_W), nxt),
                  pl.BlockSpec((tt, 4 * SC_W), lambda i: (i, 0)),
                  pl.BlockSpec((HALO, 4 * SC_W), lambda i: (jnp.maximum(i * hb - 1, 0), 0)),
                  pl.BlockSpec((HALO, 4 * SC_W), nxt),
                  pl.BlockSpec((SC_CONV, SC_W), lambda i: (0, 0))],
        out_specs=[pl.BlockSpec((tt, 4 * SC_W), lambda i: (i, 0)), pl.BlockSpec((SC_CONV, SC_W), lambda i: (0, 0))],
        out_shape=[jax.ShapeDtypeStruct((T, 4 * SC_W), BF16), jax.ShapeDtypeStruct((SC_CONV, SC_W), F32)],
        compiler_params=_params("arbitrary"),
    )(dyg, dyg, proj, proj, proj, conv_w)


def _split3_dot(x, m):
    hi = x.astype(BF16)
    r1 = x - hi.astype(F32)
    mid = r1.astype(BF16)
    lo = (r1 - mid.astype(F32)).astype(BF16)
    return _dot(hi, m) + _dot(mid, m) + _dot(lo, m)


def _split2_dot(x, m):
    hi = x.astype(BF16)
    lo = (x - hi.astype(F32)).astype(BF16)
    return _dot(hi, m) + _dot(lo, m)


def _head_mean_matrix():
    r, c = _iota2((LANE, LANE), 0), _iota2((LANE, LANE), 1)
    return jnp.where((r // SB_DH) == (c // SB_DH), 1.0 / SB_DH, 0.0).astype(BF16)


def _sb_prep(proj, qg2, kg2, name):
    T = proj.shape[0]
    tt = _tile(T, WIDE_ROW_TILE, SUBLANE)

    def body(p_ref, qg_ref, kg_ref, q_ref, k_ref, v_ref):
        bd = _head_mean_matrix()

        def norm(x, g, scale):
            r = lax.rsqrt(_split3_dot(x * x, bd) + RMS_EPS)
            return (x * r * g * scale).astype(BF16)

        v_ref[...] = p_ref[:, 2 * SB_W:3 * SB_W].astype(BF16)
        for p in range(SB_PAIRS):
            cols = slice(p * LANE, (p + 1) * LANE)
            q_ref[:, cols] = norm(p_ref[:, cols], qg_ref[...], SB_DH ** -0.5)
            k_ref[:, cols] = norm(p_ref[:, SB_W + p * LANE:SB_W + (p + 1) * LANE], kg_ref[...], 1.0)

    blk = pl.BlockSpec((tt, SB_W), lambda i: (i, 0))
    vec = pl.BlockSpec((1, LANE), lambda i: (0, 0))
    return pl.pallas_call(
        body, name=name, grid=(T // tt,),
        in_specs=[pl.BlockSpec((tt, 4 * SB_W), lambda i: (i, 0)), vec, vec],
        out_specs=[blk, blk, blk],
        out_shape=[jax.ShapeDtypeStruct((T, SB_W), BF16)] * 3,
        compiler_params=_params("parallel"),
    )(proj, qg2, kg2)


def _sb_prep_bwd(proj, dqn, dkn, dv, dgate, qg2, kg2, name):
    T = proj.shape[0]
    tt = _tile(T, WIDE_ROW_TILE, SUBLANE)

    def body(p_ref, dq_ref, dk_ref, dv_ref, dg_ref, qg_ref, kg_ref, o_ref, dqg_ref, dkg_ref):
        i = pl.program_id(0)
        bd = _head_mean_matrix()

        def norm_bwd(x, g, dy):
            r = lax.rsqrt(_split3_dot(x * x, bd) + RMS_EPS)
            xh = x * r
            dxh = dy * g
            m = _split3_dot(dxh * xh, bd)
            return r * (dxh - xh * m), jnp.sum(dy * xh, axis=0, keepdims=True)

        o_ref[:, 2 * SB_W:3 * SB_W] = dv_ref[...].astype(BF16)
        o_ref[:, 3 * SB_W:4 * SB_W] = dg_ref[...].astype(BF16)
        pq = pk = jnp.zeros((1, LANE), F32)
        for p in range(SB_PAIRS):
            cols, kcols = slice(p * LANE, (p + 1) * LANE), slice(SB_W + p * LANE, SB_W + (p + 1) * LANE)
            dxq, sq = norm_bwd(p_ref[:, cols], qg_ref[...], dq_ref[:, cols])
            dxk, sk = norm_bwd(p_ref[:, kcols], kg_ref[...], dk_ref[:, cols])
            o_ref[:, cols] = dxq.astype(BF16)
            o_ref[:, kcols] = dxk.astype(BF16)
            pq, pk = pq + sq, pk + sk

        @pl.when(i == 0)
        def _():
            dqg_ref[...] = pq
            dkg_ref[...] = pk

        @pl.when(i > 0)
        def _():
            dqg_ref[...] += pq
            dkg_ref[...] += pk

    blk = pl.BlockSpec((tt, SB_W), lambda i: (i, 0))
    vec = pl.BlockSpec((1, LANE), lambda i: (0, 0))
    wide = pl.BlockSpec((tt, 4 * SB_W), lambda i: (i, 0))
    return pl.pallas_call(
        body, name=name, grid=(T // tt,),
        in_specs=[wide, blk, blk, blk, blk, vec, vec],
        out_specs=[wide, vec, vec],
        out_shape=[jax.ShapeDtypeStruct((T, 4 * SB_W), BF16)] + [jax.ShapeDtypeStruct((1, LANE), F32)] * 2,
        compiler_params=_params("arbitrary"),
    )(proj, dqn, dkn, dv, dgate, qg2, kg2)


def _fold_heads(part, name):
    def body(p_ref, o_ref):
        r, c = _iota2((LANE, SB_DH), 0), _iota2((LANE, SB_DH), 1)
        fold = jnp.where((r % SB_DH) == c, 1.0, 0.0).astype(F32)
        o_ref[...] = jnp.sum(_hdot(p_ref[...], fold), axis=0, keepdims=True)

    return pl.pallas_call(body, name=name, out_shape=jax.ShapeDtypeStruct((1, SB_DH), F32))(part)


def _sb_masks():
    lane = _iota2((1, LANE), 1)
    return lane < SB_DH


def _sb_attn_fwd(qn, kn, vb, proj, name, comm=None):
    T = qn.shape[0]
    tq, tk = _tile(T, SB_TQ, SUBLANE), SB_TK
    assert tq % tk == 0

    def body(q_ref, k_ref, v_ref, g_ref, o_ref, og_ref, lt_ref, done_ref):
        i = pl.program_id(1)
        ma = _sb_masks()
        q2 = q_ref[...]
        zero = jnp.zeros_like(q2)
        qs = (jnp.where(ma, q2, zero), jnp.where(ma, zero, q2))
        upper = (_iota2((tk, tk), 0) > _iota2((tk, tk), 1)).astype(BF16)
        qpos = i * tq + _iota2((tq, tk), 0)
        nb = tq // tk

        def trip(kb_top, masked, carry):
            acc, la, lb = carry
            chains = [(b, h) for b in range(nb) for h in range(2)]
            k2s, vss, masks = [], [], []
            for b in range(nb):
                kb = kb_top - b
                rows = pl.ds(pl.multiple_of(kb * tk, tk), tk)
                k2s.append(k_ref[rows, :])
                v2 = v_ref[rows, :]
                zv = jnp.zeros_like(v2)
                vss.append((jnp.where(ma, v2, zv), jnp.where(ma, zv, v2)))
                masks.append((kb * tk + _iota2((tq, tk), 1)) < qpos if masked else None)
            zs = [_dot(qs[h], k2s[b], NT) for b, h in chains]
            ts = [jnp.log(1.0 + jnp.exp(-jnp.abs(z))) for z in zs]
            ls = [-(jnp.maximum(z, 0.0) + t) for z, t in zip(zs, ts)]
            if masked:
                ls = [jnp.where(masks[b], l, 0.0) for (b, h), l in zip(chains, ls)]
            cums = [_split2_dot(l, upper) for l in ls]
            sums = [jnp.sum(l, axis=1, keepdims=True) for l in ls]
            offs, tot = {}, [la, lb]
            for b in range(nb):
                for h in range(2):
                    offs[(b, h)] = tot[h]
                    tot[h] = tot[h] + sums[chains.index((b, h))]
            ws = [jnp.exp(jnp.minimum(z, 0.0) - t + c + offs[ch]) for ch, z, t, c in zip(chains, zs, ts, cums)]
            if masked:
                ws = [jnp.where(masks[b], w, 0.0) for (b, h), w in zip(chains, ws)]
            for (b, h), w in zip(chains, ws):
                acc = acc + _dot(w.astype(BF16), vss[b][h])
            return acc, tot[0], tot[1]

        def largest(la, lb):
            return jnp.max(jnp.maximum(la, lb))

        z1 = jnp.zeros((tq, 1), F32)
        acc, la, lb = trip((i + 1) * nb - 1, True, (jnp.zeros((tq, LANE), F32), z1, z1))

        def live(c):
            return (c[0] < i) & (c[4] > SB_DEAD)

        def more(c):
            j, acc, la, lb, _ = c
            acc, la, lb = trip((i - j) * nb - 1, False, (acc, la, lb))
            return j + 1, acc, la, lb, largest(la, lb)

        done, acc, la, lb, _ = lax.while_loop(live, more, (jnp.int32(0), acc, la, lb, largest(la, lb)))
        gate = g_ref[...]
        o_ref[...] = acc
        og_ref[...] = (acc * (gate * _sigmoid(gate))).astype(BF16)
        lt_ref[...] = jnp.where(_iota2((tq, 2), 1) == 0, la, lb)
        done_ref[...] = jnp.full((SUBLANE, LANE), done, F32)

    nq = T // tq
    qblk = pl.BlockSpec((tq, LANE), lambda p, i: (i, p))
    full = pl.BlockSpec((T, LANE), lambda p, i: (0, p))
    return _call(
        body, comm, name=name, grid=(SB_PAIRS, nq),
        in_specs=[qblk, full, full, pl.BlockSpec((tq, LANE), lambda p, i: (i, 3 * SB_PAIRS + p))],
        out_specs=[qblk, qblk, pl.BlockSpec((None, tq, 2), lambda p, i: (p, i, 0)),
                   pl.BlockSpec((None, None, SUBLANE, LANE), lambda p, i: (p, i, 0, 0))],
        out_shape=[jax.ShapeDtypeStruct((T, SB_W), F32), jax.ShapeDtypeStruct((T, SB_W), BF16),
                   jax.ShapeDtypeStruct((SB_PAIRS, T, 2), F32), jax.ShapeDtypeStruct((SB_PAIRS, nq, SUBLANE, LANE), F32)],
        scratch_shapes=[], semantics=("parallel", "parallel"), args=(qn, kn, vb, proj))


def _sb_attn_bwd(qn, kn, vb, dog, o, ltot, done, proj, name, comm=None):
    T = qn.shape[0]
    tq, tk = _tile(T, SB_TQ, SUBLANE), SB_TK

    def body(q_ref, k_ref, v_ref, dog_ref, o_ref, lt_ref, done_ref, g_ref, dq_ref, dk_ref, dv_ref, dgate_ref):
        i = pl.program_id(1)
        first_trip = i - jnp.max(done_ref[...]).astype(jnp.int32)

        @pl.when(i == 0)
        def _():
            dk_ref[...] = jnp.zeros_like(dk_ref)
            dv_ref[...] = jnp.zeros_like(dv_ref)

        ma = _sb_masks()
        gate, o2, dog2 = g_ref[...], o_ref[...], dog_ref[...]
        sg, dsg = _silu_and_grad(gate)
        do2 = dog2 * sg
        dgate_ref[...] = dog2 * o2 * dsg
        lt = lt_ref[...]
        first = _iota2((tq, 2), 1) == 0
        ltots = (jnp.sum(jnp.where(first, lt, 0.0), axis=1, keepdims=True),
                 jnp.sum(jnp.where(first, 0.0, lt), axis=1, keepdims=True))
        q2 = q_ref[...]
        zq = jnp.zeros_like(q2)
        qs = (jnp.where(ma, q2, zq), jnp.where(ma, zq, q2))
        dob = do2.astype(BF16)
        dos = (jnp.where(ma, dob, zq), jnp.where(ma, zq, dob))
        upto = (_iota2((tk, tk), 0) <= _iota2((tk, tk), 1)).astype(BF16)
        before = (_iota2((tk, tk), 0) < _iota2((tk, tk), 1)).astype(BF16)
        qpos = i * tq + _iota2((tq, tk), 0)
        nb = tq // tk

        def trip(kb_bot, masked, carry):
            dq, la, lb, ea, eb = carry
            chains = [(b, h) for b in range(nb) for h in range(2)]
            rows, k2s, v2s, kss, masks = [], [], [], [], []
            for b in range(nb):
                kb = kb_bot + b
                rows.append(pl.ds(pl.multiple_of(kb * tk, tk), tk))
                k2 = k_ref[rows[b], :]
                zk = jnp.zeros_like(k2)
                k2s.append(k2)
                v2s.append(v_ref[rows[b], :])
                kss.append((jnp.where(ma, k2, zk), jnp.where(ma, zk, k2)))
                masks.append((kb * tk + _iota2((tq, tk), 1)) < qpos if masked else None)

            def keep(vals):
                return [jnp.where(masks[b], x, 0.0) for (b, h), x in zip(chains, vals)] if masked else vals

            zs = [_dot(qs[h], k2s[b], NT) for b, h in chains]
            dws = [_dot(dos[h], v2s[b], NT) for b, h in chains]
            ts = [jnp.log(1.0 + jnp.exp(-jnp.abs(z))) for z in zs]
            ls = keep([-(jnp.maximum(z, 0.0) + t) for z, t in zip(zs, ts)])
            lps = [jnp.minimum(z, 0.0) - t for z, t in zip(zs, ts)]
            cums = [_split3_dot(l, upto) for l in ls]
            lsums = [jnp.sum(l, axis=1, keepdims=True) for l in ls]
            offs, tot = {}, [la, lb]
            for b in range(nb):
                for h in range(2):
                    offs[(b, h)] = tot[h]
                    tot[h] = tot[h] + lsums[chains.index((b, h))]
            ws = keep([jnp.exp(lp + (ltots[h] - (offs[(b, h)] + c))) for (b, h), lp, c in zip(chains, lps, cums)])
            es = [dw * w for dw, w in zip(dws, ws)]
            ecums = [_split2_dot(e, before) for e in es]
            esums = [jnp.sum(e, axis=1, keepdims=True) for e in es]
            eoffs, etot = {}, [ea, eb]
            for b in range(nb):
                for h in range(2):
                    eoffs[(b, h)] = etot[h]
                    etot[h] = etot[h] + esums[chains.index((b, h))]
            dzs = keep([e - jnp.exp(lp) * (e + eoffs[ch] + ec) for ch, e, lp, ec in zip(chains, es, lps, ecums)])
            dzs = [dz.astype(BF16) for dz in dzs]
            wbs = [w.astype(BF16) for w in ws]
            for (b, h), dz in zip(chains, dzs):
                dq = dq + _dot(dz, kss[b][h])
            for b in range(nb):
                ia, ib = chains.index((b, 0)), chains.index((b, 1))
                dk_ref[rows[b], :] += _dot(dzs[ia], qs[0], TN) + _dot(dzs[ib], qs[1], TN)
                dv_ref[rows[b], :] += _dot(wbs[ia], dos[0], TN) + _dot(wbs[ib], dos[1], TN)
            return dq, tot[0], tot[1], etot[0], etot[1]

        z1 = jnp.zeros((tq, 1), F32)
        carry = lax.fori_loop(first_trip, i, lambda j, c: trip(j * nb, False, c),
                              (jnp.zeros((tq, LANE), F32), z1, z1, z1, z1))
        dq = trip(i * nb, True, carry)[0]
        dq_ref[...] = dq * (SB_DH ** -0.5)

    qblk = pl.BlockSpec((tq, LANE), lambda p, i: (i, p))
    full = pl.BlockSpec((T, LANE), lambda p, i: (0, p))
    return _call(
        body, comm, name=name, grid=(SB_PAIRS, T // tq),
        in_specs=[qblk, full, full, qblk, qblk, pl.BlockSpec((None, tq, 2), lambda p, i: (p, i, 0)),
                  pl.BlockSpec((None, None, SUBLANE, LANE), lambda p, i: (p, i, 0, 0)),
                  pl.BlockSpec((tq, LANE), lambda p, i: (i, 3 * SB_PAIRS + p))],
        out_specs=[qblk, full, full, qblk],
        out_shape=[jax.ShapeDtypeStruct((T, SB_W), F32)] * 4,
        scratch_shapes=[], semantics=("parallel", "arbitrary"), args=(qn, kn, vb, dog, o, ltot, done, proj))


def _dn_conv(ext, w_ref, cw):
    return (w_ref[3:4, cw] * ext + w_ref[2:3, cw] * _down(ext, 1) + w_ref[1:2, cw] * _down(ext, 2)
            + w_ref[0:1, cw] * _down(ext, 3))


def _dn_prep(pqkv, conv_w, name):
    T, W = pqkv.shape
    tt = _tile(T, CONV_ROW_TILE, SUBLANE)
    B = DN_PREP_BLK
    nq, nqk = DN_QK_W // B, 2 * DN_QK_W // B

    def body(p_ref, ph_ref, w_ref, o_ref):
        keep = (pl.program_id(0) > 0).astype(F32)
        for cb in range(W // B):
            cw = slice(cb * B, (cb + 1) * B)
            ext = jnp.concatenate([ph_ref[:, cw] * keep, p_ref[:, cw]], axis=0)
            c = _dn_conv(ext, w_ref, cw)[HALO:]
            a = c * _sigmoid(c)
            if cb >= nqk:
                o_ref[:, cw] = a
                continue
            scale = DN_DK ** -0.5 if cb < nq else 1.0
            for hh in range(B // DN_DK):
                ah = a[:, hh * DN_DK:(hh + 1) * DN_DK]
                r = lax.rsqrt(jnp.sum(ah * ah, axis=-1, keepdims=True) + L2_EPS)
                o_ref[:, cb * B + hh * DN_DK:cb * B + (hh + 1) * DN_DK] = ah * (r * scale)

    return pl.pallas_call(
        body, name=name, grid=(T // tt,),
        in_specs=[pl.BlockSpec((tt, W), lambda i: (i, 0)),
                  pl.BlockSpec((HALO, W), lambda i: (jnp.maximum(i * (tt // HALO) - 1, 0), 0)),
                  pl.BlockSpec((DN_CONV, W), lambda i: (0, 0))],
        out_specs=pl.BlockSpec((tt, W), lambda i: (i, 0)),
        out_shape=jax.ShapeDtypeStruct((T, W), F32),
        compiler_params=_params("parallel"),
    )(pqkv, pqkv, conv_w)


def _dn_prep_bwd(pqkv, conv_w, dact, name):
    T, W = pqkv.shape
    tt = _tile(T, CONV_ROW_TILE, SUBLANE)
    nt = T // tt
    hb = tt // HALO
    B = DN_PREP_BLK
    nq, nqk = DN_QK_W // B, 2 * DN_QK_W // B

    def body(p_ref, pp_ref, pn_ref, w_ref, d_ref, dn_ref, o_ref, dw_ref):
        i = pl.program_id(0)
        keep_p = (i > 0).astype(F32)
        keep_n = (i < nt - 1).astype(F32)
        main = slice(HALO, HALO + tt)
        parts = []
        for cb in range(W // B):
            cw = slice(cb * B, (cb + 1) * B)
            ext = jnp.concatenate([pp_ref[:, cw] * keep_p, p_ref[:, cw], pn_ref[:, cw]], axis=0)
            c = _dn_conv(ext, w_ref, cw)
            s = _sigmoid(c)
            da_dc = s * (1.0 + c * (1.0 - s))
            d_up = jnp.concatenate([jnp.zeros((HALO, B), F32), d_ref[:, cw], dn_ref[:, cw] * keep_n], axis=0)
            if cb < nqk:
                a = c * s
                scale = DN_DK ** -0.5 if cb < nq else 1.0
                normed = []
                for hh in range(B // DN_DK):
                    cols = slice(hh * DN_DK, (hh + 1) * DN_DK)
                    ah = a[:, cols]
                    r = lax.rsqrt(jnp.sum(ah * ah, axis=-1, keepdims=True) + L2_EPS)
                    y = ah * r
                    dy = d_up[:, cols] * scale
                    normed.append(r * (dy - y * jnp.sum(dy * y, axis=-1, keepdims=True)))
                d_up = jnp.concatenate(normed, axis=1)
            dc = d_up * da_dc
            dp = (w_ref[3:4, cw] * dc + w_ref[2:3, cw] * _up(dc, 1) + w_ref[1:2, cw] * _up(dc, 2)
                  + w_ref[0:1, cw] * _up(dc, 3))
            o_ref[:, cw] = dp[main].astype(BF16)
            dcm = dc[main]
            parts.append(jnp.concatenate([jnp.sum(dcm * _down(ext, 3 - k)[main], axis=0, keepdims=True)
                                          for k in range(DN_CONV)], axis=0))
        part = jnp.concatenate(parts, axis=1)

        @pl.when(i == 0)
        def _():
            dw_ref[...] = part

        @pl.when(i > 0)
        def _():
            dw_ref[...] += part

    main_spec = pl.BlockSpec((tt, W), lambda i: (i, 0))
    prev_spec = pl.BlockSpec((HALO, W), lambda i: (jnp.maximum(i * hb - 1, 0), 0))
    next_spec = pl.BlockSpec((HALO, W), lambda i: (jnp.minimum((i + 1) * hb, nt * hb - 1), 0))
    w_spec = pl.BlockSpec((DN_CONV, W), lambda i: (0, 0))
    return pl.pallas_call(
        body, name=name, grid=(nt,),
        in_specs=[main_spec, prev_spec, next_spec, w_spec, main_spec, next_spec],
        out_specs=[main_spec, w_spec],
        out_shape=[jax.ShapeDtypeStruct((T, W), BF16), jax.ShapeDtypeStruct((DN_CONV, W), F32)],
        compiler_params=_params("arbitrary"),
    )(pqkv, pqkv, pqkv, conv_w, dact, dact)


def _dn_gates(a_in, b_in, a_log, dt_bias, name):
    T, H = a_in.shape
    C = DN_CHUNK

    def body(a_ref, b_ref, al_ref, dt_ref, g_ref, beta_ref):
        beta_ref[...] = _sigmoid(b_ref[...])
        g_ref[...] = -jnp.exp(al_ref[...]) * _softplus(a_ref[...] + dt_ref[...])
        tri = (_iota2((C, C), 0) >= _iota2((C, C), 1)).astype(F32)

        def chunk(n, carry):
            rows = pl.ds(pl.multiple_of(n * C, C), C)
            g_ref[rows, :] = _hdot(tri, g_ref[rows, :])
            return carry

        lax.fori_loop(0, T // C, chunk, 0)

    return pl.pallas_call(body, name=name, out_shape=[jax.ShapeDtypeStruct((T, H), F32)] * 2)(a_in, b_in, a_log, dt_bias)


def _dn_gates_bwd(dg, dbeta, a_in, b_in, a_log, dt_bias, name):
    T, H = a_in.shape
    C = DN_CHUNK

    def body(dg_ref, db_ref, a_ref, b_ref, al_ref, dt_ref, da_ref, dbi_ref, dal_ref, ddt_ref):
        tri_t = (_iota2((C, C), 0) <= _iota2((C, C), 1)).astype(F32)

        def chunk(n, carry):
            rows = pl.ds(pl.multiple_of(n * C, C), C)
            da_ref[rows, :] = _hdot(tri_t, dg_ref[rows, :])
            return carry

        lax.fori_loop(0, T // C, chunk, 0)
        dla = da_ref[...]
        x = a_ref[...] + dt_ref[...]
        ea = jnp.exp(al_ref[...])
        da = dla * (-ea) * _sigmoid(x)
        da_ref[...] = da
        dal_ref[...] = jnp.sum(dla * (-ea * _softplus(x)), axis=0, keepdims=True)
        ddt_ref[...] = jnp.sum(da, axis=0, keepdims=True)
        beta = _sigmoid(b_ref[...])
        dbi_ref[...] = db_ref[...] * beta * (1.0 - beta)

    return pl.pallas_call(
        body, name=name,
        out_shape=[jax.ShapeDtypeStruct((T, H), F32)] * 2 + [jax.ShapeDtypeStruct((1, H), F32)] * 2,
    )(dg, dbeta, a_in, b_in, a_log, dt_bias)


def _dn_post(o_raw, pgate, gn, name):
    T = o_raw.shape[0]
    tt = _tile(T, WIDE_ROW_TILE, SUBLANE)

    def body(o_ref, g_ref, gn_ref, out_ref):
        for hh in range(DN_HEADS):
            cols = slice(hh * DN_DV, (hh + 1) * DN_DV)
            o, gate = o_ref[:, cols], g_ref[:, cols]
            r = lax.rsqrt(jnp.mean(o * o, axis=-1, keepdims=True) + RMS_EPS)
            out_ref[:, cols] = (o * r * gn_ref[...] * (gate * _sigmoid(gate))).astype(BF16)

    blk = pl.BlockSpec((tt, DN_V_W), lambda i: (i, 0))
    return pl.pallas_call(
        body, name=name, grid=(T // tt,),
        in_specs=[blk, blk, pl.BlockSpec((1, DN_DV), lambda i: (0, 0))], out_specs=blk,
        out_shape=jax.ShapeDtypeStruct((T, DN_V_W), BF16),
        compiler_params=_params("parallel"),
    )(o_raw, pgate, gn)


def _dn_post_bwd(dog, o_raw, pgate, gn, name):
    T = o_raw.shape[0]
    tt = _tile(T, WIDE_ROW_TILE, SUBLANE)

    def body(d_ref, o_ref, g_ref, gn_ref, do_ref, dgate_ref, dgn_ref):
        gn_v = gn_ref[...]
        part = jnp.zeros((1, DN_DV), F32)
        for hh in range(DN_HEADS):
            cols = slice(hh * DN_DV, (hh + 1) * DN_DV)
            d, o, gate = d_ref[:, cols], o_ref[:, cols], g_ref[:, cols]
            sg, dsg = _silu_and_grad(gate)
            r = lax.rsqrt(jnp.mean(o * o, axis=-1, keepdims=True) + RMS_EPS)
            n = o * r
            dy = d * sg
            dgate_ref[:, cols] = (d * (n * gn_v) * dsg).astype(BF16)
            dn = dy * gn_v
            do_ref[:, cols] = r * (dn - n * jnp.mean(dn * n, axis=-1, keepdims=True))
            part = part + jnp.sum(dy * n, axis=0, keepdims=True)

        @pl.when(pl.program_id(0) == 0)
        def _():
            dgn_ref[...] = part

        @pl.when(pl.program_id(0) > 0)
        def _():
            dgn_ref[...] += part

    blk = pl.BlockSpec((tt, DN_V_W), lambda i: (i, 0))
    vec = pl.BlockSpec((1, DN_DV), lambda i: (0, 0))
    return pl.pallas_call(
        body, name=name, grid=(T // tt,),
        in_specs=[blk, blk, blk, vec], out_specs=[blk, blk, vec],
        out_shape=[jax.ShapeDtypeStruct((T, DN_V_W), F32), jax.ShapeDtypeStruct((T, DN_V_W), BF16),
                   jax.ShapeDtypeStruct((1, DN_DV), F32)],
        compiler_params=_params("arbitrary"),
    )(dog, o_raw, pgate, gn)


def _dn_chunk_terms(q, k, gc, bc):
    C = DN_CHUNK
    r, c = _iota2((C, C), 0), _iota2((C, C), 1)
    lower, strict, eye = r >= c, r > c, r == c
    grow = jnp.sum(jnp.where(eye, gc, 0.0), axis=0, keepdims=True)
    decay = jnp.where(lower, jnp.exp(jnp.where(lower, gc - grow, 0.0)), 0.0)
    last = _iota2((C, 1), 0) == C - 1
    gl = jnp.sum(jnp.where(last, gc, 0.0), axis=0, keepdims=True)
    eg = jnp.exp(gc)
    egl = jnp.exp(gl - gc)
    kb = k * bc
    lmat = jnp.where(strict, _bdot(kb, k, NT) * decay, 0.0)
    aqk = jnp.where(lower, _bdot(q, k, NT) * decay, 0.0)
    return dict(lower=lower, strict=strict, eye=eye, last=last, decay=decay, gl=gl, eg=eg, egl=egl, kb=kb,
                lmat=lmat, aqk=aqk, qd=q * eg, kd=k * egl)


def _split(x):
    hi = x.astype(BF16)
    return hi, (x - hi.astype(F32)).astype(BF16)


def _x3dot(a, b, dims=NN):
    ah, al = a if isinstance(a, tuple) else _split(a)
    bh, bl = b if isinstance(b, tuple) else _split(b)
    return _dot(ah, bh, dims) + (_dot(ah, bl, dims) + _dot(al, bh, dims))


def _interleave(gens):
    for _ in itertools.zip_longest(*gens):
        pass


def _unit_lower_inverse_steps(lmat, eye, out):
    ident = jnp.where(eye, 1.0, 0.0).astype(F32)
    m = -lmat
    inv = ident + m
    for _ in range(int(math.log2(DN_CHUNK)) - 1):
        ms = _split(m)
        m = _x3dot(ms, ms)
        yield
        inv = inv + _x3dot(inv, m)
        yield
    out["tm"] = inv


def _dn_chunk_fwd(act, g, beta, name, comm=None):
    T = act.shape[0]
    C, H = DN_CHUNK, DN_HEADS
    N = T // C

    def body(a_ref, g_ref, b_ref, o_ref, s_out, t_out, vn_out, u_out, w_out, s_scr):
        n = pl.program_id(0)

        @pl.when(n == 0)
        def _():
            s_scr[...] = jnp.zeros_like(s_scr)

        head_lane = _iota2((C, H), 1)

        def head(hh):
            qs, vs = slice(hh * DN_DK, (hh + 1) * DN_DK), slice(hh * DN_DV, (hh + 1) * DN_DV)
            q, k, v = a_ref[:, qs], a_ref[:, DN_QK_W + hh * DN_DK:DN_QK_W + (hh + 1) * DN_DK], \
                a_ref[:, 2 * DN_QK_W + hh * DN_DV:2 * DN_QK_W + (hh + 1) * DN_DV]
            gc = jnp.sum(jnp.where(head_lane == hh, g_ref[...], 0.0), axis=1, keepdims=True)
            bc = jnp.sum(jnp.where(head_lane == hh, b_ref[...], 0.0), axis=1, keepdims=True)
            t = _dn_chunk_terms(q, k, gc, bc)
            yield
            res = {}
            yield from _unit_lower_inverse_steps(t["lmat"], t["eye"], res)
            tms = _split(res["tm"])
            u = _x3dot(tms, v * bc)
            yield
            w = _x3dot(tms, t["kb"] * t["eg"])
            yield
            s = s_scr[hh]
            s_out[hh] = s
            t_out[hh] = res["tm"]
            sb = s.astype(BF16)
            vn = u - _dot(w.astype(BF16), sb)
            yield
            o_ref[:, vs] = _dot(t["qd"].astype(BF16), sb) + _bdot(t["aqk"], vn)
            yield
            s_scr[hh] = s * jnp.exp(t["gl"]) + _bdot(t["kd"], vn, TN)
            vn_out[:, vs] = vn
            u_out[:, vs] = u
            w_out[:, qs] = w

        _interleave([head(hh) for hh in range(H)])

    row = lambda w: pl.BlockSpec((C, w), lambda n: (n, 0))
    return _call(
        body, comm, name=name, grid=(N,),
        in_specs=[row(DN_CONV_W), row(H), row(H)],
        out_specs=[row(DN_V_W),
                   pl.BlockSpec((H, None, DN_DK, DN_DV), lambda n: (0, n, 0, 0)),
                   pl.BlockSpec((H, None, C, C), lambda n: (0, n, 0, 0)),
                   row(DN_V_W), row(DN_V_W), row(DN_QK_W)],
        out_shape=[jax.ShapeDtypeStruct((T, DN_V_W), F32),
                   jax.ShapeDtypeStruct((H, N, DN_DK, DN_DV), F32),
                   jax.ShapeDtypeStruct((H, N, C, C), F32),
                   jax.ShapeDtypeStruct((T, DN_V_W), F32),
                   jax.ShapeDtypeStruct((T, DN_V_W), F32),
                   jax.ShapeDtypeStruct((T, DN_QK_W), F32)],
        scratch_shapes=[pltpu.VMEM((H, DN_DK, DN_DV), F32)], semantics=("arbitrary",), args=(act, g, beta))


def _dn_chunk_bwd(act, g, beta, s_saved, tm_saved, vn_saved, u_saved, w_saved, do, name, comm=None):
    T = act.shape[0]
    C, H = DN_CHUNK, DN_HEADS
    N = T // C

    def body(a_ref, g_ref, b_ref, s_ref, t_ref, vn_ref, u_ref, w_ref, do_ref, da_ref, dg_ref, db_ref, ds_scr):
        @pl.when(pl.program_id(0) == 0)
        def _():
            ds_scr[...] = jnp.zeros_like(ds_scr)

        head_lane = _iota2((C, H), 1)
        dg_cols, db_cols = {}, {}

        def head(hh):
            qs, vs = slice(hh * DN_DK, (hh + 1) * DN_DK), slice(hh * DN_DV, (hh + 1) * DN_DV)
            ks = slice(DN_QK_W + hh * DN_DK, DN_QK_W + (hh + 1) * DN_DK)
            vas = slice(2 * DN_QK_W + hh * DN_DV, 2 * DN_QK_W + (hh + 1) * DN_DV)
            q, k, v = a_ref[:, qs], a_ref[:, ks], a_ref[:, vas]
            gc = jnp.sum(jnp.where(head_lane == hh, g_ref[...], 0.0), axis=1, keepdims=True)
            bc = jnp.sum(jnp.where(head_lane == hh, b_ref[...], 0.0), axis=1, keepdims=True)
            t = _dn_chunk_terms(q, k, gc, bc)
            yield
            lower, strict, eye = t["lower"], t["strict"], t["eye"]
            decay, eg, egl, kb, qd, kd = t["decay"], t["eg"], t["egl"], t["kb"], t["qd"], t["kd"]
            s, tm, vn, u, w, d_o = s_ref[hh], t_ref[hh], vn_ref[:, vs], u_ref[:, vs], w_ref[:, qs], do_ref[:, vs]
            ds_next = ds_scr[hh]
            egl_tot = jnp.exp(t["gl"])
            dob, sb, dsb, vnb = d_o.astype(BF16), s.astype(BF16), ds_next.astype(BF16), vn.astype(BF16)

            dvn = _bdot(t["aqk"], dob, TN) + _bdot(kd, dsb)
            yield
            daqk = jnp.where(lower, _dot(dob, vnb, NT), 0.0)
            dqd = _dot(dob, sb, NT)
            dkd = _dot(vnb, dsb, NT)
            yield
            dvnb = dvn.astype(BF16)
            ds_scr[hh] = _bdot(qd, dob, TN) + egl_tot * ds_next - _bdot(w, dvnb, TN)
            dgl = egl_tot * jnp.sum(jnp.sum(s * ds_next, axis=1, keepdims=True), axis=0, keepdims=True)
            dw = -_dot(dvnb, sb, NT)
            yield
            tms = _split(tm)
            dru = _x3dot(tms, dvn, TN)
            drw = _x3dot(tms, dw, TN)
            yield
            dl = -jnp.where(strict, _x3dot(dru, u, NT) + _x3dot(drw, w, NT), 0.0)
            yield
            dkk = (dl * decay).astype(BF16)
            dqk = (daqk * decay).astype(BF16)
            dkb = _bdot(dkk, k) + drw * eg
            yield
            da_ref[:, ks] = _bdot(dkk, kb, TN) + _bdot(dqk, q, TN) + dkd * egl + dkb * bc
            da_ref[:, qs] = _bdot(dqk, k) + dqd * eg
            da_ref[:, vas] = dru * bc
            yield
            db_cols[hh] = jnp.sum(dru * v, axis=1, keepdims=True) + jnp.sum(dkb * k, axis=1, keepdims=True)
            pm = dl * t["lmat"] + daqk * t["aqk"]
            col_as_col = jnp.sum(jnp.where(eye, jnp.sum(pm, axis=0, keepdims=True), 0.0), axis=1, keepdims=True)
            kdsum = jnp.sum(dkd * kd, axis=1, keepdims=True)
            dgc = (jnp.sum(pm, axis=1, keepdims=True) - col_as_col + jnp.sum(dqd * qd, axis=1, keepdims=True)
                   - kdsum + jnp.sum(drw * (kb * eg), axis=1, keepdims=True))
            dgl = dgl + jnp.sum(kdsum, axis=0, keepdims=True)
            dg_cols[hh] = dgc + jnp.where(t["last"], dgl, 0.0)

        _interleave([head(hh) for hh in range(H)])
        dg_ref[...] = sum(jnp.where(head_lane == hh, dg_cols[hh], 0.0) for hh in range(H))
        db_ref[...] = sum(jnp.where(head_lane == hh, db_cols[hh], 0.0) for hh in range(H))

    row = lambda w: pl.BlockSpec((C, w), lambda n: (N - 1 - n, 0))
    return _call(
        body, comm, name=name, grid=(N,),
        in_specs=[row(DN_CONV_W), row(H), row(H),
                  pl.BlockSpec((H, None, DN_DK, DN_DV), lambda n: (0, N - 1 - n, 0, 0)),
                  pl.BlockSpec((H, None, C, C), lambda n: (0, N - 1 - n, 0, 0)),
                  row(DN_V_W), row(DN_V_W), row(DN_QK_W), row(DN_V_W)],
        out_specs=[row(DN_CONV_W), row(H), row(H)],
        out_shape=[jax.ShapeDtypeStruct((T, DN_CONV_W), F32),
                   jax.ShapeDtypeStruct((T, H), F32), jax.ShapeDtypeStruct((T, H), F32)],
        scratch_shapes=[pltpu.VMEM((H, DN_DK, DN_DV), F32)], semantics=("arbitrary",),
        args=(act, g, beta, s_saved, tm_saved, vn_saved, u_saved, w_saved, do))


def _dn_split_w_in(w):
    wab = jnp.pad(w[:, DN_CONV_W + DN_V_W:], ((0, 0), (0, DN_AB_PAD - 2 * DN_HEADS)))
    return w[:, :DN_CONV_W], w[:, DN_CONV_W:DN_CONV_W + DN_V_W], wab


def _dn_layer_fwd(h, wts, conv_w, a_log, dt_bias, gn, w_out, x_res, tag, comm=None):
    wqkv, wgate, wab = wts
    H = DN_HEADS
    pqkv = _matmul(h, wqkv, "nn", tag + "_pqkv")
    pgate = _matmul(h, wgate, "nn", tag + "_pgate")
    pab = _matmul(h, wab, "nn", tag + "_pab")
    a_in, b_in = pab[:, :H], pab[:, H:2 * H]
    g, beta = _dn_gates(a_in, b_in, a_log, dt_bias, tag + "_gates")
    act = _dn_prep(pqkv, conv_w, tag + "_prep")
    (o_raw, s_sv, tm_sv, vn_sv, u_sv, w_sv), landed = _dn_chunk_fwd(act, g, beta, tag + "_chunk_fwd", comm)
    og = _dn_post(o_raw, pgate, gn, tag + "_post")
    if callable(w_out):
        w_out = w_out(landed)
    y = _matmul(og, w_out, "nn", tag + "_out", add=x_res)
    saved = dict(h=h, wts=wts, conv_w=conv_w, a_log=a_log, dt_bias=dt_bias, gn=gn, w_out=w_out, pqkv=pqkv, pgate=pgate,
                 a_in=a_in, b_in=b_in, g=g, beta=beta, act=act, o_raw=o_raw, chunk=(s_sv, tm_sv, vn_sv, u_sv, w_sv), og=og)
    return y, saved, landed


def _dn_layer_bwd(dout, sv, tag, comm_of=None):
    wqkv, wgate, wab = sv["wts"]
    h = sv["h"]
    dog = _matmul(dout, sv["w_out"], "nt", tag + "_dog")
    dw_out = _matmul(sv["og"], dout, "tn", tag + "_dwout", out_dtype=BF16)
    do_raw, dgate, dgn = _dn_post_bwd(dog, sv["o_raw"], sv["pgate"], sv["gn"], tag + "_post_bwd")
    comm = comm_of(dw_out) if comm_of is not None else None
    (dact, dg, dbeta), landed = _dn_chunk_bwd(sv["act"], sv["g"], sv["beta"], *sv["chunk"], do_raw, tag + "_chunk_bwd", comm)
    da_in, db_in, da_log, ddt = _dn_gates_bwd(dg, dbeta, sv["a_in"], sv["b_in"], sv["a_log"], sv["dt_bias"],
                                              tag + "_gates_bwd")
    dpqkv, dconv = _dn_prep_bwd(sv["pqkv"], sv["conv_w"], dact, tag + "_prep_bwd")
    dpab = jnp.pad(jnp.concatenate([da_in, db_in], axis=1), ((0, 0), (0, DN_AB_PAD - 2 * DN_HEADS)))
    dwqkv = _matmul(h, dpqkv, "tn", tag + "_dwqkv", out_dtype=BF16)
    dwgate = _matmul(h, dgate, "tn", tag + "_dwgate", out_dtype=BF16)
    dwab = _matmul(h, dpab, "tn", tag + "_dwab", out_dtype=BF16)
    dh = _matmul(dpqkv, wqkv, "nt", tag + "_dh0")
    dh = _matmul(dgate, wgate, "nt", tag + "_dh1", add=dh)
    dh = _matmul(dpab, wab, "nt", tag + "_dh2", add=dh)
    dw_in = jnp.concatenate([dwqkv, dwgate, dwab[:, :2 * DN_HEADS]], axis=1)
    return dh, dict(dn_w_in=dw_in, dn_conv_w=dconv, dn_a_log=da_log, dn_dt_bias=ddt, dn_o_norm_g=dgn, dn_w_out=dw_out), landed


def _sb_layer_fwd(h, w_in, qg, kg, w_out, x_res, tag, comm=None):
    qg2, kg2 = jnp.tile(qg, (1, 2)), jnp.tile(kg, (1, 2))
    proj = _matmul(h, w_in, "nn", tag + "_proj")
    qn, kn, vb = _sb_prep(proj, qg2, kg2, tag + "_prep")
    (o, og, ltot, done), landed = _sb_attn_fwd(qn, kn, vb, proj, tag + "_attn_fwd", comm)
    y = _matmul(og, w_out, "nn", tag + "_out", add=x_res)
    saved = dict(h=h, w_in=w_in, qg2=qg2, kg2=kg2, w_out=w_out, proj=proj, qn=qn, kn=kn, vb=vb, o=o, og=og, ltot=ltot,
                 done=done)
    return y, saved, landed


def _sb_layer_bwd(dout, sv, tag, comm=None):
    dog = _matmul(dout, sv["w_out"], "nt", tag + "_dog")
    dw_out = _matmul(sv["og"], dout, "tn", tag + "_dwout", out_dtype=BF16)
    (dqn, dkn, dv, dgate), landed = _sb_attn_bwd(sv["qn"], sv["kn"], sv["vb"], dog, sv["o"], sv["ltot"], sv["done"],
                                                 sv["proj"], tag + "_attn_bwd", comm)
    dproj, dqgp, dkgp = _sb_prep_bwd(sv["proj"], dqn, dkn, dv, dgate, sv["qg2"], sv["kg2"], tag + "_prep_bwd")
    dw_in = _matmul(sv["h"], dproj, "tn", tag + "_dwin", out_dtype=BF16)
    dh = _matmul(dproj, sv["w_in"], "nt", tag + "_dh")
    dqg = _fold_heads(dqgp, tag + "_dqg")
    dkg = _fold_heads(dkgp, tag + "_dkg")
    return dh, dict(sb_w_in=dw_in, sb_q_norm_g=dqg, sb_k_norm_g=dkg, sb_w_out=dw_out), landed


def _sc_layer_fwd(h, w_in, conv_w, w_out, x_res, tag):
    proj = _matmul(h, w_in, "nn", tag + "_proj")
    yg = _sc_fwd(proj, conv_w, tag + "_fwd")
    y = _matmul(yg, w_out, "nn", tag + "_out", add=x_res)
    return y, dict(h=h, w_in=w_in, conv_w=conv_w, w_out=w_out, proj=proj, yg=yg)


def _sc_layer_bwd(dout, sv, tag):
    dyg = _matmul(dout, sv["w_out"], "nt", tag + "_dyg")
    dw_out = _matmul(sv["yg"], dout, "tn", tag + "_dwout", out_dtype=BF16)
    dproj, dconv = _sc_bwd(dyg, sv["proj"], sv["conv_w"], tag + "_bwd")
    dw_in = _matmul(sv["h"], dproj, "tn", tag + "_dwin", out_dtype=BF16)
    dh = _matmul(dproj, sv["w_in"], "nt", tag + "_dh")
    return dh, dict(sc_w_in=dw_in, sc_conv_w=dconv, sc_w_out=dw_out)


def _adamw(w, m, v, parts, name):
    R, C = w.shape
    tr = _tile(R, 128, SUBLANE)

    def body(w_ref, m_ref, v_ref, p_ref, g_ref, d_ref, nm_ref, nv_ref):
        g = p_ref[0].astype(F32)
        for s in range(1, N_DEV):
            g = g + p_ref[s].astype(F32)
        m2 = ADAM_B1 * m_ref[...] + (1.0 - ADAM_B1) * g
        v2 = ADAM_B2 * v_ref[...] + (1.0 - ADAM_B2) * (g * g)
        m_hat = m2 / (1.0 - ADAM_B1 ** ADAM_STEP)
        v_hat = v2 / (1.0 - ADAM_B2 ** ADAM_STEP)
        g_ref[...] = g
        d_ref[...] = -ADAM_LR * (m_hat / (jnp.sqrt(v_hat) + ADAM_EPS) + ADAM_WD * w_ref[...])
        nm_ref[...] = m2
        nv_ref[...] = v2

    blk = pl.BlockSpec((tr, C), lambda i: (i, 0))
    return pl.pallas_call(
        body, name=name, grid=(R // tr,),
        in_specs=[blk, blk, blk, pl.BlockSpec((N_DEV, tr, C), lambda i: (0, i, 0))],
        out_specs=[blk] * 4, out_shape=[jax.ShapeDtypeStruct((R, C), F32)] * 4,
        compiler_params=_params("parallel"),
    )(w, m, v, parts)


_HBM = pl.BlockSpec(memory_space=pltpu.HBM)
_MESH = pl.DeviceIdType.MESH


def _slot(x, y, c):
    return 4 * x + 2 * y + c


class _Gather:
    def __init__(self, shards):
        self.arrays = list(shards)
        n = len(self.arrays)
        self.out_shapes = [jax.ShapeDtypeStruct((N_DEV,) + s.shape, s.dtype) for s in self.arrays]
        self.scratch = [pltpu.SemaphoreType.DMA((n, N_DEV - 1)), pltpu.SemaphoreType.DMA((n, N_DEV - 1)),
                        pltpu.SemaphoreType.DMA((n,))]

    def _parts(self, ins, outs, sems):
        send_sems, recv_sems, local_sems = sems
        n = len(self.arrays)
        x, y, c = lax.axis_index("x"), lax.axis_index("y"), lax.axis_index("c")
        me, sibling = (x, y, c), (x, y, 1 - c)
        chips = [(1 - x, y), (x, 1 - y), (1 - x, 1 - y)]

        def copy(a, k, block, to, src=None):
            dst = outs[a].at[_slot(*block)]
            return pltpu.make_async_remote_copy(src_ref=dst if src is None else src, dst_ref=dst,
                                                send_sem=send_sems.at[a, k], recv_sem=recv_sems.at[a, k],
                                                device_id=to, device_id_type=_MESH)

        mine = [pltpu.make_async_copy(ins[a], outs[a].at[_slot(*me)], local_sems.at[a]) for a in range(n)]
        first = []
        for a in range(n):
            first.append(copy(a, 0, me, sibling, src=ins[a]))
            first += [copy(a, 1 + j, me, (*chip, c), src=ins[a]) for j, chip in enumerate(chips)]
        return n, c, me, sibling, chips, copy, mine, first

    def start(self, ins, outs, sems):
        _, _, _, _, _, _, mine, first = self._parts(ins, outs, sems)
        for cp in mine + first:
            cp.start()

    def finish(self, ins, outs, sems):
        n, c, me, sibling, chips, copy, mine, first = self._parts(ins, outs, sems)
        passed = []
        for j, chip in enumerate(chips):
            for a in range(n):
                copy(a, 1 + j, (*chip, c), me).wait_recv()
                fwd = copy(a, 4 + j, (*chip, c), sibling)
                fwd.start()
                passed.append(fwd)
        for a in range(n):
            copy(a, 0, sibling, me).wait_recv()
            for j, chip in enumerate(chips):
                copy(a, 4 + j, (*chip, 1 - c), me).wait_recv()
        for cp in first + passed:
            cp.wait_send()
        for cp in mine:
            cp.wait()


class _Exchange:
    def __init__(self, arrays, scatter):
        self.arrays, self.scatter = list(arrays), list(scatter)
        n = len(self.arrays)
        shapes = [a.shape[1:] if s else a.shape for a, s in zip(self.arrays, self.scatter)]
        self.out_shapes = [jax.ShapeDtypeStruct((N_DEV,) + tuple(s), a.dtype) for s, a in zip(shapes, self.arrays)]
        self.scratch = [pltpu.SemaphoreType.DMA((n, N_DEV - 1)), pltpu.SemaphoreType.DMA((n, N_DEV - 1)),
                        pltpu.SemaphoreType.DMA((n,))]

    def _copies(self, ins, outs, sems):
        send_sems, recv_sems, local_sems = sems
        n, scatter = len(self.arrays), self.scatter
        x, y, c = lax.axis_index("x"), lax.axis_index("y"), lax.axis_index("c")
        me = _slot(x, y, c)
        copies = [pltpu.make_async_copy(ins[a].at[me] if scatter[a] else ins[a], outs[a].at[me], local_sems.at[a])
                  for a in range(n)]
        for r in range(1, N_DEV):
            px = 1 - x if r & 4 else x
            py = 1 - y if r & 2 else y
            pc = 1 - c if r & 1 else c
            for a in range(n):
                copies.append(pltpu.make_async_remote_copy(
                    src_ref=ins[a].at[_slot(px, py, pc)] if scatter[a] else ins[a], dst_ref=outs[a].at[me],
                    send_sem=send_sems.at[a, r - 1], recv_sem=recv_sems.at[a, r - 1],
                    device_id=(px, py, pc), device_id_type=_MESH))
        return copies

    def start(self, ins, outs, sems):
        for cp in self._copies(ins, outs, sems):
            cp.start()

    def finish(self, ins, outs, sems):
        for cp in self._copies(ins, outs, sems):
            cp.wait()


def _comm_call(comm, name):
    n = len(comm.arrays)

    def body(*refs):
        ins, outs, sems = refs[:n], refs[n:2 * n], refs[2 * n:]
        comm.start(ins, outs, sems)
        comm.finish(ins, outs, sems)

    return pl.pallas_call(body, name=name, in_specs=[_HBM] * n, out_specs=[_HBM] * n, out_shape=comm.out_shapes,
                          scratch_shapes=comm.scratch)(*comm.arrays)


def _call(body, comm, *, name, grid, in_specs, out_specs, out_shape, scratch_shapes, semantics, args):
    if comm is None:
        outs = pl.pallas_call(body, name=name, grid=grid, in_specs=in_specs, out_specs=out_specs, out_shape=out_shape,
                              scratch_shapes=scratch_shapes, compiler_params=_params(*semantics))(*args)
        return outs, []
    n_in, n_out, n_scr, n_c = len(in_specs), len(out_specs), len(scratch_shapes), len(comm.arrays)

    def fused(*refs):
        ins, refs = refs[:n_in], refs[n_in:]
        c_ins, refs = refs[:n_c], refs[n_c:]
        outs, refs = refs[:n_out], refs[n_out:]
        c_outs, refs = refs[:n_c], refs[n_c:]
        scr, sems = refs[:n_scr], refs[n_scr:]
        ids = [pl.program_id(d) for d in range(len(grid))]
        first = functools.reduce(jnp.logical_and, [i == 0 for i in ids])
        last = functools.reduce(jnp.logical_and, [i == g - 1 for i, g in zip(ids, grid)])

        @pl.when(first)
        def _():
            comm.start(c_ins, c_outs, sems)

        body(*ins, *outs, *scr)

        @pl.when(last)
        def _():
            comm.finish(c_ins, c_outs, sems)

    outs = pl.pallas_call(
        fused, name=name, grid=grid, in_specs=list(in_specs) + [_HBM] * n_c, out_specs=list(out_specs) + [_HBM] * n_c,
        out_shape=list(out_shape) + comm.out_shapes, scratch_shapes=list(scratch_shapes) + comm.scratch,
        compiler_params=_params(*["arbitrary"] * len(grid)))(*args, *comm.arrays)
    return outs[:n_out], outs[n_out:]


_GATHER_0 = (("dn_w_in", 0), ("dn_conv_w", 0), ("dn_o_norm_g", 0))
_GATHER_1 = (("dn_w_out", 0), ("sb_w_in", 0), ("sb_w_out", 0))
_GATHER_2 = (("sc_w_in", 0), ("sc_conv_w", 0), ("sc_w_out", 0), ("dn_w_in", 1), ("dn_conv_w", 1), ("dn_o_norm_g", 1),
             ("dn_w_out", 1))
_EXCHANGE_A = _GATHER_2
_EXCHANGE_B = (("sb_w_in", 0), ("sb_w_out", 0), ("dn_w_out", 0))
_EXCHANGE_C = _GATHER_0
_MATMUL_WEIGHTS = ("dn_w_in", "dn_w_out", "sb_w_in", "sb_w_out", "sc_w_in", "sc_w_out")
_COLUMN_SHARDED = ("dn_w_in", "dn_conv_w", "dn_o_norm_g", "sb_w_in", "sc_w_in", "sc_conv_w")
_REPLICATED = ("norm_g", "dn_a_log", "dn_dt_bias", "sb_q_norm_g", "sb_k_norm_g")
_ORDER = ("norm_g", "dn_w_in", "dn_conv_w", "dn_a_log", "dn_dt_bias", "dn_o_norm_g", "dn_w_out", "sb_w_in", "sb_q_norm_g",
          "sb_k_norm_g", "sb_w_out", "sc_w_in", "sc_conv_w", "sc_w_out")
_PACK_COLS = D_MODEL


def _as_2d(a):
    return a.reshape(1, -1) if a.ndim == 1 else a


def _assemble(name, gathered):
    n, r, c = gathered.shape
    if name in _COLUMN_SHARDED:
        return jnp.moveaxis(gathered, 0, 1).reshape(r, n * c)
    return gathered.reshape(n * r, c)


def _disassemble(name, full):
    r, c = full.shape
    if name in _COLUMN_SHARDED:
        return jnp.moveaxis(full.reshape(r, N_DEV, c // N_DEV), 1, 0)
    return full.reshape(N_DEV, r // N_DEV, c)


def _pack_replicated(d):
    rows = [d["norm_g"]]
    for name in _REPLICATED[1:]:
        flat = d[name].reshape(1, -1)
        rows.append(jnp.pad(flat, ((0, 0), (0, _PACK_COLS - flat.shape[1]))))
    return jnp.concatenate(rows, axis=0)


def _unpack_replicated(p, like):
    out = {"norm_g": p[:4]}
    for r, name in enumerate(_REPLICATED[1:]):
        shape = like[name].shape
        out[name] = p[4 + r, :math.prod(shape)].reshape(shape)
    return out


def kernel(x, norm_g, dn_w_in, dn_conv_w, dn_a_log, dn_dt_bias, dn_o_norm_g, dn_w_out, sb_w_in, sb_q_norm_g, sb_k_norm_g, sb_w_out, sc_w_in, sc_conv_w, sc_w_out, loss_target, m_norm_g, m_dn_w_in, m_dn_conv_w, m_dn_a_log, m_dn_dt_bias, m_dn_o_norm_g, m_dn_w_out, m_sb_w_in, m_sb_q_norm_g, m_sb_k_norm_g, m_sb_w_out, m_sc_w_in, m_sc_conv_w, m_sc_w_out, v_norm_g, v_dn_w_in, v_dn_conv_w, v_dn_a_log, v_dn_dt_bias, v_dn_o_norm_g, v_dn_w_out, v_sb_w_in, v_sb_q_norm_g, v_sb_k_norm_g, v_sb_w_out, v_sc_w_in, v_sc_conv_w, v_sc_w_out):
    w = dict(norm_g=norm_g, dn_w_in=dn_w_in, dn_conv_w=dn_conv_w, dn_a_log=dn_a_log, dn_dt_bias=dn_dt_bias,
             dn_o_norm_g=dn_o_norm_g, dn_w_out=dn_w_out, sb_w_in=sb_w_in, sb_q_norm_g=sb_q_norm_g, sb_k_norm_g=sb_k_norm_g,
             sb_w_out=sb_w_out, sc_w_in=sc_w_in, sc_conv_w=sc_conv_w, sc_w_out=sc_w_out)
    m = dict(norm_g=m_norm_g, dn_w_in=m_dn_w_in, dn_conv_w=m_dn_conv_w, dn_a_log=m_dn_a_log, dn_dt_bias=m_dn_dt_bias,
             dn_o_norm_g=m_dn_o_norm_g, dn_w_out=m_dn_w_out, sb_w_in=m_sb_w_in, sb_q_norm_g=m_sb_q_norm_g,
             sb_k_norm_g=m_sb_k_norm_g, sb_w_out=m_sb_w_out, sc_w_in=m_sc_w_in, sc_conv_w=m_sc_conv_w, sc_w_out=m_sc_w_out)
    v = dict(norm_g=v_norm_g, dn_w_in=v_dn_w_in, dn_conv_w=v_dn_conv_w, dn_a_log=v_dn_a_log, dn_dt_bias=v_dn_dt_bias,
             dn_o_norm_g=v_dn_o_norm_g, dn_w_out=v_dn_w_out, sb_w_in=v_sb_w_in, sb_q_norm_g=v_sb_q_norm_g,
             sb_k_norm_g=v_sb_k_norm_g, sb_w_out=v_sb_w_out, sc_w_in=v_sc_w_in, sc_conv_w=v_sc_conv_w, sc_w_out=v_sc_w_out)

    def gather_of(keys):
        return _Gather([_as_2d(w[k][j]).astype(BF16) if k in _MATMUL_WEIGHTS else _as_2d(w[k][j]) for k, j in keys])

    def full_weights(keys, gathered):
        return {key: _assemble(key[0], g) for key, g in zip(keys, gathered)}

    def exchange_of(keys, grads, extra=()):
        out = [_disassemble(k, grads[k, j].astype(BF16) if k in _MATMUL_WEIGHTS else grads[k, j]) for k, j in keys]
        return _Exchange(out + list(extra), [True] * len(out) + [False] * len(extra))

    F = full_weights(_GATHER_0, _comm_call(gather_of(_GATHER_0), "gather_first"))
    xs, saves = [x[0]], []
    h = _rmsnorm_fwd(xs[0], norm_g[0:1], "norm0")

    def w_out_0(got):
        F.update(full_weights(_GATHER_1, got))
        return F["dn_w_out", 0]

    y, sv, _ = _dn_layer_fwd(h, _dn_split_w_in(F["dn_w_in", 0]), F["dn_conv_w", 0], dn_a_log[0:1], dn_dt_bias[0:1],
                             F["dn_o_norm_g", 0], w_out_0, xs[0], "dn0", gather_of(_GATHER_1))
    xs.append(y)
    saves.append(sv)
    h = _rmsnorm_fwd(xs[1], norm_g[1:2], "norm1")
    y, sv, got = _sb_layer_fwd(h, F["sb_w_in", 0], sb_q_norm_g, sb_k_norm_g, F["sb_w_out", 0], xs[1], "sb", gather_of(_GATHER_2))
    F.update(full_weights(_GATHER_2, got))
    xs.append(y)
    saves.append(sv)
    h = _rmsnorm_fwd(xs[2], norm_g[2:3], "norm2")
    y, sv = _sc_layer_fwd(h, F["sc_w_in", 0], F["sc_conv_w", 0], F["sc_w_out", 0], xs[2], "sc")
    xs.append(y)
    saves.append(sv)
    h = _rmsnorm_fwd(xs[3], norm_g[3:4], "norm3")
    y, sv, _ = _dn_layer_fwd(h, _dn_split_w_in(F["dn_w_in", 1]), F["dn_conv_w", 1], dn_a_log[1:2], dn_dt_bias[1:2],
                             F["dn_o_norm_g", 1], F["dn_w_out", 1], xs[3], "dn1")
    xs.append(y)
    saves.append(sv)
    dx, loss_part = _loss_head(xs[4], loss_target[0])

    G, dnorm, landed = {}, [None] * 4, {}

    def keep(grads, j):
        G.update({(k, j): g for k, g in grads.items()})

    dh, grads, _ = _dn_layer_bwd(dx, saves[3], "dn1")
    keep(grads, 1)
    dx, dnorm[3] = _rmsnorm_bwd(dh, xs[3], norm_g[3:4], dx, "norm3_bwd")
    dh, grads = _sc_layer_bwd(dx, saves[2], "sc")
    keep(grads, 0)
    dx, dnorm[2] = _rmsnorm_bwd(dh, xs[2], norm_g[2:3], dx, "norm2_bwd")
    dh, grads, got = _sb_layer_bwd(dx, saves[1], "sb", exchange_of(_EXCHANGE_A, G))
    keep(grads, 0)
    landed.update(zip(_EXCHANGE_A, got))
    dx, dnorm[1] = _rmsnorm_bwd(dh, xs[1], norm_g[1:2], dx, "norm1_bwd")

    def exchange_b(dw_out):
        G["dn_w_out", 0] = dw_out
        return exchange_of(_EXCHANGE_B, G)

    dh, grads, got = _dn_layer_bwd(dx, saves[0], "dn0", exchange_b)
    keep(grads, 0)
    landed.update(zip(_EXCHANGE_B, got))
    dx, dnorm[0] = _rmsnorm_bwd(dh, xs[0], norm_g[0:1], dx, "norm0_bwd")
    replicated = dict(norm_g=jnp.concatenate(dnorm, axis=0),
                      dn_a_log=jnp.concatenate([G["dn_a_log", 0], G["dn_a_log", 1]], axis=0),
                      dn_dt_bias=jnp.concatenate([G["dn_dt_bias", 0], G["dn_dt_bias", 1]], axis=0),
                      sb_q_norm_g=G["sb_q_norm_g", 0], sb_k_norm_g=G["sb_k_norm_g", 0])
    got = _comm_call(exchange_of(_EXCHANGE_C, G, extra=[_pack_replicated(replicated)]), "exchange_last")
    landed.update(zip(_EXCHANGE_C, got[:-1]))

    res = {}
    for k in _ORDER:
        if k in _REPLICATED:
            continue
        per_layer = []
        for j in range(w[k].shape[0]):
            shape = w[k][j].shape
            outs = _adamw(_as_2d(w[k][j]), _as_2d(m[k][j]), _as_2d(v[k][j]), landed[k, j], f"adamw_{k}{j}")
            per_layer.append([o.reshape(shape) for o in outs])
        res[k] = [jnp.stack([layer[i] for layer in per_layer], axis=0) for i in range(4)]
    outs = _adamw(_pack_replicated(w), _pack_replicated(m), _pack_replicated(v), got[-1], "adamw_replicated")
    unpacked = [_unpack_replicated(o, w) for o in outs]
    for k in _REPLICATED:
        res[k] = [u[k] for u in unpacked]

    loss = lax.psum(loss_part[0, 0], ("x", "y", "c"))
    return (loss, dx[None]) + tuple(res[k][0] for k in _ORDER) + tuple(res[k][1] for k in _ORDER) \
        + tuple(res[k][2] for k in _ORDER) + tuple(res[k][3] for k in _ORDER)
```

```python
import functools
import itertools
import math

import jax
import jax.numpy as jnp
from jax import lax
from jax.experimental import pallas as pl
from jax.experimental.pallas import tpu as pltpu

F32 = jnp.float32
BF16 = jnp.bfloat16
HIGHEST = lax.Precision.HIGHEST

N_DEV = 8
D_MODEL = 1024
RMS_EPS = 1e-6
L2_EPS = 1e-6

DN_HEADS = 8
DN_DK = 128
DN_DV = 256
DN_QK_W = DN_HEADS * DN_DK
DN_V_W = DN_HEADS * DN_DV
DN_CONV = 4
DN_CHUNK = 64
DN_CONV_W = 2 * DN_QK_W + DN_V_W
DN_IN = DN_CONV_W + DN_V_W + 2 * DN_HEADS
DN_AB_PAD = 128
DN_PREP_BLK = 512

SB_HEADS = 16
SB_DH = 64
SB_W = SB_HEADS * SB_DH
SB_PAIRS = SB_HEADS // 2
SB_TQ = 256
SB_TK = 128
SB_DEAD = -106.0

SC_W = 2 * D_MODEL
SC_CONV = 3
SC_BLK = 512
SC_NBLK = SC_W // SC_BLK

ADAM_LR = 0.001
ADAM_B1 = 0.9
ADAM_B2 = 0.999
ADAM_EPS = 1e-08
ADAM_WD = 0.01
ADAM_STEP = 10

LANE = 128
SUBLANE = 8
HALO = SUBLANE
ROW_TILE = 256
WIDE_ROW_TILE = 128
CONV_ROW_TILE = 256
VMEM_LIMIT = 48 * 2 ** 20

NN = ((1,), (0,))
NT = ((1,), (1,))
TN = ((0,), (0,))


def _dot(a, b, dims=NN, precision=None):
    return lax.dot_general(a, b, (dims, ((), ())), precision=precision, preferred_element_type=F32)


def _bdot(a, b, dims=NN):
    return _dot(a.astype(BF16), b.astype(BF16), dims)


def _hdot(a, b, dims=NN):
    return _dot(a, b, dims, precision=HIGHEST)


def _tile(dim, pref, align=LANE):
    t = (min(pref, dim) // align) * align
    while t >= align:
        if dim % t == 0:
            return t
        t -= align
    return dim


def _params(*sem):
    return pltpu.CompilerParams(dimension_semantics=sem, vmem_limit_bytes=VMEM_LIMIT)


def _sigmoid(x):
    return 0.5 * jnp.tanh(0.5 * x) + 0.5


def _softplus(x):
    return jnp.maximum(x, 0.0) + jnp.log(1.0 + jnp.exp(-jnp.abs(x)))


def _silu_and_grad(x):
    s = _sigmoid(x)
    return x * s, s * (1.0 + x * (1.0 - s))


def _iota2(shape, dim):
    return lax.broadcasted_iota(jnp.int32, shape, dim)


def _matmul(a, b, mode, name, out_dtype=F32, add=None, tm=1024, tn=1024, tk=1024):
    if mode == "nn":
        (M, K), (K2, N) = a.shape, b.shape
    elif mode == "nt":
        (M, K), (N, K2) = a.shape, b.shape
    else:
        (K, M), (K2, N) = a.shape, b.shape
    assert K == K2, (a.shape, b.shape, mode)
    tm, tn, tk = _tile(M, tm), _tile(N, tn), _tile(K, tk)
    nk = K // tk
    dims = {"nn": NN, "nt": NT, "tn": TN}[mode]
    a_spec = pl.BlockSpec((tk, tm), lambda i, j, k: (k, i)) if mode == "tn" else pl.BlockSpec((tm, tk), lambda i, j, k: (i, k))
    b_spec = pl.BlockSpec((tn, tk), lambda i, j, k: (j, k)) if mode == "nt" else pl.BlockSpec((tk, tn), lambda i, j, k: (k, j))
    o_spec = pl.BlockSpec((tm, tn), lambda i, j, k: (i, j))
    has_add = add is not None

    def body(*refs):
        a_ref, b_ref = refs[0], refs[1]
        add_ref = refs[2] if has_add else None
        o_ref = refs[3] if has_add else refs[2]
        p = _bdot(a_ref[...], b_ref[...], dims)

        def finish(acc):
            if has_add:
                acc = acc + add_ref[...]
            o_ref[...] = acc.astype(out_dtype)

        if nk == 1:
            finish(p)
        else:
            acc_ref = refs[-1]
            k = pl.program_id(2)

            @pl.when(k == 0)
            def _():
                acc_ref[...] = p

            @pl.when(k > 0)
            def _():
                acc_ref[...] += p

            @pl.when(k == nk - 1)
            def _():
                finish(acc_ref[...])

    in_specs = [a_spec, b_spec] + ([o_spec] if has_add else [])
    args = (a, b) + ((add,) if has_add else ())
    return pl.pallas_call(
        body, name=name, grid=(M // tm, N // tn, nk),
        in_specs=in_specs, out_specs=o_spec,
        out_shape=jax.ShapeDtypeStruct((M, N), out_dtype),
        scratch_shapes=[pltpu.VMEM((tm, tn), F32)] if nk > 1 else [],
        compiler_params=_params("parallel", "parallel", "arbitrary"),
    )(*args)


def _rmsnorm_fwd(x, g, name):
    T, D = x.shape
    tt = _tile(T, 512, SUBLANE)

    def body(x_ref, g_ref, o_ref):
        xv = x_ref[...]
        r = lax.rsqrt(jnp.mean(xv * xv, axis=-1, keepdims=True) + RMS_EPS)
        o_ref[...] = (xv * r * g_ref[...]).astype(BF16)

    return pl.pallas_call(
        body, name=name, grid=(T // tt,),
        in_specs=[pl.BlockSpec((tt, D), lambda i: (i, 0)), pl.BlockSpec((1, D), lambda i: (0, 0))],
        out_specs=pl.BlockSpec((tt, D), lambda i: (i, 0)),
        out_shape=jax.ShapeDtypeStruct((T, D), BF16),
        compiler_params=_params("parallel"),
    )(x, g)


def _rmsnorm_bwd(dh, x, g, dx_res, name):
    T, D = x.shape
    tt = _tile(T, 256, SUBLANE)

    def body(dh_ref, x_ref, g_ref, res_ref, dx_ref, dg_ref):
        xv, dhv = x_ref[...], dh_ref[...]
        r = lax.rsqrt(jnp.mean(xv * xv, axis=-1, keepdims=True) + RMS_EPS)
        xh = xv * r
        dxh = dhv * g_ref[...]
        m = jnp.mean(dxh * xh, axis=-1, keepdims=True)
        dx_ref[...] = res_ref[...] + r * (dxh - xh * m)
        part = jnp.sum(dhv * xh, axis=0, keepdims=True)

        @pl.when(pl.program_id(0) == 0)
        def _():
            dg_ref[...] = part

        @pl.when(pl.program_id(0) > 0)
        def _():
            dg_ref[...] += part

    row = pl.BlockSpec((tt, D), lambda i: (i, 0))
    vec = pl.BlockSpec((1, D), lambda i: (0, 0))
    return pl.pallas_call(
        body, name=name, grid=(T // tt,),
        in_specs=[row, row, vec, row], out_specs=[row, vec],
        out_shape=[jax.ShapeDtypeStruct((T, D), F32), jax.ShapeDtypeStruct((1, D), F32)],
        compiler_params=_params("arbitrary"),
    )(dh, x, g, dx_res)


def _loss_head(y, target, name="loss_head"):
    T, D = y.shape
    tt = _tile(T, 512, SUBLANE)

    def body(y_ref, t_ref, dy_ref, l_ref):
        e = y_ref[...] - t_ref[...]
        dy_ref[...] = e * (1.0 / D)
        s = jnp.sum(jnp.sum(e * e, axis=1, keepdims=True), axis=0, keepdims=True) * (0.5 / D)
        s = jnp.broadcast_to(s, (1, LANE))

        @pl.when(pl.program_id(0) == 0)
        def _():
            l_ref[...] = s

        @pl.when(pl.program_id(0) > 0)
        def _():
            l_ref[...] += s

    row = pl.BlockSpec((tt, D), lambda i: (i, 0))
    return pl.pallas_call(
        body, name=name, grid=(T // tt,),
        in_specs=[row, row], out_specs=[row, pl.BlockSpec((1, LANE), lambda i: (0, 0))],
        out_shape=[jax.ShapeDtypeStruct((T, D), F32), jax.ShapeDtypeStruct((1, LANE), F32)],
        compiler_params=_params("arbitrary"),
    )(y, target)


def _down(x, k):
    return pltpu.roll(x, k, 0) if k else x


def _up(x, k):
    return pltpu.roll(x, x.shape[0] - k, 0) if k else x


def _sc_fwd(proj, conv_w, name):
    T = proj.shape[0]
    tt = _tile(T, WIDE_ROW_TILE, SUBLANE)
    B = SC_BLK

    def body(p_ref, ph_ref, w_ref, o_ref):
        keep = (pl.program_id(0) > 0).astype(F32)
        for j in range(SC_NBLK):
            cb, cc, cu, cg = (slice(k * SC_W + j * B, k * SC_W + (j + 1) * B) for k in range(4))
            cw = slice(j * B, (j + 1) * B)
            z = jnp.concatenate([ph_ref[:, cc] * ph_ref[:, cu] * keep, p_ref[:, cc] * p_ref[:, cu]], axis=0)
            cz = (w_ref[2:3, cw] * z + w_ref[1:2, cw] * _down(z, 1) + w_ref[0:1, cw] * _down(z, 2))[HALO:]
            gate = p_ref[:, cg]
            o_ref[:, cw] = (p_ref[:, cb] * cz * (gate * _sigmoid(gate))).astype(BF16)

    return pl.pallas_call(
        body, name=name, grid=(T // tt,),
        in_specs=[pl.BlockSpec((tt, 4 * SC_W), lambda i: (i, 0)),
                  pl.BlockSpec((HALO, 4 * SC_W), lambda i: (jnp.maximum(i * (tt // HALO) - 1, 0), 0)),
                  pl.BlockSpec((SC_CONV, SC_W), lambda i: (0, 0))],
        out_specs=pl.BlockSpec((tt, SC_W), lambda i: (i, 0)),
        out_shape=jax.ShapeDtypeStruct((T, SC_W), BF16),
        compiler_params=_params("parallel"),
    )(proj, proj, conv_w)


def _sc_bwd(dyg, proj, conv_w, name):
    T = proj.shape[0]
    tt = _tile(T, WIDE_ROW_TILE, SUBLANE)
    nt = T // tt
    B = SC_BLK
    hb = tt // HALO

    def body(d_ref, dn_ref, p_ref, pp_ref, pn_ref, w_ref, o_ref, dw_ref):
        i = pl.program_id(0)
        keep_p = (i > 0).astype(F32)
        keep_n = (i < nt - 1).astype(F32)
        main = slice(HALO, HALO + tt)
        parts = []
        for j in range(SC_NBLK):
            cw = slice(j * B, (j + 1) * B)

            def ext(k):
                s = slice(k * SC_W + j * B, k * SC_W + (j + 1) * B)
                return s, jnp.concatenate([pp_ref[:, s] * keep_p, p_ref[:, s], pn_ref[:, s]], axis=0)

            (sb, b), (sc, c), (su, u), (sg_, gate) = ext(0), ext(1), ext(2), ext(3)
            dyg_e = jnp.concatenate([jnp.zeros((HALO, B), F32), d_ref[:, cw], dn_ref[:, cw] * keep_n], axis=0)
            w0, w1, w2 = w_ref[0:1, cw], w_ref[1:2, cw], w_ref[2:3, cw]
            z = c * u
            z1, z2 = _down(z, 1), _down(z, 2)
            cz = w2 * z + w1 * z1 + w0 * z2
            sg, dsg = _silu_and_grad(gate)
            dy = dyg_e * sg
            dcz = dy * b
            dz = w2 * dcz + w1 * _up(dcz, 1) + w0 * _up(dcz, 2)
            o_ref[:, sb] = (dy * cz)[main].astype(BF16)
            o_ref[:, sc] = (dz * u)[main].astype(BF16)
            o_ref[:, su] = (dz * c)[main].astype(BF16)
            o_ref[:, sg_] = (dyg_e * (b * cz) * dsg)[main].astype(BF16)
            dcm = dcz[main]
            parts.append(jnp.concatenate([jnp.sum(dcm * z2[main], axis=0, keepdims=True),
                                          jnp.sum(dcm * z1[main], axis=0, keepdims=True),
                                          jnp.sum(dcm * z[main], axis=0, keepdims=True)], axis=0))
        part = jnp.concatenate(parts, axis=1)

        @pl.when(i == 0)
        def _():
            dw_ref[...] = part

        @pl.when(i > 0)
        def _():
            dw_ref[...] += part

    nxt = lambda i: (jnp.minimum((i + 1) * hb, nt * hb - 1), 0)
    return pl.pallas_call(
        body, name=name, grid=(nt,),
        in_specs=[pl.BlockSpec((tt, SC_W), lambda i: (i, 0)),
                  pl.BlockSpec((HALO, SC_W), nxt),
                  pl.BlockSpec((tt, 4 * SC_W), lambda i: (i, 0)),
                  pl.BlockSpec((HALO, 4 * SC_W), lambda i: (jnp.maximum(i * hb - 1, 0), 0)),
                  pl.BlockSpec((HALO, 4 * SC_W), nxt),
                  pl.BlockSpec((SC_CONV, SC_W), lambda i: (0, 0))],
        out_specs=[pl.BlockSpec((tt, 4 * SC_W), lambda i: (i, 0)), pl.BlockSpec((SC_CONV, SC_W), lambda i: (0, 0))],
        out_shape=[jax.ShapeDtypeStruct((T, 4 * SC_W), BF16), jax.ShapeDtypeStruct((SC_CONV, SC_W), F32)],
        compiler_params=_params("arbitrary"),
    )(dyg, dyg, proj, proj, proj, conv_w)


def _split3_dot(x, m):
    hi = x.astype(BF16)
    r1 = x - hi.astype(F32)
    mid = r1.astype(BF16)
    lo = (r1 - mid.astype(F32)).astype(BF16)
    return _dot(hi, m) + _dot(mid, m) + _dot(lo, m)


def _split2_dot(x, m):
    hi = x.astype(BF16)
    lo = (x - hi.astype(F32)).astype(BF16)
    return _dot(hi, m) + _dot(lo, m)


def _head_mean_matrix():
    r, c = _iota2((LANE, LANE), 0), _iota2((LANE, LANE), 1)
    return jnp.where((r // SB_DH) == (c // SB_DH), 1.0 / SB_DH, 0.0).astype(BF16)


def _sb_prep(proj, qg2, kg2, name):
    T = proj.shape[0]
    tt = _tile(T, WIDE_ROW_TILE, SUBLANE)

    def body(p_ref, qg_ref, kg_ref, q_ref, k_ref, v_ref):
        bd = _head_mean_matrix()

        def norm(x, g, scale):
            r = lax.rsqrt(_split3_dot(x * x, bd) + RMS_EPS)
            return (x * r * g * scale).astype(BF16)

        v_ref[...] = p_ref[:, 2 * SB_W:3 * SB_W].astype(BF16)
        for p in range(SB_PAIRS):
            cols = slice(p * LANE, (p + 1) * LANE)
            q_ref[:, cols] = norm(p_ref[:, cols], qg_ref[...], SB_DH ** -0.5)
            k_ref[:, cols] = norm(p_ref[:, SB_W + p * LANE:SB_W + (p + 1) * LANE], kg_ref[...], 1.0)

    blk = pl.BlockSpec((tt, SB_W), lambda i: (i, 0))
    vec = pl.BlockSpec((1, LANE), lambda i: (0, 0))
    return pl.pallas_call(
        body, name=name, grid=(T // tt,),
        in_specs=[pl.BlockSpec((tt, 4 * SB_W), lambda i: (i, 0)), vec, vec],
        out_specs=[blk, blk, blk],
        out_shape=[jax.ShapeDtypeStruct((T, SB_W), BF16)] * 3,
        compiler_params=_params("parallel"),
    )(proj, qg2, kg2)


def _sb_prep_bwd(proj, dqn, dkn, dv, dgate, qg2, kg2, name):
    T = proj.shape[0]
    tt = _tile(T, WIDE_ROW_TILE, SUBLANE)

    def body(p_ref, dq_ref, dk_ref, dv_ref, dg_ref, qg_ref, kg_ref, o_ref, dqg_ref, dkg_ref):
        i = pl.program_id(0)
        bd = _head_mean_matrix()

        def norm_bwd(x, g, dy):
            r = lax.rsqrt(_split3_dot(x * x, bd) + RMS_EPS)
            xh = x * r
            dxh = dy * g
            m = _split3_dot(dxh * xh, bd)
            return r * (dxh - xh * m), jnp.sum(dy * xh, axis=0, keepdims=True)

        o_ref[:, 2 * SB_W:3 * SB_W] = dv_ref[...].astype(BF16)
        o_ref[:, 3 * SB_W:4 * SB_W] = dg_ref[...].astype(BF16)
        pq = pk = jnp.zeros((1, LANE), F32)
        for p in range(SB_PAIRS):
            cols, kcols = slice(p * LANE, (p + 1) * LANE), slice(SB_W + p * LANE, SB_W + (p + 1) * LANE)
            dxq, sq = norm_bwd(p_ref[:, cols], qg_ref[...], dq_ref[:, cols])
            dxk, sk = norm_bwd(p_ref[:, kcols], kg_ref[...], dk_ref[:, cols])
            o_ref[:, cols] = dxq.astype(BF16)
            o_ref[:, kcols] = dxk.astype(BF16)
            pq, pk = pq + sq, pk + sk

        @pl.when(i == 0)
        def _():
            dqg_ref[...] = pq
            dkg_ref[...] = pk

        @pl.when(i > 0)
        def _():
            dqg_ref[...] += pq
            dkg_ref[...] += pk

    blk = pl.BlockSpec((tt, SB_W), lambda i: (i, 0))
    vec = pl.BlockSpec((1, LANE), lambda i: (0, 0))
    wide = pl.BlockSpec((tt, 4 * SB_W), lambda i: (i, 0))
    return pl.pallas_call(
        body, name=name, grid=(T // tt,),
        in_specs=[wide, blk, blk, blk, blk, vec, vec],
        out_specs=[wide, vec, vec],
        out_shape=[jax.ShapeDtypeStruct((T, 4 * SB_W), BF16)] + [jax.ShapeDtypeStruct((1, LANE), F32)] * 2,
        compiler_params=_params("arbitrary"),
    )(proj, dqn, dkn, dv, dgate, qg2, kg2)


def _fold_heads(part, name):
    def body(p_ref, o_ref):
        r, c = _iota2((LANE, SB_DH), 0), _iota2((LANE, SB_DH), 1)
        fold = jnp.where((r % SB_DH) == c, 1.0, 0.0).astype(F32)
        o_ref[...] = jnp.sum(_hdot(p_ref[...], fold), axis=0, keepdims=True)

    return pl.pallas_call(body, name=name, out_shape=jax.ShapeDtypeStruct((1, SB_DH), F32))(part)


def _sb_masks():
    lane = _iota2((1, LANE), 1)
    return lane < SB_DH


def _sb_attn_fwd(qn, kn, vb, proj, name, comm=None):
    T = qn.shape[0]
    tq, tk = _tile(T, SB_TQ, SUBLANE), SB_TK
    assert tq % tk == 0

    def body(q_ref, k_ref, v_ref, g_ref, o_ref, og_ref, lt_ref, done_ref):
        i = pl.program_id(1)
        ma = _sb_masks()
        q2 = q_ref[...]
        zero = jnp.zeros_like(q2)
        qs = (jnp.where(ma, q2, zero), jnp.where(ma, zero, q2))
        upper = (_iota2((tk, tk), 0) > _iota2((tk, tk), 1)).astype(BF16)
        qpos = i * tq + _iota2((tq, tk), 0)
        nb = tq // tk

        def trip(kb_top, masked, carry):
            acc, la, lb = carry
            chains = [(b, h) for b in range(nb) for h in range(2)]
            k2s, vss, masks = [], [], []
            for b in range(nb):
                kb = kb_top - b
                rows = pl.ds(pl.multiple_of(kb * tk, tk), tk)
                k2s.append(k_ref[rows, :])
                v2 = v_ref[rows, :]
                zv = jnp.zeros_like(v2)
                vss.append((jnp.where(ma, v2, zv), jnp.where(ma, zv, v2)))
                masks.append((kb * tk + _iota2((tq, tk), 1)) < qpos if masked else None)
            zs = [_dot(qs[h], k2s[b], NT) for b, h in chains]
            ts = [jnp.log(1.0 + jnp.exp(-jnp.abs(z))) for z in zs]
            ls = [-(jnp.maximum(z, 0.0) + t) for z, t in zip(zs, ts)]
            if masked:
                ls = [jnp.where(masks[b], l, 0.0) for (b, h), l in zip(chains, ls)]
            cums = [_split2_dot(l, upper) for l in ls]
            sums = [jnp.sum(l, axis=1, keepdims=True) for l in ls]
            offs, tot = {}, [la, lb]
            for b in range(nb):
                for h in range(2):
                    offs[(b, h)] = tot[h]
                    tot[h] = tot[h] + sums[chains.index((b, h))]
            ws = [jnp.exp(jnp.minimum(z, 0.0) - t + c + offs[ch]) for ch, z, t, c in zip(chains, zs, ts, cums)]
            if masked:
                ws = [jnp.where(masks[b], w, 0.0) for (b, h), w in zip(chains, ws)]
            for (b, h), w in zip(chains, ws):
                acc = acc + _dot(w.astype(BF16), vss[b][h])
            return acc, tot[0], tot[1]

        def largest(la, lb):
            return jnp.max(jnp.maximum(la, lb))

        z1 = jnp.zeros((tq, 1), F32)
        acc, la, lb = trip((i + 1) * nb - 1, True, (jnp.zeros((tq, LANE), F32), z1, z1))

        def live(c):
            return (c[0] < i) & (c[4] > SB_DEAD)

        def more(c):
            j, acc, la, lb, _ = c
            acc, la, lb = trip((i - j) * nb - 1, False, (acc, la, lb))
            return j + 1, acc, la, lb, largest(la, lb)

        done, acc, la, lb, _ = lax.while_loop(live, more, (jnp.int32(0), acc, la, lb, largest(la, lb)))
        gate = g_ref[...]
        o_ref[...] = acc
        og_ref[...] = (acc * (gate * _sigmoid(gate))).astype(BF16)
        lt_ref[...] = jnp.where(_iota2((tq, 2), 1) == 0, la, lb)
        done_ref[...] = jnp.full((SUBLANE, LANE), done, F32)

    nq = T // tq
    qblk = pl.BlockSpec((tq, LANE), lambda p, i: (i, p))
    full = pl.BlockSpec((T, LANE), lambda p, i: (0, p))
    return _call(
        body, comm, name=name, grid=(SB_PAIRS, nq),
        in_specs=[qblk, full, full, pl.BlockSpec((tq, LANE), lambda p, i: (i, 3 * SB_PAIRS + p))],
        out_specs=[qblk, qblk, pl.BlockSpec((None, tq, 2), lambda p, i: (p, i, 0)),
                   pl.BlockSpec((None, None, SUBLANE, LANE), lambda p, i: (p, i, 0, 0))],
        out_shape=[jax.ShapeDtypeStruct((T, SB_W), F32), jax.ShapeDtypeStruct((T, SB_W), BF16),
                   jax.ShapeDtypeStruct((SB_PAIRS, T, 2), F32), jax.ShapeDtypeStruct((SB_PAIRS, nq, SUBLANE, LANE), F32)],
        scratch_shapes=[], semantics=("parallel", "parallel"), args=(qn, kn, vb, proj))


def _sb_attn_bwd(qn, kn, vb, dog, o, ltot, done, proj, name, comm=None):
    T = qn.shape[0]
    tq, tk = _tile(T, SB_TQ, SUBLANE), SB_TK

    def body(q_ref, k_ref, v_ref, dog_ref, o_ref, lt_ref, done_ref, g_ref, dq_ref, dk_ref, dv_ref, dgate_ref):
        i = pl.program_id(1)
        first_trip = i - jnp.max(done_ref[...]).astype(jnp.int32)

        @pl.when(i == 0)
        def _():
            dk_ref[...] = jnp.zeros_like(dk_ref)
            dv_ref[...] = jnp.zeros_like(dv_ref)

        ma = _sb_masks()
        gate, o2, dog2 = g_ref[...], o_ref[...], dog_ref[...]
        sg, dsg = _silu_and_grad(gate)
        do2 = dog2 * sg
        dgate_ref[...] = dog2 * o2 * dsg
        lt = lt_ref[...]
        first = _iota2((tq, 2), 1) == 0
        ltots = (jnp.sum(jnp.where(first, lt, 0.0), axis=1, keepdims=True),
                 jnp.sum(jnp.where(first, 0.0, lt), axis=1, keepdims=True))
        q2 = q_ref[...]
        zq = jnp.zeros_like(q2)
        qs = (jnp.where(ma, q2, zq), jnp.where(ma, zq, q2))
        dob = do2.astype(BF16)
        dos = (jnp.where(ma, dob, zq), jnp.where(ma, zq, dob))
        upto = (_iota2((tk, tk), 0) <= _iota2((tk, tk), 1)).astype(BF16)
        before = (_iota2((tk, tk), 0) < _iota2((tk, tk), 1)).astype(BF16)
        qpos = i * tq + _iota2((tq, tk), 0)
        nb = tq // tk

        def trip(kb_bot, masked, carry):
            dq, la, lb, ea, eb = carry
            chains = [(b, h) for b in range(nb) for h in range(2)]
            rows, k2s, v2s, kss, masks = [], [], [], [], []
            for b in range(nb):
                kb = kb_bot + b
                rows.append(pl.ds(pl.multiple_of(kb * tk, tk), tk))
                k2 = k_ref[rows[b], :]
                zk = jnp.zeros_like(k2)
                k2s.append(k2)
                v2s.append(v_ref[rows[b], :])
                kss.append((jnp.where(ma, k2, zk), jnp.where(ma, zk, k2)))
                masks.append((kb * tk + _iota2((tq, tk), 1)) < qpos if masked else None)

            def keep(vals):
                return [jnp.where(masks[b], x, 0.0) for (b, h), x in zip(chains, vals)] if masked else vals

            zs = [_dot(qs[h], k2s[b], NT) for b, h in chains]
            dws = [_dot(dos[h], v2s[b], NT) for b, h in chains]
            ts = [jnp.log(1.0 + jnp.exp(-jnp.abs(z))) for z in zs]
            ls = keep([-(jnp.maximum(z, 0.0) + t) for z, t in zip(zs, ts)])
            lps = [jnp.minimum(z, 0.0) - t for z, t in zip(zs, ts)]
            cums = [_split3_dot(l, upto) for l in ls]
            lsums = [jnp.sum(l, axis=1, keepdims=True) for l in ls]
            offs, tot = {}, [la, lb]
            for b in range(nb):
                for h in range(2):
                    offs[(b, h)] = tot[h]
                    tot[h] = tot[h] + lsums[chains.index((b, h))]
            ws = keep([jnp.exp(lp + (ltots[h] - (offs[(b, h)] + c))) for (b, h), lp, c in zip(chains, lps, cums)])
            es = [dw * w for dw, w in zip(dws, ws)]
            ecums = [_split2_dot(e, before) for e in es]
            esums = [jnp.sum(e, axis=1, keepdims=True) for e in es]
            eoffs, etot = {}, [ea, eb]
            for b in range(nb):
                for h in range(2):
                    eoffs[(b, h)] = etot[h]
                    etot[h] = etot[h] + esums[chains.index((b, h))]
            dzs = keep([e - jnp.exp(lp) * (e + eoffs[ch] + ec) for ch, e, lp, ec in zip(chains, es, lps, ecums)])
            dzs = [dz.astype(BF16) for dz in dzs]
            wbs = [w.astype(BF16) for w in ws]
            for (b, h), dz in zip(chains, dzs):
                dq = dq + _dot(dz, kss[b][h])
            for b in range(nb):
                ia, ib = chains.index((b, 0)), chains.index((b, 1))
                dk_ref[rows[b], :] += _dot(dzs[ia], qs[0], TN) + _dot(dzs[ib], qs[1], TN)
                dv_ref[rows[b], :] += _dot(wbs[ia], dos[0], TN) + _dot(wbs[ib], dos[1], TN)
            return dq, tot[0], tot[1], etot[0], etot[1]

        z1 = jnp.zeros((tq, 1), F32)
        carry = lax.fori_loop(first_trip, i, lambda j, c: trip(j * nb, False, c),
                              (jnp.zeros((tq, LANE), F32), z1, z1, z1, z1))
        dq = trip(i * nb, True, carry)[0]
        dq_ref[...] = dq * (SB_DH ** -0.5)

    qblk = pl.BlockSpec((tq, LANE), lambda p, i: (i, p))
    full = pl.BlockSpec((T, LANE), lambda p, i: (0, p))
    return _call(
        body, comm, name=name, grid=(SB_PAIRS, T // tq),
        in_specs=[qblk, full, full, qblk, qblk, pl.BlockSpec((None, tq, 2), lambda p, i: (p, i, 0)),
                  pl.BlockSpec((None, None, SUBLANE, LANE), lambda p, i: (p, i, 0, 0)),
                  pl.BlockSpec((tq, LANE), lambda p, i: (i, 3 * SB_PAIRS + p))],
        out_specs=[qblk, full, full, qblk],
        out_shape=[jax.ShapeDtypeStruct((T, SB_W), F32)] * 4,
        scratch_shapes=[], semantics=("parallel", "arbitrary"), args=(qn, kn, vb, dog, o, ltot, done, proj))


def _dn_conv(ext, w_ref, cw):
    return (w_ref[3:4, cw] * ext + w_ref[2:3, cw] * _down(ext, 1) + w_ref[1:2, cw] * _down(ext, 2)
            + w_ref[0:1, cw] * _down(ext, 3))


def _dn_prep(pqkv, conv_w, name):
    T, W = pqkv.shape
    tt = _tile(T, CONV_ROW_TILE, SUBLANE)
    B = DN_PREP_BLK
    nq, nqk = DN_QK_W // B, 2 * DN_QK_W // B

    def body(p_ref, ph_ref, w_ref, o_ref):
        keep = (pl.program_id(0) > 0).astype(F32)
        for cb in range(W // B):
            cw = slice(cb * B, (cb + 1) * B)
            ext = jnp.concatenate([ph_ref[:, cw] * keep, p_ref[:, cw]], axis=0)
            c = _dn_conv(ext, w_ref, cw)[HALO:]
            a = c * _sigmoid(c)
            if cb >= nqk:
                o_ref[:, cw] = a
                continue
            scale = DN_DK ** -0.5 if cb < nq else 1.0
            for hh in range(B // DN_DK):
                ah = a[:, hh * DN_DK:(hh + 1) * DN_DK]
                r = lax.rsqrt(jnp.sum(ah * ah, axis=-1, keepdims=True) + L2_EPS)
                o_ref[:, cb * B + hh * DN_DK:cb * B + (hh + 1) * DN_DK] = ah * (r * scale)

    return pl.pallas_call(
        body, name=name, grid=(T // tt,),
        in_specs=[pl.BlockSpec((tt, W), lambda i: (i, 0)),
                  pl.BlockSpec((HALO, W), lambda i: (jnp.maximum(i * (tt // HALO) - 1, 0), 0)),
                  pl.BlockSpec((DN_CONV, W), lambda i: (0, 0))],
        out_specs=pl.BlockSpec((tt, W), lambda i: (i, 0)),
        out_shape=jax.ShapeDtypeStruct((T, W), F32),
        compiler_params=_params("parallel"),
    )(pqkv, pqkv, conv_w)


def _dn_prep_bwd(pqkv, conv_w, dact, name):
    T, W = pqkv.shape
    tt = _tile(T, CONV_ROW_TILE, SUBLANE)
    nt = T // tt
    hb = tt // HALO
    B = DN_PREP_BLK
    nq, nqk = DN_QK_W // B, 2 * DN_QK_W // B

    def body(p_ref, pp_ref, pn_ref, w_ref, d_ref, dn_ref, o_ref, dw_ref):
        i = pl.program_id(0)
        keep_p = (i > 0).astype(F32)
        keep_n = (i < nt - 1).astype(F32)
        main = slice(HALO, HALO + tt)
        parts = []
        for cb in range(W // B):
            cw = slice(cb * B, (cb + 1) * B)
            ext = jnp.concatenate([pp_ref[:, cw] * keep_p, p_ref[:, cw], pn_ref[:, cw]], axis=0)
            c = _dn_conv(ext, w_ref, cw)
            s = _sigmoid(c)
            da_dc = s * (1.0 + c * (1.0 - s))
            d_up = jnp.concatenate([jnp.zeros((HALO, B), F32), d_ref[:, cw], dn_ref[:, cw] * keep_n], axis=0)
            if cb < nqk:
                a = c * s
                scale = DN_DK ** -0.5 if cb < nq else 1.0
                normed = []
                for hh in range(B // DN_DK):
                    cols = slice(hh * DN_DK, (hh + 1) * DN_DK)
                    ah = a[:, cols]
                    r = lax.rsqrt(jnp.sum(ah * ah, axis=-1, keepdims=True) + L2_EPS)
                    y = ah * r
                    dy = d_up[:, cols] * scale
                    normed.append(r * (dy - y * jnp.sum(dy * y, axis=-1, keepdims=True)))
                d_up = jnp.concatenate(normed, axis=1)
            dc = d_up * da_dc
            dp = (w_ref[3:4, cw] * dc + w_ref[2:3, cw] * _up(dc, 1) + w_ref[1:2, cw] * _up(dc, 2)
                  + w_ref[0:1, cw] * _up(dc, 3))
            o_ref[:, cw] = dp[main].astype(BF16)
            dcm = dc[main]
            parts.append(jnp.concatenate([jnp.sum(dcm * _down(ext, 3 - k)[main], axis=0, keepdims=True)
                                          for k in range(DN_CONV)], axis=0))
        part = jnp.concatenate(parts, axis=1)

        @pl.when(i == 0)
        def _():
            dw_ref[...] = part

        @pl.when(i > 0)
        def _():
            dw_ref[...] += part

    main_spec = pl.BlockSpec((tt, W), lambda i: (i, 0))
    prev_spec = pl.BlockSpec((HALO, W), lambda i: (jnp.maximum(i * hb - 1, 0), 0))
    next_spec = pl.BlockSpec((HALO, W), lambda i: (jnp.minimum((i + 1) * hb, nt * hb - 1), 0))
    w_spec = pl.BlockSpec((DN_CONV, W), lambda i: (0, 0))
    return pl.pallas_call(
        body, name=name, grid=(nt,),
        in_specs=[main_spec, prev_spec, next_spec, w_spec, main_spec, next_spec],
        out_specs=[main_spec, w_spec],
        out_shape=[jax.ShapeDtypeStruct((T, W), BF16), jax.ShapeDtypeStruct((DN_CONV, W), F32)],
        compiler_params=_params("arbitrary"),
    )(pqkv, pqkv, pqkv, conv_w, dact, dact)


def _dn_gates(a_in, b_in, a_log, dt_bias, name):
    T, H = a_in.shape
    C = DN_CHUNK

    def body(a_ref, b_ref, al_ref, dt_ref, g_ref, beta_ref):
        beta_ref[...] = _sigmoid(b_ref[...])
        g_ref[...] = -jnp.exp(al_ref[...]) * _softplus(a_ref[...] + dt_ref[...])
        tri = (_iota2((C, C), 0) >= _iota2((C, C), 1)).astype(F32)

        def chunk(n, carry):
            rows = pl.ds(pl.multiple_of(n * C, C), C)
            g_ref[rows, :] = _hdot(tri, g_ref[rows, :])
            return carry

        lax.fori_loop(0, T // C, chunk, 0)

    return pl.pallas_call(body, name=name, out_shape=[jax.ShapeDtypeStruct((T, H), F32)] * 2)(a_in, b_in, a_log, dt_bias)


def _dn_gates_bwd(dg, dbeta, a_in, b_in, a_log, dt_bias, name):
    T, H = a_in.shape
    C = DN_CHUNK

    def body(dg_ref, db_ref, a_ref, b_ref, al_ref, dt_ref, da_ref, dbi_ref, dal_ref, ddt_ref):
        tri_t = (_iota2((C, C), 0) <= _iota2((C, C), 1)).astype(F32)

        def chunk(n, carry):
            rows = pl.ds(pl.multiple_of(n * C, C), C)
            da_ref[rows, :] = _hdot(tri_t, dg_ref[rows, :])
            return carry

        lax.fori_loop(0, T // C, chunk, 0)
        dla = da_ref[...]
        x = a_ref[...] + dt_ref[...]
        ea = jnp.exp(al_ref[...])
        da = dla * (-ea) * _sigmoid(x)
        da_ref[...] = da
        dal_ref[...] = jnp.sum(dla * (-ea * _softplus(x)), axis=0, keepdims=True)
        ddt_ref[...] = jnp.sum(da, axis=0, keepdims=True)
        beta = _sigmoid(b_ref[...])
        dbi_ref[...] = db_ref[...] * beta * (1.0 - beta)

    return pl.pallas_call(
        body, name=name,
        out_shape=[jax.ShapeDtypeStruct((T, H), F32)] * 2 + [jax.ShapeDtypeStruct((1, H), F32)] * 2,
    )(dg, dbeta, a_in, b_in, a_log, dt_bias)


def _dn_chunk_terms(q, k, gc, bc):
    C = DN_CHUNK
    r, c = _iota2((C, C), 0), _iota2((C, C), 1)
    lower, strict, eye = r >= c, r > c, r == c
    grow = jnp.sum(jnp.where(eye, gc, 0.0), axis=0, keepdims=True)
    decay = jnp.where(lower, jnp.exp(jnp.where(lower, gc - grow, 0.0)), 0.0)
    last = _iota2((C, 1), 0) == C - 1
    gl = jnp.sum(jnp.where(last, gc, 0.0), axis=0, keepdims=True)
    eg = jnp.exp(gc)
    egl = jnp.exp(gl - gc)
    kb = k * bc
    lmat = jnp.where(strict, _bdot(kb, k, NT) * decay, 0.0)
    aqk = jnp.where(lower, _bdot(q, k, NT) * decay, 0.0)
    return dict(lower=lower, strict=strict, eye=eye, last=last, decay=decay, gl=gl, eg=eg, egl=egl, kb=kb,
                lmat=lmat, aqk=aqk, qd=q * eg, kd=k * egl)


def _split(x):
    hi = x.astype(BF16)
    return hi, (x - hi.astype(F32)).astype(BF16)


def _x3dot(a, b, dims=NN):
    ah, al = a if isinstance(a, tuple) else _split(a)
    bh, bl = b if isinstance(b, tuple) else _split(b)
    return _dot(ah, bh, dims) + (_dot(ah, bl, dims) + _dot(al, bh, dims))


def _interleave(gens):
    for _ in itertools.zip_longest(*gens):
        pass


def _unit_lower_inverse_steps(lmat, eye, out):
    ident = jnp.where(eye, 1.0, 0.0).astype(F32)
    m = -lmat
    inv = ident + m
    for _ in range(int(math.log2(DN_CHUNK)) - 1):
        ms = _split(m)
        m = _x3dot(ms, ms)
        yield
        inv = inv + _x3dot(inv, m)
        yield
    out["tm"] = inv


def _dn_chunk_fwd(act, g, beta, pgate, gn, name, comm=None):
    T = act.shape[0]
    C, H = DN_CHUNK, DN_HEADS
    N = T // C

    def body(a_ref, g_ref, b_ref, pg_ref, gn_ref, o_ref, og_ref, s_out, t_out, vn_out, u_out, w_out, s_scr):
        n = pl.program_id(0)

        @pl.when(n == 0)
        def _():
            s_scr[...] = jnp.zeros_like(s_scr)

        head_lane = _iota2((C, H), 1)

        def head(hh):
            qs, vs = slice(hh * DN_DK, (hh + 1) * DN_DK), slice(hh * DN_DV, (hh + 1) * DN_DV)
            q, k, v = a_ref[:, qs], a_ref[:, DN_QK_W + hh * DN_DK:DN_QK_W + (hh + 1) * DN_DK], \
                a_ref[:, 2 * DN_QK_W + hh * DN_DV:2 * DN_QK_W + (hh + 1) * DN_DV]
            gc = jnp.sum(jnp.where(head_lane == hh, g_ref[...], 0.0), axis=1, keepdims=True)
            bc = jnp.sum(jnp.where(head_lane == hh, b_ref[...], 0.0), axis=1, keepdims=True)
            t = _dn_chunk_terms(q, k, gc, bc)
            yield
            res = {}
            yield from _unit_lower_inverse_steps(t["lmat"], t["eye"], res)
            tms = _split(res["tm"])
            u = _x3dot(tms, v * bc)
            yield
            w = _x3dot(tms, t["kb"] * t["eg"])
            yield
            s = s_scr[hh]
            s_out[hh] = s
            t_out[hh] = res["tm"]
            sb = s.astype(BF16)
            vn = u - _dot(w.astype(BF16), sb)
            yield
            o = _dot(t["qd"].astype(BF16), sb) + _bdot(t["aqk"], vn)
            yield
            s_scr[hh] = s * jnp.exp(t["gl"]) + _bdot(t["kd"], vn, TN)
            vn_out[:, vs] = vn
            u_out[:, vs] = u
            w_out[:, qs] = w
            o_ref[:, vs] = o
            gate = pg_ref[:, vs]
            r = lax.rsqrt(jnp.mean(o * o, axis=-1, keepdims=True) + RMS_EPS)
            og_ref[:, vs] = (o * r * gn_ref[...] * (gate * _sigmoid(gate))).astype(BF16)

        _interleave([head(hh) for hh in range(H)])

    row = lambda w: pl.BlockSpec((C, w), lambda n: (n, 0))
    return _call(
        body, comm, name=name, grid=(N,),
        in_specs=[row(DN_CONV_W), row(H), row(H), row(DN_V_W), pl.BlockSpec((1, DN_DV), lambda n: (0, 0))],
        out_specs=[row(DN_V_W), row(DN_V_W),
                   pl.BlockSpec((H, None, DN_DK, DN_DV), lambda n: (0, n, 0, 0)),
                   pl.BlockSpec((H, None, C, C), lambda n: (0, n, 0, 0)),
                   row(DN_V_W), row(DN_V_W), row(DN_QK_W)],
        out_shape=[jax.ShapeDtypeStruct((T, DN_V_W), F32), jax.ShapeDtypeStruct((T, DN_V_W), BF16),
                   jax.ShapeDtypeStruct((H, N, DN_DK, DN_DV), F32),
                   jax.ShapeDtypeStruct((H, N, C, C), F32),
                   jax.ShapeDtypeStruct((T, DN_V_W), F32),
                   jax.ShapeDtypeStruct((T, DN_V_W), F32),
                   jax.ShapeDtypeStruct((T, DN_QK_W), F32)],
        scratch_shapes=[pltpu.VMEM((H, DN_DK, DN_DV), F32)], semantics=("arbitrary",), args=(act, g, beta, pgate, gn))


def _dn_chunk_bwd(act, g, beta, s_saved, tm_saved, vn_saved, u_saved, w_saved, dog, o_raw, pgate, gn, name, comm=None):
    T = act.shape[0]
    C, H = DN_CHUNK, DN_HEADS
    N = T // C

    def body(a_ref, g_ref, b_ref, s_ref, t_ref, vn_ref, u_ref, w_ref, dog_ref, o_ref, pg_ref, gn_ref,
             da_ref, dg_ref, db_ref, dgate_ref, dgn_ref, ds_scr):
        @pl.when(pl.program_id(0) == 0)
        def _():
            ds_scr[...] = jnp.zeros_like(ds_scr)

        head_lane = _iota2((C, H), 1)
        dg_cols, db_cols, dgn_parts = {}, {}, {}

        def output_gate_bwd(hh, vs):
            d, o, gate, gn_v = dog_ref[:, vs], o_ref[:, vs], pg_ref[:, vs], gn_ref[...]
            sg, dsg = _silu_and_grad(gate)
            r = lax.rsqrt(jnp.mean(o * o, axis=-1, keepdims=True) + RMS_EPS)
            n = o * r
            dy = d * sg
            dgate_ref[:, vs] = (d * (n * gn_v) * dsg).astype(BF16)
            dn = dy * gn_v
            dgn_parts[hh] = jnp.sum(dy * n, axis=0, keepdims=True)
            return r * (dn - n * jnp.mean(dn * n, axis=-1, keepdims=True))

        def head(hh):
            qs, vs = slice(hh * DN_DK, (hh + 1) * DN_DK), slice(hh * DN_DV, (hh + 1) * DN_DV)
            ks = slice(DN_QK_W + hh * DN_DK, DN_QK_W + (hh + 1) * DN_DK)
            vas = slice(2 * DN_QK_W + hh * DN_DV, 2 * DN_QK_W + (hh + 1) * DN_DV)
            q, k, v = a_ref[:, qs], a_ref[:, ks], a_ref[:, vas]
            gc = jnp.sum(jnp.where(head_lane == hh, g_ref[...], 0.0), axis=1, keepdims=True)
            bc = jnp.sum(jnp.where(head_lane == hh, b_ref[...], 0.0), axis=1, keepdims=True)
            t = _dn_chunk_terms(q, k, gc, bc)
            yield
            lower, strict, eye = t["lower"], t["strict"], t["eye"]
            decay, eg, egl, kb, qd, kd = t["decay"], t["eg"], t["egl"], t["kb"], t["qd"], t["kd"]
            s, tm, vn, u, w = s_ref[hh], t_ref[hh], vn_ref[:, vs], u_ref[:, vs], w_ref[:, qs]
            d_o = output_gate_bwd(hh, vs)
            ds_next = ds_scr[hh]
            egl_tot = jnp.exp(t["gl"])
            dob, sb, dsb, vnb = d_o.astype(BF16), s.astype(BF16), ds_next.astype(BF16), vn.astype(BF16)

            dvn = _bdot(t["aqk"], dob, TN) + _bdot(kd, dsb)
            yield
            daqk = jnp.where(lower, _dot(dob, vnb, NT), 0.0)
            dqd = _dot(dob, sb, NT)
            dkd = _dot(vnb, dsb, NT)
            yield
            dvnb = dvn.astype(BF16)
            ds_scr[hh] = _bdot(qd, dob, TN) + egl_tot * ds_next - _bdot(w, dvnb, TN)
            dgl = egl_tot * jnp.sum(jnp.sum(s * ds_next, axis=1, keepdims=True), axis=0, keepdims=True)
            dw = -_dot(dvnb, sb, NT)
            yield
            tms = _split(tm)
            dru = _x3dot(tms, dvn, TN)
            drw = _x3dot(tms, dw, TN)
            yield
            dl = -jnp.where(strict, _x3dot(dru, u, NT) + _x3dot(drw, w, NT), 0.0)
            yield
            dkk = (dl * decay).astype(BF16)
            dqk = (daqk * decay).astype(BF16)
            dkb = _bdot(dkk, k) + drw * eg
            yield
            da_ref[:, ks] = _bdot(dkk, kb, TN) + _bdot(dqk, q, TN) + dkd * egl + dkb * bc
            da_ref[:, qs] = _bdot(dqk, k) + dqd * eg
            da_ref[:, vas] = dru * bc
            yield
            db_cols[hh] = jnp.sum(dru * v, axis=1, keepdims=True) + jnp.sum(dkb * k, axis=1, keepdims=True)
            pm = dl * t["lmat"] + daqk * t["aqk"]
            col_as_col = jnp.sum(jnp.where(eye, jnp.sum(pm, axis=0, keepdims=True), 0.0), axis=1, keepdims=True)
            kdsum = jnp.sum(dkd * kd, axis=1, keepdims=True)
            dgc = (jnp.sum(pm, axis=1, keepdims=True) - col_as_col + jnp.sum(dqd * qd, axis=1, keepdims=True)
                   - kdsum + jnp.sum(drw * (kb * eg), axis=1, keepdims=True))
            dgl = dgl + jnp.sum(kdsum, axis=0, keepdims=True)
            dg_cols[hh] = dgc + jnp.where(t["last"], dgl, 0.0)

        _interleave([head(hh) for hh in range(H)])
        dg_ref[...] = sum(jnp.where(head_lane == hh, dg_cols[hh], 0.0) for hh in range(H))
        db_ref[...] = sum(jnp.where(head_lane == hh, db_cols[hh], 0.0) for hh in range(H))
        dgn_part = sum(dgn_parts[hh] for hh in range(H))

        @pl.when(pl.program_id(0) == 0)
        def _():
            dgn_ref[...] = dgn_part

        @pl.when(pl.program_id(0) > 0)
        def _():
            dgn_ref[...] += dgn_part

    row = lambda w: pl.BlockSpec((C, w), lambda n: (N - 1 - n, 0))
    vec = pl.BlockSpec((1, DN_DV), lambda n: (0, 0))
    return _call(
        body, comm, name=name, grid=(N,),
        in_specs=[row(DN_CONV_W), row(H), row(H),
                  pl.BlockSpec((H, None, DN_DK, DN_DV), lambda n: (0, N - 1 - n, 0, 0)),
                  pl.BlockSpec((H, None, C, C), lambda n: (0, N - 1 - n, 0, 0)),
                  row(DN_V_W), row(DN_V_W), row(DN_QK_W), row(DN_V_W), row(DN_V_W), row(DN_V_W), vec],
        out_specs=[row(DN_CONV_W), row(H), row(H), row(DN_V_W), vec],
        out_shape=[jax.ShapeDtypeStruct((T, DN_CONV_W), F32),
                   jax.ShapeDtypeStruct((T, H), F32), jax.ShapeDtypeStruct((T, H), F32),
                   jax.ShapeDtypeStruct((T, DN_V_W), BF16), jax.ShapeDtypeStruct((1, DN_DV), F32)],
        scratch_shapes=[pltpu.VMEM((H, DN_DK, DN_DV), F32)], semantics=("arbitrary",),
        args=(act, g, beta, s_saved, tm_saved, vn_saved, u_saved, w_saved, dog, o_raw, pgate, gn))


def _dn_split_w_in(w):
    wab = jnp.pad(w[:, DN_CONV_W + DN_V_W:], ((0, 0), (0, DN_AB_PAD - 2 * DN_HEADS)))
    return w[:, :DN_CONV_W], w[:, DN_CONV_W:DN_CONV_W + DN_V_W], wab


def _dn_layer_fwd(h, wts, conv_w, a_log, dt_bias, gn, w_out, x_res, tag, comm=None):
    wqkv, wgate, wab = wts
    H = DN_HEADS
    pqkv = _matmul(h, wqkv, "nn", tag + "_pqkv")
    pgate = _matmul(h, wgate, "nn", tag + "_pgate")
    pab = _matmul(h, wab, "nn", tag + "_pab")
    a_in, b_in = pab[:, :H], pab[:, H:2 * H]
    g, beta = _dn_gates(a_in, b_in, a_log, dt_bias, tag + "_gates")
    act = _dn_prep(pqkv, conv_w, tag + "_prep")
    (o_raw, og, s_sv, tm_sv, vn_sv, u_sv, w_sv), landed = _dn_chunk_fwd(act, g, beta, pgate, gn, tag + "_chunk_fwd", comm)
    if callable(w_out):
        w_out = w_out(landed)
    y = _matmul(og, w_out, "nn", tag + "_out", add=x_res)
    saved = dict(h=h, wts=wts, conv_w=conv_w, a_log=a_log, dt_bias=dt_bias, gn=gn, w_out=w_out, pqkv=pqkv, pgate=pgate,
                 a_in=a_in, b_in=b_in, g=g, beta=beta, act=act, o_raw=o_raw, chunk=(s_sv, tm_sv, vn_sv, u_sv, w_sv), og=og)
    return y, saved, landed


def _dn_layer_bwd(dout, sv, tag, comm_of=None):
    wqkv, wgate, wab = sv["wts"]
    h = sv["h"]
    dog = _matmul(dout, sv["w_out"], "nt", tag + "_dog")
    dw_out = _matmul(sv["og"], dout, "tn", tag + "_dwout", out_dtype=BF16)
    comm = comm_of(dw_out) if comm_of is not None else None
    (dact, dg, dbeta, dgate, dgn), landed = _dn_chunk_bwd(sv["act"], sv["g"], sv["beta"], *sv["chunk"], dog, sv["o_raw"],
                                                          sv["pgate"], sv["gn"], tag + "_chunk_bwd", comm)
    da_in, db_in, da_log, ddt = _dn_gates_bwd(dg, dbeta, sv["a_in"], sv["b_in"], sv["a_log"], sv["dt_bias"],
                                              tag + "_gates_bwd")
    dpqkv, dconv = _dn_prep_bwd(sv["pqkv"], sv["conv_w"], dact, tag + "_prep_bwd")
    dpab = jnp.pad(jnp.concatenate([da_in, db_in], axis=1), ((0, 0), (0, DN_AB_PAD - 2 * DN_HEADS)))
    dwqkv = _matmul(h, dpqkv, "tn", tag + "_dwqkv", out_dtype=BF16)
    dwgate = _matmul(h, dgate, "tn", tag + "_dwgate", out_dtype=BF16)
    dwab = _matmul(h, dpab, "tn", tag + "_dwab", out_dtype=BF16)
    dh = _matmul(dpqkv, wqkv, "nt", tag + "_dh0")
    dh = _matmul(dgate, wgate, "nt", tag + "_dh1", add=dh)
    dh = _matmul(dpab, wab, "nt", tag + "_dh2", add=dh)
    dw_in = jnp.concatenate([dwqkv, dwgate, dwab[:, :2 * DN_HEADS]], axis=1)
    return dh, dict(dn_w_in=dw_in, dn_conv_w=dconv, dn_a_log=da_log, dn_dt_bias=ddt, dn_o_norm_g=dgn, dn_w_out=dw_out), landed


def _sb_layer_fwd(h, w_in, qg, kg, w_out, x_res, tag, comm=None):
    qg2, kg2 = jnp.tile(qg, (1, 2)), jnp.tile(kg, (1, 2))
    proj = _matmul(h, w_in, "nn", tag + "_proj")
    qn, kn, vb = _sb_prep(proj, qg2, kg2, tag + "_prep")
    (o, og, ltot, done), landed = _sb_attn_fwd(qn, kn, vb, proj, tag + "_attn_fwd", comm)
    y = _matmul(og, w_out, "nn", tag + "_out", add=x_res)
    saved = dict(h=h, w_in=w_in, qg2=qg2, kg2=kg2, w_out=w_out, proj=proj, qn=qn, kn=kn, vb=vb, o=o, og=og, ltot=ltot,
                 done=done)
    return y, saved, landed


def _sb_layer_bwd(dout, sv, tag, comm=None):
    dog = _matmul(dout, sv["w_out"], "nt", tag + "_dog")
    dw_out = _matmul(sv["og"], dout, "tn", tag + "_dwout", out_dtype=BF16)
    (dqn, dkn, dv, dgate), landed = _sb_attn_bwd(sv["qn"], sv["kn"], sv["vb"], dog, sv["o"], sv["ltot"], sv["done"],
                                                 sv["proj"], tag + "_attn_bwd", comm)
    dproj, dqgp, dkgp = _sb_prep_bwd(sv["proj"], dqn, dkn, dv, dgate, sv["qg2"], sv["kg2"], tag + "_prep_bwd")
    dw_in = _matmul(sv["h"], dproj, "tn", tag + "_dwin", out_dtype=BF16)
    dh = _matmul(dproj, sv["w_in"], "nt", tag + "_dh")
    dqg = _fold_heads(dqgp, tag + "_dqg")
    dkg = _fold_heads(dkgp, tag + "_dkg")
    return dh, dict(sb_w_in=dw_in, sb_q_norm_g=dqg, sb_k_norm_g=dkg, sb_w_out=dw_out), landed


def _sc_layer_fwd(h, w_in, conv_w, w_out, x_res, tag):
    proj = _matmul(h, w_in, "nn", tag + "_proj")
    yg = _sc_fwd(proj, conv_w, tag + "_fwd")
    y = _matmul(yg, w_out, "nn", tag + "_out", add=x_res)
    return y, dict(h=h, w_in=w_in, conv_w=conv_w, w_out=w_out, proj=proj, yg=yg)


def _sc_layer_bwd(dout, sv, tag):
    dyg = _matmul(dout, sv["w_out"], "nt", tag + "_dyg")
    dw_out = _matmul(sv["yg"], dout, "tn", tag + "_dwout", out_dtype=BF16)
    dproj, dconv = _sc_bwd(dyg, sv["proj"], sv["conv_w"], tag + "_bwd")
    dw_in = _matmul(sv["h"], dproj, "tn", tag + "_dwin", out_dtype=BF16)
    dh = _matmul(dproj, sv["w_in"], "nt", tag + "_dh")
    return dh, dict(sc_w_in=dw_in, sc_conv_w=dconv, sc_w_out=dw_out)


def _adamw(w, m, v, parts, name):
    R, C = w.shape
    tr = _tile(R, 128, SUBLANE)

    def body(w_ref, m_ref, v_ref, p_ref, g_ref, d_ref, nm_ref, nv_ref):
        g = p_ref[0].astype(F32)
        for s in range(1, N_DEV):
            g = g + p_ref[s].astype(F32)
        m2 = ADAM_B1 * m_ref[...] + (1.0 - ADAM_B1) * g
        v2 = ADAM_B2 * v_ref[...] + (1.0 - ADAM_B2) * (g * g)
        m_hat = m2 / (1.0 - ADAM_B1 ** ADAM_STEP)
        v_hat = v2 / (1.0 - ADAM_B2 ** ADAM_STEP)
        g_ref[...] = g
        d_ref[...] = -ADAM_LR * (m_hat / (jnp.sqrt(v_hat) + ADAM_EPS) + ADAM_WD * w_ref[...])
        nm_ref[...] = m2
        nv_ref[...] = v2

    blk = pl.BlockSpec((tr, C), lambda i: (i, 0))
    return pl.pallas_call(
        body, name=name, grid=(R // tr,),
        in_specs=[blk, blk, blk, pl.BlockSpec((N_DEV, tr, C), lambda i: (0, i, 0))],
        out_specs=[blk] * 4, out_shape=[jax.ShapeDtypeStruct((R, C), F32)] * 4,
        compiler_params=_params("parallel"),
    )(w, m, v, parts)


_HBM = pl.BlockSpec(memory_space=pltpu.HBM)
_MESH = pl.DeviceIdType.MESH


def _slot(x, y, c):
    return 4 * x + 2 * y + c


class _Gather:
    def __init__(self, shards):
        self.arrays = list(shards)
        n = len(self.arrays)
        self.out_shapes = [jax.ShapeDtypeStruct((N_DEV,) + s.shape, s.dtype) for s in self.arrays]
        self.scratch = [pltpu.SemaphoreType.DMA((n, N_DEV - 1)), pltpu.SemaphoreType.DMA((n, N_DEV - 1)),
                        pltpu.SemaphoreType.DMA((n,))]

    def _parts(self, ins, outs, sems):
        send_sems, recv_sems, local_sems = sems
        n = len(self.arrays)
        x, y, c = lax.axis_index("x"), lax.axis_index("y"), lax.axis_index("c")
        me, sibling = (x, y, c), (x, y, 1 - c)
        chips = [(1 - x, y), (x, 1 - y), (1 - x, 1 - y)]

        def copy(a, k, block, to, src=None):
            dst = outs[a].at[_slot(*block)]
            return pltpu.make_async_remote_copy(src_ref=dst if src is None else src, dst_ref=dst,
                                                send_sem=send_sems.at[a, k], recv_sem=recv_sems.at[a, k],
                                                device_id=to, device_id_type=_MESH)

        mine = [pltpu.make_async_copy(ins[a], outs[a].at[_slot(*me)], local_sems.at[a]) for a in range(n)]
        first = []
        for a in range(n):
            first.append(copy(a, 0, me, sibling, src=ins[a]))
            first += [copy(a, 1 + j, me, (*chip, c), src=ins[a]) for j, chip in enumerate(chips)]
        return n, c, me, sibling, chips, copy, mine, first

    def start(self, ins, outs, sems):
        _, _, _, _, _, _, mine, first = self._parts(ins, outs, sems)
        for cp in mine + first:
            cp.start()

    def finish(self, ins, outs, sems):
        n, c, me, sibling, chips, copy, mine, first = self._parts(ins, outs, sems)
        passed = []
        for j, chip in enumerate(chips):
            for a in range(n):
                copy(a, 1 + j, (*chip, c), me).wait_recv()
                fwd = copy(a, 4 + j, (*chip, c), sibling)
                fwd.start()
                passed.append(fwd)
        for a in range(n):
            copy(a, 0, sibling, me).wait_recv()
            for j, chip in enumerate(chips):
                copy(a, 4 + j, (*chip, 1 - c), me).wait_recv()
        for cp in first + passed:
            cp.wait_send()
        for cp in mine:
            cp.wait()


class _Exchange:
    def __init__(self, arrays, scatter):
        self.arrays, self.scatter = list(arrays), list(scatter)
        n = len(self.arrays)
        shapes = [a.shape[1:] if s else a.shape for a, s in zip(self.arrays, self.scatter)]
        self.out_shapes = [jax.ShapeDtypeStruct((N_DEV,) + tuple(s), a.dtype) for s, a in zip(shapes, self.arrays)]
        self.scratch = [pltpu.SemaphoreType.DMA((n, N_DEV - 1)), pltpu.SemaphoreType.DMA((n, N_DEV - 1)),
                        pltpu.SemaphoreType.DMA((n,))]

    def _copies(self, ins, outs, sems):
        send_sems, recv_sems, local_sems = sems
        n, scatter = len(self.arrays), self.scatter
        x, y, c = lax.axis_index("x"), lax.axis_index("y"), lax.axis_index("c")
        me = _slot(x, y, c)
        copies = [pltpu.make_async_copy(ins[a].at[me] if scatter[a] else ins[a], outs[a].at[me], local_sems.at[a])
                  for a in range(n)]
        for r in range(1, N_DEV):
            px = 1 - x if r & 4 else x
            py = 1 - y if r & 2 else y
            pc = 1 - c if r & 1 else c
            for a in range(n):
                copies.append(pltpu.make_async_remote_copy(
                    src_ref=ins[a].at[_slot(px, py, pc)] if scatter[a] else ins[a], dst_ref=outs[a].at[me],
                    send_sem=send_sems.at[a, r - 1], recv_sem=recv_sems.at[a, r - 1],
                    device_id=(px, py, pc), device_id_type=_MESH))
        return copies

    def start(self, ins, outs, sems):
        for cp in self._copies(ins, outs, sems):
            cp.start()

    def finish(self, ins, outs, sems):
        for cp in self._copies(ins, outs, sems):
            cp.wait()


def _comm_call(comm, name):
    n = len(comm.arrays)

    def body(*refs):
        ins, outs, sems = refs[:n], refs[n:2 * n], refs[2 * n:]
        comm.start(ins, outs, sems)
        comm.finish(ins, outs, sems)

    return pl.pallas_call(body, name=name, in_specs=[_HBM] * n, out_specs=[_HBM] * n, out_shape=comm.out_shapes,
                          scratch_shapes=comm.scratch)(*comm.arrays)


def _call(body, comm, *, name, grid, in_specs, out_specs, out_shape, scratch_shapes, semantics, args):
    if comm is None:
        outs = pl.pallas_call(body, name=name, grid=grid, in_specs=in_specs, out_specs=out_specs, out_shape=out_shape,
                              scratch_shapes=scratch_shapes, compiler_params=_params(*semantics))(*args)
        return outs, []
    n_in, n_out, n_scr, n_c = len(in_specs), len(out_specs), len(scratch_shapes), len(comm.arrays)

    def fused(*refs):
        ins, refs = refs[:n_in], refs[n_in:]
        c_ins, refs = refs[:n_c], refs[n_c:]
        outs, refs = refs[:n_out], refs[n_out:]
        c_outs, refs = refs[:n_c], refs[n_c:]
        scr, sems = refs[:n_scr], refs[n_scr:]
        ids = [pl.program_id(d) for d in range(len(grid))]
        first = functools.reduce(jnp.logical_and, [i == 0 for i in ids])
        last = functools.reduce(jnp.logical_and, [i == g - 1 for i, g in zip(ids, grid)])

        @pl.when(first)
        def _():
            comm.start(c_ins, c_outs, sems)

        body(*ins, *outs, *scr)

        @pl.when(last)
        def _():
            comm.finish(c_ins, c_outs, sems)

    outs = pl.pallas_call(
        fused, name=name, grid=grid, in_specs=list(in_specs) + [_HBM] * n_c, out_specs=list(out_specs) + [_HBM] * n_c,
        out_shape=list(out_shape) + comm.out_shapes, scratch_shapes=list(scratch_shapes) + comm.scratch,
        compiler_params=_params(*["arbitrary"] * len(grid)))(*args, *comm.arrays)
    return outs[:n_out], outs[n_out:]


_GATHER_0 = (("dn_w_in", 0), ("dn_conv_w", 0), ("dn_o_norm_g", 0))
_GATHER_1 = (("dn_w_out", 0), ("sb_w_in", 0), ("sb_w_out", 0))
_GATHER_2 = (("sc_w_in", 0), ("sc_conv_w", 0), ("sc_w_out", 0), ("dn_w_in", 1), ("dn_conv_w", 1), ("dn_o_norm_g", 1),
             ("dn_w_out", 1))
_EXCHANGE_A = _GATHER_2
_EXCHANGE_B = (("sb_w_in", 0), ("sb_w_out", 0), ("dn_w_out", 0))
_EXCHANGE_C = _GATHER_0
_MATMUL_WEIGHTS = ("dn_w_in", "dn_w_out", "sb_w_in", "sb_w_out", "sc_w_in", "sc_w_out")
_COLUMN_SHARDED = ("dn_w_in", "dn_conv_w", "dn_o_norm_g", "sb_w_in", "sc_w_in", "sc_conv_w")
_REPLICATED = ("norm_g", "dn_a_log", "dn_dt_bias", "sb_q_norm_g", "sb_k_norm_g")
_ORDER = ("norm_g", "dn_w_in", "dn_conv_w", "dn_a_log", "dn_dt_bias", "dn_o_norm_g", "dn_w_out", "sb_w_in", "sb_q_norm_g",
          "sb_k_norm_g", "sb_w_out", "sc_w_in", "sc_conv_w", "sc_w_out")
_PACK_COLS = D_MODEL


def _as_2d(a):
    return a.reshape(1, -1) if a.ndim == 1 else a


def _assemble(name, gathered):
    n, r, c = gathered.shape
    if name in _COLUMN_SHARDED:
        return jnp.moveaxis(gathered, 0, 1).reshape(r, n * c)
    return gathered.reshape(n * r, c)


def _disassemble(name, full):
    r, c = full.shape
    if name in _COLUMN_SHARDED:
        return jnp.moveaxis(full.reshape(r, N_DEV, c // N_DEV), 1, 0)
    return full.reshape(N_DEV, r // N_DEV, c)


def _pack_replicated(d):
    rows = [d["norm_g"]]
    for name in _REPLICATED[1:]:
        flat = d[name].reshape(1, -1)
        rows.append(jnp.pad(flat, ((0, 0), (0, _PACK_COLS - flat.shape[1]))))
    return jnp.concatenate(rows, axis=0)


def _unpack_replicated(p, like):
    out = {"norm_g": p[:4]}
    for r, name in enumerate(_REPLICATED[1:]):
        shape = like[name].shape
        out[name] = p[4 + r, :math.prod(shape)].reshape(shape)
    return out


def kernel(x, norm_g, dn_w_in, dn_conv_w, dn_a_log, dn_dt_bias, dn_o_norm_g, dn_w_out, sb_w_in, sb_q_norm_g, sb_k_norm_g, sb_w_out, sc_w_in, sc_conv_w, sc_w_out, loss_target, m_norm_g, m_dn_w_in, m_dn_conv_w, m_dn_a_log, m_dn_dt_bias, m_dn_o_norm_g, m_dn_w_out, m_sb_w_in, m_sb_q_norm_g, m_sb_k_norm_g, m_sb_w_out, m_sc_w_in, m_sc_conv_w, m_sc_w_out, v_norm_g, v_dn_w_in, v_dn_conv_w, v_dn_a_log, v_dn_dt_bias, v_dn_o_norm_g, v_dn_w_out, v_sb_w_in, v_sb_q_norm_g, v_sb_k_norm_g, v_sb_w_out, v_sc_w_in, v_sc_conv_w, v_sc_w_out):
    w = dict(norm_g=norm_g, dn_w_in=dn_w_in, dn_conv_w=dn_conv_w, dn_a_log=dn_a_log, dn_dt_bias=dn_dt_bias,
             dn_o_norm_g=dn_o_norm_g, dn_w_out=dn_w_out, sb_w_in=sb_w_in, sb_q_norm_g=sb_q_norm_g, sb_k_norm_g=sb_k_norm_g,
             sb_w_out=sb_w_out, sc_w_in=sc_w_in, sc_conv_w=sc_conv_w, sc_w_out=sc_w_out)
    m = dict(norm_g=m_norm_g, dn_w_in=m_dn_w_in, dn_conv_w=m_dn_conv_w, dn_a_log=m_dn_a_log, dn_dt_bias=m_dn_dt_bias,
             dn_o_norm_g=m_dn_o_norm_g, dn_w_out=m_dn_w_out, sb_w_in=m_sb_w_in, sb_q_norm_g=m_sb_q_norm_g,
             sb_k_norm_g=m_sb_k_norm_g, sb_w_out=m_sb_w_out, sc_w_in=m_sc_w_in, sc_conv_w=m_sc_conv_w, sc_w_out=m_sc_w_out)
    v = dict(norm_g=v_norm_g, dn_w_in=v_dn_w_in, dn_conv_w=v_dn_conv_w, dn_a_log=v_dn_a_log, dn_dt_bias=v_dn_dt_bias,
             dn_o_norm_g=v_dn_o_norm_g, dn_w_out=v_dn_w_out, sb_w_in=v_sb_w_in, sb_q_norm_g=v_sb_q_norm_g,
             sb_k_norm_g=v_sb_k_norm_g, sb_w_out=v_sb_w_out, sc_w_in=v_sc_w_in, sc_conv_w=v_sc_conv_w, sc_w_out=v_sc_w_out)

    def gather_of(keys):
        return _Gather([_as_2d(w[k][j]).astype(BF16) if k in _MATMUL_WEIGHTS else _as_2d(w[k][j]) for k, j in keys])

    def full_weights(keys, gathered):
        return {key: _assemble(key[0], g) for key, g in zip(keys, gathered)}

    def exchange_of(keys, grads, extra=()):
        out = [_disassemble(k, grads[k, j].astype(BF16) if k in _MATMUL_WEIGHTS else grads[k, j]) for k, j in keys]
        return _Exchange(out + list(extra), [True] * len(out) + [False] * len(extra))

    F = full_weights(_GATHER_0, _comm_call(gather_of(_GATHER_0), "gather_first"))
    xs, saves = [x[0]], []
    h = _rmsnorm_fwd(xs[0], norm_g[0:1], "norm0")

    def w_out_0(got):
        F.update(full_weights(_GATHER_1, got))
        return F["dn_w_out", 0]

    y, sv, _ = _dn_layer_fwd(h, _dn_split_w_in(F["dn_w_in", 0]), F["dn_conv_w", 0], dn_a_log[0:1], dn_dt_bias[0:1],
                             F["dn_o_norm_g", 0], w_out_0, xs[0], "dn0", gather_of(_GATHER_1))
    xs.append(y)
    saves.append(sv)
    h = _rmsnorm_fwd(xs[1], norm_g[1:2], "norm1")
    y, sv, got = _sb_layer_fwd(h, F["sb_w_in", 0], sb_q_norm_g, sb_k_norm_g, F["sb_w_out", 0], xs[1], "sb", gather_of(_GATHER_2))
    F.update(full_weights(_GATHER_2, got))
    xs.append(y)
    saves.append(sv)
    h = _rmsnorm_fwd(xs[2], norm_g[2:3], "norm2")
    y, sv = _sc_layer_fwd(h, F["sc_w_in", 0], F["sc_conv_w", 0], F["sc_w_out", 0], xs[2], "sc")
    xs.append(y)
    saves.append(sv)
    h = _rmsnorm_fwd(xs[3], norm_g[3:4], "norm3")
    y, sv, _ = _dn_layer_fwd(h, _dn_split_w_in(F["dn_w_in", 1]), F["dn_conv_w", 1], dn_a_log[1:2], dn_dt_bias[1:2],
                             F["dn_o_norm_g", 1], F["dn_w_out", 1], xs[3], "dn1")
    xs.append(y)
    saves.append(sv)
    dx, loss_part = _loss_head(xs[4], loss_target[0])

    G, dnorm, landed = {}, [None] * 4, {}

    def keep(grads, j):
        G.update({(k, j): g for k, g in grads.items()})

    dh, grads, _ = _dn_layer_bwd(dx, saves[3], "dn1")
    keep(grads, 1)
    dx, dnorm[3] = _rmsnorm_bwd(dh, xs[3], norm_g[3:4], dx, "norm3_bwd")
    dh, grads = _sc_layer_bwd(dx, saves[2], "sc")
    keep(grads, 0)
    dx, dnorm[2] = _rmsnorm_bwd(dh, xs[2], norm_g[2:3], dx, "norm2_bwd")
    dh, grads, got = _sb_layer_bwd(dx, saves[1], "sb", exchange_of(_EXCHANGE_A, G))
    keep(grads, 0)
    landed.update(zip(_EXCHANGE_A, got))
    dx, dnorm[1] = _rmsnorm_bwd(dh, xs[1], norm_g[1:2], dx, "norm1_bwd")

    def exchange_b(dw_out):
        G["dn_w_out", 0] = dw_out
        return exchange_of(_EXCHANGE_B, G)

    dh, grads, got = _dn_layer_bwd(dx, saves[0], "dn0", exchange_b)
    keep(grads, 0)
    landed.update(zip(_EXCHANGE_B, got))
    dx, dnorm[0] = _rmsnorm_bwd(dh, xs[0], norm_g[0:1], dx, "norm0_bwd")
    replicated = dict(norm_g=jnp.concatenate(dnorm, axis=0),
                      dn_a_log=jnp.concatenate([G["dn_a_log", 0], G["dn_a_log", 1]], axis=0),
                      dn_dt_bias=jnp.concatenate([G["dn_dt_bias", 0], G["dn_dt_bias", 1]], axis=0),
                      sb_q_norm_g=G["sb_q_norm_g", 0], sb_k_norm_g=G["sb_k_norm_g", 0])
    got = _comm_call(exchange_of(_EXCHANGE_C, G, extra=[_pack_replicated(replicated)]), "exchange_last")
    landed.update(zip(_EXCHANGE_C, got[:-1]))

    res = {}
    for k in _ORDER:
        if k in _REPLICATED:
            continue
        per_layer = []
        for j in range(w[k].shape[0]):
            shape = w[k][j].shape
            outs = _adamw(_as_2d(w[k][j]), _as_2d(m[k][j]), _as_2d(v[k][j]), landed[k, j], f"adamw_{k}{j}")
            per_layer.append([o.reshape(shape) for o in outs])
        res[k] = [jnp.stack([layer[i] for layer in per_layer], axis=0) for i in range(4)]
    outs = _adamw(_pack_replicated(w), _pack_replicated(m), _pack_replicated(v), got[-1], "adamw_replicated")
    unpacked = [_unpack_replicated(o, w) for o in outs]
    for k in _REPLICATED:
        res[k] = [u[k] for u in unpacked]

    loss = lax.psum(loss_part[0, 0], ("x", "y", "c"))
    return (loss, dx[None]) + tuple(res[k][0] for k in _ORDER) + tuple(res[k][1] for k in _ORDER) \
        + tuple(res[k][2] for k in _ORDER) + tuple(res[k][3] for k in _ORDER)
```

```python
import functools
import itertools
import math

import jax
import jax.numpy as jnp
from jax import lax
from jax.experimental import pallas as pl
from jax.experimental.pallas import tpu as pltpu

F32 = jnp.float32
BF16 = jnp.bfloat16
HIGHEST = lax.Precision.HIGHEST

N_DEV = 8
D_MODEL = 1024
RMS_EPS = 1e-6
L2_EPS = 1e-6

DN_HEADS = 8
DN_DK = 128
DN_DV = 256
DN_QK_W = DN_HEADS * DN_DK
DN_V_W = DN_HEADS * DN_DV
DN_CONV = 4
DN_CHUNK = 64
DN_CONV_W = 2 * DN_QK_W + DN_V_W
DN_IN = DN_CONV_W + DN_V_W + 2 * DN_HEADS
DN_AB_PAD = 128
DN_PREP_BLK = 512

SB_HEADS = 16
SB_DH = 64
SB_W = SB_HEADS * SB_DH
SB_PAIRS = SB_HEADS // 2
SB_TQ = 256
SB_TK = 128
SB_DEAD = -106.0

SC_W = 2 * D_MODEL
SC_CONV = 3
SC_BLK = 512
SC_NBLK = SC_W // SC_BLK

ADAM_LR = 0.001
ADAM_B1 = 0.9
ADAM_B2 = 0.999
ADAM_EPS = 1e-08
ADAM_WD = 0.01
ADAM_STEP = 10

LANE = 128
SUBLANE = 8
HALO = SUBLANE
ROW_TILE = 256
WIDE_ROW_TILE = 128
CONV_ROW_TILE = 256
VMEM_LIMIT = 48 * 2 ** 20

NN = ((1,), (0,))
NT = ((1,), (1,))
TN = ((0,), (0,))


def _dot(a, b, dims=NN, precision=None):
    return lax.dot_general(a, b, (dims, ((), ())), precision=precision, preferred_element_type=F32)


def _bdot(a, b, dims=NN):
    return _dot(a.astype(BF16), b.astype(BF16), dims)


def _hdot(a, b, dims=NN):
    return _dot(a, b, dims, precision=HIGHEST)


def _tile(dim, pref, align=LANE):
    t = (min(pref, dim) // align) * align
    while t >= align:
        if dim % t == 0:
            return t
        t -= align
    return dim


def _params(*sem):
    return pltpu.CompilerParams(dimension_semantics=sem, vmem_limit_bytes=VMEM_LIMIT)


def _sigmoid(x):
    return 0.5 * jnp.tanh(0.5 * x) + 0.5


def _softplus(x):
    return jnp.maximum(x, 0.0) + jnp.log(1.0 + jnp.exp(-jnp.abs(x)))


def _silu_and_grad(x):
    s = _sigmoid(x)
    return x * s, s * (1.0 + x * (1.0 - s))


def _iota2(shape, dim):
    return lax.broadcasted_iota(jnp.int32, shape, dim)


def _matmul(a, b, mode, name, out_dtype=F32, add=None, tm=1024, tn=1024, tk=1024):
    if mode == "nn":
        (M, K), (K2, N) = a.shape, b.shape
    elif mode == "nt":
        (M, K), (N, K2) = a.shape, b.shape
    else:
        (K, M), (K2, N) = a.shape, b.shape
    assert K == K2, (a.shape, b.shape, mode)
    tm, tn, tk = _tile(M, tm), _tile(N, tn), _tile(K, tk)
    nk = K // tk
    dims = {"nn": NN, "nt": NT, "tn": TN}[mode]
    a_spec = pl.BlockSpec((tk, tm), lambda i, j, k: (k, i)) if mode == "tn" else pl.BlockSpec((tm, tk), lambda i, j, k: (i, k))
    b_spec = pl.BlockSpec((tn, tk), lambda i, j, k: (j, k)) if mode == "nt" else pl.BlockSpec((tk, tn), lambda i, j, k: (k, j))
    o_spec = pl.BlockSpec((tm, tn), lambda i, j, k: (i, j))
    has_add = add is not None

    def body(*refs):
        a_ref, b_ref = refs[0], refs[1]
        add_ref = refs[2] if has_add else None
        o_ref = refs[3] if has_add else refs[2]
        p = _bdot(a_ref[...], b_ref[...], dims)

        def finish(acc):
            if has_add:
                acc = acc + add_ref[...]
            o_ref[...] = acc.astype(out_dtype)

        if nk == 1:
            finish(p)
        else:
            acc_ref = refs[-1]
            k = pl.program_id(2)

            @pl.when(k == 0)
            def _():
                acc_ref[...] = p

            @pl.when(k > 0)
            def _():
                acc_ref[...] += p

            @pl.when(k == nk - 1)
            def _():
                finish(acc_ref[...])

    in_specs = [a_spec, b_spec] + ([o_spec] if has_add else [])
    args = (a, b) + ((add,) if has_add else ())
    return pl.pallas_call(
        body, name=name, grid=(M // tm, N // tn, nk),
        in_specs=in_specs, out_specs=o_spec,
        out_shape=jax.ShapeDtypeStruct((M, N), out_dtype),
        scratch_shapes=[pltpu.VMEM((tm, tn), F32)] if nk > 1 else [],
        compiler_params=_params("parallel", "parallel", "arbitrary"),
    )(*args)


def _matmul_nt_sum(pairs, name, comm=None, tm=1024, tk=1024):
    M, N = pairs[0][0].shape[0], pairs[0][1].shape[0]
    tm = _tile(M, tm)
    tks = [_tile(a.shape[1], tk) for a, _ in pairs]
    steps = [a.shape[1] // t for (a, _), t in zip(pairs, tks)]
    offs = [sum(steps[:p]) for p in range(len(pairs))]
    total = sum(steps)

    def body(*refs):
        a_refs, b_refs = refs[0:2 * len(pairs):2], refs[1:2 * len(pairs):2]
        o_ref, acc_ref = refs[2 * len(pairs)], refs[2 * len(pairs) + 1]
        k = pl.program_id(1)
        for p in range(len(pairs)):
            @pl.when((k >= offs[p]) & (k < offs[p] + steps[p]))
            def _(p=p):
                prod = _bdot(a_refs[p][...], b_refs[p][...], NT)
                if p == 0:
                    @pl.when(k == 0)
                    def _():
                        acc_ref[...] = prod

                    @pl.when(k > 0)
                    def _():
                        acc_ref[...] += prod
                else:
                    acc_ref[...] += prod

        @pl.when(k == total - 1)
        def _():
            o_ref[...] = acc_ref[...]

    in_specs, args = [], []
    for (a, b), t, off, n in zip(pairs, tks, offs, steps):
        pick = lambda k, off=off, n=n: jnp.clip(k - off, 0, n - 1)
        in_specs += [pl.BlockSpec((tm, t), lambda i, k, pick=pick: (i, pick(k))),
                     pl.BlockSpec((N, t), lambda i, k, pick=pick: (0, pick(k)))]
        args += [a, b]
    outs, landed = _call(body, comm, name=name, grid=(M // tm, total), in_specs=in_specs,
                         out_specs=[pl.BlockSpec((tm, N), lambda i, k: (i, 0))],
                         out_shape=[jax.ShapeDtypeStruct((M, N), F32)], scratch_shapes=[pltpu.VMEM((tm, N), F32)],
                         semantics=("parallel", "arbitrary"), args=tuple(args))
    return outs[0], landed


def _rmsnorm_fwd(x, g, name):
    T, D = x.shape
    tt = _tile(T, 512, SUBLANE)

    def body(x_ref, g_ref, o_ref):
        xv = x_ref[...]
        r = lax.rsqrt(jnp.mean(xv * xv, axis=-1, keepdims=True) + RMS_EPS)
        o_ref[...] = (xv * r * g_ref[...]).astype(BF16)

    return pl.pallas_call(
        body, name=name, grid=(T // tt,),
        in_specs=[pl.BlockSpec((tt, D), lambda i: (i, 0)), pl.BlockSpec((1, D), lambda i: (0, 0))],
        out_specs=pl.BlockSpec((tt, D), lambda i: (i, 0)),
        out_shape=jax.ShapeDtypeStruct((T, D), BF16),
        compiler_params=_params("parallel"),
    )(x, g)


def _rmsnorm_bwd(dh, x, g, dx_res, name):
    T, D = x.shape
    tt = _tile(T, 256, SUBLANE)

    def body(dh_ref, x_ref, g_ref, res_ref, dx_ref, dg_ref):
        xv, dhv = x_ref[...], dh_ref[...]
        r = lax.rsqrt(jnp.mean(xv * xv, axis=-1, keepdims=True) + RMS_EPS)
        xh = xv * r
        dxh = dhv * g_ref[...]
        m = jnp.mean(dxh * xh, axis=-1, keepdims=True)
        dx_ref[...] = res_ref[...] + r * (dxh - xh * m)
        part = jnp.sum(dhv * xh, axis=0, keepdims=True)

        @pl.when(pl.program_id(0) == 0)
        def _():
            dg_ref[...] = part

        @pl.when(pl.program_id(0) > 0)
        def _():
            dg_ref[...] += part

    row = pl.BlockSpec((tt, D), lambda i: (i, 0))
    vec = pl.BlockSpec((1, D), lambda i: (0, 0))
    return pl.pallas_call(
        body, name=name, grid=(T // tt,),
        in_specs=[row, row, vec, row], out_specs=[row, vec],
        out_shape=[jax.ShapeDtypeStruct((T, D), F32), jax.ShapeDtypeStruct((1, D), F32)],
        compiler_params=_params("arbitrary"),
    )(dh, x, g, dx_res)


def _loss_head(y, target, name="loss_head"):
    T, D = y.shape
    tt = _tile(T, 512, SUBLANE)

    def body(y_ref, t_ref, dy_ref, l_ref):
        e = y_ref[...] - t_ref[...]
        dy_ref[...] = e * (1.0 / D)
        s = jnp.sum(jnp.sum(e * e, axis=1, keepdims=True), axis=0, keepdims=True) * (0.5 / D)
        s = jnp.broadcast_to(s, (1, LANE))

        @pl.when(pl.program_id(0) == 0)
        def _():
            l_ref[...] = s

        @pl.when(pl.program_id(0) > 0)
        def _():
            l_ref[...] += s

    row = pl.BlockSpec((tt, D), lambda i: (i, 0))
    return pl.pallas_call(
        body, name=name, grid=(T // tt,),
        in_specs=[row, row], out_specs=[row, pl.BlockSpec((1, LANE), lambda i: (0, 0))],
        out_shape=[jax.ShapeDtypeStruct((T, D), F32), jax.ShapeDtypeStruct((1, LANE), F32)],
        compiler_params=_params("arbitrary"),
    )(y, target)


def _down(x, k):
    return pltpu.roll(x, k, 0) if k else x


def _up(x, k):
    return pltpu.roll(x, x.shape[0] - k, 0) if k else x


def _sc_fwd(proj, conv_w, name):
    T = proj.shape[0]
    tt = _tile(T, WIDE_ROW_TILE, SUBLANE)
    B = SC_BLK

    def body(p_ref, ph_ref, w_ref, o_ref):
        keep = (pl.program_id(0) > 0).astype(F32)
        for j in range(SC_NBLK):
            cb, cc, cu, cg = (slice(k * SC_W + j * B, k * SC_W + (j + 1) * B) for k in range(4))
            cw = slice(j * B, (j + 1) * B)
            z = jnp.concatenate([ph_ref[:, cc] * ph_ref[:, cu] * keep, p_ref[:, cc] * p_ref[:, cu]], axis=0)
            cz = (w_ref[2:3, cw] * z + w_ref[1:2, cw] * _down(z, 1) + w_ref[0:1, cw] * _down(z, 2))[HALO:]
            gate = p_ref[:, cg]
            o_ref[:, cw] = (p_ref[:, cb] * cz * (gate * _sigmoid(gate))).astype(BF16)

    return pl.pallas_call(
        body, name=name, grid=(T // tt,),
        in_specs=[pl.BlockSpec((tt, 4 * SC_W), lambda i: (i, 0)),
                  pl.BlockSpec((HALO, 4 * SC_W), lambda i: (jnp.maximum(i * (tt // HALO) - 1, 0), 0)),
                  pl.BlockSpec((SC_CONV, SC_W), lambda i: (0, 0))],
        out_specs=pl.BlockSpec((tt, SC_W), lambda i: (i, 0)),
        out_shape=jax.ShapeDtypeStruct((T, SC_W), BF16),
        compiler_params=_params("parallel"),
    )(proj, proj, conv_w)


def _sc_bwd(dyg, proj, conv_w, name):
    T = proj.shape[0]
    tt = _tile(T, WIDE_ROW_TILE, SUBLANE)
    nt = T // tt
    B = SC_BLK
    hb = tt // HALO

    def body(d_ref, dn_ref, p_ref, pp_ref, pn_ref, w_ref, o_ref, dw_ref):
        i = pl.program_id(0)
        keep_p = (i > 0).astype(F32)
        keep_n = (i < nt - 1).astype(F32)
        main = slice(HALO, HALO + tt)
        parts = []
        for j in range(SC_NBLK):
            cw = slice(j * B, (j + 1) * B)

            def ext(k):
                s = slice(k * SC_W + j * B, k * SC_W + (j + 1) * B)
                return s, jnp.concatenate([pp_ref[:, s] * keep_p, p_ref[:, s], pn_ref[:, s]], axis=0)

            (sb, b), (sc, c), (su, u), (sg_, gate) = ext(0), ext(1), ext(2), ext(3)
            dyg_e = jnp.concatenate([jnp.zeros((HALO, B), F32), d_ref[:, cw], dn_ref[:, cw] * keep_n], axis=0)
            w0, w1, w2 = w_ref[0:1, cw], w_ref[1:2, cw], w_ref[2:3, cw]
            z = c * u
            z1, z2 = _down(z, 1), _down(z, 2)
            cz = w2 * z + w1 * z1 + w0 * z2
            sg, dsg = _silu_and_grad(gate)
            dy = dyg_e * sg
            dcz = dy * b
            dz = w2 * dcz + w1 * _up(dcz, 1) + w0 * _up(dcz, 2)
            o_ref[:, sb] = (dy * cz)[main].astype(BF16)
            o_ref[:, sc] = (dz * u)[main].astype(BF16)
            o_ref[:, su] = (dz * c)[main].astype(BF16)
            o_ref[:, sg_] = (dyg_e * (b * cz) * dsg)[main].astype(BF16)
            dcm = dcz[main]
            parts.append(jnp.concatenate([jnp.sum(dcm * z2[main], axis=0, keepdims=True),
                                          jnp.sum(dcm * z1[main], axis=0, keepdims=True),
                                          jnp.sum(dcm * z[main], axis=0, keepdims=True)], axis=0))
        part = jnp.concatenate(parts, axis=1)

        @pl.when(i == 0)
        def _():
            dw_ref[...] = part

        @pl.when(i > 0)
        def _():
            dw_ref[...] += part

    nxt = lambda i: (jnp.minimum((i + 1) * hb, nt * hb - 1), 0)
    return pl.pallas_call(
        body, name=name, grid=(nt,),
        in_specs=[pl.BlockSpec((tt, SC_W), lambda i: (i, 0)),
                  pl.BlockSpec((HALO, SC_W), nxt),
                  pl.BlockSpec((tt, 4 * SC_W), lambda i: (i, 0)),
                  pl.BlockSpec((HALO, 4 * SC_W), lambda i: (jnp.maximum(i * hb - 1, 0), 0)),
                  pl.BlockSpec((HALO, 4 * SC_W), nxt),
                  pl.BlockSpec((SC_CONV, SC_W), lambda i: (0, 0))],
        out_specs=[pl.BlockSpec((tt, 4 * SC_W), lambda i: (i, 0)), pl.BlockSpec((SC_CONV, SC_W), lambda i: (0, 0))],
        out_shape=[jax.ShapeDtypeStruct((T, 4 * SC_W), BF16), jax.ShapeDtypeStruct((SC_CONV, SC_W), F32)],
        compiler_params=_params("arbitrary"),
    )(dyg, dyg, proj, proj, proj, conv_w)


def _split3_dot(x, m):
    hi = x.astype(BF16)
    r1 = x - hi.astype(F32)
    mid = r1.astype(BF16)
    lo = (r1 - mid.astype(F32)).astype(BF16)
    return _dot(hi, m) + _dot(mid, m) + _dot(lo, m)


def _split2_dot(x, m):
    hi = x.astype(BF16)
    lo = (x - hi.astype(F32)).astype(BF16)
    return _dot(hi, m) + _dot(lo, m)


def _head_mean_matrix():
    r, c = _iota2((LANE, LANE), 0), _iota2((LANE, LANE), 1)
    return jnp.where((r // SB_DH) == (c // SB_DH), 1.0 / SB_DH, 0.0).astype(BF16)


def _sb_prep(proj, qg2, kg2, name):
    T = proj.shape[0]
    tt = _tile(T, WIDE_ROW_TILE, SUBLANE)

    def body(p_ref, qg_ref, kg_ref, q_ref, k_ref, v_ref):
        bd = _head_mean_matrix()

        def norm(x, g, scale):
            r = lax.rsqrt(_split3_dot(x * x, bd) + RMS_EPS)
            return (x * r * g * scale).astype(BF16)

        v_ref[...] = p_ref[:, 2 * SB_W:3 * SB_W].astype(BF16)
        for p in range(SB_PAIRS):
            cols = slice(p * LANE, (p + 1) * LANE)
            q_ref[:, cols] = norm(p_ref[:, cols], qg_ref[...], SB_DH ** -0.5)
            k_ref[:, cols] = norm(p_ref[:, SB_W + p * LANE:SB_W + (p + 1) * LANE], kg_ref[...], 1.0)

    blk = pl.BlockSpec((tt, SB_W), lambda i: (i, 0))
    vec = pl.BlockSpec((1, LANE), lambda i: (0, 0))
    return pl.pallas_call(
        body, name=name, grid=(T // tt,),
        in_specs=[pl.BlockSpec((tt, 4 * SB_W), lambda i: (i, 0)), vec, vec],
        out_specs=[blk, blk, blk],
        out_shape=[jax.ShapeDtypeStruct((T, SB_W), BF16)] * 3,
        compiler_params=_params("parallel"),
    )(proj, qg2, kg2)


def _sb_prep_bwd(proj, dqn, dkn, dv, dgate, qg2, kg2, name):
    T = proj.shape[0]
    tt = _tile(T, WIDE_ROW_TILE, SUBLANE)

    def body(p_ref, dq_ref, dk_ref, dv_ref, dg_ref, qg_ref, kg_ref, o_ref, dqg_ref, dkg_ref):
        i = pl.program_id(0)
        bd = _head_mean_matrix()

        def norm_bwd(x, g, dy):
            r = lax.rsqrt(_split3_dot(x * x, bd) + RMS_EPS)
            xh = x * r
            dxh = dy * g
            m = _split3_dot(dxh * xh, bd)
            return r * (dxh - xh * m), jnp.sum(dy * xh, axis=0, keepdims=True)

        o_ref[:, 2 * SB_W:3 * SB_W] = dv_ref[...].astype(BF16)
        o_ref[:, 3 * SB_W:4 * SB_W] = dg_ref[...].astype(BF16)
        pq = pk = jnp.zeros((1, LANE), F32)
        for p in range(SB_PAIRS):
            cols, kcols = slice(p * LANE, (p + 1) * LANE), slice(SB_W + p * LANE, SB_W + (p + 1) * LANE)
            dxq, sq = norm_bwd(p_ref[:, cols], qg_ref[...], dq_ref[:, cols])
            dxk, sk = norm_bwd(p_ref[:, kcols], kg_ref[...], dk_ref[:, cols])
            o_ref[:, cols] = dxq.astype(BF16)
            o_ref[:, kcols] = dxk.astype(BF16)
            pq, pk = pq + sq, pk + sk

        @pl.when(i == 0)
        def _():
            dqg_ref[...] = pq
            dkg_ref[...] = pk

        @pl.when(i > 0)
        def _():
            dqg_ref[...] += pq
            dkg_ref[...] += pk

    blk = pl.BlockSpec((tt, SB_W), lambda i: (i, 0))
    vec = pl.BlockSpec((1, LANE), lambda i: (0, 0))
    wide = pl.BlockSpec((tt, 4 * SB_W), lambda i: (i, 0))
    return pl.pallas_call(
        body, name=name, grid=(T // tt,),
        in_specs=[wide, blk, blk, blk, blk, vec, vec],
        out_specs=[wide, vec, vec],
        out_shape=[jax.ShapeDtypeStruct((T, 4 * SB_W), BF16)] + [jax.ShapeDtypeStruct((1, LANE), F32)] * 2,
        compiler_params=_params("arbitrary"),
    )(proj, dqn, dkn, dv, dgate, qg2, kg2)


def _fold_heads(part, name):
    def body(p_ref, o_ref):
        r, c = _iota2((LANE, SB_DH), 0), _iota2((LANE, SB_DH), 1)
        fold = jnp.where((r % SB_DH) == c, 1.0, 0.0).astype(F32)
        o_ref[...] = jnp.sum(_hdot(p_ref[...], fold), axis=0, keepdims=True)

    return pl.pallas_call(body, name=name, out_shape=jax.ShapeDtypeStruct((1, SB_DH), F32))(part)


def _sb_masks():
    lane = _iota2((1, LANE), 1)
    return lane < SB_DH


def _sb_attn_fwd(qn, kn, vb, proj, name, comm=None):
    T = qn.shape[0]
    tq, tk = _tile(T, SB_TQ, SUBLANE), SB_TK
    assert tq % tk == 0

    def body(q_ref, k_ref, v_ref, g_ref, o_ref, og_ref, lt_ref, done_ref):
        i = pl.program_id(1)
        ma = _sb_masks()
        q2 = q_ref[...]
        zero = jnp.zeros_like(q2)
        qs = (jnp.where(ma, q2, zero), jnp.where(ma, zero, q2))
        upper = (_iota2((tk, tk), 0) > _iota2((tk, tk), 1)).astype(BF16)
        qpos = i * tq + _iota2((tq, tk), 0)
        nb = tq // tk

        def trip(kb_top, masked, carry):
            acc, la, lb = carry
            chains = [(b, h) for b in range(nb) for h in range(2)]
            k2s, vss, masks = [], [], []
            for b in range(nb):
                kb = kb_top - b
                rows = pl.ds(pl.multiple_of(kb * tk, tk), tk)
                k2s.append(k_ref[rows, :])
                v2 = v_ref[rows, :]
                zv = jnp.zeros_like(v2)
                vss.append((jnp.where(ma, v2, zv), jnp.where(ma, zv, v2)))
                masks.append((kb * tk + _iota2((tq, tk), 1)) < qpos if masked else None)
            zs = [_dot(qs[h], k2s[b], NT) for b, h in chains]
            ts = [jnp.log(1.0 + jnp.exp(-jnp.abs(z))) for z in zs]
            ls = [-(jnp.maximum(z, 0.0) + t) for z, t in zip(zs, ts)]
            if masked:
                ls = [jnp.where(masks[b], l, 0.0) for (b, h), l in zip(chains, ls)]
            cums = [_split2_dot(l, upper) for l in ls]
            sums = [jnp.sum(l, axis=1, keepdims=True) for l in ls]
            offs, tot = {}, [la, lb]
            for b in range(nb):
                for h in range(2):
                    offs[(b, h)] = tot[h]
                    tot[h] = tot[h] + sums[chains.index((b, h))]
            ws = [jnp.exp(jnp.minimum(z, 0.0) - t + c + offs[ch]) for ch, z, t, c in zip(chains, zs, ts, cums)]
            if masked:
                ws = [jnp.where(masks[b], w, 0.0) for (b, h), w in zip(chains, ws)]
            for (b, h), w in zip(chains, ws):
                acc = acc + _dot(w.astype(BF16), vss[b][h])
            return acc, tot[0], tot[1]

        def largest(la, lb):
            return jnp.max(jnp.maximum(la, lb))

        z1 = jnp.zeros((tq, 1), F32)
        acc, la, lb = trip((i + 1) * nb - 1, True, (jnp.zeros((tq, LANE), F32), z1, z1))

        def live(c):
            return (c[0] < i) & (c[4] > SB_DEAD)

        def more(c):
            j, acc, la, lb, _ = c
            acc, la, lb = trip((i - j) * nb - 1, False, (acc, la, lb))
            return j + 1, acc, la, lb, largest(la, lb)

        done, acc, la, lb, _ = lax.while_loop(live, more, (jnp.int32(0), acc, la, lb, largest(la, lb)))
        gate = g_ref[...]
        o_ref[...] = acc
        og_ref[...] = (acc * (gate * _sigmoid(gate))).astype(BF16)
        lt_ref[...] = jnp.where(_iota2((tq, 2), 1) == 0, la, lb)
        done_ref[...] = jnp.full((SUBLANE, LANE), done, F32)

    nq = T // tq
    qblk = pl.BlockSpec((tq, LANE), lambda p, i: (i, p))
    full = pl.BlockSpec((T, LANE), lambda p, i: (0, p))
    return _call(
        body, comm, name=name, grid=(SB_PAIRS, nq),
        in_specs=[qblk, full, full, pl.BlockSpec((tq, LANE), lambda p, i: (i, 3 * SB_PAIRS + p))],
        out_specs=[qblk, qblk, pl.BlockSpec((None, tq, 2), lambda p, i: (p, i, 0)),
                   pl.BlockSpec((None, None, SUBLANE, LANE), lambda p, i: (p, i, 0, 0))],
        out_shape=[jax.ShapeDtypeStruct((T, SB_W), F32), jax.ShapeDtypeStruct((T, SB_W), BF16),
                   jax.ShapeDtypeStruct((SB_PAIRS, T, 2), F32), jax.ShapeDtypeStruct((SB_PAIRS, nq, SUBLANE, LANE), F32)],
        scratch_shapes=[], semantics=("parallel", "parallel"), args=(qn, kn, vb, proj))


def _sb_attn_bwd(qn, kn, vb, dog, o, ltot, done, proj, name, comm=None):
    T = qn.shape[0]
    tq, tk = _tile(T, SB_TQ, SUBLANE), SB_TK

    def body(q_ref, k_ref, v_ref, dog_ref, o_ref, lt_ref, done_ref, g_ref, dq_ref, dk_ref, dv_ref, dgate_ref):
        i = pl.program_id(1)
        first_trip = i - jnp.max(done_ref[...]).astype(jnp.int32)

        @pl.when(i == 0)
        def _():
            dk_ref[...] = jnp.zeros_like(dk_ref)
            dv_ref[...] = jnp.zeros_like(dv_ref)

        ma = _sb_masks()
        gate, o2, dog2 = g_ref[...], o_ref[...], dog_ref[...]
        sg, dsg = _silu_and_grad(gate)
        do2 = dog2 * sg
        dgate_ref[...] = dog2 * o2 * dsg
        lt = lt_ref[...]
        first = _iota2((tq, 2), 1) == 0
        ltots = (jnp.sum(jnp.where(first, lt, 0.0), axis=1, keepdims=True),
                 jnp.sum(jnp.where(first, 0.0, lt), axis=1, keepdims=True))
        q2 = q_ref[...]
        zq = jnp.zeros_like(q2)
        qs = (jnp.where(ma, q2, zq), jnp.where(ma, zq, q2))
        dob = do2.astype(BF16)
        dos = (jnp.where(ma, dob, zq), jnp.where(ma, zq, dob))
        upto = (_iota2((tk, tk), 0) <= _iota2((tk, tk), 1)).astype(BF16)
        before = (_iota2((tk, tk), 0) < _iota2((tk, tk), 1)).astype(BF16)
        qpos = i * tq + _iota2((tq, tk), 0)
        nb = tq // tk

        def trip(kb_bot, masked, carry):
            dq, la, lb, ea, eb = carry
            chains = [(b, h) for b in range(nb) for h in range(2)]
            rows, k2s, v2s, kss, masks = [], [], [], [], []
            for b in range(nb):
                kb = kb_bot + b
                rows.append(pl.ds(pl.multiple_of(kb * tk, tk), tk))
                k2 = k_ref[rows[b], :]
                zk = jnp.zeros_like(k2)
                k2s.append(k2)
                v2s.append(v_ref[rows[b], :])
                kss.append((jnp.where(ma, k2, zk), jnp.where(ma, zk, k2)))
                masks.append((kb * tk + _iota2((tq, tk), 1)) < qpos if masked else None)

            def keep(vals):
                return [jnp.where(masks[b], x, 0.0) for (b, h), x in zip(chains, vals)] if masked else vals

            zs = [_dot(qs[h], k2s[b], NT) for b, h in chains]
            dws = [_dot(dos[h], v2s[b], NT) for b, h in chains]
            ts = [jnp.log(1.0 + jnp.exp(-jnp.abs(z))) for z in zs]
            ls = keep([-(jnp.maximum(z, 0.0) + t) for z, t in zip(zs, ts)])
            lps = [jnp.minimum(z, 0.0) - t for z, t in zip(zs, ts)]
            cums = [_split3_dot(l, upto) for l in ls]
            lsums = [jnp.sum(l, axis=1, keepdims=True) for l in ls]
            offs, tot = {}, [la, lb]
            for b in range(nb):
                for h in range(2):
                    offs[(b, h)] = tot[h]
                    tot[h] = tot[h] + lsums[chains.index((b, h))]
            ws = keep([jnp.exp(lp + (ltots[h] - (offs[(b, h)] + c))) for (b, h), lp, c in zip(chains, lps, cums)])
            es = [dw * w for dw, w in zip(dws, ws)]
            ecums = [_split2_dot(e, before) for e in es]
            esums = [jnp.sum(e, axis=1, keepdims=True) for e in es]
            eoffs, etot = {}, [ea, eb]
            for b in range(nb):
                for h in range(2):
                    eoffs[(b, h)] = etot[h]
                    etot[h] = etot[h] + esums[chains.index((b, h))]
            dzs = keep([e - jnp.exp(lp) * (e + eoffs[ch] + ec) for ch, e, lp, ec in zip(chains, es, lps, ecums)])
            dzs = [dz.astype(BF16) for dz in dzs]
            wbs = [w.astype(BF16) for w in ws]
            for (b, h), dz in zip(chains, dzs):
                dq = dq + _dot(dz, kss[b][h])
            for b in range(nb):
                ia, ib = chains.index((b, 0)), chains.index((b, 1))
                dk_ref[rows[b], :] += _dot(dzs[ia], qs[0], TN) + _dot(dzs[ib], qs[1], TN)
                dv_ref[rows[b], :] += _dot(wbs[ia], dos[0], TN) + _dot(wbs[ib], dos[1], TN)
            return dq, tot[0], tot[1], etot[0], etot[1]

        z1 = jnp.zeros((tq, 1), F32)
        carry = lax.fori_loop(first_trip, i, lambda j, c: trip(j * nb, False, c),
                              (jnp.zeros((tq, LANE), F32), z1, z1, z1, z1))
        dq = trip(i * nb, True, carry)[0]
        dq_ref[...] = dq * (SB_DH ** -0.5)

    qblk = pl.BlockSpec((tq, LANE), lambda p, i: (i, p))
    full = pl.BlockSpec((T, LANE), lambda p, i: (0, p))
    return _call(
        body, comm, name=name, grid=(SB_PAIRS, T // tq),
        in_specs=[qblk, full, full, qblk, qblk, pl.BlockSpec((None, tq, 2), lambda p, i: (p, i, 0)),
                  pl.BlockSpec((None, None, SUBLANE, LANE), lambda p, i: (p, i, 0, 0)),
                  pl.BlockSpec((tq, LANE), lambda p, i: (i, 3 * SB_PAIRS + p))],
        out_specs=[qblk, full, full, qblk],
        out_shape=[jax.ShapeDtypeStruct((T, SB_W), F32)] * 4,
        scratch_shapes=[], semantics=("parallel", "arbitrary"), args=(qn, kn, vb, dog, o, ltot, done, proj))


def _dn_conv(ext, w_ref, cw):
    return (w_ref[3:4, cw] * ext + w_ref[2:3, cw] * _down(ext, 1) + w_ref[1:2, cw] * _down(ext, 2)
            + w_ref[0:1, cw] * _down(ext, 3))


def _dn_prep(pqkv, conv_w, name):
    T, W = pqkv.shape
    tt = _tile(T, CONV_ROW_TILE, SUBLANE)
    B = DN_PREP_BLK
    nq, nqk = DN_QK_W // B, 2 * DN_QK_W // B

    def body(p_ref, ph_ref, w_ref, o_ref):
        keep = (pl.program_id(0) > 0).astype(F32)
        for cb in range(W // B):
            cw = slice(cb * B, (cb + 1) * B)
            ext = jnp.concatenate([ph_ref[:, cw] * keep, p_ref[:, cw]], axis=0)
            c = _dn_conv(ext, w_ref, cw)[HALO:]
            a = c * _sigmoid(c)
            if cb >= nqk:
                o_ref[:, cw] = a
                continue
            scale = DN_DK ** -0.5 if cb < nq else 1.0
            for hh in range(B // DN_DK):
                ah = a[:, hh * DN_DK:(hh + 1) * DN_DK]
                r = lax.rsqrt(jnp.sum(ah * ah, axis=-1, keepdims=True) + L2_EPS)
                o_ref[:, cb * B + hh * DN_DK:cb * B + (hh + 1) * DN_DK] = ah * (r * scale)

    return pl.pallas_call(
        body, name=name, grid=(T // tt,),
        in_specs=[pl.BlockSpec((tt, W), lambda i: (i, 0)),
                  pl.BlockSpec((HALO, W), lambda i: (jnp.maximum(i * (tt // HALO) - 1, 0), 0)),
                  pl.BlockSpec((DN_CONV, W), lambda i: (0, 0))],
        out_specs=pl.BlockSpec((tt, W), lambda i: (i, 0)),
        out_shape=jax.ShapeDtypeStruct((T, W), F32),
        compiler_params=_params("parallel"),
    )(pqkv, pqkv, conv_w)


def _dn_prep_bwd(pqkv, conv_w, dact, name):
    T, W = pqkv.shape
    tt = _tile(T, CONV_ROW_TILE, SUBLANE)
    nt = T // tt
    hb = tt // HALO
    B = DN_PREP_BLK
    nq, nqk = DN_QK_W // B, 2 * DN_QK_W // B

    def body(p_ref, pp_ref, pn_ref, w_ref, d_ref, dn_ref, o_ref, dw_ref):
        i = pl.program_id(0)
        keep_p = (i > 0).astype(F32)
        keep_n = (i < nt - 1).astype(F32)
        main = slice(HALO, HALO + tt)
        parts = []
        for cb in range(W // B):
            cw = slice(cb * B, (cb + 1) * B)
            ext = jnp.concatenate([pp_ref[:, cw] * keep_p, p_ref[:, cw], pn_ref[:, cw]], axis=0)
            c = _dn_conv(ext, w_ref, cw)
            s = _sigmoid(c)
            da_dc = s * (1.0 + c * (1.0 - s))
            d_up = jnp.concatenate([jnp.zeros((HALO, B), F32), d_ref[:, cw], dn_ref[:, cw] * keep_n], axis=0)
            if cb < nqk:
                a = c * s
                scale = DN_DK ** -0.5 if cb < nq else 1.0
                normed = []
                for hh in range(B // DN_DK):
                    cols = slice(hh * DN_DK, (hh + 1) * DN_DK)
                    ah = a[:, cols]
                    r = lax.rsqrt(jnp.sum(ah * ah, axis=-1, keepdims=True) + L2_EPS)
                    y = ah * r
                    dy = d_up[:, cols] * scale
                    normed.append(r * (dy - y * jnp.sum(dy * y, axis=-1, keepdims=True)))
                d_up = jnp.concatenate(normed, axis=1)
            dc = d_up * da_dc
            dp = (w_ref[3:4, cw] * dc + w_ref[2:3, cw] * _up(dc, 1) + w_ref[1:2, cw] * _up(dc, 2)
                  + w_ref[0:1, cw] * _up(dc, 3))
            o_ref[:, cw] = dp[main].astype(BF16)
            dcm = dc[main]
            parts.append(jnp.concatenate([jnp.sum(dcm * _down(ext, 3 - k)[main], axis=0, keepdims=True)
                                          for k in range(DN_CONV)], axis=0))
        part = jnp.concatenate(parts, axis=1)

        @pl.when(i == 0)
        def _():
            dw_ref[...] = part

        @pl.when(i > 0)
        def _():
            dw_ref[...] += part

    main_spec = pl.BlockSpec((tt, W), lambda i: (i, 0))
    prev_spec = pl.BlockSpec((HALO, W), lambda i: (jnp.maximum(i * hb - 1, 0), 0))
    next_spec = pl.BlockSpec((HALO, W), lambda i: (jnp.minimum((i + 1) * hb, nt * hb - 1), 0))
    w_spec = pl.BlockSpec((DN_CONV, W), lambda i: (0, 0))
    return pl.pallas_call(
        body, name=name, grid=(nt,),
        in_specs=[main_spec, prev_spec, next_spec, w_spec, main_spec, next_spec],
        out_specs=[main_spec, w_spec],
        out_shape=[jax.ShapeDtypeStruct((T, W), BF16), jax.ShapeDtypeStruct((DN_CONV, W), F32)],
        compiler_params=_params("arbitrary"),
    )(pqkv, pqkv, pqkv, conv_w, dact, dact)


def _dn_gates(a_in, b_in, a_log, dt_bias, name):
    T, H = a_in.shape
    C = DN_CHUNK

    def body(a_ref, b_ref, al_ref, dt_ref, g_ref, beta_ref):
        beta_ref[...] = _sigmoid(b_ref[...])
        g_ref[...] = -jnp.exp(al_ref[...]) * _softplus(a_ref[...] + dt_ref[...])
        tri = (_iota2((C, C), 0) >= _iota2((C, C), 1)).astype(F32)

        def chunk(n, carry):
            rows = pl.ds(pl.multiple_of(n * C, C), C)
            g_ref[rows, :] = _hdot(tri, g_ref[rows, :])
            return carry

        lax.fori_loop(0, T // C, chunk, 0)

    return pl.pallas_call(body, name=name, out_shape=[jax.ShapeDtypeStruct((T, H), F32)] * 2)(a_in, b_in, a_log, dt_bias)


def _dn_gates_bwd(dg, dbeta, a_in, b_in, a_log, dt_bias, name):
    T, H = a_in.shape
    C = DN_CHUNK

    def body(dg_ref, db_ref, a_ref, b_ref, al_ref, dt_ref, da_ref, dbi_ref, dal_ref, ddt_ref):
        tri_t = (_iota2((C, C), 0) <= _iota2((C, C), 1)).astype(F32)

        def chunk(n, carry):
            rows = pl.ds(pl.multiple_of(n * C, C), C)
            da_ref[rows, :] = _hdot(tri_t, dg_ref[rows, :])
            return carry

        lax.fori_loop(0, T // C, chunk, 0)
        dla = da_ref[...]
        x = a_ref[...] + dt_ref[...]
        ea = jnp.exp(al_ref[...])
        da = dla * (-ea) * _sigmoid(x)
        da_ref[...] = da
        dal_ref[...] = jnp.sum(dla * (-ea * _softplus(x)), axis=0, keepdims=True)
        ddt_ref[...] = jnp.sum(da, axis=0, keepdims=True)
        beta = _sigmoid(b_ref[...])
        dbi_ref[...] = db_ref[...] * beta * (1.0 - beta)

    return pl.pallas_call(
        body, name=name,
        out_shape=[jax.ShapeDtypeStruct((T, H), F32)] * 2 + [jax.ShapeDtypeStruct((1, H), F32)] * 2,
    )(dg, dbeta, a_in, b_in, a_log, dt_bias)


def _dn_chunk_terms(q, k, gc, bc):
    C = DN_CHUNK
    r, c = _iota2((C, C), 0), _iota2((C, C), 1)
    lower, strict, eye = r >= c, r > c, r == c
    grow = jnp.sum(jnp.where(eye, gc, 0.0), axis=0, keepdims=True)
    decay = jnp.where(lower, jnp.exp(jnp.where(lower, gc - grow, 0.0)), 0.0)
    last = _iota2((C, 1), 0) == C - 1
    gl = jnp.sum(jnp.where(last, gc, 0.0), axis=0, keepdims=True)
    eg = jnp.exp(gc)
    egl = jnp.exp(gl - gc)
    kb = k * bc
    lmat = jnp.where(strict, _bdot(kb, k, NT) * decay, 0.0)
    aqk = jnp.where(lower, _bdot(q, k, NT) * decay, 0.0)
    return dict(lower=lower, strict=strict, eye=eye, last=last, decay=decay, gl=gl, eg=eg, egl=egl, kb=kb,
                lmat=lmat, aqk=aqk, qd=q * eg, kd=k * egl)


def _split(x):
    hi = x.astype(BF16)
    return hi, (x - hi.astype(F32)).astype(BF16)


def _x3dot(a, b, dims=NN):
    ah, al = a if isinstance(a, tuple) else _split(a)
    bh, bl = b if isinstance(b, tuple) else _split(b)
    return _dot(ah, bh, dims) + (_dot(ah, bl, dims) + _dot(al, bh, dims))


def _interleave(gens):
    for _ in itertools.zip_longest(*gens):
        pass


def _unit_lower_inverse_steps(lmat, eye, out):
    ident = jnp.where(eye, 1.0, 0.0).astype(F32)
    m = -lmat
    inv = ident + m
    for _ in range(int(math.log2(DN_CHUNK)) - 1):
        ms = _split(m)
        m = _x3dot(ms, ms)
        yield
        inv = inv + _x3dot(inv, m)
        yield
    out["tm"] = inv


def _dn_chunk_fwd(act, g, beta, pgate, gn, name, comm=None):
    T = act.shape[0]
    C, H = DN_CHUNK, DN_HEADS
    N = T // C

    def body(a_ref, g_ref, b_ref, pg_ref, gn_ref, o_ref, og_ref, s_out, t_out, vn_out, u_out, w_out, s_scr):
        n = pl.program_id(0)

        @pl.when(n == 0)
        def _():
            s_scr[...] = jnp.zeros_like(s_scr)

        head_lane = _iota2((C, H), 1)

        def head(hh):
            qs, vs = slice(hh * DN_DK, (hh + 1) * DN_DK), slice(hh * DN_DV, (hh + 1) * DN_DV)
            q, k, v = a_ref[:, qs], a_ref[:, DN_QK_W + hh * DN_DK:DN_QK_W + (hh + 1) * DN_DK], \
                a_ref[:, 2 * DN_QK_W + hh * DN_DV:2 * DN_QK_W + (hh + 1) * DN_DV]
            gc = jnp.sum(jnp.where(head_lane == hh, g_ref[...], 0.0), axis=1, keepdims=True)
            bc = jnp.sum(jnp.where(head_lane == hh, b_ref[...], 0.0), axis=1, keepdims=True)
            t = _dn_chunk_terms(q, k, gc, bc)
            yield
            res = {}
            yield from _unit_lower_inverse_steps(t["lmat"], t["eye"], res)
            tms = _split(res["tm"])
            u = _x3dot(tms, v * bc)
            yield
            w = _x3dot(tms, t["kb"] * t["eg"])
            yield
            s = s_scr[hh]
            s_out[hh] = s
            t_out[hh] = res["tm"]
            sb = s.astype(BF16)
            vn = u - _dot(w.astype(BF16), sb)
            yield
            o = _dot(t["qd"].astype(BF16), sb) + _bdot(t["aqk"], vn)
            yield
            s_scr[hh] = s * jnp.exp(t["gl"]) + _bdot(t["kd"], vn, TN)
            vn_out[:, vs] = vn
            u_out[:, vs] = u
            w_out[:, qs] = w
            o_ref[:, vs] = o
            gate = pg_ref[:, vs]
            r = lax.rsqrt(jnp.mean(o * o, axis=-1, keepdims=True) + RMS_EPS)
            og_ref[:, vs] = (o * r * gn_ref[...] * (gate * _sigmoid(gate))).astype(BF16)

        _interleave([head(hh) for hh in range(H)])

    row = lambda w: pl.BlockSpec((C, w), lambda n: (n, 0))
    return _call(
        body, comm, name=name, grid=(N,),
        in_specs=[row(DN_CONV_W), row(H), row(H), row(DN_V_W), pl.BlockSpec((1, DN_DV), lambda n: (0, 0))],
        out_specs=[row(DN_V_W), row(DN_V_W),
                   pl.BlockSpec((H, None, DN_DK, DN_DV), lambda n: (0, n, 0, 0)),
                   pl.BlockSpec((H, None, C, C), lambda n: (0, n, 0, 0)),
                   row(DN_V_W), row(DN_V_W), row(DN_QK_W)],
        out_shape=[jax.ShapeDtypeStruct((T, DN_V_W), F32), jax.ShapeDtypeStruct((T, DN_V_W), BF16),
                   jax.ShapeDtypeStruct((H, N, DN_DK, DN_DV), F32),
                   jax.ShapeDtypeStruct((H, N, C, C), F32),
                   jax.ShapeDtypeStruct((T, DN_V_W), F32),
                   jax.ShapeDtypeStruct((T, DN_V_W), F32),
                   jax.ShapeDtypeStruct((T, DN_QK_W), F32)],
        scratch_shapes=[pltpu.VMEM((H, DN_DK, DN_DV), F32)], semantics=("arbitrary",), args=(act, g, beta, pgate, gn))


def _dn_chunk_bwd(act, g, beta, s_saved, tm_saved, vn_saved, u_saved, w_saved, dog, o_raw, pgate, gn, name, comm=None):
    T = act.shape[0]
    C, H = DN_CHUNK, DN_HEADS
    N = T // C

    def body(a_ref, g_ref, b_ref, s_ref, t_ref, vn_ref, u_ref, w_ref, dog_ref, o_ref, pg_ref, gn_ref,
             da_ref, dg_ref, db_ref, dgate_ref, dgn_ref, ds_scr):
        @pl.when(pl.program_id(0) == 0)
        def _():
            ds_scr[...] = jnp.zeros_like(ds_scr)

        head_lane = _iota2((C, H), 1)
        dg_cols, db_cols, dgn_parts = {}, {}, {}

        def output_gate_bwd(hh, vs):
            d, o, gate, gn_v = dog_ref[:, vs], o_ref[:, vs], pg_ref[:, vs], gn_ref[...]
            sg, dsg = _silu_and_grad(gate)
            r = lax.rsqrt(jnp.mean(o * o, axis=-1, keepdims=True) + RMS_EPS)
            n = o * r
            dy = d * sg
            dgate_ref[:, vs] = (d * (n * gn_v) * dsg).astype(BF16)
            dn = dy * gn_v
            dgn_parts[hh] = jnp.sum(dy * n, axis=0, keepdims=True)
            return r * (dn - n * jnp.mean(dn * n, axis=-1, keepdims=True))

        def head(hh):
            qs, vs = slice(hh * DN_DK, (hh + 1) * DN_DK), slice(hh * DN_DV, (hh + 1) * DN_DV)
            ks = slice(DN_QK_W + hh * DN_DK, DN_QK_W + (hh + 1) * DN_DK)
            vas = slice(2 * DN_QK_W + hh * DN_DV, 2 * DN_QK_W + (hh + 1) * DN_DV)
            q, k, v = a_ref[:, qs], a_ref[:, ks], a_ref[:, vas]
            gc = jnp.sum(jnp.where(head_lane == hh, g_ref[...], 0.0), axis=1, keepdims=True)
            bc = jnp.sum(jnp.where(head_lane == hh, b_ref[...], 0.0), axis=1, keepdims=True)
            t = _dn_chunk_terms(q, k, gc, bc)
            yield
            lower, strict, eye = t["lower"], t["strict"], t["eye"]
            decay, eg, egl, kb, qd, kd = t["decay"], t["eg"], t["egl"], t["kb"], t["qd"], t["kd"]
            s, tm, vn, u, w = s_ref[hh], t_ref[hh], vn_ref[:, vs], u_ref[:, vs], w_ref[:, qs]
            d_o = output_gate_bwd(hh, vs)
            ds_next = ds_scr[hh]
            egl_tot = jnp.exp(t["gl"])
            dob, sb, dsb, vnb = d_o.astype(BF16), s.astype(BF16), ds_next.astype(BF16), vn.astype(BF16)

            dvn = _bdot(t["aqk"], dob, TN) + _bdot(kd, dsb)
            yield
            daqk = jnp.where(lower, _dot(dob, vnb, NT), 0.0)
            dqd = _dot(dob, sb, NT)
            dkd = _dot(vnb, dsb, NT)
            yield
            dvnb = dvn.astype(BF16)
            ds_scr[hh] = _bdot(qd, dob, TN) + egl_tot * ds_next - _bdot(w, dvnb, TN)
            dgl = egl_tot * jnp.sum(jnp.sum(s * ds_next, axis=1, keepdims=True), axis=0, keepdims=True)
            dw = -_dot(dvnb, sb, NT)
            yield
            tms = _split(tm)
            dru = _x3dot(tms, dvn, TN)
            drw = _x3dot(tms, dw, TN)
            yield
            dl = -jnp.where(strict, _x3dot(dru, u, NT) + _x3dot(drw, w, NT), 0.0)
            yield
            dkk = (dl * decay).astype(BF16)
            dqk = (daqk * decay).astype(BF16)
            dkb = _bdot(dkk, k) + drw * eg
            yield
            da_ref[:, ks] = _bdot(dkk, kb, TN) + _bdot(dqk, q, TN) + dkd * egl + dkb * bc
            da_ref[:, qs] = _bdot(dqk, k) + dqd * eg
            da_ref[:, vas] = dru * bc
            yield
            db_cols[hh] = jnp.sum(dru * v, axis=1, keepdims=True) + jnp.sum(dkb * k, axis=1, keepdims=True)
            pm = dl * t["lmat"] + daqk * t["aqk"]
            col_as_col = jnp.sum(jnp.where(eye, jnp.sum(pm, axis=0, keepdims=True), 0.0), axis=1, keepdims=True)
            kdsum = jnp.sum(dkd * kd, axis=1, keepdims=True)
            dgc = (jnp.sum(pm, axis=1, keepdims=True) - col_as_col + jnp.sum(dqd * qd, axis=1, keepdims=True)
                   - kdsum + jnp.sum(drw * (kb * eg), axis=1, keepdims=True))
            dgl = dgl + jnp.sum(kdsum, axis=0, keepdims=True)
            dg_cols[hh] = dgc + jnp.where(t["last"], dgl, 0.0)

        _interleave([head(hh) for hh in range(H)])
        dg_ref[...] = sum(jnp.where(head_lane == hh, dg_cols[hh], 0.0) for hh in range(H))
        db_ref[...] = sum(jnp.where(head_lane == hh, db_cols[hh], 0.0) for hh in range(H))
        dgn_part = sum(dgn_parts[hh] for hh in range(H))

        @pl.when(pl.program_id(0) == 0)
        def _():
            dgn_ref[...] = dgn_part

        @pl.when(pl.program_id(0) > 0)
        def _():
            dgn_ref[...] += dgn_part

    row = lambda w: pl.BlockSpec((C, w), lambda n: (N - 1 - n, 0))
    vec = pl.BlockSpec((1, DN_DV), lambda n: (0, 0))
    return _call(
        body, comm, name=name, grid=(N,),
        in_specs=[row(DN_CONV_W), row(H), row(H),
                  pl.BlockSpec((H, None, DN_DK, DN_DV), lambda n: (0, N - 1 - n, 0, 0)),
                  pl.BlockSpec((H, None, C, C), lambda n: (0, N - 1 - n, 0, 0)),
                  row(DN_V_W), row(DN_V_W), row(DN_QK_W), row(DN_V_W), row(DN_V_W), row(DN_V_W), vec],
        out_specs=[row(DN_CONV_W), row(H), row(H), row(DN_V_W), vec],
        out_shape=[jax.ShapeDtypeStruct((T, DN_CONV_W), F32),
                   jax.ShapeDtypeStruct((T, H), F32), jax.ShapeDtypeStruct((T, H), F32),
                   jax.ShapeDtypeStruct((T, DN_V_W), BF16), jax.ShapeDtypeStruct((1, DN_DV), F32)],
        scratch_shapes=[pltpu.VMEM((H, DN_DK, DN_DV), F32)], semantics=("arbitrary",),
        args=(act, g, beta, s_saved, tm_saved, vn_saved, u_saved, w_saved, dog, o_raw, pgate, gn))


def _dn_split_w_in(w):
    wab = jnp.pad(w[:, DN_CONV_W + DN_V_W:], ((0, 0), (0, DN_AB_PAD - 2 * DN_HEADS)))
    return w[:, :DN_CONV_W], w[:, DN_CONV_W:DN_CONV_W + DN_V_W], wab


def _dn_layer_fwd(h, wts, conv_w, a_log, dt_bias, gn, w_out, x_res, tag, comm=None):
    wqkv, wgate, wab = wts
    H = DN_HEADS
    pqkv = _matmul(h, wqkv, "nn", tag + "_pqkv")
    pgate = _matmul(h, wgate, "nn", tag + "_pgate")
    pab = _matmul(h, wab, "nn", tag + "_pab")
    a_in, b_in = pab[:, :H], pab[:, H:2 * H]
    g, beta = _dn_gates(a_in, b_in, a_log, dt_bias, tag + "_gates")
    act = _dn_prep(pqkv, conv_w, tag + "_prep")
    (o_raw, og, s_sv, tm_sv, vn_sv, u_sv, w_sv), landed = _dn_chunk_fwd(act, g, beta, pgate, gn, tag + "_chunk_fwd", comm)
    if callable(w_out):
        w_out = w_out(landed)
    y = _matmul(og, w_out, "nn", tag + "_out", add=x_res)
    saved = dict(h=h, wts=wts, conv_w=conv_w, a_log=a_log, dt_bias=dt_bias, gn=gn, w_out=w_out, pqkv=pqkv, pgate=pgate,
                 a_in=a_in, b_in=b_in, g=g, beta=beta, act=act, o_raw=o_raw, chunk=(s_sv, tm_sv, vn_sv, u_sv, w_sv), og=og)
    return y, saved, landed


def _dn_layer_bwd(dout, sv, tag, comm_of=None, late_comm_of=None):
    wqkv, wgate, wab = sv["wts"]
    h = sv["h"]
    dog = _matmul(dout, sv["w_out"], "nt", tag + "_dog")
    dw_out = _matmul(sv["og"], dout, "tn", tag + "_dwout", out_dtype=BF16)
    comm = comm_of(dw_out) if comm_of is not None else None
    (dact, dg, dbeta, dgate, dgn), landed = _dn_chunk_bwd(sv["act"], sv["g"], sv["beta"], *sv["chunk"], dog, sv["o_raw"],
                                                          sv["pgate"], sv["gn"], tag + "_chunk_bwd", comm)
    da_in, db_in, da_log, ddt = _dn_gates_bwd(dg, dbeta, sv["a_in"], sv["b_in"], sv["a_log"], sv["dt_bias"],
                                              tag + "_gates_bwd")
    dpqkv, dconv = _dn_prep_bwd(sv["pqkv"], sv["conv_w"], dact, tag + "_prep_bwd")
    dpab = jnp.pad(jnp.concatenate([da_in, db_in], axis=1), ((0, 0), (0, DN_AB_PAD - 2 * DN_HEADS)))
    dwqkv = _matmul(h, dpqkv, "tn", tag + "_dwqkv", out_dtype=BF16)
    dwgate = _matmul(h, dgate, "tn", tag + "_dwgate", out_dtype=BF16)
    dwab = _matmul(h, dpab, "tn", tag + "_dwab", out_dtype=BF16)
    dw_in = jnp.concatenate([dwqkv, dwgate, dwab[:, :2 * DN_HEADS]], axis=1)
    grads = dict(dn_w_in=dw_in, dn_conv_w=dconv, dn_a_log=da_log, dn_dt_bias=ddt, dn_o_norm_g=dgn, dn_w_out=dw_out)
    dh, landed_late = _matmul_nt_sum([(dpqkv, wqkv), (dgate, wgate), (dpab, wab)], tag + "_dh",
                                     late_comm_of(grads) if late_comm_of is not None else None)
    return dh, grads, landed, landed_late


def _sb_layer_fwd(h, w_in, qg, kg, w_out, x_res, tag, comm=None):
    qg2, kg2 = jnp.tile(qg, (1, 2)), jnp.tile(kg, (1, 2))
    proj = _matmul(h, w_in, "nn", tag + "_proj")
    qn, kn, vb = _sb_prep(proj, qg2, kg2, tag + "_prep")
    (o, og, ltot, done), landed = _sb_attn_fwd(qn, kn, vb, proj, tag + "_attn_fwd", comm)
    y = _matmul(og, w_out, "nn", tag + "_out", add=x_res)
    saved = dict(h=h, w_in=w_in, qg2=qg2, kg2=kg2, w_out=w_out, proj=proj, qn=qn, kn=kn, vb=vb, o=o, og=og, ltot=ltot,
                 done=done)
    return y, saved, landed


def _sb_layer_bwd(dout, sv, tag, comm=None):
    dog = _matmul(dout, sv["w_out"], "nt", tag + "_dog")
    dw_out = _matmul(sv["og"], dout, "tn", tag + "_dwout", out_dtype=BF16)
    (dqn, dkn, dv, dgate), landed = _sb_attn_bwd(sv["qn"], sv["kn"], sv["vb"], dog, sv["o"], sv["ltot"], sv["done"],
                                                 sv["proj"], tag + "_attn_bwd", comm)
    dproj, dqgp, dkgp = _sb_prep_bwd(sv["proj"], dqn, dkn, dv, dgate, sv["qg2"], sv["kg2"], tag + "_prep_bwd")
    dw_in = _matmul(sv["h"], dproj, "tn", tag + "_dwin", out_dtype=BF16)
    dh = _matmul(dproj, sv["w_in"], "nt", tag + "_dh")
    dqg = _fold_heads(dqgp, tag + "_dqg")
    dkg = _fold_heads(dkgp, tag + "_dkg")
    return dh, dict(sb_w_in=dw_in, sb_q_norm_g=dqg, sb_k_norm_g=dkg, sb_w_out=dw_out), landed


def _sc_layer_fwd(h, w_in, conv_w, w_out, x_res, tag):
    proj = _matmul(h, w_in, "nn", tag + "_proj")
    yg = _sc_fwd(proj, conv_w, tag + "_fwd")
    y = _matmul(yg, w_out, "nn", tag + "_out", add=x_res)
    return y, dict(h=h, w_in=w_in, conv_w=conv_w, w_out=w_out, proj=proj, yg=yg)


def _sc_layer_bwd(dout, sv, tag):
    dyg = _matmul(dout, sv["w_out"], "nt", tag + "_dyg")
    dw_out = _matmul(sv["yg"], dout, "tn", tag + "_dwout", out_dtype=BF16)
    dproj, dconv = _sc_bwd(dyg, sv["proj"], sv["conv_w"], tag + "_bwd")
    dw_in = _matmul(sv["h"], dproj, "tn", tag + "_dwin", out_dtype=BF16)
    dh = _matmul(dproj, sv["w_in"], "nt", tag + "_dh")
    return dh, dict(sc_w_in=dw_in, sc_conv_w=dconv, sc_w_out=dw_out)


def _adamw(w, m, v, parts, name):
    R, C = w.shape
    tr = _tile(R, 128, SUBLANE)

    def body(w_ref, m_ref, v_ref, p_ref, g_ref, d_ref, nm_ref, nv_ref):
        g = p_ref[0].astype(F32)
        for s in range(1, N_DEV):
            g = g + p_ref[s].astype(F32)
        m2 = ADAM_B1 * m_ref[...] + (1.0 - ADAM_B1) * g
        v2 = ADAM_B2 * v_ref[...] + (1.0 - ADAM_B2) * (g * g)
        m_hat = m2 / (1.0 - ADAM_B1 ** ADAM_STEP)
        v_hat = v2 / (1.0 - ADAM_B2 ** ADAM_STEP)
        g_ref[...] = g
        d_ref[...] = -ADAM_LR * (m_hat / (jnp.sqrt(v_hat) + ADAM_EPS) + ADAM_WD * w_ref[...])
        nm_ref[...] = m2
        nv_ref[...] = v2

    blk = pl.BlockSpec((tr, C), lambda i: (i, 0))
    return pl.pallas_call(
        body, name=name, grid=(R // tr,),
        in_specs=[blk, blk, blk, pl.BlockSpec((N_DEV, tr, C), lambda i: (0, i, 0))],
        out_specs=[blk] * 4, out_shape=[jax.ShapeDtypeStruct((R, C), F32)] * 4,
        compiler_params=_params("parallel"),
    )(w, m, v, parts)


_HBM = pl.BlockSpec(memory_space=pltpu.HBM)
_MESH = pl.DeviceIdType.MESH


def _slot(x, y, c):
    return 4 * x + 2 * y + c


class _Gather:
    def __init__(self, shards):
        self.arrays = list(shards)
        n = len(self.arrays)
        self.out_shapes = [jax.ShapeDtypeStruct((N_DEV,) + s.shape, s.dtype) for s in self.arrays]
        self.scratch = [pltpu.SemaphoreType.DMA((n, N_DEV - 1)), pltpu.SemaphoreType.DMA((n, N_DEV - 1)),
                        pltpu.SemaphoreType.DMA((n,))]

    def _parts(self, ins, outs, sems):
        send_sems, recv_sems, local_sems = sems
        n = len(self.arrays)
        x, y, c = lax.axis_index("x"), lax.axis_index("y"), lax.axis_index("c")
        me, sibling = (x, y, c), (x, y, 1 - c)
        chips = [(1 - x, y), (x, 1 - y), (1 - x, 1 - y)]

        def copy(a, k, block, to, src=None):
            dst = outs[a].at[_slot(*block)]
            return pltpu.make_async_remote_copy(src_ref=dst if src is None else src, dst_ref=dst,
                                                send_sem=send_sems.at[a, k], recv_sem=recv_sems.at[a, k],
                                                device_id=to, device_id_type=_MESH)

        mine = [pltpu.make_async_copy(ins[a], outs[a].at[_slot(*me)], local_sems.at[a]) for a in range(n)]
        first = []
        for a in range(n):
            first.append(copy(a, 0, me, sibling, src=ins[a]))
            first += [copy(a, 1 + j, me, (*chip, c), src=ins[a]) for j, chip in enumerate(chips)]
        return n, c, me, sibling, chips, copy, mine, first

    def start(self, ins, outs, sems):
        _, _, _, _, _, _, mine, first = self._parts(ins, outs, sems)
        for cp in mine + first:
            cp.start()

    def finish(self, ins, outs, sems):
        n, c, me, sibling, chips, copy, mine, first = self._parts(ins, outs, sems)
        passed = []
        for j, chip in enumerate(chips):
            for a in range(n):
                copy(a, 1 + j, (*chip, c), me).wait_recv()
                fwd = copy(a, 4 + j, (*chip, c), sibling)
                fwd.start()
                passed.append(fwd)
        for a in range(n):
            copy(a, 0, sibling, me).wait_recv()
            for j, chip in enumerate(chips):
                copy(a, 4 + j, (*chip, 1 - c), me).wait_recv()
        for cp in first + passed:
            cp.wait_send()
        for cp in mine:
            cp.wait()


class _Exchange:
    def __init__(self, arrays, scatter):
        self.arrays, self.scatter = list(arrays), list(scatter)
        n = len(self.arrays)
        shapes = [a.shape[1:] if s else a.shape for a, s in zip(self.arrays, self.scatter)]
        self.out_shapes = [jax.ShapeDtypeStruct((N_DEV,) + tuple(s), a.dtype) for s, a in zip(shapes, self.arrays)]
        self.scratch = [pltpu.SemaphoreType.DMA((n, N_DEV - 1)), pltpu.SemaphoreType.DMA((n, N_DEV - 1)),
                        pltpu.SemaphoreType.DMA((n,))]

    def _copies(self, ins, outs, sems):
        send_sems, recv_sems, local_sems = sems
        n, scatter = len(self.arrays), self.scatter
        x, y, c = lax.axis_index("x"), lax.axis_index("y"), lax.axis_index("c")
        me = _slot(x, y, c)
        copies = [pltpu.make_async_copy(ins[a].at[me] if scatter[a] else ins[a], outs[a].at[me], local_sems.at[a])
                  for a in range(n)]
        for r in range(1, N_DEV):
            px = 1 - x if r & 4 else x
            py = 1 - y if r & 2 else y
            pc = 1 - c if r & 1 else c
            for a in range(n):
                copies.append(pltpu.make_async_remote_copy(
                    src_ref=ins[a].at[_slot(px, py, pc)] if scatter[a] else ins[a], dst_ref=outs[a].at[me],
                    send_sem=send_sems.at[a, r - 1], recv_sem=recv_sems.at[a, r - 1],
                    device_id=(px, py, pc), device_id_type=_MESH))
        return copies

    def start(self, ins, outs, sems):
        for cp in self._copies(ins, outs, sems):
            cp.start()

    def finish(self, ins, outs, sems):
        for cp in self._copies(ins, outs, sems):
            cp.wait()


def _comm_call(comm, name):
    n = len(comm.arrays)

    def body(*refs):
        ins, outs, sems = refs[:n], refs[n:2 * n], refs[2 * n:]
        comm.start(ins, outs, sems)
        comm.finish(ins, outs, sems)

    return pl.pallas_call(body, name=name, in_specs=[_HBM] * n, out_specs=[_HBM] * n, out_shape=comm.out_shapes,
                          scratch_shapes=comm.scratch)(*comm.arrays)


def _call(body, comm, *, name, grid, in_specs, out_specs, out_shape, scratch_shapes, semantics, args):
    if comm is None:
        outs = pl.pallas_call(body, name=name, grid=grid, in_specs=in_specs, out_specs=out_specs, out_shape=out_shape,
                              scratch_shapes=scratch_shapes, compiler_params=_params(*semantics))(*args)
        return outs, []
    n_in, n_out, n_scr, n_c = len(in_specs), len(out_specs), len(scratch_shapes), len(comm.arrays)

    def fused(*refs):
        ins, refs = refs[:n_in], refs[n_in:]
        c_ins, refs = refs[:n_c], refs[n_c:]
        outs, refs = refs[:n_out], refs[n_out:]
        c_outs, refs = refs[:n_c], refs[n_c:]
        scr, sems = refs[:n_scr], refs[n_scr:]
        ids = [pl.program_id(d) for d in range(len(grid))]
        first = functools.reduce(jnp.logical_and, [i == 0 for i in ids])
        last = functools.reduce(jnp.logical_and, [i == g - 1 for i, g in zip(ids, grid)])

        @pl.when(first)
        def _():
            comm.start(c_ins, c_outs, sems)

        body(*ins, *outs, *scr)

        @pl.when(last)
        def _():
            comm.finish(c_ins, c_outs, sems)

    outs = pl.pallas_call(
        fused, name=name, grid=grid, in_specs=list(in_specs) + [_HBM] * n_c, out_specs=list(out_specs) + [_HBM] * n_c,
        out_shape=list(out_shape) + comm.out_shapes, scratch_shapes=list(scratch_shapes) + comm.scratch,
        compiler_params=_params(*["arbitrary"] * len(grid)))(*args, *comm.arrays)
    return outs[:n_out], outs[n_out:]


_GATHER_0 = (("dn_w_in", 0), ("dn_conv_w", 0), ("dn_o_norm_g", 0))
_GATHER_1 = (("dn_w_out", 0), ("sb_w_in", 0), ("sb_w_out", 0))
_GATHER_2 = (("sc_w_in", 0), ("sc_conv_w", 0), ("sc_w_out", 0), ("dn_w_in", 1), ("dn_conv_w", 1), ("dn_o_norm_g", 1),
             ("dn_w_out", 1))
_EXCHANGE_A = _GATHER_2
_EXCHANGE_B = (("sb_w_in", 0), ("sb_w_out", 0), ("dn_w_out", 0))
_EXCHANGE_C = _GATHER_0
_MATMUL_WEIGHTS = ("dn_w_in", "dn_w_out", "sb_w_in", "sb_w_out", "sc_w_in", "sc_w_out")
_COLUMN_SHARDED = ("dn_w_in", "dn_conv_w", "dn_o_norm_g", "sb_w_in", "sc_w_in", "sc_conv_w")
_REPLICATED = ("norm_g", "dn_a_log", "dn_dt_bias", "sb_q_norm_g", "sb_k_norm_g")
_ORDER = ("norm_g", "dn_w_in", "dn_conv_w", "dn_a_log", "dn_dt_bias", "dn_o_norm_g", "dn_w_out", "sb_w_in", "sb_q_norm_g",
          "sb_k_norm_g", "sb_w_out", "sc_w_in", "sc_conv_w", "sc_w_out")
_PACK_COLS = D_MODEL


def _as_2d(a):
    return a.reshape(1, -1) if a.ndim == 1 else a


def _assemble(name, gathered):
    n, r, c = gathered.shape
    if name in _COLUMN_SHARDED:
        return jnp.moveaxis(gathered, 0, 1).reshape(r, n * c)
    return gathered.reshape(n * r, c)


def _disassemble(name, full):
    r, c = full.shape
    if name in _COLUMN_SHARDED:
        return jnp.moveaxis(full.reshape(r, N_DEV, c // N_DEV), 1, 0)
    return full.reshape(N_DEV, r // N_DEV, c)


def _pack_replicated(d):
    rows = [d["norm_g"]]
    for name in _REPLICATED[1:]:
        flat = d[name].reshape(1, -1)
        rows.append(jnp.pad(flat, ((0, 0), (0, _PACK_COLS - flat.shape[1]))))
    return jnp.concatenate(rows, axis=0)


def _unpack_replicated(p, like):
    out = {"norm_g": p[:4]}
    for r, name in enumerate(_REPLICATED[1:]):
        shape = like[name].shape
        out[name] = p[4 + r, :math.prod(shape)].reshape(shape)
    return out


def kernel(x, norm_g, dn_w_in, dn_conv_w, dn_a_log, dn_dt_bias, dn_o_norm_g, dn_w_out, sb_w_in, sb_q_norm_g, sb_k_norm_g, sb_w_out, sc_w_in, sc_conv_w, sc_w_out, loss_target, m_norm_g, m_dn_w_in, m_dn_conv_w, m_dn_a_log, m_dn_dt_bias, m_dn_o_norm_g, m_dn_w_out, m_sb_w_in, m_sb_q_norm_g, m_sb_k_norm_g, m_sb_w_out, m_sc_w_in, m_sc_conv_w, m_sc_w_out, v_norm_g, v_dn_w_in, v_dn_conv_w, v_dn_a_log, v_dn_dt_bias, v_dn_o_norm_g, v_dn_w_out, v_sb_w_in, v_sb_q_norm_g, v_sb_k_norm_g, v_sb_w_out, v_sc_w_in, v_sc_conv_w, v_sc_w_out):
    w = dict(norm_g=norm_g, dn_w_in=dn_w_in, dn_conv_w=dn_conv_w, dn_a_log=dn_a_log, dn_dt_bias=dn_dt_bias,
             dn_o_norm_g=dn_o_norm_g, dn_w_out=dn_w_out, sb_w_in=sb_w_in, sb_q_norm_g=sb_q_norm_g, sb_k_norm_g=sb_k_norm_g,
             sb_w_out=sb_w_out, sc_w_in=sc_w_in, sc_conv_w=sc_conv_w, sc_w_out=sc_w_out)
    m = dict(norm_g=m_norm_g, dn_w_in=m_dn_w_in, dn_conv_w=m_dn_conv_w, dn_a_log=m_dn_a_log, dn_dt_bias=m_dn_dt_bias,
             dn_o_norm_g=m_dn_o_norm_g, dn_w_out=m_dn_w_out, sb_w_in=m_sb_w_in, sb_q_norm_g=m_sb_q_norm_g,
             sb_k_norm_g=m_sb_k_norm_g, sb_w_out=m_sb_w_out, sc_w_in=m_sc_w_in, sc_conv_w=m_sc_conv_w, sc_w_out=m_sc_w_out)
    v = dict(norm_g=v_norm_g, dn_w_in=v_dn_w_in, dn_conv_w=v_dn_conv_w, dn_a_log=v_dn_a_log, dn_dt_bias=v_dn_dt_bias,
             dn_o_norm_g=v_dn_o_norm_g, dn_w_out=v_dn_w_out, sb_w_in=v_sb_w_in, sb_q_norm_g=v_sb_q_norm_g,
             sb_k_norm_g=v_sb_k_norm_g, sb_w_out=v_sb_w_out, sc_w_in=v_sc_w_in, sc_conv_w=v_sc_conv_w, sc_w_out=v_sc_w_out)

    def gather_of(keys):
        return _Gather([_as_2d(w[k][j]).astype(BF16) if k in _MATMUL_WEIGHTS else _as_2d(w[k][j]) for k, j in keys])

    def full_weights(keys, gathered):
        return {key: _assemble(key[0], g) for key, g in zip(keys, gathered)}

    def exchange_of(keys, grads, extra=()):
        out = [_disassemble(k, grads[k, j].astype(BF16) if k in _MATMUL_WEIGHTS else grads[k, j]) for k, j in keys]
        return _Exchange(out + list(extra), [True] * len(out) + [False] * len(extra))

    F = full_weights(_GATHER_0, _comm_call(gather_of(_GATHER_0), "gather_first"))
    xs, saves = [x[0]], []
    h = _rmsnorm_fwd(xs[0], norm_g[0:1], "norm0")

    def w_out_0(got):
        F.update(full_weights(_GATHER_1, got))
        return F["dn_w_out", 0]

    y, sv, _ = _dn_layer_fwd(h, _dn_split_w_in(F["dn_w_in", 0]), F["dn_conv_w", 0], dn_a_log[0:1], dn_dt_bias[0:1],
                             F["dn_o_norm_g", 0], w_out_0, xs[0], "dn0", gather_of(_GATHER_1))
    xs.append(y)
    saves.append(sv)
    h = _rmsnorm_fwd(xs[1], norm_g[1:2], "norm1")
    y, sv, got = _sb_layer_fwd(h, F["sb_w_in", 0], sb_q_norm_g, sb_k_norm_g, F["sb_w_out", 0], xs[1], "sb", gather_of(_GATHER_2))
    F.update(full_weights(_GATHER_2, got))
    xs.append(y)
    saves.append(sv)
    h = _rmsnorm_fwd(xs[2], norm_g[2:3], "norm2")
    y, sv = _sc_layer_fwd(h, F["sc_w_in", 0], F["sc_conv_w", 0], F["sc_w_out", 0], xs[2], "sc")
    xs.append(y)
    saves.append(sv)
    h = _rmsnorm_fwd(xs[3], norm_g[3:4], "norm3")
    y, sv, _ = _dn_layer_fwd(h, _dn_split_w_in(F["dn_w_in", 1]), F["dn_conv_w", 1], dn_a_log[1:2], dn_dt_bias[1:2],
                             F["dn_o_norm_g", 1], F["dn_w_out", 1], xs[3], "dn1")
    xs.append(y)
    saves.append(sv)
    dx, loss_part = _loss_head(xs[4], loss_target[0])

    G, dnorm, landed = {}, [None] * 4, {}

    def keep(grads, j):
        G.update({(k, j): g for k, g in grads.items()})

    dh, grads, _, _ = _dn_layer_bwd(dx, saves[3], "dn1")
    keep(grads, 1)
    dx, dnorm[3] = _rmsnorm_bwd(dh, xs[3], norm_g[3:4], dx, "norm3_bwd")
    dh, grads = _sc_layer_bwd(dx, saves[2], "sc")
    keep(grads, 0)
    dx, dnorm[2] = _rmsnorm_bwd(dh, xs[2], norm_g[2:3], dx, "norm2_bwd")
    dh, grads, got = _sb_layer_bwd(dx, saves[1], "sb", exchange_of(_EXCHANGE_A, G))
    keep(grads, 0)
    landed.update(zip(_EXCHANGE_A, got))
    dx, dnorm[1] = _rmsnorm_bwd(dh, xs[1], norm_g[1:2], dx, "norm1_bwd")

    def exchange_b(dw_out):
        G["dn_w_out", 0] = dw_out
        return exchange_of(_EXCHANGE_B, G)

    def exchange_c(grads):
        keep(grads, 0)
        return exchange_of(_EXCHANGE_C, G)

    dh, grads, got, got_late = _dn_layer_bwd(dx, saves[0], "dn0", exchange_b, exchange_c)
    landed.update(zip(_EXCHANGE_B, got))
    landed.update(zip(_EXCHANGE_C, got_late))
    dx, dnorm[0] = _rmsnorm_bwd(dh, xs[0], norm_g[0:1], dx, "norm0_bwd")
    replicated = dict(norm_g=jnp.concatenate(dnorm, axis=0),
                      dn_a_log=jnp.concatenate([G["dn_a_log", 0], G["dn_a_log", 1]], axis=0),
                      dn_dt_bias=jnp.concatenate([G["dn_dt_bias", 0], G["dn_dt_bias", 1]], axis=0),
                      sb_q_norm_g=G["sb_q_norm_g", 0], sb_k_norm_g=G["sb_k_norm_g", 0])
    got = _comm_call(_Exchange([_pack_replicated(replicated)], [False]), "exchange_replicated")

    res = {}
    for k in _ORDER:
        if k in _REPLICATED:
            continue
        per_layer = []
        for j in range(w[k].shape[0]):
            shape = w[k][j].shape
            outs = _adamw(_as_2d(w[k][j]), _as_2d(m[k][j]), _as_2d(v[k][j]), landed[k, j], f"adamw_{k}{j}")
            per_layer.append([o.reshape(shape) for o in outs])
        res[k] = [jnp.stack([layer[i] for layer in per_layer], axis=0) for i in range(4)]
    outs = _adamw(_pack_replicated(w), _pack_replicated(m), _pack_replicated(v), got[-1], "adamw_replicated")
    unpacked = [_unpack_replicated(o, w) for o in outs]
    for k in _REPLICATED:
        res[k] = [u[k] for u in unpacked]

    loss = lax.psum(loss_part[0, 0], ("x", "y", "c"))
    return (loss, dx[None]) + tuple(res[k][0] for k in _ORDER) + tuple(res[k][1] for k in _ORDER) \
        + tuple(res[k][2] for k in _ORDER) + tuple(res[k][3] for k in _ORDER)
```

```python
import functools
import itertools
import math

import jax
import jax.numpy as jnp
from jax import lax
from jax.experimental import pallas as pl
from jax.experimental.pallas import tpu as pltpu

F32 = jnp.float32
BF16 = jnp.bfloat16
HIGHEST = lax.Precision.HIGHEST

N_DEV = 8
D_MODEL = 1024
RMS_EPS = 1e-6
L2_EPS = 1e-6

DN_HEADS = 8
DN_DK = 128
DN_DV = 256
DN_QK_W = DN_HEADS * DN_DK
DN_V_W = DN_HEADS * DN_DV
DN_CONV = 4
DN_CHUNK = 64
DN_CONV_W = 2 * DN_QK_W + DN_V_W
DN_IN = DN_CONV_W + DN_V_W + 2 * DN_HEADS
DN_AB_PAD = 128
DN_PREP_BLK = 512

SB_HEADS = 16
SB_DH = 64
SB_W = SB_HEADS * SB_DH
SB_PAIRS = SB_HEADS // 2
SB_TQ = 256
SB_TK = 128
SB_DEAD = -106.0

SC_W = 2 * D_MODEL
SC_CONV = 3
SC_BLK = 512
SC_NBLK = SC_W // SC_BLK

ADAM_LR = 0.001
ADAM_B1 = 0.9
ADAM_B2 = 0.999
ADAM_EPS = 1e-08
ADAM_WD = 0.01
ADAM_STEP = 10

LANE = 128
SUBLANE = 8
HALO = SUBLANE
ROW_TILE = 256
WIDE_ROW_TILE = 128
CONV_ROW_TILE = 256
VMEM_LIMIT = 48 * 2 ** 20

NN = ((1,), (0,))
NT = ((1,), (1,))
TN = ((0,), (0,))


def _dot(a, b, dims=NN, precision=None):
    return lax.dot_general(a, b, (dims, ((), ())), precision=precision, preferred_element_type=F32)


def _bdot(a, b, dims=NN):
    return _dot(a.astype(BF16), b.astype(BF16), dims)


def _hdot(a, b, dims=NN):
    return _dot(a, b, dims, precision=HIGHEST)


def _tile(dim, pref, align=LANE):
    t = (min(pref, dim) // align) * align
    while t >= align:
        if dim % t == 0:
            return t
        t -= align
    return dim


def _params(*sem):
    return pltpu.CompilerParams(dimension_semantics=sem, vmem_limit_bytes=VMEM_LIMIT)


def _sigmoid(x):
    return 0.5 * jnp.tanh(0.5 * x) + 0.5


def _softplus(x):
    return jnp.maximum(x, 0.0) + jnp.log(1.0 + jnp.exp(-jnp.abs(x)))


def _silu_and_grad(x):
    s = _sigmoid(x)
    return x * s, s * (1.0 + x * (1.0 - s))


def _iota2(shape, dim):
    return lax.broadcasted_iota(jnp.int32, shape, dim)


def _matmul(a, b, mode, name, out_dtype=F32, add=None, b_cols=None, blocked_b=False, blocked_out=0,
            tm=1024, tn=1024, tk=1024):
    b_rows, b_width = (b.shape[1], b.shape[0] * b.shape[2]) if blocked_b else b.shape
    c0, b_used = b_cols if b_cols is not None else (0, b_width)
    if mode == "nn":
        (M, K), (K2, N) = a.shape, (b_rows, b_used)
    elif mode == "nt":
        (M, K), (N, K2) = a.shape, (b_rows, b_used)
    else:
        (K, M), (K2, N) = a.shape, (b_rows, b_used)
    assert K == K2, (a.shape, b.shape, mode)
    tm, tn, tk = _tile(M, tm), _tile(N, tn), _tile(K, tk)
    if blocked_b and mode == "nt":
        tk = b.shape[2]
    elif blocked_b:
        tn = b.shape[2]
    if blocked_out:
        tn = N // blocked_out
    nk = K // tk
    dims = {"nn": NN, "nt": NT, "tn": TN}[mode]
    a_spec = pl.BlockSpec((tk, tm), lambda i, j, k: (k, i)) if mode == "tn" else pl.BlockSpec((tm, tk), lambda i, j, k: (i, k))
    if mode == "nt":
        cb0 = c0 // tk
        assert c0 % tk == 0
        b_spec = (pl.BlockSpec((None, tn, tk), lambda i, j, k: (k + cb0, j, 0)) if blocked_b
                  else pl.BlockSpec((tn, tk), lambda i, j, k: (j, k + cb0)))
    else:
        cb0 = c0 // tn
        assert c0 % tn == 0
        b_spec = (pl.BlockSpec((None, tk, tn), lambda i, j, k: (j + cb0, k, 0)) if blocked_b
                  else pl.BlockSpec((tk, tn), lambda i, j, k: (k, j + cb0)))
    o_spec = pl.BlockSpec((tm, tn), lambda i, j, k: (i, j))
    out_spec = pl.BlockSpec((None, tm, tn), lambda i, j, k: (j, i, 0)) if blocked_out else o_spec
    out_shape = (blocked_out, M, tn) if blocked_out else (M, N)
    has_add = add is not None

    def body(*refs):
        a_ref, b_ref = refs[0], refs[1]
        add_ref = refs[2] if has_add else None
        o_ref = refs[3] if has_add else refs[2]
        p = _bdot(a_ref[...], b_ref[...], dims)

        def finish(acc):
            if has_add:
                acc = acc + add_ref[...]
            o_ref[...] = acc.astype(out_dtype)

        if nk == 1:
            finish(p)
        else:
            acc_ref = refs[-1]
            k = pl.program_id(2)

            @pl.when(k == 0)
            def _():
                acc_ref[...] = p

            @pl.when(k > 0)
            def _():
                acc_ref[...] += p

            @pl.when(k == nk - 1)
            def _():
                finish(acc_ref[...])

    in_specs = [a_spec, b_spec] + ([o_spec] if has_add else [])
    args = (a, b) + ((add,) if has_add else ())
    return pl.pallas_call(
        body, name=name, grid=(M // tm, N // tn, nk),
        in_specs=in_specs, out_specs=out_spec,
        out_shape=jax.ShapeDtypeStruct(out_shape, out_dtype),
        scratch_shapes=[pltpu.VMEM((tm, tn), F32)] if nk > 1 else [],
        compiler_params=_params("parallel", "parallel", "arbitrary"),
    )(*args)


def _matmul_nt_sum(pairs, name, comm=None, tm=1024, tk=1024):
    M, N = pairs[0][0].shape[0], pairs[0][1].shape[0]
    tm = _tile(M, tm)
    tks = [_tile(a.shape[1], tk) for a, _, _ in pairs]
    steps = [a.shape[1] // t for (a, _, _), t in zip(pairs, tks)]
    offs = [sum(steps[:p]) for p in range(len(pairs))]
    total = sum(steps)

    def body(*refs):
        a_refs, b_refs = refs[0:2 * len(pairs):2], refs[1:2 * len(pairs):2]
        o_ref, acc_ref = refs[2 * len(pairs)], refs[2 * len(pairs) + 1]
        k = pl.program_id(1)
        for p in range(len(pairs)):
            @pl.when((k >= offs[p]) & (k < offs[p] + steps[p]))
            def _(p=p):
                prod = _bdot(a_refs[p][...], b_refs[p][...], NT)
                if p == 0:
                    @pl.when(k == 0)
                    def _():
                        acc_ref[...] = prod

                    @pl.when(k > 0)
                    def _():
                        acc_ref[...] += prod
                else:
                    acc_ref[...] += prod

        @pl.when(k == total - 1)
        def _():
            o_ref[...] = acc_ref[...]

    in_specs, args = [], []
    for (a, b, c0), t, off, n in zip(pairs, tks, offs, steps):
        assert c0 % t == 0
        pick = lambda k, off=off, n=n: jnp.clip(k - off, 0, n - 1)
        in_specs += [pl.BlockSpec((tm, t), lambda i, k, pick=pick: (i, pick(k))),
                     pl.BlockSpec((N, t), lambda i, k, pick=pick, cb0=c0 // t: (0, pick(k) + cb0))]
        args += [a, b]
    outs, landed = _call(body, comm, name=name, grid=(M // tm, total), in_specs=in_specs,
                         out_specs=[pl.BlockSpec((tm, N), lambda i, k: (i, 0))],
                         out_shape=[jax.ShapeDtypeStruct((M, N), F32)], scratch_shapes=[pltpu.VMEM((tm, N), F32)],
                         semantics=("parallel", "arbitrary"), args=tuple(args))
    return outs[0], landed


def _rmsnorm_fwd(x, g, name):
    T, D = x.shape
    tt = _tile(T, 512, SUBLANE)

    def body(x_ref, g_ref, o_ref):
        xv = x_ref[...]
        r = lax.rsqrt(jnp.mean(xv * xv, axis=-1, keepdims=True) + RMS_EPS)
        o_ref[...] = (xv * r * g_ref[...]).astype(BF16)

    return pl.pallas_call(
        body, name=name, grid=(T // tt,),
        in_specs=[pl.BlockSpec((tt, D), lambda i: (i, 0)), pl.BlockSpec((1, D), lambda i: (0, 0))],
        out_specs=pl.BlockSpec((tt, D), lambda i: (i, 0)),
        out_shape=jax.ShapeDtypeStruct((T, D), BF16),
        compiler_params=_params("parallel"),
    )(x, g)


def _rmsnorm_bwd(dh, x, g, dx_res, name):
    T, D = x.shape
    tt = _tile(T, 256, SUBLANE)

    def body(dh_ref, x_ref, g_ref, res_ref, dx_ref, dg_ref):
        xv, dhv = x_ref[...], dh_ref[...]
        r = lax.rsqrt(jnp.mean(xv * xv, axis=-1, keepdims=True) + RMS_EPS)
        xh = xv * r
        dxh = dhv * g_ref[...]
        m = jnp.mean(dxh * xh, axis=-1, keepdims=True)
        dx_ref[...] = res_ref[...] + r * (dxh - xh * m)
        part = jnp.sum(dhv * xh, axis=0, keepdims=True)

        @pl.when(pl.program_id(0) == 0)
        def _():
            dg_ref[...] = part

        @pl.when(pl.program_id(0) > 0)
        def _():
            dg_ref[...] += part

    row = pl.BlockSpec((tt, D), lambda i: (i, 0))
    vec = pl.BlockSpec((1, D), lambda i: (0, 0))
    return pl.pallas_call(
        body, name=name, grid=(T // tt,),
        in_specs=[row, row, vec, row], out_specs=[row, vec],
        out_shape=[jax.ShapeDtypeStruct((T, D), F32), jax.ShapeDtypeStruct((1, D), F32)],
        compiler_params=_params("arbitrary"),
    )(dh, x, g, dx_res)


def _loss_head(y, target, name="loss_head"):
    T, D = y.shape
    tt = _tile(T, 512, SUBLANE)

    def body(y_ref, t_ref, dy_ref, l_ref):
        e = y_ref[...] - t_ref[...]
        dy_ref[...] = e * (1.0 / D)
        s = jnp.sum(jnp.sum(e * e, axis=1, keepdims=True), axis=0, keepdims=True) * (0.5 / D)
        s = jnp.broadcast_to(s, (1, LANE))

        @pl.when(pl.program_id(0) == 0)
        def _():
            l_ref[...] = s

        @pl.when(pl.program_id(0) > 0)
        def _():
            l_ref[...] += s

    row = pl.BlockSpec((tt, D), lambda i: (i, 0))
    return pl.pallas_call(
        body, name=name, grid=(T // tt,),
        in_specs=[row, row], out_specs=[row, pl.BlockSpec((1, LANE), lambda i: (0, 0))],
        out_shape=[jax.ShapeDtypeStruct((T, D), F32), jax.ShapeDtypeStruct((1, LANE), F32)],
        compiler_params=_params("arbitrary"),
    )(y, target)


def _down(x, k):
    return pltpu.roll(x, k, 0) if k else x


def _up(x, k):
    return pltpu.roll(x, x.shape[0] - k, 0) if k else x


def _sc_fwd(proj, conv_w, name):
    T = proj.shape[0]
    tt = _tile(T, WIDE_ROW_TILE, SUBLANE)
    B = SC_BLK

    def body(p_ref, ph_ref, w_ref, o_ref):
        keep = (pl.program_id(0) > 0).astype(F32)
        for j in range(SC_NBLK):
            cb, cc, cu, cg = (slice(k * SC_W + j * B, k * SC_W + (j + 1) * B) for k in range(4))
            cw = slice(j * B, (j + 1) * B)
            z = jnp.concatenate([ph_ref[:, cc] * ph_ref[:, cu] * keep, p_ref[:, cc] * p_ref[:, cu]], axis=0)
            cz = (w_ref[2:3, cw] * z + w_ref[1:2, cw] * _down(z, 1) + w_ref[0:1, cw] * _down(z, 2))[HALO:]
            gate = p_ref[:, cg]
            o_ref[:, cw] = (p_ref[:, cb] * cz * (gate * _sigmoid(gate))).astype(BF16)

    return pl.pallas_call(
        body, name=name, grid=(T // tt,),
        in_specs=[pl.BlockSpec((tt, 4 * SC_W), lambda i: (i, 0)),
                  pl.BlockSpec((HALO, 4 * SC_W), lambda i: (jnp.maximum(i * (tt // HALO) - 1, 0), 0)),
                  pl.BlockSpec((SC_CONV, SC_W), lambda i: (0, 0))],
        out_specs=pl.BlockSpec((tt, SC_W), lambda i: (i, 0)),
        out_shape=jax.ShapeDtypeStruct((T, SC_W), BF16),
        compiler_params=_params("parallel"),
    )(proj, proj, conv_w)


def _sc_bwd(dyg, proj, conv_w, name):
    T = proj.shape[0]
    tt = _tile(T, WIDE_ROW_TILE, SUBLANE)
    nt = T // tt
    B = SC_BLK
    hb = tt // HALO

    def body(d_ref, dn_ref, p_ref, pp_ref, pn_ref, w_ref, o_ref, dw_ref):
        i = pl.program_id(0)
        keep_p = (i > 0).astype(F32)
        keep_n = (i < nt - 1).astype(F32)
        main = slice(HALO, HALO + tt)
        parts = []
        for j in range(SC_NBLK):
            cw = slice(j * B, (j + 1) * B)

            def ext(k):
                s = slice(k * SC_W + j * B, k * SC_W + (j + 1) * B)
                return s, jnp.concatenate([pp_ref[:, s] * keep_p, p_ref[:, s], pn_ref[:, s]], axis=0)

            (sb, b), (sc, c), (su, u), (sg_, gate) = ext(0), ext(1), ext(2), ext(3)
            dyg_e = jnp.concatenate([jnp.zeros((HALO, B), F32), d_ref[:, cw], dn_ref[:, cw] * keep_n], axis=0)
            w0, w1, w2 = w_ref[0:1, cw], w_ref[1:2, cw], w_ref[2:3, cw]
            z = c * u
            z1, z2 = _down(z, 1), _down(z, 2)
            cz = w2 * z + w1 * z1 + w0 * z2
            sg, dsg = _silu_and_grad(gate)
            dy = dyg_e * sg
            dcz = dy * b
            dz = w2 * dcz + w1 * _up(dcz, 1) + w0 * _up(dcz, 2)
            o_ref[:, sb] = (dy * cz)[main].astype(BF16)
            o_ref[:, sc] = (dz * u)[main].astype(BF16)
            o_ref[:, su] = (dz * c)[main].astype(BF16)
            o_ref[:, sg_] = (dyg_e * (b * cz) * dsg)[main].astype(BF16)
            dcm = dcz[main]
            parts.append(jnp.concatenate([jnp.sum(dcm * z2[main], axis=0, keepdims=True),
                                          jnp.sum(dcm * z1[main], axis=0, keepdims=True),
                                          jnp.sum(dcm * z[main], axis=0, keepdims=True)], axis=0))
        part = jnp.concatenate(parts, axis=1)

        @pl.when(i == 0)
        def _():
            dw_ref[...] = part

        @pl.when(i > 0)
        def _():
            dw_ref[...] += part

    nxt = lambda i: (jnp.minimum((i + 1) * hb, nt * hb - 1), 0)
    return pl.pallas_call(
        body, name=name, grid=(nt,),
        in_specs=[pl.BlockSpec((tt, SC_W), lambda i: (i, 0)),
                  pl.BlockSpec((HALO, SC_W), nxt),
                  pl.BlockSpec((tt, 4 * SC_W), lambda i: (i, 0)),
                  pl.BlockSpec((HALO, 4 * SC_W), lambda i: (jnp.maximum(i * hb - 1, 0), 0)),
                  pl.BlockSpec((HALO, 4 * SC_W), nxt),
                  pl.BlockSpec((SC_CONV, SC_W), lambda i: (0, 0))],
        out_specs=[pl.BlockSpec((tt, 4 * SC_W), lambda i: (i, 0)), pl.BlockSpec((SC_CONV, SC_W), lambda i: (0, 0))],
        out_shape=[jax.ShapeDtypeStruct((T, 4 * SC_W), BF16), jax.ShapeDtypeStruct((SC_CONV, SC_W), F32)],
        compiler_params=_params("arbitrary"),
    )(dyg, dyg, proj, proj, proj, conv_w)


def _split3_dot(x, m):
    hi = x.astype(BF16)
    r1 = x - hi.astype(F32)
    mid = r1.astype(BF16)
    lo = (r1 - mid.astype(F32)).astype(BF16)
    return _dot(hi, m) + _dot(mid, m) + _dot(lo, m)


def _split2_dot(x, m):
    hi = x.astype(BF16)
    lo = (x - hi.astype(F32)).astype(BF16)
    return _dot(hi, m) + _dot(lo, m)


def _head_mean_matrix():
    r, c = _iota2((LANE, LANE), 0), _iota2((LANE, LANE), 1)
    return jnp.where((r // SB_DH) == (c // SB_DH), 1.0 / SB_DH, 0.0).astype(BF16)


def _sb_prep(proj, qg2, kg2, name):
    T = proj.shape[0]
    tt = _tile(T, WIDE_ROW_TILE, SUBLANE)

    def body(p_ref, qg_ref, kg_ref, q_ref, k_ref, v_ref):
        bd = _head_mean_matrix()

        def norm(x, g, scale):
            r = lax.rsqrt(_split3_dot(x * x, bd) + RMS_EPS)
            return (x * r * g * scale).astype(BF16)

        v_ref[...] = p_ref[:, 2 * SB_W:3 * SB_W].astype(BF16)
        for p in range(SB_PAIRS):
            cols = slice(p * LANE, (p + 1) * LANE)
            q_ref[:, cols] = norm(p_ref[:, cols], qg_ref[...], SB_DH ** -0.5)
            k_ref[:, cols] = norm(p_ref[:, SB_W + p * LANE:SB_W + (p + 1) * LANE], kg_ref[...], 1.0)

    blk = pl.BlockSpec((tt, SB_W), lambda i: (i, 0))
    vec = pl.BlockSpec((1, LANE), lambda i: (0, 0))
    return pl.pallas_call(
        body, name=name, grid=(T // tt,),
        in_specs=[pl.BlockSpec((tt, 4 * SB_W), lambda i: (i, 0)), vec, vec],
        out_specs=[blk, blk, blk],
        out_shape=[jax.ShapeDtypeStruct((T, SB_W), BF16)] * 3,
        compiler_params=_params("parallel"),
    )(proj, qg2, kg2)


def _sb_prep_bwd(proj, dqn, dkn, dv, dgate, qg2, kg2, name):
    T = proj.shape[0]
    tt = _tile(T, WIDE_ROW_TILE, SUBLANE)

    def body(p_ref, dq_ref, dk_ref, dv_ref, dg_ref, qg_ref, kg_ref, o_ref, dqg_ref, dkg_ref):
        i = pl.program_id(0)
        bd = _head_mean_matrix()

        def norm_bwd(x, g, dy):
            r = lax.rsqrt(_split3_dot(x * x, bd) + RMS_EPS)
            xh = x * r
            dxh = dy * g
            m = _split3_dot(dxh * xh, bd)
            return r * (dxh - xh * m), jnp.sum(dy * xh, axis=0, keepdims=True)

        o_ref[:, 2 * SB_W:3 * SB_W] = dv_ref[...].astype(BF16)
        o_ref[:, 3 * SB_W:4 * SB_W] = dg_ref[...].astype(BF16)
        pq = pk = jnp.zeros((1, LANE), F32)
        for p in range(SB_PAIRS):
            cols, kcols = slice(p * LANE, (p + 1) * LANE), slice(SB_W + p * LANE, SB_W + (p + 1) * LANE)
            dxq, sq = norm_bwd(p_ref[:, cols], qg_ref[...], dq_ref[:, cols])
            dxk, sk = norm_bwd(p_ref[:, kcols], kg_ref[...], dk_ref[:, cols])
            o_ref[:, cols] = dxq.astype(BF16)
            o_ref[:, kcols] = dxk.astype(BF16)
            pq, pk = pq + sq, pk + sk

        @pl.when(i == 0)
        def _():
            dqg_ref[...] = pq
            dkg_ref[...] = pk

        @pl.when(i > 0)
        def _():
            dqg_ref[...] += pq
            dkg_ref[...] += pk

    blk = pl.BlockSpec((tt, SB_W), lambda i: (i, 0))
    vec = pl.BlockSpec((1, LANE), lambda i: (0, 0))
    wide = pl.BlockSpec((tt, 4 * SB_W), lambda i: (i, 0))
    return pl.pallas_call(
        body, name=name, grid=(T // tt,),
        in_specs=[wide, blk, blk, blk, blk, vec, vec],
        out_specs=[wide, vec, vec],
        out_shape=[jax.ShapeDtypeStruct((T, 4 * SB_W), BF16)] + [jax.ShapeDtypeStruct((1, LANE), F32)] * 2,
        compiler_params=_params("arbitrary"),
    )(proj, dqn, dkn, dv, dgate, qg2, kg2)


def _fold_heads(part, name):
    def body(p_ref, o_ref):
        r, c = _iota2((LANE, SB_DH), 0), _iota2((LANE, SB_DH), 1)
        fold = jnp.where((r % SB_DH) == c, 1.0, 0.0).astype(F32)
        o_ref[...] = jnp.sum(_hdot(p_ref[...], fold), axis=0, keepdims=True)

    return pl.pallas_call(body, name=name, out_shape=jax.ShapeDtypeStruct((1, SB_DH), F32))(part)


def _sb_masks():
    lane = _iota2((1, LANE), 1)
    return lane < SB_DH


def _sb_attn_fwd(qn, kn, vb, proj, name, comm=None):
    T = qn.shape[0]
    tq, tk = _tile(T, SB_TQ, SUBLANE), SB_TK
    assert tq % tk == 0

    def body(q_ref, k_ref, v_ref, g_ref, o_ref, og_ref, lt_ref, done_ref):
        i = pl.program_id(1)
        ma = _sb_masks()
        q2 = q_ref[...]
        zero = jnp.zeros_like(q2)
        qs = (jnp.where(ma, q2, zero), jnp.where(ma, zero, q2))
        upper = (_iota2((tk, tk), 0) > _iota2((tk, tk), 1)).astype(BF16)
        qpos = i * tq + _iota2((tq, tk), 0)
        nb = tq // tk

        def trip(kb_top, masked, carry):
            acc, la, lb = carry
            chains = [(b, h) for b in range(nb) for h in range(2)]
            k2s, vss, masks = [], [], []
            for b in range(nb):
                kb = kb_top - b
                rows = pl.ds(pl.multiple_of(kb * tk, tk), tk)
                k2s.append(k_ref[rows, :])
                v2 = v_ref[rows, :]
                zv = jnp.zeros_like(v2)
                vss.append((jnp.where(ma, v2, zv), jnp.where(ma, zv, v2)))
                masks.append((kb * tk + _iota2((tq, tk), 1)) < qpos if masked else None)
            zs = [_dot(qs[h], k2s[b], NT) for b, h in chains]
            ts = [jnp.log(1.0 + jnp.exp(-jnp.abs(z))) for z in zs]
            ls = [-(jnp.maximum(z, 0.0) + t) for z, t in zip(zs, ts)]
            if masked:
                ls = [jnp.where(masks[b], l, 0.0) for (b, h), l in zip(chains, ls)]
            cums = [_split2_dot(l, upper) for l in ls]
            sums = [jnp.sum(l, axis=1, keepdims=True) for l in ls]
            offs, tot = {}, [la, lb]
            for b in range(nb):
                for h in range(2):
                    offs[(b, h)] = tot[h]
                    tot[h] = tot[h] + sums[chains.index((b, h))]
            ws = [jnp.exp(jnp.minimum(z, 0.0) - t + c + offs[ch]) for ch, z, t, c in zip(chains, zs, ts, cums)]
            if masked:
                ws = [jnp.where(masks[b], w, 0.0) for (b, h), w in zip(chains, ws)]
            for (b, h), w in zip(chains, ws):
                acc = acc + _dot(w.astype(BF16), vss[b][h])
            return acc, tot[0], tot[1]

        def largest(la, lb):
            return jnp.max(jnp.maximum(la, lb))

        z1 = jnp.zeros((tq, 1), F32)
        acc, la, lb = trip((i + 1) * nb - 1, True, (jnp.zeros((tq, LANE), F32), z1, z1))

        def live(c):
            return (c[0] < i) & (c[4] > SB_DEAD)

        def more(c):
            j, acc, la, lb, _ = c
            acc, la, lb = trip((i - j) * nb - 1, False, (acc, la, lb))
            return j + 1, acc, la, lb, largest(la, lb)

        done, acc, la, lb, _ = lax.while_loop(live, more, (jnp.int32(0), acc, la, lb, largest(la, lb)))
        gate = g_ref[...]
        o_ref[...] = acc
        og_ref[...] = (acc * (gate * _sigmoid(gate))).astype(BF16)
        lt_ref[...] = jnp.where(_iota2((tq, 2), 1) == 0, la, lb)
        done_ref[...] = jnp.full((SUBLANE, LANE), done, F32)

    nq = T // tq
    qblk = pl.BlockSpec((tq, LANE), lambda p, i: (i, p))
    full = pl.BlockSpec((T, LANE), lambda p, i: (0, p))
    return _call(
        body, comm, name=name, grid=(SB_PAIRS, nq),
        in_specs=[qblk, full, full, pl.BlockSpec((tq, LANE), lambda p, i: (i, 3 * SB_PAIRS + p))],
        out_specs=[qblk, qblk, pl.BlockSpec((None, tq, 2), lambda p, i: (p, i, 0)),
                   pl.BlockSpec((None, None, SUBLANE, LANE), lambda p, i: (p, i, 0, 0))],
        out_shape=[jax.ShapeDtypeStruct((T, SB_W), F32), jax.ShapeDtypeStruct((T, SB_W), BF16),
                   jax.ShapeDtypeStruct((SB_PAIRS, T, 2), F32), jax.ShapeDtypeStruct((SB_PAIRS, nq, SUBLANE, LANE), F32)],
        scratch_shapes=[], semantics=("parallel", "parallel"), args=(qn, kn, vb, proj))


def _sb_attn_bwd(qn, kn, vb, dog, o, ltot, done, proj, name, comm=None):
    T = qn.shape[0]
    tq, tk = _tile(T, SB_TQ, SUBLANE), SB_TK

    def body(q_ref, k_ref, v_ref, dog_ref, o_ref, lt_ref, done_ref, g_ref, dq_ref, dk_ref, dv_ref, dgate_ref):
        i = pl.program_id(1)
        first_trip = i - jnp.max(done_ref[...]).astype(jnp.int32)

        @pl.when(i == 0)
        def _():
            dk_ref[...] = jnp.zeros_like(dk_ref)
            dv_ref[...] = jnp.zeros_like(dv_ref)

        ma = _sb_masks()
        gate, o2, dog2 = g_ref[...], o_ref[...], dog_ref[...]
        sg, dsg = _silu_and_grad(gate)
        do2 = dog2 * sg
        dgate_ref[...] = dog2 * o2 * dsg
        lt = lt_ref[...]
        first = _iota2((tq, 2), 1) == 0
        ltots = (jnp.sum(jnp.where(first, lt, 0.0), axis=1, keepdims=True),
                 jnp.sum(jnp.where(first, 0.0, lt), axis=1, keepdims=True))
        q2 = q_ref[...]
        zq = jnp.zeros_like(q2)
        qs = (jnp.where(ma, q2, zq), jnp.where(ma, zq, q2))
        dob = do2.astype(BF16)
        dos = (jnp.where(ma, dob, zq), jnp.where(ma, zq, dob))
        upto = (_iota2((tk, tk), 0) <= _iota2((tk, tk), 1)).astype(BF16)
        before = (_iota2((tk, tk), 0) < _iota2((tk, tk), 1)).astype(BF16)
        qpos = i * tq + _iota2((tq, tk), 0)
        nb = tq // tk

        def trip(kb_bot, masked, carry):
            dq, la, lb, ea, eb = carry
            chains = [(b, h) for b in range(nb) for h in range(2)]
            rows, k2s, v2s, kss, masks = [], [], [], [], []
            for b in range(nb):
                kb = kb_bot + b
                rows.append(pl.ds(pl.multiple_of(kb * tk, tk), tk))
                k2 = k_ref[rows[b], :]
                zk = jnp.zeros_like(k2)
                k2s.append(k2)
                v2s.append(v_ref[rows[b], :])
                kss.append((jnp.where(ma, k2, zk), jnp.where(ma, zk, k2)))
                masks.append((kb * tk + _iota2((tq, tk), 1)) < qpos if masked else None)

            def keep(vals):
                return [jnp.where(masks[b], x, 0.0) for (b, h), x in zip(chains, vals)] if masked else vals

            zs = [_dot(qs[h], k2s[b], NT) for b, h in chains]
            dws = [_dot(dos[h], v2s[b], NT) for b, h in chains]
            ts = [jnp.log(1.0 + jnp.exp(-jnp.abs(z))) for z in zs]
            ls = keep([-(jnp.maximum(z, 0.0) + t) for z, t in zip(zs, ts)])
            lps = [jnp.minimum(z, 0.0) - t for z, t in zip(zs, ts)]
            cums = [_split3_dot(l, upto) for l in ls]
            lsums = [jnp.sum(l, axis=1, keepdims=True) for l in ls]
            offs, tot = {}, [la, lb]
            for b in range(nb):
                for h in range(2):
                    offs[(b, h)] = tot[h]
                    tot[h] = tot[h] + lsums[chains.index((b, h))]
            ws = keep([jnp.exp(lp + (ltots[h] - (offs[(b, h)] + c))) for (b, h), lp, c in zip(chains, lps, cums)])
            es = [dw * w for dw, w in zip(dws, ws)]
            ecums = [_split2_dot(e, before) for e in es]
            esums = [jnp.sum(e, axis=1, keepdims=True) for e in es]
            eoffs, etot = {}, [ea, eb]
            for b in range(nb):
                for h in range(2):
                    eoffs[(b, h)] = etot[h]
                    etot[h] = etot[h] + esums[chains.index((b, h))]
            dzs = keep([e - jnp.exp(lp) * (e + eoffs[ch] + ec) for ch, e, lp, ec in zip(chains, es, lps, ecums)])
            dzs = [dz.astype(BF16) for dz in dzs]
            wbs = [w.astype(BF16) for w in ws]
            for (b, h), dz in zip(chains, dzs):
                dq = dq + _dot(dz, kss[b][h])
            for b in range(nb):
                ia, ib = chains.index((b, 0)), chains.index((b, 1))
                dk_ref[rows[b], :] += _dot(dzs[ia], qs[0], TN) + _dot(dzs[ib], qs[1], TN)
                dv_ref[rows[b], :] += _dot(wbs[ia], dos[0], TN) + _dot(wbs[ib], dos[1], TN)
            return dq, tot[0], tot[1], etot[0], etot[1]

        z1 = jnp.zeros((tq, 1), F32)
        carry = lax.fori_loop(first_trip, i, lambda j, c: trip(j * nb, False, c),
                              (jnp.zeros((tq, LANE), F32), z1, z1, z1, z1))
        dq = trip(i * nb, True, carry)[0]
        dq_ref[...] = dq * (SB_DH ** -0.5)

    qblk = pl.BlockSpec((tq, LANE), lambda p, i: (i, p))
    full = pl.BlockSpec((T, LANE), lambda p, i: (0, p))
    return _call(
        body, comm, name=name, grid=(SB_PAIRS, T // tq),
        in_specs=[qblk, full, full, qblk, qblk, pl.BlockSpec((None, tq, 2), lambda p, i: (p, i, 0)),
                  pl.BlockSpec((None, None, SUBLANE, LANE), lambda p, i: (p, i, 0, 0)),
                  pl.BlockSpec((tq, LANE), lambda p, i: (i, 3 * SB_PAIRS + p))],
        out_specs=[qblk, full, full, qblk],
        out_shape=[jax.ShapeDtypeStruct((T, SB_W), F32)] * 4,
        scratch_shapes=[], semantics=("parallel", "arbitrary"), args=(qn, kn, vb, dog, o, ltot, done, proj))


def _dn_conv(ext, w_ref, cw):
    return (w_ref[3:4, cw] * ext + w_ref[2:3, cw] * _down(ext, 1) + w_ref[1:2, cw] * _down(ext, 2)
            + w_ref[0:1, cw] * _down(ext, 3))


def _dn_prep(pqkv, conv_w, name):
    T, W = pqkv.shape
    tt = _tile(T, CONV_ROW_TILE, SUBLANE)
    B = DN_PREP_BLK
    nq, nqk = DN_QK_W // B, 2 * DN_QK_W // B

    def body(p_ref, ph_ref, w_ref, o_ref):
        keep = (pl.program_id(0) > 0).astype(F32)
        for cb in range(W // B):
            cw = slice(cb * B, (cb + 1) * B)
            ext = jnp.concatenate([ph_ref[:, cw] * keep, p_ref[:, cw]], axis=0)
            c = _dn_conv(ext, w_ref, cw)[HALO:]
            a = c * _sigmoid(c)
            if cb >= nqk:
                o_ref[:, cw] = a
                continue
            scale = DN_DK ** -0.5 if cb < nq else 1.0
            for hh in range(B // DN_DK):
                ah = a[:, hh * DN_DK:(hh + 1) * DN_DK]
                r = lax.rsqrt(jnp.sum(ah * ah, axis=-1, keepdims=True) + L2_EPS)
                o_ref[:, cb * B + hh * DN_DK:cb * B + (hh + 1) * DN_DK] = ah * (r * scale)

    return pl.pallas_call(
        body, name=name, grid=(T // tt,),
        in_specs=[pl.BlockSpec((tt, W), lambda i: (i, 0)),
                  pl.BlockSpec((HALO, W), lambda i: (jnp.maximum(i * (tt // HALO) - 1, 0), 0)),
                  pl.BlockSpec((DN_CONV, W), lambda i: (0, 0))],
        out_specs=pl.BlockSpec((tt, W), lambda i: (i, 0)),
        out_shape=jax.ShapeDtypeStruct((T, W), F32),
        compiler_params=_params("parallel"),
    )(pqkv, pqkv, conv_w)


def _dn_prep_bwd(pqkv, conv_w, dact, name):
    T, W = pqkv.shape
    tt = _tile(T, CONV_ROW_TILE, SUBLANE)
    nt = T // tt
    hb = tt // HALO
    B = DN_PREP_BLK
    nq, nqk = DN_QK_W // B, 2 * DN_QK_W // B

    def body(p_ref, pp_ref, pn_ref, w_ref, d_ref, dn_ref, o_ref, dw_ref):
        i = pl.program_id(0)
        keep_p = (i > 0).astype(F32)
        keep_n = (i < nt - 1).astype(F32)
        main = slice(HALO, HALO + tt)
        parts = []
        for cb in range(W // B):
            cw = slice(cb * B, (cb + 1) * B)
            ext = jnp.concatenate([pp_ref[:, cw] * keep_p, p_ref[:, cw], pn_ref[:, cw]], axis=0)
            c = _dn_conv(ext, w_ref, cw)
            s = _sigmoid(c)
            da_dc = s * (1.0 + c * (1.0 - s))
            d_up = jnp.concatenate([jnp.zeros((HALO, B), F32), d_ref[:, cw], dn_ref[:, cw] * keep_n], axis=0)
            if cb < nqk:
                a = c * s
                scale = DN_DK ** -0.5 if cb < nq else 1.0
                normed = []
                for hh in range(B // DN_DK):
                    cols = slice(hh * DN_DK, (hh + 1) * DN_DK)
                    ah = a[:, cols]
                    r = lax.rsqrt(jnp.sum(ah * ah, axis=-1, keepdims=True) + L2_EPS)
                    y = ah * r
                    dy = d_up[:, cols] * scale
                    normed.append(r * (dy - y * jnp.sum(dy * y, axis=-1, keepdims=True)))
                d_up = jnp.concatenate(normed, axis=1)
            dc = d_up * da_dc
            dp = (w_ref[3:4, cw] * dc + w_ref[2:3, cw] * _up(dc, 1) + w_ref[1:2, cw] * _up(dc, 2)
                  + w_ref[0:1, cw] * _up(dc, 3))
            o_ref[:, cw] = dp[main].astype(BF16)
            dcm = dc[main]
            parts.append(jnp.concatenate([jnp.sum(dcm * _down(ext, 3 - k)[main], axis=0, keepdims=True)
                                          for k in range(DN_CONV)], axis=0))
        part = jnp.concatenate(parts, axis=1)

        @pl.when(i == 0)
        def _():
            dw_ref[...] = part

        @pl.when(i > 0)
        def _():
            dw_ref[...] += part

    main_spec = pl.BlockSpec((tt, W), lambda i: (i, 0))
    prev_spec = pl.BlockSpec((HALO, W), lambda i: (jnp.maximum(i * hb - 1, 0), 0))
    next_spec = pl.BlockSpec((HALO, W), lambda i: (jnp.minimum((i + 1) * hb, nt * hb - 1), 0))
    w_spec = pl.BlockSpec((DN_CONV, W), lambda i: (0, 0))
    return pl.pallas_call(
        body, name=name, grid=(nt,),
        in_specs=[main_spec, prev_spec, next_spec, w_spec, main_spec, next_spec],
        out_specs=[main_spec, w_spec],
        out_shape=[jax.ShapeDtypeStruct((T, W), BF16), jax.ShapeDtypeStruct((DN_CONV, W), F32)],
        compiler_params=_params("arbitrary"),
    )(pqkv, pqkv, pqkv, conv_w, dact, dact)


def _dn_gates(a_in, b_in, a_log, dt_bias, name):
    T, H = a_in.shape
    C = DN_CHUNK

    def body(a_ref, b_ref, al_ref, dt_ref, g_ref, beta_ref):
        beta_ref[...] = _sigmoid(b_ref[...])
        g_ref[...] = -jnp.exp(al_ref[...]) * _softplus(a_ref[...] + dt_ref[...])
        tri = (_iota2((C, C), 0) >= _iota2((C, C), 1)).astype(F32)

        def chunk(n, carry):
            rows = pl.ds(pl.multiple_of(n * C, C), C)
            g_ref[rows, :] = _hdot(tri, g_ref[rows, :])
            return carry

        lax.fori_loop(0, T // C, chunk, 0)

    return pl.pallas_call(body, name=name, out_shape=[jax.ShapeDtypeStruct((T, H), F32)] * 2)(a_in, b_in, a_log, dt_bias)


def _dn_gates_bwd(dg, dbeta, a_in, b_in, a_log, dt_bias, name):
    T, H = a_in.shape
    C = DN_CHUNK

    def body(dg_ref, db_ref, a_ref, b_ref, al_ref, dt_ref, da_ref, dbi_ref, dal_ref, ddt_ref):
        tri_t = (_iota2((C, C), 0) <= _iota2((C, C), 1)).astype(F32)

        def chunk(n, carry):
            rows = pl.ds(pl.multiple_of(n * C, C), C)
            da_ref[rows, :] = _hdot(tri_t, dg_ref[rows, :])
            return carry

        lax.fori_loop(0, T // C, chunk, 0)
        dla = da_ref[...]
        x = a_ref[...] + dt_ref[...]
        ea = jnp.exp(al_ref[...])
        da = dla * (-ea) * _sigmoid(x)
        da_ref[...] = da
        dal_ref[...] = jnp.sum(dla * (-ea * _softplus(x)), axis=0, keepdims=True)
        ddt_ref[...] = jnp.sum(da, axis=0, keepdims=True)
        beta = _sigmoid(b_ref[...])
        dbi_ref[...] = db_ref[...] * beta * (1.0 - beta)

    return pl.pallas_call(
        body, name=name,
        out_shape=[jax.ShapeDtypeStruct((T, H), F32)] * 2 + [jax.ShapeDtypeStruct((1, H), F32)] * 2,
    )(dg, dbeta, a_in, b_in, a_log, dt_bias)


def _dn_chunk_terms(q, k, gc, bc):
    C = DN_CHUNK
    r, c = _iota2((C, C), 0), _iota2((C, C), 1)
    lower, strict, eye = r >= c, r > c, r == c
    grow = jnp.sum(jnp.where(eye, gc, 0.0), axis=0, keepdims=True)
    decay = jnp.where(lower, jnp.exp(jnp.where(lower, gc - grow, 0.0)), 0.0)
    last = _iota2((C, 1), 0) == C - 1
    gl = jnp.sum(jnp.where(last, gc, 0.0), axis=0, keepdims=True)
    eg = jnp.exp(gc)
    egl = jnp.exp(gl - gc)
    kb = k * bc
    lmat = jnp.where(strict, _bdot(kb, k, NT) * decay, 0.0)
    aqk = jnp.where(lower, _bdot(q, k, NT) * decay, 0.0)
    return dict(lower=lower, strict=strict, eye=eye, last=last, decay=decay, gl=gl, eg=eg, egl=egl, kb=kb,
                lmat=lmat, aqk=aqk, qd=q * eg, kd=k * egl)


def _split(x):
    hi = x.astype(BF16)
    return hi, (x - hi.astype(F32)).astype(BF16)


def _x3dot(a, b, dims=NN):
    ah, al = a if isinstance(a, tuple) else _split(a)
    bh, bl = b if isinstance(b, tuple) else _split(b)
    return _dot(ah, bh, dims) + (_dot(ah, bl, dims) + _dot(al, bh, dims))


def _interleave(gens):
    for _ in itertools.zip_longest(*gens):
        pass


def _unit_lower_inverse_steps(lmat, eye, out):
    ident = jnp.where(eye, 1.0, 0.0).astype(F32)
    m = -lmat
    inv = ident + m
    for _ in range(int(math.log2(DN_CHUNK)) - 1):
        ms = _split(m)
        m = _x3dot(ms, ms)
        yield
        inv = inv + _x3dot(inv, m)
        yield
    out["tm"] = inv


def _dn_chunk_fwd(act, g, beta, pgate, gn, name, comm=None):
    T = act.shape[0]
    C, H = DN_CHUNK, DN_HEADS
    N = T // C

    def body(a_ref, g_ref, b_ref, pg_ref, gn_ref, o_ref, og_ref, s_out, t_out, vn_out, u_out, w_out, s_scr):
        n = pl.program_id(0)

        @pl.when(n == 0)
        def _():
            s_scr[...] = jnp.zeros_like(s_scr)

        head_lane = _iota2((C, H), 1)

        def head(hh):
            qs, vs = slice(hh * DN_DK, (hh + 1) * DN_DK), slice(hh * DN_DV, (hh + 1) * DN_DV)
            q, k, v = a_ref[:, qs], a_ref[:, DN_QK_W + hh * DN_DK:DN_QK_W + (hh + 1) * DN_DK], \
                a_ref[:, 2 * DN_QK_W + hh * DN_DV:2 * DN_QK_W + (hh + 1) * DN_DV]
            gc = jnp.sum(jnp.where(head_lane == hh, g_ref[...], 0.0), axis=1, keepdims=True)
            bc = jnp.sum(jnp.where(head_lane == hh, b_ref[...], 0.0), axis=1, keepdims=True)
            t = _dn_chunk_terms(q, k, gc, bc)
            yield
            res = {}
            yield from _unit_lower_inverse_steps(t["lmat"], t["eye"], res)
            tms = _split(res["tm"])
            u = _x3dot(tms, v * bc)
            yield
            w = _x3dot(tms, t["kb"] * t["eg"])
            yield
            s = s_scr[hh]
            s_out[hh] = s
            t_out[hh] = res["tm"]
            sb = s.astype(BF16)
            vn = u - _dot(w.astype(BF16), sb)
            yield
            o = _dot(t["qd"].astype(BF16), sb) + _bdot(t["aqk"], vn)
            yield
            s_scr[hh] = s * jnp.exp(t["gl"]) + _bdot(t["kd"], vn, TN)
            vn_out[:, vs] = vn
            u_out[:, vs] = u
            w_out[:, qs] = w
            o_ref[:, vs] = o
            gate = pg_ref[:, vs]
            r = lax.rsqrt(jnp.mean(o * o, axis=-1, keepdims=True) + RMS_EPS)
            og_ref[:, vs] = (o * r * gn_ref[...] * (gate * _sigmoid(gate))).astype(BF16)

        _interleave([head(hh) for hh in range(H)])

    row = lambda w: pl.BlockSpec((C, w), lambda n: (n, 0))
    return _call(
        body, comm, name=name, grid=(N,),
        in_specs=[row(DN_CONV_W), row(H), row(H), row(DN_V_W), pl.BlockSpec((1, DN_DV), lambda n: (0, 0))],
        out_specs=[row(DN_V_W), row(DN_V_W),
                   pl.BlockSpec((H, None, DN_DK, DN_DV), lambda n: (0, n, 0, 0)),
                   pl.BlockSpec((H, None, C, C), lambda n: (0, n, 0, 0)),
                   row(DN_V_W), row(DN_V_W), row(DN_QK_W)],
        out_shape=[jax.ShapeDtypeStruct((T, DN_V_W), F32), jax.ShapeDtypeStruct((T, DN_V_W), BF16),
                   jax.ShapeDtypeStruct((H, N, DN_DK, DN_DV), F32),
                   jax.ShapeDtypeStruct((H, N, C, C), F32),
                   jax.ShapeDtypeStruct((T, DN_V_W), F32),
                   jax.ShapeDtypeStruct((T, DN_V_W), F32),
                   jax.ShapeDtypeStruct((T, DN_QK_W), F32)],
        scratch_shapes=[pltpu.VMEM((H, DN_DK, DN_DV), F32)], semantics=("arbitrary",), args=(act, g, beta, pgate, gn))


def _dn_chunk_bwd(act, g, beta, s_saved, tm_saved, vn_saved, u_saved, w_saved, dog, o_raw, pgate, gn, name, comm=None):
    T = act.shape[0]
    C, H = DN_CHUNK, DN_HEADS
    N = T // C

    def body(a_ref, g_ref, b_ref, s_ref, t_ref, vn_ref, u_ref, w_ref, dog_ref, o_ref, pg_ref, gn_ref,
             da_ref, dg_ref, db_ref, dgate_ref, dgn_ref, ds_scr):
        @pl.when(pl.program_id(0) == 0)
        def _():
            ds_scr[...] = jnp.zeros_like(ds_scr)

        head_lane = _iota2((C, H), 1)
        dg_cols, db_cols, dgn_parts = {}, {}, {}

        def output_gate_bwd(hh, vs):
            d, o, gate, gn_v = dog_ref[:, vs], o_ref[:, vs], pg_ref[:, vs], gn_ref[...]
            sg, dsg = _silu_and_grad(gate)
            r = lax.rsqrt(jnp.mean(o * o, axis=-1, keepdims=True) + RMS_EPS)
            n = o * r
            dy = d * sg
            dgate_ref[:, vs] = (d * (n * gn_v) * dsg).astype(BF16)
            dn = dy * gn_v
            dgn_parts[hh] = jnp.sum(dy * n, axis=0, keepdims=True)
            return r * (dn - n * jnp.mean(dn * n, axis=-1, keepdims=True))

        def head(hh):
            qs, vs = slice(hh * DN_DK, (hh + 1) * DN_DK), slice(hh * DN_DV, (hh + 1) * DN_DV)
            ks = slice(DN_QK_W + hh * DN_DK, DN_QK_W + (hh + 1) * DN_DK)
            vas = slice(2 * DN_QK_W + hh * DN_DV, 2 * DN_QK_W + (hh + 1) * DN_DV)
            q, k, v = a_ref[:, qs], a_ref[:, ks], a_ref[:, vas]
            gc = jnp.sum(jnp.where(head_lane == hh, g_ref[...], 0.0), axis=1, keepdims=True)
            bc = jnp.sum(jnp.where(head_lane == hh, b_ref[...], 0.0), axis=1, keepdims=True)
            t = _dn_chunk_terms(q, k, gc, bc)
            yield
            lower, strict, eye = t["lower"], t["strict"], t["eye"]
            decay, eg, egl, kb, qd, kd = t["decay"], t["eg"], t["egl"], t["kb"], t["qd"], t["kd"]
            s, tm, vn, u, w = s_ref[hh], t_ref[hh], vn_ref[:, vs], u_ref[:, vs], w_ref[:, qs]
            d_o = output_gate_bwd(hh, vs)
            ds_next = ds_scr[hh]
            egl_tot = jnp.exp(t["gl"])
            dob, sb, dsb, vnb = d_o.astype(BF16), s.astype(BF16), ds_next.astype(BF16), vn.astype(BF16)

            dvn = _bdot(t["aqk"], dob, TN) + _bdot(kd, dsb)
            yield
            daqk = jnp.where(lower, _dot(dob, vnb, NT), 0.0)
            dqd = _dot(dob, sb, NT)
            dkd = _dot(vnb, dsb, NT)
            yield
            dvnb = dvn.astype(BF16)
            ds_scr[hh] = _bdot(qd, dob, TN) + egl_tot * ds_next - _bdot(w, dvnb, TN)
            dgl = egl_tot * jnp.sum(jnp.sum(s * ds_next, axis=1, keepdims=True), axis=0, keepdims=True)
            dw = -_dot(dvnb, sb, NT)
            yield
            tms = _split(tm)
            dru = _x3dot(tms, dvn, TN)
            drw = _x3dot(tms, dw, TN)
            yield
            dl = -jnp.where(strict, _x3dot(dru, u, NT) + _x3dot(drw, w, NT), 0.0)
            yield
            dkk = (dl * decay).astype(BF16)
            dqk = (daqk * decay).astype(BF16)
            dkb = _bdot(dkk, k) + drw * eg
            yield
            da_ref[:, ks] = _bdot(dkk, kb, TN) + _bdot(dqk, q, TN) + dkd * egl + dkb * bc
            da_ref[:, qs] = _bdot(dqk, k) + dqd * eg
            da_ref[:, vas] = dru * bc
            yield
            db_cols[hh] = jnp.sum(dru * v, axis=1, keepdims=True) + jnp.sum(dkb * k, axis=1, keepdims=True)
            pm = dl * t["lmat"] + daqk * t["aqk"]
            col_as_col = jnp.sum(jnp.where(eye, jnp.sum(pm, axis=0, keepdims=True), 0.0), axis=1, keepdims=True)
            kdsum = jnp.sum(dkd * kd, axis=1, keepdims=True)
            dgc = (jnp.sum(pm, axis=1, keepdims=True) - col_as_col + jnp.sum(dqd * qd, axis=1, keepdims=True)
                   - kdsum + jnp.sum(drw * (kb * eg), axis=1, keepdims=True))
            dgl = dgl + jnp.sum(kdsum, axis=0, keepdims=True)
            dg_cols[hh] = dgc + jnp.where(t["last"], dgl, 0.0)

        _interleave([head(hh) for hh in range(H)])
        dg_ref[...] = sum(jnp.where(head_lane == hh, dg_cols[hh], 0.0) for hh in range(H))
        db_ref[...] = sum(jnp.where(head_lane == hh, db_cols[hh], 0.0) for hh in range(H))
        dgn_part = sum(dgn_parts[hh] for hh in range(H))

        @pl.when(pl.program_id(0) == 0)
        def _():
            dgn_ref[...] = dgn_part

        @pl.when(pl.program_id(0) > 0)
        def _():
            dgn_ref[...] += dgn_part

    row = lambda w: pl.BlockSpec((C, w), lambda n: (N - 1 - n, 0))
    vec = pl.BlockSpec((1, DN_DV), lambda n: (0, 0))
    return _call(
        body, comm, name=name, grid=(N,),
        in_specs=[row(DN_CONV_W), row(H), row(H),
                  pl.BlockSpec((H, None, DN_DK, DN_DV), lambda n: (0, N - 1 - n, 0, 0)),
                  pl.BlockSpec((H, None, C, C), lambda n: (0, N - 1 - n, 0, 0)),
                  row(DN_V_W), row(DN_V_W), row(DN_QK_W), row(DN_V_W), row(DN_V_W), row(DN_V_W), vec],
        out_specs=[row(DN_CONV_W), row(H), row(H), row(DN_V_W), vec],
        out_shape=[jax.ShapeDtypeStruct((T, DN_CONV_W), F32),
                   jax.ShapeDtypeStruct((T, H), F32), jax.ShapeDtypeStruct((T, H), F32),
                   jax.ShapeDtypeStruct((T, DN_V_W), BF16), jax.ShapeDtypeStruct((1, DN_DV), F32)],
        scratch_shapes=[pltpu.VMEM((H, DN_DK, DN_DV), F32)], semantics=("arbitrary",),
        args=(act, g, beta, s_saved, tm_saved, vn_saved, u_saved, w_saved, dog, o_raw, pgate, gn))


def _dn_split_w_in(w):
    return w, jnp.pad(w[:, DN_CONV_W + DN_V_W:], ((0, 0), (0, DN_AB_PAD - 2 * DN_HEADS)))


def _dn_layer_fwd(h, wts, conv_w, a_log, dt_bias, gn, w_out, x_res, tag, comm=None):
    w_in, wab = wts
    H = DN_HEADS
    pqkv = _matmul(h, w_in, "nn", tag + "_pqkv", b_cols=(0, DN_CONV_W))
    pgate = _matmul(h, w_in, "nn", tag + "_pgate", b_cols=(DN_CONV_W, DN_V_W))
    pab = _matmul(h, wab, "nn", tag + "_pab")
    a_in, b_in = pab[:, :H], pab[:, H:2 * H]
    g, beta = _dn_gates(a_in, b_in, a_log, dt_bias, tag + "_gates")
    act = _dn_prep(pqkv, conv_w, tag + "_prep")
    (o_raw, og, s_sv, tm_sv, vn_sv, u_sv, w_sv), landed = _dn_chunk_fwd(act, g, beta, pgate, gn, tag + "_chunk_fwd", comm)
    if callable(w_out):
        w_out = w_out(landed)
    y = _matmul(og, w_out, "nn", tag + "_out", add=x_res)
    saved = dict(h=h, wts=wts, conv_w=conv_w, a_log=a_log, dt_bias=dt_bias, gn=gn, w_out=w_out, pqkv=pqkv, pgate=pgate,
                 a_in=a_in, b_in=b_in, g=g, beta=beta, act=act, o_raw=o_raw, chunk=(s_sv, tm_sv, vn_sv, u_sv, w_sv), og=og)
    return y, saved, landed


def _dn_layer_bwd(dout, sv, tag, comm_of=None, late_comm_of=None):
    w_in, wab = sv["wts"]
    h = sv["h"]
    dog = _matmul(dout, sv["w_out"], "nt", tag + "_dog")
    dw_out = _matmul(sv["og"], dout, "tn", tag + "_dwout", out_dtype=BF16)
    comm = comm_of(dw_out) if comm_of is not None else None
    (dact, dg, dbeta, dgate, dgn), landed = _dn_chunk_bwd(sv["act"], sv["g"], sv["beta"], *sv["chunk"], dog, sv["o_raw"],
                                                          sv["pgate"], sv["gn"], tag + "_chunk_bwd", comm)
    da_in, db_in, da_log, ddt = _dn_gates_bwd(dg, dbeta, sv["a_in"], sv["b_in"], sv["a_log"], sv["dt_bias"],
                                              tag + "_gates_bwd")
    dpqkv, dconv = _dn_prep_bwd(sv["pqkv"], sv["conv_w"], dact, tag + "_prep_bwd")
    dpab = jnp.pad(jnp.concatenate([da_in, db_in], axis=1), ((0, 0), (0, DN_AB_PAD - 2 * DN_HEADS)))
    dwqkv = _matmul(h, dpqkv, "tn", tag + "_dwqkv", out_dtype=BF16)
    dwgate = _matmul(h, dgate, "tn", tag + "_dwgate", out_dtype=BF16)
    dwab = _matmul(h, dpab, "tn", tag + "_dwab", out_dtype=BF16)
    dw_in = jnp.concatenate([dwqkv, dwgate, dwab[:, :2 * DN_HEADS]], axis=1)
    grads = dict(dn_w_in=dw_in, dn_conv_w=dconv, dn_a_log=da_log, dn_dt_bias=ddt, dn_o_norm_g=dgn, dn_w_out=dw_out)
    dh, landed_late = _matmul_nt_sum([(dpqkv, w_in, 0), (dgate, w_in, DN_CONV_W), (dpab, wab, 0)], tag + "_dh",
                                     late_comm_of(grads) if late_comm_of is not None else None)
    return dh, grads, landed, landed_late


def _sb_layer_fwd(h, w_in, qg, kg, w_out, x_res, tag, comm=None):
    qg2, kg2 = jnp.tile(qg, (1, 2)), jnp.tile(kg, (1, 2))
    proj = _matmul(h, w_in, "nn", tag + "_proj", blocked_b=True)
    qn, kn, vb = _sb_prep(proj, qg2, kg2, tag + "_prep")
    (o, og, ltot, done), landed = _sb_attn_fwd(qn, kn, vb, proj, tag + "_attn_fwd", comm)
    y = _matmul(og, w_out, "nn", tag + "_out", add=x_res)
    saved = dict(h=h, w_in=w_in, qg2=qg2, kg2=kg2, w_out=w_out, proj=proj, qn=qn, kn=kn, vb=vb, o=o, og=og, ltot=ltot,
                 done=done)
    return y, saved, landed


def _sb_layer_bwd(dout, sv, tag, comm=None):
    dog = _matmul(dout, sv["w_out"], "nt", tag + "_dog")
    dw_out = _matmul(sv["og"], dout, "tn", tag + "_dwout", out_dtype=BF16)
    (dqn, dkn, dv, dgate), landed = _sb_attn_bwd(sv["qn"], sv["kn"], sv["vb"], dog, sv["o"], sv["ltot"], sv["done"],
                                                 sv["proj"], tag + "_attn_bwd", comm)
    dproj, dqgp, dkgp = _sb_prep_bwd(sv["proj"], dqn, dkn, dv, dgate, sv["qg2"], sv["kg2"], tag + "_prep_bwd")
    dw_in = _matmul(sv["h"], dproj, "tn", tag + "_dwin", out_dtype=BF16, blocked_out=N_DEV)
    dh = _matmul(dproj, sv["w_in"], "nt", tag + "_dh", blocked_b=True)
    dqg = _fold_heads(dqgp, tag + "_dqg")
    dkg = _fold_heads(dkgp, tag + "_dkg")
    return dh, dict(sb_w_in=dw_in, sb_q_norm_g=dqg, sb_k_norm_g=dkg, sb_w_out=dw_out), landed


def _sc_layer_fwd(h, w_in, conv_w, w_out, x_res, tag):
    proj = _matmul(h, w_in, "nn", tag + "_proj", blocked_b=True)
    yg = _sc_fwd(proj, conv_w, tag + "_fwd")
    y = _matmul(yg, w_out, "nn", tag + "_out", add=x_res)
    return y, dict(h=h, w_in=w_in, conv_w=conv_w, w_out=w_out, proj=proj, yg=yg)


def _sc_layer_bwd(dout, sv, tag):
    dyg = _matmul(dout, sv["w_out"], "nt", tag + "_dyg")
    dw_out = _matmul(sv["yg"], dout, "tn", tag + "_dwout", out_dtype=BF16)
    dproj, dconv = _sc_bwd(dyg, sv["proj"], sv["conv_w"], tag + "_bwd")
    dw_in = _matmul(sv["h"], dproj, "tn", tag + "_dwin", out_dtype=BF16, blocked_out=N_DEV)
    dh = _matmul(dproj, sv["w_in"], "nt", tag + "_dh", blocked_b=True)
    return dh, dict(sc_w_in=dw_in, sc_conv_w=dconv, sc_w_out=dw_out)


def _adamw(w, m, v, parts, name):
    R, C = w.shape
    tr = _tile(R, 128, SUBLANE)

    def body(w_ref, m_ref, v_ref, p_ref, g_ref, d_ref, nm_ref, nv_ref):
        g = p_ref[0].astype(F32)
        for s in range(1, N_DEV):
            g = g + p_ref[s].astype(F32)
        m2 = ADAM_B1 * m_ref[...] + (1.0 - ADAM_B1) * g
        v2 = ADAM_B2 * v_ref[...] + (1.0 - ADAM_B2) * (g * g)
        m_hat = m2 / (1.0 - ADAM_B1 ** ADAM_STEP)
        v_hat = v2 / (1.0 - ADAM_B2 ** ADAM_STEP)
        g_ref[...] = g
        d_ref[...] = -ADAM_LR * (m_hat / (jnp.sqrt(v_hat) + ADAM_EPS) + ADAM_WD * w_ref[...])
        nm_ref[...] = m2
        nv_ref[...] = v2

    blk = pl.BlockSpec((tr, C), lambda i: (i, 0))
    return pl.pallas_call(
        body, name=name, grid=(R // tr,),
        in_specs=[blk, blk, blk, pl.BlockSpec((N_DEV, tr, C), lambda i: (0, i, 0))],
        out_specs=[blk] * 4, out_shape=[jax.ShapeDtypeStruct((R, C), F32)] * 4,
        compiler_params=_params("parallel"),
    )(w, m, v, parts)


_HBM = pl.BlockSpec(memory_space=pltpu.HBM)
_MESH = pl.DeviceIdType.MESH


def _slot(x, y, c):
    return 4 * x + 2 * y + c


class _Gather:
    def __init__(self, shards):
        self.arrays = list(shards)
        n = len(self.arrays)
        self.out_shapes = [jax.ShapeDtypeStruct((N_DEV,) + s.shape, s.dtype) for s in self.arrays]
        self.scratch = [pltpu.SemaphoreType.DMA((n, N_DEV - 1)), pltpu.SemaphoreType.DMA((n, N_DEV - 1)),
                        pltpu.SemaphoreType.DMA((n,))]

    def _parts(self, ins, outs, sems):
        send_sems, recv_sems, local_sems = sems
        n = len(self.arrays)
        x, y, c = lax.axis_index("x"), lax.axis_index("y"), lax.axis_index("c")
        me, sibling = (x, y, c), (x, y, 1 - c)
        chips = [(1 - x, y), (x, 1 - y), (1 - x, 1 - y)]

        def copy(a, k, block, to, src=None):
            dst = outs[a].at[_slot(*block)]
            return pltpu.make_async_remote_copy(src_ref=dst if src is None else src, dst_ref=dst,
                                                send_sem=send_sems.at[a, k], recv_sem=recv_sems.at[a, k],
                                                device_id=to, device_id_type=_MESH)

        mine = [pltpu.make_async_copy(ins[a], outs[a].at[_slot(*me)], local_sems.at[a]) for a in range(n)]
        first = []
        for a in range(n):
            first.append(copy(a, 0, me, sibling, src=ins[a]))
            first += [copy(a, 1 + j, me, (*chip, c), src=ins[a]) for j, chip in enumerate(chips)]
        return n, c, me, sibling, chips, copy, mine, first

    def start(self, ins, outs, sems):
        _, _, _, _, _, _, mine, first = self._parts(ins, outs, sems)
        for cp in mine + first:
            cp.start()

    def finish(self, ins, outs, sems):
        n, c, me, sibling, chips, copy, mine, first = self._parts(ins, outs, sems)
        passed = []
        for j, chip in enumerate(chips):
            for a in range(n):
                copy(a, 1 + j, (*chip, c), me).wait_recv()
                fwd = copy(a, 4 + j, (*chip, c), sibling)
                fwd.start()
                passed.append(fwd)
        for a in range(n):
            copy(a, 0, sibling, me).wait_recv()
            for j, chip in enumerate(chips):
                copy(a, 4 + j, (*chip, 1 - c), me).wait_recv()
        for cp in first + passed:
            cp.wait_send()
        for cp in mine:
            cp.wait()


class _Exchange:
    def __init__(self, arrays, scatter):
        self.arrays, self.scatter = list(arrays), list(scatter)
        n = len(self.arrays)
        shapes = [a.shape[1:] if s else a.shape for a, s in zip(self.arrays, self.scatter)]
        self.out_shapes = [jax.ShapeDtypeStruct((N_DEV,) + tuple(s), a.dtype) for s, a in zip(shapes, self.arrays)]
        self.scratch = [pltpu.SemaphoreType.DMA((n, N_DEV - 1)), pltpu.SemaphoreType.DMA((n, N_DEV - 1)),
                        pltpu.SemaphoreType.DMA((n,))]

    def _copies(self, ins, outs, sems):
        send_sems, recv_sems, local_sems = sems
        n, scatter = len(self.arrays), self.scatter
        x, y, c = lax.axis_index("x"), lax.axis_index("y"), lax.axis_index("c")
        me = _slot(x, y, c)
        copies = [pltpu.make_async_copy(ins[a].at[me] if scatter[a] else ins[a], outs[a].at[me], local_sems.at[a])
                  for a in range(n)]
        for r in range(1, N_DEV):
            px = 1 - x if r & 4 else x
            py = 1 - y if r & 2 else y
            pc = 1 - c if r & 1 else c
            for a in range(n):
                copies.append(pltpu.make_async_remote_copy(
                    src_ref=ins[a].at[_slot(px, py, pc)] if scatter[a] else ins[a], dst_ref=outs[a].at[me],
                    send_sem=send_sems.at[a, r - 1], recv_sem=recv_sems.at[a, r - 1],
                    device_id=(px, py, pc), device_id_type=_MESH))
        return copies

    def start(self, ins, outs, sems):
        for cp in self._copies(ins, outs, sems):
            cp.start()

    def finish(self, ins, outs, sems):
        for cp in self._copies(ins, outs, sems):
            cp.wait()


def _comm_call(comm, name):
    n = len(comm.arrays)

    def body(*refs):
        ins, outs, sems = refs[:n], refs[n:2 * n], refs[2 * n:]
        comm.start(ins, outs, sems)
        comm.finish(ins, outs, sems)

    return pl.pallas_call(body, name=name, in_specs=[_HBM] * n, out_specs=[_HBM] * n, out_shape=comm.out_shapes,
                          scratch_shapes=comm.scratch)(*comm.arrays)


def _call(body, comm, *, name, grid, in_specs, out_specs, out_shape, scratch_shapes, semantics, args):
    if comm is None:
        outs = pl.pallas_call(body, name=name, grid=grid, in_specs=in_specs, out_specs=out_specs, out_shape=out_shape,
                              scratch_shapes=scratch_shapes, compiler_params=_params(*semantics))(*args)
        return outs, []
    n_in, n_out, n_scr, n_c = len(in_specs), len(out_specs), len(scratch_shapes), len(comm.arrays)

    def fused(*refs):
        ins, refs = refs[:n_in], refs[n_in:]
        c_ins, refs = refs[:n_c], refs[n_c:]
        outs, refs = refs[:n_out], refs[n_out:]
        c_outs, refs = refs[:n_c], refs[n_c:]
        scr, sems = refs[:n_scr], refs[n_scr:]
        ids = [pl.program_id(d) for d in range(len(grid))]
        first = functools.reduce(jnp.logical_and, [i == 0 for i in ids])
        last = functools.reduce(jnp.logical_and, [i == g - 1 for i, g in zip(ids, grid)])

        @pl.when(first)
        def _():
            comm.start(c_ins, c_outs, sems)

        body(*ins, *outs, *scr)

        @pl.when(last)
        def _():
            comm.finish(c_ins, c_outs, sems)

    outs = pl.pallas_call(
        fused, name=name, grid=grid, in_specs=list(in_specs) + [_HBM] * n_c, out_specs=list(out_specs) + [_HBM] * n_c,
        out_shape=list(out_shape) + comm.out_shapes, scratch_shapes=list(scratch_shapes) + comm.scratch,
        compiler_params=_params(*["arbitrary"] * len(grid)))(*args, *comm.arrays)
    return outs[:n_out], outs[n_out:]


_GATHER_0 = (("dn_w_in", 0), ("dn_conv_w", 0), ("dn_o_norm_g", 0))
_GATHER_1 = (("dn_w_out", 0), ("sb_w_in", 0), ("sb_w_out", 0))
_GATHER_2 = (("sc_w_in", 0), ("sc_conv_w", 0), ("sc_w_out", 0), ("dn_w_in", 1), ("dn_conv_w", 1), ("dn_o_norm_g", 1),
             ("dn_w_out", 1))
_EXCHANGE_A = _GATHER_2
_EXCHANGE_B = (("sb_w_in", 0), ("sb_w_out", 0), ("dn_w_out", 0))
_EXCHANGE_C = _GATHER_0
_MATMUL_WEIGHTS = ("dn_w_in", "dn_w_out", "sb_w_in", "sb_w_out", "sc_w_in", "sc_w_out")
_COLUMN_SHARDED = ("dn_w_in", "dn_conv_w", "dn_o_norm_g", "sb_w_in", "sc_w_in", "sc_conv_w")
_BLOCKED = ("sb_w_in", "sc_w_in")
_REPLICATED = ("norm_g", "dn_a_log", "dn_dt_bias", "sb_q_norm_g", "sb_k_norm_g")
_ORDER = ("norm_g", "dn_w_in", "dn_conv_w", "dn_a_log", "dn_dt_bias", "dn_o_norm_g", "dn_w_out", "sb_w_in", "sb_q_norm_g",
          "sb_k_norm_g", "sb_w_out", "sc_w_in", "sc_conv_w", "sc_w_out")
_PACK_COLS = D_MODEL


def _as_2d(a):
    return a.reshape(1, -1) if a.ndim == 1 else a


def _assemble(name, gathered):
    n, r, c = gathered.shape
    if name in _COLUMN_SHARDED:
        return jnp.moveaxis(gathered, 0, 1).reshape(r, n * c)
    return gathered.reshape(n * r, c)


def _disassemble(name, full):
    r, c = full.shape
    if name in _COLUMN_SHARDED:
        return jnp.moveaxis(full.reshape(r, N_DEV, c // N_DEV), 1, 0)
    return full.reshape(N_DEV, r // N_DEV, c)


def _pack_replicated(d):
    rows = [d["norm_g"]]
    for name in _REPLICATED[1:]:
        flat = d[name].reshape(1, -1)
        rows.append(jnp.pad(flat, ((0, 0), (0, _PACK_COLS - flat.shape[1]))))
    return jnp.concatenate(rows, axis=0)


def _unpack_replicated(p, like):
    out = {"norm_g": p[:4]}
    for r, name in enumerate(_REPLICATED[1:]):
        shape = like[name].shape
        out[name] = p[4 + r, :math.prod(shape)].reshape(shape)
    return out


def kernel(x, norm_g, dn_w_in, dn_conv_w, dn_a_log, dn_dt_bias, dn_o_norm_g, dn_w_out, sb_w_in, sb_q_norm_g, sb_k_norm_g, sb_w_out, sc_w_in, sc_conv_w, sc_w_out, loss_target, m_norm_g, m_dn_w_in, m_dn_conv_w, m_dn_a_log, m_dn_dt_bias, m_dn_o_norm_g, m_dn_w_out, m_sb_w_in, m_sb_q_norm_g, m_sb_k_norm_g, m_sb_w_out, m_sc_w_in, m_sc_conv_w, m_sc_w_out, v_norm_g, v_dn_w_in, v_dn_conv_w, v_dn_a_log, v_dn_dt_bias, v_dn_o_norm_g, v_dn_w_out, v_sb_w_in, v_sb_q_norm_g, v_sb_k_norm_g, v_sb_w_out, v_sc_w_in, v_sc_conv_w, v_sc_w_out):
    w = dict(norm_g=norm_g, dn_w_in=dn_w_in, dn_conv_w=dn_conv_w, dn_a_log=dn_a_log, dn_dt_bias=dn_dt_bias,
             dn_o_norm_g=dn_o_norm_g, dn_w_out=dn_w_out, sb_w_in=sb_w_in, sb_q_norm_g=sb_q_norm_g, sb_k_norm_g=sb_k_norm_g,
             sb_w_out=sb_w_out, sc_w_in=sc_w_in, sc_conv_w=sc_conv_w, sc_w_out=sc_w_out)
    m = dict(norm_g=m_norm_g, dn_w_in=m_dn_w_in, dn_conv_w=m_dn_conv_w, dn_a_log=m_dn_a_log, dn_dt_bias=m_dn_dt_bias,
             dn_o_norm_g=m_dn_o_norm_g, dn_w_out=m_dn_w_out, sb_w_in=m_sb_w_in, sb_q_norm_g=m_sb_q_norm_g,
             sb_k_norm_g=m_sb_k_norm_g, sb_w_out=m_sb_w_out, sc_w_in=m_sc_w_in, sc_conv_w=m_sc_conv_w, sc_w_out=m_sc_w_out)
    v = dict(norm_g=v_norm_g, dn_w_in=v_dn_w_in, dn_conv_w=v_dn_conv_w, dn_a_log=v_dn_a_log, dn_dt_bias=v_dn_dt_bias,
             dn_o_norm_g=v_dn_o_norm_g, dn_w_out=v_dn_w_out, sb_w_in=v_sb_w_in, sb_q_norm_g=v_sb_q_norm_g,
             sb_k_norm_g=v_sb_k_norm_g, sb_w_out=v_sb_w_out, sc_w_in=v_sc_w_in, sc_conv_w=v_sc_conv_w, sc_w_out=v_sc_w_out)

    def gather_of(keys):
        return _Gather([_as_2d(w[k][j]).astype(BF16) if k in _MATMUL_WEIGHTS else _as_2d(w[k][j]) for k, j in keys])

    def full_weights(keys, gathered):
        return {key: g if key[0] in _BLOCKED else _assemble(key[0], g) for key, g in zip(keys, gathered)}

    def exchange_of(keys, grads, extra=()):
        out = [grads[k, j] if k in _BLOCKED else
               _disassemble(k, grads[k, j].astype(BF16) if k in _MATMUL_WEIGHTS else grads[k, j]) for k, j in keys]
        return _Exchange(out + list(extra), [True] * len(out) + [False] * len(extra))

    F = full_weights(_GATHER_0, _comm_call(gather_of(_GATHER_0), "gather_first"))
    xs, saves = [x[0]], []
    h = _rmsnorm_fwd(xs[0], norm_g[0:1], "norm0")

    def w_out_0(got):
        F.update(full_weights(_GATHER_1, got))
        return F["dn_w_out", 0]

    y, sv, _ = _dn_layer_fwd(h, _dn_split_w_in(F["dn_w_in", 0]), F["dn_conv_w", 0], dn_a_log[0:1], dn_dt_bias[0:1],
                             F["dn_o_norm_g", 0], w_out_0, xs[0], "dn0", gather_of(_GATHER_1))
    xs.append(y)
    saves.append(sv)
    h = _rmsnorm_fwd(xs[1], norm_g[1:2], "norm1")
    y, sv, got = _sb_layer_fwd(h, F["sb_w_in", 0], sb_q_norm_g, sb_k_norm_g, F["sb_w_out", 0], xs[1], "sb", gather_of(_GATHER_2))
    F.update(full_weights(_GATHER_2, got))
    xs.append(y)
    saves.append(sv)
    h = _rmsnorm_fwd(xs[2], norm_g[2:3], "norm2")
    y, sv = _sc_layer_fwd(h, F["sc_w_in", 0], F["sc_conv_w", 0], F["sc_w_out", 0], xs[2], "sc")
    xs.append(y)
    saves.append(sv)
    h = _rmsnorm_fwd(xs[3], norm_g[3:4], "norm3")
    y, sv, _ = _dn_layer_fwd(h, _dn_split_w_in(F["dn_w_in", 1]), F["dn_conv_w", 1], dn_a_log[1:2], dn_dt_bias[1:2],
                             F["dn_o_norm_g", 1], F["dn_w_out", 1], xs[3], "dn1")
    xs.append(y)
    saves.append(sv)
    dx, loss_part = _loss_head(xs[4], loss_target[0])

    G, dnorm, landed = {}, [None] * 4, {}

    def keep(grads, j):
        G.update({(k, j): g for k, g in grads.items()})

    dh, grads, _, _ = _dn_layer_bwd(dx, saves[3], "dn1")
    keep(grads, 1)
    dx, dnorm[3] = _rmsnorm_bwd(dh, xs[3], norm_g[3:4], dx, "norm3_bwd")
    dh, grads = _sc_layer_bwd(dx, saves[2], "sc")
    keep(grads, 0)
    dx, dnorm[2] = _rmsnorm_bwd(dh, xs[2], norm_g[2:3], dx, "norm2_bwd")
    dh, grads, got = _sb_layer_bwd(dx, saves[1], "sb", exchange_of(_EXCHANGE_A, G))
    keep(grads, 0)
    landed.update(zip(_EXCHANGE_A, got))
    dx, dnorm[1] = _rmsnorm_bwd(dh, xs[1], norm_g[1:2], dx, "norm1_bwd")

    def exchange_b(dw_out):
        G["dn_w_out", 0] = dw_out
        return exchange_of(_EXCHANGE_B, G)

    def exchange_c(grads):
        keep(grads, 0)
        return exchange_of(_EXCHANGE_C, G)

    dh, grads, got, got_late = _dn_layer_bwd(dx, saves[0], "dn0", exchange_b, exchange_c)
    landed.update(zip(_EXCHANGE_B, got))
    landed.update(zip(_EXCHANGE_C, got_late))
    dx, dnorm[0] = _rmsnorm_bwd(dh, xs[0], norm_g[0:1], dx, "norm0_bwd")
    replicated = dict(norm_g=jnp.concatenate(dnorm, axis=0),
                      dn_a_log=jnp.concatenate([G["dn_a_log", 0], G["dn_a_log", 1]], axis=0),
                      dn_dt_bias=jnp.concatenate([G["dn_dt_bias", 0], G["dn_dt_bias", 1]], axis=0),
                      sb_q_norm_g=G["sb_q_norm_g", 0], sb_k_norm_g=G["sb_k_norm_g", 0])
    got = _comm_call(_Exchange([_pack_replicated(replicated)], [False]), "exchange_replicated")

    res = {}
    for k in _ORDER:
        if k in _REPLICATED:
            continue
        per_layer = []
        for j in range(w[k].shape[0]):
            shape = w[k][j].shape
            outs = _adamw(_as_2d(w[k][j]), _as_2d(m[k][j]), _as_2d(v[k][j]), landed[k, j], f"adamw_{k}{j}")
            per_layer.append([o.reshape(shape) for o in outs])
        res[k] = [jnp.stack([layer[i] for layer in per_layer], axis=0) for i in range(4)]
    outs = _adamw(_pack_replicated(w), _pack_replicated(m), _pack_replicated(v), got[-1], "adamw_replicated")
    unpacked = [_unpack_replicated(o, w) for o in outs]
    for k in _REPLICATED:
        res[k] = [u[k] for u in unpacked]

    loss = lax.psum(loss_part[0, 0], ("x", "y", "c"))
    return (loss, dx[None]) + tuple(res[k][0] for k in _ORDER) + tuple(res[k][1] for k in _ORDER) \
        + tuple(res[k][2] for k in _ORDER) + tuple(res[k][3] for k in _ORDER)
```

```python
import functools
import itertools
import math

import jax
import jax.numpy as jnp
from jax import lax
from jax.experimental import pallas as pl
from jax.experimental.pallas import tpu as pltpu

F32 = jnp.float32
BF16 = jnp.bfloat16
HIGHEST = lax.Precision.HIGHEST

N_DEV = 8
D_MODEL = 1024
RMS_EPS = 1e-6
L2_EPS = 1e-6

DN_HEADS = 8
DN_DK = 128
DN_DV = 256
DN_QK_W = DN_HEADS * DN_DK
DN_V_W = DN_HEADS * DN_DV
DN_CONV = 4
DN_CHUNK = 64
DN_CONV_W = 2 * DN_QK_W + DN_V_W
DN_IN = DN_CONV_W + DN_V_W + 2 * DN_HEADS
DN_AB_PAD = 128
DN_PREP_BLK = 512

SB_HEADS = 16
SB_DH = 64
SB_W = SB_HEADS * SB_DH
SB_PAIRS = SB_HEADS // 2
SB_TQ = 256
SB_TK = 128
SB_DEAD = -106.0

SC_W = 2 * D_MODEL
SC_CONV = 3
SC_BLK = 512
SC_NBLK = SC_W // SC_BLK

ADAM_LR = 0.001
ADAM_B1 = 0.9
ADAM_B2 = 0.999
ADAM_EPS = 1e-08
ADAM_WD = 0.01
ADAM_STEP = 10

LANE = 128
SUBLANE = 8
HALO = SUBLANE
LONG_ROW_TILE = 512
NORM_FUSED_TM = 512
WIDE_ROW_TILE = 128
CONV_ROW_TILE = 256
VMEM_LIMIT = 48 * 2 ** 20

NN = ((1,), (0,))
NT = ((1,), (1,))
TN = ((0,), (0,))


def _dot(a, b, dims=NN, precision=None):
    return lax.dot_general(a, b, (dims, ((), ())), precision=precision, preferred_element_type=F32)


def _bdot(a, b, dims=NN):
    return _dot(a.astype(BF16), b.astype(BF16), dims)


def _hdot(a, b, dims=NN):
    return _dot(a, b, dims, precision=HIGHEST)


def _tile(dim, pref, align=LANE):
    t = (min(pref, dim) // align) * align
    while t >= align:
        if dim % t == 0:
            return t
        t -= align
    return dim


def _params(*sem):
    return pltpu.CompilerParams(dimension_semantics=sem, vmem_limit_bytes=VMEM_LIMIT)


def _sigmoid(x):
    return 0.5 * jnp.tanh(0.5 * x) + 0.5


def _softplus(x):
    return jnp.maximum(x, 0.0) + jnp.log(1.0 + jnp.exp(-jnp.abs(x)))


def _silu_and_grad(x):
    s = _sigmoid(x)
    return x * s, s * (1.0 + x * (1.0 - s))


def _iota2(shape, dim):
    return lax.broadcasted_iota(jnp.int32, shape, dim)


def _matmul(a, b, mode, name, out_dtype=F32, add=None, b_cols=None, blocked_b=False, blocked_out=0,
            norm_fwd=None, norm_bwd=None, tm=1024, tn=1024, tk=1024):
    b_rows, b_width = (b.shape[1], b.shape[0] * b.shape[2]) if blocked_b else b.shape
    c0, b_used = b_cols if b_cols is not None else (0, b_width)
    if mode == "nn":
        (M, K), (K2, N) = a.shape, (b_rows, b_used)
    elif mode == "nt":
        (M, K), (N, K2) = a.shape, (b_rows, b_used)
    else:
        (K, M), (K2, N) = a.shape, (b_rows, b_used)
    assert K == K2, (a.shape, b.shape, mode)
    tm, tn, tk = _tile(M, tm), _tile(N, tn), _tile(K, tk)
    if blocked_b and mode == "nt":
        tk = b.shape[2]
    elif blocked_b:
        tn = b.shape[2]
    if blocked_out:
        tn = N // blocked_out
    nk = K // tk
    dims = {"nn": NN, "nt": NT, "tn": TN}[mode]
    a_spec = pl.BlockSpec((tk, tm), lambda i, j, k: (k, i)) if mode == "tn" else pl.BlockSpec((tm, tk), lambda i, j, k: (i, k))
    if mode == "nt":
        cb0 = c0 // tk
        assert c0 % tk == 0
        b_spec = (pl.BlockSpec((None, tn, tk), lambda i, j, k: (k + cb0, j, 0)) if blocked_b
                  else pl.BlockSpec((tn, tk), lambda i, j, k: (j, k + cb0)))
    else:
        cb0 = c0 // tn
        assert c0 % tn == 0
        b_spec = (pl.BlockSpec((None, tk, tn), lambda i, j, k: (j + cb0, k, 0)) if blocked_b
                  else pl.BlockSpec((tk, tn), lambda i, j, k: (k, j + cb0)))
    o_spec = pl.BlockSpec((tm, tn), lambda i, j, k: (i, j))
    out_spec = pl.BlockSpec((None, tm, tn), lambda i, j, k: (j, i, 0)) if blocked_out else o_spec
    out_shape = (blocked_out, M, tn) if blocked_out else (M, N)
    has_add = add is not None
    vec_spec = pl.BlockSpec((1, tn), lambda i, j, k: (0, j))
    assert not (norm_fwd is not None or norm_bwd is not None) or tn == N
    extra_in, extra_specs = [], []
    if has_add:
        extra_in, extra_specs = [add], [o_spec]
    if norm_fwd is not None:
        extra_in, extra_specs = extra_in + [norm_fwd], extra_specs + [vec_spec]
        out_specs = [o_spec, o_spec]
        out_shapes = [jax.ShapeDtypeStruct((M, N), out_dtype), jax.ShapeDtypeStruct((M, N), BF16)]
    elif norm_bwd is not None:
        extra_in, extra_specs = extra_in + list(norm_bwd), extra_specs + [o_spec, vec_spec, o_spec]
        out_specs = [o_spec, vec_spec]
        out_shapes = [jax.ShapeDtypeStruct((M, N), F32), jax.ShapeDtypeStruct((1, N), F32)]
    else:
        out_specs, out_shapes = out_spec, jax.ShapeDtypeStruct(out_shape, out_dtype)

    def body(*refs):
        a_ref, b_ref = refs[0], refs[1]
        extra = list(refs[2:2 + len(extra_in)])
        outs = refs[2 + len(extra_in):]
        add_ref = extra.pop(0) if has_add else None
        p = _bdot(a_ref[...], b_ref[...], dims)

        def finish(acc):
            if has_add:
                acc = acc + add_ref[...]
            if norm_bwd is not None:
                _rmsnorm_bwd_tile(acc, *extra, outs[0], outs[1], first=pl.program_id(0) == 0)
                return
            outs[0][...] = acc.astype(out_dtype)
            if norm_fwd is not None:
                r = lax.rsqrt(jnp.mean(acc * acc, axis=-1, keepdims=True) + RMS_EPS)
                outs[1][...] = (acc * r * extra[0][...]).astype(BF16)

        if nk == 1:
            finish(p)
        else:
            acc_ref = refs[-1]
            k = pl.program_id(2)

            @pl.when(k == 0)
            def _():
                acc_ref[...] = p

            @pl.when(k > 0)
            def _():
                acc_ref[...] += p

            @pl.when(k == nk - 1)
            def _():
                finish(acc_ref[...])

    return pl.pallas_call(
        body, name=name, grid=(M // tm, N // tn, nk),
        in_specs=[a_spec, b_spec] + extra_specs, out_specs=out_specs, out_shape=out_shapes,
        scratch_shapes=[pltpu.VMEM((tm, tn), F32)] if nk > 1 else [],
        compiler_params=(_params("arbitrary", "arbitrary", "arbitrary") if norm_bwd is not None
                         else _params("parallel", "parallel", "arbitrary")),
    )(a, b, *extra_in)


def _rmsnorm_bwd_tile(dh, x_ref, g_ref, res_ref, dx_ref, dg_ref, first):
    xv = x_ref[...]
    r = lax.rsqrt(jnp.mean(xv * xv, axis=-1, keepdims=True) + RMS_EPS)
    xh = xv * r
    dxh = dh * g_ref[...]
    m = jnp.mean(dxh * xh, axis=-1, keepdims=True)
    dx_ref[...] = res_ref[...] + r * (dxh - xh * m)
    part = jnp.sum(dh * xh, axis=0, keepdims=True)

    @pl.when(first)
    def _():
        dg_ref[...] = part

    @pl.when(jnp.logical_not(first))
    def _():
        dg_ref[...] += part


def _matmul_nt_sum(pairs, name, comm=None, norm_bwd=None, tm=NORM_FUSED_TM, tk=1024):
    M, N = pairs[0][0].shape[0], pairs[0][1].shape[0]
    tm = _tile(M, tm)
    tks = [_tile(a.shape[1], tk) for a, _, _ in pairs]
    steps = [a.shape[1] // t for (a, _, _), t in zip(pairs, tks)]
    offs = [sum(steps[:p]) for p in range(len(pairs))]
    total = sum(steps)

    n_extra = 3 if norm_bwd is not None else 0

    def body(*refs):
        a_refs, b_refs = refs[0:2 * len(pairs):2], refs[1:2 * len(pairs):2]
        extra = refs[2 * len(pairs):2 * len(pairs) + n_extra]
        outs, acc_ref = refs[2 * len(pairs) + n_extra:-1], refs[-1]
        k = pl.program_id(1)
        for p in range(len(pairs)):
            @pl.when((k >= offs[p]) & (k < offs[p] + steps[p]))
            def _(p=p):
                prod = _bdot(a_refs[p][...], b_refs[p][...], NT)
                if p == 0:
                    @pl.when(k == 0)
                    def _():
                        acc_ref[...] = prod

                    @pl.when(k > 0)
                    def _():
                        acc_ref[...] += prod
                else:
                    acc_ref[...] += prod

        @pl.when(k == total - 1)
        def _():
            if norm_bwd is not None:
                _rmsnorm_bwd_tile(acc_ref[...], *extra, outs[0], outs[1], first=pl.program_id(0) == 0)
            else:
                outs[0][...] = acc_ref[...]

    in_specs, args = [], []
    for (a, b, c0), t, off, n in zip(pairs, tks, offs, steps):
        assert c0 % t == 0
        pick = lambda k, off=off, n=n: jnp.clip(k - off, 0, n - 1)
        in_specs += [pl.BlockSpec((tm, t), lambda i, k, pick=pick: (i, pick(k))),
                     pl.BlockSpec((N, t), lambda i, k, pick=pick, cb0=c0 // t: (0, pick(k) + cb0))]
        args += [a, b]
    row, vec = pl.BlockSpec((tm, N), lambda i, k: (i, 0)), pl.BlockSpec((1, N), lambda i, k: (0, 0))
    if norm_bwd is not None:
        in_specs += [row, vec, row]
        args += list(norm_bwd)
        out_specs, out_shape = [row, vec], [jax.ShapeDtypeStruct((M, N), F32), jax.ShapeDtypeStruct((1, N), F32)]
    else:
        out_specs, out_shape = [row], [jax.ShapeDtypeStruct((M, N), F32)]
    outs, landed = _call(body, comm, name=name, grid=(M // tm, total), in_specs=in_specs, out_specs=out_specs,
                         out_shape=out_shape, scratch_shapes=[pltpu.VMEM((tm, N), F32)],
                         semantics=("arbitrary", "arbitrary"), args=tuple(args))
    return (outs if norm_bwd is not None else outs[0]), landed


def _rmsnorm_fwd(x, g, name):
    T, D = x.shape
    tt = _tile(T, LONG_ROW_TILE, SUBLANE)

    def body(x_ref, g_ref, o_ref):
        xv = x_ref[...]
        r = lax.rsqrt(jnp.mean(xv * xv, axis=-1, keepdims=True) + RMS_EPS)
        o_ref[...] = (xv * r * g_ref[...]).astype(BF16)

    return pl.pallas_call(
        body, name=name, grid=(T // tt,),
        in_specs=[pl.BlockSpec((tt, D), lambda i: (i, 0)), pl.BlockSpec((1, D), lambda i: (0, 0))],
        out_specs=pl.BlockSpec((tt, D), lambda i: (i, 0)),
        out_shape=jax.ShapeDtypeStruct((T, D), BF16),
        compiler_params=_params("parallel"),
    )(x, g)


def _loss_head(y, target, name="loss_head"):
    T, D = y.shape
    tt = _tile(T, LONG_ROW_TILE, SUBLANE)

    def body(y_ref, t_ref, dy_ref, l_ref):
        e = y_ref[...] - t_ref[...]
        dy_ref[...] = e * (1.0 / D)
        s = jnp.sum(jnp.sum(e * e, axis=1, keepdims=True), axis=0, keepdims=True) * (0.5 / D)
        s = jnp.broadcast_to(s, (1, LANE))

        @pl.when(pl.program_id(0) == 0)
        def _():
            l_ref[...] = s

        @pl.when(pl.program_id(0) > 0)
        def _():
            l_ref[...] += s

    row = pl.BlockSpec((tt, D), lambda i: (i, 0))
    return pl.pallas_call(
        body, name=name, grid=(T // tt,),
        in_specs=[row, row], out_specs=[row, pl.BlockSpec((1, LANE), lambda i: (0, 0))],
        out_shape=[jax.ShapeDtypeStruct((T, D), F32), jax.ShapeDtypeStruct((1, LANE), F32)],
        compiler_params=_params("arbitrary"),
    )(y, target)


def _down(x, k):
    return pltpu.roll(x, k, 0) if k else x


def _up(x, k):
    return pltpu.roll(x, x.shape[0] - k, 0) if k else x


def _sc_fwd(proj, conv_w, name):
    T = proj.shape[0]
    tt = _tile(T, WIDE_ROW_TILE, SUBLANE)
    B = SC_BLK

    def body(p_ref, ph_ref, w_ref, o_ref):
        keep = (pl.program_id(0) > 0).astype(F32)
        for j in range(SC_NBLK):
            cb, cc, cu, cg = (slice(k * SC_W + j * B, k * SC_W + (j + 1) * B) for k in range(4))
            cw = slice(j * B, (j + 1) * B)
            z = jnp.concatenate([ph_ref[:, cc] * ph_ref[:, cu] * keep, p_ref[:, cc] * p_ref[:, cu]], axis=0)
            cz = (w_ref[2:3, cw] * z + w_ref[1:2, cw] * _down(z, 1) + w_ref[0:1, cw] * _down(z, 2))[HALO:]
            gate = p_ref[:, cg]
            o_ref[:, cw] = (p_ref[:, cb] * cz * (gate * _sigmoid(gate))).astype(BF16)

    return pl.pallas_call(
        body, name=name, grid=(T // tt,),
        in_specs=[pl.BlockSpec((tt, 4 * SC_W), lambda i: (i, 0)),
                  pl.BlockSpec((HALO, 4 * SC_W), lambda i: (jnp.maximum(i * (tt // HALO) - 1, 0), 0)),
                  pl.BlockSpec((SC_CONV, SC_W), lambda i: (0, 0))],
        out_specs=pl.BlockSpec((tt, SC_W), lambda i: (i, 0)),
        out_shape=jax.ShapeDtypeStruct((T, SC_W), BF16),
        compiler_params=_params("parallel"),
    )(proj, proj, conv_w)


def _sc_bwd(dyg, proj, conv_w, name):
    T = proj.shape[0]
    tt = _tile(T, WIDE_ROW_TILE, SUBLANE)
    nt = T // tt
    B = SC_BLK
    hb = tt // HALO

    def body(d_ref, dn_ref, p_ref, pp_ref, pn_ref, w_ref, o_ref, dw_ref):
        i = pl.program_id(0)
        keep_p = (i > 0).astype(F32)
        keep_n = (i < nt - 1).astype(F32)
        main = slice(HALO, HALO + tt)
        parts = []
        for j in range(SC_NBLK):
            cw = slice(j * B, (j + 1) * B)

            def ext(k):
                s = slice(k * SC_W + j * B, k * SC_W + (j + 1) * B)
                return s, jnp.concatenate([pp_ref[:, s] * keep_p, p_ref[:, s], pn_ref[:, s]], axis=0)

            (sb, b), (sc, c), (su, u), (sg_, gate) = ext(0), ext(1), ext(2), ext(3)
            dyg_e = jnp.concatenate([jnp.zeros((HALO, B), F32), d_ref[:, cw], dn_ref[:, cw] * keep_n], axis=0)
            w0, w1, w2 = w_ref[0:1, cw], w_ref[1:2, cw], w_ref[2:3, cw]
            z = c * u
            z1, z2 = _down(z, 1), _down(z, 2)
            cz = w2 * z + w1 * z1 + w0 * z2
            sg, dsg = _silu_and_grad(gate)
            dy = dyg_e * sg
            dcz = dy * b
            dz = w2 * dcz + w1 * _up(dcz, 1) + w0 * _up(dcz, 2)
            o_ref[:, sb] = (dy * cz)[main].astype(BF16)
            o_ref[:, sc] = (dz * u)[main].astype(BF16)
            o_ref[:, su] = (dz * c)[main].astype(BF16)
            o_ref[:, sg_] = (dyg_e * (b * cz) * dsg)[main].astype(BF16)
            dcm = dcz[main]
            parts.append(jnp.concatenate([jnp.sum(dcm * z2[main], axis=0, keepdims=True),
                                          jnp.sum(dcm * z1[main], axis=0, keepdims=True),
                                          jnp.sum(dcm * z[main], axis=0, keepdims=True)], axis=0))
        part = jnp.concatenate(parts, axis=1)

        @pl.when(i == 0)
        def _():
            dw_ref[...] = part

        @pl.when(i > 0)
        def _():
            dw_ref[...] += part

    nxt = lambda i: (jnp.minimum((i + 1) * hb, nt * hb - 1), 0)
    return pl.pallas_call(
        body, name=name, grid=(nt,),
        in_specs=[pl.BlockSpec((tt, SC_W), lambda i: (i, 0)),
                  pl.BlockSpec((HALO, SC_W), nxt),
                  pl.BlockSpec((tt, 4 * SC_W), lambda i: (i, 0)),
                  pl.BlockSpec((HALO, 4 * SC_W), lambda i: (jnp.maximum(i * hb - 1, 0), 0)),
                  pl.BlockSpec((HALO, 4 * SC_W), nxt),
                  pl.BlockSpec((SC_CONV, SC_W), lambda i: (0, 0))],
        out_specs=[pl.BlockSpec((tt, 4 * SC_W), lambda i: (i, 0)), pl.BlockSpec((SC_CONV, SC_W), lambda i: (0, 0))],
        out_shape=[jax.ShapeDtypeStruct((T, 4 * SC_W), BF16), jax.ShapeDtypeStruct((SC_CONV, SC_W), F32)],
        compiler_params=_params("arbitrary"),
    )(dyg, dyg, proj, proj, proj, conv_w)


def _split3_dot(x, m):
    hi = x.astype(BF16)
    r1 = x - hi.astype(F32)
    mid = r1.astype(BF16)
    lo = (r1 - mid.astype(F32)).astype(BF16)
    return _dot(hi, m) + _dot(mid, m) + _dot(lo, m)


def _split2_dot(x, m):
    hi = x.astype(BF16)
    lo = (x - hi.astype(F32)).astype(BF16)
    return _dot(hi, m) + _dot(lo, m)


def _head_mean_matrix():
    r, c = _iota2((LANE, LANE), 0), _iota2((LANE, LANE), 1)
    return jnp.where((r // SB_DH) == (c // SB_DH), 1.0 / SB_DH, 0.0).astype(BF16)


def _sb_prep(proj, qg2, kg2, name):
    T = proj.shape[0]
    tt = _tile(T, WIDE_ROW_TILE, SUBLANE)

    def body(p_ref, qg_ref, kg_ref, q_ref, k_ref, v_ref):
        bd = _head_mean_matrix()

        def norm(x, g, scale):
            r = lax.rsqrt(_split3_dot(x * x, bd) + RMS_EPS)
            return (x * r * g * scale).astype(BF16)

        v_ref[...] = p_ref[:, 2 * SB_W:3 * SB_W].astype(BF16)
        for p in range(SB_PAIRS):
            cols = slice(p * LANE, (p + 1) * LANE)
            q_ref[:, cols] = norm(p_ref[:, cols], qg_ref[...], SB_DH ** -0.5)
            k_ref[:, cols] = norm(p_ref[:, SB_W + p * LANE:SB_W + (p + 1) * LANE], kg_ref[...], 1.0)

    blk = pl.BlockSpec((tt, SB_W), lambda i: (i, 0))
    vec = pl.BlockSpec((1, LANE), lambda i: (0, 0))
    return pl.pallas_call(
        body, name=name, grid=(T // tt,),
        in_specs=[pl.BlockSpec((tt, 4 * SB_W), lambda i: (i, 0)), vec, vec],
        out_specs=[blk, blk, blk],
        out_shape=[jax.ShapeDtypeStruct((T, SB_W), BF16)] * 3,
        compiler_params=_params("parallel"),
    )(proj, qg2, kg2)


def _sb_prep_bwd(proj, dqn, dkn, dv, dgate, qg2, kg2, name):
    T = proj.shape[0]
    tt = _tile(T, WIDE_ROW_TILE, SUBLANE)

    def body(p_ref, dq_ref, dk_ref, dv_ref, dg_ref, qg_ref, kg_ref, o_ref, dqg_ref, dkg_ref):
        i = pl.program_id(0)
        bd = _head_mean_matrix()

        def norm_bwd(x, g, dy):
            r = lax.rsqrt(_split3_dot(x * x, bd) + RMS_EPS)
            xh = x * r
            dxh = dy * g
            m = _split3_dot(dxh * xh, bd)
            return r * (dxh - xh * m), jnp.sum(dy * xh, axis=0, keepdims=True)

        o_ref[:, 2 * SB_W:3 * SB_W] = dv_ref[...].astype(BF16)
        o_ref[:, 3 * SB_W:4 * SB_W] = dg_ref[...].astype(BF16)
        pq = pk = jnp.zeros((1, LANE), F32)
        for p in range(SB_PAIRS):
            cols, kcols = slice(p * LANE, (p + 1) * LANE), slice(SB_W + p * LANE, SB_W + (p + 1) * LANE)
            dxq, sq = norm_bwd(p_ref[:, cols], qg_ref[...], dq_ref[:, cols])
            dxk, sk = norm_bwd(p_ref[:, kcols], kg_ref[...], dk_ref[:, cols])
            o_ref[:, cols] = dxq.astype(BF16)
            o_ref[:, kcols] = dxk.astype(BF16)
            pq, pk = pq + sq, pk + sk

        @pl.when(i == 0)
        def _():
            dqg_ref[...] = pq
            dkg_ref[...] = pk

        @pl.when(i > 0)
        def _():
            dqg_ref[...] += pq
            dkg_ref[...] += pk

    blk = pl.BlockSpec((tt, SB_W), lambda i: (i, 0))
    vec = pl.BlockSpec((1, LANE), lambda i: (0, 0))
    wide = pl.BlockSpec((tt, 4 * SB_W), lambda i: (i, 0))
    return pl.pallas_call(
        body, name=name, grid=(T // tt,),
        in_specs=[wide, blk, blk, blk, blk, vec, vec],
        out_specs=[wide, vec, vec],
        out_shape=[jax.ShapeDtypeStruct((T, 4 * SB_W), BF16)] + [jax.ShapeDtypeStruct((1, LANE), F32)] * 2,
        compiler_params=_params("arbitrary"),
    )(proj, dqn, dkn, dv, dgate, qg2, kg2)


def _fold_heads(part, name):
    def body(p_ref, o_ref):
        r, c = _iota2((LANE, SB_DH), 0), _iota2((LANE, SB_DH), 1)
        fold = jnp.where((r % SB_DH) == c, 1.0, 0.0).astype(F32)
        o_ref[...] = jnp.sum(_hdot(p_ref[...], fold), axis=0, keepdims=True)

    return pl.pallas_call(body, name=name, out_shape=jax.ShapeDtypeStruct((1, SB_DH), F32))(part)


def _sb_masks():
    lane = _iota2((1, LANE), 1)
    return lane < SB_DH


def _sb_attn_fwd(qn, kn, vb, proj, name, comm=None):
    T = qn.shape[0]
    tq, tk = _tile(T, SB_TQ, SUBLANE), SB_TK
    assert tq % tk == 0

    def body(q_ref, k_ref, v_ref, g_ref, o_ref, og_ref, lt_ref, done_ref):
        i = pl.program_id(1)
        ma = _sb_masks()
        q2 = q_ref[...]
        zero = jnp.zeros_like(q2)
        qs = (jnp.where(ma, q2, zero), jnp.where(ma, zero, q2))
        upper = (_iota2((tk, tk), 0) > _iota2((tk, tk), 1)).astype(BF16)
        qpos = i * tq + _iota2((tq, tk), 0)
        nb = tq // tk

        def trip(kb_top, masked, carry):
            acc, la, lb = carry
            chains = [(b, h) for b in range(nb) for h in range(2)]
            k2s, vss, masks = [], [], []
            for b in range(nb):
                kb = kb_top - b
                rows = pl.ds(pl.multiple_of(kb * tk, tk), tk)
                k2s.append(k_ref[rows, :])
                v2 = v_ref[rows, :]
                zv = jnp.zeros_like(v2)
                vss.append((jnp.where(ma, v2, zv), jnp.where(ma, zv, v2)))
                masks.append((kb * tk + _iota2((tq, tk), 1)) < qpos if masked else None)
            zs = [_dot(qs[h], k2s[b], NT) for b, h in chains]
            ts = [jnp.log(1.0 + jnp.exp(-jnp.abs(z))) for z in zs]
            ls = [-(jnp.maximum(z, 0.0) + t) for z, t in zip(zs, ts)]
            if masked:
                ls = [jnp.where(masks[b], l, 0.0) for (b, h), l in zip(chains, ls)]
            cums = [_split2_dot(l, upper) for l in ls]
            sums = [jnp.sum(l, axis=1, keepdims=True) for l in ls]
            offs, tot = {}, [la, lb]
            for b in range(nb):
                for h in range(2):
                    offs[(b, h)] = tot[h]
                    tot[h] = tot[h] + sums[chains.index((b, h))]
            ws = [jnp.exp(jnp.minimum(z, 0.0) - t + c + offs[ch]) for ch, z, t, c in zip(chains, zs, ts, cums)]
            if masked:
                ws = [jnp.where(masks[b], w, 0.0) for (b, h), w in zip(chains, ws)]
            for (b, h), w in zip(chains, ws):
                acc = acc + _dot(w.astype(BF16), vss[b][h])
            return acc, tot[0], tot[1]

        def largest(la, lb):
            return jnp.max(jnp.maximum(la, lb))

        z1 = jnp.zeros((tq, 1), F32)
        acc, la, lb = trip((i + 1) * nb - 1, True, (jnp.zeros((tq, LANE), F32), z1, z1))

        def live(c):
            return (c[0] < i) & (c[4] > SB_DEAD)

        def more(c):
            j, acc, la, lb, _ = c
            acc, la, lb = trip((i - j) * nb - 1, False, (acc, la, lb))
            return j + 1, acc, la, lb, largest(la, lb)

        done, acc, la, lb, _ = lax.while_loop(live, more, (jnp.int32(0), acc, la, lb, largest(la, lb)))
        gate = g_ref[...]
        o_ref[...] = acc
        og_ref[...] = (acc * (gate * _sigmoid(gate))).astype(BF16)
        lt_ref[...] = jnp.where(_iota2((tq, 2), 1) == 0, la, lb)
        done_ref[...] = jnp.full((SUBLANE, LANE), done, F32)

    nq = T // tq
    qblk = pl.BlockSpec((tq, LANE), lambda p, i: (i, p))
    full = pl.BlockSpec((T, LANE), lambda p, i: (0, p))
    return _call(
        body, comm, name=name, grid=(SB_PAIRS, nq),
        in_specs=[qblk, full, full, pl.BlockSpec((tq, LANE), lambda p, i: (i, 3 * SB_PAIRS + p))],
        out_specs=[qblk, qblk, pl.BlockSpec((None, tq, 2), lambda p, i: (p, i, 0)),
                   pl.BlockSpec((None, None, SUBLANE, LANE), lambda p, i: (p, i, 0, 0))],
        out_shape=[jax.ShapeDtypeStruct((T, SB_W), F32), jax.ShapeDtypeStruct((T, SB_W), BF16),
                   jax.ShapeDtypeStruct((SB_PAIRS, T, 2), F32), jax.ShapeDtypeStruct((SB_PAIRS, nq, SUBLANE, LANE), F32)],
        scratch_shapes=[], semantics=("parallel", "parallel"), args=(qn, kn, vb, proj))


def _sb_attn_bwd(qn, kn, vb, dog, o, ltot, done, proj, name, comm=None):
    T = qn.shape[0]
    tq, tk = _tile(T, SB_TQ, SUBLANE), SB_TK

    def body(q_ref, k_ref, v_ref, dog_ref, o_ref, lt_ref, done_ref, g_ref, dq_ref, dk_ref, dv_ref, dgate_ref):
        i = pl.program_id(1)
        first_trip = i - jnp.max(done_ref[...]).astype(jnp.int32)

        @pl.when(i == 0)
        def _():
            dk_ref[...] = jnp.zeros_like(dk_ref)
            dv_ref[...] = jnp.zeros_like(dv_ref)

        ma = _sb_masks()
        gate, o2, dog2 = g_ref[...], o_ref[...], dog_ref[...]
        sg, dsg = _silu_and_grad(gate)
        do2 = dog2 * sg
        dgate_ref[...] = dog2 * o2 * dsg
        lt = lt_ref[...]
        first = _iota2((tq, 2), 1) == 0
        ltots = (jnp.sum(jnp.where(first, lt, 0.0), axis=1, keepdims=True),
                 jnp.sum(jnp.where(first, 0.0, lt), axis=1, keepdims=True))
        q2 = q_ref[...]
        zq = jnp.zeros_like(q2)
        qs = (jnp.where(ma, q2, zq), jnp.where(ma, zq, q2))
        dob = do2.astype(BF16)
        dos = (jnp.where(ma, dob, zq), jnp.where(ma, zq, dob))
        upto = (_iota2((tk, tk), 0) <= _iota2((tk, tk), 1)).astype(BF16)
        before = (_iota2((tk, tk), 0) < _iota2((tk, tk), 1)).astype(BF16)
        qpos = i * tq + _iota2((tq, tk), 0)
        nb = tq // tk

        def trip(kb_bot, masked, carry):
            dq, la, lb, ea, eb = carry
            chains = [(b, h) for b in range(nb) for h in range(2)]
            rows, k2s, v2s, kss, masks = [], [], [], [], []
            for b in range(nb):
                kb = kb_bot + b
                rows.append(pl.ds(pl.multiple_of(kb * tk, tk), tk))
                k2 = k_ref[rows[b], :]
                zk = jnp.zeros_like(k2)
                k2s.append(k2)
                v2s.append(v_ref[rows[b], :])
                kss.append((jnp.where(ma, k2, zk), jnp.where(ma, zk, k2)))
                masks.append((kb * tk + _iota2((tq, tk), 1)) < qpos if masked else None)

            def keep(vals):
                return [jnp.where(masks[b], x, 0.0) for (b, h), x in zip(chains, vals)] if masked else vals

            zs = [_dot(qs[h], k2s[b], NT) for b, h in chains]
            dws = [_dot(dos[h], v2s[b], NT) for b, h in chains]
            ts = [jnp.log(1.0 + jnp.exp(-jnp.abs(z))) for z in zs]
            ls = keep([-(jnp.maximum(z, 0.0) + t) for z, t in zip(zs, ts)])
            lps = [jnp.minimum(z, 0.0) - t for z, t in zip(zs, ts)]
            cums = [_split3_dot(l, upto) for l in ls]
            lsums = [jnp.sum(l, axis=1, keepdims=True) for l in ls]
            offs, tot = {}, [la, lb]
            for b in range(nb):
                for h in range(2):
                    offs[(b, h)] = tot[h]
                    tot[h] = tot[h] + lsums[chains.index((b, h))]
            ws = keep([jnp.exp(lp + (ltots[h] - (offs[(b, h)] + c))) for (b, h), lp, c in zip(chains, lps, cums)])
            es = [dw * w for dw, w in zip(dws, ws)]
            ecums = [_split2_dot(e, before) for e in es]
            esums = [jnp.sum(e, axis=1, keepdims=True) for e in es]
            eoffs, etot = {}, [ea, eb]
            for b in range(nb):
                for h in range(2):
                    eoffs[(b, h)] = etot[h]
                    etot[h] = etot[h] + esums[chains.index((b, h))]
            dzs = keep([e - jnp.exp(lp) * (e + eoffs[ch] + ec) for ch, e, lp, ec in zip(chains, es, lps, ecums)])
            dzs = [dz.astype(BF16) for dz in dzs]
            wbs = [w.astype(BF16) for w in ws]
            for (b, h), dz in zip(chains, dzs):
                dq = dq + _dot(dz, kss[b][h])
            for b in range(nb):
                ia, ib = chains.index((b, 0)), chains.index((b, 1))
                dk_ref[rows[b], :] += _dot(dzs[ia], qs[0], TN) + _dot(dzs[ib], qs[1], TN)
                dv_ref[rows[b], :] += _dot(wbs[ia], dos[0], TN) + _dot(wbs[ib], dos[1], TN)
            return dq, tot[0], tot[1], etot[0], etot[1]

        z1 = jnp.zeros((tq, 1), F32)
        carry = lax.fori_loop(first_trip, i, lambda j, c: trip(j * nb, False, c),
                              (jnp.zeros((tq, LANE), F32), z1, z1, z1, z1))
        dq = trip(i * nb, True, carry)[0]
        dq_ref[...] = dq * (SB_DH ** -0.5)

    qblk = pl.BlockSpec((tq, LANE), lambda p, i: (i, p))
    full = pl.BlockSpec((T, LANE), lambda p, i: (0, p))
    return _call(
        body, comm, name=name, grid=(SB_PAIRS, T // tq),
        in_specs=[qblk, full, full, qblk, qblk, pl.BlockSpec((None, tq, 2), lambda p, i: (p, i, 0)),
                  pl.BlockSpec((None, None, SUBLANE, LANE), lambda p, i: (p, i, 0, 0)),
                  pl.BlockSpec((tq, LANE), lambda p, i: (i, 3 * SB_PAIRS + p))],
        out_specs=[qblk, full, full, qblk],
        out_shape=[jax.ShapeDtypeStruct((T, SB_W), F32)] * 4,
        scratch_shapes=[], semantics=("parallel", "arbitrary"), args=(qn, kn, vb, dog, o, ltot, done, proj))


def _dn_conv(ext, w_ref, cw):
    return (w_ref[3:4, cw] * ext + w_ref[2:3, cw] * _down(ext, 1) + w_ref[1:2, cw] * _down(ext, 2)
            + w_ref[0:1, cw] * _down(ext, 3))


def _dn_prep(pqkv, conv_w, name):
    T, W = pqkv.shape
    tt = _tile(T, CONV_ROW_TILE, SUBLANE)
    B = DN_PREP_BLK
    nq, nqk = DN_QK_W // B, 2 * DN_QK_W // B

    def body(p_ref, ph_ref, w_ref, o_ref):
        keep = (pl.program_id(0) > 0).astype(F32)
        for cb in range(W // B):
            cw = slice(cb * B, (cb + 1) * B)
            ext = jnp.concatenate([ph_ref[:, cw] * keep, p_ref[:, cw]], axis=0)
            c = _dn_conv(ext, w_ref, cw)[HALO:]
            a = c * _sigmoid(c)
            if cb >= nqk:
                o_ref[:, cw] = a
                continue
            scale = DN_DK ** -0.5 if cb < nq else 1.0
            for hh in range(B // DN_DK):
                ah = a[:, hh * DN_DK:(hh + 1) * DN_DK]
                r = lax.rsqrt(jnp.sum(ah * ah, axis=-1, keepdims=True) + L2_EPS)
                o_ref[:, cb * B + hh * DN_DK:cb * B + (hh + 1) * DN_DK] = ah * (r * scale)

    return pl.pallas_call(
        body, name=name, grid=(T // tt,),
        in_specs=[pl.BlockSpec((tt, W), lambda i: (i, 0)),
                  pl.BlockSpec((HALO, W), lambda i: (jnp.maximum(i * (tt // HALO) - 1, 0), 0)),
                  pl.BlockSpec((DN_CONV, W), lambda i: (0, 0))],
        out_specs=pl.BlockSpec((tt, W), lambda i: (i, 0)),
        out_shape=jax.ShapeDtypeStruct((T, W), F32),
        compiler_params=_params("parallel"),
    )(pqkv, pqkv, conv_w)


def _dn_prep_bwd(pqkv, conv_w, dact, name):
    T, W = pqkv.shape
    tt = _tile(T, CONV_ROW_TILE, SUBLANE)
    nt = T // tt
    hb = tt // HALO
    B = DN_PREP_BLK
    nq, nqk = DN_QK_W // B, 2 * DN_QK_W // B

    def body(p_ref, pp_ref, pn_ref, w_ref, d_ref, dn_ref, o_ref, dw_ref):
        i = pl.program_id(0)
        keep_p = (i > 0).astype(F32)
        keep_n = (i < nt - 1).astype(F32)
        main = slice(HALO, HALO + tt)
        parts = []
        for cb in range(W // B):
            cw = slice(cb * B, (cb + 1) * B)
            ext = jnp.concatenate([pp_ref[:, cw] * keep_p, p_ref[:, cw], pn_ref[:, cw]], axis=0)
            c = _dn_conv(ext, w_ref, cw)
            s = _sigmoid(c)
            da_dc = s * (1.0 + c * (1.0 - s))
            d_up = jnp.concatenate([jnp.zeros((HALO, B), F32), d_ref[:, cw], dn_ref[:, cw] * keep_n], axis=0)
            if cb < nqk:
                a = c * s
                scale = DN_DK ** -0.5 if cb < nq else 1.0
                normed = []
                for hh in range(B // DN_DK):
                    cols = slice(hh * DN_DK, (hh + 1) * DN_DK)
                    ah = a[:, cols]
                    r = lax.rsqrt(jnp.sum(ah * ah, axis=-1, keepdims=True) + L2_EPS)
                    y = ah * r
                    dy = d_up[:, cols] * scale
                    normed.append(r * (dy - y * jnp.sum(dy * y, axis=-1, keepdims=True)))
                d_up = jnp.concatenate(normed, axis=1)
            dc = d_up * da_dc
            dp = (w_ref[3:4, cw] * dc + w_ref[2:3, cw] * _up(dc, 1) + w_ref[1:2, cw] * _up(dc, 2)
                  + w_ref[0:1, cw] * _up(dc, 3))
            o_ref[:, cw] = dp[main].astype(BF16)
            dcm = dc[main]
            parts.append(jnp.concatenate([jnp.sum(dcm * _down(ext, 3 - k)[main], axis=0, keepdims=True)
                                          for k in range(DN_CONV)], axis=0))
        part = jnp.concatenate(parts, axis=1)

        @pl.when(i == 0)
        def _():
            dw_ref[...] = part

        @pl.when(i > 0)
        def _():
            dw_ref[...] += part

    main_spec = pl.BlockSpec((tt, W), lambda i: (i, 0))
    prev_spec = pl.BlockSpec((HALO, W), lambda i: (jnp.maximum(i * hb - 1, 0), 0))
    next_spec = pl.BlockSpec((HALO, W), lambda i: (jnp.minimum((i + 1) * hb, nt * hb - 1), 0))
    w_spec = pl.BlockSpec((DN_CONV, W), lambda i: (0, 0))
    return pl.pallas_call(
        body, name=name, grid=(nt,),
        in_specs=[main_spec, prev_spec, next_spec, w_spec, main_spec, next_spec],
        out_specs=[main_spec, w_spec],
        out_shape=[jax.ShapeDtypeStruct((T, W), BF16), jax.ShapeDtypeStruct((DN_CONV, W), F32)],
        compiler_params=_params("arbitrary"),
    )(pqkv, pqkv, pqkv, conv_w, dact, dact)


def _dn_gates(a_in, b_in, a_log, dt_bias, name):
    T, H = a_in.shape
    C = DN_CHUNK

    def body(a_ref, b_ref, al_ref, dt_ref, g_ref, beta_ref):
        beta_ref[...] = _sigmoid(b_ref[...])
        g_ref[...] = -jnp.exp(al_ref[...]) * _softplus(a_ref[...] + dt_ref[...])
        tri = (_iota2((C, C), 0) >= _iota2((C, C), 1)).astype(F32)

        def chunk(n, carry):
            rows = pl.ds(pl.multiple_of(n * C, C), C)
            g_ref[rows, :] = _hdot(tri, g_ref[rows, :])
            return carry

        lax.fori_loop(0, T // C, chunk, 0)

    return pl.pallas_call(body, name=name, out_shape=[jax.ShapeDtypeStruct((T, H), F32)] * 2)(a_in, b_in, a_log, dt_bias)


def _dn_gates_bwd(dg, dbeta, a_in, b_in, a_log, dt_bias, name):
    T, H = a_in.shape
    C = DN_CHUNK

    def body(dg_ref, db_ref, a_ref, b_ref, al_ref, dt_ref, da_ref, dbi_ref, dal_ref, ddt_ref):
        tri_t = (_iota2((C, C), 0) <= _iota2((C, C), 1)).astype(F32)

        def chunk(n, carry):
            rows = pl.ds(pl.multiple_of(n * C, C), C)
            da_ref[rows, :] = _hdot(tri_t, dg_ref[rows, :])
            return carry

        lax.fori_loop(0, T // C, chunk, 0)
        dla = da_ref[...]
        x = a_ref[...] + dt_ref[...]
        ea = jnp.exp(al_ref[...])
        da = dla * (-ea) * _sigmoid(x)
        da_ref[...] = da
        dal_ref[...] = jnp.sum(dla * (-ea * _softplus(x)), axis=0, keepdims=True)
        ddt_ref[...] = jnp.sum(da, axis=0, keepdims=True)
        beta = _sigmoid(b_ref[...])
        dbi_ref[...] = db_ref[...] * beta * (1.0 - beta)

    return pl.pallas_call(
        body, name=name,
        out_shape=[jax.ShapeDtypeStruct((T, H), F32)] * 2 + [jax.ShapeDtypeStruct((1, H), F32)] * 2,
    )(dg, dbeta, a_in, b_in, a_log, dt_bias)


def _dn_chunk_terms(q, k, gc, bc):
    C = DN_CHUNK
    r, c = _iota2((C, C), 0), _iota2((C, C), 1)
    lower, strict, eye = r >= c, r > c, r == c
    grow = jnp.sum(jnp.where(eye, gc, 0.0), axis=0, keepdims=True)
    decay = jnp.where(lower, jnp.exp(jnp.where(lower, gc - grow, 0.0)), 0.0)
    last = _iota2((C, 1), 0) == C - 1
    gl = jnp.sum(jnp.where(last, gc, 0.0), axis=0, keepdims=True)
    eg = jnp.exp(gc)
    egl = jnp.exp(gl - gc)
    kb = k * bc
    lmat = jnp.where(strict, _bdot(kb, k, NT) * decay, 0.0)
    aqk = jnp.where(lower, _bdot(q, k, NT) * decay, 0.0)
    return dict(lower=lower, strict=strict, eye=eye, last=last, decay=decay, gl=gl, eg=eg, egl=egl, kb=kb,
                lmat=lmat, aqk=aqk, qd=q * eg, kd=k * egl)


def _split(x):
    hi = x.astype(BF16)
    return hi, (x - hi.astype(F32)).astype(BF16)


def _x3dot(a, b, dims=NN):
    ah, al = a if isinstance(a, tuple) else _split(a)
    bh, bl = b if isinstance(b, tuple) else _split(b)
    return _dot(ah, bh, dims) + (_dot(ah, bl, dims) + _dot(al, bh, dims))


def _interleave(gens):
    for _ in itertools.zip_longest(*gens):
        pass


def _unit_lower_inverse_steps(lmat, eye, out):
    ident = jnp.where(eye, 1.0, 0.0).astype(F32)
    m = -lmat
    inv = ident + m
    for _ in range(int(math.log2(DN_CHUNK)) - 1):
        ms = _split(m)
        m = _x3dot(ms, ms)
        yield
        inv = inv + _x3dot(inv, m)
        yield
    out["tm"] = inv


def _dn_chunk_fwd(act, g, beta, pgate, gn, name, comm=None):
    T = act.shape[0]
    C, H = DN_CHUNK, DN_HEADS
    N = T // C

    def body(a_ref, g_ref, b_ref, pg_ref, gn_ref, o_ref, og_ref, s_out, t_out, vn_out, u_out, w_out, s_scr):
        n = pl.program_id(0)

        @pl.when(n == 0)
        def _():
            s_scr[...] = jnp.zeros_like(s_scr)

        head_lane = _iota2((C, H), 1)

        def head(hh):
            qs, vs = slice(hh * DN_DK, (hh + 1) * DN_DK), slice(hh * DN_DV, (hh + 1) * DN_DV)
            q, k, v = a_ref[:, qs], a_ref[:, DN_QK_W + hh * DN_DK:DN_QK_W + (hh + 1) * DN_DK], \
                a_ref[:, 2 * DN_QK_W + hh * DN_DV:2 * DN_QK_W + (hh + 1) * DN_DV]
            gc = jnp.sum(jnp.where(head_lane == hh, g_ref[...], 0.0), axis=1, keepdims=True)
            bc = jnp.sum(jnp.where(head_lane == hh, b_ref[...], 0.0), axis=1, keepdims=True)
            t = _dn_chunk_terms(q, k, gc, bc)
            yield
            res = {}
            yield from _unit_lower_inverse_steps(t["lmat"], t["eye"], res)
            tms = _split(res["tm"])
            u = _x3dot(tms, v * bc)
            yield
            w = _x3dot(tms, t["kb"] * t["eg"])
            yield
            s = s_scr[hh]
            s_out[hh] = s
            t_out[hh] = res["tm"]
            sb = s.astype(BF16)
            vn = u - _dot(w.astype(BF16), sb)
            yield
            o = _dot(t["qd"].astype(BF16), sb) + _bdot(t["aqk"], vn)
            yield
            s_scr[hh] = s * jnp.exp(t["gl"]) + _bdot(t["kd"], vn, TN)
            vn_out[:, vs] = vn
            u_out[:, vs] = u
            w_out[:, qs] = w
            o_ref[:, vs] = o
            gate = pg_ref[:, vs]
            r = lax.rsqrt(jnp.mean(o * o, axis=-1, keepdims=True) + RMS_EPS)
            og_ref[:, vs] = (o * r * gn_ref[...] * (gate * _sigmoid(gate))).astype(BF16)

        _interleave([head(hh) for hh in range(H)])

    row = lambda w: pl.BlockSpec((C, w), lambda n: (n, 0))
    return _call(
        body, comm, name=name, grid=(N,),
        in_specs=[row(DN_CONV_W), row(H), row(H), row(DN_V_W), pl.BlockSpec((1, DN_DV), lambda n: (0, 0))],
        out_specs=[row(DN_V_W), row(DN_V_W),
                   pl.BlockSpec((H, None, DN_DK, DN_DV), lambda n: (0, n, 0, 0)),
                   pl.BlockSpec((H, None, C, C), lambda n: (0, n, 0, 0)),
                   row(DN_V_W), row(DN_V_W), row(DN_QK_W)],
        out_shape=[jax.ShapeDtypeStruct((T, DN_V_W), F32), jax.ShapeDtypeStruct((T, DN_V_W), BF16),
                   jax.ShapeDtypeStruct((H, N, DN_DK, DN_DV), F32),
                   jax.ShapeDtypeStruct((H, N, C, C), F32),
                   jax.ShapeDtypeStruct((T, DN_V_W), F32),
                   jax.ShapeDtypeStruct((T, DN_V_W), F32),
                   jax.ShapeDtypeStruct((T, DN_QK_W), F32)],
        scratch_shapes=[pltpu.VMEM((H, DN_DK, DN_DV), F32)], semantics=("arbitrary",), args=(act, g, beta, pgate, gn))


def _dn_chunk_bwd(act, g, beta, s_saved, tm_saved, vn_saved, u_saved, w_saved, dog, o_raw, pgate, gn, name, comm=None):
    T = act.shape[0]
    C, H = DN_CHUNK, DN_HEADS
    N = T // C

    def body(a_ref, g_ref, b_ref, s_ref, t_ref, vn_ref, u_ref, w_ref, dog_ref, o_ref, pg_ref, gn_ref,
             da_ref, dg_ref, db_ref, dgate_ref, dgn_ref, ds_scr):
        @pl.when(pl.program_id(0) == 0)
        def _():
            ds_scr[...] = jnp.zeros_like(ds_scr)

        head_lane = _iota2((C, H), 1)
        dg_cols, db_cols, dgn_parts = {}, {}, {}

        def output_gate_bwd(hh, vs):
            d, o, gate, gn_v = dog_ref[:, vs], o_ref[:, vs], pg_ref[:, vs], gn_ref[...]
            sg, dsg = _silu_and_grad(gate)
            r = lax.rsqrt(jnp.mean(o * o, axis=-1, keepdims=True) + RMS_EPS)
            n = o * r
            dy = d * sg
            dgate_ref[:, vs] = (d * (n * gn_v) * dsg).astype(BF16)
            dn = dy * gn_v
            dgn_parts[hh] = jnp.sum(dy * n, axis=0, keepdims=True)
            return r * (dn - n * jnp.mean(dn * n, axis=-1, keepdims=True))

        def head(hh):
            qs, vs = slice(hh * DN_DK, (hh + 1) * DN_DK), slice(hh * DN_DV, (hh + 1) * DN_DV)
            ks = slice(DN_QK_W + hh * DN_DK, DN_QK_W + (hh + 1) * DN_DK)
            vas = slice(2 * DN_QK_W + hh * DN_DV, 2 * DN_QK_W + (hh + 1) * DN_DV)
            q, k, v = a_ref[:, qs], a_ref[:, ks], a_ref[:, vas]
            gc = jnp.sum(jnp.where(head_lane == hh, g_ref[...], 0.0), axis=1, keepdims=True)
            bc = jnp.sum(jnp.where(head_lane == hh, b_ref[...], 0.0), axis=1, keepdims=True)
            t = _dn_chunk_terms(q, k, gc, bc)
            yield
            lower, strict, eye = t["lower"], t["strict"], t["eye"]
            decay, eg, egl, kb, qd, kd = t["decay"], t["eg"], t["egl"], t["kb"], t["qd"], t["kd"]
            s, tm, vn, u, w = s_ref[hh], t_ref[hh], vn_ref[:, vs], u_ref[:, vs], w_ref[:, qs]
            d_o = output_gate_bwd(hh, vs)
            ds_next = ds_scr[hh]
            egl_tot = jnp.exp(t["gl"])
            dob, sb, dsb, vnb = d_o.astype(BF16), s.astype(BF16), ds_next.astype(BF16), vn.astype(BF16)

            dvn = _bdot(t["aqk"], dob, TN) + _bdot(kd, dsb)
            yield
            daqk = jnp.where(lower, _dot(dob, vnb, NT), 0.0)
            dqd = _dot(dob, sb, NT)
            dkd = _dot(vnb, dsb, NT)
            yield
            dvnb = dvn.astype(BF16)
            ds_scr[hh] = _bdot(qd, dob, TN) + egl_tot * ds_next - _bdot(w, dvnb, TN)
            dgl = egl_tot * jnp.sum(jnp.sum(s * ds_next, axis=1, keepdims=True), axis=0, keepdims=True)
            dw = -_dot(dvnb, sb, NT)
            yield
            tms = _split(tm)
            dru = _x3dot(tms, dvn, TN)
            drw = _x3dot(tms, dw, TN)
            yield
            dl = -jnp.where(strict, _x3dot(dru, u, NT) + _x3dot(drw, w, NT), 0.0)
            yield
            dkk = (dl * decay).astype(BF16)
            dqk = (daqk * decay).astype(BF16)
            dkb = _bdot(dkk, k) + drw * eg
            yield
            da_ref[:, ks] = _bdot(dkk, kb, TN) + _bdot(dqk, q, TN) + dkd * egl + dkb * bc
            da_ref[:, qs] = _bdot(dqk, k) + dqd * eg
            da_ref[:, vas] = dru * bc
            yield
            db_cols[hh] = jnp.sum(dru * v, axis=1, keepdims=True) + jnp.sum(dkb * k, axis=1, keepdims=True)
            pm = dl * t["lmat"] + daqk * t["aqk"]
            col_as_col = jnp.sum(jnp.where(eye, jnp.sum(pm, axis=0, keepdims=True), 0.0), axis=1, keepdims=True)
            kdsum = jnp.sum(dkd * kd, axis=1, keepdims=True)
            dgc = (jnp.sum(pm, axis=1, keepdims=True) - col_as_col + jnp.sum(dqd * qd, axis=1, keepdims=True)
                   - kdsum + jnp.sum(drw * (kb * eg), axis=1, keepdims=True))
            dgl = dgl + jnp.sum(kdsum, axis=0, keepdims=True)
            dg_cols[hh] = dgc + jnp.where(t["last"], dgl, 0.0)

        _interleave([head(hh) for hh in range(H)])
        dg_ref[...] = sum(jnp.where(head_lane == hh, dg_cols[hh], 0.0) for hh in range(H))
        db_ref[...] = sum(jnp.where(head_lane == hh, db_cols[hh], 0.0) for hh in range(H))
        dgn_part = sum(dgn_parts[hh] for hh in range(H))

        @pl.when(pl.program_id(0) == 0)
        def _():
            dgn_ref[...] = dgn_part

        @pl.when(pl.program_id(0) > 0)
        def _():
            dgn_ref[...] += dgn_part

    row = lambda w: pl.BlockSpec((C, w), lambda n: (N - 1 - n, 0))
    vec = pl.BlockSpec((1, DN_DV), lambda n: (0, 0))
    return _call(
        body, comm, name=name, grid=(N,),
        in_specs=[row(DN_CONV_W), row(H), row(H),
                  pl.BlockSpec((H, None, DN_DK, DN_DV), lambda n: (0, N - 1 - n, 0, 0)),
                  pl.BlockSpec((H, None, C, C), lambda n: (0, N - 1 - n, 0, 0)),
                  row(DN_V_W), row(DN_V_W), row(DN_QK_W), row(DN_V_W), row(DN_V_W), row(DN_V_W), vec],
        out_specs=[row(DN_CONV_W), row(H), row(H), row(DN_V_W), vec],
        out_shape=[jax.ShapeDtypeStruct((T, DN_CONV_W), F32),
                   jax.ShapeDtypeStruct((T, H), F32), jax.ShapeDtypeStruct((T, H), F32),
                   jax.ShapeDtypeStruct((T, DN_V_W), BF16), jax.ShapeDtypeStruct((1, DN_DV), F32)],
        scratch_shapes=[pltpu.VMEM((H, DN_DK, DN_DV), F32)], semantics=("arbitrary",),
        args=(act, g, beta, s_saved, tm_saved, vn_saved, u_saved, w_saved, dog, o_raw, pgate, gn))


def _dn_split_w_in(w):
    return w, jnp.pad(w[:, DN_CONV_W + DN_V_W:], ((0, 0), (0, DN_AB_PAD - 2 * DN_HEADS)))


def _out_proj(og, w_out, x_res, next_g, name):
    if next_g is None:
        return _matmul(og, w_out, "nn", name, add=x_res), None
    return tuple(_matmul(og, w_out, "nn", name, add=x_res, norm_fwd=next_g, tm=NORM_FUSED_TM))


def _dn_layer_fwd(h, wts, conv_w, a_log, dt_bias, gn, w_out, x_res, tag, comm=None, next_g=None):
    w_in, wab = wts
    H = DN_HEADS
    pqkv = _matmul(h, w_in, "nn", tag + "_pqkv", b_cols=(0, DN_CONV_W))
    pgate = _matmul(h, w_in, "nn", tag + "_pgate", b_cols=(DN_CONV_W, DN_V_W))
    pab = _matmul(h, wab, "nn", tag + "_pab")
    a_in, b_in = pab[:, :H], pab[:, H:2 * H]
    g, beta = _dn_gates(a_in, b_in, a_log, dt_bias, tag + "_gates")
    act = _dn_prep(pqkv, conv_w, tag + "_prep")
    (o_raw, og, s_sv, tm_sv, vn_sv, u_sv, w_sv), landed = _dn_chunk_fwd(act, g, beta, pgate, gn, tag + "_chunk_fwd", comm)
    if callable(w_out):
        w_out = w_out(landed)
    y = _out_proj(og, w_out, x_res, next_g, tag + "_out")
    saved = dict(h=h, wts=wts, conv_w=conv_w, a_log=a_log, dt_bias=dt_bias, gn=gn, w_out=w_out, pqkv=pqkv, pgate=pgate,
                 a_in=a_in, b_in=b_in, g=g, beta=beta, act=act, o_raw=o_raw, chunk=(s_sv, tm_sv, vn_sv, u_sv, w_sv), og=og)
    return y, saved, landed


def _dn_layer_bwd(dout, sv, tag, norm, comm_of=None, late_comm_of=None):
    w_in, wab = sv["wts"]
    h = sv["h"]
    dog = _matmul(dout, sv["w_out"], "nt", tag + "_dog")
    dw_out = _matmul(sv["og"], dout, "tn", tag + "_dwout", out_dtype=BF16)
    comm = comm_of(dw_out) if comm_of is not None else None
    (dact, dg, dbeta, dgate, dgn), landed = _dn_chunk_bwd(sv["act"], sv["g"], sv["beta"], *sv["chunk"], dog, sv["o_raw"],
                                                          sv["pgate"], sv["gn"], tag + "_chunk_bwd", comm)
    da_in, db_in, da_log, ddt = _dn_gates_bwd(dg, dbeta, sv["a_in"], sv["b_in"], sv["a_log"], sv["dt_bias"],
                                              tag + "_gates_bwd")
    dpqkv, dconv = _dn_prep_bwd(sv["pqkv"], sv["conv_w"], dact, tag + "_prep_bwd")
    dpab = jnp.pad(jnp.concatenate([da_in, db_in], axis=1), ((0, 0), (0, DN_AB_PAD - 2 * DN_HEADS)))
    dwqkv = _matmul(h, dpqkv, "tn", tag + "_dwqkv", out_dtype=BF16)
    dwgate = _matmul(h, dgate, "tn", tag + "_dwgate", out_dtype=BF16)
    dwab = _matmul(h, dpab, "tn", tag + "_dwab", out_dtype=BF16)
    dw_in = jnp.concatenate([dwqkv, dwgate, dwab[:, :2 * DN_HEADS]], axis=1)
    grads = dict(dn_w_in=dw_in, dn_conv_w=dconv, dn_a_log=da_log, dn_dt_bias=ddt, dn_o_norm_g=dgn, dn_w_out=dw_out)
    dx, landed_late = _matmul_nt_sum([(dpqkv, w_in, 0), (dgate, w_in, DN_CONV_W), (dpab, wab, 0)], tag + "_dh",
                                     late_comm_of(grads) if late_comm_of is not None else None, norm_bwd=norm)
    return dx, grads, landed, landed_late


def _sb_layer_fwd(h, w_in, qg, kg, w_out, x_res, tag, comm=None, next_g=None):
    qg2, kg2 = jnp.tile(qg, (1, 2)), jnp.tile(kg, (1, 2))
    proj = _matmul(h, w_in, "nn", tag + "_proj", blocked_b=True)
    qn, kn, vb = _sb_prep(proj, qg2, kg2, tag + "_prep")
    (o, og, ltot, done), landed = _sb_attn_fwd(qn, kn, vb, proj, tag + "_attn_fwd", comm)
    y = _out_proj(og, w_out, x_res, next_g, tag + "_out")
    saved = dict(h=h, w_in=w_in, qg2=qg2, kg2=kg2, w_out=w_out, proj=proj, qn=qn, kn=kn, vb=vb, o=o, og=og, ltot=ltot,
                 done=done)
    return y, saved, landed


def _sb_layer_bwd(dout, sv, tag, norm, comm=None):
    dog = _matmul(dout, sv["w_out"], "nt", tag + "_dog")
    dw_out = _matmul(sv["og"], dout, "tn", tag + "_dwout", out_dtype=BF16)
    (dqn, dkn, dv, dgate), landed = _sb_attn_bwd(sv["qn"], sv["kn"], sv["vb"], dog, sv["o"], sv["ltot"], sv["done"],
                                                 sv["proj"], tag + "_attn_bwd", comm)
    dproj, dqgp, dkgp = _sb_prep_bwd(sv["proj"], dqn, dkn, dv, dgate, sv["qg2"], sv["kg2"], tag + "_prep_bwd")
    dw_in = _matmul(sv["h"], dproj, "tn", tag + "_dwin", out_dtype=BF16, blocked_out=N_DEV)
    dx = _matmul(dproj, sv["w_in"], "nt", tag + "_dh", blocked_b=True, norm_bwd=norm, tm=NORM_FUSED_TM)
    dqg = _fold_heads(dqgp, tag + "_dqg")
    dkg = _fold_heads(dkgp, tag + "_dkg")
    return dx, dict(sb_w_in=dw_in, sb_q_norm_g=dqg, sb_k_norm_g=dkg, sb_w_out=dw_out), landed


def _sc_layer_fwd(h, w_in, conv_w, w_out, x_res, tag, next_g=None):
    proj = _matmul(h, w_in, "nn", tag + "_proj", blocked_b=True)
    yg = _sc_fwd(proj, conv_w, tag + "_fwd")
    y = _out_proj(yg, w_out, x_res, next_g, tag + "_out")
    return y, dict(h=h, w_in=w_in, conv_w=conv_w, w_out=w_out, proj=proj, yg=yg)


def _sc_layer_bwd(dout, sv, tag, norm):
    dyg = _matmul(dout, sv["w_out"], "nt", tag + "_dyg")
    dw_out = _matmul(sv["yg"], dout, "tn", tag + "_dwout", out_dtype=BF16)
    dproj, dconv = _sc_bwd(dyg, sv["proj"], sv["conv_w"], tag + "_bwd")
    dw_in = _matmul(sv["h"], dproj, "tn", tag + "_dwin", out_dtype=BF16, blocked_out=N_DEV)
    dx = _matmul(dproj, sv["w_in"], "nt", tag + "_dh", blocked_b=True, norm_bwd=norm, tm=NORM_FUSED_TM)
    return dx, dict(sc_w_in=dw_in, sc_conv_w=dconv, sc_w_out=dw_out)


def _adamw(w, m, v, parts, name):
    R, C = w.shape
    tr = _tile(R, 128, SUBLANE)

    def body(w_ref, m_ref, v_ref, p_ref, g_ref, d_ref, nm_ref, nv_ref):
        g = p_ref[0].astype(F32)
        for s in range(1, N_DEV):
            g = g + p_ref[s].astype(F32)
        m2 = ADAM_B1 * m_ref[...] + (1.0 - ADAM_B1) * g
        v2 = ADAM_B2 * v_ref[...] + (1.0 - ADAM_B2) * (g * g)
        m_hat = m2 / (1.0 - ADAM_B1 ** ADAM_STEP)
        v_hat = v2 / (1.0 - ADAM_B2 ** ADAM_STEP)
        g_ref[...] = g
        d_ref[...] = -ADAM_LR * (m_hat / (jnp.sqrt(v_hat) + ADAM_EPS) + ADAM_WD * w_ref[...])
        nm_ref[...] = m2
        nv_ref[...] = v2

    blk = pl.BlockSpec((tr, C), lambda i: (i, 0))
    return pl.pallas_call(
        body, name=name, grid=(R // tr,),
        in_specs=[blk, blk, blk, pl.BlockSpec((N_DEV, tr, C), lambda i: (0, i, 0))],
        out_specs=[blk] * 4, out_shape=[jax.ShapeDtypeStruct((R, C), F32)] * 4,
        compiler_params=_params("parallel"),
    )(w, m, v, parts)


_HBM = pl.BlockSpec(memory_space=pltpu.HBM)
_MESH = pl.DeviceIdType.MESH


def _slot(x, y, c):
    return 4 * x + 2 * y + c


class _Gather:
    def __init__(self, shards):
        self.arrays = list(shards)
        n = len(self.arrays)
        self.out_shapes = [jax.ShapeDtypeStruct((N_DEV,) + s.shape, s.dtype) for s in self.arrays]
        self.scratch = [pltpu.SemaphoreType.DMA((n, N_DEV - 1)), pltpu.SemaphoreType.DMA((n, N_DEV - 1)),
                        pltpu.SemaphoreType.DMA((n,))]

    def _parts(self, ins, outs, sems):
        send_sems, recv_sems, local_sems = sems
        n = len(self.arrays)
        x, y, c = lax.axis_index("x"), lax.axis_index("y"), lax.axis_index("c")
        me, sibling = (x, y, c), (x, y, 1 - c)
        chips = [(1 - x, y), (x, 1 - y), (1 - x, 1 - y)]

        def copy(a, k, block, to, src=None):
            dst = outs[a].at[_slot(*block)]
            return pltpu.make_async_remote_copy(src_ref=dst if src is None else src, dst_ref=dst,
                                                send_sem=send_sems.at[a, k], recv_sem=recv_sems.at[a, k],
                                                device_id=to, device_id_type=_MESH)

        mine = [pltpu.make_async_copy(ins[a], outs[a].at[_slot(*me)], local_sems.at[a]) for a in range(n)]
        first = []
        for a in range(n):
            first.append(copy(a, 0, me, sibling, src=ins[a]))
            first += [copy(a, 1 + j, me, (*chip, c), src=ins[a]) for j, chip in enumerate(chips)]
        return n, c, me, sibling, chips, copy, mine, first

    def start(self, ins, outs, sems):
        _, _, _, _, _, _, mine, first = self._parts(ins, outs, sems)
        for cp in mine + first:
            cp.start()

    def finish(self, ins, outs, sems):
        n, c, me, sibling, chips, copy, mine, first = self._parts(ins, outs, sems)
        passed = []
        for j, chip in enumerate(chips):
            for a in range(n):
                copy(a, 1 + j, (*chip, c), me).wait_recv()
                fwd = copy(a, 4 + j, (*chip, c), sibling)
                fwd.start()
                passed.append(fwd)
        for a in range(n):
            copy(a, 0, sibling, me).wait_recv()
            for j, chip in enumerate(chips):
                copy(a, 4 + j, (*chip, 1 - c), me).wait_recv()
        for cp in first + passed:
            cp.wait_send()
        for cp in mine:
            cp.wait()


class _Exchange:
    def __init__(self, arrays, scatter):
        self.arrays, self.scatter = list(arrays), list(scatter)
        n = len(self.arrays)
        shapes = [a.shape[1:] if s else a.shape for a, s in zip(self.arrays, self.scatter)]
        self.out_shapes = [jax.ShapeDtypeStruct((N_DEV,) + tuple(s), a.dtype) for s, a in zip(shapes, self.arrays)]
        self.scratch = [pltpu.SemaphoreType.DMA((n, N_DEV - 1)), pltpu.SemaphoreType.DMA((n, N_DEV - 1)),
                        pltpu.SemaphoreType.DMA((n,))]

    def _copies(self, ins, outs, sems):
        send_sems, recv_sems, local_sems = sems
        n, scatter = len(self.arrays), self.scatter
        x, y, c = lax.axis_index("x"), lax.axis_index("y"), lax.axis_index("c")
        me = _slot(x, y, c)
        copies = [pltpu.make_async_copy(ins[a].at[me] if scatter[a] else ins[a], outs[a].at[me], local_sems.at[a])
                  for a in range(n)]
        for r in range(1, N_DEV):
            px = 1 - x if r & 4 else x
            py = 1 - y if r & 2 else y
            pc = 1 - c if r & 1 else c
            for a in range(n):
                copies.append(pltpu.make_async_remote_copy(
                    src_ref=ins[a].at[_slot(px, py, pc)] if scatter[a] else ins[a], dst_ref=outs[a].at[me],
                    send_sem=send_sems.at[a, r - 1], recv_sem=recv_sems.at[a, r - 1],
                    device_id=(px, py, pc), device_id_type=_MESH))
        return copies

    def start(self, ins, outs, sems):
        for cp in self._copies(ins, outs, sems):
            cp.start()

    def finish(self, ins, outs, sems):
        for cp in self._copies(ins, outs, sems):
            cp.wait()


def _comm_call(comm, name):
    n = len(comm.arrays)

    def body(*refs):
        ins, outs, sems = refs[:n], refs[n:2 * n], refs[2 * n:]
        comm.start(ins, outs, sems)
        comm.finish(ins, outs, sems)

    return pl.pallas_call(body, name=name, in_specs=[_HBM] * n, out_specs=[_HBM] * n, out_shape=comm.out_shapes,
                          scratch_shapes=comm.scratch)(*comm.arrays)


def _call(body, comm, *, name, grid, in_specs, out_specs, out_shape, scratch_shapes, semantics, args):
    if comm is None:
        outs = pl.pallas_call(body, name=name, grid=grid, in_specs=in_specs, out_specs=out_specs, out_shape=out_shape,
                              scratch_shapes=scratch_shapes, compiler_params=_params(*semantics))(*args)
        return outs, []
    n_in, n_out, n_scr, n_c = len(in_specs), len(out_specs), len(scratch_shapes), len(comm.arrays)

    def fused(*refs):
        ins, refs = refs[:n_in], refs[n_in:]
        c_ins, refs = refs[:n_c], refs[n_c:]
        outs, refs = refs[:n_out], refs[n_out:]
        c_outs, refs = refs[:n_c], refs[n_c:]
        scr, sems = refs[:n_scr], refs[n_scr:]
        ids = [pl.program_id(d) for d in range(len(grid))]
        first = functools.reduce(jnp.logical_and, [i == 0 for i in ids])
        last = functools.reduce(jnp.logical_and, [i == g - 1 for i, g in zip(ids, grid)])

        @pl.when(first)
        def _():
            comm.start(c_ins, c_outs, sems)

        body(*ins, *outs, *scr)

        @pl.when(last)
        def _():
            comm.finish(c_ins, c_outs, sems)

    outs = pl.pallas_call(
        fused, name=name, grid=grid, in_specs=list(in_specs) + [_HBM] * n_c, out_specs=list(out_specs) + [_HBM] * n_c,
        out_shape=list(out_shape) + comm.out_shapes, scratch_shapes=list(scratch_shapes) + comm.scratch,
        compiler_params=_params(*["arbitrary"] * len(grid)))(*args, *comm.arrays)
    return outs[:n_out], outs[n_out:]


_GATHER_0 = (("dn_w_in", 0), ("dn_conv_w", 0), ("dn_o_norm_g", 0))
_GATHER_1 = (("dn_w_out", 0), ("sb_w_in", 0), ("sb_w_out", 0))
_GATHER_2 = (("sc_w_in", 0), ("sc_conv_w", 0), ("sc_w_out", 0), ("dn_w_in", 1), ("dn_conv_w", 1), ("dn_o_norm_g", 1),
             ("dn_w_out", 1))
_EXCHANGE_A = _GATHER_2
_EXCHANGE_B = (("sb_w_in", 0), ("sb_w_out", 0), ("dn_w_out", 0))
_EXCHANGE_C = _GATHER_0
_MATMUL_WEIGHTS = ("dn_w_in", "dn_w_out", "sb_w_in", "sb_w_out", "sc_w_in", "sc_w_out")
_COLUMN_SHARDED = ("dn_w_in", "dn_conv_w", "dn_o_norm_g", "sb_w_in", "sc_w_in", "sc_conv_w")
_BLOCKED = ("sb_w_in", "sc_w_in")
_REPLICATED = ("norm_g", "dn_a_log", "dn_dt_bias", "sb_q_norm_g", "sb_k_norm_g")
_ORDER = ("norm_g", "dn_w_in", "dn_conv_w", "dn_a_log", "dn_dt_bias", "dn_o_norm_g", "dn_w_out", "sb_w_in", "sb_q_norm_g",
          "sb_k_norm_g", "sb_w_out", "sc_w_in", "sc_conv_w", "sc_w_out")
_PACK_COLS = D_MODEL


def _as_2d(a):
    return a.reshape(1, -1) if a.ndim == 1 else a


def _assemble(name, gathered):
    n, r, c = gathered.shape
    if name in _COLUMN_SHARDED:
        return jnp.moveaxis(gathered, 0, 1).reshape(r, n * c)
    return gathered.reshape(n * r, c)


def _disassemble(name, full):
    r, c = full.shape
    if name in _COLUMN_SHARDED:
        return jnp.moveaxis(full.reshape(r, N_DEV, c // N_DEV), 1, 0)
    return full.reshape(N_DEV, r // N_DEV, c)


def _pack_replicated(d):
    rows = [d["norm_g"]]
    for name in _REPLICATED[1:]:
        flat = d[name].reshape(1, -1)
        rows.append(jnp.pad(flat, ((0, 0), (0, _PACK_COLS - flat.shape[1]))))
    return jnp.concatenate(rows, axis=0)


def _unpack_replicated(p, like):
    out = {"norm_g": p[:4]}
    for r, name in enumerate(_REPLICATED[1:]):
        shape = like[name].shape
        out[name] = p[4 + r, :math.prod(shape)].reshape(shape)
    return out


def kernel(x, norm_g, dn_w_in, dn_conv_w, dn_a_log, dn_dt_bias, dn_o_norm_g, dn_w_out, sb_w_in, sb_q_norm_g, sb_k_norm_g, sb_w_out, sc_w_in, sc_conv_w, sc_w_out, loss_target, m_norm_g, m_dn_w_in, m_dn_conv_w, m_dn_a_log, m_dn_dt_bias, m_dn_o_norm_g, m_dn_w_out, m_sb_w_in, m_sb_q_norm_g, m_sb_k_norm_g, m_sb_w_out, m_sc_w_in, m_sc_conv_w, m_sc_w_out, v_norm_g, v_dn_w_in, v_dn_conv_w, v_dn_a_log, v_dn_dt_bias, v_dn_o_norm_g, v_dn_w_out, v_sb_w_in, v_sb_q_norm_g, v_sb_k_norm_g, v_sb_w_out, v_sc_w_in, v_sc_conv_w, v_sc_w_out):
    w = dict(norm_g=norm_g, dn_w_in=dn_w_in, dn_conv_w=dn_conv_w, dn_a_log=dn_a_log, dn_dt_bias=dn_dt_bias,
             dn_o_norm_g=dn_o_norm_g, dn_w_out=dn_w_out, sb_w_in=sb_w_in, sb_q_norm_g=sb_q_norm_g, sb_k_norm_g=sb_k_norm_g,
             sb_w_out=sb_w_out, sc_w_in=sc_w_in, sc_conv_w=sc_conv_w, sc_w_out=sc_w_out)
    m = dict(norm_g=m_norm_g, dn_w_in=m_dn_w_in, dn_conv_w=m_dn_conv_w, dn_a_log=m_dn_a_log, dn_dt_bias=m_dn_dt_bias,
             dn_o_norm_g=m_dn_o_norm_g, dn_w_out=m_dn_w_out, sb_w_in=m_sb_w_in, sb_q_norm_g=m_sb_q_norm_g,
             sb_k_norm_g=m_sb_k_norm_g, sb_w_out=m_sb_w_out, sc_w_in=m_sc_w_in, sc_conv_w=m_sc_conv_w, sc_w_out=m_sc_w_out)
    v = dict(norm_g=v_norm_g, dn_w_in=v_dn_w_in, dn_conv_w=v_dn_conv_w, dn_a_log=v_dn_a_log, dn_dt_bias=v_dn_dt_bias,
             dn_o_norm_g=v_dn_o_norm_g, dn_w_out=v_dn_w_out, sb_w_in=v_sb_w_in, sb_q_norm_g=v_sb_q_norm_g,
             sb_k_norm_g=v_sb_k_norm_g, sb_w_out=v_sb_w_out, sc_w_in=v_sc_w_in, sc_conv_w=v_sc_conv_w, sc_w_out=v_sc_w_out)

    def gather_of(keys):
        return _Gather([_as_2d(w[k][j]).astype(BF16) if k in _MATMUL_WEIGHTS else _as_2d(w[k][j]) for k, j in keys])

    def full_weights(keys, gathered):
        return {key: g if key[0] in _BLOCKED else _assemble(key[0], g) for key, g in zip(keys, gathered)}

    def exchange_of(keys, grads, extra=()):
        out = [grads[k, j] if k in _BLOCKED else
               _disassemble(k, grads[k, j].astype(BF16) if k in _MATMUL_WEIGHTS else grads[k, j]) for k, j in keys]
        return _Exchange(out + list(extra), [True] * len(out) + [False] * len(extra))

    F = full_weights(_GATHER_0, _comm_call(gather_of(_GATHER_0), "gather_first"))
    xs, saves = [x[0]], []
    h = _rmsnorm_fwd(xs[0], norm_g[0:1], "norm0")

    def w_out_0(got):
        F.update(full_weights(_GATHER_1, got))
        return F["dn_w_out", 0]

    (y, h), sv, _ = _dn_layer_fwd(h, _dn_split_w_in(F["dn_w_in", 0]), F["dn_conv_w", 0], dn_a_log[0:1], dn_dt_bias[0:1],
                                  F["dn_o_norm_g", 0], w_out_0, xs[0], "dn0", gather_of(_GATHER_1), norm_g[1:2])
    xs.append(y)
    saves.append(sv)
    (y, h), sv, got = _sb_layer_fwd(h, F["sb_w_in", 0], sb_q_norm_g, sb_k_norm_g, F["sb_w_out", 0], xs[1], "sb",
                                    gather_of(_GATHER_2), norm_g[2:3])
    F.update(full_weights(_GATHER_2, got))
    xs.append(y)
    saves.append(sv)
    (y, h), sv = _sc_layer_fwd(h, F["sc_w_in", 0], F["sc_conv_w", 0], F["sc_w_out", 0], xs[2], "sc", norm_g[3:4])
    xs.append(y)
    saves.append(sv)
    (y, _), sv, _ = _dn_layer_fwd(h, _dn_split_w_in(F["dn_w_in", 1]), F["dn_conv_w", 1], dn_a_log[1:2], dn_dt_bias[1:2],
                                  F["dn_o_norm_g", 1], F["dn_w_out", 1], xs[3], "dn1")
    xs.append(y)
    saves.append(sv)
    dx, loss_part = _loss_head(xs[4], loss_target[0])

    G, dnorm, landed = {}, [None] * 4, {}

    def keep(grads, j):
        G.update({(k, j): g for k, g in grads.items()})

    (dx, dnorm[3]), grads, _, _ = _dn_layer_bwd(dx, saves[3], "dn1", (xs[3], norm_g[3:4], dx))
    keep(grads, 1)
    (dx, dnorm[2]), grads = _sc_layer_bwd(dx, saves[2], "sc", (xs[2], norm_g[2:3], dx))
    keep(grads, 0)
    (dx, dnorm[1]), grads, got = _sb_layer_bwd(dx, saves[1], "sb", (xs[1], norm_g[1:2], dx), exchange_of(_EXCHANGE_A, G))
    keep(grads, 0)
    landed.update(zip(_EXCHANGE_A, got))

    def exchange_b(dw_out):
        G["dn_w_out", 0] = dw_out
        return exchange_of(_EXCHANGE_B, G)

    def exchange_c(grads):
        keep(grads, 0)
        return exchange_of(_EXCHANGE_C, G)

    (dx, dnorm[0]), grads, got, got_late = _dn_layer_bwd(dx, saves[0], "dn0", (xs[0], norm_g[0:1], dx), exchange_b, exchange_c)
    landed.update(zip(_EXCHANGE_B, got))
    landed.update(zip(_EXCHANGE_C, got_late))
    replicated = dict(norm_g=jnp.concatenate(dnorm, axis=0),
                      dn_a_log=jnp.concatenate([G["dn_a_log", 0], G["dn_a_log", 1]], axis=0),
                      dn_dt_bias=jnp.concatenate([G["dn_dt_bias", 0], G["dn_dt_bias", 1]], axis=0),
                      sb_q_norm_g=G["sb_q_norm_g", 0], sb_k_norm_g=G["sb_k_norm_g", 0])
    got = _comm_call(_Exchange([_pack_replicated(replicated)], [False]), "exchange_replicated")

    res = {}
    for k in _ORDER:
        if k in _REPLICATED:
            continue
        per_layer = []
        for j in range(w[k].shape[0]):
            shape = w[k][j].shape
            outs = _adamw(_as_2d(w[k][j]), _as_2d(m[k][j]), _as_2d(v[k][j]), landed[k, j], f"adamw_{k}{j}")
            per_layer.append([o.reshape(shape) for o in outs])
        res[k] = [jnp.stack([layer[i] for layer in per_layer], axis=0) for i in range(4)]
    outs = _adamw(_pack_replicated(w), _pack_replicated(m), _pack_replicated(v), got[-1], "adamw_replicated")
    unpacked = [_unpack_replicated(o, w) for o in outs]
    for k in _REPLICATED:
        res[k] = [u[k] for u in unpacked]

    loss = lax.psum(loss_part[0, 0], ("x", "y", "c"))
    return (loss, dx[None]) + tuple(res[k][0] for k in _ORDER) + tuple(res[k][1] for k in _ORDER) \
        + tuple(res[k][2] for k in _ORDER) + tuple(res[k][3] for k in _ORDER)
```

```python
import functools
import itertools
import math

import jax
import jax.numpy as jnp
from jax import lax
from jax.experimental import pallas as pl
from jax.experimental.pallas import tpu as pltpu

F32 = jnp.float32
BF16 = jnp.bfloat16
HIGHEST = lax.Precision.HIGHEST

N_DEV = 8
D_MODEL = 1024
RMS_EPS = 1e-6
L2_EPS = 1e-6

DN_HEADS = 8
DN_DK = 128
DN_DV = 256
DN_QK_W = DN_HEADS * DN_DK
DN_V_W = DN_HEADS * DN_DV
DN_CONV = 4
DN_CHUNK = 64
DN_CONV_W = 2 * DN_QK_W + DN_V_W
DN_IN = DN_CONV_W + DN_V_W + 2 * DN_HEADS
DN_AB_PAD = 128
DN_PREP_BLK = 512

SB_HEADS = 16
SB_DH = 64
SB_W = SB_HEADS * SB_DH
SB_PAIRS = SB_HEADS // 2
SB_TQ = 256
SB_TK = 128
SB_DEAD = -106.0

SC_W = 2 * D_MODEL
SC_CONV = 3
SC_BLK = 512
SC_NBLK = SC_W // SC_BLK

ADAM_LR = 0.001
ADAM_B1 = 0.9
ADAM_B2 = 0.999
ADAM_EPS = 1e-08
ADAM_WD = 0.01
ADAM_STEP = 10

LANE = 128
SUBLANE = 8
HALO = SUBLANE
LONG_ROW_TILE = 512
NORM_FUSED_TM = 512
WIDE_ROW_TILE = 128
CONV_ROW_TILE = 256
VMEM_LIMIT = 48 * 2 ** 20

NN = ((1,), (0,))
NT = ((1,), (1,))
TN = ((0,), (0,))


def _dot(a, b, dims=NN, precision=None):
    return lax.dot_general(a, b, (dims, ((), ())), precision=precision, preferred_element_type=F32)


def _bdot(a, b, dims=NN):
    return _dot(a.astype(BF16), b.astype(BF16), dims)


def _hdot(a, b, dims=NN):
    return _dot(a, b, dims, precision=HIGHEST)


def _tile(dim, pref, align=LANE):
    t = (min(pref, dim) // align) * align
    while t >= align:
        if dim % t == 0:
            return t
        t -= align
    return dim


def _params(*sem):
    return pltpu.CompilerParams(dimension_semantics=sem, vmem_limit_bytes=VMEM_LIMIT)


def _sigmoid(x):
    return 0.5 * jnp.tanh(0.5 * x) + 0.5


def _softplus(x):
    return jnp.maximum(x, 0.0) + jnp.log(1.0 + jnp.exp(-jnp.abs(x)))


def _silu_and_grad(x):
    s = _sigmoid(x)
    return x * s, s * (1.0 + x * (1.0 - s))


def _iota2(shape, dim):
    return lax.broadcasted_iota(jnp.int32, shape, dim)


def _matmul(a, b, mode, name, out_dtype=F32, add=None, b_cols=None, blocked_b=False, blocked_out=0,
            norm_fwd=None, norm_bwd=None, tm=1024, tn=1024, tk=1024):
    b_rows, b_width = (b.shape[1], b.shape[0] * b.shape[2]) if blocked_b else b.shape
    c0, b_used = b_cols if b_cols is not None else (0, b_width)
    if mode == "nn":
        (M, K), (K2, N) = a.shape, (b_rows, b_used)
    elif mode == "nt":
        (M, K), (N, K2) = a.shape, (b_rows, b_used)
    else:
        (K, M), (K2, N) = a.shape, (b_rows, b_used)
    assert K == K2, (a.shape, b.shape, mode)
    tm, tn, tk = _tile(M, tm), _tile(N, tn), _tile(K, tk)
    if blocked_b and mode == "nt":
        tk = b.shape[2]
    elif blocked_b:
        tn = b.shape[2]
    if blocked_out:
        tn = N // blocked_out
    nk = K // tk
    dims = {"nn": NN, "nt": NT, "tn": TN}[mode]
    a_spec = pl.BlockSpec((tk, tm), lambda i, j, k: (k, i)) if mode == "tn" else pl.BlockSpec((tm, tk), lambda i, j, k: (i, k))
    if mode == "nt":
        cb0 = c0 // tk
        assert c0 % tk == 0
        b_spec = (pl.BlockSpec((None, tn, tk), lambda i, j, k: (k + cb0, j, 0)) if blocked_b
                  else pl.BlockSpec((tn, tk), lambda i, j, k: (j, k + cb0)))
    else:
        cb0 = c0 // tn
        assert c0 % tn == 0
        b_spec = (pl.BlockSpec((None, tk, tn), lambda i, j, k: (j + cb0, k, 0)) if blocked_b
                  else pl.BlockSpec((tk, tn), lambda i, j, k: (k, j + cb0)))
    o_spec = pl.BlockSpec((tm, tn), lambda i, j, k: (i, j))
    out_spec = pl.BlockSpec((None, tm, tn), lambda i, j, k: (j, i, 0)) if blocked_out else o_spec
    out_shape = (blocked_out, M, tn) if blocked_out else (M, N)
    has_add = add is not None
    vec_spec = pl.BlockSpec((1, tn), lambda i, j, k: (0, j))
    assert not (norm_fwd is not None or norm_bwd is not None) or tn == N
    extra_in, extra_specs = [], []
    if has_add:
        extra_in, extra_specs = [add], [o_spec]
    if norm_fwd is not None:
        extra_in, extra_specs = extra_in + [norm_fwd], extra_specs + [vec_spec]
        out_specs = [o_spec, o_spec]
        out_shapes = [jax.ShapeDtypeStruct((M, N), out_dtype), jax.ShapeDtypeStruct((M, N), BF16)]
    elif norm_bwd is not None:
        extra_in, extra_specs = extra_in + list(norm_bwd), extra_specs + [o_spec, vec_spec, o_spec]
        out_specs = [o_spec, vec_spec]
        out_shapes = [jax.ShapeDtypeStruct((M, N), F32), jax.ShapeDtypeStruct((1, N), F32)]
    else:
        out_specs, out_shapes = out_spec, jax.ShapeDtypeStruct(out_shape, out_dtype)

    def body(*refs):
        a_ref, b_ref = refs[0], refs[1]
        extra = list(refs[2:2 + len(extra_in)])
        outs = refs[2 + len(extra_in):]
        add_ref = extra.pop(0) if has_add else None
        p = _bdot(a_ref[...], b_ref[...], dims)

        def finish(acc):
            if has_add:
                acc = acc + add_ref[...]
            if norm_bwd is not None:
                _rmsnorm_bwd_tile(acc, *extra, outs[0], outs[1], first=pl.program_id(0) == 0)
                return
            outs[0][...] = acc.astype(out_dtype)
            if norm_fwd is not None:
                r = lax.rsqrt(jnp.mean(acc * acc, axis=-1, keepdims=True) + RMS_EPS)
                outs[1][...] = (acc * r * extra[0][...]).astype(BF16)

        if nk == 1:
            finish(p)
        else:
            acc_ref = refs[-1]
            k = pl.program_id(2)

            @pl.when(k == 0)
            def _():
                acc_ref[...] = p

            @pl.when(k > 0)
            def _():
                acc_ref[...] += p

            @pl.when(k == nk - 1)
            def _():
                finish(acc_ref[...])

    return pl.pallas_call(
        body, name=name, grid=(M // tm, N // tn, nk),
        in_specs=[a_spec, b_spec] + extra_specs, out_specs=out_specs, out_shape=out_shapes,
        scratch_shapes=[pltpu.VMEM((tm, tn), F32)] if nk > 1 else [],
        compiler_params=(_params("arbitrary", "arbitrary", "arbitrary") if norm_bwd is not None
                         else _params("parallel", "parallel", "arbitrary")),
    )(a, b, *extra_in)


def _rmsnorm_bwd_tile(dh, x_ref, g_ref, res_ref, dx_ref, dg_ref, first):
    xv = x_ref[...]
    r = lax.rsqrt(jnp.mean(xv * xv, axis=-1, keepdims=True) + RMS_EPS)
    xh = xv * r
    dxh = dh * g_ref[...]
    m = jnp.mean(dxh * xh, axis=-1, keepdims=True)
    dx_ref[...] = res_ref[...] + r * (dxh - xh * m)
    part = jnp.sum(dh * xh, axis=0, keepdims=True)

    @pl.when(first)
    def _():
        dg_ref[...] = part

    @pl.when(jnp.logical_not(first))
    def _():
        dg_ref[...] += part


def _matmul_nt_sum(pairs, name, comm=None, norm_bwd=None, tm=NORM_FUSED_TM, tk=1024):
    M, N = pairs[0][0].shape[0], pairs[0][1].shape[0]
    tm = _tile(M, tm)
    tks = [_tile(a.shape[1], tk) for a, _, _ in pairs]
    steps = [a.shape[1] // t for (a, _, _), t in zip(pairs, tks)]
    offs = [sum(steps[:p]) for p in range(len(pairs))]
    total = sum(steps)

    n_extra = 3 if norm_bwd is not None else 0

    def body(*refs):
        a_refs, b_refs = refs[0:2 * len(pairs):2], refs[1:2 * len(pairs):2]
        extra = refs[2 * len(pairs):2 * len(pairs) + n_extra]
        outs, acc_ref = refs[2 * len(pairs) + n_extra:-1], refs[-1]
        k = pl.program_id(1)
        for p in range(len(pairs)):
            @pl.when((k >= offs[p]) & (k < offs[p] + steps[p]))
            def _(p=p):
                prod = _bdot(a_refs[p][...], b_refs[p][...], NT)
                if p == 0:
                    @pl.when(k == 0)
                    def _():
                        acc_ref[...] = prod

                    @pl.when(k > 0)
                    def _():
                        acc_ref[...] += prod
                else:
                    acc_ref[...] += prod

        @pl.when(k == total - 1)
        def _():
            if norm_bwd is not None:
                _rmsnorm_bwd_tile(acc_ref[...], *extra, outs[0], outs[1], first=pl.program_id(0) == 0)
            else:
                outs[0][...] = acc_ref[...]

    in_specs, args = [], []
    for (a, b, c0), t, off, n in zip(pairs, tks, offs, steps):
        assert c0 % t == 0
        pick = lambda k, off=off, n=n: jnp.clip(k - off, 0, n - 1)
        in_specs += [pl.BlockSpec((tm, t), lambda i, k, pick=pick: (i, pick(k))),
                     pl.BlockSpec((N, t), lambda i, k, pick=pick, cb0=c0 // t: (0, pick(k) + cb0))]
        args += [a, b]
    row, vec = pl.BlockSpec((tm, N), lambda i, k: (i, 0)), pl.BlockSpec((1, N), lambda i, k: (0, 0))
    if norm_bwd is not None:
        in_specs += [row, vec, row]
        args += list(norm_bwd)
        out_specs, out_shape = [row, vec], [jax.ShapeDtypeStruct((M, N), F32), jax.ShapeDtypeStruct((1, N), F32)]
    else:
        out_specs, out_shape = [row], [jax.ShapeDtypeStruct((M, N), F32)]
    outs, landed = _call(body, comm, name=name, grid=(M // tm, total), in_specs=in_specs, out_specs=out_specs,
                         out_shape=out_shape, scratch_shapes=[pltpu.VMEM((tm, N), F32)],
                         semantics=("arbitrary", "arbitrary"), args=tuple(args))
    return (outs if norm_bwd is not None else outs[0]), landed


def _rmsnorm_fwd(x, g, name):
    T, D = x.shape
    tt = _tile(T, LONG_ROW_TILE, SUBLANE)

    def body(x_ref, g_ref, o_ref):
        xv = x_ref[...]
        r = lax.rsqrt(jnp.mean(xv * xv, axis=-1, keepdims=True) + RMS_EPS)
        o_ref[...] = (xv * r * g_ref[...]).astype(BF16)

    return pl.pallas_call(
        body, name=name, grid=(T // tt,),
        in_specs=[pl.BlockSpec((tt, D), lambda i: (i, 0)), pl.BlockSpec((1, D), lambda i: (0, 0))],
        out_specs=pl.BlockSpec((tt, D), lambda i: (i, 0)),
        out_shape=jax.ShapeDtypeStruct((T, D), BF16),
        compiler_params=_params("parallel"),
    )(x, g)


def _loss_head(y, target, name="loss_head"):
    T, D = y.shape
    tt = _tile(T, LONG_ROW_TILE, SUBLANE)

    def body(y_ref, t_ref, dy_ref, l_ref):
        e = y_ref[...] - t_ref[...]
        dy_ref[...] = e * (1.0 / D)
        s = jnp.sum(jnp.sum(e * e, axis=1, keepdims=True), axis=0, keepdims=True) * (0.5 / D)
        s = jnp.broadcast_to(s, (1, LANE))

        @pl.when(pl.program_id(0) == 0)
        def _():
            l_ref[...] = s

        @pl.when(pl.program_id(0) > 0)
        def _():
            l_ref[...] += s

    row = pl.BlockSpec((tt, D), lambda i: (i, 0))
    return pl.pallas_call(
        body, name=name, grid=(T // tt,),
        in_specs=[row, row], out_specs=[row, pl.BlockSpec((1, LANE), lambda i: (0, 0))],
        out_shape=[jax.ShapeDtypeStruct((T, D), F32), jax.ShapeDtypeStruct((1, LANE), F32)],
        compiler_params=_params("arbitrary"),
    )(y, target)


def _down(x, k):
    return pltpu.roll(x, k, 0) if k else x


def _up(x, k):
    return pltpu.roll(x, x.shape[0] - k, 0) if k else x


def _sc_fwd(proj, conv_w, name):
    T = proj.shape[0]
    tt = _tile(T, WIDE_ROW_TILE, SUBLANE)
    B = SC_BLK

    def body(p_ref, ph_ref, w_ref, o_ref):
        keep = (pl.program_id(0) > 0).astype(F32)
        for j in range(SC_NBLK):
            cb, cc, cu, cg = (slice(k * SC_W + j * B, k * SC_W + (j + 1) * B) for k in range(4))
            cw = slice(j * B, (j + 1) * B)
            z = jnp.concatenate([ph_ref[:, cc] * ph_ref[:, cu] * keep, p_ref[:, cc] * p_ref[:, cu]], axis=0)
            cz = (w_ref[2:3, cw] * z + w_ref[1:2, cw] * _down(z, 1) + w_ref[0:1, cw] * _down(z, 2))[HALO:]
            gate = p_ref[:, cg]
            o_ref[:, cw] = (p_ref[:, cb] * cz * (gate * _sigmoid(gate))).astype(BF16)

    return pl.pallas_call(
        body, name=name, grid=(T // tt,),
        in_specs=[pl.BlockSpec((tt, 4 * SC_W), lambda i: (i, 0)),
                  pl.BlockSpec((HALO, 4 * SC_W), lambda i: (jnp.maximum(i * (tt // HALO) - 1, 0), 0)),
                  pl.BlockSpec((SC_CONV, SC_W), lambda i: (0, 0))],
        out_specs=pl.BlockSpec((tt, SC_W), lambda i: (i, 0)),
        out_shape=jax.ShapeDtypeStruct((T, SC_W), BF16),
        compiler_params=_params("parallel"),
    )(proj, proj, conv_w)


def _sc_bwd(dyg, proj, conv_w, name):
    T = proj.shape[0]
    tt = _tile(T, WIDE_ROW_TILE, SUBLANE)
    nt = T // tt
    B = SC_BLK
    hb = tt // HALO

    def body(d_ref, dn_ref, p_ref, pp_ref, pn_ref, w_ref, o_ref, dw_ref):
        i = pl.program_id(0)
        keep_p = (i > 0).astype(F32)
        keep_n = (i < nt - 1).astype(F32)
        main = slice(HALO, HALO + tt)
        parts = []
        for j in range(SC_NBLK):
            cw = slice(j * B, (j + 1) * B)

            def ext(k):
                s = slice(k * SC_W + j * B, k * SC_W + (j + 1) * B)
                return s, jnp.concatenate([pp_ref[:, s] * keep_p, p_ref[:, s], pn_ref[:, s]], axis=0)

            (sb, b), (sc, c), (su, u), (sg_, gate) = ext(0), ext(1), ext(2), ext(3)
            dyg_e = jnp.concatenate([jnp.zeros((HALO, B), F32), d_ref[:, cw], dn_ref[:, cw] * keep_n], axis=0)
            w0, w1, w2 = w_ref[0:1, cw], w_ref[1:2, cw], w_ref[2:3, cw]
            z = c * u
            z1, z2 = _down(z, 1), _down(z, 2)
            cz = w2 * z + w1 * z1 + w0 * z2
            sg, dsg = _silu_and_grad(gate)
            dy = dyg_e * sg
            dcz = dy * b
            dz = w2 * dcz + w1 * _up(dcz, 1) + w0 * _up(dcz, 2)
            o_ref[:, sb] = (dy * cz)[main].astype(BF16)
            o_ref[:, sc] = (dz * u)[main].astype(BF16)
            o_ref[:, su] = (dz * c)[main].astype(BF16)
            o_ref[:, sg_] = (dyg_e * (b * cz) * dsg)[main].astype(BF16)
            dcm = dcz[main]
            parts.append(jnp.concatenate([jnp.sum(dcm * z2[main], axis=0, keepdims=True),
                                          jnp.sum(dcm * z1[main], axis=0, keepdims=True),
                                          jnp.sum(dcm * z[main], axis=0, keepdims=True)], axis=0))
        part = jnp.concatenate(parts, axis=1)

        @pl.when(i == 0)
        def _():
            dw_ref[...] = part

        @pl.when(i > 0)
        def _():
            dw_ref[...] += part

    nxt = lambda i: (jnp.minimum((i + 1) * hb, nt * hb - 1), 0)
    return pl.pallas_call(
        body, name=name, grid=(nt,),
        in_specs=[pl.BlockSpec((tt, SC_W), lambda i: (i, 0)),
                  pl.BlockSpec((HALO, SC_W), nxt),
                  pl.BlockSpec((tt, 4 * SC_W), lambda i: (i, 0)),
                  pl.BlockSpec((HALO, 4 * SC_W), lambda i: (jnp.maximum(i * hb - 1, 0), 0)),
                  pl.BlockSpec((HALO, 4 * SC_W), nxt),
                  pl.BlockSpec((SC_CONV, SC_W), lambda i: (0, 0))],
        out_specs=[pl.BlockSpec((tt, 4 * SC_W), lambda i: (i, 0)), pl.BlockSpec((SC_CONV, SC_W), lambda i: (0, 0))],
        out_shape=[jax.ShapeDtypeStruct((T, 4 * SC_W), BF16), jax.ShapeDtypeStruct((SC_CONV, SC_W), F32)],
        compiler_params=_params("arbitrary"),
    )(dyg, dyg, proj, proj, proj, conv_w)


def _split3_dot(x, m):
    hi = x.astype(BF16)
    r1 = x - hi.astype(F32)
    mid = r1.astype(BF16)
    lo = (r1 - mid.astype(F32)).astype(BF16)
    return _dot(hi, m) + _dot(mid, m) + _dot(lo, m)


def _split2_dot(x, m):
    hi = x.astype(BF16)
    lo = (x - hi.astype(F32)).astype(BF16)
    return _dot(hi, m) + _dot(lo, m)


def _head_mean_matrix():
    r, c = _iota2((LANE, LANE), 0), _iota2((LANE, LANE), 1)
    return jnp.where((r // SB_DH) == (c // SB_DH), 1.0 / SB_DH, 0.0).astype(BF16)


def _sb_prep(proj, qg2, kg2, name):
    T = proj.shape[0]
    tt = _tile(T, WIDE_ROW_TILE, SUBLANE)

    def body(p_ref, qg_ref, kg_ref, q_ref, k_ref, v_ref):
        bd = _head_mean_matrix()

        def norm(x, g, scale):
            r = lax.rsqrt(_split3_dot(x * x, bd) + RMS_EPS)
            return (x * r * g * scale).astype(BF16)

        v_ref[...] = p_ref[:, 2 * SB_W:3 * SB_W].astype(BF16)
        for p in range(SB_PAIRS):
            cols = slice(p * LANE, (p + 1) * LANE)
            q_ref[:, cols] = norm(p_ref[:, cols], qg_ref[...], SB_DH ** -0.5)
            k_ref[:, cols] = norm(p_ref[:, SB_W + p * LANE:SB_W + (p + 1) * LANE], kg_ref[...], 1.0)

    blk = pl.BlockSpec((tt, SB_W), lambda i: (i, 0))
    vec = pl.BlockSpec((1, LANE), lambda i: (0, 0))
    return pl.pallas_call(
        body, name=name, grid=(T // tt,),
        in_specs=[pl.BlockSpec((tt, 4 * SB_W), lambda i: (i, 0)), vec, vec],
        out_specs=[blk, blk, blk],
        out_shape=[jax.ShapeDtypeStruct((T, SB_W), BF16)] * 3,
        compiler_params=_params("parallel"),
    )(proj, qg2, kg2)


def _sb_prep_bwd(proj, dqn, dkn, dv, dgate, qg2, kg2, name):
    T = proj.shape[0]
    tt = _tile(T, WIDE_ROW_TILE, SUBLANE)

    def body(p_ref, dq_ref, dk_ref, dv_ref, dg_ref, qg_ref, kg_ref, o_ref, dqg_ref, dkg_ref):
        i = pl.program_id(0)
        bd = _head_mean_matrix()

        def norm_bwd(x, g, dy):
            r = lax.rsqrt(_split3_dot(x * x, bd) + RMS_EPS)
            xh = x * r
            dxh = dy * g
            m = _split3_dot(dxh * xh, bd)
            return r * (dxh - xh * m), jnp.sum(dy * xh, axis=0, keepdims=True)

        o_ref[:, 2 * SB_W:3 * SB_W] = dv_ref[...].astype(BF16)
        o_ref[:, 3 * SB_W:4 * SB_W] = dg_ref[...].astype(BF16)
        pq = pk = jnp.zeros((1, LANE), F32)
        for p in range(SB_PAIRS):
            cols, kcols = slice(p * LANE, (p + 1) * LANE), slice(SB_W + p * LANE, SB_W + (p + 1) * LANE)
            dxq, sq = norm_bwd(p_ref[:, cols], qg_ref[...], dq_ref[:, cols])
            dxk, sk = norm_bwd(p_ref[:, kcols], kg_ref[...], dk_ref[:, cols])
            o_ref[:, cols] = dxq.astype(BF16)
            o_ref[:, kcols] = dxk.astype(BF16)
            pq, pk = pq + sq, pk + sk

        @pl.when(i == 0)
        def _():
            dqg_ref[...] = pq
            dkg_ref[...] = pk

        @pl.when(i > 0)
        def _():
            dqg_ref[...] += pq
            dkg_ref[...] += pk

    blk = pl.BlockSpec((tt, SB_W), lambda i: (i, 0))
    vec = pl.BlockSpec((1, LANE), lambda i: (0, 0))
    wide = pl.BlockSpec((tt, 4 * SB_W), lambda i: (i, 0))
    return pl.pallas_call(
        body, name=name, grid=(T // tt,),
        in_specs=[wide, blk, blk, blk, blk, vec, vec],
        out_specs=[wide, vec, vec],
        out_shape=[jax.ShapeDtypeStruct((T, 4 * SB_W), BF16)] + [jax.ShapeDtypeStruct((1, LANE), F32)] * 2,
        compiler_params=_params("arbitrary"),
    )(proj, dqn, dkn, dv, dgate, qg2, kg2)


def _fold_heads(part, name):
    def body(p_ref, o_ref):
        r, c = _iota2((LANE, SB_DH), 0), _iota2((LANE, SB_DH), 1)
        fold = jnp.where((r % SB_DH) == c, 1.0, 0.0).astype(F32)
        o_ref[...] = jnp.sum(_hdot(p_ref[...], fold), axis=0, keepdims=True)

    return pl.pallas_call(body, name=name, out_shape=jax.ShapeDtypeStruct((1, SB_DH), F32))(part)


def _sb_masks():
    lane = _iota2((1, LANE), 1)
    return lane < SB_DH


def _sb_attn_fwd(qn, kn, vb, proj, name, comm=None):
    T = qn.shape[0]
    tq, tk = _tile(T, SB_TQ, SUBLANE), SB_TK
    assert tq % tk == 0

    def body(q_ref, k_ref, v_ref, g_ref, o_ref, og_ref, lt_ref, done_ref):
        i = pl.program_id(1)
        ma = _sb_masks()
        q2 = q_ref[...]
        zero = jnp.zeros_like(q2)
        qs = (jnp.where(ma, q2, zero), jnp.where(ma, zero, q2))
        upper = (_iota2((tk, tk), 0) > _iota2((tk, tk), 1)).astype(BF16)
        qpos = i * tq + _iota2((tq, tk), 0)
        nb = tq // tk

        def trip(kb_top, masked, carry):
            acc, la, lb = carry
            chains = [(b, h) for b in range(nb) for h in range(2)]
            k2s, vss, masks = [], [], []
            for b in range(nb):
                kb = kb_top - b
                rows = pl.ds(pl.multiple_of(kb * tk, tk), tk)
                k2s.append(k_ref[rows, :])
                v2 = v_ref[rows, :]
                zv = jnp.zeros_like(v2)
                vss.append((jnp.where(ma, v2, zv), jnp.where(ma, zv, v2)))
                masks.append((kb * tk + _iota2((tq, tk), 1)) < qpos if masked else None)
            zs = [_dot(qs[h], k2s[b], NT) for b, h in chains]
            ts = [jnp.log(1.0 + jnp.exp(-jnp.abs(z))) for z in zs]
            ls = [-(jnp.maximum(z, 0.0) + t) for z, t in zip(zs, ts)]
            if masked:
                ls = [jnp.where(masks[b], l, 0.0) for (b, h), l in zip(chains, ls)]
            cums = [_split2_dot(l, upper) for l in ls]
            sums = [jnp.sum(l, axis=1, keepdims=True) for l in ls]
            offs, tot = {}, [la, lb]
            for b in range(nb):
                for h in range(2):
                    offs[(b, h)] = tot[h]
                    tot[h] = tot[h] + sums[chains.index((b, h))]
            ws = [jnp.exp(jnp.minimum(z, 0.0) - t + c + offs[ch]) for ch, z, t, c in zip(chains, zs, ts, cums)]
            if masked:
                ws = [jnp.where(masks[b], w, 0.0) for (b, h), w in zip(chains, ws)]
            for (b, h), w in zip(chains, ws):
                acc = acc + _dot(w.astype(BF16), vss[b][h])
            return acc, tot[0], tot[1]

        def largest(la, lb):
            return jnp.max(jnp.maximum(la, lb))

        z1 = jnp.zeros((tq, 1), F32)
        acc, la, lb = trip((i + 1) * nb - 1, True, (jnp.zeros((tq, LANE), F32), z1, z1))

        def live(c):
            return (c[0] < i) & (c[4] > SB_DEAD)

        def more(c):
            j, acc, la, lb, _ = c
            acc, la, lb = trip((i - j) * nb - 1, False, (acc, la, lb))
            return j + 1, acc, la, lb, largest(la, lb)

        done, acc, la, lb, _ = lax.while_loop(live, more, (jnp.int32(0), acc, la, lb, largest(la, lb)))
        gate = g_ref[...]
        o_ref[...] = acc
        og_ref[...] = (acc * (gate * _sigmoid(gate))).astype(BF16)
        lt_ref[...] = jnp.where(_iota2((tq, 2), 1) == 0, la, lb)
        done_ref[...] = jnp.full((SUBLANE, LANE), done, F32)

    nq = T // tq
    qblk = pl.BlockSpec((tq, LANE), lambda p, i: (i, p))
    full = pl.BlockSpec((T, LANE), lambda p, i: (0, p))
    return _call(
        body, comm, name=name, grid=(SB_PAIRS, nq),
        in_specs=[qblk, full, full, pl.BlockSpec((tq, LANE), lambda p, i: (i, 3 * SB_PAIRS + p))],
        out_specs=[qblk, qblk, pl.BlockSpec((None, tq, 2), lambda p, i: (p, i, 0)),
                   pl.BlockSpec((None, None, SUBLANE, LANE), lambda p, i: (p, i, 0, 0))],
        out_shape=[jax.ShapeDtypeStruct((T, SB_W), F32), jax.ShapeDtypeStruct((T, SB_W), BF16),
                   jax.ShapeDtypeStruct((SB_PAIRS, T, 2), F32), jax.ShapeDtypeStruct((SB_PAIRS, nq, SUBLANE, LANE), F32)],
        scratch_shapes=[], semantics=("parallel", "parallel"), args=(qn, kn, vb, proj))


def _sb_attn_bwd(qn, kn, vb, dog, o, ltot, done, proj, name, comm=None):
    T = qn.shape[0]
    tq, tk = _tile(T, SB_TQ, SUBLANE), SB_TK

    def body(q_ref, k_ref, v_ref, dog_ref, o_ref, lt_ref, done_ref, g_ref, dq_ref, dk_ref, dv_ref, dgate_ref):
        i = pl.program_id(1)
        first_trip = i - jnp.max(done_ref[...]).astype(jnp.int32)

        @pl.when(i == 0)
        def _():
            dk_ref[...] = jnp.zeros_like(dk_ref)
            dv_ref[...] = jnp.zeros_like(dv_ref)

        ma = _sb_masks()
        gate, o2, dog2 = g_ref[...], o_ref[...], dog_ref[...]
        sg, dsg = _silu_and_grad(gate)
        do2 = dog2 * sg
        dgate_ref[...] = dog2 * o2 * dsg
        lt = lt_ref[...]
        first = _iota2((tq, 2), 1) == 0
        ltots = (jnp.sum(jnp.where(first, lt, 0.0), axis=1, keepdims=True),
                 jnp.sum(jnp.where(first, 0.0, lt), axis=1, keepdims=True))
        q2 = q_ref[...]
        zq = jnp.zeros_like(q2)
        qs = (jnp.where(ma, q2, zq), jnp.where(ma, zq, q2))
        dob = do2.astype(BF16)
        dos = (jnp.where(ma, dob, zq), jnp.where(ma, zq, dob))
        upto = (_iota2((tk, tk), 0) <= _iota2((tk, tk), 1)).astype(BF16)
        before = (_iota2((tk, tk), 0) < _iota2((tk, tk), 1)).astype(BF16)
        qpos = i * tq + _iota2((tq, tk), 0)
        nb = tq // tk

        def trip(kb_bot, masked, carry):
            dq, la, lb, ea, eb = carry
            chains = [(b, h) for b in range(nb) for h in range(2)]
            rows, k2s, v2s, kss, masks = [], [], [], [], []
            for b in range(nb):
                kb = kb_bot + b
                rows.append(pl.ds(pl.multiple_of(kb * tk, tk), tk))
                k2 = k_ref[rows[b], :]
                zk = jnp.zeros_like(k2)
                k2s.append(k2)
                v2s.append(v_ref[rows[b], :])
                kss.append((jnp.where(ma, k2, zk), jnp.where(ma, zk, k2)))
                masks.append((kb * tk + _iota2((tq, tk), 1)) < qpos if masked else None)

            def keep(vals):
                return [jnp.where(masks[b], x, 0.0) for (b, h), x in zip(chains, vals)] if masked else vals

            zs = [_dot(qs[h], k2s[b], NT) for b, h in chains]
            dws = [_dot(dos[h], v2s[b], NT) for b, h in chains]
            ts = [jnp.log(1.0 + jnp.exp(-jnp.abs(z))) for z in zs]
            ls = keep([-(jnp.maximum(z, 0.0) + t) for z, t in zip(zs, ts)])
            lps = [jnp.minimum(z, 0.0) - t for z, t in zip(zs, ts)]
            cums = [_split3_dot(l, upto) for l in ls]
            lsums = [jnp.sum(l, axis=1, keepdims=True) for l in ls]
            offs, tot = {}, [la, lb]
            for b in range(nb):
                for h in range(2):
                    offs[(b, h)] = tot[h]
                    tot[h] = tot[h] + lsums[chains.index((b, h))]
            ws = keep([jnp.exp(lp + (ltots[h] - (offs[(b, h)] + c))) for (b, h), lp, c in zip(chains, lps, cums)])
            es = [dw * w for dw, w in zip(dws, ws)]
            ecums = [_split2_dot(e, before) for e in es]
            esums = [jnp.sum(e, axis=1, keepdims=True) for e in es]
            eoffs, etot = {}, [ea, eb]
            for b in range(nb):
                for h in range(2):
                    eoffs[(b, h)] = etot[h]
                    etot[h] = etot[h] + esums[chains.index((b, h))]
            dzs = keep([e - jnp.exp(lp) * (e + eoffs[ch] + ec) for ch, e, lp, ec in zip(chains, es, lps, ecums)])
            dzs = [dz.astype(BF16) for dz in dzs]
            wbs = [w.astype(BF16) for w in ws]
            for (b, h), dz in zip(chains, dzs):
                dq = dq + _dot(dz, kss[b][h])
            for b in range(nb):
                ia, ib = chains.index((b, 0)), chains.index((b, 1))
                dk_ref[rows[b], :] += _dot(dzs[ia], qs[0], TN) + _dot(dzs[ib], qs[1], TN)
                dv_ref[rows[b], :] += _dot(wbs[ia], dos[0], TN) + _dot(wbs[ib], dos[1], TN)
            return dq, tot[0], tot[1], etot[0], etot[1]

        z1 = jnp.zeros((tq, 1), F32)
        carry = lax.fori_loop(first_trip, i, lambda j, c: trip(j * nb, False, c),
                              (jnp.zeros((tq, LANE), F32), z1, z1, z1, z1))
        dq = trip(i * nb, True, carry)[0]
        dq_ref[...] = dq * (SB_DH ** -0.5)

    qblk = pl.BlockSpec((tq, LANE), lambda p, i: (i, p))
    full = pl.BlockSpec((T, LANE), lambda p, i: (0, p))
    return _call(
        body, comm, name=name, grid=(SB_PAIRS, T // tq),
        in_specs=[qblk, full, full, qblk, qblk, pl.BlockSpec((None, tq, 2), lambda p, i: (p, i, 0)),
                  pl.BlockSpec((None, None, SUBLANE, LANE), lambda p, i: (p, i, 0, 0)),
                  pl.BlockSpec((tq, LANE), lambda p, i: (i, 3 * SB_PAIRS + p))],
        out_specs=[qblk, full, full, qblk],
        out_shape=[jax.ShapeDtypeStruct((T, SB_W), F32)] * 4,
        scratch_shapes=[], semantics=("parallel", "arbitrary"), args=(qn, kn, vb, dog, o, ltot, done, proj))


def _dn_conv(ext, w_ref, cw):
    return (w_ref[3:4, cw] * ext + w_ref[2:3, cw] * _down(ext, 1) + w_ref[1:2, cw] * _down(ext, 2)
            + w_ref[0:1, cw] * _down(ext, 3))


def _dn_prep_bwd(pqkv, conv_w, dact, name):
    T, W = pqkv.shape
    tt = _tile(T, CONV_ROW_TILE, SUBLANE)
    nt = T // tt
    hb = tt // HALO
    B = DN_PREP_BLK
    nq, nqk = DN_QK_W // B, 2 * DN_QK_W // B

    def body(p_ref, pp_ref, pn_ref, w_ref, d_ref, dn_ref, o_ref, dw_ref):
        i = pl.program_id(0)
        keep_p = (i > 0).astype(F32)
        keep_n = (i < nt - 1).astype(F32)
        main = slice(HALO, HALO + tt)
        parts = []
        for cb in range(W // B):
            cw = slice(cb * B, (cb + 1) * B)
            ext = jnp.concatenate([pp_ref[:, cw] * keep_p, p_ref[:, cw], pn_ref[:, cw]], axis=0)
            c = _dn_conv(ext, w_ref, cw)
            s = _sigmoid(c)
            da_dc = s * (1.0 + c * (1.0 - s))
            d_up = jnp.concatenate([jnp.zeros((HALO, B), F32), d_ref[:, cw], dn_ref[:, cw] * keep_n], axis=0)
            if cb < nqk:
                a = c * s
                scale = DN_DK ** -0.5 if cb < nq else 1.0
                normed = []
                for hh in range(B // DN_DK):
                    cols = slice(hh * DN_DK, (hh + 1) * DN_DK)
                    ah = a[:, cols]
                    r = lax.rsqrt(jnp.sum(ah * ah, axis=-1, keepdims=True) + L2_EPS)
                    y = ah * r
                    dy = d_up[:, cols] * scale
                    normed.append(r * (dy - y * jnp.sum(dy * y, axis=-1, keepdims=True)))
                d_up = jnp.concatenate(normed, axis=1)
            dc = d_up * da_dc
            dp = (w_ref[3:4, cw] * dc + w_ref[2:3, cw] * _up(dc, 1) + w_ref[1:2, cw] * _up(dc, 2)
                  + w_ref[0:1, cw] * _up(dc, 3))
            o_ref[:, cw] = dp[main].astype(BF16)
            dcm = dc[main]
            parts.append(jnp.concatenate([jnp.sum(dcm * _down(ext, 3 - k)[main], axis=0, keepdims=True)
                                          for k in range(DN_CONV)], axis=0))
        part = jnp.concatenate(parts, axis=1)

        @pl.when(i == 0)
        def _():
            dw_ref[...] = part

        @pl.when(i > 0)
        def _():
            dw_ref[...] += part

    main_spec = pl.BlockSpec((tt, W), lambda i: (i, 0))
    prev_spec = pl.BlockSpec((HALO, W), lambda i: (jnp.maximum(i * hb - 1, 0), 0))
    next_spec = pl.BlockSpec((HALO, W), lambda i: (jnp.minimum((i + 1) * hb, nt * hb - 1), 0))
    w_spec = pl.BlockSpec((DN_CONV, W), lambda i: (0, 0))
    return pl.pallas_call(
        body, name=name, grid=(nt,),
        in_specs=[main_spec, prev_spec, next_spec, w_spec, main_spec, next_spec],
        out_specs=[main_spec, w_spec],
        out_shape=[jax.ShapeDtypeStruct((T, W), BF16), jax.ShapeDtypeStruct((DN_CONV, W), F32)],
        compiler_params=_params("arbitrary"),
    )(pqkv, pqkv, pqkv, conv_w, dact, dact)


def _dn_gates(a_in, b_in, a_log, dt_bias, name):
    T, H = a_in.shape
    C = DN_CHUNK

    def body(a_ref, b_ref, al_ref, dt_ref, g_ref, beta_ref):
        beta_ref[...] = _sigmoid(b_ref[...])
        g_ref[...] = -jnp.exp(al_ref[...]) * _softplus(a_ref[...] + dt_ref[...])
        tri = (_iota2((C, C), 0) >= _iota2((C, C), 1)).astype(F32)

        def chunk(n, carry):
            rows = pl.ds(pl.multiple_of(n * C, C), C)
            g_ref[rows, :] = _hdot(tri, g_ref[rows, :])
            return carry

        lax.fori_loop(0, T // C, chunk, 0)

    return pl.pallas_call(body, name=name, out_shape=[jax.ShapeDtypeStruct((T, H), F32)] * 2)(a_in, b_in, a_log, dt_bias)


def _dn_gates_bwd(dg, dbeta, a_in, b_in, a_log, dt_bias, name):
    T, H = a_in.shape
    C = DN_CHUNK

    def body(dg_ref, db_ref, a_ref, b_ref, al_ref, dt_ref, da_ref, dbi_ref, dal_ref, ddt_ref):
        tri_t = (_iota2((C, C), 0) <= _iota2((C, C), 1)).astype(F32)

        def chunk(n, carry):
            rows = pl.ds(pl.multiple_of(n * C, C), C)
            da_ref[rows, :] = _hdot(tri_t, dg_ref[rows, :])
            return carry

        lax.fori_loop(0, T // C, chunk, 0)
        dla = da_ref[...]
        x = a_ref[...] + dt_ref[...]
        ea = jnp.exp(al_ref[...])
        da = dla * (-ea) * _sigmoid(x)
        da_ref[...] = da
        dal_ref[...] = jnp.sum(dla * (-ea * _softplus(x)), axis=0, keepdims=True)
        ddt_ref[...] = jnp.sum(da, axis=0, keepdims=True)
        beta = _sigmoid(b_ref[...])
        dbi_ref[...] = db_ref[...] * beta * (1.0 - beta)

    return pl.pallas_call(
        body, name=name,
        out_shape=[jax.ShapeDtypeStruct((T, H), F32)] * 2 + [jax.ShapeDtypeStruct((1, H), F32)] * 2,
    )(dg, dbeta, a_in, b_in, a_log, dt_bias)


def _dn_chunk_terms(q, k, gc, bc):
    C = DN_CHUNK
    r, c = _iota2((C, C), 0), _iota2((C, C), 1)
    lower, strict, eye = r >= c, r > c, r == c
    grow = jnp.sum(jnp.where(eye, gc, 0.0), axis=0, keepdims=True)
    decay = jnp.where(lower, jnp.exp(jnp.where(lower, gc - grow, 0.0)), 0.0)
    last = _iota2((C, 1), 0) == C - 1
    gl = jnp.sum(jnp.where(last, gc, 0.0), axis=0, keepdims=True)
    eg = jnp.exp(gc)
    egl = jnp.exp(gl - gc)
    kb = k * bc
    lmat = jnp.where(strict, _bdot(kb, k, NT) * decay, 0.0)
    aqk = jnp.where(lower, _bdot(q, k, NT) * decay, 0.0)
    return dict(lower=lower, strict=strict, eye=eye, last=last, decay=decay, gl=gl, eg=eg, egl=egl, kb=kb,
                lmat=lmat, aqk=aqk, qd=q * eg, kd=k * egl)


def _split(x):
    hi = x.astype(BF16)
    return hi, (x - hi.astype(F32)).astype(BF16)


def _x3dot(a, b, dims=NN):
    ah, al = a if isinstance(a, tuple) else _split(a)
    bh, bl = b if isinstance(b, tuple) else _split(b)
    return _dot(ah, bh, dims) + (_dot(ah, bl, dims) + _dot(al, bh, dims))


def _interleave(gens):
    for _ in itertools.zip_longest(*gens):
        pass


def _unit_lower_inverse_steps(lmat, eye, out):
    ident = jnp.where(eye, 1.0, 0.0).astype(F32)
    m = -lmat
    inv = ident + m
    for _ in range(int(math.log2(DN_CHUNK)) - 1):
        ms = _split(m)
        m = _x3dot(ms, ms)
        yield
        inv = inv + _x3dot(inv, m)
        yield
    out["tm"] = inv


def _dn_chunk_fwd(pqkv, conv_w, g, beta, pgate, gn, name, comm=None):
    T = pqkv.shape[0]
    C, H = DN_CHUNK, DN_HEADS
    N = T // C
    B = DN_PREP_BLK
    nq, nqk = DN_QK_W // B, 2 * DN_QK_W // B

    def step(p_ref, cw_ref, g_ref, b_ref, pg_ref, gn_ref, act_out, o_ref, og_ref, s_out, t_out, vn_out, u_out, w_out,
             s_scr, tail_scr, a_ref, a_next):
        head_lane = _iota2((C, H), 1)

        def prepare(cb):
            cw = slice(cb * B, (cb + 1) * B)
            ext = jnp.concatenate([tail_scr[:, cw], p_ref[:, cw]], axis=0)
            c = _dn_conv(ext, cw_ref, cw)[HALO:]
            yield
            a = c * _sigmoid(c)
            if cb >= nqk:
                a_next[:, cw] = a
                act_out[:, cw] = a
                return
            scale = DN_DK ** -0.5 if cb < nq else 1.0
            for hh in range(B // DN_DK):
                yield
                ah = a[:, hh * DN_DK:(hh + 1) * DN_DK]
                val = ah * (lax.rsqrt(jnp.sum(ah * ah, axis=-1, keepdims=True) + L2_EPS) * scale)
                cols = slice(cb * B + hh * DN_DK, cb * B + (hh + 1) * DN_DK)
                a_next[:, cols] = val
                act_out[:, cols] = val

        def head(hh):
            qs, vs = slice(hh * DN_DK, (hh + 1) * DN_DK), slice(hh * DN_DV, (hh + 1) * DN_DV)
            q, k, v = a_ref[:, qs], a_ref[:, DN_QK_W + hh * DN_DK:DN_QK_W + (hh + 1) * DN_DK], \
                a_ref[:, 2 * DN_QK_W + hh * DN_DV:2 * DN_QK_W + (hh + 1) * DN_DV]
            gc = jnp.sum(jnp.where(head_lane == hh, g_ref[...], 0.0), axis=1, keepdims=True)
            bc = jnp.sum(jnp.where(head_lane == hh, b_ref[...], 0.0), axis=1, keepdims=True)
            t = _dn_chunk_terms(q, k, gc, bc)
            yield
            res = {}
            yield from _unit_lower_inverse_steps(t["lmat"], t["eye"], res)
            tms = _split(res["tm"])
            u = _x3dot(tms, v * bc)
            yield
            w = _x3dot(tms, t["kb"] * t["eg"])
            yield
            s = s_scr[hh]
            s_out[hh] = s
            t_out[hh] = res["tm"]
            sb = s.astype(BF16)
            vn = u - _dot(w.astype(BF16), sb)
            yield
            o = _dot(t["qd"].astype(BF16), sb) + _bdot(t["aqk"], vn)
            yield
            s_scr[hh] = s * jnp.exp(t["gl"]) + _bdot(t["kd"], vn, TN)
            vn_out[:, vs] = vn
            u_out[:, vs] = u
            w_out[:, qs] = w
            o_ref[:, vs] = o
            gate = pg_ref[:, vs]
            r = lax.rsqrt(jnp.mean(o * o, axis=-1, keepdims=True) + RMS_EPS)
            og_ref[:, vs] = (o * r * gn_ref[...] * (gate * _sigmoid(gate))).astype(BF16)

        _interleave([head(hh) for hh in range(H)] + [prepare(cb) for cb in range(DN_CONV_W // B)])

        @pl.when(pl.program_id(0) < N - 1)
        def _():
            tail_scr[...] = p_ref[C - HALO:C, :]

    def body(*refs):
        s = pl.program_id(0)
        io, (s_scr, tail_scr, buf_a, buf_b) = refs[:-4], refs[-4:]

        @pl.when(s == 0)
        def _():
            tail_scr[...] = jnp.zeros_like(tail_scr)
            buf_b[...] = jnp.zeros_like(buf_b)

        @pl.when(s <= 1)
        def _():
            s_scr[...] = jnp.zeros_like(s_scr)

        @pl.when(s % 2 == 0)
        def _():
            step(*io, s_scr, tail_scr, buf_b, buf_a)

        @pl.when(s % 2 == 1)
        def _():
            step(*io, s_scr, tail_scr, buf_a, buf_b)

    nxt = lambda w: pl.BlockSpec((C, w), lambda s: (jnp.minimum(s, N - 1), 0))
    cur = lambda w: pl.BlockSpec((C, w), lambda s: (jnp.maximum(s - 1, 0), 0))
    per_chunk = lambda a, b: pl.BlockSpec((H, None, a, b), lambda s: (0, jnp.maximum(s - 1, 0), 0, 0))
    return _call(
        body, comm, name=name, grid=(N + 1,),
        in_specs=[nxt(DN_CONV_W), pl.BlockSpec((DN_CONV, DN_CONV_W), lambda s: (0, 0)), cur(H), cur(H), cur(DN_V_W),
                  pl.BlockSpec((1, DN_DV), lambda s: (0, 0))],
        out_specs=[nxt(DN_CONV_W), cur(DN_V_W), cur(DN_V_W), per_chunk(DN_DK, DN_DV), per_chunk(C, C),
                   cur(DN_V_W), cur(DN_V_W), cur(DN_QK_W)],
        out_shape=[jax.ShapeDtypeStruct((T, DN_CONV_W), F32),
                   jax.ShapeDtypeStruct((T, DN_V_W), F32), jax.ShapeDtypeStruct((T, DN_V_W), BF16),
                   jax.ShapeDtypeStruct((H, N, DN_DK, DN_DV), F32),
                   jax.ShapeDtypeStruct((H, N, C, C), F32),
                   jax.ShapeDtypeStruct((T, DN_V_W), F32),
                   jax.ShapeDtypeStruct((T, DN_V_W), F32),
                   jax.ShapeDtypeStruct((T, DN_QK_W), F32)],
        scratch_shapes=[pltpu.VMEM((H, DN_DK, DN_DV), F32), pltpu.VMEM((HALO, DN_CONV_W), F32),
                        pltpu.VMEM((C, DN_CONV_W), F32), pltpu.VMEM((C, DN_CONV_W), F32)],
        semantics=("arbitrary",), args=(pqkv, conv_w, g, beta, pgate, gn))


def _dn_chunk_bwd(act, g, beta, s_saved, tm_saved, vn_saved, u_saved, w_saved, dog, o_raw, pgate, gn, name, comm=None):
    T = act.shape[0]
    C, H = DN_CHUNK, DN_HEADS
    N = T // C

    def body(a_ref, g_ref, b_ref, s_ref, t_ref, vn_ref, u_ref, w_ref, dog_ref, o_ref, pg_ref, gn_ref,
             da_ref, dg_ref, db_ref, dgate_ref, dgn_ref, ds_scr):
        @pl.when(pl.program_id(0) == 0)
        def _():
            ds_scr[...] = jnp.zeros_like(ds_scr)

        head_lane = _iota2((C, H), 1)
        dg_cols, db_cols, dgn_parts = {}, {}, {}

        def output_gate_bwd(hh, vs):
            d, o, gate, gn_v = dog_ref[:, vs], o_ref[:, vs], pg_ref[:, vs], gn_ref[...]
            sg, dsg = _silu_and_grad(gate)
            r = lax.rsqrt(jnp.mean(o * o, axis=-1, keepdims=True) + RMS_EPS)
            n = o * r
            dy = d * sg
            dgate_ref[:, vs] = (d * (n * gn_v) * dsg).astype(BF16)
            dn = dy * gn_v
            dgn_parts[hh] = jnp.sum(dy * n, axis=0, keepdims=True)
            return r * (dn - n * jnp.mean(dn * n, axis=-1, keepdims=True))

        def head(hh):
            qs, vs = slice(hh * DN_DK, (hh + 1) * DN_DK), slice(hh * DN_DV, (hh + 1) * DN_DV)
            ks = slice(DN_QK_W + hh * DN_DK, DN_QK_W + (hh + 1) * DN_DK)
            vas = slice(2 * DN_QK_W + hh * DN_DV, 2 * DN_QK_W + (hh + 1) * DN_DV)
            q, k, v = a_ref[:, qs], a_ref[:, ks], a_ref[:, vas]
            gc = jnp.sum(jnp.where(head_lane == hh, g_ref[...], 0.0), axis=1, keepdims=True)
            bc = jnp.sum(jnp.where(head_lane == hh, b_ref[...], 0.0), axis=1, keepdims=True)
            t = _dn_chunk_terms(q, k, gc, bc)
            yield
            lower, strict, eye = t["lower"], t["strict"], t["eye"]
            decay, eg, egl, kb, qd, kd = t["decay"], t["eg"], t["egl"], t["kb"], t["qd"], t["kd"]
            s, tm, vn, u, w = s_ref[hh], t_ref[hh], vn_ref[:, vs], u_ref[:, vs], w_ref[:, qs]
            d_o = output_gate_bwd(hh, vs)
            ds_next = ds_scr[hh]
            egl_tot = jnp.exp(t["gl"])
            dob, sb, dsb, vnb = d_o.astype(BF16), s.astype(BF16), ds_next.astype(BF16), vn.astype(BF16)

            dvn = _bdot(t["aqk"], dob, TN) + _bdot(kd, dsb)
            yield
            daqk = jnp.where(lower, _dot(dob, vnb, NT), 0.0)
            dqd = _dot(dob, sb, NT)
            dkd = _dot(vnb, dsb, NT)
            yield
            dvnb = dvn.astype(BF16)
            ds_scr[hh] = _bdot(qd, dob, TN) + egl_tot * ds_next - _bdot(w, dvnb, TN)
            dgl = egl_tot * jnp.sum(jnp.sum(s * ds_next, axis=1, keepdims=True), axis=0, keepdims=True)
            dw = -_dot(dvnb, sb, NT)
            yield
            tms = _split(tm)
            dru = _x3dot(tms, dvn, TN)
            drw = _x3dot(tms, dw, TN)
            yield
            dl = -jnp.where(strict, _x3dot(dru, u, NT) + _x3dot(drw, w, NT), 0.0)
            yield
            dkk = (dl * decay).astype(BF16)
            dqk = (daqk * decay).astype(BF16)
            dkb = _bdot(dkk, k) + drw * eg
            yield
            da_ref[:, ks] = _bdot(dkk, kb, TN) + _bdot(dqk, q, TN) + dkd * egl + dkb * bc
            da_ref[:, qs] = _bdot(dqk, k) + dqd * eg
            da_ref[:, vas] = dru * bc
            yield
            db_cols[hh] = jnp.sum(dru * v, axis=1, keepdims=True) + jnp.sum(dkb * k, axis=1, keepdims=True)
            pm = dl * t["lmat"] + daqk * t["aqk"]
            col_as_col = jnp.sum(jnp.where(eye, jnp.sum(pm, axis=0, keepdims=True), 0.0), axis=1, keepdims=True)
            kdsum = jnp.sum(dkd * kd, axis=1, keepdims=True)
            dgc = (jnp.sum(pm, axis=1, keepdims=True) - col_as_col + jnp.sum(dqd * qd, axis=1, keepdims=True)
                   - kdsum + jnp.sum(drw * (kb * eg), axis=1, keepdims=True))
            dgl = dgl + jnp.sum(kdsum, axis=0, keepdims=True)
            dg_cols[hh] = dgc + jnp.where(t["last"], dgl, 0.0)

        _interleave([head(hh) for hh in range(H)])
        dg_ref[...] = sum(jnp.where(head_lane == hh, dg_cols[hh], 0.0) for hh in range(H))
        db_ref[...] = sum(jnp.where(head_lane == hh, db_cols[hh], 0.0) for hh in range(H))
        dgn_part = sum(dgn_parts[hh] for hh in range(H))

        @pl.when(pl.program_id(0) == 0)
        def _():
            dgn_ref[...] = dgn_part

        @pl.when(pl.program_id(0) > 0)
        def _():
            dgn_ref[...] += dgn_part

    row = lambda w: pl.BlockSpec((C, w), lambda n: (N - 1 - n, 0))
    vec = pl.BlockSpec((1, DN_DV), lambda n: (0, 0))
    return _call(
        body, comm, name=name, grid=(N,),
        in_specs=[row(DN_CONV_W), row(H), row(H),
                  pl.BlockSpec((H, None, DN_DK, DN_DV), lambda n: (0, N - 1 - n, 0, 0)),
                  pl.BlockSpec((H, None, C, C), lambda n: (0, N - 1 - n, 0, 0)),
                  row(DN_V_W), row(DN_V_W), row(DN_QK_W), row(DN_V_W), row(DN_V_W), row(DN_V_W), vec],
        out_specs=[row(DN_CONV_W), row(H), row(H), row(DN_V_W), vec],
        out_shape=[jax.ShapeDtypeStruct((T, DN_CONV_W), F32),
                   jax.ShapeDtypeStruct((T, H), F32), jax.ShapeDtypeStruct((T, H), F32),
                   jax.ShapeDtypeStruct((T, DN_V_W), BF16), jax.ShapeDtypeStruct((1, DN_DV), F32)],
        scratch_shapes=[pltpu.VMEM((H, DN_DK, DN_DV), F32)], semantics=("arbitrary",),
        args=(act, g, beta, s_saved, tm_saved, vn_saved, u_saved, w_saved, dog, o_raw, pgate, gn))


def _dn_split_w_in(w):
    return w, jnp.pad(w[:, DN_CONV_W + DN_V_W:], ((0, 0), (0, DN_AB_PAD - 2 * DN_HEADS)))


def _out_proj(og, w_out, x_res, next_g, name):
    if next_g is None:
        return _matmul(og, w_out, "nn", name, add=x_res), None
    return tuple(_matmul(og, w_out, "nn", name, add=x_res, norm_fwd=next_g, tm=NORM_FUSED_TM))


def _dn_layer_fwd(h, wts, conv_w, a_log, dt_bias, gn, w_out, x_res, tag, comm=None, next_g=None):
    w_in, wab = wts
    H = DN_HEADS
    pqkv = _matmul(h, w_in, "nn", tag + "_pqkv", b_cols=(0, DN_CONV_W))
    pgate = _matmul(h, w_in, "nn", tag + "_pgate", b_cols=(DN_CONV_W, DN_V_W))
    pab = _matmul(h, wab, "nn", tag + "_pab")
    a_in, b_in = pab[:, :H], pab[:, H:2 * H]
    g, beta = _dn_gates(a_in, b_in, a_log, dt_bias, tag + "_gates")
    (act, o_raw, og, s_sv, tm_sv, vn_sv, u_sv, w_sv), landed = _dn_chunk_fwd(pqkv, conv_w, g, beta, pgate, gn,
                                                                             tag + "_chunk_fwd", comm)
    if callable(w_out):
        w_out = w_out(landed)
    y = _out_proj(og, w_out, x_res, next_g, tag + "_out")
    saved = dict(h=h, wts=wts, conv_w=conv_w, a_log=a_log, dt_bias=dt_bias, gn=gn, w_out=w_out, pqkv=pqkv, pgate=pgate,
                 a_in=a_in, b_in=b_in, g=g, beta=beta, act=act, o_raw=o_raw, chunk=(s_sv, tm_sv, vn_sv, u_sv, w_sv), og=og)
    return y, saved, landed


def _dn_layer_bwd(dout, sv, tag, norm, comm_of=None, late_comm_of=None):
    w_in, wab = sv["wts"]
    h = sv["h"]
    dog = _matmul(dout, sv["w_out"], "nt", tag + "_dog")
    dw_out = _matmul(sv["og"], dout, "tn", tag + "_dwout", out_dtype=BF16)
    comm = comm_of(dw_out) if comm_of is not None else None
    (dact, dg, dbeta, dgate, dgn), landed = _dn_chunk_bwd(sv["act"], sv["g"], sv["beta"], *sv["chunk"], dog, sv["o_raw"],
                                                          sv["pgate"], sv["gn"], tag + "_chunk_bwd", comm)
    da_in, db_in, da_log, ddt = _dn_gates_bwd(dg, dbeta, sv["a_in"], sv["b_in"], sv["a_log"], sv["dt_bias"],
                                              tag + "_gates_bwd")
    dpqkv, dconv = _dn_prep_bwd(sv["pqkv"], sv["conv_w"], dact, tag + "_prep_bwd")
    dpab = jnp.pad(jnp.concatenate([da_in, db_in], axis=1), ((0, 0), (0, DN_AB_PAD - 2 * DN_HEADS)))
    dwqkv = _matmul(h, dpqkv, "tn", tag + "_dwqkv", out_dtype=BF16)
    dwgate = _matmul(h, dgate, "tn", tag + "_dwgate", out_dtype=BF16)
    dwab = _matmul(h, dpab, "tn", tag + "_dwab", out_dtype=BF16)
    dw_in = jnp.concatenate([dwqkv, dwgate, dwab[:, :2 * DN_HEADS]], axis=1)
    grads = dict(dn_w_in=dw_in, dn_conv_w=dconv, dn_a_log=da_log, dn_dt_bias=ddt, dn_o_norm_g=dgn, dn_w_out=dw_out)
    dx, landed_late = _matmul_nt_sum([(dpqkv, w_in, 0), (dgate, w_in, DN_CONV_W), (dpab, wab, 0)], tag + "_dh",
                                     late_comm_of(grads) if late_comm_of is not None else None, norm_bwd=norm)
    return dx, grads, landed, landed_late


def _sb_layer_fwd(h, w_in, qg, kg, w_out, x_res, tag, comm=None, next_g=None):
    qg2, kg2 = jnp.tile(qg, (1, 2)), jnp.tile(kg, (1, 2))
    proj = _matmul(h, w_in, "nn", tag + "_proj", blocked_b=True)
    qn, kn, vb = _sb_prep(proj, qg2, kg2, tag + "_prep")
    (o, og, ltot, done), landed = _sb_attn_fwd(qn, kn, vb, proj, tag + "_attn_fwd", comm)
    y = _out_proj(og, w_out, x_res, next_g, tag + "_out")
    saved = dict(h=h, w_in=w_in, qg2=qg2, kg2=kg2, w_out=w_out, proj=proj, qn=qn, kn=kn, vb=vb, o=o, og=og, ltot=ltot,
                 done=done)
    return y, saved, landed


def _sb_layer_bwd(dout, sv, tag, norm, comm=None):
    dog = _matmul(dout, sv["w_out"], "nt", tag + "_dog")
    dw_out = _matmul(sv["og"], dout, "tn", tag + "_dwout", out_dtype=BF16)
    (dqn, dkn, dv, dgate), landed = _sb_attn_bwd(sv["qn"], sv["kn"], sv["vb"], dog, sv["o"], sv["ltot"], sv["done"],
                                                 sv["proj"], tag + "_attn_bwd", comm)
    dproj, dqgp, dkgp = _sb_prep_bwd(sv["proj"], dqn, dkn, dv, dgate, sv["qg2"], sv["kg2"], tag + "_prep_bwd")
    dw_in = _matmul(sv["h"], dproj, "tn", tag + "_dwin", out_dtype=BF16, blocked_out=N_DEV)
    dx = _matmul(dproj, sv["w_in"], "nt", tag + "_dh", blocked_b=True, norm_bwd=norm, tm=NORM_FUSED_TM)
    dqg = _fold_heads(dqgp, tag + "_dqg")
    dkg = _fold_heads(dkgp, tag + "_dkg")
    return dx, dict(sb_w_in=dw_in, sb_q_norm_g=dqg, sb_k_norm_g=dkg, sb_w_out=dw_out), landed


def _sc_layer_fwd(h, w_in, conv_w, w_out, x_res, tag, next_g=None):
    proj = _matmul(h, w_in, "nn", tag + "_proj", blocked_b=True)
    yg = _sc_fwd(proj, conv_w, tag + "_fwd")
    y = _out_proj(yg, w_out, x_res, next_g, tag + "_out")
    return y, dict(h=h, w_in=w_in, conv_w=conv_w, w_out=w_out, proj=proj, yg=yg)


def _sc_layer_bwd(dout, sv, tag, norm):
    dyg = _matmul(dout, sv["w_out"], "nt", tag + "_dyg")
    dw_out = _matmul(sv["yg"], dout, "tn", tag + "_dwout", out_dtype=BF16)
    dproj, dconv = _sc_bwd(dyg, sv["proj"], sv["conv_w"], tag + "_bwd")
    dw_in = _matmul(sv["h"], dproj, "tn", tag + "_dwin", out_dtype=BF16, blocked_out=N_DEV)
    dx = _matmul(dproj, sv["w_in"], "nt", tag + "_dh", blocked_b=True, norm_bwd=norm, tm=NORM_FUSED_TM)
    return dx, dict(sc_w_in=dw_in, sc_conv_w=dconv, sc_w_out=dw_out)


def _adamw(w, m, v, parts, name):
    R, C = w.shape
    tr = _tile(R, 128, SUBLANE)

    def body(w_ref, m_ref, v_ref, p_ref, g_ref, d_ref, nm_ref, nv_ref):
        g = p_ref[0].astype(F32)
        for s in range(1, N_DEV):
            g = g + p_ref[s].astype(F32)
        m2 = ADAM_B1 * m_ref[...] + (1.0 - ADAM_B1) * g
        v2 = ADAM_B2 * v_ref[...] + (1.0 - ADAM_B2) * (g * g)
        m_hat = m2 / (1.0 - ADAM_B1 ** ADAM_STEP)
        v_hat = v2 / (1.0 - ADAM_B2 ** ADAM_STEP)
        g_ref[...] = g
        d_ref[...] = -ADAM_LR * (m_hat / (jnp.sqrt(v_hat) + ADAM_EPS) + ADAM_WD * w_ref[...])
        nm_ref[...] = m2
        nv_ref[...] = v2

    blk = pl.BlockSpec((tr, C), lambda i: (i, 0))
    return pl.pallas_call(
        body, name=name, grid=(R // tr,),
        in_specs=[blk, blk, blk, pl.BlockSpec((N_DEV, tr, C), lambda i: (0, i, 0))],
        out_specs=[blk] * 4, out_shape=[jax.ShapeDtypeStruct((R, C), F32)] * 4,
        compiler_params=_params("parallel"),
    )(w, m, v, parts)


_HBM = pl.BlockSpec(memory_space=pltpu.HBM)
_MESH = pl.DeviceIdType.MESH


def _slot(x, y, c):
    return 4 * x + 2 * y + c


class _Gather:
    def __init__(self, shards):
        self.arrays = list(shards)
        n = len(self.arrays)
        self.out_shapes = [jax.ShapeDtypeStruct((N_DEV,) + s.shape, s.dtype) for s in self.arrays]
        self.scratch = [pltpu.SemaphoreType.DMA((n, N_DEV - 1)), pltpu.SemaphoreType.DMA((n, N_DEV - 1)),
                        pltpu.SemaphoreType.DMA((n,))]

    def _parts(self, ins, outs, sems):
        send_sems, recv_sems, local_sems = sems
        n = len(self.arrays)
        x, y, c = lax.axis_index("x"), lax.axis_index("y"), lax.axis_index("c")
        me, sibling = (x, y, c), (x, y, 1 - c)
        chips = [(1 - x, y), (x, 1 - y), (1 - x, 1 - y)]

        def copy(a, k, block, to, src=None):
            dst = outs[a].at[_slot(*block)]
            return pltpu.make_async_remote_copy(src_ref=dst if src is None else src, dst_ref=dst,
                                                send_sem=send_sems.at[a, k], recv_sem=recv_sems.at[a, k],
                                                device_id=to, device_id_type=_MESH)

        mine = [pltpu.make_async_copy(ins[a], outs[a].at[_slot(*me)], local_sems.at[a]) for a in range(n)]
        first = []
        for a in range(n):
            first.append(copy(a, 0, me, sibling, src=ins[a]))
            first += [copy(a, 1 + j, me, (*chip, c), src=ins[a]) for j, chip in enumerate(chips)]
        return n, c, me, sibling, chips, copy, mine, first

    def start(self, ins, outs, sems):
        _, _, _, _, _, _, mine, first = self._parts(ins, outs, sems)
        for cp in mine + first:
            cp.start()

    def finish(self, ins, outs, sems):
        n, c, me, sibling, chips, copy, mine, first = self._parts(ins, outs, sems)
        passed = []
        for j, chip in enumerate(chips):
            for a in range(n):
                copy(a, 1 + j, (*chip, c), me).wait_recv()
                fwd = copy(a, 4 + j, (*chip, c), sibling)
                fwd.start()
                passed.append(fwd)
        for a in range(n):
            copy(a, 0, sibling, me).wait_recv()
            for j, chip in enumerate(chips):
                copy(a, 4 + j, (*chip, 1 - c), me).wait_recv()
        for cp in first + passed:
            cp.wait_send()
        for cp in mine:
            cp.wait()


class _Exchange:
    def __init__(self, arrays, scatter):
        self.arrays, self.scatter = list(arrays), list(scatter)
        n = len(self.arrays)
        shapes = [a.shape[1:] if s else a.shape for a, s in zip(self.arrays, self.scatter)]
        self.out_shapes = [jax.ShapeDtypeStruct((N_DEV,) + tuple(s), a.dtype) for s, a in zip(shapes, self.arrays)]
        self.scratch = [pltpu.SemaphoreType.DMA((n, N_DEV - 1)), pltpu.SemaphoreType.DMA((n, N_DEV - 1)),
                        pltpu.SemaphoreType.DMA((n,))]

    def _copies(self, ins, outs, sems):
        send_sems, recv_sems, local_sems = sems
        n, scatter = len(self.arrays), self.scatter
        x, y, c = lax.axis_index("x"), lax.axis_index("y"), lax.axis_index("c")
        me = _slot(x, y, c)
        copies = [pltpu.make_async_copy(ins[a].at[me] if scatter[a] else ins[a], outs[a].at[me], local_sems.at[a])
                  for a in range(n)]
        for r in range(1, N_DEV):
            px = 1 - x if r & 4 else x
            py = 1 - y if r & 2 else y
            pc = 1 - c if r & 1 else c
            for a in range(n):
                copies.append(pltpu.make_async_remote_copy(
                    src_ref=ins[a].at[_slot(px, py, pc)] if scatter[a] else ins[a], dst_ref=outs[a].at[me],
                    send_sem=send_sems.at[a, r - 1], recv_sem=recv_sems.at[a, r - 1],
                    device_id=(px, py, pc), device_id_type=_MESH))
        return copies

    def start(self, ins, outs, sems):
        for cp in self._copies(ins, outs, sems):
            cp.start()

    def finish(self, ins, outs, sems):
        for cp in self._copies(ins, outs, sems):
            cp.wait()


def _comm_call(comm, name):
    n = len(comm.arrays)

    def body(*refs):
        ins, outs, sems = refs[:n], refs[n:2 * n], refs[2 * n:]
        comm.start(ins, outs, sems)
        comm.finish(ins, outs, sems)

    return pl.pallas_call(body, name=name, in_specs=[_HBM] * n, out_specs=[_HBM] * n, out_shape=comm.out_shapes,
                          scratch_shapes=comm.scratch)(*comm.arrays)


def _call(body, comm, *, name, grid, in_specs, out_specs, out_shape, scratch_shapes, semantics, args):
    if comm is None:
        outs = pl.pallas_call(body, name=name, grid=grid, in_specs=in_specs, out_specs=out_specs, out_shape=out_shape,
                              scratch_shapes=scratch_shapes, compiler_params=_params(*semantics))(*args)
        return outs, []
    n_in, n_out, n_scr, n_c = len(in_specs), len(out_specs), len(scratch_shapes), len(comm.arrays)

    def fused(*refs):
        ins, refs = refs[:n_in], refs[n_in:]
        c_ins, refs = refs[:n_c], refs[n_c:]
        outs, refs = refs[:n_out], refs[n_out:]
        c_outs, refs = refs[:n_c], refs[n_c:]
        scr, sems = refs[:n_scr], refs[n_scr:]
        ids = [pl.program_id(d) for d in range(len(grid))]
        first = functools.reduce(jnp.logical_and, [i == 0 for i in ids])
        last = functools.reduce(jnp.logical_and, [i == g - 1 for i, g in zip(ids, grid)])

        @pl.when(first)
        def _():
            comm.start(c_ins, c_outs, sems)

        body(*ins, *outs, *scr)

        @pl.when(last)
        def _():
            comm.finish(c_ins, c_outs, sems)

    outs = pl.pallas_call(
        fused, name=name, grid=grid, in_specs=list(in_specs) + [_HBM] * n_c, out_specs=list(out_specs) + [_HBM] * n_c,
        out_shape=list(out_shape) + comm.out_shapes, scratch_shapes=list(scratch_shapes) + comm.scratch,
        compiler_params=_params(*["arbitrary"] * len(grid)))(*args, *comm.arrays)
    return outs[:n_out], outs[n_out:]


_GATHER_0 = (("dn_w_in", 0), ("dn_conv_w", 0), ("dn_o_norm_g", 0))
_GATHER_1 = (("dn_w_out", 0), ("sb_w_in", 0), ("sb_w_out", 0))
_GATHER_2 = (("sc_w_in", 0), ("sc_conv_w", 0), ("sc_w_out", 0), ("dn_w_in", 1), ("dn_conv_w", 1), ("dn_o_norm_g", 1),
             ("dn_w_out", 1))
_EXCHANGE_A = _GATHER_2
_EXCHANGE_B = (("sb_w_in", 0), ("sb_w_out", 0), ("dn_w_out", 0))
_EXCHANGE_C = _GATHER_0
_MATMUL_WEIGHTS = ("dn_w_in", "dn_w_out", "sb_w_in", "sb_w_out", "sc_w_in", "sc_w_out")
_COLUMN_SHARDED = ("dn_w_in", "dn_conv_w", "dn_o_norm_g", "sb_w_in", "sc_w_in", "sc_conv_w")
_BLOCKED = ("sb_w_in", "sc_w_in")
_REPLICATED = ("norm_g", "dn_a_log", "dn_dt_bias", "sb_q_norm_g", "sb_k_norm_g")
_ORDER = ("norm_g", "dn_w_in", "dn_conv_w", "dn_a_log", "dn_dt_bias", "dn_o_norm_g", "dn_w_out", "sb_w_in", "sb_q_norm_g",
          "sb_k_norm_g", "sb_w_out", "sc_w_in", "sc_conv_w", "sc_w_out")
_PACK_COLS = D_MODEL


def _as_2d(a):
    return a.reshape(1, -1) if a.ndim == 1 else a


def _assemble(name, gathered):
    n, r, c = gathered.shape
    if name in _COLUMN_SHARDED:
        return jnp.moveaxis(gathered, 0, 1).reshape(r, n * c)
    return gathered.reshape(n * r, c)


def _disassemble(name, full):
    r, c = full.shape
    if name in _COLUMN_SHARDED:
        return jnp.moveaxis(full.reshape(r, N_DEV, c // N_DEV), 1, 0)
    return full.reshape(N_DEV, r // N_DEV, c)


def _pack_replicated(d):
    rows = [d["norm_g"]]
    for name in _REPLICATED[1:]:
        flat = d[name].reshape(1, -1)
        rows.append(jnp.pad(flat, ((0, 0), (0, _PACK_COLS - flat.shape[1]))))
    return jnp.concatenate(rows, axis=0)


def _unpack_replicated(p, like):
    out = {"norm_g": p[:4]}
    for r, name in enumerate(_REPLICATED[1:]):
        shape = like[name].shape
        out[name] = p[4 + r, :math.prod(shape)].reshape(shape)
    return out


def kernel(x, norm_g, dn_w_in, dn_conv_w, dn_a_log, dn_dt_bias, dn_o_norm_g, dn_w_out, sb_w_in, sb_q_norm_g, sb_k_norm_g, sb_w_out, sc_w_in, sc_conv_w, sc_w_out, loss_target, m_norm_g, m_dn_w_in, m_dn_conv_w, m_dn_a_log, m_dn_dt_bias, m_dn_o_norm_g, m_dn_w_out, m_sb_w_in, m_sb_q_norm_g, m_sb_k_norm_g, m_sb_w_out, m_sc_w_in, m_sc_conv_w, m_sc_w_out, v_norm_g, v_dn_w_in, v_dn_conv_w, v_dn_a_log, v_dn_dt_bias, v_dn_o_norm_g, v_dn_w_out, v_sb_w_in, v_sb_q_norm_g, v_sb_k_norm_g, v_sb_w_out, v_sc_w_in, v_sc_conv_w, v_sc_w_out):
    w = dict(norm_g=norm_g, dn_w_in=dn_w_in, dn_conv_w=dn_conv_w, dn_a_log=dn_a_log, dn_dt_bias=dn_dt_bias,
             dn_o_norm_g=dn_o_norm_g, dn_w_out=dn_w_out, sb_w_in=sb_w_in, sb_q_norm_g=sb_q_norm_g, sb_k_norm_g=sb_k_norm_g,
             sb_w_out=sb_w_out, sc_w_in=sc_w_in, sc_conv_w=sc_conv_w, sc_w_out=sc_w_out)
    m = dict(norm_g=m_norm_g, dn_w_in=m_dn_w_in, dn_conv_w=m_dn_conv_w, dn_a_log=m_dn_a_log, dn_dt_bias=m_dn_dt_bias,
             dn_o_norm_g=m_dn_o_norm_g, dn_w_out=m_dn_w_out, sb_w_in=m_sb_w_in, sb_q_norm_g=m_sb_q_norm_g,
             sb_k_norm_g=m_sb_k_norm_g, sb_w_out=m_sb_w_out, sc_w_in=m_sc_w_in, sc_conv_w=m_sc_conv_w, sc_w_out=m_sc_w_out)
    v = dict(norm_g=v_norm_g, dn_w_in=v_dn_w_in, dn_conv_w=v_dn_conv_w, dn_a_log=v_dn_a_log, dn_dt_bias=v_dn_dt_bias,
             dn_o_norm_g=v_dn_o_norm_g, dn_w_out=v_dn_w_out, sb_w_in=v_sb_w_in, sb_q_norm_g=v_sb_q_norm_g,
             sb_k_norm_g=v_sb_k_norm_g, sb_w_out=v_sb_w_out, sc_w_in=v_sc_w_in, sc_conv_w=v_sc_conv_w, sc_w_out=v_sc_w_out)

    def gather_of(keys):
        return _Gather([_as_2d(w[k][j]).astype(BF16) if k in _MATMUL_WEIGHTS else _as_2d(w[k][j]) for k, j in keys])

    def full_weights(keys, gathered):
        return {key: g if key[0] in _BLOCKED else _assemble(key[0], g) for key, g in zip(keys, gathered)}

    def exchange_of(keys, grads, extra=()):
        out = [grads[k, j] if k in _BLOCKED else
               _disassemble(k, grads[k, j].astype(BF16) if k in _MATMUL_WEIGHTS else grads[k, j]) for k, j in keys]
        return _Exchange(out + list(extra), [True] * len(out) + [False] * len(extra))

    F = full_weights(_GATHER_0, _comm_call(gather_of(_GATHER_0), "gather_first"))
    xs, saves = [x[0]], []
    h = _rmsnorm_fwd(xs[0], norm_g[0:1], "norm0")

    def w_out_0(got):
        F.update(full_weights(_GATHER_1, got))
        return F["dn_w_out", 0]

    (y, h), sv, _ = _dn_layer_fwd(h, _dn_split_w_in(F["dn_w_in", 0]), F["dn_conv_w", 0], dn_a_log[0:1], dn_dt_bias[0:1],
                                  F["dn_o_norm_g", 0], w_out_0, xs[0], "dn0", gather_of(_GATHER_1), norm_g[1:2])
    xs.append(y)
    saves.append(sv)
    (y, h), sv, got = _sb_layer_fwd(h, F["sb_w_in", 0], sb_q_norm_g, sb_k_norm_g, F["sb_w_out", 0], xs[1], "sb",
                                    gather_of(_GATHER_2), norm_g[2:3])
    F.update(full_weights(_GATHER_2, got))
    xs.append(y)
    saves.append(sv)
    (y, h), sv = _sc_layer_fwd(h, F["sc_w_in", 0], F["sc_conv_w", 0], F["sc_w_out", 0], xs[2], "sc", norm_g[3:4])
    xs.append(y)
    saves.append(sv)
    (y, _), sv, _ = _dn_layer_fwd(h, _dn_split_w_in(F["dn_w_in", 1]), F["dn_conv_w", 1], dn_a_log[1:2], dn_dt_bias[1:2],
                                  F["dn_o_norm_g", 1], F["dn_w_out", 1], xs[3], "dn1")
    xs.append(y)
    saves.append(sv)
    dx, loss_part = _loss_head(xs[4], loss_target[0])

    G, dnorm, landed = {}, [None] * 4, {}

    def keep(grads, j):
        G.update({(k, j): g for k, g in grads.items()})

    (dx, dnorm[3]), grads, _, _ = _dn_layer_bwd(dx, saves[3], "dn1", (xs[3], norm_g[3:4], dx))
    keep(grads, 1)
    (dx, dnorm[2]), grads = _sc_layer_bwd(dx, saves[2], "sc", (xs[2], norm_g[2:3], dx))
    keep(grads, 0)
    (dx, dnorm[1]), grads, got = _sb_layer_bwd(dx, saves[1], "sb", (xs[1], norm_g[1:2], dx), exchange_of(_EXCHANGE_A, G))
    keep(grads, 0)
    landed.update(zip(_EXCHANGE_A, got))

    def exchange_b(dw_out):
        G["dn_w_out", 0] = dw_out
        return exchange_of(_EXCHANGE_B, G)

    def exchange_c(grads):
        keep(grads, 0)
        return exchange_of(_EXCHANGE_C, G)

    (dx, dnorm[0]), grads, got, got_late = _dn_layer_bwd(dx, saves[0], "dn0", (xs[0], norm_g[0:1], dx), exchange_b, exchange_c)
    landed.update(zip(_EXCHANGE_B, got))
    landed.update(zip(_EXCHANGE_C, got_late))
    replicated = dict(norm_g=jnp.concatenate(dnorm, axis=0),
                      dn_a_log=jnp.concatenate([G["dn_a_log", 0], G["dn_a_log", 1]], axis=0),
                      dn_dt_bias=jnp.concatenate([G["dn_dt_bias", 0], G["dn_dt_bias", 1]], axis=0),
                      sb_q_norm_g=G["sb_q_norm_g", 0], sb_k_norm_g=G["sb_k_norm_g", 0])
    got = _comm_call(_Exchange([_pack_replicated(replicated)], [False]), "exchange_replicated")

    res = {}
    for k in _ORDER:
        if k in _REPLICATED:
            continue
        per_layer = []
        for j in range(w[k].shape[0]):
            shape = w[k][j].shape
            outs = _adamw(_as_2d(w[k][j]), _as_2d(m[k][j]), _as_2d(v[k][j]), landed[k, j], f"adamw_{k}{j}")
            per_layer.append([o.reshape(shape) for o in outs])
        res[k] = [jnp.stack([layer[i] for layer in per_layer], axis=0) for i in range(4)]
    outs = _adamw(_pack_replicated(w), _pack_replicated(m), _pack_replicated(v), got[-1], "adamw_replicated")
    unpacked = [_unpack_replicated(o, w) for o in outs]
    for k in _REPLICATED:
        res[k] = [u[k] for u in unpacked]

    loss = lax.psum(loss_part[0, 0], ("x", "y", "c"))
    return (loss, dx[None]) + tuple(res[k][0] for k in _ORDER) + tuple(res[k][1] for k in _ORDER) \
        + tuple(res[k][2] for k in _ORDER) + tuple(res[k][3] for k in _ORDER)
```

```python
import functools
import itertools
import math

import jax
import jax.numpy as jnp
from jax import lax
from jax.experimental import pallas as pl
from jax.experimental.pallas import tpu as pltpu

F32 = jnp.float32
BF16 = jnp.bfloat16
HIGHEST = lax.Precision.HIGHEST

N_DEV = 8
D_MODEL = 1024
RMS_EPS = 1e-6
L2_EPS = 1e-6

DN_HEADS = 8
DN_DK = 128
DN_DV = 256
DN_QK_W = DN_HEADS * DN_DK
DN_V_W = DN_HEADS * DN_DV
DN_CONV = 4
DN_CHUNK = 64
DN_CONV_W = 2 * DN_QK_W + DN_V_W
DN_IN = DN_CONV_W + DN_V_W + 2 * DN_HEADS
DN_AB_PAD = 128
DN_PREP_BLK = 512

SB_HEADS = 16
SB_DH = 64
SB_W = SB_HEADS * SB_DH
SB_PAIRS = SB_HEADS // 2
SB_TQ = 256
SB_TK = 128
SB_DEAD = -106.0

SC_W = 2 * D_MODEL
SC_CONV = 3
SC_BLK = 512
SC_NBLK = SC_W // SC_BLK

ADAM_LR = 0.001
ADAM_B1 = 0.9
ADAM_B2 = 0.999
ADAM_EPS = 1e-08
ADAM_WD = 0.01
ADAM_STEP = 10

LANE = 128
SUBLANE = 8
HALO = SUBLANE
LONG_ROW_TILE = 512
NORM_FUSED_TM = 512
DEEP_TK = 2048
WIDE_TN = 2048
WIDE_ROW_TILE = 128
CONV_ROW_TILE = 256
VMEM_LIMIT = 48 * 2 ** 20

NN = ((1,), (0,))
NT = ((1,), (1,))
TN = ((0,), (0,))


def _dot(a, b, dims=NN, precision=None):
    return lax.dot_general(a, b, (dims, ((), ())), precision=precision, preferred_element_type=F32)


def _bdot(a, b, dims=NN):
    return _dot(a.astype(BF16), b.astype(BF16), dims)


def _hdot(a, b, dims=NN):
    return _dot(a, b, dims, precision=HIGHEST)


def _tile(dim, pref, align=LANE):
    t = (min(pref, dim) // align) * align
    while t >= align:
        if dim % t == 0:
            return t
        t -= align
    return dim


def _params(*sem):
    return pltpu.CompilerParams(dimension_semantics=sem, vmem_limit_bytes=VMEM_LIMIT)


def _sigmoid(x):
    return 0.5 * jnp.tanh(0.5 * x) + 0.5


def _softplus(x):
    return jnp.maximum(x, 0.0) + jnp.log(1.0 + jnp.exp(-jnp.abs(x)))


def _silu_and_grad(x):
    s = _sigmoid(x)
    return x * s, s * (1.0 + x * (1.0 - s))


def _iota2(shape, dim):
    return lax.broadcasted_iota(jnp.int32, shape, dim)


def _matmul(a, b, mode, name, out_dtype=F32, add=None, b_cols=None, blocked_b=False, blocked_out=0,
            norm_fwd=None, norm_bwd=None, tm=1024, tn=1024, tk=1024):
    b_rows, b_width = (b.shape[1], b.shape[0] * b.shape[2]) if blocked_b else b.shape
    c0, b_used = b_cols if b_cols is not None else (0, b_width)
    if mode == "nn":
        (M, K), (K2, N) = a.shape, (b_rows, b_used)
    elif mode == "nt":
        (M, K), (N, K2) = a.shape, (b_rows, b_used)
    else:
        (K, M), (K2, N) = a.shape, (b_rows, b_used)
    assert K == K2, (a.shape, b.shape, mode)
    if mode == "tn":
        tk = max(tk, DEEP_TK)
    elif mode == "nn" and norm_fwd is None and add is None:
        tn = max(tn, WIDE_TN)
    tm, tn, tk = _tile(M, tm), _tile(N, tn), _tile(K, tk)
    if blocked_b and mode == "nt":
        tk = b.shape[2]
    elif blocked_b:
        tn = b.shape[2]
    if blocked_out:
        tn = N // blocked_out
    nk = K // tk
    dims = {"nn": NN, "nt": NT, "tn": TN}[mode]
    a_spec = pl.BlockSpec((tk, tm), lambda i, j, k: (k, i)) if mode == "tn" else pl.BlockSpec((tm, tk), lambda i, j, k: (i, k))
    if mode == "nt":
        cb0 = c0 // tk
        assert c0 % tk == 0
        b_spec = (pl.BlockSpec((None, tn, tk), lambda i, j, k: (k + cb0, j, 0)) if blocked_b
                  else pl.BlockSpec((tn, tk), lambda i, j, k: (j, k + cb0)))
    else:
        cb0 = c0 // tn
        assert c0 % tn == 0
        b_spec = (pl.BlockSpec((None, tk, tn), lambda i, j, k: (j + cb0, k, 0)) if blocked_b
                  else pl.BlockSpec((tk, tn), lambda i, j, k: (k, j + cb0)))
    o_spec = pl.BlockSpec((tm, tn), lambda i, j, k: (i, j))
    out_spec = pl.BlockSpec((None, tm, tn), lambda i, j, k: (j, i, 0)) if blocked_out else o_spec
    out_shape = (blocked_out, M, tn) if blocked_out else (M, N)
    has_add = add is not None
    vec_spec = pl.BlockSpec((1, tn), lambda i, j, k: (0, j))
    assert not (norm_fwd is not None or norm_bwd is not None) or tn == N
    extra_in, extra_specs = [], []
    if has_add:
        extra_in, extra_specs = [add], [o_spec]
    if norm_fwd is not None:
        extra_in, extra_specs = extra_in + [norm_fwd], extra_specs + [vec_spec]
        out_specs = [o_spec, o_spec]
        out_shapes = [jax.ShapeDtypeStruct((M, N), out_dtype), jax.ShapeDtypeStruct((M, N), BF16)]
    elif norm_bwd is not None:
        extra_in, extra_specs = extra_in + list(norm_bwd), extra_specs + [o_spec, vec_spec, o_spec]
        out_specs = [o_spec, vec_spec]
        out_shapes = [jax.ShapeDtypeStruct((M, N), F32), jax.ShapeDtypeStruct((1, N), F32)]
    else:
        out_specs, out_shapes = out_spec, jax.ShapeDtypeStruct(out_shape, out_dtype)

    def body(*refs):
        a_ref, b_ref = refs[0], refs[1]
        extra = list(refs[2:2 + len(extra_in)])
        outs = refs[2 + len(extra_in):]
        add_ref = extra.pop(0) if has_add else None
        p = _bdot(a_ref[...], b_ref[...], dims)

        def finish(acc):
            if has_add:
                acc = acc + add_ref[...]
            if norm_bwd is not None:
                _rmsnorm_bwd_tile(acc, *extra, outs[0], outs[1], first=pl.program_id(0) == 0)
                return
            outs[0][...] = acc.astype(out_dtype)
            if norm_fwd is not None:
                r = lax.rsqrt(jnp.mean(acc * acc, axis=-1, keepdims=True) + RMS_EPS)
                outs[1][...] = (acc * r * extra[0][...]).astype(BF16)

        if nk == 1:
            finish(p)
        else:
            acc_ref = refs[-1]
            k = pl.program_id(2)

            @pl.when(k == 0)
            def _():
                acc_ref[...] = p

            @pl.when(k > 0)
            def _():
                acc_ref[...] += p

            @pl.when(k == nk - 1)
            def _():
                finish(acc_ref[...])

    return pl.pallas_call(
        body, name=name, grid=(M // tm, N // tn, nk),
        in_specs=[a_spec, b_spec] + extra_specs, out_specs=out_specs, out_shape=out_shapes,
        scratch_shapes=[pltpu.VMEM((tm, tn), F32)] if nk > 1 else [],
        compiler_params=(_params("arbitrary", "arbitrary", "arbitrary") if norm_bwd is not None
                         else _params("parallel", "parallel", "arbitrary")),
    )(a, b, *extra_in)


def _rmsnorm_bwd_tile(dh, x_ref, g_ref, res_ref, dx_ref, dg_ref, first):
    xv = x_ref[...]
    r = lax.rsqrt(jnp.mean(xv * xv, axis=-1, keepdims=True) + RMS_EPS)
    xh = xv * r
    dxh = dh * g_ref[...]
    m = jnp.mean(dxh * xh, axis=-1, keepdims=True)
    dx_ref[...] = res_ref[...] + r * (dxh - xh * m)
    part = jnp.sum(dh * xh, axis=0, keepdims=True)

    @pl.when(first)
    def _():
        dg_ref[...] = part

    @pl.when(jnp.logical_not(first))
    def _():
        dg_ref[...] += part


def _matmul_nt_sum(pairs, name, comm=None, norm_bwd=None, tm=NORM_FUSED_TM, tk=1024):
    M, N = pairs[0][0].shape[0], pairs[0][1].shape[0]
    tm = _tile(M, tm)
    tks = [_tile(a.shape[1], tk) for a, _, _ in pairs]
    steps = [a.shape[1] // t for (a, _, _), t in zip(pairs, tks)]
    offs = [sum(steps[:p]) for p in range(len(pairs))]
    total = sum(steps)

    n_extra = 3 if norm_bwd is not None else 0

    def body(*refs):
        a_refs, b_refs = refs[0:2 * len(pairs):2], refs[1:2 * len(pairs):2]
        extra = refs[2 * len(pairs):2 * len(pairs) + n_extra]
        outs, acc_ref = refs[2 * len(pairs) + n_extra:-1], refs[-1]
        k = pl.program_id(1)
        for p in range(len(pairs)):
            @pl.when((k >= offs[p]) & (k < offs[p] + steps[p]))
            def _(p=p):
                prod = _bdot(a_refs[p][...], b_refs[p][...], NT)
                if p == 0:
                    @pl.when(k == 0)
                    def _():
                        acc_ref[...] = prod

                    @pl.when(k > 0)
                    def _():
                        acc_ref[...] += prod
                else:
                    acc_ref[...] += prod

        @pl.when(k == total - 1)
        def _():
            if norm_bwd is not None:
                _rmsnorm_bwd_tile(acc_ref[...], *extra, outs[0], outs[1], first=pl.program_id(0) == 0)
            else:
                outs[0][...] = acc_ref[...]

    in_specs, args = [], []
    for (a, b, c0), t, off, n in zip(pairs, tks, offs, steps):
        assert c0 % t == 0
        pick = lambda k, off=off, n=n: jnp.clip(k - off, 0, n - 1)
        in_specs += [pl.BlockSpec((tm, t), lambda i, k, pick=pick: (i, pick(k))),
                     pl.BlockSpec((N, t), lambda i, k, pick=pick, cb0=c0 // t: (0, pick(k) + cb0))]
        args += [a, b]
    row, vec = pl.BlockSpec((tm, N), lambda i, k: (i, 0)), pl.BlockSpec((1, N), lambda i, k: (0, 0))
    if norm_bwd is not None:
        in_specs += [row, vec, row]
        args += list(norm_bwd)
        out_specs, out_shape = [row, vec], [jax.ShapeDtypeStruct((M, N), F32), jax.ShapeDtypeStruct((1, N), F32)]
    else:
        out_specs, out_shape = [row], [jax.ShapeDtypeStruct((M, N), F32)]
    outs, landed = _call(body, comm, name=name, grid=(M // tm, total), in_specs=in_specs, out_specs=out_specs,
                         out_shape=out_shape, scratch_shapes=[pltpu.VMEM((tm, N), F32)],
                         semantics=("arbitrary", "arbitrary"), args=tuple(args))
    return (outs if norm_bwd is not None else outs[0]), landed


def _rmsnorm_fwd(x, g, name):
    T, D = x.shape
    tt = _tile(T, LONG_ROW_TILE, SUBLANE)

    def body(x_ref, g_ref, o_ref):
        xv = x_ref[...]
        r = lax.rsqrt(jnp.mean(xv * xv, axis=-1, keepdims=True) + RMS_EPS)
        o_ref[...] = (xv * r * g_ref[...]).astype(BF16)

    return pl.pallas_call(
        body, name=name, grid=(T // tt,),
        in_specs=[pl.BlockSpec((tt, D), lambda i: (i, 0)), pl.BlockSpec((1, D), lambda i: (0, 0))],
        out_specs=pl.BlockSpec((tt, D), lambda i: (i, 0)),
        out_shape=jax.ShapeDtypeStruct((T, D), BF16),
        compiler_params=_params("parallel"),
    )(x, g)


def _loss_head(y, target, name="loss_head"):
    T, D = y.shape
    tt = _tile(T, LONG_ROW_TILE, SUBLANE)

    def body(y_ref, t_ref, dy_ref, l_ref):
        e = y_ref[...] - t_ref[...]
        dy_ref[...] = e * (1.0 / D)
        s = jnp.sum(jnp.sum(e * e, axis=1, keepdims=True), axis=0, keepdims=True) * (0.5 / D)
        s = jnp.broadcast_to(s, (1, LANE))

        @pl.when(pl.program_id(0) == 0)
        def _():
            l_ref[...] = s

        @pl.when(pl.program_id(0) > 0)
        def _():
            l_ref[...] += s

    row = pl.BlockSpec((tt, D), lambda i: (i, 0))
    return pl.pallas_call(
        body, name=name, grid=(T // tt,),
        in_specs=[row, row], out_specs=[row, pl.BlockSpec((1, LANE), lambda i: (0, 0))],
        out_shape=[jax.ShapeDtypeStruct((T, D), F32), jax.ShapeDtypeStruct((1, LANE), F32)],
        compiler_params=_params("arbitrary"),
    )(y, target)


def _down(x, k):
    return pltpu.roll(x, k, 0) if k else x


def _up(x, k):
    return pltpu.roll(x, x.shape[0] - k, 0) if k else x


def _sc_fwd(proj, conv_w, name):
    T = proj.shape[0]
    tt = _tile(T, WIDE_ROW_TILE, SUBLANE)
    B = SC_BLK

    def body(p_ref, ph_ref, w_ref, o_ref):
        keep = (pl.program_id(0) > 0).astype(F32)
        for j in range(SC_NBLK):
            cb, cc, cu, cg = (slice(k * SC_W + j * B, k * SC_W + (j + 1) * B) for k in range(4))
            cw = slice(j * B, (j + 1) * B)
            z = jnp.concatenate([ph_ref[:, cc] * ph_ref[:, cu] * keep, p_ref[:, cc] * p_ref[:, cu]], axis=0)
            cz = (w_ref[2:3, cw] * z + w_ref[1:2, cw] * _down(z, 1) + w_ref[0:1, cw] * _down(z, 2))[HALO:]
            gate = p_ref[:, cg]
            o_ref[:, cw] = (p_ref[:, cb] * cz * (gate * _sigmoid(gate))).astype(BF16)

    return pl.pallas_call(
        body, name=name, grid=(T // tt,),
        in_specs=[pl.BlockSpec((tt, 4 * SC_W), lambda i: (i, 0)),
                  pl.BlockSpec((HALO, 4 * SC_W), lambda i: (jnp.maximum(i * (tt // HALO) - 1, 0), 0)),
                  pl.BlockSpec((SC_CONV, SC_W), lambda i: (0, 0))],
        out_specs=pl.BlockSpec((tt, SC_W), lambda i: (i, 0)),
        out_shape=jax.ShapeDtypeStruct((T, SC_W), BF16),
        compiler_params=_params("parallel"),
    )(proj, proj, conv_w)


def _sc_bwd(dyg, proj, conv_w, name):
    T = proj.shape[0]
    tt = _tile(T, WIDE_ROW_TILE, SUBLANE)
    nt = T // tt
    B = SC_BLK
    hb = tt // HALO

    def body(d_ref, dn_ref, p_ref, pp_ref, pn_ref, w_ref, o_ref, dw_ref):
        i = pl.program_id(0)
        keep_p = (i > 0).astype(F32)
        keep_n = (i < nt - 1).astype(F32)
        main = slice(HALO, HALO + tt)
        parts = []
        for j in range(SC_NBLK):
            cw = slice(j * B, (j + 1) * B)

            def ext(k):
                s = slice(k * SC_W + j * B, k * SC_W + (j + 1) * B)
                return s, jnp.concatenate([pp_ref[:, s] * keep_p, p_ref[:, s], pn_ref[:, s]], axis=0)

            (sb, b), (sc, c), (su, u), (sg_, gate) = ext(0), ext(1), ext(2), ext(3)
            dyg_e = jnp.concatenate([jnp.zeros((HALO, B), F32), d_ref[:, cw], dn_ref[:, cw] * keep_n], axis=0)
            w0, w1, w2 = w_ref[0:1, cw], w_ref[1:2, cw], w_ref[2:3, cw]
            z = c * u
            z1, z2 = _down(z, 1), _down(z, 2)
            cz = w2 * z + w1 * z1 + w0 * z2
            sg, dsg = _silu_and_grad(gate)
            dy = dyg_e * sg
            dcz = dy * b
            dz = w2 * dcz + w1 * _up(dcz, 1) + w0 * _up(dcz, 2)
            o_ref[:, sb] = (dy * cz)[main].astype(BF16)
            o_ref[:, sc] = (dz * u)[main].astype(BF16)
            o_ref[:, su] = (dz * c)[main].astype(BF16)
            o_ref[:, sg_] = (dyg_e * (b * cz) * dsg)[main].astype(BF16)
            dcm = dcz[main]
            parts.append(jnp.concatenate([jnp.sum(dcm * z2[main], axis=0, keepdims=True),
                                          jnp.sum(dcm * z1[main], axis=0, keepdims=True),
                                          jnp.sum(dcm * z[main], axis=0, keepdims=True)], axis=0))
        part = jnp.concatenate(parts, axis=1)

        @pl.when(i == 0)
        def _():
            dw_ref[...] = part

        @pl.when(i > 0)
        def _():
            dw_ref[...] += part

    nxt = lambda i: (jnp.minimum((i + 1) * hb, nt * hb - 1), 0)
    return pl.pallas_call(
        body, name=name, grid=(nt,),
        in_specs=[pl.BlockSpec((tt, SC_W), lambda i: (i, 0)),
                  pl.BlockSpec((HALO, SC_W), nxt),
                  pl.BlockSpec((tt, 4 * SC_W), lambda i: (i, 0)),
                  pl.BlockSpec((HALO, 4 * SC_W), lambda i: (jnp.maximum(i * hb - 1, 0), 0)),
                  pl.BlockSpec((HALO, 4 * SC_W), nxt),
                  pl.BlockSpec((SC_CONV, SC_W), lambda i: (0, 0))],
        out_specs=[pl.BlockSpec((tt, 4 * SC_W), lambda i: (i, 0)), pl.BlockSpec((SC_CONV, SC_W), lambda i: (0, 0))],
        out_shape=[jax.ShapeDtypeStruct((T, 4 * SC_W), BF16), jax.ShapeDtypeStruct((SC_CONV, SC_W), F32)],
        compiler_params=_params("arbitrary"),
    )(dyg, dyg, proj, proj, proj, conv_w)


def _split3_dot(x, m):
    hi = x.astype(BF16)
    r1 = x - hi.astype(F32)
    mid = r1.astype(BF16)
    lo = (r1 - mid.astype(F32)).astype(BF16)
    return _dot(hi, m) + _dot(mid, m) + _dot(lo, m)


def _split2_dot(x, m):
    hi = x.astype(BF16)
    lo = (x - hi.astype(F32)).astype(BF16)
    return _dot(hi, m) + _dot(lo, m)


def _head_mean_matrix():
    r, c = _iota2((LANE, LANE), 0), _iota2((LANE, LANE), 1)
    return jnp.where((r // SB_DH) == (c // SB_DH), 1.0 / SB_DH, 0.0).astype(BF16)


def _sb_prep(proj, qg2, kg2, name):
    T = proj.shape[0]
    tt = _tile(T, WIDE_ROW_TILE, SUBLANE)

    def body(p_ref, qg_ref, kg_ref, q_ref, k_ref, v_ref):
        bd = _head_mean_matrix()

        def norm(x, g, scale):
            r = lax.rsqrt(_split3_dot(x * x, bd) + RMS_EPS)
            return (x * r * g * scale).astype(BF16)

        v_ref[...] = p_ref[:, 2 * SB_W:3 * SB_W].astype(BF16)
        for p in range(SB_PAIRS):
            cols = slice(p * LANE, (p + 1) * LANE)
            q_ref[:, cols] = norm(p_ref[:, cols], qg_ref[...], SB_DH ** -0.5)
            k_ref[:, cols] = norm(p_ref[:, SB_W + p * LANE:SB_W + (p + 1) * LANE], kg_ref[...], 1.0)

    blk = pl.BlockSpec((tt, SB_W), lambda i: (i, 0))
    vec = pl.BlockSpec((1, LANE), lambda i: (0, 0))
    return pl.pallas_call(
        body, name=name, grid=(T // tt,),
        in_specs=[pl.BlockSpec((tt, 4 * SB_W), lambda i: (i, 0)), vec, vec],
        out_specs=[blk, blk, blk],
        out_shape=[jax.ShapeDtypeStruct((T, SB_W), BF16)] * 3,
        compiler_params=_params("parallel"),
    )(proj, qg2, kg2)


def _sb_prep_bwd(proj, dqn, dkn, dv, dgate, qg2, kg2, name):
    T = proj.shape[0]
    tt = _tile(T, WIDE_ROW_TILE, SUBLANE)

    def body(p_ref, dq_ref, dk_ref, dv_ref, dg_ref, qg_ref, kg_ref, o_ref, dqg_ref, dkg_ref):
        i = pl.program_id(0)
        bd = _head_mean_matrix()

        def norm_bwd(x, g, dy):
            r = lax.rsqrt(_split3_dot(x * x, bd) + RMS_EPS)
            xh = x * r
            dxh = dy * g
            m = _split3_dot(dxh * xh, bd)
            return r * (dxh - xh * m), jnp.sum(dy * xh, axis=0, keepdims=True)

        o_ref[:, 2 * SB_W:3 * SB_W] = dv_ref[...].astype(BF16)
        o_ref[:, 3 * SB_W:4 * SB_W] = dg_ref[...].astype(BF16)
        pq = pk = jnp.zeros((1, LANE), F32)
        for p in range(SB_PAIRS):
            cols, kcols = slice(p * LANE, (p + 1) * LANE), slice(SB_W + p * LANE, SB_W + (p + 1) * LANE)
            dxq, sq = norm_bwd(p_ref[:, cols], qg_ref[...], dq_ref[:, cols])
            dxk, sk = norm_bwd(p_ref[:, kcols], kg_ref[...], dk_ref[:, cols])
            o_ref[:, cols] = dxq.astype(BF16)
            o_ref[:, kcols] = dxk.astype(BF16)
            pq, pk = pq + sq, pk + sk

        @pl.when(i == 0)
        def _():
            dqg_ref[...] = pq
            dkg_ref[...] = pk

        @pl.when(i > 0)
        def _():
            dqg_ref[...] += pq
            dkg_ref[...] += pk

    blk = pl.BlockSpec((tt, SB_W), lambda i: (i, 0))
    vec = pl.BlockSpec((1, LANE), lambda i: (0, 0))
    wide = pl.BlockSpec((tt, 4 * SB_W), lambda i: (i, 0))
    return pl.pallas_call(
        body, name=name, grid=(T // tt,),
        in_specs=[wide, blk, blk, blk, blk, vec, vec],
        out_specs=[wide, vec, vec],
        out_shape=[jax.ShapeDtypeStruct((T, 4 * SB_W), BF16)] + [jax.ShapeDtypeStruct((1, LANE), F32)] * 2,
        compiler_params=_params("arbitrary"),
    )(proj, dqn, dkn, dv, dgate, qg2, kg2)


def _fold_heads(part, name):
    def body(p_ref, o_ref):
        r, c = _iota2((LANE, SB_DH), 0), _iota2((LANE, SB_DH), 1)
        fold = jnp.where((r % SB_DH) == c, 1.0, 0.0).astype(F32)
        o_ref[...] = jnp.sum(_hdot(p_ref[...], fold), axis=0, keepdims=True)

    return pl.pallas_call(body, name=name, out_shape=jax.ShapeDtypeStruct((1, SB_DH), F32))(part)


def _sb_masks():
    lane = _iota2((1, LANE), 1)
    return lane < SB_DH


def _sb_attn_fwd(qn, kn, vb, proj, name, comm=None):
    T = qn.shape[0]
    tq, tk = _tile(T, SB_TQ, SUBLANE), SB_TK
    assert tq % tk == 0

    def body(q_ref, k_ref, v_ref, g_ref, o_ref, og_ref, lt_ref, done_ref):
        i = pl.program_id(1)
        ma = _sb_masks()
        q2 = q_ref[...]
        zero = jnp.zeros_like(q2)
        qs = (jnp.where(ma, q2, zero), jnp.where(ma, zero, q2))
        upper = (_iota2((tk, tk), 0) > _iota2((tk, tk), 1)).astype(BF16)
        qpos = i * tq + _iota2((tq, tk), 0)
        nb = tq // tk

        def trip(kb_top, masked, carry):
            acc, la, lb = carry
            chains = [(b, h) for b in range(nb) for h in range(2)]
            k2s, vss, masks = [], [], []
            for b in range(nb):
                kb = kb_top - b
                rows = pl.ds(pl.multiple_of(kb * tk, tk), tk)
                k2s.append(k_ref[rows, :])
                v2 = v_ref[rows, :]
                zv = jnp.zeros_like(v2)
                vss.append((jnp.where(ma, v2, zv), jnp.where(ma, zv, v2)))
                masks.append((kb * tk + _iota2((tq, tk), 1)) < qpos if masked else None)
            zs = [_dot(qs[h], k2s[b], NT) for b, h in chains]
            ts = [jnp.log(1.0 + jnp.exp(-jnp.abs(z))) for z in zs]
            ls = [-(jnp.maximum(z, 0.0) + t) for z, t in zip(zs, ts)]
            if masked:
                ls = [jnp.where(masks[b], l, 0.0) for (b, h), l in zip(chains, ls)]
            cums = [_split2_dot(l, upper) for l in ls]
            sums = [jnp.sum(l, axis=1, keepdims=True) for l in ls]
            offs, tot = {}, [la, lb]
            for b in range(nb):
                for h in range(2):
                    offs[(b, h)] = tot[h]
                    tot[h] = tot[h] + sums[chains.index((b, h))]
            ws = [jnp.exp(jnp.minimum(z, 0.0) - t + c + offs[ch]) for ch, z, t, c in zip(chains, zs, ts, cums)]
            if masked:
                ws = [jnp.where(masks[b], w, 0.0) for (b, h), w in zip(chains, ws)]
            for (b, h), w in zip(chains, ws):
                acc = acc + _dot(w.astype(BF16), vss[b][h])
            return acc, tot[0], tot[1]

        def largest(la, lb):
            return jnp.max(jnp.maximum(la, lb))

        z1 = jnp.zeros((tq, 1), F32)
        acc, la, lb = trip((i + 1) * nb - 1, True, (jnp.zeros((tq, LANE), F32), z1, z1))

        def live(c):
            return (c[0] < i) & (c[4] > SB_DEAD)

        def more(c):
            j, acc, la, lb, _ = c
            acc, la, lb = trip((i - j) * nb - 1, False, (acc, la, lb))
            return j + 1, acc, la, lb, largest(la, lb)

        done, acc, la, lb, _ = lax.while_loop(live, more, (jnp.int32(0), acc, la, lb, largest(la, lb)))
        gate = g_ref[...]
        o_ref[...] = acc
        og_ref[...] = (acc * (gate * _sigmoid(gate))).astype(BF16)
        lt_ref[...] = jnp.where(_iota2((tq, 2), 1) == 0, la, lb)
        done_ref[...] = jnp.full((SUBLANE, LANE), done, F32)

    nq = T // tq
    qblk = pl.BlockSpec((tq, LANE), lambda p, i: (i, p))
    full = pl.BlockSpec((T, LANE), lambda p, i: (0, p))
    return _call(
        body, comm, name=name, grid=(SB_PAIRS, nq),
        in_specs=[qblk, full, full, pl.BlockSpec((tq, LANE), lambda p, i: (i, 3 * SB_PAIRS + p))],
        out_specs=[qblk, qblk, pl.BlockSpec((None, tq, 2), lambda p, i: (p, i, 0)),
                   pl.BlockSpec((None, None, SUBLANE, LANE), lambda p, i: (p, i, 0, 0))],
        out_shape=[jax.ShapeDtypeStruct((T, SB_W), F32), jax.ShapeDtypeStruct((T, SB_W), BF16),
                   jax.ShapeDtypeStruct((SB_PAIRS, T, 2), F32), jax.ShapeDtypeStruct((SB_PAIRS, nq, SUBLANE, LANE), F32)],
        scratch_shapes=[], semantics=("parallel", "parallel"), args=(qn, kn, vb, proj))


def _sb_attn_bwd(qn, kn, vb, dog, o, ltot, done, proj, name, comm=None):
    T = qn.shape[0]
    tq, tk = _tile(T, SB_TQ, SUBLANE), SB_TK

    def body(q_ref, k_ref, v_ref, dog_ref, o_ref, lt_ref, done_ref, g_ref, dq_ref, dk_ref, dv_ref, dgate_ref):
        i = pl.program_id(1)
        first_trip = i - jnp.max(done_ref[...]).astype(jnp.int32)

        @pl.when(i == 0)
        def _():
            dk_ref[...] = jnp.zeros_like(dk_ref)
            dv_ref[...] = jnp.zeros_like(dv_ref)

        ma = _sb_masks()
        gate, o2, dog2 = g_ref[...], o_ref[...], dog_ref[...]
        sg, dsg = _silu_and_grad(gate)
        do2 = dog2 * sg
        dgate_ref[...] = dog2 * o2 * dsg
        lt = lt_ref[...]
        first = _iota2((tq, 2), 1) == 0
        ltots = (jnp.sum(jnp.where(first, lt, 0.0), axis=1, keepdims=True),
                 jnp.sum(jnp.where(first, 0.0, lt), axis=1, keepdims=True))
        q2 = q_ref[...]
        zq = jnp.zeros_like(q2)
        qs = (jnp.where(ma, q2, zq), jnp.where(ma, zq, q2))
        dob = do2.astype(BF16)
        dos = (jnp.where(ma, dob, zq), jnp.where(ma, zq, dob))
        upto = (_iota2((tk, tk), 0) <= _iota2((tk, tk), 1)).astype(BF16)
        before = (_iota2((tk, tk), 0) < _iota2((tk, tk), 1)).astype(BF16)
        qpos = i * tq + _iota2((tq, tk), 0)
        nb = tq // tk

        def trip(kb_bot, masked, carry):
            dq, la, lb, ea, eb = carry
            chains = [(b, h) for b in range(nb) for h in range(2)]
            rows, k2s, v2s, kss, masks = [], [], [], [], []
            for b in range(nb):
                kb = kb_bot + b
                rows.append(pl.ds(pl.multiple_of(kb * tk, tk), tk))
                k2 = k_ref[rows[b], :]
                zk = jnp.zeros_like(k2)
                k2s.append(k2)
                v2s.append(v_ref[rows[b], :])
                kss.append((jnp.where(ma, k2, zk), jnp.where(ma, zk, k2)))
                masks.append((kb * tk + _iota2((tq, tk), 1)) < qpos if masked else None)

            def keep(vals):
                return [jnp.where(masks[b], x, 0.0) for (b, h), x in zip(chains, vals)] if masked else vals

            zs = [_dot(qs[h], k2s[b], NT) for b, h in chains]
            dws = [_dot(dos[h], v2s[b], NT) for b, h in chains]
            ts = [jnp.log(1.0 + jnp.exp(-jnp.abs(z))) for z in zs]
            ls = keep([-(jnp.maximum(z, 0.0) + t) for z, t in zip(zs, ts)])
            lps = [jnp.minimum(z, 0.0) - t for z, t in zip(zs, ts)]
            cums = [_split3_dot(l, upto) for l in ls]
            lsums = [jnp.sum(l, axis=1, keepdims=True) for l in ls]
            offs, tot = {}, [la, lb]
            for b in range(nb):
                for h in range(2):
                    offs[(b, h)] = tot[h]
                    tot[h] = tot[h] + lsums[chains.index((b, h))]
            ws = keep([jnp.exp(lp + (ltots[h] - (offs[(b, h)] + c))) for (b, h), lp, c in zip(chains, lps, cums)])
            es = [dw * w for dw, w in zip(dws, ws)]
            ecums = [_split2_dot(e, before) for e in es]
            esums = [jnp.sum(e, axis=1, keepdims=True) for e in es]
            eoffs, etot = {}, [ea, eb]
            for b in range(nb):
                for h in range(2):
                    eoffs[(b, h)] = etot[h]
                    etot[h] = etot[h] + esums[chains.index((b, h))]
            dzs = keep([e - jnp.exp(lp) * (e + eoffs[ch] + ec) for ch, e, lp, ec in zip(chains, es, lps, ecums)])
            dzs = [dz.astype(BF16) for dz in dzs]
            wbs = [w.astype(BF16) for w in ws]
            for (b, h), dz in zip(chains, dzs):
                dq = dq + _dot(dz, kss[b][h])
            for b in range(nb):
                ia, ib = chains.index((b, 0)), chains.index((b, 1))
                dk_ref[rows[b], :] += _dot(dzs[ia], qs[0], TN) + _dot(dzs[ib], qs[1], TN)
                dv_ref[rows[b], :] += _dot(wbs[ia], dos[0], TN) + _dot(wbs[ib], dos[1], TN)
            return dq, tot[0], tot[1], etot[0], etot[1]

        z1 = jnp.zeros((tq, 1), F32)
        carry = lax.fori_loop(first_trip, i, lambda j, c: trip(j * nb, False, c),
                              (jnp.zeros((tq, LANE), F32), z1, z1, z1, z1))
        dq = trip(i * nb, True, carry)[0]
        dq_ref[...] = dq * (SB_DH ** -0.5)

    qblk = pl.BlockSpec((tq, LANE), lambda p, i: (i, p))
    full = pl.BlockSpec((T, LANE), lambda p, i: (0, p))
    return _call(
        body, comm, name=name, grid=(SB_PAIRS, T // tq),
        in_specs=[qblk, full, full, qblk, qblk, pl.BlockSpec((None, tq, 2), lambda p, i: (p, i, 0)),
                  pl.BlockSpec((None, None, SUBLANE, LANE), lambda p, i: (p, i, 0, 0)),
                  pl.BlockSpec((tq, LANE), lambda p, i: (i, 3 * SB_PAIRS + p))],
        out_specs=[qblk, full, full, qblk],
        out_shape=[jax.ShapeDtypeStruct((T, SB_W), F32)] * 4,
        scratch_shapes=[], semantics=("parallel", "arbitrary"), args=(qn, kn, vb, dog, o, ltot, done, proj))


def _dn_conv(ext, w_ref, cw):
    return (w_ref[3:4, cw] * ext + w_ref[2:3, cw] * _down(ext, 1) + w_ref[1:2, cw] * _down(ext, 2)
            + w_ref[0:1, cw] * _down(ext, 3))


def _dn_prep_bwd(pqkv, conv_w, dact, name):
    T, W = pqkv.shape
    tt = _tile(T, CONV_ROW_TILE, SUBLANE)
    nt = T // tt
    hb = tt // HALO
    B = DN_PREP_BLK
    nq, nqk = DN_QK_W // B, 2 * DN_QK_W // B

    def body(p_ref, pp_ref, pn_ref, w_ref, d_ref, dn_ref, o_ref, dw_ref):
        i = pl.program_id(0)
        keep_p = (i > 0).astype(F32)
        keep_n = (i < nt - 1).astype(F32)
        main = slice(HALO, HALO + tt)
        parts = []
        for cb in range(W // B):
            cw = slice(cb * B, (cb + 1) * B)
            ext = jnp.concatenate([pp_ref[:, cw] * keep_p, p_ref[:, cw], pn_ref[:, cw]], axis=0)
            c = _dn_conv(ext, w_ref, cw)
            s = _sigmoid(c)
            da_dc = s * (1.0 + c * (1.0 - s))
            d_up = jnp.concatenate([jnp.zeros((HALO, B), F32), d_ref[:, cw], dn_ref[:, cw] * keep_n], axis=0)
            if cb < nqk:
                a = c * s
                scale = DN_DK ** -0.5 if cb < nq else 1.0
                normed = []
                for hh in range(B // DN_DK):
                    cols = slice(hh * DN_DK, (hh + 1) * DN_DK)
                    ah = a[:, cols]
                    r = lax.rsqrt(jnp.sum(ah * ah, axis=-1, keepdims=True) + L2_EPS)
                    y = ah * r
                    dy = d_up[:, cols] * scale
                    normed.append(r * (dy - y * jnp.sum(dy * y, axis=-1, keepdims=True)))
                d_up = jnp.concatenate(normed, axis=1)
            dc = d_up * da_dc
            dp = (w_ref[3:4, cw] * dc + w_ref[2:3, cw] * _up(dc, 1) + w_ref[1:2, cw] * _up(dc, 2)
                  + w_ref[0:1, cw] * _up(dc, 3))
            o_ref[:, cw] = dp[main].astype(BF16)
            dcm = dc[main]
            parts.append(jnp.concatenate([jnp.sum(dcm * _down(ext, 3 - k)[main], axis=0, keepdims=True)
                                          for k in range(DN_CONV)], axis=0))
        part = jnp.concatenate(parts, axis=1)

        @pl.when(i == 0)
        def _():
            dw_ref[...] = part

        @pl.when(i > 0)
        def _():
            dw_ref[...] += part

    main_spec = pl.BlockSpec((tt, W), lambda i: (i, 0))
    prev_spec = pl.BlockSpec((HALO, W), lambda i: (jnp.maximum(i * hb - 1, 0), 0))
    next_spec = pl.BlockSpec((HALO, W), lambda i: (jnp.minimum((i + 1) * hb, nt * hb - 1), 0))
    w_spec = pl.BlockSpec((DN_CONV, W), lambda i: (0, 0))
    return pl.pallas_call(
        body, name=name, grid=(nt,),
        in_specs=[main_spec, prev_spec, next_spec, w_spec, main_spec, next_spec],
        out_specs=[main_spec, w_spec],
        out_shape=[jax.ShapeDtypeStruct((T, W), BF16), jax.ShapeDtypeStruct((DN_CONV, W), F32)],
        compiler_params=_params("arbitrary"),
    )(pqkv, pqkv, pqkv, conv_w, dact, dact)


def _dn_gates(a_in, b_in, a_log, dt_bias, name):
    T, H = a_in.shape
    C = DN_CHUNK

    def body(a_ref, b_ref, al_ref, dt_ref, g_ref, beta_ref):
        beta_ref[...] = _sigmoid(b_ref[...])
        g_ref[...] = -jnp.exp(al_ref[...]) * _softplus(a_ref[...] + dt_ref[...])
        tri = (_iota2((C, C), 0) >= _iota2((C, C), 1)).astype(F32)

        def chunk(n, carry):
            rows = pl.ds(pl.multiple_of(n * C, C), C)
            g_ref[rows, :] = _hdot(tri, g_ref[rows, :])
            return carry

        lax.fori_loop(0, T // C, chunk, 0)

    return pl.pallas_call(body, name=name, out_shape=[jax.ShapeDtypeStruct((T, H), F32)] * 2)(a_in, b_in, a_log, dt_bias)


def _dn_gates_bwd(dg, dbeta, a_in, b_in, a_log, dt_bias, name):
    T, H = a_in.shape
    C = DN_CHUNK

    def body(dg_ref, db_ref, a_ref, b_ref, al_ref, dt_ref, da_ref, dbi_ref, dal_ref, ddt_ref):
        tri_t = (_iota2((C, C), 0) <= _iota2((C, C), 1)).astype(F32)

        def chunk(n, carry):
            rows = pl.ds(pl.multiple_of(n * C, C), C)
            da_ref[rows, :] = _hdot(tri_t, dg_ref[rows, :])
            return carry

        lax.fori_loop(0, T // C, chunk, 0)
        dla = da_ref[...]
        x = a_ref[...] + dt_ref[...]
        ea = jnp.exp(al_ref[...])
        da = dla * (-ea) * _sigmoid(x)
        da_ref[...] = da
        dal_ref[...] = jnp.sum(dla * (-ea * _softplus(x)), axis=0, keepdims=True)
        ddt_ref[...] = jnp.sum(da, axis=0, keepdims=True)
        beta = _sigmoid(b_ref[...])
        dbi_ref[...] = db_ref[...] * beta * (1.0 - beta)

    return pl.pallas_call(
        body, name=name,
        out_shape=[jax.ShapeDtypeStruct((T, H), F32)] * 2 + [jax.ShapeDtypeStruct((1, H), F32)] * 2,
    )(dg, dbeta, a_in, b_in, a_log, dt_bias)


def _dn_chunk_terms(q, k, gc, bc):
    C = DN_CHUNK
    r, c = _iota2((C, C), 0), _iota2((C, C), 1)
    lower, strict, eye = r >= c, r > c, r == c
    grow = jnp.sum(jnp.where(eye, gc, 0.0), axis=0, keepdims=True)
    decay = jnp.where(lower, jnp.exp(jnp.where(lower, gc - grow, 0.0)), 0.0)
    last = _iota2((C, 1), 0) == C - 1
    gl = jnp.sum(jnp.where(last, gc, 0.0), axis=0, keepdims=True)
    eg = jnp.exp(gc)
    egl = jnp.exp(gl - gc)
    kb = k * bc
    lmat = jnp.where(strict, _bdot(kb, k, NT) * decay, 0.0)
    aqk = jnp.where(lower, _bdot(q, k, NT) * decay, 0.0)
    return dict(lower=lower, strict=strict, eye=eye, last=last, decay=decay, gl=gl, eg=eg, egl=egl, kb=kb,
                lmat=lmat, aqk=aqk, qd=q * eg, kd=k * egl)


def _split(x):
    hi = x.astype(BF16)
    return hi, (x - hi.astype(F32)).astype(BF16)


def _x3dot(a, b, dims=NN):
    ah, al = a if isinstance(a, tuple) else _split(a)
    bh, bl = b if isinstance(b, tuple) else _split(b)
    return _dot(ah, bh, dims) + (_dot(ah, bl, dims) + _dot(al, bh, dims))


def _interleave(gens):
    for _ in itertools.zip_longest(*gens):
        pass


def _unit_lower_inverse_steps(lmat, eye, out):
    ident = jnp.where(eye, 1.0, 0.0).astype(F32)
    m = -lmat
    inv = ident + m
    for _ in range(int(math.log2(DN_CHUNK)) - 1):
        ms = _split(m)
        m = _x3dot(ms, ms)
        yield
        inv = inv + _x3dot(inv, m)
        yield
    out["tm"] = inv


def _dn_chunk_fwd(pqkv, conv_w, g, beta, pgate, gn, name, comm=None):
    T = pqkv.shape[0]
    C, H = DN_CHUNK, DN_HEADS
    N = T // C
    B = DN_PREP_BLK
    nq, nqk = DN_QK_W // B, 2 * DN_QK_W // B

    def step(p_ref, cw_ref, g_ref, b_ref, pg_ref, gn_ref, act_out, o_ref, og_ref, s_out, t_out, vn_out, u_out, w_out,
             s_scr, tail_scr, a_ref, a_next):
        head_lane = _iota2((C, H), 1)

        def prepare(cb):
            cw = slice(cb * B, (cb + 1) * B)
            ext = jnp.concatenate([tail_scr[:, cw], p_ref[:, cw]], axis=0)
            c = _dn_conv(ext, cw_ref, cw)[HALO:]
            yield
            a = c * _sigmoid(c)
            if cb >= nqk:
                a_next[:, cw] = a
                act_out[:, cw] = a
                return
            scale = DN_DK ** -0.5 if cb < nq else 1.0
            for hh in range(B // DN_DK):
                yield
                ah = a[:, hh * DN_DK:(hh + 1) * DN_DK]
                val = ah * (lax.rsqrt(jnp.sum(ah * ah, axis=-1, keepdims=True) + L2_EPS) * scale)
                cols = slice(cb * B + hh * DN_DK, cb * B + (hh + 1) * DN_DK)
                a_next[:, cols] = val
                act_out[:, cols] = val

        def head(hh):
            qs, vs = slice(hh * DN_DK, (hh + 1) * DN_DK), slice(hh * DN_DV, (hh + 1) * DN_DV)
            q, k, v = a_ref[:, qs], a_ref[:, DN_QK_W + hh * DN_DK:DN_QK_W + (hh + 1) * DN_DK], \
                a_ref[:, 2 * DN_QK_W + hh * DN_DV:2 * DN_QK_W + (hh + 1) * DN_DV]
            gc = jnp.sum(jnp.where(head_lane == hh, g_ref[...], 0.0), axis=1, keepdims=True)
            bc = jnp.sum(jnp.where(head_lane == hh, b_ref[...], 0.0), axis=1, keepdims=True)
            t = _dn_chunk_terms(q, k, gc, bc)
            yield
            res = {}
            yield from _unit_lower_inverse_steps(t["lmat"], t["eye"], res)
            tms = _split(res["tm"])
            u = _x3dot(tms, v * bc)
            yield
            w = _x3dot(tms, t["kb"] * t["eg"])
            yield
            s = s_scr[hh]
            s_out[hh] = s
            t_out[hh] = res["tm"]
            sb = s.astype(BF16)
            vn = u - _dot(w.astype(BF16), sb)
            yield
            o = _dot(t["qd"].astype(BF16), sb) + _bdot(t["aqk"], vn)
            yield
            s_scr[hh] = s * jnp.exp(t["gl"]) + _bdot(t["kd"], vn, TN)
            vn_out[:, vs] = vn
            u_out[:, vs] = u
            w_out[:, qs] = w
            o_ref[:, vs] = o
            gate = pg_ref[:, vs]
            r = lax.rsqrt(jnp.mean(o * o, axis=-1, keepdims=True) + RMS_EPS)
            og_ref[:, vs] = (o * r * gn_ref[...] * (gate * _sigmoid(gate))).astype(BF16)

        _interleave([head(hh) for hh in range(H)] + [prepare(cb) for cb in range(DN_CONV_W // B)])

        @pl.when(pl.program_id(0) < N - 1)
        def _():
            tail_scr[...] = p_ref[C - HALO:C, :]

    def body(*refs):
        s = pl.program_id(0)
        io, (s_scr, tail_scr, buf_a, buf_b) = refs[:-4], refs[-4:]

        @pl.when(s == 0)
        def _():
            tail_scr[...] = jnp.zeros_like(tail_scr)
            buf_b[...] = jnp.zeros_like(buf_b)

        @pl.when(s <= 1)
        def _():
            s_scr[...] = jnp.zeros_like(s_scr)

        @pl.when(s % 2 == 0)
        def _():
            step(*io, s_scr, tail_scr, buf_b, buf_a)

        @pl.when(s % 2 == 1)
        def _():
            step(*io, s_scr, tail_scr, buf_a, buf_b)

    nxt = lambda w: pl.BlockSpec((C, w), lambda s: (jnp.minimum(s, N - 1), 0))
    cur = lambda w: pl.BlockSpec((C, w), lambda s: (jnp.maximum(s - 1, 0), 0))
    per_chunk = lambda a, b: pl.BlockSpec((H, None, a, b), lambda s: (0, jnp.maximum(s - 1, 0), 0, 0))
    return _call(
        body, comm, name=name, grid=(N + 1,),
        in_specs=[nxt(DN_CONV_W), pl.BlockSpec((DN_CONV, DN_CONV_W), lambda s: (0, 0)), cur(H), cur(H), cur(DN_V_W),
                  pl.BlockSpec((1, DN_DV), lambda s: (0, 0))],
        out_specs=[nxt(DN_CONV_W), cur(DN_V_W), cur(DN_V_W), per_chunk(DN_DK, DN_DV), per_chunk(C, C),
                   cur(DN_V_W), cur(DN_V_W), cur(DN_QK_W)],
        out_shape=[jax.ShapeDtypeStruct((T, DN_CONV_W), F32),
                   jax.ShapeDtypeStruct((T, DN_V_W), F32), jax.ShapeDtypeStruct((T, DN_V_W), BF16),
                   jax.ShapeDtypeStruct((H, N, DN_DK, DN_DV), F32),
                   jax.ShapeDtypeStruct((H, N, C, C), F32),
                   jax.ShapeDtypeStruct((T, DN_V_W), F32),
                   jax.ShapeDtypeStruct((T, DN_V_W), F32),
                   jax.ShapeDtypeStruct((T, DN_QK_W), F32)],
        scratch_shapes=[pltpu.VMEM((H, DN_DK, DN_DV), F32), pltpu.VMEM((HALO, DN_CONV_W), F32),
                        pltpu.VMEM((C, DN_CONV_W), F32), pltpu.VMEM((C, DN_CONV_W), F32)],
        semantics=("arbitrary",), args=(pqkv, conv_w, g, beta, pgate, gn))


def _dn_chunk_bwd(act, g, beta, s_saved, tm_saved, vn_saved, u_saved, w_saved, dog, o_raw, pgate, gn, name, comm=None):
    T = act.shape[0]
    C, H = DN_CHUNK, DN_HEADS
    N = T // C

    def body(a_ref, g_ref, b_ref, s_ref, t_ref, vn_ref, u_ref, w_ref, dog_ref, o_ref, pg_ref, gn_ref,
             da_ref, dg_ref, db_ref, dgate_ref, dgn_ref, ds_scr):
        @pl.when(pl.program_id(0) == 0)
        def _():
            ds_scr[...] = jnp.zeros_like(ds_scr)

        head_lane = _iota2((C, H), 1)
        dg_cols, db_cols, dgn_parts = {}, {}, {}

        def output_gate_bwd(hh, vs):
            d, o, gate, gn_v = dog_ref[:, vs], o_ref[:, vs], pg_ref[:, vs], gn_ref[...]
            sg, dsg = _silu_and_grad(gate)
            r = lax.rsqrt(jnp.mean(o * o, axis=-1, keepdims=True) + RMS_EPS)
            n = o * r
            dy = d * sg
            dgate_ref[:, vs] = (d * (n * gn_v) * dsg).astype(BF16)
            dn = dy * gn_v
            dgn_parts[hh] = jnp.sum(dy * n, axis=0, keepdims=True)
            return r * (dn - n * jnp.mean(dn * n, axis=-1, keepdims=True))

        def head(hh):
            qs, vs = slice(hh * DN_DK, (hh + 1) * DN_DK), slice(hh * DN_DV, (hh + 1) * DN_DV)
            ks = slice(DN_QK_W + hh * DN_DK, DN_QK_W + (hh + 1) * DN_DK)
            vas = slice(2 * DN_QK_W + hh * DN_DV, 2 * DN_QK_W + (hh + 1) * DN_DV)
            q, k, v = a_ref[:, qs], a_ref[:, ks], a_ref[:, vas]
            gc = jnp.sum(jnp.where(head_lane == hh, g_ref[...], 0.0), axis=1, keepdims=True)
            bc = jnp.sum(jnp.where(head_lane == hh, b_ref[...], 0.0), axis=1, keepdims=True)
            t = _dn_chunk_terms(q, k, gc, bc)
            yield
            lower, strict, eye = t["lower"], t["strict"], t["eye"]
            decay, eg, egl, kb, qd, kd = t["decay"], t["eg"], t["egl"], t["kb"], t["qd"], t["kd"]
            s, tm, vn, u, w = s_ref[hh], t_ref[hh], vn_ref[:, vs], u_ref[:, vs], w_ref[:, qs]
            d_o = output_gate_bwd(hh, vs)
            ds_next = ds_scr[hh]
            egl_tot = jnp.exp(t["gl"])
            dob, sb, dsb, vnb = d_o.astype(BF16), s.astype(BF16), ds_next.astype(BF16), vn.astype(BF16)

            dvn = _bdot(t["aqk"], dob, TN) + _bdot(kd, dsb)
            yield
            daqk = jnp.where(lower, _dot(dob, vnb, NT), 0.0)
            dqd = _dot(dob, sb, NT)
            dkd = _dot(vnb, dsb, NT)
            yield
            dvnb = dvn.astype(BF16)
            ds_scr[hh] = _bdot(qd, dob, TN) + egl_tot * ds_next - _bdot(w, dvnb, TN)
            dgl = egl_tot * jnp.sum(jnp.sum(s * ds_next, axis=1, keepdims=True), axis=0, keepdims=True)
            dw = -_dot(dvnb, sb, NT)
            yield
            tms = _split(tm)
            dru = _x3dot(tms, dvn, TN)
            drw = _x3dot(tms, dw, TN)
            yield
            dl = -jnp.where(strict, _x3dot(dru, u, NT) + _x3dot(drw, w, NT), 0.0)
            yield
            dkk = (dl * decay).astype(BF16)
            dqk = (daqk * decay).astype(BF16)
            dkb = _bdot(dkk, k) + drw * eg
            yield
            da_ref[:, ks] = _bdot(dkk, kb, TN) + _bdot(dqk, q, TN) + dkd * egl + dkb * bc
            da_ref[:, qs] = _bdot(dqk, k) + dqd * eg
            da_ref[:, vas] = dru * bc
            yield
            db_cols[hh] = jnp.sum(dru * v, axis=1, keepdims=True) + jnp.sum(dkb * k, axis=1, keepdims=True)
            pm = dl * t["lmat"] + daqk * t["aqk"]
            col_as_col = jnp.sum(jnp.where(eye, jnp.sum(pm, axis=0, keepdims=True), 0.0), axis=1, keepdims=True)
            kdsum = jnp.sum(dkd * kd, axis=1, keepdims=True)
            dgc = (jnp.sum(pm, axis=1, keepdims=True) - col_as_col + jnp.sum(dqd * qd, axis=1, keepdims=True)
                   - kdsum + jnp.sum(drw * (kb * eg), axis=1, keepdims=True))
            dgl = dgl + jnp.sum(kdsum, axis=0, keepdims=True)
            dg_cols[hh] = dgc + jnp.where(t["last"], dgl, 0.0)

        _interleave([head(hh) for hh in range(H)])
        dg_ref[...] = sum(jnp.where(head_lane == hh, dg_cols[hh], 0.0) for hh in range(H))
        db_ref[...] = sum(jnp.where(head_lane == hh, db_cols[hh], 0.0) for hh in range(H))
        dgn_part = sum(dgn_parts[hh] for hh in range(H))

        @pl.when(pl.program_id(0) == 0)
        def _():
            dgn_ref[...] = dgn_part

        @pl.when(pl.program_id(0) > 0)
        def _():
            dgn_ref[...] += dgn_part

    row = lambda w: pl.BlockSpec((C, w), lambda n: (N - 1 - n, 0))
    vec = pl.BlockSpec((1, DN_DV), lambda n: (0, 0))
    return _call(
        body, comm, name=name, grid=(N,),
        in_specs=[row(DN_CONV_W), row(H), row(H),
                  pl.BlockSpec((H, None, DN_DK, DN_DV), lambda n: (0, N - 1 - n, 0, 0)),
                  pl.BlockSpec((H, None, C, C), lambda n: (0, N - 1 - n, 0, 0)),
                  row(DN_V_W), row(DN_V_W), row(DN_QK_W), row(DN_V_W), row(DN_V_W), row(DN_V_W), vec],
        out_specs=[row(DN_CONV_W), row(H), row(H), row(DN_V_W), vec],
        out_shape=[jax.ShapeDtypeStruct((T, DN_CONV_W), F32),
                   jax.ShapeDtypeStruct((T, H), F32), jax.ShapeDtypeStruct((T, H), F32),
                   jax.ShapeDtypeStruct((T, DN_V_W), BF16), jax.ShapeDtypeStruct((1, DN_DV), F32)],
        scratch_shapes=[pltpu.VMEM((H, DN_DK, DN_DV), F32)], semantics=("arbitrary",),
        args=(act, g, beta, s_saved, tm_saved, vn_saved, u_saved, w_saved, dog, o_raw, pgate, gn))


def _dn_split_w_in(w):
    return w, jnp.pad(w[:, DN_CONV_W + DN_V_W:], ((0, 0), (0, DN_AB_PAD - 2 * DN_HEADS)))


def _out_proj(og, w_out, x_res, next_g, name):
    if next_g is None:
        return _matmul(og, w_out, "nn", name, add=x_res), None
    return tuple(_matmul(og, w_out, "nn", name, add=x_res, norm_fwd=next_g, tm=NORM_FUSED_TM))


def _dn_layer_fwd(h, wts, conv_w, a_log, dt_bias, gn, w_out, x_res, tag, comm=None, next_g=None):
    w_in, wab = wts
    H = DN_HEADS
    pqkv = _matmul(h, w_in, "nn", tag + "_pqkv", b_cols=(0, DN_CONV_W))
    pgate = _matmul(h, w_in, "nn", tag + "_pgate", b_cols=(DN_CONV_W, DN_V_W))
    pab = _matmul(h, wab, "nn", tag + "_pab")
    a_in, b_in = pab[:, :H], pab[:, H:2 * H]
    g, beta = _dn_gates(a_in, b_in, a_log, dt_bias, tag + "_gates")
    (act, o_raw, og, s_sv, tm_sv, vn_sv, u_sv, w_sv), landed = _dn_chunk_fwd(pqkv, conv_w, g, beta, pgate, gn,
                                                                             tag + "_chunk_fwd", comm)
    if callable(w_out):
        w_out = w_out(landed)
    y = _out_proj(og, w_out, x_res, next_g, tag + "_out")
    saved = dict(h=h, wts=wts, conv_w=conv_w, a_log=a_log, dt_bias=dt_bias, gn=gn, w_out=w_out, pqkv=pqkv, pgate=pgate,
                 a_in=a_in, b_in=b_in, g=g, beta=beta, act=act, o_raw=o_raw, chunk=(s_sv, tm_sv, vn_sv, u_sv, w_sv), og=og)
    return y, saved, landed


def _dn_layer_bwd(dout, sv, tag, norm, comm_of=None, late_comm_of=None):
    w_in, wab = sv["wts"]
    h = sv["h"]
    dog = _matmul(dout, sv["w_out"], "nt", tag + "_dog")
    dw_out = _matmul(sv["og"], dout, "tn", tag + "_dwout", out_dtype=BF16)
    comm = comm_of(dw_out) if comm_of is not None else None
    (dact, dg, dbeta, dgate, dgn), landed = _dn_chunk_bwd(sv["act"], sv["g"], sv["beta"], *sv["chunk"], dog, sv["o_raw"],
                                                          sv["pgate"], sv["gn"], tag + "_chunk_bwd", comm)
    da_in, db_in, da_log, ddt = _dn_gates_bwd(dg, dbeta, sv["a_in"], sv["b_in"], sv["a_log"], sv["dt_bias"],
                                              tag + "_gates_bwd")
    dpqkv, dconv = _dn_prep_bwd(sv["pqkv"], sv["conv_w"], dact, tag + "_prep_bwd")
    dpab = jnp.pad(jnp.concatenate([da_in, db_in], axis=1), ((0, 0), (0, DN_AB_PAD - 2 * DN_HEADS)))
    dwqkv = _matmul(h, dpqkv, "tn", tag + "_dwqkv", out_dtype=BF16)
    dwgate = _matmul(h, dgate, "tn", tag + "_dwgate", out_dtype=BF16)
    dwab = _matmul(h, dpab, "tn", tag + "_dwab", out_dtype=BF16)
    dw_in = jnp.concatenate([dwqkv, dwgate, dwab[:, :2 * DN_HEADS]], axis=1)
    grads = dict(dn_w_in=dw_in, dn_conv_w=dconv, dn_a_log=da_log, dn_dt_bias=ddt, dn_o_norm_g=dgn, dn_w_out=dw_out)
    dx, landed_late = _matmul_nt_sum([(dpqkv, w_in, 0), (dgate, w_in, DN_CONV_W), (dpab, wab, 0)], tag + "_dh",
                                     late_comm_of(grads) if late_comm_of is not None else None, norm_bwd=norm)
    return dx, grads, landed, landed_late


def _sb_layer_fwd(h, w_in, qg, kg, w_out, x_res, tag, comm=None, next_g=None):
    qg2, kg2 = jnp.tile(qg, (1, 2)), jnp.tile(kg, (1, 2))
    proj = _matmul(h, w_in, "nn", tag + "_proj", blocked_b=True)
    qn, kn, vb = _sb_prep(proj, qg2, kg2, tag + "_prep")
    (o, og, ltot, done), landed = _sb_attn_fwd(qn, kn, vb, proj, tag + "_attn_fwd", comm)
    y = _out_proj(og, w_out, x_res, next_g, tag + "_out")
    saved = dict(h=h, w_in=w_in, qg2=qg2, kg2=kg2, w_out=w_out, proj=proj, qn=qn, kn=kn, vb=vb, o=o, og=og, ltot=ltot,
                 done=done)
    return y, saved, landed


def _sb_layer_bwd(dout, sv, tag, norm, comm=None):
    dog = _matmul(dout, sv["w_out"], "nt", tag + "_dog")
    dw_out = _matmul(sv["og"], dout, "tn", tag + "_dwout", out_dtype=BF16)
    (dqn, dkn, dv, dgate), landed = _sb_attn_bwd(sv["qn"], sv["kn"], sv["vb"], dog, sv["o"], sv["ltot"], sv["done"],
                                                 sv["proj"], tag + "_attn_bwd", comm)
    dproj, dqgp, dkgp = _sb_prep_bwd(sv["proj"], dqn, dkn, dv, dgate, sv["qg2"], sv["kg2"], tag + "_prep_bwd")
    dw_in = _matmul(sv["h"], dproj, "tn", tag + "_dwin", out_dtype=BF16, blocked_out=N_DEV)
    dx = _matmul(dproj, sv["w_in"], "nt", tag + "_dh", blocked_b=True, norm_bwd=norm, tm=NORM_FUSED_TM)
    dqg = _fold_heads(dqgp, tag + "_dqg")
    dkg = _fold_heads(dkgp, tag + "_dkg")
    return dx, dict(sb_w_in=dw_in, sb_q_norm_g=dqg, sb_k_norm_g=dkg, sb_w_out=dw_out), landed


def _sc_layer_fwd(h, w_in, conv_w, w_out, x_res, tag, next_g=None):
    proj = _matmul(h, w_in, "nn", tag + "_proj", blocked_b=True)
    yg = _sc_fwd(proj, conv_w, tag + "_fwd")
    y = _out_proj(yg, w_out, x_res, next_g, tag + "_out")
    return y, dict(h=h, w_in=w_in, conv_w=conv_w, w_out=w_out, proj=proj, yg=yg)


def _sc_layer_bwd(dout, sv, tag, norm):
    dyg = _matmul(dout, sv["w_out"], "nt", tag + "_dyg")
    dw_out = _matmul(sv["yg"], dout, "tn", tag + "_dwout", out_dtype=BF16)
    dproj, dconv = _sc_bwd(dyg, sv["proj"], sv["conv_w"], tag + "_bwd")
    dw_in = _matmul(sv["h"], dproj, "tn", tag + "_dwin", out_dtype=BF16, blocked_out=N_DEV)
    dx = _matmul(dproj, sv["w_in"], "nt", tag + "_dh", blocked_b=True, norm_bwd=norm, tm=NORM_FUSED_TM)
    return dx, dict(sc_w_in=dw_in, sc_conv_w=dconv, sc_w_out=dw_out)


def _adamw(w, m, v, parts, name):
    R, C = w.shape
    tr = _tile(R, 128, SUBLANE)

    def body(w_ref, m_ref, v_ref, p_ref, g_ref, d_ref, nm_ref, nv_ref):
        g = p_ref[0].astype(F32)
        for s in range(1, N_DEV):
            g = g + p_ref[s].astype(F32)
        m2 = ADAM_B1 * m_ref[...] + (1.0 - ADAM_B1) * g
        v2 = ADAM_B2 * v_ref[...] + (1.0 - ADAM_B2) * (g * g)
        m_hat = m2 / (1.0 - ADAM_B1 ** ADAM_STEP)
        v_hat = v2 / (1.0 - ADAM_B2 ** ADAM_STEP)
        g_ref[...] = g
        d_ref[...] = -ADAM_LR * (m_hat / (jnp.sqrt(v_hat) + ADAM_EPS) + ADAM_WD * w_ref[...])
        nm_ref[...] = m2
        nv_ref[...] = v2

    blk = pl.BlockSpec((tr, C), lambda i: (i, 0))
    return pl.pallas_call(
        body, name=name, grid=(R // tr,),
        in_specs=[blk, blk, blk, pl.BlockSpec((N_DEV, tr, C), lambda i: (0, i, 0))],
        out_specs=[blk] * 4, out_shape=[jax.ShapeDtypeStruct((R, C), F32)] * 4,
        compiler_params=_params("parallel"),
    )(w, m, v, parts)


_HBM = pl.BlockSpec(memory_space=pltpu.HBM)
_MESH = pl.DeviceIdType.MESH


def _slot(x, y, c):
    return 4 * x + 2 * y + c


class _Gather:
    def __init__(self, shards):
        self.arrays = list(shards)
        n = len(self.arrays)
        self.out_shapes = [jax.ShapeDtypeStruct((N_DEV,) + s.shape, s.dtype) for s in self.arrays]
        self.scratch = [pltpu.SemaphoreType.DMA((n, N_DEV - 1)), pltpu.SemaphoreType.DMA((n, N_DEV - 1)),
                        pltpu.SemaphoreType.DMA((n,))]

    def _parts(self, ins, outs, sems):
        send_sems, recv_sems, local_sems = sems
        n = len(self.arrays)
        x, y, c = lax.axis_index("x"), lax.axis_index("y"), lax.axis_index("c")
        me, sibling = (x, y, c), (x, y, 1 - c)
        chips = [(1 - x, y), (x, 1 - y), (1 - x, 1 - y)]

        def copy(a, k, block, to, src=None):
            dst = outs[a].at[_slot(*block)]
            return pltpu.make_async_remote_copy(src_ref=dst if src is None else src, dst_ref=dst,
                                                send_sem=send_sems.at[a, k], recv_sem=recv_sems.at[a, k],
                                                device_id=to, device_id_type=_MESH)

        mine = [pltpu.make_async_copy(ins[a], outs[a].at[_slot(*me)], local_sems.at[a]) for a in range(n)]
        first = []
        for a in range(n):
            first.append(copy(a, 0, me, sibling, src=ins[a]))
            first += [copy(a, 1 + j, me, (*chip, c), src=ins[a]) for j, chip in enumerate(chips)]
        return n, c, me, sibling, chips, copy, mine, first

    def start(self, ins, outs, sems):
        _, _, _, _, _, _, mine, first = self._parts(ins, outs, sems)
        for cp in mine + first:
            cp.start()

    def finish(self, ins, outs, sems):
        n, c, me, sibling, chips, copy, mine, first = self._parts(ins, outs, sems)
        passed = []
        for j, chip in enumerate(chips):
            for a in range(n):
                copy(a, 1 + j, (*chip, c), me).wait_recv()
                fwd = copy(a, 4 + j, (*chip, c), sibling)
                fwd.start()
                passed.append(fwd)
        for a in range(n):
            copy(a, 0, sibling, me).wait_recv()
            for j, chip in enumerate(chips):
                copy(a, 4 + j, (*chip, 1 - c), me).wait_recv()
        for cp in first + passed:
            cp.wait_send()
        for cp in mine:
            cp.wait()


class _Exchange:
    def __init__(self, arrays, scatter):
        self.arrays, self.scatter = list(arrays), list(scatter)
        n = len(self.arrays)
        shapes = [a.shape[1:] if s else a.shape for a, s in zip(self.arrays, self.scatter)]
        self.out_shapes = [jax.ShapeDtypeStruct((N_DEV,) + tuple(s), a.dtype) for s, a in zip(shapes, self.arrays)]
        self.scratch = [pltpu.SemaphoreType.DMA((n, N_DEV - 1)), pltpu.SemaphoreType.DMA((n, N_DEV - 1)),
                        pltpu.SemaphoreType.DMA((n,))]

    def _copies(self, ins, outs, sems):
        send_sems, recv_sems, local_sems = sems
        n, scatter = len(self.arrays), self.scatter
        x, y, c = lax.axis_index("x"), lax.axis_index("y"), lax.axis_index("c")
        me = _slot(x, y, c)
        copies = [pltpu.make_async_copy(ins[a].at[me] if scatter[a] else ins[a], outs[a].at[me], local_sems.at[a])
                  for a in range(n)]
        for r in range(1, N_DEV):
            px = 1 - x if r & 4 else x
            py = 1 - y if r & 2 else y
            pc = 1 - c if r & 1 else c
            for a in range(n):
                copies.append(pltpu.make_async_remote_copy(
                    src_ref=ins[a].at[_slot(px, py, pc)] if scatter[a] else ins[a], dst_ref=outs[a].at[me],
                    send_sem=send_sems.at[a, r - 1], recv_sem=recv_sems.at[a, r - 1],
                    device_id=(px, py, pc), device_id_type=_MESH))
        return copies

    def start(self, ins, outs, sems):
        for cp in self._copies(ins, outs, sems):
            cp.start()

    def finish(self, ins, outs, sems):
        for cp in self._copies(ins, outs, sems):
            cp.wait()


def _comm_call(comm, name):
    n = len(comm.arrays)

    def body(*refs):
        ins, outs, sems = refs[:n], refs[n:2 * n], refs[2 * n:]
        comm.start(ins, outs, sems)
        comm.finish(ins, outs, sems)

    return pl.pallas_call(body, name=name, in_specs=[_HBM] * n, out_specs=[_HBM] * n, out_shape=comm.out_shapes,
                          scratch_shapes=comm.scratch)(*comm.arrays)


def _call(body, comm, *, name, grid, in_specs, out_specs, out_shape, scratch_shapes, semantics, args):
    if comm is None:
        outs = pl.pallas_call(body, name=name, grid=grid, in_specs=in_specs, out_specs=out_specs, out_shape=out_shape,
                              scratch_shapes=scratch_shapes, compiler_params=_params(*semantics))(*args)
        return outs, []
    n_in, n_out, n_scr, n_c = len(in_specs), len(out_specs), len(scratch_shapes), len(comm.arrays)

    def fused(*refs):
        ins, refs = refs[:n_in], refs[n_in:]
        c_ins, refs = refs[:n_c], refs[n_c:]
        outs, refs = refs[:n_out], refs[n_out:]
        c_outs, refs = refs[:n_c], refs[n_c:]
        scr, sems = refs[:n_scr], refs[n_scr:]
        ids = [pl.program_id(d) for d in range(len(grid))]
        first = functools.reduce(jnp.logical_and, [i == 0 for i in ids])
        last = functools.reduce(jnp.logical_and, [i == g - 1 for i, g in zip(ids, grid)])

        @pl.when(first)
        def _():
            comm.start(c_ins, c_outs, sems)

        body(*ins, *outs, *scr)

        @pl.when(last)
        def _():
            comm.finish(c_ins, c_outs, sems)

    outs = pl.pallas_call(
        fused, name=name, grid=grid, in_specs=list(in_specs) + [_HBM] * n_c, out_specs=list(out_specs) + [_HBM] * n_c,
        out_shape=list(out_shape) + comm.out_shapes, scratch_shapes=list(scratch_shapes) + comm.scratch,
        compiler_params=_params(*["arbitrary"] * len(grid)))(*args, *comm.arrays)
    return outs[:n_out], outs[n_out:]


_GATHER_0 = (("dn_w_in", 0), ("dn_conv_w", 0), ("dn_o_norm_g", 0))
_GATHER_1 = (("dn_w_out", 0), ("sb_w_in", 0), ("sb_w_out", 0))
_GATHER_2 = (("sc_w_in", 0), ("sc_conv_w", 0), ("sc_w_out", 0), ("dn_w_in", 1), ("dn_conv_w", 1), ("dn_o_norm_g", 1),
             ("dn_w_out", 1))
_EXCHANGE_A = _GATHER_2
_EXCHANGE_B = (("sb_w_in", 0), ("sb_w_out", 0), ("dn_w_out", 0))
_EXCHANGE_C = _GATHER_0
_MATMUL_WEIGHTS = ("dn_w_in", "dn_w_out", "sb_w_in", "sb_w_out", "sc_w_in", "sc_w_out")
_COLUMN_SHARDED = ("dn_w_in", "dn_conv_w", "dn_o_norm_g", "sb_w_in", "sc_w_in", "sc_conv_w")
_BLOCKED = ("sb_w_in", "sc_w_in")
_REPLICATED = ("norm_g", "dn_a_log", "dn_dt_bias", "sb_q_norm_g", "sb_k_norm_g")
_ORDER = ("norm_g", "dn_w_in", "dn_conv_w", "dn_a_log", "dn_dt_bias", "dn_o_norm_g", "dn_w_out", "sb_w_in", "sb_q_norm_g",
          "sb_k_norm_g", "sb_w_out", "sc_w_in", "sc_conv_w", "sc_w_out")
_PACK_COLS = D_MODEL


def _as_2d(a):
    return a.reshape(1, -1) if a.ndim == 1 else a


def _assemble(name, gathered):
    n, r, c = gathered.shape
    if name in _COLUMN_SHARDED:
        return jnp.moveaxis(gathered, 0, 1).reshape(r, n * c)
    return gathered.reshape(n * r, c)


def _disassemble(name, full):
    r, c = full.shape
    if name in _COLUMN_SHARDED:
        return jnp.moveaxis(full.reshape(r, N_DEV, c // N_DEV), 1, 0)
    return full.reshape(N_DEV, r // N_DEV, c)


def _pack_replicated(d):
    rows = [d["norm_g"]]
    for name in _REPLICATED[1:]:
        flat = d[name].reshape(1, -1)
        rows.append(jnp.pad(flat, ((0, 0), (0, _PACK_COLS - flat.shape[1]))))
    return jnp.concatenate(rows, axis=0)


def _unpack_replicated(p, like):
    out = {"norm_g": p[:4]}
    for r, name in enumerate(_REPLICATED[1:]):
        shape = like[name].shape
        out[name] = p[4 + r, :math.prod(shape)].reshape(shape)
    return out


def kernel(x, norm_g, dn_w_in, dn_conv_w, dn_a_log, dn_dt_bias, dn_o_norm_g, dn_w_out, sb_w_in, sb_q_norm_g, sb_k_norm_g, sb_w_out, sc_w_in, sc_conv_w, sc_w_out, loss_target, m_norm_g, m_dn_w_in, m_dn_conv_w, m_dn_a_log, m_dn_dt_bias, m_dn_o_norm_g, m_dn_w_out, m_sb_w_in, m_sb_q_norm_g, m_sb_k_norm_g, m_sb_w_out, m_sc_w_in, m_sc_conv_w, m_sc_w_out, v_norm_g, v_dn_w_in, v_dn_conv_w, v_dn_a_log, v_dn_dt_bias, v_dn_o_norm_g, v_dn_w_out, v_sb_w_in, v_sb_q_norm_g, v_sb_k_norm_g, v_sb_w_out, v_sc_w_in, v_sc_conv_w, v_sc_w_out):
    w = dict(norm_g=norm_g, dn_w_in=dn_w_in, dn_conv_w=dn_conv_w, dn_a_log=dn_a_log, dn_dt_bias=dn_dt_bias,
             dn_o_norm_g=dn_o_norm_g, dn_w_out=dn_w_out, sb_w_in=sb_w_in, sb_q_norm_g=sb_q_norm_g, sb_k_norm_g=sb_k_norm_g,
             sb_w_out=sb_w_out, sc_w_in=sc_w_in, sc_conv_w=sc_conv_w, sc_w_out=sc_w_out)
    m = dict(norm_g=m_norm_g, dn_w_in=m_dn_w_in, dn_conv_w=m_dn_conv_w, dn_a_log=m_dn_a_log, dn_dt_bias=m_dn_dt_bias,
             dn_o_norm_g=m_dn_o_norm_g, dn_w_out=m_dn_w_out, sb_w_in=m_sb_w_in, sb_q_norm_g=m_sb_q_norm_g,
             sb_k_norm_g=m_sb_k_norm_g, sb_w_out=m_sb_w_out, sc_w_in=m_sc_w_in, sc_conv_w=m_sc_conv_w, sc_w_out=m_sc_w_out)
    v = dict(norm_g=v_norm_g, dn_w_in=v_dn_w_in, dn_conv_w=v_dn_conv_w, dn_a_log=v_dn_a_log, dn_dt_bias=v_dn_dt_bias,
             dn_o_norm_g=v_dn_o_norm_g, dn_w_out=v_dn_w_out, sb_w_in=v_sb_w_in, sb_q_norm_g=v_sb_q_norm_g,
             sb_k_norm_g=v_sb_k_norm_g, sb_w_out=v_sb_w_out, sc_w_in=v_sc_w_in, sc_conv_w=v_sc_conv_w, sc_w_out=v_sc_w_out)

    def gather_of(keys):
        return _Gather([_as_2d(w[k][j]).astype(BF16) if k in _MATMUL_WEIGHTS else _as_2d(w[k][j]) for k, j in keys])

    def full_weights(keys, gathered):
        return {key: g if key[0] in _BLOCKED else _assemble(key[0], g) for key, g in zip(keys, gathered)}

    def exchange_of(keys, grads, extra=()):
        out = [grads[k, j] if k in _BLOCKED else
               _disassemble(k, grads[k, j].astype(BF16) if k in _MATMUL_WEIGHTS else grads[k, j]) for k, j in keys]
        return _Exchange(out + list(extra), [True] * len(out) + [False] * len(extra))

    F = full_weights(_GATHER_0, _comm_call(gather_of(_GATHER_0), "gather_first"))
    xs, saves = [x[0]], []
    h = _rmsnorm_fwd(xs[0], norm_g[0:1], "norm0")

    def w_out_0(got):
        F.update(full_weights(_GATHER_1, got))
        return F["dn_w_out", 0]

    (y, h), sv, _ = _dn_layer_fwd(h, _dn_split_w_in(F["dn_w_in", 0]), F["dn_conv_w", 0], dn_a_log[0:1], dn_dt_bias[0:1],
                                  F["dn_o_norm_g", 0], w_out_0, xs[0], "dn0", gather_of(_GATHER_1), norm_g[1:2])
    xs.append(y)
    saves.append(sv)
    (y, h), sv, got = _sb_layer_fwd(h, F["sb_w_in", 0], sb_q_norm_g, sb_k_norm_g, F["sb_w_out", 0], xs[1], "sb",
                                    gather_of(_GATHER_2), norm_g[2:3])
    F.update(full_weights(_GATHER_2, got))
    xs.append(y)
    saves.append(sv)
    (y, h), sv = _sc_layer_fwd(h, F["sc_w_in", 0], F["sc_conv_w", 0], F["sc_w_out", 0], xs[2], "sc", norm_g[3:4])
    xs.append(y)
    saves.append(sv)
    (y, _), sv, _ = _dn_layer_fwd(h, _dn_split_w_in(F["dn_w_in", 1]), F["dn_conv_w", 1], dn_a_log[1:2], dn_dt_bias[1:2],
                                  F["dn_o_norm_g", 1], F["dn_w_out", 1], xs[3], "dn1")
    xs.append(y)
    saves.append(sv)
    dx, loss_part = _loss_head(xs[4], loss_target[0])

    G, dnorm, landed = {}, [None] * 4, {}

    def keep(grads, j):
        G.update({(k, j): g for k, g in grads.items()})

    (dx, dnorm[3]), grads, _, _ = _dn_layer_bwd(dx, saves[3], "dn1", (xs[3], norm_g[3:4], dx))
    keep(grads, 1)
    (dx, dnorm[2]), grads = _sc_layer_bwd(dx, saves[2], "sc", (xs[2], norm_g[2:3], dx))
    keep(grads, 0)
    (dx, dnorm[1]), grads, got = _sb_layer_bwd(dx, saves[1], "sb", (xs[1], norm_g[1:2], dx), exchange_of(_EXCHANGE_A, G))
    keep(grads, 0)
    landed.update(zip(_EXCHANGE_A, got))

    def exchange_b(dw_out):
        G["dn_w_out", 0] = dw_out
        return exchange_of(_EXCHANGE_B, G)

    def exchange_c(grads):
        keep(grads, 0)
        return exchange_of(_EXCHANGE_C, G)

    (dx, dnorm[0]), grads, got, got_late = _dn_layer_bwd(dx, saves[0], "dn0", (xs[0], norm_g[0:1], dx), exchange_b, exchange_c)
    landed.update(zip(_EXCHANGE_B, got))
    landed.update(zip(_EXCHANGE_C, got_late))
    replicated = dict(norm_g=jnp.concatenate(dnorm, axis=0),
                      dn_a_log=jnp.concatenate([G["dn_a_log", 0], G["dn_a_log", 1]], axis=0),
                      dn_dt_bias=jnp.concatenate([G["dn_dt_bias", 0], G["dn_dt_bias", 1]], axis=0),
                      sb_q_norm_g=G["sb_q_norm_g", 0], sb_k_norm_g=G["sb_k_norm_g", 0])
    got = _comm_call(_Exchange([_pack_replicated(replicated)], [False]), "exchange_replicated")

    res = {}
    for k in _ORDER:
        if k in _REPLICATED:
            continue
        per_layer = []
        for j in range(w[k].shape[0]):
            shape = w[k][j].shape
            outs = _adamw(_as_2d(w[k][j]), _as_2d(m[k][j]), _as_2d(v[k][j]), landed[k, j], f"adamw_{k}{j}")
            per_layer.append([o.reshape(shape) for o in outs])
        res[k] = [jnp.stack([layer[i] for layer in per_layer], axis=0) for i in range(4)]
    outs = _adamw(_pack_replicated(w), _pack_replicated(m), _pack_replicated(v), got[-1], "adamw_replicated")
    unpacked = [_unpack_replicated(o, w) for o in outs]
    for k in _REPLICATED:
        res[k] = [u[k] for u in unpacked]

    loss = lax.psum(loss_part[0, 0], ("x", "y", "c"))
    return (loss, dx[None]) + tuple(res[k][0] for k in _ORDER) + tuple(res[k][1] for k in _ORDER) \
        + tuple(res[k][2] for k in _ORDER) + tuple(res[k][3] for k in _ORDER)
```

```python
import functools
import itertools
import math

import jax
import jax.numpy as jnp
from jax import lax
from jax.experimental import pallas as pl
from jax.experimental.pallas import tpu as pltpu

F32 = jnp.float32
BF16 = jnp.bfloat16
HIGHEST = lax.Precision.HIGHEST

N_DEV = 8
D_MODEL = 1024
RMS_EPS = 1e-6
L2_EPS = 1e-6

DN_HEADS = 8
DN_DK = 128
DN_DV = 256
DN_QK_W = DN_HEADS * DN_DK
DN_V_W = DN_HEADS * DN_DV
DN_CONV = 4
DN_CHUNK = 64
DN_CONV_W = 2 * DN_QK_W + DN_V_W
DN_IN = DN_CONV_W + DN_V_W + 2 * DN_HEADS
DN_AB_PAD = 128
DN_PREP_BLK = 512

SB_HEADS = 16
SB_DH = 64
SB_W = SB_HEADS * SB_DH
SB_PAIRS = SB_HEADS // 2
SB_TQ = 256
SB_TK = 128
SB_DEAD = -106.0

SC_W = 2 * D_MODEL
SC_CONV = 3
SC_BLK = 512
SC_NBLK = SC_W // SC_BLK

ADAM_LR = 0.001
ADAM_B1 = 0.9
ADAM_B2 = 0.999
ADAM_EPS = 1e-08
ADAM_WD = 0.01
ADAM_STEP = 10

LANE = 128
SUBLANE = 8
HALO = SUBLANE
LONG_ROW_TILE = 512
NORM_FUSED_TM = 512
DEEP_TK = 2048
WIDE_TN = 2048
WIDE_ROW_TILE = 128
CONV_ROW_TILE = 256
VMEM_LIMIT = 48 * 2 ** 20

NN = ((1,), (0,))
NT = ((1,), (1,))
TN = ((0,), (0,))


def _dot(a, b, dims=NN, precision=None):
    return lax.dot_general(a, b, (dims, ((), ())), precision=precision, preferred_element_type=F32)


def _bdot(a, b, dims=NN):
    return _dot(a.astype(BF16), b.astype(BF16), dims)


def _hdot(a, b, dims=NN):
    return _dot(a, b, dims, precision=HIGHEST)


def _tile(dim, pref, align=LANE):
    t = (min(pref, dim) // align) * align
    while t >= align:
        if dim % t == 0:
            return t
        t -= align
    return dim


def _params(*sem):
    return pltpu.CompilerParams(dimension_semantics=sem, vmem_limit_bytes=VMEM_LIMIT)


def _sigmoid(x):
    return 0.5 * jnp.tanh(0.5 * x) + 0.5


def _softplus(x):
    return jnp.maximum(x, 0.0) + jnp.log(1.0 + jnp.exp(-jnp.abs(x)))


def _silu_and_grad(x):
    s = _sigmoid(x)
    return x * s, s * (1.0 + x * (1.0 - s))


def _iota2(shape, dim):
    return lax.broadcasted_iota(jnp.int32, shape, dim)


def _matmul(a, b, mode, name, out_dtype=F32, add=None, b_cols=None, blocked_b=False, blocked_out=0,
            norm_fwd=None, norm_bwd=None, tm=1024, tn=1024, tk=1024):
    b_rows, b_width = (b.shape[1], b.shape[0] * b.shape[2]) if blocked_b else b.shape
    c0, b_used = b_cols if b_cols is not None else (0, b_width)
    if mode == "nn":
        (M, K), (K2, N) = a.shape, (b_rows, b_used)
    elif mode == "nt":
        (M, K), (N, K2) = a.shape, (b_rows, b_used)
    else:
        (K, M), (K2, N) = a.shape, (b_rows, b_used)
    assert K == K2, (a.shape, b.shape, mode)
    if mode == "tn":
        tk = max(tk, DEEP_TK)
    elif norm_fwd is None and norm_bwd is None and add is None:
        tn = max(tn, WIDE_TN)
    tm, tn, tk = _tile(M, tm), _tile(N, tn), _tile(K, tk)
    if blocked_b and mode == "nt":
        tk = b.shape[2]
    elif blocked_b:
        tn = b.shape[2]
    if blocked_out:
        tn = N // blocked_out
    nk = K // tk
    dims = {"nn": NN, "nt": NT, "tn": TN}[mode]
    a_spec = pl.BlockSpec((tk, tm), lambda i, j, k: (k, i)) if mode == "tn" else pl.BlockSpec((tm, tk), lambda i, j, k: (i, k))
    if mode == "nt":
        cb0 = c0 // tk
        assert c0 % tk == 0
        b_spec = (pl.BlockSpec((None, tn, tk), lambda i, j, k: (k + cb0, j, 0)) if blocked_b
                  else pl.BlockSpec((tn, tk), lambda i, j, k: (j, k + cb0)))
    else:
        cb0 = c0 // tn
        assert c0 % tn == 0
        b_spec = (pl.BlockSpec((None, tk, tn), lambda i, j, k: (j + cb0, k, 0)) if blocked_b
                  else pl.BlockSpec((tk, tn), lambda i, j, k: (k, j + cb0)))
    o_spec = pl.BlockSpec((tm, tn), lambda i, j, k: (i, j))
    out_spec = pl.BlockSpec((None, tm, tn), lambda i, j, k: (j, i, 0)) if blocked_out else o_spec
    out_shape = (blocked_out, M, tn) if blocked_out else (M, N)
    has_add = add is not None
    vec_spec = pl.BlockSpec((1, tn), lambda i, j, k: (0, j))
    assert not (norm_fwd is not None or norm_bwd is not None) or tn == N
    extra_in, extra_specs = [], []
    if has_add:
        extra_in, extra_specs = [add], [o_spec]
    if norm_fwd is not None:
        extra_in, extra_specs = extra_in + [norm_fwd], extra_specs + [vec_spec]
        out_specs = [o_spec, o_spec]
        out_shapes = [jax.ShapeDtypeStruct((M, N), out_dtype), jax.ShapeDtypeStruct((M, N), BF16)]
    elif norm_bwd is not None:
        extra_in, extra_specs = extra_in + list(norm_bwd), extra_specs + [o_spec, vec_spec, o_spec]
        out_specs = [o_spec, vec_spec]
        out_shapes = [jax.ShapeDtypeStruct((M, N), F32), jax.ShapeDtypeStruct((1, N), F32)]
    else:
        out_specs, out_shapes = out_spec, jax.ShapeDtypeStruct(out_shape, out_dtype)

    def body(*refs):
        a_ref, b_ref = refs[0], refs[1]
        extra = list(refs[2:2 + len(extra_in)])
        outs = refs[2 + len(extra_in):]
        add_ref = extra.pop(0) if has_add else None
        p = _bdot(a_ref[...], b_ref[...], dims)

        def finish(acc):
            if has_add:
                acc = acc + add_ref[...]
            if norm_bwd is not None:
                _rmsnorm_bwd_tile(acc, *extra, outs[0], outs[1], first=pl.program_id(0) == 0)
                return
            outs[0][...] = acc.astype(out_dtype)
            if norm_fwd is not None:
                r = lax.rsqrt(jnp.mean(acc * acc, axis=-1, keepdims=True) + RMS_EPS)
                outs[1][...] = (acc * r * extra[0][...]).astype(BF16)

        if nk == 1:
            finish(p)
        else:
            acc_ref = refs[-1]
            k = pl.program_id(2)

            @pl.when(k == 0)
            def _():
                acc_ref[...] = p

            @pl.when(k > 0)
            def _():
                acc_ref[...] += p

            @pl.when(k == nk - 1)
            def _():
                finish(acc_ref[...])

    return pl.pallas_call(
        body, name=name, grid=(M // tm, N // tn, nk),
        in_specs=[a_spec, b_spec] + extra_specs, out_specs=out_specs, out_shape=out_shapes,
        scratch_shapes=[pltpu.VMEM((tm, tn), F32)] if nk > 1 else [],
        compiler_params=(_params("arbitrary", "arbitrary", "arbitrary") if norm_bwd is not None
                         else _params("parallel", "parallel", "arbitrary")),
    )(a, b, *extra_in)


def _rmsnorm_bwd_tile(dh, x_ref, g_ref, res_ref, dx_ref, dg_ref, first):
    xv = x_ref[...]
    r = lax.rsqrt(jnp.mean(xv * xv, axis=-1, keepdims=True) + RMS_EPS)
    xh = xv * r
    dxh = dh * g_ref[...]
    m = jnp.mean(dxh * xh, axis=-1, keepdims=True)
    dx_ref[...] = res_ref[...] + r * (dxh - xh * m)
    part = jnp.sum(dh * xh, axis=0, keepdims=True)

    @pl.when(first)
    def _():
        dg_ref[...] = part

    @pl.when(jnp.logical_not(first))
    def _():
        dg_ref[...] += part


def _matmul_nt_sum(pairs, name, comm=None, norm_bwd=None, tm=NORM_FUSED_TM, tk=1024):
    M, N = pairs[0][0].shape[0], pairs[0][1].shape[0]
    tm = _tile(M, tm)
    tks = [_tile(a.shape[1], tk) for a, _, _ in pairs]
    steps = [a.shape[1] // t for (a, _, _), t in zip(pairs, tks)]
    offs = [sum(steps[:p]) for p in range(len(pairs))]
    total = sum(steps)

    n_extra = 3 if norm_bwd is not None else 0

    def body(*refs):
        a_refs, b_refs = refs[0:2 * len(pairs):2], refs[1:2 * len(pairs):2]
        extra = refs[2 * len(pairs):2 * len(pairs) + n_extra]
        outs, acc_ref = refs[2 * len(pairs) + n_extra:-1], refs[-1]
        k = pl.program_id(1)
        for p in range(len(pairs)):
            @pl.when((k >= offs[p]) & (k < offs[p] + steps[p]))
            def _(p=p):
                prod = _bdot(a_refs[p][...], b_refs[p][...], NT)
                if p == 0:
                    @pl.when(k == 0)
                    def _():
                        acc_ref[...] = prod

                    @pl.when(k > 0)
                    def _():
                        acc_ref[...] += prod
                else:
                    acc_ref[...] += prod

        @pl.when(k == total - 1)
        def _():
            if norm_bwd is not None:
                _rmsnorm_bwd_tile(acc_ref[...], *extra, outs[0], outs[1], first=pl.program_id(0) == 0)
            else:
                outs[0][...] = acc_ref[...]

    in_specs, args = [], []
    for (a, b, c0), t, off, n in zip(pairs, tks, offs, steps):
        assert c0 % t == 0
        pick = lambda k, off=off, n=n: jnp.clip(k - off, 0, n - 1)
        in_specs += [pl.BlockSpec((tm, t), lambda i, k, pick=pick: (i, pick(k))),
                     pl.BlockSpec((N, t), lambda i, k, pick=pick, cb0=c0 // t: (0, pick(k) + cb0))]
        args += [a, b]
    row, vec = pl.BlockSpec((tm, N), lambda i, k: (i, 0)), pl.BlockSpec((1, N), lambda i, k: (0, 0))
    if norm_bwd is not None:
        in_specs += [row, vec, row]
        args += list(norm_bwd)
        out_specs, out_shape = [row, vec], [jax.ShapeDtypeStruct((M, N), F32), jax.ShapeDtypeStruct((1, N), F32)]
    else:
        out_specs, out_shape = [row], [jax.ShapeDtypeStruct((M, N), F32)]
    outs, landed = _call(body, comm, name=name, grid=(M // tm, total), in_specs=in_specs, out_specs=out_specs,
                         out_shape=out_shape, scratch_shapes=[pltpu.VMEM((tm, N), F32)],
                         semantics=("arbitrary", "arbitrary"), args=tuple(args))
    return (outs if norm_bwd is not None else outs[0]), landed


def _rmsnorm_fwd(x, g, name, comm=None):
    T, D = x.shape
    tt = _tile(T, LONG_ROW_TILE, SUBLANE)

    def body(x_ref, g_ref, o_ref):
        xv = x_ref[...]
        r = lax.rsqrt(jnp.mean(xv * xv, axis=-1, keepdims=True) + RMS_EPS)
        o_ref[...] = (xv * r * g_ref[...]).astype(BF16)

    outs, landed = _call(
        body, comm, name=name, grid=(T // tt,),
        in_specs=[pl.BlockSpec((tt, D), lambda i: (i, 0)), pl.BlockSpec((1, D), lambda i: (0, 0))],
        out_specs=[pl.BlockSpec((tt, D), lambda i: (i, 0))], out_shape=[jax.ShapeDtypeStruct((T, D), BF16)],
        scratch_shapes=[], semantics=("parallel",), args=(x, g))
    return outs[0], landed


def _rmsnorm_bwd(dh, x, g, dx_res, name):
    T, D = x.shape
    tt = _tile(T, LONG_ROW_TILE // 2, SUBLANE)

    def body(dh_ref, x_ref, g_ref, res_ref, dx_ref, dg_ref):
        _rmsnorm_bwd_tile(dh_ref[...], x_ref, g_ref, res_ref, dx_ref, dg_ref, first=pl.program_id(0) == 0)

    row = pl.BlockSpec((tt, D), lambda i: (i, 0))
    vec = pl.BlockSpec((1, D), lambda i: (0, 0))
    return pl.pallas_call(
        body, name=name, grid=(T // tt,),
        in_specs=[row, row, vec, row], out_specs=[row, vec],
        out_shape=[jax.ShapeDtypeStruct((T, D), F32), jax.ShapeDtypeStruct((1, D), F32)],
        compiler_params=_params("arbitrary"),
    )(dh, x, g, dx_res)


def _loss_head(y, target, name="loss_head"):
    T, D = y.shape
    tt = _tile(T, LONG_ROW_TILE, SUBLANE)

    def body(y_ref, t_ref, dy_ref, l_ref):
        e = y_ref[...] - t_ref[...]
        dy_ref[...] = e * (1.0 / D)
        s = jnp.sum(jnp.sum(e * e, axis=1, keepdims=True), axis=0, keepdims=True) * (0.5 / D)
        s = jnp.broadcast_to(s, (1, LANE))

        @pl.when(pl.program_id(0) == 0)
        def _():
            l_ref[...] = s

        @pl.when(pl.program_id(0) > 0)
        def _():
            l_ref[...] += s

    row = pl.BlockSpec((tt, D), lambda i: (i, 0))
    return pl.pallas_call(
        body, name=name, grid=(T // tt,),
        in_specs=[row, row], out_specs=[row, pl.BlockSpec((1, LANE), lambda i: (0, 0))],
        out_shape=[jax.ShapeDtypeStruct((T, D), F32), jax.ShapeDtypeStruct((1, LANE), F32)],
        compiler_params=_params("arbitrary"),
    )(y, target)


def _down(x, k):
    return pltpu.roll(x, k, 0) if k else x


def _up(x, k):
    return pltpu.roll(x, x.shape[0] - k, 0) if k else x


def _sc_fwd(proj, conv_w, name):
    T = proj.shape[0]
    tt = _tile(T, WIDE_ROW_TILE, SUBLANE)
    B = SC_BLK

    def body(p_ref, ph_ref, w_ref, o_ref):
        keep = (pl.program_id(0) > 0).astype(F32)
        for j in range(SC_NBLK):
            cb, cc, cu, cg = (slice(k * SC_W + j * B, k * SC_W + (j + 1) * B) for k in range(4))
            cw = slice(j * B, (j + 1) * B)
            z = jnp.concatenate([ph_ref[:, cc] * ph_ref[:, cu] * keep, p_ref[:, cc] * p_ref[:, cu]], axis=0)
            cz = (w_ref[2:3, cw] * z + w_ref[1:2, cw] * _down(z, 1) + w_ref[0:1, cw] * _down(z, 2))[HALO:]
            gate = p_ref[:, cg]
            o_ref[:, cw] = (p_ref[:, cb] * cz * (gate * _sigmoid(gate))).astype(BF16)

    return pl.pallas_call(
        body, name=name, grid=(T // tt,),
        in_specs=[pl.BlockSpec((tt, 4 * SC_W), lambda i: (i, 0)),
                  pl.BlockSpec((HALO, 4 * SC_W), lambda i: (jnp.maximum(i * (tt // HALO) - 1, 0), 0)),
                  pl.BlockSpec((SC_CONV, SC_W), lambda i: (0, 0))],
        out_specs=pl.BlockSpec((tt, SC_W), lambda i: (i, 0)),
        out_shape=jax.ShapeDtypeStruct((T, SC_W), BF16),
        compiler_params=_params("parallel"),
    )(proj, proj, conv_w)


def _sc_bwd(dyg, proj, conv_w, name):
    T = proj.shape[0]
    tt = _tile(T, WIDE_ROW_TILE, SUBLANE)
    nt = T // tt
    B = SC_BLK
    hb = tt // HALO

    def body(d_ref, dn_ref, p_ref, pp_ref, pn_ref, w_ref, o_ref, dw_ref):
        i = pl.program_id(0)
        keep_p = (i > 0).astype(F32)
        keep_n = (i < nt - 1).astype(F32)
        main = slice(HALO, HALO + tt)
        parts = []
        for j in range(SC_NBLK):
            cw = slice(j * B, (j + 1) * B)

            def ext(k):
                s = slice(k * SC_W + j * B, k * SC_W + (j + 1) * B)
                return s, jnp.concatenate([pp_ref[:, s] * keep_p, p_ref[:, s], pn_ref[:, s]], axis=0)

            (sb, b), (sc, c), (su, u), (sg_, gate) = ext(0), ext(1), ext(2), ext(3)
            dyg_e = jnp.concatenate([jnp.zeros((HALO, B), F32), d_ref[:, cw], dn_ref[:, cw] * keep_n], axis=0)
            w0, w1, w2 = w_ref[0:1, cw], w_ref[1:2, cw], w_ref[2:3, cw]
            z = c * u
            z1, z2 = _down(z, 1), _down(z, 2)
            cz = w2 * z + w1 * z1 + w0 * z2
            sg, dsg = _silu_and_grad(gate)
            dy = dyg_e * sg
            dcz = dy * b
            dz = w2 * dcz + w1 * _up(dcz, 1) + w0 * _up(dcz, 2)
            o_ref[:, sb] = (dy * cz)[main].astype(BF16)
            o_ref[:, sc] = (dz * u)[main].astype(BF16)
            o_ref[:, su] = (dz * c)[main].astype(BF16)
            o_ref[:, sg_] = (dyg_e * (b * cz) * dsg)[main].astype(BF16)
            dcm = dcz[main]
            parts.append(jnp.concatenate([jnp.sum(dcm * z2[main], axis=0, keepdims=True),
                                          jnp.sum(dcm * z1[main], axis=0, keepdims=True),
                                          jnp.sum(dcm * z[main], axis=0, keepdims=True)], axis=0))
        part = jnp.concatenate(parts, axis=1)

        @pl.when(i == 0)
        def _():
            dw_ref[...] = part

        @pl.when(i > 0)
        def _():
            dw_ref[...] += part

    nxt = lambda i: (jnp.minimum((i + 1) * hb, nt * hb - 1), 0)
    return pl.pallas_call(
        body, name=name, grid=(nt,),
        in_specs=[pl.BlockSpec((tt, SC_W), lambda i: (i, 0)),
                  pl.BlockSpec((HALO, SC_W), nxt),
                  pl.BlockSpec((tt, 4 * SC_W), lambda i: (i, 0)),
                  pl.BlockSpec((HALO, 4 * SC_W), lambda i: (jnp.maximum(i * hb - 1, 0), 0)),
                  pl.BlockSpec((HALO, 4 * SC_W), nxt),
                  pl.BlockSpec((SC_CONV, SC_W), lambda i: (0, 0))],
        out_specs=[pl.BlockSpec((tt, 4 * SC_W), lambda i: (i, 0)), pl.BlockSpec((SC_CONV, SC_W), lambda i: (0, 0))],
        out_shape=[jax.ShapeDtypeStruct((T, 4 * SC_W), BF16), jax.ShapeDtypeStruct((SC_CONV, SC_W), F32)],
        compiler_params=_params("arbitrary"),
    )(dyg, dyg, proj, proj, proj, conv_w)


def _split3_dot(x, m):
    hi = x.astype(BF16)
    r1 = x - hi.astype(F32)
    mid = r1.astype(BF16)
    lo = (r1 - mid.astype(F32)).astype(BF16)
    return _dot(hi, m) + _dot(mid, m) + _dot(lo, m)


def _split2_dot(x, m):
    hi = x.astype(BF16)
    lo = (x - hi.astype(F32)).astype(BF16)
    return _dot(hi, m) + _dot(lo, m)


def _head_mean_matrix():
    r, c = _iota2((LANE, LANE), 0), _iota2((LANE, LANE), 1)
    return jnp.where((r // SB_DH) == (c // SB_DH), 1.0 / SB_DH, 0.0).astype(BF16)


def _sb_prep(proj, qg2, kg2, name):
    T = proj.shape[0]
    tt = _tile(T, WIDE_ROW_TILE, SUBLANE)

    def body(p_ref, qg_ref, kg_ref, q_ref, k_ref, v_ref):
        bd = _head_mean_matrix()

        def norm(x, g, scale):
            r = lax.rsqrt(_split3_dot(x * x, bd) + RMS_EPS)
            return (x * r * g * scale).astype(BF16)

        v_ref[...] = p_ref[:, 2 * SB_W:3 * SB_W].astype(BF16)
        for p in range(SB_PAIRS):
            cols = slice(p * LANE, (p + 1) * LANE)
            q_ref[:, cols] = norm(p_ref[:, cols], qg_ref[...], SB_DH ** -0.5)
            k_ref[:, cols] = norm(p_ref[:, SB_W + p * LANE:SB_W + (p + 1) * LANE], kg_ref[...], 1.0)

    blk = pl.BlockSpec((tt, SB_W), lambda i: (i, 0))
    vec = pl.BlockSpec((1, LANE), lambda i: (0, 0))
    return pl.pallas_call(
        body, name=name, grid=(T // tt,),
        in_specs=[pl.BlockSpec((tt, 4 * SB_W), lambda i: (i, 0)), vec, vec],
        out_specs=[blk, blk, blk],
        out_shape=[jax.ShapeDtypeStruct((T, SB_W), BF16)] * 3,
        compiler_params=_params("parallel"),
    )(proj, qg2, kg2)


def _sb_prep_bwd(proj, dqn, dkn, dv, dgate, qg2, kg2, name):
    T = proj.shape[0]
    tt = _tile(T, WIDE_ROW_TILE, SUBLANE)

    def body(p_ref, dq_ref, dk_ref, dv_ref, dg_ref, qg_ref, kg_ref, o_ref, dqg_ref, dkg_ref):
        i = pl.program_id(0)
        bd = _head_mean_matrix()

        def norm_bwd(x, g, dy):
            r = lax.rsqrt(_split3_dot(x * x, bd) + RMS_EPS)
            xh = x * r
            dxh = dy * g
            m = _split3_dot(dxh * xh, bd)
            return r * (dxh - xh * m), jnp.sum(dy * xh, axis=0, keepdims=True)

        o_ref[:, 2 * SB_W:3 * SB_W] = dv_ref[...].astype(BF16)
        o_ref[:, 3 * SB_W:4 * SB_W] = dg_ref[...].astype(BF16)
        pq = pk = jnp.zeros((1, LANE), F32)
        for p in range(SB_PAIRS):
            cols, kcols = slice(p * LANE, (p + 1) * LANE), slice(SB_W + p * LANE, SB_W + (p + 1) * LANE)
            dxq, sq = norm_bwd(p_ref[:, cols], qg_ref[...], dq_ref[:, cols])
            dxk, sk = norm_bwd(p_ref[:, kcols], kg_ref[...], dk_ref[:, cols])
            o_ref[:, cols] = dxq.astype(BF16)
            o_ref[:, kcols] = dxk.astype(BF16)
            pq, pk = pq + sq, pk + sk

        @pl.when(i == 0)
        def _():
            dqg_ref[...] = pq
            dkg_ref[...] = pk

        @pl.when(i > 0)
        def _():
            dqg_ref[...] += pq
            dkg_ref[...] += pk

    blk = pl.BlockSpec((tt, SB_W), lambda i: (i, 0))
    vec = pl.BlockSpec((1, LANE), lambda i: (0, 0))
    wide = pl.BlockSpec((tt, 4 * SB_W), lambda i: (i, 0))
    return pl.pallas_call(
        body, name=name, grid=(T // tt,),
        in_specs=[wide, blk, blk, blk, blk, vec, vec],
        out_specs=[wide, vec, vec],
        out_shape=[jax.ShapeDtypeStruct((T, 4 * SB_W), BF16)] + [jax.ShapeDtypeStruct((1, LANE), F32)] * 2,
        compiler_params=_params("arbitrary"),
    )(proj, dqn, dkn, dv, dgate, qg2, kg2)


def _fold_heads(part, name):
    def body(p_ref, o_ref):
        r, c = _iota2((LANE, SB_DH), 0), _iota2((LANE, SB_DH), 1)
        fold = jnp.where((r % SB_DH) == c, 1.0, 0.0).astype(F32)
        o_ref[...] = jnp.sum(_hdot(p_ref[...], fold), axis=0, keepdims=True)

    return pl.pallas_call(body, name=name, out_shape=jax.ShapeDtypeStruct((1, SB_DH), F32))(part)


def _sb_masks():
    lane = _iota2((1, LANE), 1)
    return lane < SB_DH


def _sb_attn_fwd(qn, kn, vb, proj, name, comm=None):
    T = qn.shape[0]
    tq, tk = _tile(T, SB_TQ, SUBLANE), SB_TK
    assert tq % tk == 0

    def body(q_ref, k_ref, v_ref, g_ref, o_ref, og_ref, lt_ref, done_ref):
        i = pl.program_id(1)
        ma = _sb_masks()
        q2 = q_ref[...]
        zero = jnp.zeros_like(q2)
        qs = (jnp.where(ma, q2, zero), jnp.where(ma, zero, q2))
        upper = (_iota2((tk, tk), 0) > _iota2((tk, tk), 1)).astype(BF16)
        qpos = i * tq + _iota2((tq, tk), 0)
        nb = tq // tk

        def trip(kb_top, masked, carry):
            acc, la, lb = carry
            chains = [(b, h) for b in range(nb) for h in range(2)]
            k2s, vss, masks = [], [], []
            for b in range(nb):
                kb = kb_top - b
                rows = pl.ds(pl.multiple_of(kb * tk, tk), tk)
                k2s.append(k_ref[rows, :])
                v2 = v_ref[rows, :]
                zv = jnp.zeros_like(v2)
                vss.append((jnp.where(ma, v2, zv), jnp.where(ma, zv, v2)))
                masks.append((kb * tk + _iota2((tq, tk), 1)) < qpos if masked else None)
            zs = [_dot(qs[h], k2s[b], NT) for b, h in chains]
            ts = [jnp.log(1.0 + jnp.exp(-jnp.abs(z))) for z in zs]
            ls = [-(jnp.maximum(z, 0.0) + t) for z, t in zip(zs, ts)]
            if masked:
                ls = [jnp.where(masks[b], l, 0.0) for (b, h), l in zip(chains, ls)]
            cums = [_split2_dot(l, upper) for l in ls]
            sums = [jnp.sum(l, axis=1, keepdims=True) for l in ls]
            offs, tot = {}, [la, lb]
            for b in range(nb):
                for h in range(2):
                    offs[(b, h)] = tot[h]
                    tot[h] = tot[h] + sums[chains.index((b, h))]
            ws = [jnp.exp(jnp.minimum(z, 0.0) - t + c + offs[ch]) for ch, z, t, c in zip(chains, zs, ts, cums)]
            if masked:
                ws = [jnp.where(masks[b], w, 0.0) for (b, h), w in zip(chains, ws)]
            for (b, h), w in zip(chains, ws):
                acc = acc + _dot(w.astype(BF16), vss[b][h])
            return acc, tot[0], tot[1]

        def largest(la, lb):
            return jnp.max(jnp.maximum(la, lb))

        z1 = jnp.zeros((tq, 1), F32)
        acc, la, lb = trip((i + 1) * nb - 1, True, (jnp.zeros((tq, LANE), F32), z1, z1))

        def live(c):
            return (c[0] < i) & (c[4] > SB_DEAD)

        def more(c):
            j, acc, la, lb, _ = c
            acc, la, lb = trip((i - j) * nb - 1, False, (acc, la, lb))
            return j + 1, acc, la, lb, largest(la, lb)

        done, acc, la, lb, _ = lax.while_loop(live, more, (jnp.int32(0), acc, la, lb, largest(la, lb)))
        gate = g_ref[...]
        o_ref[...] = acc
        og_ref[...] = (acc * (gate * _sigmoid(gate))).astype(BF16)
        lt_ref[...] = jnp.where(_iota2((tq, 2), 1) == 0, la, lb)
        done_ref[...] = jnp.full((SUBLANE, LANE), done, F32)

    nq = T // tq
    qblk = pl.BlockSpec((tq, LANE), lambda p, i: (i, p))
    full = pl.BlockSpec((T, LANE), lambda p, i: (0, p))
    return _call(
        body, comm, name=name, grid=(SB_PAIRS, nq),
        in_specs=[qblk, full, full, pl.BlockSpec((tq, LANE), lambda p, i: (i, 3 * SB_PAIRS + p))],
        out_specs=[qblk, qblk, pl.BlockSpec((None, tq, 2), lambda p, i: (p, i, 0)),
                   pl.BlockSpec((None, None, SUBLANE, LANE), lambda p, i: (p, i, 0, 0))],
        out_shape=[jax.ShapeDtypeStruct((T, SB_W), F32), jax.ShapeDtypeStruct((T, SB_W), BF16),
                   jax.ShapeDtypeStruct((SB_PAIRS, T, 2), F32), jax.ShapeDtypeStruct((SB_PAIRS, nq, SUBLANE, LANE), F32)],
        scratch_shapes=[], semantics=("parallel", "parallel"), args=(qn, kn, vb, proj))


def _sb_attn_bwd(qn, kn, vb, dog, o, ltot, done, proj, name, comm=None):
    T = qn.shape[0]
    tq, tk = _tile(T, SB_TQ, SUBLANE), SB_TK

    def body(q_ref, k_ref, v_ref, dog_ref, o_ref, lt_ref, done_ref, g_ref, dq_ref, dk_ref, dv_ref, dgate_ref):
        i = pl.program_id(1)
        first_trip = i - jnp.max(done_ref[...]).astype(jnp.int32)

        @pl.when(i == 0)
        def _():
            dk_ref[...] = jnp.zeros_like(dk_ref)
            dv_ref[...] = jnp.zeros_like(dv_ref)

        ma = _sb_masks()
        gate, o2, dog2 = g_ref[...], o_ref[...], dog_ref[...]
        sg, dsg = _silu_and_grad(gate)
        do2 = dog2 * sg
        dgate_ref[...] = dog2 * o2 * dsg
        lt = lt_ref[...]
        first = _iota2((tq, 2), 1) == 0
        ltots = (jnp.sum(jnp.where(first, lt, 0.0), axis=1, keepdims=True),
                 jnp.sum(jnp.where(first, 0.0, lt), axis=1, keepdims=True))
        q2 = q_ref[...]
        zq = jnp.zeros_like(q2)
        qs = (jnp.where(ma, q2, zq), jnp.where(ma, zq, q2))
        dob = do2.astype(BF16)
        dos = (jnp.where(ma, dob, zq), jnp.where(ma, zq, dob))
        upto = (_iota2((tk, tk), 0) <= _iota2((tk, tk), 1)).astype(BF16)
        before = (_iota2((tk, tk), 0) < _iota2((tk, tk), 1)).astype(BF16)
        qpos = i * tq + _iota2((tq, tk), 0)
        nb = tq // tk

        def trip(kb_bot, masked, carry):
            dq, la, lb, ea, eb = carry
            chains = [(b, h) for b in range(nb) for h in range(2)]
            rows, k2s, v2s, kss, masks = [], [], [], [], []
            for b in range(nb):
                kb = kb_bot + b
                rows.append(pl.ds(pl.multiple_of(kb * tk, tk), tk))
                k2 = k_ref[rows[b], :]
                zk = jnp.zeros_like(k2)
                k2s.append(k2)
                v2s.append(v_ref[rows[b], :])
                kss.append((jnp.where(ma, k2, zk), jnp.where(ma, zk, k2)))
                masks.append((kb * tk + _iota2((tq, tk), 1)) < qpos if masked else None)

            def keep(vals):
                return [jnp.where(masks[b], x, 0.0) for (b, h), x in zip(chains, vals)] if masked else vals

            zs = [_dot(qs[h], k2s[b], NT) for b, h in chains]
            dws = [_dot(dos[h], v2s[b], NT) for b, h in chains]
            ts = [jnp.log(1.0 + jnp.exp(-jnp.abs(z))) for z in zs]
            ls = keep([-(jnp.maximum(z, 0.0) + t) for z, t in zip(zs, ts)])
            lps = [jnp.minimum(z, 0.0) - t for z, t in zip(zs, ts)]
            cums = [_split3_dot(l, upto) for l in ls]
            lsums = [jnp.sum(l, axis=1, keepdims=True) for l in ls]
            offs, tot = {}, [la, lb]
            for b in range(nb):
                for h in range(2):
                    offs[(b, h)] = tot[h]
                    tot[h] = tot[h] + lsums[chains.index((b, h))]
            ws = keep([jnp.exp(lp + (ltots[h] - (offs[(b, h)] + c))) for (b, h), lp, c in zip(chains, lps, cums)])
            es = [dw * w for dw, w in zip(dws, ws)]
            ecums = [_split2_dot(e, before) for e in es]
            esums = [jnp.sum(e, axis=1, keepdims=True) for e in es]
            eoffs, etot = {}, [ea, eb]
            for b in range(nb):
                for h in range(2):
                    eoffs[(b, h)] = etot[h]
                    etot[h] = etot[h] + esums[chains.index((b, h))]
            dzs = keep([e - jnp.exp(lp) * (e + eoffs[ch] + ec) for ch, e, lp, ec in zip(chains, es, lps, ecums)])
            dzs = [dz.astype(BF16) for dz in dzs]
            wbs = [w.astype(BF16) for w in ws]
            for (b, h), dz in zip(chains, dzs):
                dq = dq + _dot(dz, kss[b][h])
            for b in range(nb):
                ia, ib = chains.index((b, 0)), chains.index((b, 1))
                dk_ref[rows[b], :] += _dot(dzs[ia], qs[0], TN) + _dot(dzs[ib], qs[1], TN)
                dv_ref[rows[b], :] += _dot(wbs[ia], dos[0], TN) + _dot(wbs[ib], dos[1], TN)
            return dq, tot[0], tot[1], etot[0], etot[1]

        z1 = jnp.zeros((tq, 1), F32)
        carry = lax.fori_loop(first_trip, i, lambda j, c: trip(j * nb, False, c),
                              (jnp.zeros((tq, LANE), F32), z1, z1, z1, z1))
        dq = trip(i * nb, True, carry)[0]
        dq_ref[...] = dq * (SB_DH ** -0.5)

    qblk = pl.BlockSpec((tq, LANE), lambda p, i: (i, p))
    full = pl.BlockSpec((T, LANE), lambda p, i: (0, p))
    return _call(
        body, comm, name=name, grid=(SB_PAIRS, T // tq),
        in_specs=[qblk, full, full, qblk, qblk, pl.BlockSpec((None, tq, 2), lambda p, i: (p, i, 0)),
                  pl.BlockSpec((None, None, SUBLANE, LANE), lambda p, i: (p, i, 0, 0)),
                  pl.BlockSpec((tq, LANE), lambda p, i: (i, 3 * SB_PAIRS + p))],
        out_specs=[qblk, full, full, qblk],
        out_shape=[jax.ShapeDtypeStruct((T, SB_W), F32)] * 4,
        scratch_shapes=[], semantics=("parallel", "arbitrary"), args=(qn, kn, vb, dog, o, ltot, done, proj))


def _dn_conv(ext, w_ref, cw):
    return (w_ref[3:4, cw] * ext + w_ref[2:3, cw] * _down(ext, 1) + w_ref[1:2, cw] * _down(ext, 2)
            + w_ref[0:1, cw] * _down(ext, 3))


def _dn_prep_bwd(pqkv, conv_w, dact, name):
    T, W = pqkv.shape
    tt = _tile(T, CONV_ROW_TILE, SUBLANE)
    nt = T // tt
    hb = tt // HALO
    B = DN_PREP_BLK
    nq, nqk = DN_QK_W // B, 2 * DN_QK_W // B

    def body(p_ref, pp_ref, pn_ref, w_ref, d_ref, dn_ref, o_ref, dw_ref):
        i = pl.program_id(0)
        keep_p = (i > 0).astype(F32)
        keep_n = (i < nt - 1).astype(F32)
        main = slice(HALO, HALO + tt)
        parts = []
        for cb in range(W // B):
            cw = slice(cb * B, (cb + 1) * B)
            ext = jnp.concatenate([pp_ref[:, cw] * keep_p, p_ref[:, cw], pn_ref[:, cw]], axis=0)
            c = _dn_conv(ext, w_ref, cw)
            s = _sigmoid(c)
            da_dc = s * (1.0 + c * (1.0 - s))
            d_up = jnp.concatenate([jnp.zeros((HALO, B), F32), d_ref[:, cw], dn_ref[:, cw] * keep_n], axis=0)
            if cb < nqk:
                a = c * s
                scale = DN_DK ** -0.5 if cb < nq else 1.0
                normed = []
                for hh in range(B // DN_DK):
                    cols = slice(hh * DN_DK, (hh + 1) * DN_DK)
                    ah = a[:, cols]
                    r = lax.rsqrt(jnp.sum(ah * ah, axis=-1, keepdims=True) + L2_EPS)
                    y = ah * r
                    dy = d_up[:, cols] * scale
                    normed.append(r * (dy - y * jnp.sum(dy * y, axis=-1, keepdims=True)))
                d_up = jnp.concatenate(normed, axis=1)
            dc = d_up * da_dc
            dp = (w_ref[3:4, cw] * dc + w_ref[2:3, cw] * _up(dc, 1) + w_ref[1:2, cw] * _up(dc, 2)
                  + w_ref[0:1, cw] * _up(dc, 3))
            o_ref[:, cw] = dp[main].astype(BF16)
            dcm = dc[main]
            parts.append(jnp.concatenate([jnp.sum(dcm * _down(ext, 3 - k)[main], axis=0, keepdims=True)
                                          for k in range(DN_CONV)], axis=0))
        part = jnp.concatenate(parts, axis=1)

        @pl.when(i == 0)
        def _():
            dw_ref[...] = part

        @pl.when(i > 0)
        def _():
            dw_ref[...] += part

    main_spec = pl.BlockSpec((tt, W), lambda i: (i, 0))
    prev_spec = pl.BlockSpec((HALO, W), lambda i: (jnp.maximum(i * hb - 1, 0), 0))
    next_spec = pl.BlockSpec((HALO, W), lambda i: (jnp.minimum((i + 1) * hb, nt * hb - 1), 0))
    w_spec = pl.BlockSpec((DN_CONV, W), lambda i: (0, 0))
    return pl.pallas_call(
        body, name=name, grid=(nt,),
        in_specs=[main_spec, prev_spec, next_spec, w_spec, main_spec, next_spec],
        out_specs=[main_spec, w_spec],
        out_shape=[jax.ShapeDtypeStruct((T, W), BF16), jax.ShapeDtypeStruct((DN_CONV, W), F32)],
        compiler_params=_params("arbitrary"),
    )(pqkv, pqkv, pqkv, conv_w, dact, dact)


def _dn_gates(a_in, b_in, a_log, dt_bias, name):
    T, H = a_in.shape
    C = DN_CHUNK

    def body(a_ref, b_ref, al_ref, dt_ref, g_ref, beta_ref):
        beta_ref[...] = _sigmoid(b_ref[...])
        g_ref[...] = -jnp.exp(al_ref[...]) * _softplus(a_ref[...] + dt_ref[...])
        tri = (_iota2((C, C), 0) >= _iota2((C, C), 1)).astype(F32)

        def chunk(n, carry):
            rows = pl.ds(pl.multiple_of(n * C, C), C)
            g_ref[rows, :] = _hdot(tri, g_ref[rows, :])
            return carry

        lax.fori_loop(0, T // C, chunk, 0)

    return pl.pallas_call(body, name=name, out_shape=[jax.ShapeDtypeStruct((T, H), F32)] * 2)(a_in, b_in, a_log, dt_bias)


def _dn_gates_bwd(dg, dbeta, a_in, b_in, a_log, dt_bias, name):
    T, H = a_in.shape
    C = DN_CHUNK

    def body(dg_ref, db_ref, a_ref, b_ref, al_ref, dt_ref, da_ref, dbi_ref, dal_ref, ddt_ref):
        tri_t = (_iota2((C, C), 0) <= _iota2((C, C), 1)).astype(F32)

        def chunk(n, carry):
            rows = pl.ds(pl.multiple_of(n * C, C), C)
            da_ref[rows, :] = _hdot(tri_t, dg_ref[rows, :])
            return carry

        lax.fori_loop(0, T // C, chunk, 0)
        dla = da_ref[...]
        x = a_ref[...] + dt_ref[...]
        ea = jnp.exp(al_ref[...])
        da = dla * (-ea) * _sigmoid(x)
        da_ref[...] = da
        dal_ref[...] = jnp.sum(dla * (-ea * _softplus(x)), axis=0, keepdims=True)
        ddt_ref[...] = jnp.sum(da, axis=0, keepdims=True)
        beta = _sigmoid(b_ref[...])
        dbi_ref[...] = db_ref[...] * beta * (1.0 - beta)

    return pl.pallas_call(
        body, name=name,
        out_shape=[jax.ShapeDtypeStruct((T, H), F32)] * 2 + [jax.ShapeDtypeStruct((1, H), F32)] * 2,
    )(dg, dbeta, a_in, b_in, a_log, dt_bias)


def _dn_chunk_terms(q, k, gc, bc):
    C = DN_CHUNK
    r, c = _iota2((C, C), 0), _iota2((C, C), 1)
    lower, strict, eye = r >= c, r > c, r == c
    grow = jnp.sum(jnp.where(eye, gc, 0.0), axis=0, keepdims=True)
    decay = jnp.where(lower, jnp.exp(jnp.where(lower, gc - grow, 0.0)), 0.0)
    last = _iota2((C, 1), 0) == C - 1
    gl = jnp.sum(jnp.where(last, gc, 0.0), axis=0, keepdims=True)
    eg = jnp.exp(gc)
    egl = jnp.exp(gl - gc)
    kb = k * bc
    lmat = jnp.where(strict, _bdot(kb, k, NT) * decay, 0.0)
    aqk = jnp.where(lower, _bdot(q, k, NT) * decay, 0.0)
    return dict(lower=lower, strict=strict, eye=eye, last=last, decay=decay, gl=gl, eg=eg, egl=egl, kb=kb,
                lmat=lmat, aqk=aqk, qd=q * eg, kd=k * egl)


def _split(x):
    hi = x.astype(BF16)
    return hi, (x - hi.astype(F32)).astype(BF16)


def _x3dot(a, b, dims=NN):
    ah, al = a if isinstance(a, tuple) else _split(a)
    bh, bl = b if isinstance(b, tuple) else _split(b)
    return _dot(ah, bh, dims) + (_dot(ah, bl, dims) + _dot(al, bh, dims))


def _interleave(gens):
    for _ in itertools.zip_longest(*gens):
        pass


def _unit_lower_inverse_steps(lmat, eye, out):
    ident = jnp.where(eye, 1.0, 0.0).astype(F32)
    m = -lmat
    inv = ident + m
    for _ in range(int(math.log2(DN_CHUNK)) - 1):
        ms = _split(m)
        m = _x3dot(ms, ms)
        yield
        inv = inv + _x3dot(inv, m)
        yield
    out["tm"] = inv


def _dn_chunk_fwd(pqkv, conv_w, g, beta, pgate, gn, name, comm=None):
    T = pqkv.shape[0]
    C, H = DN_CHUNK, DN_HEADS
    N = T // C
    B = DN_PREP_BLK
    nq, nqk = DN_QK_W // B, 2 * DN_QK_W // B

    def step(p_ref, cw_ref, g_ref, b_ref, pg_ref, gn_ref, act_out, o_ref, og_ref, s_out, t_out, vn_out, u_out, w_out,
             s_scr, tail_scr, a_ref, a_next):
        head_lane = _iota2((C, H), 1)

        def prepare(cb):
            cw = slice(cb * B, (cb + 1) * B)
            ext = jnp.concatenate([tail_scr[:, cw], p_ref[:, cw]], axis=0)
            c = _dn_conv(ext, cw_ref, cw)[HALO:]
            yield
            a = c * _sigmoid(c)
            if cb >= nqk:
                a_next[:, cw] = a
                act_out[:, cw] = a
                return
            scale = DN_DK ** -0.5 if cb < nq else 1.0
            for hh in range(B // DN_DK):
                yield
                ah = a[:, hh * DN_DK:(hh + 1) * DN_DK]
                val = ah * (lax.rsqrt(jnp.sum(ah * ah, axis=-1, keepdims=True) + L2_EPS) * scale)
                cols = slice(cb * B + hh * DN_DK, cb * B + (hh + 1) * DN_DK)
                a_next[:, cols] = val
                act_out[:, cols] = val

        def head(hh):
            qs, vs = slice(hh * DN_DK, (hh + 1) * DN_DK), slice(hh * DN_DV, (hh + 1) * DN_DV)
            q, k, v = a_ref[:, qs], a_ref[:, DN_QK_W + hh * DN_DK:DN_QK_W + (hh + 1) * DN_DK], \
                a_ref[:, 2 * DN_QK_W + hh * DN_DV:2 * DN_QK_W + (hh + 1) * DN_DV]
            gc = jnp.sum(jnp.where(head_lane == hh, g_ref[...], 0.0), axis=1, keepdims=True)
            bc = jnp.sum(jnp.where(head_lane == hh, b_ref[...], 0.0), axis=1, keepdims=True)
            t = _dn_chunk_terms(q, k, gc, bc)
            yield
            res = {}
            yield from _unit_lower_inverse_steps(t["lmat"], t["eye"], res)
            tms = _split(res["tm"])
            u = _x3dot(tms, v * bc)
            yield
            w = _x3dot(tms, t["kb"] * t["eg"])
            yield
            s = s_scr[hh]
            s_out[hh] = s
            t_out[hh] = res["tm"]
            sb = s.astype(BF16)
            vn = u - _dot(w.astype(BF16), sb)
            yield
            o = _dot(t["qd"].astype(BF16), sb) + _bdot(t["aqk"], vn)
            yield
            s_scr[hh] = s * jnp.exp(t["gl"]) + _bdot(t["kd"], vn, TN)
            vn_out[:, vs] = vn
            u_out[:, vs] = u
            w_out[:, qs] = w
            o_ref[:, vs] = o
            gate = pg_ref[:, vs]
            r = lax.rsqrt(jnp.mean(o * o, axis=-1, keepdims=True) + RMS_EPS)
            og_ref[:, vs] = (o * r * gn_ref[...] * (gate * _sigmoid(gate))).astype(BF16)

        _interleave([head(hh) for hh in range(H)] + [prepare(cb) for cb in range(DN_CONV_W // B)])

        @pl.when(pl.program_id(0) < N - 1)
        def _():
            tail_scr[...] = p_ref[C - HALO:C, :]

    def body(*refs):
        s = pl.program_id(0)
        io, (s_scr, tail_scr, buf_a, buf_b) = refs[:-4], refs[-4:]

        @pl.when(s == 0)
        def _():
            tail_scr[...] = jnp.zeros_like(tail_scr)
            buf_b[...] = jnp.zeros_like(buf_b)

        @pl.when(s <= 1)
        def _():
            s_scr[...] = jnp.zeros_like(s_scr)

        @pl.when(s % 2 == 0)
        def _():
            step(*io, s_scr, tail_scr, buf_b, buf_a)

        @pl.when(s % 2 == 1)
        def _():
            step(*io, s_scr, tail_scr, buf_a, buf_b)

    nxt = lambda w: pl.BlockSpec((C, w), lambda s: (jnp.minimum(s, N - 1), 0))
    cur = lambda w: pl.BlockSpec((C, w), lambda s: (jnp.maximum(s - 1, 0), 0))
    per_chunk = lambda a, b: pl.BlockSpec((H, None, a, b), lambda s: (0, jnp.maximum(s - 1, 0), 0, 0))
    return _call(
        body, comm, name=name, grid=(N + 1,),
        in_specs=[nxt(DN_CONV_W), pl.BlockSpec((DN_CONV, DN_CONV_W), lambda s: (0, 0)), cur(H), cur(H), cur(DN_V_W),
                  pl.BlockSpec((1, DN_DV), lambda s: (0, 0))],
        out_specs=[nxt(DN_CONV_W), cur(DN_V_W), cur(DN_V_W), per_chunk(DN_DK, DN_DV), per_chunk(C, C),
                   cur(DN_V_W), cur(DN_V_W), cur(DN_QK_W)],
        out_shape=[jax.ShapeDtypeStruct((T, DN_CONV_W), F32),
                   jax.ShapeDtypeStruct((T, DN_V_W), F32), jax.ShapeDtypeStruct((T, DN_V_W), BF16),
                   jax.ShapeDtypeStruct((H, N, DN_DK, DN_DV), F32),
                   jax.ShapeDtypeStruct((H, N, C, C), F32),
                   jax.ShapeDtypeStruct((T, DN_V_W), F32),
                   jax.ShapeDtypeStruct((T, DN_V_W), F32),
                   jax.ShapeDtypeStruct((T, DN_QK_W), F32)],
        scratch_shapes=[pltpu.VMEM((H, DN_DK, DN_DV), F32), pltpu.VMEM((HALO, DN_CONV_W), F32),
                        pltpu.VMEM((C, DN_CONV_W), F32), pltpu.VMEM((C, DN_CONV_W), F32)],
        semantics=("arbitrary",), args=(pqkv, conv_w, g, beta, pgate, gn))


def _dn_chunk_bwd(act, g, beta, s_saved, tm_saved, vn_saved, u_saved, w_saved, dog, o_raw, pgate, gn, name, comm=None):
    T = act.shape[0]
    C, H = DN_CHUNK, DN_HEADS
    N = T // C

    def body(a_ref, g_ref, b_ref, s_ref, t_ref, vn_ref, u_ref, w_ref, dog_ref, o_ref, pg_ref, gn_ref,
             da_ref, dg_ref, db_ref, dgate_ref, dgn_ref, ds_scr):
        @pl.when(pl.program_id(0) == 0)
        def _():
            ds_scr[...] = jnp.zeros_like(ds_scr)

        head_lane = _iota2((C, H), 1)
        dg_cols, db_cols, dgn_parts = {}, {}, {}

        def output_gate_bwd(hh, vs):
            d, o, gate, gn_v = dog_ref[:, vs], o_ref[:, vs], pg_ref[:, vs], gn_ref[...]
            sg, dsg = _silu_and_grad(gate)
            r = lax.rsqrt(jnp.mean(o * o, axis=-1, keepdims=True) + RMS_EPS)
            n = o * r
            dy = d * sg
            dgate_ref[:, vs] = (d * (n * gn_v) * dsg).astype(BF16)
            dn = dy * gn_v
            dgn_parts[hh] = jnp.sum(dy * n, axis=0, keepdims=True)
            return r * (dn - n * jnp.mean(dn * n, axis=-1, keepdims=True))

        def head(hh):
            qs, vs = slice(hh * DN_DK, (hh + 1) * DN_DK), slice(hh * DN_DV, (hh + 1) * DN_DV)
            ks = slice(DN_QK_W + hh * DN_DK, DN_QK_W + (hh + 1) * DN_DK)
            vas = slice(2 * DN_QK_W + hh * DN_DV, 2 * DN_QK_W + (hh + 1) * DN_DV)
            q, k, v = a_ref[:, qs], a_ref[:, ks], a_ref[:, vas]
            gc = jnp.sum(jnp.where(head_lane == hh, g_ref[...], 0.0), axis=1, keepdims=True)
            bc = jnp.sum(jnp.where(head_lane == hh, b_ref[...], 0.0), axis=1, keepdims=True)
            t = _dn_chunk_terms(q, k, gc, bc)
            yield
            lower, strict, eye = t["lower"], t["strict"], t["eye"]
            decay, eg, egl, kb, qd, kd = t["decay"], t["eg"], t["egl"], t["kb"], t["qd"], t["kd"]
            s, tm, vn, u, w = s_ref[hh], t_ref[hh], vn_ref[:, vs], u_ref[:, vs], w_ref[:, qs]
            d_o = output_gate_bwd(hh, vs)
            ds_next = ds_scr[hh]
            egl_tot = jnp.exp(t["gl"])
            dob, sb, dsb, vnb = d_o.astype(BF16), s.astype(BF16), ds_next.astype(BF16), vn.astype(BF16)

            dvn = _bdot(t["aqk"], dob, TN) + _bdot(kd, dsb)
            yield
            daqk = jnp.where(lower, _dot(dob, vnb, NT), 0.0)
            dqd = _dot(dob, sb, NT)
            dkd = _dot(vnb, dsb, NT)
            yield
            dvnb = dvn.astype(BF16)
            ds_scr[hh] = _bdot(qd, dob, TN) + egl_tot * ds_next - _bdot(w, dvnb, TN)
            dgl = egl_tot * jnp.sum(jnp.sum(s * ds_next, axis=1, keepdims=True), axis=0, keepdims=True)
            dw = -_dot(dvnb, sb, NT)
            yield
            tms = _split(tm)
            dru = _x3dot(tms, dvn, TN)
            drw = _x3dot(tms, dw, TN)
            yield
            dl = -jnp.where(strict, _x3dot(dru, u, NT) + _x3dot(drw, w, NT), 0.0)
            yield
            dkk = (dl * decay).astype(BF16)
            dqk = (daqk * decay).astype(BF16)
            dkb = _bdot(dkk, k) + drw * eg
            yield
            da_ref[:, ks] = _bdot(dkk, kb, TN) + _bdot(dqk, q, TN) + dkd * egl + dkb * bc
            da_ref[:, qs] = _bdot(dqk, k) + dqd * eg
            da_ref[:, vas] = dru * bc
            yield
            db_cols[hh] = jnp.sum(dru * v, axis=1, keepdims=True) + jnp.sum(dkb * k, axis=1, keepdims=True)
            pm = dl * t["lmat"] + daqk * t["aqk"]
            col_as_col = jnp.sum(jnp.where(eye, jnp.sum(pm, axis=0, keepdims=True), 0.0), axis=1, keepdims=True)
            kdsum = jnp.sum(dkd * kd, axis=1, keepdims=True)
            dgc = (jnp.sum(pm, axis=1, keepdims=True) - col_as_col + jnp.sum(dqd * qd, axis=1, keepdims=True)
                   - kdsum + jnp.sum(drw * (kb * eg), axis=1, keepdims=True))
            dgl = dgl + jnp.sum(kdsum, axis=0, keepdims=True)
            dg_cols[hh] = dgc + jnp.where(t["last"], dgl, 0.0)

        _interleave([head(hh) for hh in range(H)])
        dg_ref[...] = sum(jnp.where(head_lane == hh, dg_cols[hh], 0.0) for hh in range(H))
        db_ref[...] = sum(jnp.where(head_lane == hh, db_cols[hh], 0.0) for hh in range(H))
        dgn_part = sum(dgn_parts[hh] for hh in range(H))

        @pl.when(pl.program_id(0) == 0)
        def _():
            dgn_ref[...] = dgn_part

        @pl.when(pl.program_id(0) > 0)
        def _():
            dgn_ref[...] += dgn_part

    row = lambda w: pl.BlockSpec((C, w), lambda n: (N - 1 - n, 0))
    vec = pl.BlockSpec((1, DN_DV), lambda n: (0, 0))
    return _call(
        body, comm, name=name, grid=(N,),
        in_specs=[row(DN_CONV_W), row(H), row(H),
                  pl.BlockSpec((H, None, DN_DK, DN_DV), lambda n: (0, N - 1 - n, 0, 0)),
                  pl.BlockSpec((H, None, C, C), lambda n: (0, N - 1 - n, 0, 0)),
                  row(DN_V_W), row(DN_V_W), row(DN_QK_W), row(DN_V_W), row(DN_V_W), row(DN_V_W), vec],
        out_specs=[row(DN_CONV_W), row(H), row(H), row(DN_V_W), vec],
        out_shape=[jax.ShapeDtypeStruct((T, DN_CONV_W), F32),
                   jax.ShapeDtypeStruct((T, H), F32), jax.ShapeDtypeStruct((T, H), F32),
                   jax.ShapeDtypeStruct((T, DN_V_W), BF16), jax.ShapeDtypeStruct((1, DN_DV), F32)],
        scratch_shapes=[pltpu.VMEM((H, DN_DK, DN_DV), F32)], semantics=("arbitrary",),
        args=(act, g, beta, s_saved, tm_saved, vn_saved, u_saved, w_saved, dog, o_raw, pgate, gn))


def _dn_split_w_in(w):
    return w, jnp.pad(w[:, DN_CONV_W + DN_V_W:], ((0, 0), (0, DN_AB_PAD - 2 * DN_HEADS)))


def _out_proj(og, w_out, x_res, next_g, name):
    if next_g is None:
        return _matmul(og, w_out, "nn", name, add=x_res), None
    return tuple(_matmul(og, w_out, "nn", name, add=x_res, norm_fwd=next_g, tm=NORM_FUSED_TM))


def _dn_layer_fwd(h, wts, conv_w, a_log, dt_bias, gn, w_out, x_res, tag, comm=None, next_g=None):
    w_in, wab = wts
    H = DN_HEADS
    pqkv = _matmul(h, w_in, "nn", tag + "_pqkv", b_cols=(0, DN_CONV_W))
    pgate = _matmul(h, w_in, "nn", tag + "_pgate", b_cols=(DN_CONV_W, DN_V_W))
    pab = _matmul(h, wab, "nn", tag + "_pab")
    a_in, b_in = pab[:, :H], pab[:, H:2 * H]
    g, beta = _dn_gates(a_in, b_in, a_log, dt_bias, tag + "_gates")
    (act, o_raw, og, s_sv, tm_sv, vn_sv, u_sv, w_sv), landed = _dn_chunk_fwd(pqkv, conv_w, g, beta, pgate, gn,
                                                                             tag + "_chunk_fwd", comm)
    if callable(w_out):
        w_out = w_out(landed)
    y = _out_proj(og, w_out, x_res, next_g, tag + "_out")
    saved = dict(h=h, wts=wts, conv_w=conv_w, a_log=a_log, dt_bias=dt_bias, gn=gn, w_out=w_out, pqkv=pqkv, pgate=pgate,
                 a_in=a_in, b_in=b_in, g=g, beta=beta, act=act, o_raw=o_raw, chunk=(s_sv, tm_sv, vn_sv, u_sv, w_sv), og=og)
    return y, saved, landed


def _dn_layer_bwd(dout, sv, tag, norm, comm_of=None, late_comm_of=None):
    w_in, wab = sv["wts"]
    h = sv["h"]
    dog = _matmul(dout, sv["w_out"], "nt", tag + "_dog")
    dw_out = _matmul(sv["og"], dout, "tn", tag + "_dwout", out_dtype=BF16)
    comm = comm_of(dw_out) if comm_of is not None else None
    (dact, dg, dbeta, dgate, dgn), landed = _dn_chunk_bwd(sv["act"], sv["g"], sv["beta"], *sv["chunk"], dog, sv["o_raw"],
                                                          sv["pgate"], sv["gn"], tag + "_chunk_bwd", comm)
    da_in, db_in, da_log, ddt = _dn_gates_bwd(dg, dbeta, sv["a_in"], sv["b_in"], sv["a_log"], sv["dt_bias"],
                                              tag + "_gates_bwd")
    dpqkv, dconv = _dn_prep_bwd(sv["pqkv"], sv["conv_w"], dact, tag + "_prep_bwd")
    dpab = jnp.pad(jnp.concatenate([da_in, db_in], axis=1), ((0, 0), (0, DN_AB_PAD - 2 * DN_HEADS)))
    dwqkv = _matmul(h, dpqkv, "tn", tag + "_dwqkv", out_dtype=BF16)
    dwgate = _matmul(h, dgate, "tn", tag + "_dwgate", out_dtype=BF16)
    dwab = _matmul(h, dpab, "tn", tag + "_dwab", out_dtype=BF16)
    dw_in = jnp.concatenate([dwqkv, dwgate, dwab[:, :2 * DN_HEADS]], axis=1)
    grads = dict(dn_w_in=dw_in, dn_conv_w=dconv, dn_a_log=da_log, dn_dt_bias=ddt, dn_o_norm_g=dgn, dn_w_out=dw_out)
    dx, landed_late = _matmul_nt_sum([(dpqkv, w_in, 0), (dgate, w_in, DN_CONV_W), (dpab, wab, 0)], tag + "_dh",
                                     late_comm_of(grads) if late_comm_of is not None else None, norm_bwd=norm,
                                     tm=NORM_FUSED_TM if norm is not None else 1024)
    return dx, grads, landed, landed_late


def _sb_layer_fwd(h, w_in, qg, kg, w_out, x_res, tag, comm=None, next_g=None):
    qg2, kg2 = jnp.tile(qg, (1, 2)), jnp.tile(kg, (1, 2))
    proj = _matmul(h, w_in, "nn", tag + "_proj", blocked_b=True)
    qn, kn, vb = _sb_prep(proj, qg2, kg2, tag + "_prep")
    (o, og, ltot, done), landed = _sb_attn_fwd(qn, kn, vb, proj, tag + "_attn_fwd", comm)
    y = _out_proj(og, w_out, x_res, next_g, tag + "_out")
    saved = dict(h=h, w_in=w_in, qg2=qg2, kg2=kg2, w_out=w_out, proj=proj, qn=qn, kn=kn, vb=vb, o=o, og=og, ltot=ltot,
                 done=done)
    return y, saved, landed


def _sb_layer_bwd(dout, sv, tag, comm=None):
    dog = _matmul(dout, sv["w_out"], "nt", tag + "_dog")
    dw_out = _matmul(sv["og"], dout, "tn", tag + "_dwout", out_dtype=BF16)
    (dqn, dkn, dv, dgate), landed = _sb_attn_bwd(sv["qn"], sv["kn"], sv["vb"], dog, sv["o"], sv["ltot"], sv["done"],
                                                 sv["proj"], tag + "_attn_bwd", comm)
    dproj, dqgp, dkgp = _sb_prep_bwd(sv["proj"], dqn, dkn, dv, dgate, sv["qg2"], sv["kg2"], tag + "_prep_bwd")
    dw_in = _matmul(sv["h"], dproj, "tn", tag + "_dwin", out_dtype=BF16, blocked_out=N_DEV)
    dh = _matmul(dproj, sv["w_in"], "nt", tag + "_dh", blocked_b=True)
    dqg = _fold_heads(dqgp, tag + "_dqg")
    dkg = _fold_heads(dkgp, tag + "_dkg")
    return dh, dict(sb_w_in=dw_in, sb_q_norm_g=dqg, sb_k_norm_g=dkg, sb_w_out=dw_out), landed


def _sc_layer_fwd(h, w_in, conv_w, w_out, x_res, tag, next_g=None):
    proj = _matmul(h, w_in, "nn", tag + "_proj", blocked_b=True)
    yg = _sc_fwd(proj, conv_w, tag + "_fwd")
    y = _out_proj(yg, w_out, x_res, next_g, tag + "_out")
    return y, dict(h=h, w_in=w_in, conv_w=conv_w, w_out=w_out, proj=proj, yg=yg)


def _sc_layer_bwd(dout, sv, tag):
    dyg = _matmul(dout, sv["w_out"], "nt", tag + "_dyg")
    dw_out = _matmul(sv["yg"], dout, "tn", tag + "_dwout", out_dtype=BF16)
    dproj, dconv = _sc_bwd(dyg, sv["proj"], sv["conv_w"], tag + "_bwd")
    dw_in = _matmul(sv["h"], dproj, "tn", tag + "_dwin", out_dtype=BF16, blocked_out=N_DEV)
    dh = _matmul(dproj, sv["w_in"], "nt", tag + "_dh", blocked_b=True)
    return dh, dict(sc_w_in=dw_in, sc_conv_w=dconv, sc_w_out=dw_out)


def _adamw(w, m, v, parts, name):
    R, C = w.shape
    tr = _tile(R, 128, SUBLANE)

    def body(w_ref, m_ref, v_ref, p_ref, g_ref, d_ref, nm_ref, nv_ref):
        g = p_ref[0].astype(F32)
        for s in range(1, N_DEV):
            g = g + p_ref[s].astype(F32)
        m2 = ADAM_B1 * m_ref[...] + (1.0 - ADAM_B1) * g
        v2 = ADAM_B2 * v_ref[...] + (1.0 - ADAM_B2) * (g * g)
        m_hat = m2 / (1.0 - ADAM_B1 ** ADAM_STEP)
        v_hat = v2 / (1.0 - ADAM_B2 ** ADAM_STEP)
        g_ref[...] = g
        d_ref[...] = -ADAM_LR * (m_hat / (jnp.sqrt(v_hat) + ADAM_EPS) + ADAM_WD * w_ref[...])
        nm_ref[...] = m2
        nv_ref[...] = v2

    blk = pl.BlockSpec((tr, C), lambda i: (i, 0))
    return pl.pallas_call(
        body, name=name, grid=(R // tr,),
        in_specs=[blk, blk, blk, pl.BlockSpec((N_DEV, tr, C), lambda i: (0, i, 0))],
        out_specs=[blk] * 4, out_shape=[jax.ShapeDtypeStruct((R, C), F32)] * 4,
        compiler_params=_params("parallel"),
    )(w, m, v, parts)


_HBM = pl.BlockSpec(memory_space=pltpu.HBM)
_MESH = pl.DeviceIdType.MESH


def _slot(x, y, c):
    return 4 * x + 2 * y + c


class _Gather:
    def __init__(self, shards):
        self.arrays = list(shards)
        n = len(self.arrays)
        self.out_shapes = [jax.ShapeDtypeStruct((N_DEV,) + s.shape, s.dtype) for s in self.arrays]
        self.scratch = [pltpu.SemaphoreType.DMA((n, N_DEV - 1)), pltpu.SemaphoreType.DMA((n, N_DEV - 1)),
                        pltpu.SemaphoreType.DMA((n,))]

    def _parts(self, ins, outs, sems):
        send_sems, recv_sems, local_sems = sems
        n = len(self.arrays)
        x, y, c = lax.axis_index("x"), lax.axis_index("y"), lax.axis_index("c")
        me, sibling = (x, y, c), (x, y, 1 - c)
        chips = [(1 - x, y), (x, 1 - y), (1 - x, 1 - y)]

        def copy(a, k, block, to, src=None):
            dst = outs[a].at[_slot(*block)]
            return pltpu.make_async_remote_copy(src_ref=dst if src is None else src, dst_ref=dst,
                                                send_sem=send_sems.at[a, k], recv_sem=recv_sems.at[a, k],
                                                device_id=to, device_id_type=_MESH)

        mine = [pltpu.make_async_copy(ins[a], outs[a].at[_slot(*me)], local_sems.at[a]) for a in range(n)]
        first = []
        for a in range(n):
            first.append(copy(a, 0, me, sibling, src=ins[a]))
            first += [copy(a, 1 + j, me, (*chip, c), src=ins[a]) for j, chip in enumerate(chips)]
        return n, c, me, sibling, chips, copy, mine, first

    def start(self, ins, outs, sems):
        _, _, _, _, _, _, mine, first = self._parts(ins, outs, sems)
        for cp in mine + first:
            cp.start()

    def finish(self, ins, outs, sems):
        n, c, me, sibling, chips, copy, mine, first = self._parts(ins, outs, sems)
        passed = []
        for j, chip in enumerate(chips):
            for a in range(n):
                copy(a, 1 + j, (*chip, c), me).wait_recv()
                fwd = copy(a, 4 + j, (*chip, c), sibling)
                fwd.start()
                passed.append(fwd)
        for a in range(n):
            copy(a, 0, sibling, me).wait_recv()
            for j, chip in enumerate(chips):
                copy(a, 4 + j, (*chip, 1 - c), me).wait_recv()
        for cp in first + passed:
            cp.wait_send()
        for cp in mine:
            cp.wait()


class _Exchange:
    def __init__(self, arrays, scatter):
        self.arrays, self.scatter = list(arrays), list(scatter)
        n = len(self.arrays)
        shapes = [a.shape[1:] if s else a.shape for a, s in zip(self.arrays, self.scatter)]
        self.out_shapes = [jax.ShapeDtypeStruct((N_DEV,) + tuple(s), a.dtype) for s, a in zip(shapes, self.arrays)]
        self.scratch = [pltpu.SemaphoreType.DMA((n, N_DEV - 1)), pltpu.SemaphoreType.DMA((n, N_DEV - 1)),
                        pltpu.SemaphoreType.DMA((n,))]

    def _copies(self, ins, outs, sems):
        send_sems, recv_sems, local_sems = sems
        n, scatter = len(self.arrays), self.scatter
        x, y, c = lax.axis_index("x"), lax.axis_index("y"), lax.axis_index("c")
        me = _slot(x, y, c)
        copies = [pltpu.make_async_copy(ins[a].at[me] if scatter[a] else ins[a], outs[a].at[me], local_sems.at[a])
                  for a in range(n)]
        for r in range(1, N_DEV):
            px = 1 - x if r & 4 else x
            py = 1 - y if r & 2 else y
            pc = 1 - c if r & 1 else c
            for a in range(n):
                copies.append(pltpu.make_async_remote_copy(
                    src_ref=ins[a].at[_slot(px, py, pc)] if scatter[a] else ins[a], dst_ref=outs[a].at[me],
                    send_sem=send_sems.at[a, r - 1], recv_sem=recv_sems.at[a, r - 1],
                    device_id=(px, py, pc), device_id_type=_MESH))
        return copies

    def start(self, ins, outs, sems):
        for cp in self._copies(ins, outs, sems):
            cp.start()

    def finish(self, ins, outs, sems):
        for cp in self._copies(ins, outs, sems):
            cp.wait()


def _comm_call(comm, name):
    n = len(comm.arrays)

    def body(*refs):
        ins, outs, sems = refs[:n], refs[n:2 * n], refs[2 * n:]
        comm.start(ins, outs, sems)
        comm.finish(ins, outs, sems)

    return pl.pallas_call(body, name=name, in_specs=[_HBM] * n, out_specs=[_HBM] * n, out_shape=comm.out_shapes,
                          scratch_shapes=comm.scratch)(*comm.arrays)


def _call(body, comm, *, name, grid, in_specs, out_specs, out_shape, scratch_shapes, semantics, args):
    if comm is None:
        outs = pl.pallas_call(body, name=name, grid=grid, in_specs=in_specs, out_specs=out_specs, out_shape=out_shape,
                              scratch_shapes=scratch_shapes, compiler_params=_params(*semantics))(*args)
        return outs, []
    n_in, n_out, n_scr, n_c = len(in_specs), len(out_specs), len(scratch_shapes), len(comm.arrays)

    def fused(*refs):
        ins, refs = refs[:n_in], refs[n_in:]
        c_ins, refs = refs[:n_c], refs[n_c:]
        outs, refs = refs[:n_out], refs[n_out:]
        c_outs, refs = refs[:n_c], refs[n_c:]
        scr, sems = refs[:n_scr], refs[n_scr:]
        ids = [pl.program_id(d) for d in range(len(grid))]
        first = functools.reduce(jnp.logical_and, [i == 0 for i in ids])
        last = functools.reduce(jnp.logical_and, [i == g - 1 for i, g in zip(ids, grid)])

        @pl.when(first)
        def _():
            comm.start(c_ins, c_outs, sems)

        body(*ins, *outs, *scr)

        @pl.when(last)
        def _():
            comm.finish(c_ins, c_outs, sems)

    outs = pl.pallas_call(
        fused, name=name, grid=grid, in_specs=list(in_specs) + [_HBM] * n_c, out_specs=list(out_specs) + [_HBM] * n_c,
        out_shape=list(out_shape) + comm.out_shapes, scratch_shapes=list(scratch_shapes) + comm.scratch,
        compiler_params=_params(*["arbitrary"] * len(grid)))(*args, *comm.arrays)
    return outs[:n_out], outs[n_out:]


_GATHER_0 = (("dn_w_in", 0), ("dn_conv_w", 0), ("dn_o_norm_g", 0))
_GATHER_1 = (("dn_w_out", 0), ("sb_w_in", 0), ("sb_w_out", 0))
_GATHER_2 = (("sc_w_in", 0), ("sc_conv_w", 0), ("sc_w_out", 0), ("dn_w_in", 1), ("dn_conv_w", 1), ("dn_o_norm_g", 1),
             ("dn_w_out", 1))
_EXCHANGE_A = _GATHER_2
_EXCHANGE_B = (("sb_w_in", 0), ("sb_w_out", 0), ("dn_w_out", 0))
_EXCHANGE_C = _GATHER_0
_MATMUL_WEIGHTS = ("dn_w_in", "dn_w_out", "sb_w_in", "sb_w_out", "sc_w_in", "sc_w_out")
_COLUMN_SHARDED = ("dn_w_in", "dn_conv_w", "dn_o_norm_g", "sb_w_in", "sc_w_in", "sc_conv_w")
_BLOCKED = ("sb_w_in", "sc_w_in")
_REPLICATED = ("norm_g", "dn_a_log", "dn_dt_bias", "sb_q_norm_g", "sb_k_norm_g")
_ORDER = ("norm_g", "dn_w_in", "dn_conv_w", "dn_a_log", "dn_dt_bias", "dn_o_norm_g", "dn_w_out", "sb_w_in", "sb_q_norm_g",
          "sb_k_norm_g", "sb_w_out", "sc_w_in", "sc_conv_w", "sc_w_out")
_PACK_COLS = D_MODEL


def _as_2d(a):
    return a.reshape(1, -1) if a.ndim == 1 else a


def _assemble(name, gathered):
    n, r, c = gathered.shape
    if name in _COLUMN_SHARDED:
        return jnp.moveaxis(gathered, 0, 1).reshape(r, n * c)
    return gathered.reshape(n * r, c)


def _disassemble(name, full):
    r, c = full.shape
    if name in _COLUMN_SHARDED:
        return jnp.moveaxis(full.reshape(r, N_DEV, c // N_DEV), 1, 0)
    return full.reshape(N_DEV, r // N_DEV, c)


def _pack_replicated(d):
    rows = [d["norm_g"]]
    for name in _REPLICATED[1:]:
        flat = d[name].reshape(1, -1)
        rows.append(jnp.pad(flat, ((0, 0), (0, _PACK_COLS - flat.shape[1]))))
    return jnp.concatenate(rows, axis=0)


def _unpack_replicated(p, like):
    out = {"norm_g": p[:4]}
    for r, name in enumerate(_REPLICATED[1:]):
        shape = like[name].shape
        out[name] = p[4 + r, :math.prod(shape)].reshape(shape)
    return out


def kernel(x, norm_g, dn_w_in, dn_conv_w, dn_a_log, dn_dt_bias, dn_o_norm_g, dn_w_out, sb_w_in, sb_q_norm_g, sb_k_norm_g, sb_w_out, sc_w_in, sc_conv_w, sc_w_out, loss_target, m_norm_g, m_dn_w_in, m_dn_conv_w, m_dn_a_log, m_dn_dt_bias, m_dn_o_norm_g, m_dn_w_out, m_sb_w_in, m_sb_q_norm_g, m_sb_k_norm_g, m_sb_w_out, m_sc_w_in, m_sc_conv_w, m_sc_w_out, v_norm_g, v_dn_w_in, v_dn_conv_w, v_dn_a_log, v_dn_dt_bias, v_dn_o_norm_g, v_dn_w_out, v_sb_w_in, v_sb_q_norm_g, v_sb_k_norm_g, v_sb_w_out, v_sc_w_in, v_sc_conv_w, v_sc_w_out):
    w = dict(norm_g=norm_g, dn_w_in=dn_w_in, dn_conv_w=dn_conv_w, dn_a_log=dn_a_log, dn_dt_bias=dn_dt_bias,
             dn_o_norm_g=dn_o_norm_g, dn_w_out=dn_w_out, sb_w_in=sb_w_in, sb_q_norm_g=sb_q_norm_g, sb_k_norm_g=sb_k_norm_g,
             sb_w_out=sb_w_out, sc_w_in=sc_w_in, sc_conv_w=sc_conv_w, sc_w_out=sc_w_out)
    m = dict(norm_g=m_norm_g, dn_w_in=m_dn_w_in, dn_conv_w=m_dn_conv_w, dn_a_log=m_dn_a_log, dn_dt_bias=m_dn_dt_bias,
             dn_o_norm_g=m_dn_o_norm_g, dn_w_out=m_dn_w_out, sb_w_in=m_sb_w_in, sb_q_norm_g=m_sb_q_norm_g,
             sb_k_norm_g=m_sb_k_norm_g, sb_w_out=m_sb_w_out, sc_w_in=m_sc_w_in, sc_conv_w=m_sc_conv_w, sc_w_out=m_sc_w_out)
    v = dict(norm_g=v_norm_g, dn_w_in=v_dn_w_in, dn_conv_w=v_dn_conv_w, dn_a_log=v_dn_a_log, dn_dt_bias=v_dn_dt_bias,
             dn_o_norm_g=v_dn_o_norm_g, dn_w_out=v_dn_w_out, sb_w_in=v_sb_w_in, sb_q_norm_g=v_sb_q_norm_g,
             sb_k_norm_g=v_sb_k_norm_g, sb_w_out=v_sb_w_out, sc_w_in=v_sc_w_in, sc_conv_w=v_sc_conv_w, sc_w_out=v_sc_w_out)

    def gather_of(keys):
        return _Gather([_as_2d(w[k][j]).astype(BF16) if k in _MATMUL_WEIGHTS else _as_2d(w[k][j]) for k, j in keys])

    def full_weights(keys, gathered):
        return {key: g if key[0] in _BLOCKED else _assemble(key[0], g) for key, g in zip(keys, gathered)}

    def exchange_of(keys, grads, extra=()):
        out = [grads[k, j] if k in _BLOCKED else
               _disassemble(k, grads[k, j].astype(BF16) if k in _MATMUL_WEIGHTS else grads[k, j]) for k, j in keys]
        return _Exchange(out + list(extra), [True] * len(out) + [False] * len(extra))

    xs, saves = [x[0]], []
    h, got = _rmsnorm_fwd(xs[0], norm_g[0:1], "norm0", gather_of(_GATHER_0))
    F = full_weights(_GATHER_0, got)

    def w_out_0(got):
        F.update(full_weights(_GATHER_1, got))
        return F["dn_w_out", 0]

    (y, h), sv, _ = _dn_layer_fwd(h, _dn_split_w_in(F["dn_w_in", 0]), F["dn_conv_w", 0], dn_a_log[0:1], dn_dt_bias[0:1],
                                  F["dn_o_norm_g", 0], w_out_0, xs[0], "dn0", gather_of(_GATHER_1), norm_g[1:2])
    xs.append(y)
    saves.append(sv)
    (y, h), sv, got = _sb_layer_fwd(h, F["sb_w_in", 0], sb_q_norm_g, sb_k_norm_g, F["sb_w_out", 0], xs[1], "sb",
                                    gather_of(_GATHER_2), norm_g[2:3])
    F.update(full_weights(_GATHER_2, got))
    xs.append(y)
    saves.append(sv)
    (y, h), sv = _sc_layer_fwd(h, F["sc_w_in", 0], F["sc_conv_w", 0], F["sc_w_out", 0], xs[2], "sc", norm_g[3:4])
    xs.append(y)
    saves.append(sv)
    (y, _), sv, _ = _dn_layer_fwd(h, _dn_split_w_in(F["dn_w_in", 1]), F["dn_conv_w", 1], dn_a_log[1:2], dn_dt_bias[1:2],
                                  F["dn_o_norm_g", 1], F["dn_w_out", 1], xs[3], "dn1")
    xs.append(y)
    saves.append(sv)
    dx, loss_part = _loss_head(xs[4], loss_target[0])

    G, dnorm, landed = {}, [None] * 4, {}

    def keep(grads, j):
        G.update({(k, j): g for k, g in grads.items()})

    dh, grads, _, _ = _dn_layer_bwd(dx, saves[3], "dn1", None)
    keep(grads, 1)
    dx, dnorm[3] = _rmsnorm_bwd(dh, xs[3], norm_g[3:4], dx, "norm3_bwd")
    dh, grads = _sc_layer_bwd(dx, saves[2], "sc")
    keep(grads, 0)
    dx, dnorm[2] = _rmsnorm_bwd(dh, xs[2], norm_g[2:3], dx, "norm2_bwd")
    dh, grads, got = _sb_layer_bwd(dx, saves[1], "sb", exchange_of(_EXCHANGE_A, G))
    keep(grads, 0)
    landed.update(zip(_EXCHANGE_A, got))
    dx, dnorm[1] = _rmsnorm_bwd(dh, xs[1], norm_g[1:2], dx, "norm1_bwd")

    def exchange_b(dw_out):
        G["dn_w_out", 0] = dw_out
        return exchange_of(_EXCHANGE_B, G)

    def exchange_c(grads):
        keep(grads, 0)
        return exchange_of(_EXCHANGE_C, G)

    (dx, dnorm[0]), grads, got, got_late = _dn_layer_bwd(dx, saves[0], "dn0", (xs[0], norm_g[0:1], dx), exchange_b, exchange_c)
    landed.update(zip(_EXCHANGE_B, got))
    landed.update(zip(_EXCHANGE_C, got_late))
    replicated = dict(norm_g=jnp.concatenate(dnorm, axis=0),
                      dn_a_log=jnp.concatenate([G["dn_a_log", 0], G["dn_a_log", 1]], axis=0),
                      dn_dt_bias=jnp.concatenate([G["dn_dt_bias", 0], G["dn_dt_bias", 1]], axis=0),
                      sb_q_norm_g=G["sb_q_norm_g", 0], sb_k_norm_g=G["sb_k_norm_g", 0])
    got = _comm_call(_Exchange([_pack_replicated(replicated)], [False]), "exchange_replicated")

    res = {}
    for k in _ORDER:
        if k in _REPLICATED:
            continue
        per_layer = []
        for j in range(w[k].shape[0]):
            shape = w[k][j].shape
            outs = _adamw(_as_2d(w[k][j]), _as_2d(m[k][j]), _as_2d(v[k][j]), landed[k, j], f"adamw_{k}{j}")
            per_layer.append([o.reshape(shape) for o in outs])
        res[k] = [jnp.stack([layer[i] for layer in per_layer], axis=0) for i in range(4)]
    outs = _adamw(_pack_replicated(w), _pack_replicated(m), _pack_replicated(v), got[-1], "adamw_replicated")
    unpacked = [_unpack_replicated(o, w) for o in outs]
    for k in _REPLICATED:
        res[k] = [u[k] for u in unpacked]

    loss = lax.psum(loss_part[0, 0], ("x", "y", "c"))
    return (loss, dx[None]) + tuple(res[k][0] for k in _ORDER) + tuple(res[k][1] for k in _ORDER) \
        + tuple(res[k][2] for k in _ORDER) + tuple(res[k][3] for k in _ORDER)
```

```python
import functools
import itertools
import math

import jax
import jax.numpy as jnp
from jax import lax
from jax.experimental import pallas as pl
from jax.experimental.pallas import tpu as pltpu

F32 = jnp.float32
BF16 = jnp.bfloat16
HIGHEST = lax.Precision.HIGHEST

N_DEV = 8
D_MODEL = 1024
RMS_EPS = 1e-6
L2_EPS = 1e-6

DN_HEADS = 8
DN_DK = 128
DN_DV = 256
DN_QK_W = DN_HEADS * DN_DK
DN_V_W = DN_HEADS * DN_DV
DN_CONV = 4
DN_CHUNK = 64
DN_CONV_W = 2 * DN_QK_W + DN_V_W
DN_IN = DN_CONV_W + DN_V_W + 2 * DN_HEADS
DN_AB_PAD = 128
DN_PREP_BLK = 512

SB_HEADS = 16
SB_DH = 64
SB_W = SB_HEADS * SB_DH
SB_PAIRS = SB_HEADS // 2
SB_TQ = 256
SB_TK = 128
SB_DEAD = -106.0

SC_W = 2 * D_MODEL
SC_CONV = 3
SC_BLK = 512
SC_NBLK = SC_W // SC_BLK

ADAM_LR = 0.001
ADAM_B1 = 0.9
ADAM_B2 = 0.999
ADAM_EPS = 1e-08
ADAM_WD = 0.01
ADAM_STEP = 10

LANE = 128
SUBLANE = 8
HALO = SUBLANE
LONG_ROW_TILE = 512
NORM_FUSED_TM = 512
DEEP_TK = 2048
WIDE_TN = 2048
WIDE_ROW_TILE = 128
VMEM_LIMIT = 48 * 2 ** 20

NN = ((1,), (0,))
NT = ((1,), (1,))
TN = ((0,), (0,))


def _dot(a, b, dims=NN, precision=None):
    return lax.dot_general(a, b, (dims, ((), ())), precision=precision, preferred_element_type=F32)


def _bdot(a, b, dims=NN):
    return _dot(a.astype(BF16), b.astype(BF16), dims)


def _hdot(a, b, dims=NN):
    return _dot(a, b, dims, precision=HIGHEST)


def _tile(dim, pref, align=LANE):
    t = (min(pref, dim) // align) * align
    while t >= align:
        if dim % t == 0:
            return t
        t -= align
    return dim


def _params(*sem):
    return pltpu.CompilerParams(dimension_semantics=sem, vmem_limit_bytes=VMEM_LIMIT)


def _sigmoid(x):
    return 0.5 * jnp.tanh(0.5 * x) + 0.5


def _softplus(x):
    return jnp.maximum(x, 0.0) + jnp.log(1.0 + jnp.exp(-jnp.abs(x)))


def _silu_and_grad(x):
    s = _sigmoid(x)
    return x * s, s * (1.0 + x * (1.0 - s))


def _iota2(shape, dim):
    return lax.broadcasted_iota(jnp.int32, shape, dim)


def _matmul(a, b, mode, name, out_dtype=F32, add=None, b_cols=None, blocked_b=False, blocked_out=0,
            norm_fwd=None, norm_bwd=None, tm=1024, tn=1024, tk=1024):
    b_rows, b_width = (b.shape[1], b.shape[0] * b.shape[2]) if blocked_b else b.shape
    c0, b_used = b_cols if b_cols is not None else (0, b_width)
    if mode == "nn":
        (M, K), (K2, N) = a.shape, (b_rows, b_used)
    elif mode == "nt":
        (M, K), (N, K2) = a.shape, (b_rows, b_used)
    else:
        (K, M), (K2, N) = a.shape, (b_rows, b_used)
    assert K == K2, (a.shape, b.shape, mode)
    if mode == "tn":
        tk = max(tk, DEEP_TK)
    elif norm_fwd is None and norm_bwd is None and add is None:
        tn = max(tn, WIDE_TN)
    tm, tn, tk = _tile(M, tm), _tile(N, tn), _tile(K, tk)
    if blocked_b and mode == "nt":
        tk = b.shape[2]
    elif blocked_b:
        tn = b.shape[2]
    if blocked_out:
        tn = N // blocked_out
    nk = K // tk
    dims = {"nn": NN, "nt": NT, "tn": TN}[mode]
    a_spec = pl.BlockSpec((tk, tm), lambda i, j, k: (k, i)) if mode == "tn" else pl.BlockSpec((tm, tk), lambda i, j, k: (i, k))
    if mode == "nt":
        cb0 = c0 // tk
        assert c0 % tk == 0
        b_spec = (pl.BlockSpec((None, tn, tk), lambda i, j, k: (k + cb0, j, 0)) if blocked_b
                  else pl.BlockSpec((tn, tk), lambda i, j, k: (j, k + cb0)))
    else:
        cb0 = c0 // tn
        assert c0 % tn == 0
        b_spec = (pl.BlockSpec((None, tk, tn), lambda i, j, k: (j + cb0, k, 0)) if blocked_b
                  else pl.BlockSpec((tk, tn), lambda i, j, k: (k, j + cb0)))
    o_spec = pl.BlockSpec((tm, tn), lambda i, j, k: (i, j))
    out_spec = pl.BlockSpec((None, tm, tn), lambda i, j, k: (j, i, 0)) if blocked_out else o_spec
    out_shape = (blocked_out, M, tn) if blocked_out else (M, N)
    has_add = add is not None
    vec_spec = pl.BlockSpec((1, tn), lambda i, j, k: (0, j))
    assert not (norm_fwd is not None or norm_bwd is not None) or tn == N
    extra_in, extra_specs = [], []
    if has_add:
        extra_in, extra_specs = [add], [o_spec]
    if norm_fwd is not None:
        extra_in, extra_specs = extra_in + [norm_fwd], extra_specs + [vec_spec]
        out_specs = [o_spec, o_spec]
        out_shapes = [jax.ShapeDtypeStruct((M, N), out_dtype), jax.ShapeDtypeStruct((M, N), BF16)]
    elif norm_bwd is not None:
        extra_in, extra_specs = extra_in + list(norm_bwd), extra_specs + [o_spec, vec_spec, o_spec]
        out_specs = [o_spec, vec_spec]
        out_shapes = [jax.ShapeDtypeStruct((M, N), F32), jax.ShapeDtypeStruct((1, N), F32)]
    else:
        out_specs, out_shapes = out_spec, jax.ShapeDtypeStruct(out_shape, out_dtype)

    def body(*refs):
        a_ref, b_ref = refs[0], refs[1]
        extra = list(refs[2:2 + len(extra_in)])
        outs = refs[2 + len(extra_in):]
        add_ref = extra.pop(0) if has_add else None
        p = _bdot(a_ref[...], b_ref[...], dims)

        def finish(acc):
            if has_add:
                acc = acc + add_ref[...]
            if norm_bwd is not None:
                _rmsnorm_bwd_tile(acc, *extra, outs[0], outs[1], first=pl.program_id(0) == 0)
                return
            outs[0][...] = acc.astype(out_dtype)
            if norm_fwd is not None:
                r = lax.rsqrt(jnp.mean(acc * acc, axis=-1, keepdims=True) + RMS_EPS)
                outs[1][...] = (acc * r * extra[0][...]).astype(BF16)

        if nk == 1:
            finish(p)
        else:
            acc_ref = refs[-1]
            k = pl.program_id(2)

            @pl.when(k == 0)
            def _():
                acc_ref[...] = p

            @pl.when(k > 0)
            def _():
                acc_ref[...] += p

            @pl.when(k == nk - 1)
            def _():
                finish(acc_ref[...])

    return pl.pallas_call(
        body, name=name, grid=(M // tm, N // tn, nk),
        in_specs=[a_spec, b_spec] + extra_specs, out_specs=out_specs, out_shape=out_shapes,
        scratch_shapes=[pltpu.VMEM((tm, tn), F32)] if nk > 1 else [],
        compiler_params=(_params("arbitrary", "arbitrary", "arbitrary") if norm_bwd is not None
                         else _params("parallel", "parallel", "arbitrary")),
    )(a, b, *extra_in)


def _rmsnorm_bwd_tile(dh, x_ref, g_ref, res_ref, dx_ref, dg_ref, first):
    xv = x_ref[...]
    r = lax.rsqrt(jnp.mean(xv * xv, axis=-1, keepdims=True) + RMS_EPS)
    xh = xv * r
    dxh = dh * g_ref[...]
    m = jnp.mean(dxh * xh, axis=-1, keepdims=True)
    dx_ref[...] = res_ref[...] + r * (dxh - xh * m)
    part = jnp.sum(dh * xh, axis=0, keepdims=True)

    @pl.when(first)
    def _():
        dg_ref[...] = part

    @pl.when(jnp.logical_not(first))
    def _():
        dg_ref[...] += part


def _matmul_nt_sum(pairs, name, comm=None, norm_bwd=None, tm=NORM_FUSED_TM, tk=1024):
    M, N = pairs[0][0].shape[0], pairs[0][1].shape[0]
    tm = _tile(M, tm)
    tks = [_tile(a.shape[1], tk) for a, _, _ in pairs]
    steps = [a.shape[1] // t for (a, _, _), t in zip(pairs, tks)]
    offs = [sum(steps[:p]) for p in range(len(pairs))]
    total = sum(steps)

    n_extra = 3 if norm_bwd is not None else 0

    def body(*refs):
        a_refs, b_refs = refs[0:2 * len(pairs):2], refs[1:2 * len(pairs):2]
        extra = refs[2 * len(pairs):2 * len(pairs) + n_extra]
        outs, acc_ref = refs[2 * len(pairs) + n_extra:-1], refs[-1]
        k = pl.program_id(1)
        for p in range(len(pairs)):
            @pl.when((k >= offs[p]) & (k < offs[p] + steps[p]))
            def _(p=p):
                prod = _bdot(a_refs[p][...], b_refs[p][...], NT)
                if p == 0:
                    @pl.when(k == 0)
                    def _():
                        acc_ref[...] = prod

                    @pl.when(k > 0)
                    def _():
                        acc_ref[...] += prod
                else:
                    acc_ref[...] += prod

        @pl.when(k == total - 1)
        def _():
            if norm_bwd is not None:
                _rmsnorm_bwd_tile(acc_ref[...], *extra, outs[0], outs[1], first=pl.program_id(0) == 0)
            else:
                outs[0][...] = acc_ref[...]

    in_specs, args = [], []
    for (a, b, c0), t, off, n in zip(pairs, tks, offs, steps):
        assert c0 % t == 0
        pick = lambda k, off=off, n=n: jnp.clip(k - off, 0, n - 1)
        in_specs += [pl.BlockSpec((tm, t), lambda i, k, pick=pick: (i, pick(k))),
                     pl.BlockSpec((N, t), lambda i, k, pick=pick, cb0=c0 // t: (0, pick(k) + cb0))]
        args += [a, b]
    row, vec = pl.BlockSpec((tm, N), lambda i, k: (i, 0)), pl.BlockSpec((1, N), lambda i, k: (0, 0))
    if norm_bwd is not None:
        in_specs += [row, vec, row]
        args += list(norm_bwd)
        out_specs, out_shape = [row, vec], [jax.ShapeDtypeStruct((M, N), F32), jax.ShapeDtypeStruct((1, N), F32)]
    else:
        out_specs, out_shape = [row], [jax.ShapeDtypeStruct((M, N), F32)]
    outs, landed = _call(body, comm, name=name, grid=(M // tm, total), in_specs=in_specs, out_specs=out_specs,
                         out_shape=out_shape, scratch_shapes=[pltpu.VMEM((tm, N), F32)],
                         semantics=("arbitrary", "arbitrary"), args=tuple(args))
    return (outs if norm_bwd is not None else outs[0]), landed


def _rmsnorm_fwd(x, g, name, comm=None):
    T, D = x.shape
    tt = _tile(T, LONG_ROW_TILE, SUBLANE)

    def body(x_ref, g_ref, o_ref):
        xv = x_ref[...]
        r = lax.rsqrt(jnp.mean(xv * xv, axis=-1, keepdims=True) + RMS_EPS)
        o_ref[...] = (xv * r * g_ref[...]).astype(BF16)

    outs, landed = _call(
        body, comm, name=name, grid=(T // tt,),
        in_specs=[pl.BlockSpec((tt, D), lambda i: (i, 0)), pl.BlockSpec((1, D), lambda i: (0, 0))],
        out_specs=[pl.BlockSpec((tt, D), lambda i: (i, 0))], out_shape=[jax.ShapeDtypeStruct((T, D), BF16)],
        scratch_shapes=[], semantics=("parallel",), args=(x, g))
    return outs[0], landed


def _rmsnorm_bwd(dh, x, g, dx_res, name):
    T, D = x.shape
    tt = _tile(T, LONG_ROW_TILE // 2, SUBLANE)

    def body(dh_ref, x_ref, g_ref, res_ref, dx_ref, dg_ref):
        _rmsnorm_bwd_tile(dh_ref[...], x_ref, g_ref, res_ref, dx_ref, dg_ref, first=pl.program_id(0) == 0)

    row = pl.BlockSpec((tt, D), lambda i: (i, 0))
    vec = pl.BlockSpec((1, D), lambda i: (0, 0))
    return pl.pallas_call(
        body, name=name, grid=(T // tt,),
        in_specs=[row, row, vec, row], out_specs=[row, vec],
        out_shape=[jax.ShapeDtypeStruct((T, D), F32), jax.ShapeDtypeStruct((1, D), F32)],
        compiler_params=_params("arbitrary"),
    )(dh, x, g, dx_res)


def _loss_head(y, target, name="loss_head"):
    T, D = y.shape
    tt = _tile(T, LONG_ROW_TILE, SUBLANE)

    def body(y_ref, t_ref, dy_ref, l_ref):
        e = y_ref[...] - t_ref[...]
        dy_ref[...] = e * (1.0 / D)
        s = jnp.sum(jnp.sum(e * e, axis=1, keepdims=True), axis=0, keepdims=True) * (0.5 / D)
        s = jnp.broadcast_to(s, (1, LANE))

        @pl.when(pl.program_id(0) == 0)
        def _():
            l_ref[...] = s

        @pl.when(pl.program_id(0) > 0)
        def _():
            l_ref[...] += s

    row = pl.BlockSpec((tt, D), lambda i: (i, 0))
    return pl.pallas_call(
        body, name=name, grid=(T // tt,),
        in_specs=[row, row], out_specs=[row, pl.BlockSpec((1, LANE), lambda i: (0, 0))],
        out_shape=[jax.ShapeDtypeStruct((T, D), F32), jax.ShapeDtypeStruct((1, LANE), F32)],
        compiler_params=_params("arbitrary"),
    )(y, target)


def _down(x, k):
    return pltpu.roll(x, k, 0) if k else x


def _up(x, k):
    return pltpu.roll(x, x.shape[0] - k, 0) if k else x


def _sc_fwd(proj, conv_w, name):
    T = proj.shape[0]
    tt = _tile(T, WIDE_ROW_TILE, SUBLANE)
    B = SC_BLK

    def body(p_ref, ph_ref, w_ref, o_ref):
        keep = (pl.program_id(0) > 0).astype(F32)
        for j in range(SC_NBLK):
            cb, cc, cu, cg = (slice(k * SC_W + j * B, k * SC_W + (j + 1) * B) for k in range(4))
            cw = slice(j * B, (j + 1) * B)
            z = jnp.concatenate([ph_ref[:, cc] * ph_ref[:, cu] * keep, p_ref[:, cc] * p_ref[:, cu]], axis=0)
            cz = (w_ref[2:3, cw] * z + w_ref[1:2, cw] * _down(z, 1) + w_ref[0:1, cw] * _down(z, 2))[HALO:]
            gate = p_ref[:, cg]
            o_ref[:, cw] = (p_ref[:, cb] * cz * (gate * _sigmoid(gate))).astype(BF16)

    return pl.pallas_call(
        body, name=name, grid=(T // tt,),
        in_specs=[pl.BlockSpec((tt, 4 * SC_W), lambda i: (i, 0)),
                  pl.BlockSpec((HALO, 4 * SC_W), lambda i: (jnp.maximum(i * (tt // HALO) - 1, 0), 0)),
                  pl.BlockSpec((SC_CONV, SC_W), lambda i: (0, 0))],
        out_specs=pl.BlockSpec((tt, SC_W), lambda i: (i, 0)),
        out_shape=jax.ShapeDtypeStruct((T, SC_W), BF16),
        compiler_params=_params("parallel"),
    )(proj, proj, conv_w)


def _sc_bwd(dyg, proj, conv_w, name):
    T = proj.shape[0]
    tt = _tile(T, WIDE_ROW_TILE, SUBLANE)
    nt = T // tt
    B = SC_BLK
    hb = tt // HALO

    def body(d_ref, dn_ref, p_ref, pp_ref, pn_ref, w_ref, o_ref, dw_ref):
        i = pl.program_id(0)
        keep_p = (i > 0).astype(F32)
        keep_n = (i < nt - 1).astype(F32)
        main = slice(HALO, HALO + tt)
        parts = []
        for j in range(SC_NBLK):
            cw = slice(j * B, (j + 1) * B)

            def ext(k):
                s = slice(k * SC_W + j * B, k * SC_W + (j + 1) * B)
                return s, jnp.concatenate([pp_ref[:, s] * keep_p, p_ref[:, s], pn_ref[:, s]], axis=0)

            (sb, b), (sc, c), (su, u), (sg_, gate) = ext(0), ext(1), ext(2), ext(3)
            dyg_e = jnp.concatenate([jnp.zeros((HALO, B), F32), d_ref[:, cw], dn_ref[:, cw] * keep_n], axis=0)
            w0, w1, w2 = w_ref[0:1, cw], w_ref[1:2, cw], w_ref[2:3, cw]
            z = c * u
            z1, z2 = _down(z, 1), _down(z, 2)
            cz = w2 * z + w1 * z1 + w0 * z2
            sg, dsg = _silu_and_grad(gate)
            dy = dyg_e * sg
            dcz = dy * b
            dz = w2 * dcz + w1 * _up(dcz, 1) + w0 * _up(dcz, 2)
            o_ref[:, sb] = (dy * cz)[main].astype(BF16)
            o_ref[:, sc] = (dz * u)[main].astype(BF16)
            o_ref[:, su] = (dz * c)[main].astype(BF16)
            o_ref[:, sg_] = (dyg_e * (b * cz) * dsg)[main].astype(BF16)
            dcm = dcz[main]
            parts.append(jnp.concatenate([jnp.sum(dcm * z2[main], axis=0, keepdims=True),
                                          jnp.sum(dcm * z1[main], axis=0, keepdims=True),
                                          jnp.sum(dcm * z[main], axis=0, keepdims=True)], axis=0))
        part = jnp.concatenate(parts, axis=1)

        @pl.when(i == 0)
        def _():
            dw_ref[...] = part

        @pl.when(i > 0)
        def _():
            dw_ref[...] += part

    nxt = lambda i: (jnp.minimum((i + 1) * hb, nt * hb - 1), 0)
    return pl.pallas_call(
        body, name=name, grid=(nt,),
        in_specs=[pl.BlockSpec((tt, SC_W), lambda i: (i, 0)),
                  pl.BlockSpec((HALO, SC_W), nxt),
                  pl.BlockSpec((tt, 4 * SC_W), lambda i: (i, 0)),
                  pl.BlockSpec((HALO, 4 * SC_W), lambda i: (jnp.maximum(i * hb - 1, 0), 0)),
                  pl.BlockSpec((HALO, 4 * SC_W), nxt),
                  pl.BlockSpec((SC_CONV, SC_W), lambda i: (0, 0))],
        out_specs=[pl.BlockSpec((tt, 4 * SC_W), lambda i: (i, 0)), pl.BlockSpec((SC_CONV, SC_W), lambda i: (0, 0))],
        out_shape=[jax.ShapeDtypeStruct((T, 4 * SC_W), BF16), jax.ShapeDtypeStruct((SC_CONV, SC_W), F32)],
        compiler_params=_params("arbitrary"),
    )(dyg, dyg, proj, proj, proj, conv_w)


def _split3_dot(x, m):
    hi = x.astype(BF16)
    r1 = x - hi.astype(F32)
    mid = r1.astype(BF16)
    lo = (r1 - mid.astype(F32)).astype(BF16)
    return _dot(hi, m) + _dot(mid, m) + _dot(lo, m)


def _split2_dot(x, m):
    hi = x.astype(BF16)
    lo = (x - hi.astype(F32)).astype(BF16)
    return _dot(hi, m) + _dot(lo, m)


def _head_mean_matrix():
    r, c = _iota2((LANE, LANE), 0), _iota2((LANE, LANE), 1)
    return jnp.where((r // SB_DH) == (c // SB_DH), 1.0 / SB_DH, 0.0).astype(BF16)


def _sb_prep(proj, qg2, kg2, name):
    T = proj.shape[0]
    tt = _tile(T, WIDE_ROW_TILE, SUBLANE)

    def body(p_ref, qg_ref, kg_ref, q_ref, k_ref, v_ref):
        bd = _head_mean_matrix()

        def norm(x, g, scale):
            r = lax.rsqrt(_split3_dot(x * x, bd) + RMS_EPS)
            return (x * r * g * scale).astype(BF16)

        v_ref[...] = p_ref[:, 2 * SB_W:3 * SB_W].astype(BF16)
        for p in range(SB_PAIRS):
            cols = slice(p * LANE, (p + 1) * LANE)
            q_ref[:, cols] = norm(p_ref[:, cols], qg_ref[...], SB_DH ** -0.5)
            k_ref[:, cols] = norm(p_ref[:, SB_W + p * LANE:SB_W + (p + 1) * LANE], kg_ref[...], 1.0)

    blk = pl.BlockSpec((tt, SB_W), lambda i: (i, 0))
    vec = pl.BlockSpec((1, LANE), lambda i: (0, 0))
    return pl.pallas_call(
        body, name=name, grid=(T // tt,),
        in_specs=[pl.BlockSpec((tt, 4 * SB_W), lambda i: (i, 0)), vec, vec],
        out_specs=[blk, blk, blk],
        out_shape=[jax.ShapeDtypeStruct((T, SB_W), BF16)] * 3,
        compiler_params=_params("parallel"),
    )(proj, qg2, kg2)


def _sb_prep_bwd(proj, dqn, dkn, dv, dgate, qg2, kg2, name):
    T = proj.shape[0]
    tt = _tile(T, WIDE_ROW_TILE, SUBLANE)

    def body(p_ref, dq_ref, dk_ref, dv_ref, dg_ref, qg_ref, kg_ref, o_ref, dqg_ref, dkg_ref):
        i = pl.program_id(0)
        bd = _head_mean_matrix()

        def norm_bwd(x, g, dy):
            r = lax.rsqrt(_split3_dot(x * x, bd) + RMS_EPS)
            xh = x * r
            dxh = dy * g
            m = _split3_dot(dxh * xh, bd)
            return r * (dxh - xh * m), jnp.sum(dy * xh, axis=0, keepdims=True)

        o_ref[:, 2 * SB_W:3 * SB_W] = dv_ref[...].astype(BF16)
        o_ref[:, 3 * SB_W:4 * SB_W] = dg_ref[...].astype(BF16)
        pq = pk = jnp.zeros((1, LANE), F32)
        for p in range(SB_PAIRS):
            cols, kcols = slice(p * LANE, (p + 1) * LANE), slice(SB_W + p * LANE, SB_W + (p + 1) * LANE)
            dxq, sq = norm_bwd(p_ref[:, cols], qg_ref[...], dq_ref[:, cols])
            dxk, sk = norm_bwd(p_ref[:, kcols], kg_ref[...], dk_ref[:, cols])
            o_ref[:, cols] = dxq.astype(BF16)
            o_ref[:, kcols] = dxk.astype(BF16)
            pq, pk = pq + sq, pk + sk

        @pl.when(i == 0)
        def _():
            dqg_ref[...] = pq
            dkg_ref[...] = pk

        @pl.when(i > 0)
        def _():
            dqg_ref[...] += pq
            dkg_ref[...] += pk

    blk = pl.BlockSpec((tt, SB_W), lambda i: (i, 0))
    vec = pl.BlockSpec((1, LANE), lambda i: (0, 0))
    wide = pl.BlockSpec((tt, 4 * SB_W), lambda i: (i, 0))
    return pl.pallas_call(
        body, name=name, grid=(T // tt,),
        in_specs=[wide, blk, blk, blk, blk, vec, vec],
        out_specs=[wide, vec, vec],
        out_shape=[jax.ShapeDtypeStruct((T, 4 * SB_W), BF16)] + [jax.ShapeDtypeStruct((1, LANE), F32)] * 2,
        compiler_params=_params("arbitrary"),
    )(proj, dqn, dkn, dv, dgate, qg2, kg2)


def _fold_heads(part, name):
    def body(p_ref, o_ref):
        r, c = _iota2((LANE, SB_DH), 0), _iota2((LANE, SB_DH), 1)
        fold = jnp.where((r % SB_DH) == c, 1.0, 0.0).astype(F32)
        o_ref[...] = jnp.sum(_hdot(p_ref[...], fold), axis=0, keepdims=True)

    return pl.pallas_call(body, name=name, out_shape=jax.ShapeDtypeStruct((1, SB_DH), F32))(part)


def _sb_masks():
    lane = _iota2((1, LANE), 1)
    return lane < SB_DH


def _sb_attn_fwd(qn, kn, vb, proj, name, comm=None):
    T = qn.shape[0]
    tq, tk = _tile(T, SB_TQ, SUBLANE), SB_TK
    assert tq % tk == 0

    def body(q_ref, k_ref, v_ref, g_ref, o_ref, og_ref, lt_ref, done_ref):
        i = pl.program_id(1)
        ma = _sb_masks()
        q2 = q_ref[...]
        zero = jnp.zeros_like(q2)
        qs = (jnp.where(ma, q2, zero), jnp.where(ma, zero, q2))
        upper = (_iota2((tk, tk), 0) > _iota2((tk, tk), 1)).astype(BF16)
        qpos = i * tq + _iota2((tq, tk), 0)
        nb = tq // tk

        def trip(kb_top, masked, carry):
            acc, la, lb = carry
            chains = [(b, h) for b in range(nb) for h in range(2)]
            k2s, vss, masks = [], [], []
            for b in range(nb):
                kb = kb_top - b
                rows = pl.ds(pl.multiple_of(kb * tk, tk), tk)
                k2s.append(k_ref[rows, :])
                v2 = v_ref[rows, :]
                zv = jnp.zeros_like(v2)
                vss.append((jnp.where(ma, v2, zv), jnp.where(ma, zv, v2)))
                masks.append((kb * tk + _iota2((tq, tk), 1)) < qpos if masked else None)
            zs = [_dot(qs[h], k2s[b], NT) for b, h in chains]
            ts = [jnp.log(1.0 + jnp.exp(-jnp.abs(z))) for z in zs]
            ls = [-(jnp.maximum(z, 0.0) + t) for z, t in zip(zs, ts)]
            if masked:
                ls = [jnp.where(masks[b], l, 0.0) for (b, h), l in zip(chains, ls)]
            cums = [_split2_dot(l, upper) for l in ls]
            sums = [jnp.sum(l, axis=1, keepdims=True) for l in ls]
            offs, tot = {}, [la, lb]
            for b in range(nb):
                for h in range(2):
                    offs[(b, h)] = tot[h]
                    tot[h] = tot[h] + sums[chains.index((b, h))]
            ws = [jnp.exp(jnp.minimum(z, 0.0) - t + c + offs[ch]) for ch, z, t, c in zip(chains, zs, ts, cums)]
            if masked:
                ws = [jnp.where(masks[b], w, 0.0) for (b, h), w in zip(chains, ws)]
            for (b, h), w in zip(chains, ws):
                acc = acc + _dot(w.astype(BF16), vss[b][h])
            return acc, tot[0], tot[1]

        def largest(la, lb):
            return jnp.max(jnp.maximum(la, lb))

        z1 = jnp.zeros((tq, 1), F32)
        acc, la, lb = trip((i + 1) * nb - 1, True, (jnp.zeros((tq, LANE), F32), z1, z1))

        def live(c):
            return (c[0] < i) & (c[4] > SB_DEAD)

        def more(c):
            j, acc, la, lb, _ = c
            acc, la, lb = trip((i - j) * nb - 1, False, (acc, la, lb))
            return j + 1, acc, la, lb, largest(la, lb)

        done, acc, la, lb, _ = lax.while_loop(live, more, (jnp.int32(0), acc, la, lb, largest(la, lb)))
        gate = g_ref[...]
        o_ref[...] = acc
        og_ref[...] = (acc * (gate * _sigmoid(gate))).astype(BF16)
        lt_ref[...] = jnp.where(_iota2((tq, 2), 1) == 0, la, lb)
        done_ref[...] = jnp.full((SUBLANE, LANE), done, F32)

    nq = T // tq
    qblk = pl.BlockSpec((tq, LANE), lambda p, i: (i, p))
    full = pl.BlockSpec((T, LANE), lambda p, i: (0, p))
    return _call(
        body, comm, name=name, grid=(SB_PAIRS, nq),
        in_specs=[qblk, full, full, pl.BlockSpec((tq, LANE), lambda p, i: (i, 3 * SB_PAIRS + p))],
        out_specs=[qblk, qblk, pl.BlockSpec((None, tq, 2), lambda p, i: (p, i, 0)),
                   pl.BlockSpec((None, None, SUBLANE, LANE), lambda p, i: (p, i, 0, 0))],
        out_shape=[jax.ShapeDtypeStruct((T, SB_W), F32), jax.ShapeDtypeStruct((T, SB_W), BF16),
                   jax.ShapeDtypeStruct((SB_PAIRS, T, 2), F32), jax.ShapeDtypeStruct((SB_PAIRS, nq, SUBLANE, LANE), F32)],
        scratch_shapes=[], semantics=("parallel", "parallel"), args=(qn, kn, vb, proj))


def _sb_attn_bwd(qn, kn, vb, dog, o, ltot, done, proj, name, comm=None):
    T = qn.shape[0]
    tq, tk = _tile(T, SB_TQ, SUBLANE), SB_TK

    def body(q_ref, k_ref, v_ref, dog_ref, o_ref, lt_ref, done_ref, g_ref, dq_ref, dk_ref, dv_ref, dgate_ref):
        i = pl.program_id(1)
        first_trip = i - jnp.max(done_ref[...]).astype(jnp.int32)

        @pl.when(i == 0)
        def _():
            dk_ref[...] = jnp.zeros_like(dk_ref)
            dv_ref[...] = jnp.zeros_like(dv_ref)

        ma = _sb_masks()
        gate, o2, dog2 = g_ref[...], o_ref[...], dog_ref[...]
        sg, dsg = _silu_and_grad(gate)
        do2 = dog2 * sg
        dgate_ref[...] = dog2 * o2 * dsg
        lt = lt_ref[...]
        first = _iota2((tq, 2), 1) == 0
        ltots = (jnp.sum(jnp.where(first, lt, 0.0), axis=1, keepdims=True),
                 jnp.sum(jnp.where(first, 0.0, lt), axis=1, keepdims=True))
        q2 = q_ref[...]
        zq = jnp.zeros_like(q2)
        qs = (jnp.where(ma, q2, zq), jnp.where(ma, zq, q2))
        dob = do2.astype(BF16)
        dos = (jnp.where(ma, dob, zq), jnp.where(ma, zq, dob))
        upto = (_iota2((tk, tk), 0) <= _iota2((tk, tk), 1)).astype(BF16)
        before = (_iota2((tk, tk), 0) < _iota2((tk, tk), 1)).astype(BF16)
        qpos = i * tq + _iota2((tq, tk), 0)
        nb = tq // tk

        def trip(kb_bot, masked, carry):
            dq, la, lb, ea, eb = carry
            chains = [(b, h) for b in range(nb) for h in range(2)]
            rows, k2s, v2s, kss, masks = [], [], [], [], []
            for b in range(nb):
                kb = kb_bot + b
                rows.append(pl.ds(pl.multiple_of(kb * tk, tk), tk))
                k2 = k_ref[rows[b], :]
                zk = jnp.zeros_like(k2)
                k2s.append(k2)
                v2s.append(v_ref[rows[b], :])
                kss.append((jnp.where(ma, k2, zk), jnp.where(ma, zk, k2)))
                masks.append((kb * tk + _iota2((tq, tk), 1)) < qpos if masked else None)

            def keep(vals):
                return [jnp.where(masks[b], x, 0.0) for (b, h), x in zip(chains, vals)] if masked else vals

            zs = [_dot(qs[h], k2s[b], NT) for b, h in chains]
            dws = [_dot(dos[h], v2s[b], NT) for b, h in chains]
            ts = [jnp.log(1.0 + jnp.exp(-jnp.abs(z))) for z in zs]
            ls = keep([-(jnp.maximum(z, 0.0) + t) for z, t in zip(zs, ts)])
            lps = [jnp.minimum(z, 0.0) - t for z, t in zip(zs, ts)]
            cums = [_split3_dot(l, upto) for l in ls]
            lsums = [jnp.sum(l, axis=1, keepdims=True) for l in ls]
            offs, tot = {}, [la, lb]
            for b in range(nb):
                for h in range(2):
                    offs[(b, h)] = tot[h]
                    tot[h] = tot[h] + lsums[chains.index((b, h))]
            ws = keep([jnp.exp(lp + (ltots[h] - (offs[(b, h)] + c))) for (b, h), lp, c in zip(chains, lps, cums)])
            es = [dw * w for dw, w in zip(dws, ws)]
            ecums = [_split2_dot(e, before) for e in es]
            esums = [jnp.sum(e, axis=1, keepdims=True) for e in es]
            eoffs, etot = {}, [ea, eb]
            for b in range(nb):
                for h in range(2):
                    eoffs[(b, h)] = etot[h]
                    etot[h] = etot[h] + esums[chains.index((b, h))]
            dzs = keep([e - jnp.exp(lp) * (e + eoffs[ch] + ec) for ch, e, lp, ec in zip(chains, es, lps, ecums)])
            dzs = [dz.astype(BF16) for dz in dzs]
            wbs = [w.astype(BF16) for w in ws]
            for (b, h), dz in zip(chains, dzs):
                dq = dq + _dot(dz, kss[b][h])
            for b in range(nb):
                ia, ib = chains.index((b, 0)), chains.index((b, 1))
                dk_ref[rows[b], :] += _dot(dzs[ia], qs[0], TN) + _dot(dzs[ib], qs[1], TN)
                dv_ref[rows[b], :] += _dot(wbs[ia], dos[0], TN) + _dot(wbs[ib], dos[1], TN)
            return dq, tot[0], tot[1], etot[0], etot[1]

        z1 = jnp.zeros((tq, 1), F32)
        carry = lax.fori_loop(first_trip, i, lambda j, c: trip(j * nb, False, c),
                              (jnp.zeros((tq, LANE), F32), z1, z1, z1, z1))
        dq = trip(i * nb, True, carry)[0]
        dq_ref[...] = dq * (SB_DH ** -0.5)

    qblk = pl.BlockSpec((tq, LANE), lambda p, i: (i, p))
    full = pl.BlockSpec((T, LANE), lambda p, i: (0, p))
    return _call(
        body, comm, name=name, grid=(SB_PAIRS, T // tq),
        in_specs=[qblk, full, full, qblk, qblk, pl.BlockSpec((None, tq, 2), lambda p, i: (p, i, 0)),
                  pl.BlockSpec((None, None, SUBLANE, LANE), lambda p, i: (p, i, 0, 0)),
                  pl.BlockSpec((tq, LANE), lambda p, i: (i, 3 * SB_PAIRS + p))],
        out_specs=[qblk, full, full, qblk],
        out_shape=[jax.ShapeDtypeStruct((T, SB_W), F32)] * 4,
        scratch_shapes=[], semantics=("parallel", "arbitrary"), args=(qn, kn, vb, dog, o, ltot, done, proj))


def _dn_conv(ext, w_ref, cw):
    return (w_ref[3:4, cw] * ext + w_ref[2:3, cw] * _down(ext, 1) + w_ref[1:2, cw] * _down(ext, 2)
            + w_ref[0:1, cw] * _down(ext, 3))


def _dn_gates(a_in, b_in, a_log, dt_bias, name):
    T, H = a_in.shape
    C = DN_CHUNK

    def body(a_ref, b_ref, al_ref, dt_ref, g_ref, beta_ref):
        beta_ref[...] = _sigmoid(b_ref[...])
        g_ref[...] = -jnp.exp(al_ref[...]) * _softplus(a_ref[...] + dt_ref[...])
        tri = (_iota2((C, C), 0) >= _iota2((C, C), 1)).astype(F32)

        def chunk(n, carry):
            rows = pl.ds(pl.multiple_of(n * C, C), C)
            g_ref[rows, :] = _hdot(tri, g_ref[rows, :])
            return carry

        lax.fori_loop(0, T // C, chunk, 0)

    return pl.pallas_call(body, name=name, out_shape=[jax.ShapeDtypeStruct((T, H), F32)] * 2)(a_in, b_in, a_log, dt_bias)


def _dn_gates_bwd(dg, dbeta, a_in, b_in, a_log, dt_bias, name):
    T, H = a_in.shape
    C = DN_CHUNK

    def body(dg_ref, db_ref, a_ref, b_ref, al_ref, dt_ref, da_ref, dbi_ref, dal_ref, ddt_ref):
        tri_t = (_iota2((C, C), 0) <= _iota2((C, C), 1)).astype(F32)

        def chunk(n, carry):
            rows = pl.ds(pl.multiple_of(n * C, C), C)
            da_ref[rows, :] = _hdot(tri_t, dg_ref[rows, :])
            return carry

        lax.fori_loop(0, T // C, chunk, 0)
        dla = da_ref[...]
        x = a_ref[...] + dt_ref[...]
        ea = jnp.exp(al_ref[...])
        da = dla * (-ea) * _sigmoid(x)
        da_ref[...] = da
        dal_ref[...] = jnp.sum(dla * (-ea * _softplus(x)), axis=0, keepdims=True)
        ddt_ref[...] = jnp.sum(da, axis=0, keepdims=True)
        beta = _sigmoid(b_ref[...])
        dbi_ref[...] = db_ref[...] * beta * (1.0 - beta)

    return pl.pallas_call(
        body, name=name,
        out_shape=[jax.ShapeDtypeStruct((T, H), F32)] * 2 + [jax.ShapeDtypeStruct((1, H), F32)] * 2,
    )(dg, dbeta, a_in, b_in, a_log, dt_bias)


def _dn_chunk_terms(q, k, gc, bc):
    C = DN_CHUNK
    r, c = _iota2((C, C), 0), _iota2((C, C), 1)
    lower, strict, eye = r >= c, r > c, r == c
    grow = jnp.sum(jnp.where(eye, gc, 0.0), axis=0, keepdims=True)
    decay = jnp.where(lower, jnp.exp(jnp.where(lower, gc - grow, 0.0)), 0.0)
    last = _iota2((C, 1), 0) == C - 1
    gl = jnp.sum(jnp.where(last, gc, 0.0), axis=0, keepdims=True)
    eg = jnp.exp(gc)
    egl = jnp.exp(gl - gc)
    kb = k * bc
    lmat = jnp.where(strict, _bdot(kb, k, NT) * decay, 0.0)
    aqk = jnp.where(lower, _bdot(q, k, NT) * decay, 0.0)
    return dict(lower=lower, strict=strict, eye=eye, last=last, decay=decay, gl=gl, eg=eg, egl=egl, kb=kb,
                lmat=lmat, aqk=aqk, qd=q * eg, kd=k * egl)


def _split(x):
    hi = x.astype(BF16)
    return hi, (x - hi.astype(F32)).astype(BF16)


def _x3dot(a, b, dims=NN):
    ah, al = a if isinstance(a, tuple) else _split(a)
    bh, bl = b if isinstance(b, tuple) else _split(b)
    return _dot(ah, bh, dims) + (_dot(ah, bl, dims) + _dot(al, bh, dims))


def _interleave(gens):
    for _ in itertools.zip_longest(*gens):
        pass


def _unit_lower_inverse_steps(lmat, eye, out):
    ident = jnp.where(eye, 1.0, 0.0).astype(F32)
    m = -lmat
    inv = ident + m
    for _ in range(int(math.log2(DN_CHUNK)) - 1):
        ms = _split(m)
        m = _x3dot(ms, ms)
        yield
        inv = inv + _x3dot(inv, m)
        yield
    out["tm"] = inv


def _dn_chunk_fwd(pqkv, conv_w, g, beta, pgate, gn, name, comm=None):
    T = pqkv.shape[0]
    C, H = DN_CHUNK, DN_HEADS
    N = T // C
    B = DN_PREP_BLK
    nq, nqk = DN_QK_W // B, 2 * DN_QK_W // B

    def step(p_ref, cw_ref, g_ref, b_ref, pg_ref, gn_ref, act_out, o_ref, og_ref, s_out, t_out, vn_out, u_out, w_out,
             s_scr, tail_scr, a_ref, a_next):
        head_lane = _iota2((C, H), 1)

        def prepare(cb):
            cw = slice(cb * B, (cb + 1) * B)
            ext = jnp.concatenate([tail_scr[:, cw], p_ref[:, cw]], axis=0)
            c = _dn_conv(ext, cw_ref, cw)[HALO:]
            yield
            a = c * _sigmoid(c)
            if cb >= nqk:
                a_next[:, cw] = a
                act_out[:, cw] = a
                return
            scale = DN_DK ** -0.5 if cb < nq else 1.0
            for hh in range(B // DN_DK):
                yield
                ah = a[:, hh * DN_DK:(hh + 1) * DN_DK]
                val = ah * (lax.rsqrt(jnp.sum(ah * ah, axis=-1, keepdims=True) + L2_EPS) * scale)
                cols = slice(cb * B + hh * DN_DK, cb * B + (hh + 1) * DN_DK)
                a_next[:, cols] = val
                act_out[:, cols] = val

        def head(hh):
            qs, vs = slice(hh * DN_DK, (hh + 1) * DN_DK), slice(hh * DN_DV, (hh + 1) * DN_DV)
            q, k, v = a_ref[:, qs], a_ref[:, DN_QK_W + hh * DN_DK:DN_QK_W + (hh + 1) * DN_DK], \
                a_ref[:, 2 * DN_QK_W + hh * DN_DV:2 * DN_QK_W + (hh + 1) * DN_DV]
            gc = jnp.sum(jnp.where(head_lane == hh, g_ref[...], 0.0), axis=1, keepdims=True)
            bc = jnp.sum(jnp.where(head_lane == hh, b_ref[...], 0.0), axis=1, keepdims=True)
            t = _dn_chunk_terms(q, k, gc, bc)
            yield
            res = {}
            yield from _unit_lower_inverse_steps(t["lmat"], t["eye"], res)
            tms = _split(res["tm"])
            u = _x3dot(tms, v * bc)
            yield
            w = _x3dot(tms, t["kb"] * t["eg"])
            yield
            s = s_scr[hh]
            s_out[hh] = s
            t_out[hh] = res["tm"]
            sb = s.astype(BF16)
            vn = u - _dot(w.astype(BF16), sb)
            yield
            o = _dot(t["qd"].astype(BF16), sb) + _bdot(t["aqk"], vn)
            yield
            s_scr[hh] = s * jnp.exp(t["gl"]) + _bdot(t["kd"], vn, TN)
            vn_out[:, vs] = vn
            u_out[:, vs] = u
            w_out[:, qs] = w
            o_ref[:, vs] = o
            gate = pg_ref[:, vs]
            r = lax.rsqrt(jnp.mean(o * o, axis=-1, keepdims=True) + RMS_EPS)
            og_ref[:, vs] = (o * r * gn_ref[...] * (gate * _sigmoid(gate))).astype(BF16)

        _interleave([prepare(cb) for cb in range(DN_CONV_W // B)] + [head(hh) for hh in range(H)])

        @pl.when(pl.program_id(0) < N - 1)
        def _():
            tail_scr[...] = p_ref[C - HALO:C, :]

    def body(*refs):
        s = pl.program_id(0)
        io, (s_scr, tail_scr, buf_a, buf_b) = refs[:-4], refs[-4:]

        @pl.when(s == 0)
        def _():
            tail_scr[...] = jnp.zeros_like(tail_scr)
            buf_b[...] = jnp.zeros_like(buf_b)

        @pl.when(s <= 1)
        def _():
            s_scr[...] = jnp.zeros_like(s_scr)

        @pl.when(s % 2 == 0)
        def _():
            step(*io, s_scr, tail_scr, buf_b, buf_a)

        @pl.when(s % 2 == 1)
        def _():
            step(*io, s_scr, tail_scr, buf_a, buf_b)

    nxt = lambda w: pl.BlockSpec((C, w), lambda s: (jnp.minimum(s, N - 1), 0))
    cur = lambda w: pl.BlockSpec((C, w), lambda s: (jnp.maximum(s - 1, 0), 0))
    per_chunk = lambda a, b: pl.BlockSpec((H, None, a, b), lambda s: (0, jnp.maximum(s - 1, 0), 0, 0))
    return _call(
        body, comm, name=name, grid=(N + 1,),
        in_specs=[nxt(DN_CONV_W), pl.BlockSpec((DN_CONV, DN_CONV_W), lambda s: (0, 0)), cur(H), cur(H), cur(DN_V_W),
                  pl.BlockSpec((1, DN_DV), lambda s: (0, 0))],
        out_specs=[nxt(DN_CONV_W), cur(DN_V_W), cur(DN_V_W), per_chunk(DN_DK, DN_DV), per_chunk(C, C),
                   cur(DN_V_W), cur(DN_V_W), cur(DN_QK_W)],
        out_shape=[jax.ShapeDtypeStruct((T, DN_CONV_W), F32),
                   jax.ShapeDtypeStruct((T, DN_V_W), F32), jax.ShapeDtypeStruct((T, DN_V_W), BF16),
                   jax.ShapeDtypeStruct((H, N, DN_DK, DN_DV), F32),
                   jax.ShapeDtypeStruct((H, N, C, C), F32),
                   jax.ShapeDtypeStruct((T, DN_V_W), F32),
                   jax.ShapeDtypeStruct((T, DN_V_W), F32),
                   jax.ShapeDtypeStruct((T, DN_QK_W), F32)],
        scratch_shapes=[pltpu.VMEM((H, DN_DK, DN_DV), F32), pltpu.VMEM((HALO, DN_CONV_W), F32),
                        pltpu.VMEM((C, DN_CONV_W), F32), pltpu.VMEM((C, DN_CONV_W), F32)],
        semantics=("arbitrary",), args=(pqkv, conv_w, g, beta, pgate, gn))


def _dn_chunk_bwd(pqkv, conv_w, act, g, beta, s_saved, tm_saved, vn_saved, u_saved, w_saved, dog, o_raw, pgate, gn,
                  name, comm=None):
    T = act.shape[0]
    C, H = DN_CHUNK, DN_HEADS
    N = T // C
    assert N % 2 == 0
    B = DN_PREP_BLK
    nq, nqk = DN_QK_W // B, 2 * DN_QK_W // B
    main = slice(HALO, HALO + C)

    def prepare_bwd(cb, p_ref, pp_ref, pn_ref, cw_ref, dread, dnext_scr, dp_ref, conv_parts):
        s = pl.program_id(0)
        keep_p = (N - s > 0).astype(F32)
        keep_n = (s > 1).astype(F32)
        cw = slice(cb * B, (cb + 1) * B)
        ext = jnp.concatenate([pp_ref[:, cw] * keep_p, p_ref[:, cw], pn_ref[:, cw]], axis=0)
        c = _dn_conv(ext, cw_ref, cw)
        yield
        sg = _sigmoid(c)
        da_dc = sg * (1.0 + c * (1.0 - sg))
        d_up = jnp.concatenate([jnp.zeros((HALO, B), F32), dread[:, cw], dnext_scr[:, cw] * keep_n], axis=0)
        if cb < nqk:
            a = c * sg
            scale = DN_DK ** -0.5 if cb < nq else 1.0
            normed = []
            for hh in range(B // DN_DK):
                yield
                cols = slice(hh * DN_DK, (hh + 1) * DN_DK)
                ah = a[:, cols]
                r = lax.rsqrt(jnp.sum(ah * ah, axis=-1, keepdims=True) + L2_EPS)
                y = ah * r
                dy = d_up[:, cols] * scale
                normed.append(r * (dy - y * jnp.sum(dy * y, axis=-1, keepdims=True)))
            d_up = jnp.concatenate(normed, axis=1)
        yield
        dc = d_up * da_dc
        dp = (cw_ref[3:4, cw] * dc + cw_ref[2:3, cw] * _up(dc, 1) + cw_ref[1:2, cw] * _up(dc, 2)
              + cw_ref[0:1, cw] * _up(dc, 3))
        dp_ref[:, cw] = dp[main].astype(BF16)
        yield
        dcm = dc[main]
        conv_parts[cb] = jnp.concatenate([jnp.sum(dcm * _down(ext, 3 - k)[main], axis=0, keepdims=True)
                                          for k in range(DN_CONV)], axis=0)

    def finish_prepare(conv_parts, dconv_ref, dread, dnext_scr):
        part = jnp.concatenate([conv_parts[cb] for cb in range(DN_CONV_W // B)], axis=1)

        @pl.when(pl.program_id(0) == 0)
        def _():
            dconv_ref[...] = part

        @pl.when(pl.program_id(0) > 0)
        def _():
            dconv_ref[...] += part

        dnext_scr[...] = dread[0:HALO, :]

    def step(a_ref, g_ref, b_ref, s_ref, t_ref, vn_ref, u_ref, w_ref, dog_ref, o_ref, pg_ref, gn_ref,
             p_ref, pp_ref, pn_ref, cw_ref, dp_ref, dconv_ref, dg_ref, db_ref, dgate_ref, dgn_ref,
             ds_scr, dnext_scr, dwrite, dread):
        head_lane = _iota2((C, H), 1)
        dg_cols, db_cols, dgn_parts, conv_parts = {}, {}, {}, {}

        def output_gate_bwd(hh, vs):
            d, o, gate, gn_v = dog_ref[:, vs], o_ref[:, vs], pg_ref[:, vs], gn_ref[...]
            sg, dsg = _silu_and_grad(gate)
            r = lax.rsqrt(jnp.mean(o * o, axis=-1, keepdims=True) + RMS_EPS)
            n = o * r
            dy = d * sg
            dgate_ref[:, vs] = (d * (n * gn_v) * dsg).astype(BF16)
            dn = dy * gn_v
            dgn_parts[hh] = jnp.sum(dy * n, axis=0, keepdims=True)
            return r * (dn - n * jnp.mean(dn * n, axis=-1, keepdims=True))

        def head(hh):
            qs, vs = slice(hh * DN_DK, (hh + 1) * DN_DK), slice(hh * DN_DV, (hh + 1) * DN_DV)
            ks = slice(DN_QK_W + hh * DN_DK, DN_QK_W + (hh + 1) * DN_DK)
            vas = slice(2 * DN_QK_W + hh * DN_DV, 2 * DN_QK_W + (hh + 1) * DN_DV)
            q, k, v = a_ref[:, qs], a_ref[:, ks], a_ref[:, vas]
            gc = jnp.sum(jnp.where(head_lane == hh, g_ref[...], 0.0), axis=1, keepdims=True)
            bc = jnp.sum(jnp.where(head_lane == hh, b_ref[...], 0.0), axis=1, keepdims=True)
            t = _dn_chunk_terms(q, k, gc, bc)
            yield
            lower, strict, eye = t["lower"], t["strict"], t["eye"]
            decay, eg, egl, kb, qd, kd = t["decay"], t["eg"], t["egl"], t["kb"], t["qd"], t["kd"]
            s, tm, vn, u, w = s_ref[hh], t_ref[hh], vn_ref[:, vs], u_ref[:, vs], w_ref[:, qs]
            d_o = output_gate_bwd(hh, vs)
            ds_next = ds_scr[hh]
            egl_tot = jnp.exp(t["gl"])
            dob, sb, dsb, vnb = d_o.astype(BF16), s.astype(BF16), ds_next.astype(BF16), vn.astype(BF16)

            dvn = _bdot(t["aqk"], dob, TN) + _bdot(kd, dsb)
            yield
            daqk = jnp.where(lower, _dot(dob, vnb, NT), 0.0)
            dqd = _dot(dob, sb, NT)
            dkd = _dot(vnb, dsb, NT)
            yield
            dvnb = dvn.astype(BF16)
            ds_scr[hh] = _bdot(qd, dob, TN) + egl_tot * ds_next - _bdot(w, dvnb, TN)
            dgl = egl_tot * jnp.sum(jnp.sum(s * ds_next, axis=1, keepdims=True), axis=0, keepdims=True)
            dw = -_dot(dvnb, sb, NT)
            yield
            tms = _split(tm)
            dru = _x3dot(tms, dvn, TN)
            drw = _x3dot(tms, dw, TN)
            yield
            dl = -jnp.where(strict, _x3dot(dru, u, NT) + _x3dot(drw, w, NT), 0.0)
            yield
            dkk = (dl * decay).astype(BF16)
            dqk = (daqk * decay).astype(BF16)
            dkb = _bdot(dkk, k) + drw * eg
            yield
            dwrite[:, ks] = _bdot(dkk, kb, TN) + _bdot(dqk, q, TN) + dkd * egl + dkb * bc
            dwrite[:, qs] = _bdot(dqk, k) + dqd * eg
            dwrite[:, vas] = dru * bc
            yield
            db_cols[hh] = jnp.sum(dru * v, axis=1, keepdims=True) + jnp.sum(dkb * k, axis=1, keepdims=True)
            pm = dl * t["lmat"] + daqk * t["aqk"]
            col_as_col = jnp.sum(jnp.where(eye, jnp.sum(pm, axis=0, keepdims=True), 0.0), axis=1, keepdims=True)
            kdsum = jnp.sum(dkd * kd, axis=1, keepdims=True)
            dgc = (jnp.sum(pm, axis=1, keepdims=True) - col_as_col + jnp.sum(dqd * qd, axis=1, keepdims=True)
                   - kdsum + jnp.sum(drw * (kb * eg), axis=1, keepdims=True))
            dgl = dgl + jnp.sum(kdsum, axis=0, keepdims=True)
            dg_cols[hh] = dgc + jnp.where(t["last"], dgl, 0.0)

        _interleave([head(hh) for hh in range(H)]
                    + [prepare_bwd(cb, p_ref, pp_ref, pn_ref, cw_ref, dread, dnext_scr, dp_ref, conv_parts)
                       for cb in range(DN_CONV_W // B)])
        dg_ref[...] = sum(jnp.where(head_lane == hh, dg_cols[hh], 0.0) for hh in range(H))
        db_ref[...] = sum(jnp.where(head_lane == hh, db_cols[hh], 0.0) for hh in range(H))
        dgn_part = sum(dgn_parts[hh] for hh in range(H))

        @pl.when(pl.program_id(0) == 0)
        def _():
            dgn_ref[...] = dgn_part

        @pl.when(pl.program_id(0) > 0)
        def _():
            dgn_ref[...] += dgn_part

        finish_prepare(conv_parts, dconv_ref, dread, dnext_scr)

    def body(*refs):
        s = pl.program_id(0)
        io, (ds_scr, dnext_scr, buf_a, buf_b) = refs[:-4], refs[-4:]
        p_ref, pp_ref, pn_ref, cw_ref, dp_ref, dconv_ref = refs[12:18]

        @pl.when(s == 0)
        def _():
            ds_scr[...] = jnp.zeros_like(ds_scr)
            dnext_scr[...] = jnp.zeros_like(dnext_scr)
            buf_b[...] = jnp.zeros_like(buf_b)

        @pl.when((s < N) & (s % 2 == 0))
        def _():
            step(*io, ds_scr, dnext_scr, buf_a, buf_b)

        @pl.when((s < N) & (s % 2 == 1))
        def _():
            step(*io, ds_scr, dnext_scr, buf_b, buf_a)

        @pl.when(s == N)
        def _():
            conv_parts = {}
            _interleave([prepare_bwd(cb, p_ref, pp_ref, pn_ref, cw_ref, buf_b, dnext_scr, dp_ref, conv_parts)
                         for cb in range(DN_CONV_W // B)])
            finish_prepare(conv_parts, dconv_ref, buf_b, dnext_scr)

    cc = lambda s: jnp.maximum(N - 1 - s, 0)
    pc = lambda s: jnp.clip(N - s, 0, N - 1)
    row = lambda w: pl.BlockSpec((C, w), lambda s: (cc(s), 0))
    per_chunk = lambda a, b: pl.BlockSpec((H, None, a, b), lambda s: (0, cc(s), 0, 0))
    vec = pl.BlockSpec((1, DN_DV), lambda s: (0, 0))
    per_c = C // HALO
    conv_spec = pl.BlockSpec((DN_CONV, DN_CONV_W), lambda s: (0, 0))
    return _call(
        body, comm, name=name, grid=(N + 1,),
        in_specs=[row(DN_CONV_W), row(H), row(H), per_chunk(DN_DK, DN_DV), per_chunk(C, C),
                  row(DN_V_W), row(DN_V_W), row(DN_QK_W), row(DN_V_W), row(DN_V_W), row(DN_V_W), vec,
                  pl.BlockSpec((C, DN_CONV_W), lambda s: (pc(s), 0)),
                  pl.BlockSpec((HALO, DN_CONV_W), lambda s: (jnp.maximum(pc(s) * per_c - 1, 0), 0)),
                  pl.BlockSpec((HALO, DN_CONV_W), lambda s: (jnp.minimum((pc(s) + 1) * per_c, N * per_c - 1), 0)),
                  conv_spec],
        out_specs=[pl.BlockSpec((C, DN_CONV_W), lambda s: (pc(s), 0)), conv_spec, row(H), row(H), row(DN_V_W), vec],
        out_shape=[jax.ShapeDtypeStruct((T, DN_CONV_W), BF16), jax.ShapeDtypeStruct((DN_CONV, DN_CONV_W), F32),
                   jax.ShapeDtypeStruct((T, H), F32), jax.ShapeDtypeStruct((T, H), F32),
                   jax.ShapeDtypeStruct((T, DN_V_W), BF16), jax.ShapeDtypeStruct((1, DN_DV), F32)],
        scratch_shapes=[pltpu.VMEM((H, DN_DK, DN_DV), F32), pltpu.VMEM((HALO, DN_CONV_W), F32),
                        pltpu.VMEM((C, DN_CONV_W), F32), pltpu.VMEM((C, DN_CONV_W), F32)],
        semantics=("arbitrary",),
        args=(act, g, beta, s_saved, tm_saved, vn_saved, u_saved, w_saved, dog, o_raw, pgate, gn,
              pqkv, pqkv, pqkv, conv_w))


def _dn_split_w_in(w):
    return w, jnp.pad(w[:, DN_CONV_W + DN_V_W:], ((0, 0), (0, DN_AB_PAD - 2 * DN_HEADS)))


def _out_proj(og, w_out, x_res, next_g, name):
    if next_g is None:
        return _matmul(og, w_out, "nn", name, add=x_res), None
    return tuple(_matmul(og, w_out, "nn", name, add=x_res, norm_fwd=next_g, tm=NORM_FUSED_TM))


def _dn_layer_fwd(h, wts, conv_w, a_log, dt_bias, gn, w_out, x_res, tag, comm=None, next_g=None):
    w_in, wab = wts
    H = DN_HEADS
    pqkv = _matmul(h, w_in, "nn", tag + "_pqkv", b_cols=(0, DN_CONV_W))
    pgate = _matmul(h, w_in, "nn", tag + "_pgate", b_cols=(DN_CONV_W, DN_V_W))
    pab = _matmul(h, wab, "nn", tag + "_pab")
    a_in, b_in = pab[:, :H], pab[:, H:2 * H]
    g, beta = _dn_gates(a_in, b_in, a_log, dt_bias, tag + "_gates")
    (act, o_raw, og, s_sv, tm_sv, vn_sv, u_sv, w_sv), landed = _dn_chunk_fwd(pqkv, conv_w, g, beta, pgate, gn,
                                                                             tag + "_chunk_fwd", comm)
    if callable(w_out):
        w_out = w_out(landed)
    y = _out_proj(og, w_out, x_res, next_g, tag + "_out")
    saved = dict(h=h, wts=wts, conv_w=conv_w, a_log=a_log, dt_bias=dt_bias, gn=gn, w_out=w_out, pqkv=pqkv, pgate=pgate,
                 a_in=a_in, b_in=b_in, g=g, beta=beta, act=act, o_raw=o_raw, chunk=(s_sv, tm_sv, vn_sv, u_sv, w_sv), og=og)
    return y, saved, landed


def _dn_layer_bwd(dout, sv, tag, norm, comm_of=None, late_comm_of=None):
    w_in, wab = sv["wts"]
    h = sv["h"]
    dog = _matmul(dout, sv["w_out"], "nt", tag + "_dog")
    dw_out = _matmul(sv["og"], dout, "tn", tag + "_dwout", out_dtype=BF16)
    comm = comm_of(dw_out) if comm_of is not None else None
    (dpqkv, dconv, dg, dbeta, dgate, dgn), landed = _dn_chunk_bwd(
        sv["pqkv"], sv["conv_w"], sv["act"], sv["g"], sv["beta"], *sv["chunk"], dog, sv["o_raw"], sv["pgate"], sv["gn"],
        tag + "_chunk_bwd", comm)
    da_in, db_in, da_log, ddt = _dn_gates_bwd(dg, dbeta, sv["a_in"], sv["b_in"], sv["a_log"], sv["dt_bias"],
                                              tag + "_gates_bwd")
    dpab = jnp.pad(jnp.concatenate([da_in, db_in], axis=1), ((0, 0), (0, DN_AB_PAD - 2 * DN_HEADS)))
    dwqkv = _matmul(h, dpqkv, "tn", tag + "_dwqkv", out_dtype=BF16)
    dwgate = _matmul(h, dgate, "tn", tag + "_dwgate", out_dtype=BF16)
    dwab = _matmul(h, dpab, "tn", tag + "_dwab", out_dtype=BF16)
    dw_in = jnp.concatenate([dwqkv, dwgate, dwab[:, :2 * DN_HEADS]], axis=1)
    grads = dict(dn_w_in=dw_in, dn_conv_w=dconv, dn_a_log=da_log, dn_dt_bias=ddt, dn_o_norm_g=dgn, dn_w_out=dw_out)
    dx, landed_late = _matmul_nt_sum([(dpqkv, w_in, 0), (dgate, w_in, DN_CONV_W), (dpab, wab, 0)], tag + "_dh",
                                     late_comm_of(grads) if late_comm_of is not None else None, norm_bwd=norm,
                                     tm=NORM_FUSED_TM if norm is not None else 1024)
    return dx, grads, landed, landed_late


def _sb_layer_fwd(h, w_in, qg, kg, w_out, x_res, tag, comm=None, next_g=None):
    qg2, kg2 = jnp.tile(qg, (1, 2)), jnp.tile(kg, (1, 2))
    proj = _matmul(h, w_in, "nn", tag + "_proj", blocked_b=True)
    qn, kn, vb = _sb_prep(proj, qg2, kg2, tag + "_prep")
    (o, og, ltot, done), landed = _sb_attn_fwd(qn, kn, vb, proj, tag + "_attn_fwd", comm)
    y = _out_proj(og, w_out, x_res, next_g, tag + "_out")
    saved = dict(h=h, w_in=w_in, qg2=qg2, kg2=kg2, w_out=w_out, proj=proj, qn=qn, kn=kn, vb=vb, o=o, og=og, ltot=ltot,
                 done=done)
    return y, saved, landed


def _sb_layer_bwd(dout, sv, tag, comm=None):
    dog = _matmul(dout, sv["w_out"], "nt", tag + "_dog")
    dw_out = _matmul(sv["og"], dout, "tn", tag + "_dwout", out_dtype=BF16)
    (dqn, dkn, dv, dgate), landed = _sb_attn_bwd(sv["qn"], sv["kn"], sv["vb"], dog, sv["o"], sv["ltot"], sv["done"],
                                                 sv["proj"], tag + "_attn_bwd", comm)
    dproj, dqgp, dkgp = _sb_prep_bwd(sv["proj"], dqn, dkn, dv, dgate, sv["qg2"], sv["kg2"], tag + "_prep_bwd")
    dw_in = _matmul(sv["h"], dproj, "tn", tag + "_dwin", out_dtype=BF16, blocked_out=N_DEV)
    dh = _matmul(dproj, sv["w_in"], "nt", tag + "_dh", blocked_b=True)
    dqg = _fold_heads(dqgp, tag + "_dqg")
    dkg = _fold_heads(dkgp, tag + "_dkg")
    return dh, dict(sb_w_in=dw_in, sb_q_norm_g=dqg, sb_k_norm_g=dkg, sb_w_out=dw_out), landed


def _sc_layer_fwd(h, w_in, conv_w, w_out, x_res, tag, next_g=None):
    proj = _matmul(h, w_in, "nn", tag + "_proj", blocked_b=True)
    yg = _sc_fwd(proj, conv_w, tag + "_fwd")
    y = _out_proj(yg, w_out, x_res, next_g, tag + "_out")
    return y, dict(h=h, w_in=w_in, conv_w=conv_w, w_out=w_out, proj=proj, yg=yg)


def _sc_layer_bwd(dout, sv, tag):
    dyg = _matmul(dout, sv["w_out"], "nt", tag + "_dyg")
    dw_out = _matmul(sv["yg"], dout, "tn", tag + "_dwout", out_dtype=BF16)
    dproj, dconv = _sc_bwd(dyg, sv["proj"], sv["conv_w"], tag + "_bwd")
    dw_in = _matmul(sv["h"], dproj, "tn", tag + "_dwin", out_dtype=BF16, blocked_out=N_DEV)
    dh = _matmul(dproj, sv["w_in"], "nt", tag + "_dh", blocked_b=True)
    return dh, dict(sc_w_in=dw_in, sc_conv_w=dconv, sc_w_out=dw_out)


def _adamw(w, m, v, parts, name):
    R, C = w.shape
    tr = _tile(R, 128, SUBLANE)

    def body(w_ref, m_ref, v_ref, p_ref, g_ref, d_ref, nm_ref, nv_ref):
        g = p_ref[0].astype(F32)
        for s in range(1, N_DEV):
            g = g + p_ref[s].astype(F32)
        m2 = ADAM_B1 * m_ref[...] + (1.0 - ADAM_B1) * g
        v2 = ADAM_B2 * v_ref[...] + (1.0 - ADAM_B2) * (g * g)
        m_hat = m2 / (1.0 - ADAM_B1 ** ADAM_STEP)
        v_hat = v2 / (1.0 - ADAM_B2 ** ADAM_STEP)
        g_ref[...] = g
        d_ref[...] = -ADAM_LR * (m_hat / (jnp.sqrt(v_hat) + ADAM_EPS) + ADAM_WD * w_ref[...])
        nm_ref[...] = m2
        nv_ref[...] = v2

    blk = pl.BlockSpec((tr, C), lambda i: (i, 0))
    return pl.pallas_call(
        body, name=name, grid=(R // tr,),
        in_specs=[blk, blk, blk, pl.BlockSpec((N_DEV, tr, C), lambda i: (0, i, 0))],
        out_specs=[blk] * 4, out_shape=[jax.ShapeDtypeStruct((R, C), F32)] * 4,
        compiler_params=_params("parallel"),
    )(w, m, v, parts)


_HBM = pl.BlockSpec(memory_space=pltpu.HBM)
_MESH = pl.DeviceIdType.MESH


def _slot(x, y, c):
    return 4 * x + 2 * y + c


class _Gather:
    def __init__(self, shards):
        self.arrays = list(shards)
        n = len(self.arrays)
        self.out_shapes = [jax.ShapeDtypeStruct((N_DEV,) + s.shape, s.dtype) for s in self.arrays]
        self.scratch = [pltpu.SemaphoreType.DMA((n, N_DEV - 1)), pltpu.SemaphoreType.DMA((n, N_DEV - 1)),
                        pltpu.SemaphoreType.DMA((n,))]

    def _parts(self, ins, outs, sems):
        send_sems, recv_sems, local_sems = sems
        n = len(self.arrays)
        x, y, c = lax.axis_index("x"), lax.axis_index("y"), lax.axis_index("c")
        me, sibling = (x, y, c), (x, y, 1 - c)
        chips = [(1 - x, y), (x, 1 - y), (1 - x, 1 - y)]

        def copy(a, k, block, to, src=None):
            dst = outs[a].at[_slot(*block)]
            return pltpu.make_async_remote_copy(src_ref=dst if src is None else src, dst_ref=dst,
                                                send_sem=send_sems.at[a, k], recv_sem=recv_sems.at[a, k],
                                                device_id=to, device_id_type=_MESH)

        mine = [pltpu.make_async_copy(ins[a], outs[a].at[_slot(*me)], local_sems.at[a]) for a in range(n)]
        first = []
        for a in range(n):
            first.append(copy(a, 0, me, sibling, src=ins[a]))
            first += [copy(a, 1 + j, me, (*chip, c), src=ins[a]) for j, chip in enumerate(chips)]
        return n, c, me, sibling, chips, copy, mine, first

    def start(self, ins, outs, sems):
        _, _, _, _, _, _, mine, first = self._parts(ins, outs, sems)
        for cp in mine + first:
            cp.start()

    def finish(self, ins, outs, sems):
        n, c, me, sibling, chips, copy, mine, first = self._parts(ins, outs, sems)
        passed = []
        for j, chip in enumerate(chips):
            for a in range(n):
                copy(a, 1 + j, (*chip, c), me).wait_recv()
                fwd = copy(a, 4 + j, (*chip, c), sibling)
                fwd.start()
                passed.append(fwd)
        for a in range(n):
            copy(a, 0, sibling, me).wait_recv()
            for j, chip in enumerate(chips):
                copy(a, 4 + j, (*chip, 1 - c), me).wait_recv()
        for cp in first + passed:
            cp.wait_send()
        for cp in mine:
            cp.wait()


class _Exchange:
    def __init__(self, arrays, scatter):
        self.arrays, self.scatter = list(arrays), list(scatter)
        n = len(self.arrays)
        shapes = [a.shape[1:] if s else a.shape for a, s in zip(self.arrays, self.scatter)]
        self.out_shapes = [jax.ShapeDtypeStruct((N_DEV,) + tuple(s), a.dtype) for s, a in zip(shapes, self.arrays)]
        self.scratch = [pltpu.SemaphoreType.DMA((n, N_DEV - 1)), pltpu.SemaphoreType.DMA((n, N_DEV - 1)),
                        pltpu.SemaphoreType.DMA((n,))]

    def _copies(self, ins, outs, sems):
        send_sems, recv_sems, local_sems = sems
        n, scatter = len(self.arrays), self.scatter
        x, y, c = lax.axis_index("x"), lax.axis_index("y"), lax.axis_index("c")
        me = _slot(x, y, c)
        copies = [pltpu.make_async_copy(ins[a].at[me] if scatter[a] else ins[a], outs[a].at[me], local_sems.at[a])
                  for a in range(n)]
        for r in range(1, N_DEV):
            px = 1 - x if r & 4 else x
            py = 1 - y if r & 2 else y
            pc = 1 - c if r & 1 else c
            for a in range(n):
                copies.append(pltpu.make_async_remote_copy(
                    src_ref=ins[a].at[_slot(px, py, pc)] if scatter[a] else ins[a], dst_ref=outs[a].at[me],
                    send_sem=send_sems.at[a, r - 1], recv_sem=recv_sems.at[a, r - 1],
                    device_id=(px, py, pc), device_id_type=_MESH))
        return copies

    def start(self, ins, outs, sems):
        for cp in self._copies(ins, outs, sems):
            cp.start()

    def finish(self, ins, outs, sems):
        for cp in self._copies(ins, outs, sems):
            cp.wait()


def _comm_call(comm, name):
    n = len(comm.arrays)

    def body(*refs):
        ins, outs, sems = refs[:n], refs[n:2 * n], refs[2 * n:]
        comm.start(ins, outs, sems)
        comm.finish(ins, outs, sems)

    return pl.pallas_call(body, name=name, in_specs=[_HBM] * n, out_specs=[_HBM] * n, out_shape=comm.out_shapes,
                          scratch_shapes=comm.scratch)(*comm.arrays)


def _call(body, comm, *, name, grid, in_specs, out_specs, out_shape, scratch_shapes, semantics, args):
    if comm is None:
        outs = pl.pallas_call(body, name=name, grid=grid, in_specs=in_specs, out_specs=out_specs, out_shape=out_shape,
                              scratch_shapes=scratch_shapes, compiler_params=_params(*semantics))(*args)
        return outs, []
    n_in, n_out, n_scr, n_c = len(in_specs), len(out_specs), len(scratch_shapes), len(comm.arrays)

    def fused(*refs):
        ins, refs = refs[:n_in], refs[n_in:]
        c_ins, refs = refs[:n_c], refs[n_c:]
        outs, refs = refs[:n_out], refs[n_out:]
        c_outs, refs = refs[:n_c], refs[n_c:]
        scr, sems = refs[:n_scr], refs[n_scr:]
        ids = [pl.program_id(d) for d in range(len(grid))]
        first = functools.reduce(jnp.logical_and, [i == 0 for i in ids])
        last = functools.reduce(jnp.logical_and, [i == g - 1 for i, g in zip(ids, grid)])

        @pl.when(first)
        def _():
            comm.start(c_ins, c_outs, sems)

        body(*ins, *outs, *scr)

        @pl.when(last)
        def _():
            comm.finish(c_ins, c_outs, sems)

    outs = pl.pallas_call(
        fused, name=name, grid=grid, in_specs=list(in_specs) + [_HBM] * n_c, out_specs=list(out_specs) + [_HBM] * n_c,
        out_shape=list(out_shape) + comm.out_shapes, scratch_shapes=list(scratch_shapes) + comm.scratch,
        compiler_params=_params(*["arbitrary"] * len(grid)))(*args, *comm.arrays)
    return outs[:n_out], outs[n_out:]


_GATHER_0 = (("dn_w_in", 0), ("dn_conv_w", 0), ("dn_o_norm_g", 0))
_GATHER_1 = (("dn_w_out", 0), ("sb_w_in", 0), ("sb_w_out", 0))
_GATHER_2 = (("sc_w_in", 0), ("sc_conv_w", 0), ("sc_w_out", 0), ("dn_w_in", 1), ("dn_conv_w", 1), ("dn_o_norm_g", 1),
             ("dn_w_out", 1))
_EXCHANGE_A = _GATHER_2
_EXCHANGE_B = (("sb_w_in", 0), ("sb_w_out", 0), ("dn_w_out", 0))
_EXCHANGE_C = _GATHER_0
_MATMUL_WEIGHTS = ("dn_w_in", "dn_w_out", "sb_w_in", "sb_w_out", "sc_w_in", "sc_w_out")
_COLUMN_SHARDED = ("dn_w_in", "dn_conv_w", "dn_o_norm_g", "sb_w_in", "sc_w_in", "sc_conv_w")
_BLOCKED = ("sb_w_in", "sc_w_in")
_REPLICATED = ("norm_g", "dn_a_log", "dn_dt_bias", "sb_q_norm_g", "sb_k_norm_g")
_ORDER = ("norm_g", "dn_w_in", "dn_conv_w", "dn_a_log", "dn_dt_bias", "dn_o_norm_g", "dn_w_out", "sb_w_in", "sb_q_norm_g",
          "sb_k_norm_g", "sb_w_out", "sc_w_in", "sc_conv_w", "sc_w_out")
_PACK_COLS = D_MODEL


def _as_2d(a):
    return a.reshape(1, -1) if a.ndim == 1 else a


def _assemble(name, gathered):
    n, r, c = gathered.shape
    if name in _COLUMN_SHARDED:
        return jnp.moveaxis(gathered, 0, 1).reshape(r, n * c)
    return gathered.reshape(n * r, c)


def _disassemble(name, full):
    r, c = full.shape
    if name in _COLUMN_SHARDED:
        return jnp.moveaxis(full.reshape(r, N_DEV, c // N_DEV), 1, 0)
    return full.reshape(N_DEV, r // N_DEV, c)


def _pack_replicated(d):
    rows = [d["norm_g"]]
    for name in _REPLICATED[1:]:
        flat = d[name].reshape(1, -1)
        rows.append(jnp.pad(flat, ((0, 0), (0, _PACK_COLS - flat.shape[1]))))
    return jnp.concatenate(rows, axis=0)


def _unpack_replicated(p, like):
    out = {"norm_g": p[:4]}
    for r, name in enumerate(_REPLICATED[1:]):
        shape = like[name].shape
        out[name] = p[4 + r, :math.prod(shape)].reshape(shape)
    return out


def kernel(x, norm_g, dn_w_in, dn_conv_w, dn_a_log, dn_dt_bias, dn_o_norm_g, dn_w_out, sb_w_in, sb_q_norm_g, sb_k_norm_g, sb_w_out, sc_w_in, sc_conv_w, sc_w_out, loss_target, m_norm_g, m_dn_w_in, m_dn_conv_w, m_dn_a_log, m_dn_dt_bias, m_dn_o_norm_g, m_dn_w_out, m_sb_w_in, m_sb_q_norm_g, m_sb_k_norm_g, m_sb_w_out, m_sc_w_in, m_sc_conv_w, m_sc_w_out, v_norm_g, v_dn_w_in, v_dn_conv_w, v_dn_a_log, v_dn_dt_bias, v_dn_o_norm_g, v_dn_w_out, v_sb_w_in, v_sb_q_norm_g, v_sb_k_norm_g, v_sb_w_out, v_sc_w_in, v_sc_conv_w, v_sc_w_out):
    w = dict(norm_g=norm_g, dn_w_in=dn_w_in, dn_conv_w=dn_conv_w, dn_a_log=dn_a_log, dn_dt_bias=dn_dt_bias,
             dn_o_norm_g=dn_o_norm_g, dn_w_out=dn_w_out, sb_w_in=sb_w_in, sb_q_norm_g=sb_q_norm_g, sb_k_norm_g=sb_k_norm_g,
             sb_w_out=sb_w_out, sc_w_in=sc_w_in, sc_conv_w=sc_conv_w, sc_w_out=sc_w_out)
    m = dict(norm_g=m_norm_g, dn_w_in=m_dn_w_in, dn_conv_w=m_dn_conv_w, dn_a_log=m_dn_a_log, dn_dt_bias=m_dn_dt_bias,
             dn_o_norm_g=m_dn_o_norm_g, dn_w_out=m_dn_w_out, sb_w_in=m_sb_w_in, sb_q_norm_g=m_sb_q_norm_g,
             sb_k_norm_g=m_sb_k_norm_g, sb_w_out=m_sb_w_out, sc_w_in=m_sc_w_in, sc_conv_w=m_sc_conv_w, sc_w_out=m_sc_w_out)
    v = dict(norm_g=v_norm_g, dn_w_in=v_dn_w_in, dn_conv_w=v_dn_conv_w, dn_a_log=v_dn_a_log, dn_dt_bias=v_dn_dt_bias,
             dn_o_norm_g=v_dn_o_norm_g, dn_w_out=v_dn_w_out, sb_w_in=v_sb_w_in, sb_q_norm_g=v_sb_q_norm_g,
             sb_k_norm_g=v_sb_k_norm_g, sb_w_out=v_sb_w_out, sc_w_in=v_sc_w_in, sc_conv_w=v_sc_conv_w, sc_w_out=v_sc_w_out)

    def gather_of(keys):
        return _Gather([_as_2d(w[k][j]).astype(BF16) if k in _MATMUL_WEIGHTS else _as_2d(w[k][j]) for k, j in keys])

    def full_weights(keys, gathered):
        return {key: g if key[0] in _BLOCKED else _assemble(key[0], g) for key, g in zip(keys, gathered)}

    def exchange_of(keys, grads, extra=()):
        out = [grads[k, j] if k in _BLOCKED else
               _disassemble(k, grads[k, j].astype(BF16) if k in _MATMUL_WEIGHTS else grads[k, j]) for k, j in keys]
        return _Exchange(out + list(extra), [True] * len(out) + [False] * len(extra))

    xs, saves = [x[0]], []
    h, got = _rmsnorm_fwd(xs[0], norm_g[0:1], "norm0", gather_of(_GATHER_0))
    F = full_weights(_GATHER_0, got)

    def w_out_0(got):
        F.update(full_weights(_GATHER_1, got))
        return F["dn_w_out", 0]

    (y, h), sv, _ = _dn_layer_fwd(h, _dn_split_w_in(F["dn_w_in", 0]), F["dn_conv_w", 0], dn_a_log[0:1], dn_dt_bias[0:1],
                                  F["dn_o_norm_g", 0], w_out_0, xs[0], "dn0", gather_of(_GATHER_1), norm_g[1:2])
    xs.append(y)
    saves.append(sv)
    (y, h), sv, got = _sb_layer_fwd(h, F["sb_w_in", 0], sb_q_norm_g, sb_k_norm_g, F["sb_w_out", 0], xs[1], "sb",
                                    gather_of(_GATHER_2), norm_g[2:3])
    F.update(full_weights(_GATHER_2, got))
    xs.append(y)
    saves.append(sv)
    (y, h), sv = _sc_layer_fwd(h, F["sc_w_in", 0], F["sc_conv_w", 0], F["sc_w_out", 0], xs[2], "sc", norm_g[3:4])
    xs.append(y)
    saves.append(sv)
    (y, _), sv, _ = _dn_layer_fwd(h, _dn_split_w_in(F["dn_w_in", 1]), F["dn_conv_w", 1], dn_a_log[1:2], dn_dt_bias[1:2],
                                  F["dn_o_norm_g", 1], F["dn_w_out", 1], xs[3], "dn1")
    xs.append(y)
    saves.append(sv)
    dx, loss_part = _loss_head(xs[4], loss_target[0])

    G, dnorm, landed = {}, [None] * 4, {}

    def keep(grads, j):
        G.update({(k, j): g for k, g in grads.items()})

    dh, grads, _, _ = _dn_layer_bwd(dx, saves[3], "dn1", None)
    keep(grads, 1)
    dx, dnorm[3] = _rmsnorm_bwd(dh, xs[3], norm_g[3:4], dx, "norm3_bwd")
    dh, grads = _sc_layer_bwd(dx, saves[2], "sc")
    keep(grads, 0)
    dx, dnorm[2] = _rmsnorm_bwd(dh, xs[2], norm_g[2:3], dx, "norm2_bwd")
    dh, grads, got = _sb_layer_bwd(dx, saves[1], "sb", exchange_of(_EXCHANGE_A, G))
    keep(grads, 0)
    landed.update(zip(_EXCHANGE_A, got))
    dx, dnorm[1] = _rmsnorm_bwd(dh, xs[1], norm_g[1:2], dx, "norm1_bwd")

    def exchange_b(dw_out):
        G["dn_w_out", 0] = dw_out
        return exchange_of(_EXCHANGE_B, G)

    def exchange_c(grads):
        keep(grads, 0)
        return exchange_of(_EXCHANGE_C, G)

    (dx, dnorm[0]), grads, got, got_late = _dn_layer_bwd(dx, saves[0], "dn0", (xs[0], norm_g[0:1], dx), exchange_b, exchange_c)
    landed.update(zip(_EXCHANGE_B, got))
    landed.update(zip(_EXCHANGE_C, got_late))
    replicated = dict(norm_g=jnp.concatenate(dnorm, axis=0),
                      dn_a_log=jnp.concatenate([G["dn_a_log", 0], G["dn_a_log", 1]], axis=0),
                      dn_dt_bias=jnp.concatenate([G["dn_dt_bias", 0], G["dn_dt_bias", 1]], axis=0),
                      sb_q_norm_g=G["sb_q_norm_g", 0], sb_k_norm_g=G["sb_k_norm_g", 0])
    got = _comm_call(_Exchange([_pack_replicated(replicated)], [False]), "exchange_replicated")

    res = {}
    for k in _ORDER:
        if k in _REPLICATED:
            continue
        per_layer = []
        for j in range(w[k].shape[0]):
            shape = w[k][j].shape
            outs = _adamw(_as_2d(w[k][j]), _as_2d(m[k][j]), _as_2d(v[k][j]), landed[k, j], f"adamw_{k}{j}")
            per_layer.append([o.reshape(shape) for o in outs])
        res[k] = [jnp.stack([layer[i] for layer in per_layer], axis=0) for i in range(4)]
    outs = _adamw(_pack_replicated(w), _pack_replicated(m), _pack_replicated(v), got[-1], "adamw_replicated")
    unpacked = [_unpack_replicated(o, w) for o in outs]
    for k in _REPLICATED:
        res[k] = [u[k] for u in unpacked]

    loss = lax.psum(loss_part[0, 0], ("x", "y", "c"))
    return (loss, dx[None]) + tuple(res[k][0] for k in _ORDER) + tuple(res[k][1] for k in _ORDER) \
        + tuple(res[k][2] for k in _ORDER) + tuple(res[k][3] for k in _ORDER)
```

```python
import functools
import itertools
import math

import jax
import jax.numpy as jnp
from jax import lax
from jax.experimental import pallas as pl
from jax.experimental.pallas import tpu as pltpu

F32 = jnp.float32
BF16 = jnp.bfloat16
HIGHEST = lax.Precision.HIGHEST

N_DEV = 8
D_MODEL = 1024
RMS_EPS = 1e-6
L2_EPS = 1e-6

DN_HEADS = 8
DN_DK = 128
DN_DV = 256
DN_QK_W = DN_HEADS * DN_DK
DN_V_W = DN_HEADS * DN_DV
DN_CONV = 4
DN_CHUNK = 64
DN_CONV_W = 2 * DN_QK_W + DN_V_W
DN_IN = DN_CONV_W + DN_V_W + 2 * DN_HEADS
DN_AB_PAD = 128
DN_PREP_BLK = 512

SB_HEADS = 16
SB_DH = 64
SB_W = SB_HEADS * SB_DH
SB_PAIRS = SB_HEADS // 2
SB_TQ = 256
SB_TK = 128
SB_DEAD = -106.0

SC_W = 2 * D_MODEL
SC_CONV = 3
SC_BLK = 512
SC_NBLK = SC_W // SC_BLK

ADAM_LR = 0.001
ADAM_B1 = 0.9
ADAM_B2 = 0.999
ADAM_EPS = 1e-08
ADAM_WD = 0.01
ADAM_STEP = 10

LANE = 128
SUBLANE = 8
HALO = SUBLANE
LONG_ROW_TILE = 512
NORM_FUSED_TM = 512
DEEP_TK = 2048
WIDE_TN = 2048
WIDE_ROW_TILE = 128
VMEM_LIMIT = 48 * 2 ** 20

NN = ((1,), (0,))
NT = ((1,), (1,))
TN = ((0,), (0,))


def _dot(a, b, dims=NN, precision=None):
    return lax.dot_general(a, b, (dims, ((), ())), precision=precision, preferred_element_type=F32)


def _bdot(a, b, dims=NN):
    return _dot(a.astype(BF16), b.astype(BF16), dims)


def _hdot(a, b, dims=NN):
    return _dot(a, b, dims, precision=HIGHEST)


def _tile(dim, pref, align=LANE):
    t = (min(pref, dim) // align) * align
    while t >= align:
        if dim % t == 0:
            return t
        t -= align
    return dim


def _params(*sem):
    return pltpu.CompilerParams(dimension_semantics=sem, vmem_limit_bytes=VMEM_LIMIT)


def _sigmoid(x):
    return 0.5 * jnp.tanh(0.5 * x) + 0.5


def _softplus(x):
    return jnp.maximum(x, 0.0) + jnp.log(1.0 + jnp.exp(-jnp.abs(x)))


def _silu_and_grad(x):
    s = _sigmoid(x)
    return x * s, s * (1.0 + x * (1.0 - s))


def _iota2(shape, dim):
    return lax.broadcasted_iota(jnp.int32, shape, dim)


def _matmul(a, b, mode, name, out_dtype=F32, add=None, b_cols=None, blocked_b=False, blocked_out=0,
            norm_fwd=None, norm_bwd=None, tm=1024, tn=1024, tk=1024):
    b_rows, b_width = (b.shape[1], b.shape[0] * b.shape[2]) if blocked_b else b.shape
    c0, b_used = b_cols if b_cols is not None else (0, b_width)
    if mode == "nn":
        (M, K), (K2, N) = a.shape, (b_rows, b_used)
    elif mode == "nt":
        (M, K), (N, K2) = a.shape, (b_rows, b_used)
    else:
        (K, M), (K2, N) = a.shape, (b_rows, b_used)
    assert K == K2, (a.shape, b.shape, mode)
    if mode == "tn":
        tk = max(tk, DEEP_TK)
    elif norm_fwd is None and norm_bwd is None and add is None:
        tn = max(tn, WIDE_TN)
    tm, tn, tk = _tile(M, tm), _tile(N, tn), _tile(K, tk)
    if blocked_b and mode == "nt":
        tk = b.shape[2]
    elif blocked_b:
        tn = b.shape[2]
    if blocked_out:
        tn = N // blocked_out
    nk = K // tk
    dims = {"nn": NN, "nt": NT, "tn": TN}[mode]
    a_spec = pl.BlockSpec((tk, tm), lambda i, j, k: (k, i)) if mode == "tn" else pl.BlockSpec((tm, tk), lambda i, j, k: (i, k))
    if mode == "nt":
        cb0 = c0 // tk
        assert c0 % tk == 0
        b_spec = (pl.BlockSpec((None, tn, tk), lambda i, j, k: (k + cb0, j, 0)) if blocked_b
                  else pl.BlockSpec((tn, tk), lambda i, j, k: (j, k + cb0)))
    else:
        cb0 = c0 // tn
        assert c0 % tn == 0
        b_spec = (pl.BlockSpec((None, tk, tn), lambda i, j, k: (j + cb0, k, 0)) if blocked_b
                  else pl.BlockSpec((tk, tn), lambda i, j, k: (k, j + cb0)))
    o_spec = pl.BlockSpec((tm, tn), lambda i, j, k: (i, j))
    out_spec = pl.BlockSpec((None, tm, tn), lambda i, j, k: (j, i, 0)) if blocked_out else o_spec
    out_shape = (blocked_out, M, tn) if blocked_out else (M, N)
    has_add = add is not None
    vec_spec = pl.BlockSpec((1, tn), lambda i, j, k: (0, j))
    assert not (norm_fwd is not None or norm_bwd is not None) or tn == N
    extra_in, extra_specs = [], []
    if has_add:
        extra_in, extra_specs = [add], [o_spec]
    if norm_fwd is not None:
        extra_in, extra_specs = extra_in + [norm_fwd], extra_specs + [vec_spec]
        out_specs = [o_spec, o_spec]
        out_shapes = [jax.ShapeDtypeStruct((M, N), out_dtype), jax.ShapeDtypeStruct((M, N), BF16)]
    elif norm_bwd is not None:
        extra_in, extra_specs = extra_in + list(norm_bwd), extra_specs + [o_spec, vec_spec, o_spec]
        out_specs = [o_spec, vec_spec]
        out_shapes = [jax.ShapeDtypeStruct((M, N), F32), jax.ShapeDtypeStruct((1, N), F32)]
    else:
        out_specs, out_shapes = out_spec, jax.ShapeDtypeStruct(out_shape, out_dtype)

    def body(*refs):
        a_ref, b_ref = refs[0], refs[1]
        extra = list(refs[2:2 + len(extra_in)])
        outs = refs[2 + len(extra_in):]
        add_ref = extra.pop(0) if has_add else None
        p = _bdot(a_ref[...], b_ref[...], dims)

        def finish(acc):
            if has_add:
                acc = acc + add_ref[...]
            if norm_bwd is not None:
                _rmsnorm_bwd_tile(acc, *extra, outs[0], outs[1], first=pl.program_id(0) == 0)
                return
            outs[0][...] = acc.astype(out_dtype)
            if norm_fwd is not None:
                r = lax.rsqrt(jnp.mean(acc * acc, axis=-1, keepdims=True) + RMS_EPS)
                outs[1][...] = (acc * r * extra[0][...]).astype(BF16)

        if nk == 1:
            finish(p)
        else:
            acc_ref = refs[-1]
            k = pl.program_id(2)

            @pl.when(k == 0)
            def _():
                acc_ref[...] = p

            @pl.when(k > 0)
            def _():
                acc_ref[...] += p

            @pl.when(k == nk - 1)
            def _():
                finish(acc_ref[...])

    return pl.pallas_call(
        body, name=name, grid=(M // tm, N // tn, nk),
        in_specs=[a_spec, b_spec] + extra_specs, out_specs=out_specs, out_shape=out_shapes,
        scratch_shapes=[pltpu.VMEM((tm, tn), F32)] if nk > 1 else [],
        compiler_params=(_params("arbitrary", "arbitrary", "arbitrary") if norm_bwd is not None
                         else _params("parallel", "parallel", "arbitrary")),
    )(a, b, *extra_in)


def _rmsnorm_bwd_tile(dh, x_ref, g_ref, res_ref, dx_ref, dg_ref, first):
    xv = x_ref[...]
    r = lax.rsqrt(jnp.mean(xv * xv, axis=-1, keepdims=True) + RMS_EPS)
    xh = xv * r
    dxh = dh * g_ref[...]
    m = jnp.mean(dxh * xh, axis=-1, keepdims=True)
    dx_ref[...] = res_ref[...] + r * (dxh - xh * m)
    part = jnp.sum(dh * xh, axis=0, keepdims=True)

    @pl.when(first)
    def _():
        dg_ref[...] = part

    @pl.when(jnp.logical_not(first))
    def _():
        dg_ref[...] += part


def _matmul_nt_sum(pairs, name, comm=None, norm_bwd=None, tm=NORM_FUSED_TM, tk=1024):
    M, N = pairs[0][0].shape[0], pairs[0][1].shape[0]
    tm = _tile(M, tm)
    tks = [_tile(a.shape[1], tk) for a, _, _ in pairs]
    steps = [a.shape[1] // t for (a, _, _), t in zip(pairs, tks)]
    offs = [sum(steps[:p]) for p in range(len(pairs))]
    total = sum(steps)

    n_extra = 3 if norm_bwd is not None else 0

    def body(*refs):
        a_refs, b_refs = refs[0:2 * len(pairs):2], refs[1:2 * len(pairs):2]
        extra = refs[2 * len(pairs):2 * len(pairs) + n_extra]
        outs, acc_ref = refs[2 * len(pairs) + n_extra:-1], refs[-1]
        k = pl.program_id(1)
        for p in range(len(pairs)):
            @pl.when((k >= offs[p]) & (k < offs[p] + steps[p]))
            def _(p=p):
                prod = _bdot(a_refs[p][...], b_refs[p][...], NT)
                if p == 0:
                    @pl.when(k == 0)
                    def _():
                        acc_ref[...] = prod

                    @pl.when(k > 0)
                    def _():
                        acc_ref[...] += prod
                else:
                    acc_ref[...] += prod

        @pl.when(k == total - 1)
        def _():
            if norm_bwd is not None:
                _rmsnorm_bwd_tile(acc_ref[...], *extra, outs[0], outs[1], first=pl.program_id(0) == 0)
            else:
                outs[0][...] = acc_ref[...]

    in_specs, args = [], []
    for (a, b, c0), t, off, n in zip(pairs, tks, offs, steps):
        assert c0 % t == 0
        pick = lambda k, off=off, n=n: jnp.clip(k - off, 0, n - 1)
        in_specs += [pl.BlockSpec((tm, t), lambda i, k, pick=pick: (i, pick(k))),
                     pl.BlockSpec((N, t), lambda i, k, pick=pick, cb0=c0 // t: (0, pick(k) + cb0))]
        args += [a, b]
    row, vec = pl.BlockSpec((tm, N), lambda i, k: (i, 0)), pl.BlockSpec((1, N), lambda i, k: (0, 0))
    if norm_bwd is not None:
        in_specs += [row, vec, row]
        args += list(norm_bwd)
        out_specs, out_shape = [row, vec], [jax.ShapeDtypeStruct((M, N), F32), jax.ShapeDtypeStruct((1, N), F32)]
    else:
        out_specs, out_shape = [row], [jax.ShapeDtypeStruct((M, N), F32)]
    outs, landed = _call(body, comm, name=name, grid=(M // tm, total), in_specs=in_specs, out_specs=out_specs,
                         out_shape=out_shape, scratch_shapes=[pltpu.VMEM((tm, N), F32)],
                         semantics=("arbitrary", "arbitrary"), args=tuple(args))
    return (outs if norm_bwd is not None else outs[0]), landed


def _rmsnorm_fwd(x, g, name, comm=None):
    T, D = x.shape
    tt = _tile(T, LONG_ROW_TILE, SUBLANE)

    def body(x_ref, g_ref, o_ref):
        xv = x_ref[...]
        r = lax.rsqrt(jnp.mean(xv * xv, axis=-1, keepdims=True) + RMS_EPS)
        o_ref[...] = (xv * r * g_ref[...]).astype(BF16)

    outs, landed = _call(
        body, comm, name=name, grid=(T // tt,),
        in_specs=[pl.BlockSpec((tt, D), lambda i: (i, 0)), pl.BlockSpec((1, D), lambda i: (0, 0))],
        out_specs=[pl.BlockSpec((tt, D), lambda i: (i, 0))], out_shape=[jax.ShapeDtypeStruct((T, D), BF16)],
        scratch_shapes=[], semantics=("parallel",), args=(x, g))
    return outs[0], landed


def _rmsnorm_bwd(dh, x, g, dx_res, name):
    T, D = x.shape
    tt = _tile(T, LONG_ROW_TILE // 2, SUBLANE)

    def body(dh_ref, x_ref, g_ref, res_ref, dx_ref, dg_ref):
        _rmsnorm_bwd_tile(dh_ref[...], x_ref, g_ref, res_ref, dx_ref, dg_ref, first=pl.program_id(0) == 0)

    row = pl.BlockSpec((tt, D), lambda i: (i, 0))
    vec = pl.BlockSpec((1, D), lambda i: (0, 0))
    return pl.pallas_call(
        body, name=name, grid=(T // tt,),
        in_specs=[row, row, vec, row], out_specs=[row, vec],
        out_shape=[jax.ShapeDtypeStruct((T, D), F32), jax.ShapeDtypeStruct((1, D), F32)],
        compiler_params=_params("arbitrary"),
    )(dh, x, g, dx_res)


def _loss_head(y, target, name="loss_head"):
    T, D = y.shape
    tt = _tile(T, LONG_ROW_TILE, SUBLANE)

    def body(y_ref, t_ref, dy_ref, l_ref):
        e = y_ref[...] - t_ref[...]
        dy_ref[...] = e * (1.0 / D)
        s = jnp.sum(jnp.sum(e * e, axis=1, keepdims=True), axis=0, keepdims=True) * (0.5 / D)
        s = jnp.broadcast_to(s, (1, LANE))

        @pl.when(pl.program_id(0) == 0)
        def _():
            l_ref[...] = s

        @pl.when(pl.program_id(0) > 0)
        def _():
            l_ref[...] += s

    row = pl.BlockSpec((tt, D), lambda i: (i, 0))
    return pl.pallas_call(
        body, name=name, grid=(T // tt,),
        in_specs=[row, row], out_specs=[row, pl.BlockSpec((1, LANE), lambda i: (0, 0))],
        out_shape=[jax.ShapeDtypeStruct((T, D), F32), jax.ShapeDtypeStruct((1, LANE), F32)],
        compiler_params=_params("arbitrary"),
    )(y, target)


def _down(x, k):
    return pltpu.roll(x, k, 0) if k else x


def _up(x, k):
    return pltpu.roll(x, x.shape[0] - k, 0) if k else x


def _sc_fwd(proj, conv_w, name):
    T = proj.shape[0]
    tt = _tile(T, WIDE_ROW_TILE, SUBLANE)
    B = SC_BLK

    def body(p_ref, ph_ref, w_ref, o_ref):
        keep = (pl.program_id(0) > 0).astype(F32)
        for j in range(SC_NBLK):
            cb, cc, cu, cg = (slice(k * SC_W + j * B, k * SC_W + (j + 1) * B) for k in range(4))
            cw = slice(j * B, (j + 1) * B)
            z = jnp.concatenate([ph_ref[:, cc] * ph_ref[:, cu] * keep, p_ref[:, cc] * p_ref[:, cu]], axis=0)
            cz = (w_ref[2:3, cw] * z + w_ref[1:2, cw] * _down(z, 1) + w_ref[0:1, cw] * _down(z, 2))[HALO:]
            gate = p_ref[:, cg]
            o_ref[:, cw] = (p_ref[:, cb] * cz * (gate * _sigmoid(gate))).astype(BF16)

    return pl.pallas_call(
        body, name=name, grid=(T // tt,),
        in_specs=[pl.BlockSpec((tt, 4 * SC_W), lambda i: (i, 0)),
                  pl.BlockSpec((HALO, 4 * SC_W), lambda i: (jnp.maximum(i * (tt // HALO) - 1, 0), 0)),
                  pl.BlockSpec((SC_CONV, SC_W), lambda i: (0, 0))],
        out_specs=pl.BlockSpec((tt, SC_W), lambda i: (i, 0)),
        out_shape=jax.ShapeDtypeStruct((T, SC_W), BF16),
        compiler_params=_params("parallel"),
    )(proj, proj, conv_w)


def _sc_bwd(dyg, proj, conv_w, name):
    T = proj.shape[0]
    tt = _tile(T, WIDE_ROW_TILE, SUBLANE)
    nt = T // tt
    B = SC_BLK
    hb = tt // HALO

    def body(d_ref, dn_ref, p_ref, pp_ref, pn_ref, w_ref, o_ref, dw_ref):
        i = pl.program_id(0)
        keep_p = (i > 0).astype(F32)
        keep_n = (i < nt - 1).astype(F32)
        main = slice(HALO, HALO + tt)
        parts = []
        for j in range(SC_NBLK):
            cw = slice(j * B, (j + 1) * B)

            def ext(k):
                s = slice(k * SC_W + j * B, k * SC_W + (j + 1) * B)
                return s, jnp.concatenate([pp_ref[:, s] * keep_p, p_ref[:, s], pn_ref[:, s]], axis=0)

            (sb, b), (sc, c), (su, u), (sg_, gate) = ext(0), ext(1), ext(2), ext(3)
            dyg_e = jnp.concatenate([jnp.zeros((HALO, B), F32), d_ref[:, cw], dn_ref[:, cw] * keep_n], axis=0)
            w0, w1, w2 = w_ref[0:1, cw], w_ref[1:2, cw], w_ref[2:3, cw]
            z = c * u
            z1, z2 = _down(z, 1), _down(z, 2)
            cz = w2 * z + w1 * z1 + w0 * z2
            sg, dsg = _silu_and_grad(gate)
            dy = dyg_e * sg
            dcz = dy * b
            dz = w2 * dcz + w1 * _up(dcz, 1) + w0 * _up(dcz, 2)
            o_ref[:, sb] = (dy * cz)[main].astype(BF16)
            o_ref[:, sc] = (dz * u)[main].astype(BF16)
            o_ref[:, su] = (dz * c)[main].astype(BF16)
            o_ref[:, sg_] = (dyg_e * (b * cz) * dsg)[main].astype(BF16)
            dcm = dcz[main]
            parts.append(jnp.concatenate([jnp.sum(dcm * z2[main], axis=0, keepdims=True),
                                          jnp.sum(dcm * z1[main], axis=0, keepdims=True),
                                          jnp.sum(dcm * z[main], axis=0, keepdims=True)], axis=0))
        part = jnp.concatenate(parts, axis=1)

        @pl.when(i == 0)
        def _():
            dw_ref[...] = part

        @pl.when(i > 0)
        def _():
            dw_ref[...] += part

    nxt = lambda i: (jnp.minimum((i + 1) * hb, nt * hb - 1), 0)
    return pl.pallas_call(
        body, name=name, grid=(nt,),
        in_specs=[pl.BlockSpec((tt, SC_W), lambda i: (i, 0)),
                  pl.BlockSpec((HALO, SC_W), nxt),
                  pl.BlockSpec((tt, 4 * SC_W), lambda i: (i, 0)),
                  pl.BlockSpec((HALO, 4 * SC_W), lambda i: (jnp.maximum(i * hb - 1, 0), 0)),
                  pl.BlockSpec((HALO, 4 * SC_W), nxt),
                  pl.BlockSpec((SC_CONV, SC_W), lambda i: (0, 0))],
        out_specs=[pl.BlockSpec((tt, 4 * SC_W), lambda i: (i, 0)), pl.BlockSpec((SC_CONV, SC_W), lambda i: (0, 0))],
        out_shape=[jax.ShapeDtypeStruct((T, 4 * SC_W), BF16), jax.ShapeDtypeStruct((SC_CONV, SC_W), F32)],
        compiler_params=_params("arbitrary"),
    )(dyg, dyg, proj, proj, proj, conv_w)


def _split3_dot(x, m):
    hi = x.astype(BF16)
    r1 = x - hi.astype(F32)
    mid = r1.astype(BF16)
    lo = (r1 - mid.astype(F32)).astype(BF16)
    return _dot(hi, m) + _dot(mid, m) + _dot(lo, m)


def _split2_dot(x, m):
    hi = x.astype(BF16)
    lo = (x - hi.astype(F32)).astype(BF16)
    return _dot(hi, m) + _dot(lo, m)


def _head_mean_matrix():
    r, c = _iota2((LANE, LANE), 0), _iota2((LANE, LANE), 1)
    return jnp.where((r // SB_DH) == (c // SB_DH), 1.0 / SB_DH, 0.0).astype(BF16)


def _sb_prep(proj, qg2, kg2, name):
    T = proj.shape[0]
    tt = _tile(T, WIDE_ROW_TILE, SUBLANE)

    def body(p_ref, qg_ref, kg_ref, q_ref, k_ref, v_ref):
        bd = _head_mean_matrix()

        def norm(x, g, scale):
            r = lax.rsqrt(_split3_dot(x * x, bd) + RMS_EPS)
            return (x * r * g * scale).astype(BF16)

        v_ref[...] = p_ref[:, 2 * SB_W:3 * SB_W].astype(BF16)
        for p in range(SB_PAIRS):
            cols = slice(p * LANE, (p + 1) * LANE)
            q_ref[:, cols] = norm(p_ref[:, cols], qg_ref[...], SB_DH ** -0.5)
            k_ref[:, cols] = norm(p_ref[:, SB_W + p * LANE:SB_W + (p + 1) * LANE], kg_ref[...], 1.0)

    blk = pl.BlockSpec((tt, SB_W), lambda i: (i, 0))
    vec = pl.BlockSpec((1, LANE), lambda i: (0, 0))
    return pl.pallas_call(
        body, name=name, grid=(T // tt,),
        in_specs=[pl.BlockSpec((tt, 4 * SB_W), lambda i: (i, 0)), vec, vec],
        out_specs=[blk, blk, blk],
        out_shape=[jax.ShapeDtypeStruct((T, SB_W), BF16)] * 3,
        compiler_params=_params("parallel"),
    )(proj, qg2, kg2)


def _sb_prep_bwd(proj, dqn, dkn, dv, dgate, qg2, kg2, name):
    T = proj.shape[0]
    tt = _tile(T, WIDE_ROW_TILE, SUBLANE)

    def body(p_ref, dq_ref, dk_ref, dv_ref, dg_ref, qg_ref, kg_ref, o_ref, dqg_ref, dkg_ref):
        i = pl.program_id(0)
        bd = _head_mean_matrix()

        def norm_bwd(x, g, dy):
            r = lax.rsqrt(_split3_dot(x * x, bd) + RMS_EPS)
            xh = x * r
            dxh = dy * g
            m = _split3_dot(dxh * xh, bd)
            return r * (dxh - xh * m), jnp.sum(dy * xh, axis=0, keepdims=True)

        o_ref[:, 2 * SB_W:3 * SB_W] = dv_ref[...].astype(BF16)
        o_ref[:, 3 * SB_W:4 * SB_W] = dg_ref[...].astype(BF16)
        pq = pk = jnp.zeros((1, LANE), F32)
        for p in range(SB_PAIRS):
            cols, kcols = slice(p * LANE, (p + 1) * LANE), slice(SB_W + p * LANE, SB_W + (p + 1) * LANE)
            dxq, sq = norm_bwd(p_ref[:, cols], qg_ref[...], dq_ref[:, cols])
            dxk, sk = norm_bwd(p_ref[:, kcols], kg_ref[...], dk_ref[:, cols])
            o_ref[:, cols] = dxq.astype(BF16)
            o_ref[:, kcols] = dxk.astype(BF16)
            pq, pk = pq + sq, pk + sk

        @pl.when(i == 0)
        def _():
            dqg_ref[...] = pq
            dkg_ref[...] = pk

        @pl.when(i > 0)
        def _():
            dqg_ref[...] += pq
            dkg_ref[...] += pk

    blk = pl.BlockSpec((tt, SB_W), lambda i: (i, 0))
    vec = pl.BlockSpec((1, LANE), lambda i: (0, 0))
    wide = pl.BlockSpec((tt, 4 * SB_W), lambda i: (i, 0))
    return pl.pallas_call(
        body, name=name, grid=(T // tt,),
        in_specs=[wide, blk, blk, blk, blk, vec, vec],
        out_specs=[wide, vec, vec],
        out_shape=[jax.ShapeDtypeStruct((T, 4 * SB_W), BF16)] + [jax.ShapeDtypeStruct((1, LANE), F32)] * 2,
        compiler_params=_params("arbitrary"),
    )(proj, dqn, dkn, dv, dgate, qg2, kg2)


def _fold_heads(part, name):
    def body(p_ref, o_ref):
        r, c = _iota2((LANE, SB_DH), 0), _iota2((LANE, SB_DH), 1)
        fold = jnp.where((r % SB_DH) == c, 1.0, 0.0).astype(F32)
        o_ref[...] = jnp.sum(_hdot(p_ref[...], fold), axis=0, keepdims=True)

    return pl.pallas_call(body, name=name, out_shape=jax.ShapeDtypeStruct((1, SB_DH), F32))(part)


def _sb_masks():
    lane = _iota2((1, LANE), 1)
    return lane < SB_DH


def _sb_attn_fwd(qn, kn, vb, proj, name, comm=None):
    T = qn.shape[0]
    tq, tk = _tile(T, SB_TQ, SUBLANE), SB_TK
    assert tq % tk == 0

    def body(q_ref, k_ref, v_ref, g_ref, o_ref, og_ref, lt_ref, done_ref):
        i = pl.program_id(1)
        ma = _sb_masks()
        q2 = q_ref[...]
        zero = jnp.zeros_like(q2)
        qs = (jnp.where(ma, q2, zero), jnp.where(ma, zero, q2))
        upper = (_iota2((tk, tk), 0) > _iota2((tk, tk), 1)).astype(BF16)
        qpos = i * tq + _iota2((tq, tk), 0)
        nb = tq // tk

        def trip(kb_top, masked, carry):
            acc, la, lb = carry
            chains = [(b, h) for b in range(nb) for h in range(2)]
            k2s, vss, masks = [], [], []
            for b in range(nb):
                kb = kb_top - b
                rows = pl.ds(pl.multiple_of(kb * tk, tk), tk)
                k2s.append(k_ref[rows, :])
                v2 = v_ref[rows, :]
                zv = jnp.zeros_like(v2)
                vss.append((jnp.where(ma, v2, zv), jnp.where(ma, zv, v2)))
                masks.append((kb * tk + _iota2((tq, tk), 1)) < qpos if masked else None)
            zs = [_dot(qs[h], k2s[b], NT) for b, h in chains]
            ts = [jnp.log(1.0 + jnp.exp(-jnp.abs(z))) for z in zs]
            ls = [-(jnp.maximum(z, 0.0) + t) for z, t in zip(zs, ts)]
            if masked:
                ls = [jnp.where(masks[b], l, 0.0) for (b, h), l in zip(chains, ls)]
            cums = [_split2_dot(l, upper) for l in ls]
            sums = [jnp.sum(l, axis=1, keepdims=True) for l in ls]
            offs, tot = {}, [la, lb]
            for b in range(nb):
                for h in range(2):
                    offs[(b, h)] = tot[h]
                    tot[h] = tot[h] + sums[chains.index((b, h))]
            ws = [jnp.exp(jnp.minimum(z, 0.0) - t + c + offs[ch]) for ch, z, t, c in zip(chains, zs, ts, cums)]
            if masked:
                ws = [jnp.where(masks[b], w, 0.0) for (b, h), w in zip(chains, ws)]
            for (b, h), w in zip(chains, ws):
                acc = acc + _dot(w.astype(BF16), vss[b][h])
            return acc, tot[0], tot[1]

        def largest(la, lb):
            return jnp.max(jnp.maximum(la, lb))

        z1 = jnp.zeros((tq, 1), F32)
        acc, la, lb = trip((i + 1) * nb - 1, True, (jnp.zeros((tq, LANE), F32), z1, z1))

        def live(c):
            return (c[0] < i) & (c[4] > SB_DEAD)

        def more(c):
            j, acc, la, lb, _ = c
            acc, la, lb = trip((i - j) * nb - 1, False, (acc, la, lb))
            return j + 1, acc, la, lb, largest(la, lb)

        done, acc, la, lb, _ = lax.while_loop(live, more, (jnp.int32(0), acc, la, lb, largest(la, lb)))
        gate = g_ref[...]
        o_ref[...] = acc
        og_ref[...] = (acc * (gate * _sigmoid(gate))).astype(BF16)
        lt_ref[...] = jnp.where(_iota2((tq, 2), 1) == 0, la, lb)
        done_ref[...] = jnp.full((SUBLANE, LANE), done, F32)

    nq = T // tq
    qblk = pl.BlockSpec((tq, LANE), lambda p, i: (i, p))
    full = pl.BlockSpec((T, LANE), lambda p, i: (0, p))
    return _call(
        body, comm, name=name, grid=(SB_PAIRS, nq),
        in_specs=[qblk, full, full, pl.BlockSpec((tq, LANE), lambda p, i: (i, 3 * SB_PAIRS + p))],
        out_specs=[qblk, qblk, pl.BlockSpec((None, tq, 2), lambda p, i: (p, i, 0)),
                   pl.BlockSpec((None, None, SUBLANE, LANE), lambda p, i: (p, i, 0, 0))],
        out_shape=[jax.ShapeDtypeStruct((T, SB_W), F32), jax.ShapeDtypeStruct((T, SB_W), BF16),
                   jax.ShapeDtypeStruct((SB_PAIRS, T, 2), F32), jax.ShapeDtypeStruct((SB_PAIRS, nq, SUBLANE, LANE), F32)],
        scratch_shapes=[], semantics=("parallel", "parallel"), args=(qn, kn, vb, proj))


def _sb_attn_bwd(qn, kn, vb, dog, o, ltot, done, proj, name, comm=None):
    T = qn.shape[0]
    tq, tk = _tile(T, SB_TQ, SUBLANE), SB_TK

    def body(q_ref, k_ref, v_ref, dog_ref, o_ref, lt_ref, done_ref, g_ref, dq_ref, dk_ref, dv_ref, dgate_ref):
        i = pl.program_id(1)
        first_trip = i - jnp.max(done_ref[...]).astype(jnp.int32)

        @pl.when(i == 0)
        def _():
            dk_ref[...] = jnp.zeros_like(dk_ref)
            dv_ref[...] = jnp.zeros_like(dv_ref)

        ma = _sb_masks()
        gate, o2, dog2 = g_ref[...], o_ref[...], dog_ref[...]
        sg, dsg = _silu_and_grad(gate)
        do2 = dog2 * sg
        dgate_ref[...] = dog2 * o2 * dsg
        lt = lt_ref[...]
        first = _iota2((tq, 2), 1) == 0
        ltots = (jnp.sum(jnp.where(first, lt, 0.0), axis=1, keepdims=True),
                 jnp.sum(jnp.where(first, 0.0, lt), axis=1, keepdims=True))
        q2 = q_ref[...]
        zq = jnp.zeros_like(q2)
        qs = (jnp.where(ma, q2, zq), jnp.where(ma, zq, q2))
        dob = do2.astype(BF16)
        dos = (jnp.where(ma, dob, zq), jnp.where(ma, zq, dob))
        upto = (_iota2((tk, tk), 0) <= _iota2((tk, tk), 1)).astype(BF16)
        before = (_iota2((tk, tk), 0) < _iota2((tk, tk), 1)).astype(BF16)
        qpos = i * tq + _iota2((tq, tk), 0)
        nb = tq // tk

        def trip(kb_bot, masked, carry):
            dq, la, lb, ea, eb = carry
            chains = [(b, h) for b in range(nb) for h in range(2)]
            rows, k2s, v2s, kss, masks = [], [], [], [], []
            for b in range(nb):
                kb = kb_bot + b
                rows.append(pl.ds(pl.multiple_of(kb * tk, tk), tk))
                k2 = k_ref[rows[b], :]
                zk = jnp.zeros_like(k2)
                k2s.append(k2)
                v2s.append(v_ref[rows[b], :])
                kss.append((jnp.where(ma, k2, zk), jnp.where(ma, zk, k2)))
                masks.append((kb * tk + _iota2((tq, tk), 1)) < qpos if masked else None)

            def keep(vals):
                return [jnp.where(masks[b], x, 0.0) for (b, h), x in zip(chains, vals)] if masked else vals

            zs = [_dot(qs[h], k2s[b], NT) for b, h in chains]
            dws = [_dot(dos[h], v2s[b], NT) for b, h in chains]
            ts = [jnp.log(1.0 + jnp.exp(-jnp.abs(z))) for z in zs]
            ls = keep([-(jnp.maximum(z, 0.0) + t) for z, t in zip(zs, ts)])
            lps = [jnp.minimum(z, 0.0) - t for z, t in zip(zs, ts)]
            cums = [_split3_dot(l, upto) for l in ls]
            lsums = [jnp.sum(l, axis=1, keepdims=True) for l in ls]
            offs, tot = {}, [la, lb]
            for b in range(nb):
                for h in range(2):
                    offs[(b, h)] = tot[h]
                    tot[h] = tot[h] + lsums[chains.index((b, h))]
            ws = keep([jnp.exp(lp + (ltots[h] - (offs[(b, h)] + c))) for (b, h), lp, c in zip(chains, lps, cums)])
            es = [dw * w for dw, w in zip(dws, ws)]
            ecums = [_split2_dot(e, before) for e in es]
            esums = [jnp.sum(e, axis=1, keepdims=True) for e in es]
            eoffs, etot = {}, [ea, eb]
            for b in range(nb):
                for h in range(2):
                    eoffs[(b, h)] = etot[h]
                    etot[h] = etot[h] + esums[chains.index((b, h))]
            dzs = keep([e - jnp.exp(lp) * (e + eoffs[ch] + ec) for ch, e, lp, ec in zip(chains, es, lps, ecums)])
            dzs = [dz.astype(BF16) for dz in dzs]
            wbs = [w.astype(BF16) for w in ws]
            for (b, h), dz in zip(chains, dzs):
                dq = dq + _dot(dz, kss[b][h])
            for b in range(nb):
                ia, ib = chains.index((b, 0)), chains.index((b, 1))
                dk_ref[rows[b], :] += _dot(dzs[ia], qs[0], TN) + _dot(dzs[ib], qs[1], TN)
                dv_ref[rows[b], :] += _dot(wbs[ia], dos[0], TN) + _dot(wbs[ib], dos[1], TN)
            return dq, tot[0], tot[1], etot[0], etot[1]

        z1 = jnp.zeros((tq, 1), F32)
        carry = lax.fori_loop(first_trip, i, lambda j, c: trip(j * nb, False, c),
                              (jnp.zeros((tq, LANE), F32), z1, z1, z1, z1))
        dq = trip(i * nb, True, carry)[0]
        dq_ref[...] = dq * (SB_DH ** -0.5)

    qblk = pl.BlockSpec((tq, LANE), lambda p, i: (i, p))
    full = pl.BlockSpec((T, LANE), lambda p, i: (0, p))
    return _call(
        body, comm, name=name, grid=(SB_PAIRS, T // tq),
        in_specs=[qblk, full, full, qblk, qblk, pl.BlockSpec((None, tq, 2), lambda p, i: (p, i, 0)),
                  pl.BlockSpec((None, None, SUBLANE, LANE), lambda p, i: (p, i, 0, 0)),
                  pl.BlockSpec((tq, LANE), lambda p, i: (i, 3 * SB_PAIRS + p))],
        out_specs=[qblk, full, full, qblk],
        out_shape=[jax.ShapeDtypeStruct((T, SB_W), F32)] * 4,
        scratch_shapes=[], semantics=("parallel", "arbitrary"), args=(qn, kn, vb, dog, o, ltot, done, proj))


def _dn_conv(ext, w_ref, cw):
    return (w_ref[3:4, cw] * ext + w_ref[2:3, cw] * _down(ext, 1) + w_ref[1:2, cw] * _down(ext, 2)
            + w_ref[0:1, cw] * _down(ext, 3))


def _dn_gates(a_in, b_in, a_log, dt_bias, name):
    T, H = a_in.shape
    C = DN_CHUNK

    def body(a_ref, b_ref, al_ref, dt_ref, g_ref, beta_ref):
        beta_ref[...] = _sigmoid(b_ref[...])
        g_ref[...] = -jnp.exp(al_ref[...]) * _softplus(a_ref[...] + dt_ref[...])
        tri = (_iota2((C, C), 0) >= _iota2((C, C), 1)).astype(F32)

        def chunk(n, carry):
            rows = pl.ds(pl.multiple_of(n * C, C), C)
            g_ref[rows, :] = _hdot(tri, g_ref[rows, :])
            return carry

        lax.fori_loop(0, T // C, chunk, 0)

    return pl.pallas_call(body, name=name, out_shape=[jax.ShapeDtypeStruct((T, H), F32)] * 2)(a_in, b_in, a_log, dt_bias)


def _dn_gates_bwd(dg, dbeta, a_in, b_in, a_log, dt_bias, name):
    T, H = a_in.shape
    C = DN_CHUNK

    def body(dg_ref, db_ref, a_ref, b_ref, al_ref, dt_ref, da_ref, dbi_ref, dal_ref, ddt_ref):
        tri_t = (_iota2((C, C), 0) <= _iota2((C, C), 1)).astype(F32)

        def chunk(n, carry):
            rows = pl.ds(pl.multiple_of(n * C, C), C)
            da_ref[rows, :] = _hdot(tri_t, dg_ref[rows, :])
            return carry

        lax.fori_loop(0, T // C, chunk, 0)
        dla = da_ref[...]
        x = a_ref[...] + dt_ref[...]
        ea = jnp.exp(al_ref[...])
        da = dla * (-ea) * _sigmoid(x)
        da_ref[...] = da
        dal_ref[...] = jnp.sum(dla * (-ea * _softplus(x)), axis=0, keepdims=True)
        ddt_ref[...] = jnp.sum(da, axis=0, keepdims=True)
        beta = _sigmoid(b_ref[...])
        dbi_ref[...] = db_ref[...] * beta * (1.0 - beta)

    return pl.pallas_call(
        body, name=name,
        out_shape=[jax.ShapeDtypeStruct((T, H), F32)] * 2 + [jax.ShapeDtypeStruct((1, H), F32)] * 2,
    )(dg, dbeta, a_in, b_in, a_log, dt_bias)


def _dn_chunk_terms(q, k, gc, bc):
    C = DN_CHUNK
    r, c = _iota2((C, C), 0), _iota2((C, C), 1)
    lower, strict, eye = r >= c, r > c, r == c
    grow = jnp.sum(jnp.where(eye, gc, 0.0), axis=0, keepdims=True)
    decay = jnp.where(lower, jnp.exp(jnp.where(lower, gc - grow, 0.0)), 0.0)
    last = _iota2((C, 1), 0) == C - 1
    gl = jnp.sum(jnp.where(last, gc, 0.0), axis=0, keepdims=True)
    eg = jnp.exp(gc)
    egl = jnp.exp(gl - gc)
    kb = k * bc
    lmat = jnp.where(strict, _bdot(kb, k, NT) * decay, 0.0)
    aqk = jnp.where(lower, _bdot(q, k, NT) * decay, 0.0)
    return dict(lower=lower, strict=strict, eye=eye, last=last, decay=decay, gl=gl, eg=eg, egl=egl, kb=kb,
                lmat=lmat, aqk=aqk, qd=q * eg, kd=k * egl)


def _split(x):
    hi = x.astype(BF16)
    return hi, (x - hi.astype(F32)).astype(BF16)


def _x3dot(a, b, dims=NN):
    ah, al = a if isinstance(a, tuple) else _split(a)
    bh, bl = b if isinstance(b, tuple) else _split(b)
    return _dot(ah, bh, dims) + (_dot(ah, bl, dims) + _dot(al, bh, dims))


def _interleave(gens):
    for _ in itertools.zip_longest(*gens):
        pass


def _unit_lower_inverse_steps(lmat, eye, out):
    ident = jnp.where(eye, 1.0, 0.0).astype(F32)
    m = -lmat
    inv = ident + m
    for _ in range(int(math.log2(DN_CHUNK)) - 1):
        ms = _split(m)
        m = _x3dot(ms, ms)
        yield
        inv = inv + _x3dot(inv, m)
        yield
    out["tm"] = inv


def _dn_chunk_fwd(pqkv, conv_w, g, beta, pgate, gn, name, comm=None):
    T = pqkv.shape[0]
    C, H = DN_CHUNK, DN_HEADS
    N = T // C
    B = DN_PREP_BLK
    nq, nqk = DN_QK_W // B, 2 * DN_QK_W // B

    def step(p_ref, cw_ref, g_ref, b_ref, pg_ref, gn_ref, act_out, o_ref, og_ref, s_out, t_out, vn_out, u_out, w_out,
             s_scr, tail_scr, a_ref, a_next):
        head_lane = _iota2((C, H), 1)

        def prepare(cb):
            cw = slice(cb * B, (cb + 1) * B)
            ext = jnp.concatenate([tail_scr[:, cw], p_ref[:, cw]], axis=0)
            c = _dn_conv(ext, cw_ref, cw)[HALO:]
            yield
            a = c * _sigmoid(c)
            if cb >= nqk:
                a_next[:, cw] = a
                act_out[:, cw] = a
                return
            scale = DN_DK ** -0.5 if cb < nq else 1.0
            for hh in range(B // DN_DK):
                yield
                ah = a[:, hh * DN_DK:(hh + 1) * DN_DK]
                val = ah * (lax.rsqrt(jnp.sum(ah * ah, axis=-1, keepdims=True) + L2_EPS) * scale)
                cols = slice(cb * B + hh * DN_DK, cb * B + (hh + 1) * DN_DK)
                a_next[:, cols] = val
                act_out[:, cols] = val

        def head(hh):
            qs, vs = slice(hh * DN_DK, (hh + 1) * DN_DK), slice(hh * DN_DV, (hh + 1) * DN_DV)
            q, k, v = a_ref[:, qs], a_ref[:, DN_QK_W + hh * DN_DK:DN_QK_W + (hh + 1) * DN_DK], \
                a_ref[:, 2 * DN_QK_W + hh * DN_DV:2 * DN_QK_W + (hh + 1) * DN_DV]
            gc = jnp.sum(jnp.where(head_lane == hh, g_ref[...], 0.0), axis=1, keepdims=True)
            bc = jnp.sum(jnp.where(head_lane == hh, b_ref[...], 0.0), axis=1, keepdims=True)
            t = _dn_chunk_terms(q, k, gc, bc)
            yield
            res = {}
            yield from _unit_lower_inverse_steps(t["lmat"], t["eye"], res)
            tms = _split(res["tm"])
            u = _x3dot(tms, v * bc)
            yield
            w = _x3dot(tms, t["kb"] * t["eg"])
            yield
            s = s_scr[hh]
            s_out[hh] = s
            t_out[hh] = res["tm"]
            sb = s.astype(BF16)
            vn = u - _dot(w.astype(BF16), sb)
            yield
            o = _dot(t["qd"].astype(BF16), sb) + _bdot(t["aqk"], vn)
            yield
            s_scr[hh] = s * jnp.exp(t["gl"]) + _bdot(t["kd"], vn, TN)
            vn_out[:, vs] = vn
            u_out[:, vs] = u
            w_out[:, qs] = w
            o_ref[:, vs] = o
            gate = pg_ref[:, vs]
            r = lax.rsqrt(jnp.mean(o * o, axis=-1, keepdims=True) + RMS_EPS)
            og_ref[:, vs] = (o * r * gn_ref[...] * (gate * _sigmoid(gate))).astype(BF16)

        _interleave([prepare(cb) for cb in range(DN_CONV_W // B)] + [head(hh) for hh in range(H)])

        @pl.when(pl.program_id(0) < N - 1)
        def _():
            tail_scr[...] = p_ref[C - HALO:C, :]

    def body(*refs):
        s = pl.program_id(0)
        io, (s_scr, tail_scr, buf_a, buf_b) = refs[:-4], refs[-4:]

        @pl.when(s == 0)
        def _():
            tail_scr[...] = jnp.zeros_like(tail_scr)
            buf_b[...] = jnp.zeros_like(buf_b)

        @pl.when(s <= 1)
        def _():
            s_scr[...] = jnp.zeros_like(s_scr)

        @pl.when(s % 2 == 0)
        def _():
            step(*io, s_scr, tail_scr, buf_b, buf_a)

        @pl.when(s % 2 == 1)
        def _():
            step(*io, s_scr, tail_scr, buf_a, buf_b)

    nxt = lambda w: pl.BlockSpec((C, w), lambda s: (jnp.minimum(s, N - 1), 0))
    cur = lambda w: pl.BlockSpec((C, w), lambda s: (jnp.maximum(s - 1, 0), 0))
    per_chunk = lambda a, b: pl.BlockSpec((H, None, a, b), lambda s: (0, jnp.maximum(s - 1, 0), 0, 0))
    return _call(
        body, comm, name=name, grid=(N + 1,),
        in_specs=[nxt(DN_CONV_W), pl.BlockSpec((DN_CONV, DN_CONV_W), lambda s: (0, 0)), cur(H), cur(H), cur(DN_V_W),
                  pl.BlockSpec((1, DN_DV), lambda s: (0, 0))],
        out_specs=[nxt(DN_CONV_W), cur(DN_V_W), cur(DN_V_W), per_chunk(DN_DK, DN_DV), per_chunk(C, C),
                   cur(DN_V_W), cur(DN_V_W), cur(DN_QK_W)],
        out_shape=[jax.ShapeDtypeStruct((T, DN_CONV_W), F32),
                   jax.ShapeDtypeStruct((T, DN_V_W), F32), jax.ShapeDtypeStruct((T, DN_V_W), BF16),
                   jax.ShapeDtypeStruct((H, N, DN_DK, DN_DV), F32),
                   jax.ShapeDtypeStruct((H, N, C, C), F32),
                   jax.ShapeDtypeStruct((T, DN_V_W), F32),
                   jax.ShapeDtypeStruct((T, DN_V_W), F32),
                   jax.ShapeDtypeStruct((T, DN_QK_W), F32)],
        scratch_shapes=[pltpu.VMEM((H, DN_DK, DN_DV), F32), pltpu.VMEM((HALO, DN_CONV_W), F32),
                        pltpu.VMEM((C, DN_CONV_W), F32), pltpu.VMEM((C, DN_CONV_W), F32)],
        semantics=("arbitrary",), args=(pqkv, conv_w, g, beta, pgate, gn))


def _dn_chunk_bwd(pqkv, conv_w, act, g, beta, s_saved, tm_saved, vn_saved, u_saved, w_saved, dog, o_raw, pgate, gn,
                  name, comm=None):
    T = act.shape[0]
    C, H = DN_CHUNK, DN_HEADS
    N = T // C
    assert N % 2 == 0
    B = DN_PREP_BLK
    nq, nqk = DN_QK_W // B, 2 * DN_QK_W // B
    main = slice(HALO, HALO + C)

    def prepare_bwd(cb, p_ref, pp_ref, pn_ref, cw_ref, dread, dnext_scr, dp_ref, conv_parts):
        s = pl.program_id(0)
        keep_p = (N - s > 0).astype(F32)
        keep_n = (s > 1).astype(F32)
        cw = slice(cb * B, (cb + 1) * B)
        ext = jnp.concatenate([pp_ref[:, cw] * keep_p, p_ref[:, cw], pn_ref[:, cw]], axis=0)
        c = _dn_conv(ext, cw_ref, cw)
        yield
        sg = _sigmoid(c)
        da_dc = sg * (1.0 + c * (1.0 - sg))
        d_up = jnp.concatenate([jnp.zeros((HALO, B), F32), dread[:, cw], dnext_scr[:, cw] * keep_n], axis=0)
        if cb < nqk:
            a = c * sg
            scale = DN_DK ** -0.5 if cb < nq else 1.0
            normed = []
            for hh in range(B // DN_DK):
                yield
                cols = slice(hh * DN_DK, (hh + 1) * DN_DK)
                ah = a[:, cols]
                r = lax.rsqrt(jnp.sum(ah * ah, axis=-1, keepdims=True) + L2_EPS)
                y = ah * r
                dy = d_up[:, cols] * scale
                normed.append(r * (dy - y * jnp.sum(dy * y, axis=-1, keepdims=True)))
            d_up = jnp.concatenate(normed, axis=1)
        yield
        dc = d_up * da_dc
        dp = (cw_ref[3:4, cw] * dc + cw_ref[2:3, cw] * _up(dc, 1) + cw_ref[1:2, cw] * _up(dc, 2)
              + cw_ref[0:1, cw] * _up(dc, 3))
        dp_ref[:, cw] = dp[main].astype(BF16)
        yield
        dcm = dc[main]
        conv_parts[cb] = jnp.concatenate([jnp.sum(dcm * _down(ext, 3 - k)[main], axis=0, keepdims=True)
                                          for k in range(DN_CONV)], axis=0)

    def finish_prepare(conv_parts, dconv_ref, dread, dnext_scr):
        part = jnp.concatenate([conv_parts[cb] for cb in range(DN_CONV_W // B)], axis=1)

        @pl.when(pl.program_id(0) == 0)
        def _():
            dconv_ref[...] = part

        @pl.when(pl.program_id(0) > 0)
        def _():
            dconv_ref[...] += part

        dnext_scr[...] = dread[0:HALO, :]

    def step(a_ref, g_ref, b_ref, s_ref, t_ref, vn_ref, u_ref, w_ref, dog_ref, o_ref, pg_ref, gn_ref,
             p_ref, pp_ref, pn_ref, cw_ref, dp_ref, dconv_ref, dg_ref, db_ref, dgate_ref, dgn_ref,
             ds_scr, dnext_scr, dwrite, dread):
        head_lane = _iota2((C, H), 1)
        dg_cols, db_cols, dgn_parts, conv_parts = {}, {}, {}, {}

        def output_gate_bwd(hh, vs):
            d, o, gate, gn_v = dog_ref[:, vs], o_ref[:, vs], pg_ref[:, vs], gn_ref[...]
            sg, dsg = _silu_and_grad(gate)
            r = lax.rsqrt(jnp.mean(o * o, axis=-1, keepdims=True) + RMS_EPS)
            n = o * r
            dy = d * sg
            dgate_ref[:, vs] = (d * (n * gn_v) * dsg).astype(BF16)
            dn = dy * gn_v
            dgn_parts[hh] = jnp.sum(dy * n, axis=0, keepdims=True)
            return r * (dn - n * jnp.mean(dn * n, axis=-1, keepdims=True))

        def head(hh):
            qs, vs = slice(hh * DN_DK, (hh + 1) * DN_DK), slice(hh * DN_DV, (hh + 1) * DN_DV)
            ks = slice(DN_QK_W + hh * DN_DK, DN_QK_W + (hh + 1) * DN_DK)
            vas = slice(2 * DN_QK_W + hh * DN_DV, 2 * DN_QK_W + (hh + 1) * DN_DV)
            q, k, v = a_ref[:, qs], a_ref[:, ks], a_ref[:, vas]
            gc = jnp.sum(jnp.where(head_lane == hh, g_ref[...], 0.0), axis=1, keepdims=True)
            bc = jnp.sum(jnp.where(head_lane == hh, b_ref[...], 0.0), axis=1, keepdims=True)
            t = _dn_chunk_terms(q, k, gc, bc)
            yield
            lower, strict, eye = t["lower"], t["strict"], t["eye"]
            decay, eg, egl, kb, qd, kd = t["decay"], t["eg"], t["egl"], t["kb"], t["qd"], t["kd"]
            s, tm, vn, u, w = s_ref[hh], t_ref[hh], vn_ref[:, vs], u_ref[:, vs], w_ref[:, qs]
            d_o = output_gate_bwd(hh, vs)
            ds_next = ds_scr[hh]
            egl_tot = jnp.exp(t["gl"])
            dob, sb, dsb, vnb = d_o.astype(BF16), s.astype(BF16), ds_next.astype(BF16), vn.astype(BF16)

            dvn = _bdot(t["aqk"], dob, TN) + _bdot(kd, dsb)
            yield
            daqk = jnp.where(lower, _dot(dob, vnb, NT), 0.0)
            dqd = _dot(dob, sb, NT)
            dkd = _dot(vnb, dsb, NT)
            yield
            dvnb = dvn.astype(BF16)
            ds_scr[hh] = _bdot(qd, dob, TN) + egl_tot * ds_next - _bdot(w, dvnb, TN)
            dgl = egl_tot * jnp.sum(jnp.sum(s * ds_next, axis=1, keepdims=True), axis=0, keepdims=True)
            dw = -_dot(dvnb, sb, NT)
            yield
            tms = _split(tm)
            dru = _x3dot(tms, dvn, TN)
            drw = _x3dot(tms, dw, TN)
            yield
            dl = -jnp.where(strict, _x3dot(dru, u, NT) + _x3dot(drw, w, NT), 0.0)
            yield
            dkk = (dl * decay).astype(BF16)
            dqk = (daqk * decay).astype(BF16)
            dkb = _bdot(dkk, k) + drw * eg
            yield
            dwrite[:, ks] = _bdot(dkk, kb, TN) + _bdot(dqk, q, TN) + dkd * egl + dkb * bc
            dwrite[:, qs] = _bdot(dqk, k) + dqd * eg
            dwrite[:, vas] = dru * bc
            yield
            db_cols[hh] = jnp.sum(dru * v, axis=1, keepdims=True) + jnp.sum(dkb * k, axis=1, keepdims=True)
            pm = dl * t["lmat"] + daqk * t["aqk"]
            col_as_col = jnp.sum(jnp.where(eye, jnp.sum(pm, axis=0, keepdims=True), 0.0), axis=1, keepdims=True)
            kdsum = jnp.sum(dkd * kd, axis=1, keepdims=True)
            dgc = (jnp.sum(pm, axis=1, keepdims=True) - col_as_col + jnp.sum(dqd * qd, axis=1, keepdims=True)
                   - kdsum + jnp.sum(drw * (kb * eg), axis=1, keepdims=True))
            dgl = dgl + jnp.sum(kdsum, axis=0, keepdims=True)
            dg_cols[hh] = dgc + jnp.where(t["last"], dgl, 0.0)

        _interleave([head(hh) for hh in range(H)]
                    + [prepare_bwd(cb, p_ref, pp_ref, pn_ref, cw_ref, dread, dnext_scr, dp_ref, conv_parts)
                       for cb in range(DN_CONV_W // B)])
        dg_ref[...] = sum(jnp.where(head_lane == hh, dg_cols[hh], 0.0) for hh in range(H))
        db_ref[...] = sum(jnp.where(head_lane == hh, db_cols[hh], 0.0) for hh in range(H))
        dgn_part = sum(dgn_parts[hh] for hh in range(H))

        @pl.when(pl.program_id(0) == 0)
        def _():
            dgn_ref[...] = dgn_part

        @pl.when(pl.program_id(0) > 0)
        def _():
            dgn_ref[...] += dgn_part

        finish_prepare(conv_parts, dconv_ref, dread, dnext_scr)

    def body(*refs):
        s = pl.program_id(0)
        io, (ds_scr, dnext_scr, buf_a, buf_b) = refs[:-4], refs[-4:]
        p_ref, pp_ref, pn_ref, cw_ref, dp_ref, dconv_ref = refs[12:18]

        @pl.when(s == 0)
        def _():
            ds_scr[...] = jnp.zeros_like(ds_scr)
            dnext_scr[...] = jnp.zeros_like(dnext_scr)
            buf_b[...] = jnp.zeros_like(buf_b)

        @pl.when((s < N) & (s % 2 == 0))
        def _():
            step(*io, ds_scr, dnext_scr, buf_a, buf_b)

        @pl.when((s < N) & (s % 2 == 1))
        def _():
            step(*io, ds_scr, dnext_scr, buf_b, buf_a)

        @pl.when(s == N)
        def _():
            conv_parts = {}
            _interleave([prepare_bwd(cb, p_ref, pp_ref, pn_ref, cw_ref, buf_b, dnext_scr, dp_ref, conv_parts)
                         for cb in range(DN_CONV_W // B)])
            finish_prepare(conv_parts, dconv_ref, buf_b, dnext_scr)

    cc = lambda s: jnp.maximum(N - 1 - s, 0)
    pc = lambda s: jnp.clip(N - s, 0, N - 1)
    row = lambda w: pl.BlockSpec((C, w), lambda s: (cc(s), 0))
    per_chunk = lambda a, b: pl.BlockSpec((H, None, a, b), lambda s: (0, cc(s), 0, 0))
    vec = pl.BlockSpec((1, DN_DV), lambda s: (0, 0))
    per_c = C // HALO
    conv_spec = pl.BlockSpec((DN_CONV, DN_CONV_W), lambda s: (0, 0))
    return _call(
        body, comm, name=name, grid=(N + 1,),
        in_specs=[row(DN_CONV_W), row(H), row(H), per_chunk(DN_DK, DN_DV), per_chunk(C, C),
                  row(DN_V_W), row(DN_V_W), row(DN_QK_W), row(DN_V_W), row(DN_V_W), row(DN_V_W), vec,
                  pl.BlockSpec((C, DN_CONV_W), lambda s: (pc(s), 0)),
                  pl.BlockSpec((HALO, DN_CONV_W), lambda s: (jnp.maximum(pc(s) * per_c - 1, 0), 0)),
                  pl.BlockSpec((HALO, DN_CONV_W), lambda s: (jnp.minimum((pc(s) + 1) * per_c, N * per_c - 1), 0)),
                  conv_spec],
        out_specs=[pl.BlockSpec((C, DN_CONV_W), lambda s: (pc(s), 0)), conv_spec, row(H), row(H), row(DN_V_W), vec],
        out_shape=[jax.ShapeDtypeStruct((T, DN_CONV_W), BF16), jax.ShapeDtypeStruct((DN_CONV, DN_CONV_W), F32),
                   jax.ShapeDtypeStruct((T, H), F32), jax.ShapeDtypeStruct((T, H), F32),
                   jax.ShapeDtypeStruct((T, DN_V_W), BF16), jax.ShapeDtypeStruct((1, DN_DV), F32)],
        scratch_shapes=[pltpu.VMEM((H, DN_DK, DN_DV), F32), pltpu.VMEM((HALO, DN_CONV_W), F32),
                        pltpu.VMEM((C, DN_CONV_W), F32), pltpu.VMEM((C, DN_CONV_W), F32)],
        semantics=("arbitrary",),
        args=(act, g, beta, s_saved, tm_saved, vn_saved, u_saved, w_saved, dog, o_raw, pgate, gn,
              pqkv, pqkv, pqkv, conv_w))


def _dn_split_w_in(w):
    return w, jnp.pad(w[:, DN_CONV_W + DN_V_W:], ((0, 0), (0, DN_AB_PAD - 2 * DN_HEADS)))


def _out_proj(og, w_out, x_res, next_g, name):
    if next_g is None:
        return _matmul(og, w_out, "nn", name, add=x_res), None
    return tuple(_matmul(og, w_out, "nn", name, add=x_res, norm_fwd=next_g, tm=NORM_FUSED_TM))


def _dn_layer_fwd(h, wts, conv_w, a_log, dt_bias, gn, w_out, x_res, tag, comm=None, next_g=None):
    w_in, wab = wts
    H = DN_HEADS
    pqkv = _matmul(h, w_in, "nn", tag + "_pqkv", b_cols=(0, DN_CONV_W))
    pgate = _matmul(h, w_in, "nn", tag + "_pgate", b_cols=(DN_CONV_W, DN_V_W))
    pab = _matmul(h, wab, "nn", tag + "_pab")
    a_in, b_in = pab[:, :H], pab[:, H:2 * H]
    g, beta = _dn_gates(a_in, b_in, a_log, dt_bias, tag + "_gates")
    (act, o_raw, og, s_sv, tm_sv, vn_sv, u_sv, w_sv), landed = _dn_chunk_fwd(pqkv, conv_w, g, beta, pgate, gn,
                                                                             tag + "_chunk_fwd", comm)
    if callable(w_out):
        w_out = w_out(landed)
    y = _out_proj(og, w_out, x_res, next_g, tag + "_out")
    saved = dict(h=h, wts=wts, conv_w=conv_w, a_log=a_log, dt_bias=dt_bias, gn=gn, w_out=w_out, pqkv=pqkv, pgate=pgate,
                 a_in=a_in, b_in=b_in, g=g, beta=beta, act=act, o_raw=o_raw, chunk=(s_sv, tm_sv, vn_sv, u_sv, w_sv), og=og)
    return y, saved, landed


def _dn_layer_bwd(dout, sv, tag, norm, comm_of=None, late_comm_of=None):
    w_in, wab = sv["wts"]
    h = sv["h"]
    dog = _matmul(dout, sv["w_out"], "nt", tag + "_dog")
    dw_out = _matmul(sv["og"], dout, "tn", tag + "_dwout", out_dtype=BF16)
    comm = comm_of(dw_out) if comm_of is not None else None
    (dpqkv, dconv, dg, dbeta, dgate, dgn), landed = _dn_chunk_bwd(
        sv["pqkv"], sv["conv_w"], sv["act"], sv["g"], sv["beta"], *sv["chunk"], dog, sv["o_raw"], sv["pgate"], sv["gn"],
        tag + "_chunk_bwd", comm)
    da_in, db_in, da_log, ddt = _dn_gates_bwd(dg, dbeta, sv["a_in"], sv["b_in"], sv["a_log"], sv["dt_bias"],
                                              tag + "_gates_bwd")
    dpab = jnp.pad(jnp.concatenate([da_in, db_in], axis=1), ((0, 0), (0, DN_AB_PAD - 2 * DN_HEADS)))
    dwqkv = _matmul(h, dpqkv, "tn", tag + "_dwqkv", out_dtype=BF16)
    dwgate = _matmul(h, dgate, "tn", tag + "_dwgate", out_dtype=BF16)
    dwab = _matmul(h, dpab, "tn", tag + "_dwab", out_dtype=BF16)
    dw_in = jnp.concatenate([dwqkv, dwgate, dwab[:, :2 * DN_HEADS]], axis=1)
    grads = dict(dn_w_in=dw_in, dn_conv_w=dconv, dn_a_log=da_log, dn_dt_bias=ddt, dn_o_norm_g=dgn, dn_w_out=dw_out)
    dx, landed_late = _matmul_nt_sum([(dpqkv, w_in, 0), (dgate, w_in, DN_CONV_W), (dpab, wab, 0)], tag + "_dh",
                                     late_comm_of(grads) if late_comm_of is not None else None, norm_bwd=norm,
                                     tm=NORM_FUSED_TM if norm is not None else 1024)
    return dx, grads, landed, landed_late


def _sb_layer_fwd(h, w_in, qg, kg, w_out, x_res, tag, comm=None, next_g=None):
    qg2, kg2 = jnp.tile(qg, (1, 2)), jnp.tile(kg, (1, 2))
    proj = _matmul(h, w_in, "nn", tag + "_proj", blocked_b=True)
    qn, kn, vb = _sb_prep(proj, qg2, kg2, tag + "_prep")
    (o, og, ltot, done), landed = _sb_attn_fwd(qn, kn, vb, proj, tag + "_attn_fwd", comm)
    y = _out_proj(og, w_out, x_res, next_g, tag + "_out")
    saved = dict(h=h, w_in=w_in, qg2=qg2, kg2=kg2, w_out=w_out, proj=proj, qn=qn, kn=kn, vb=vb, o=o, og=og, ltot=ltot,
                 done=done)
    return y, saved, landed


def _sb_layer_bwd(dout, sv, tag, comm=None):
    dog = _matmul(dout, sv["w_out"], "nt", tag + "_dog")
    dw_out = _matmul(sv["og"], dout, "tn", tag + "_dwout", out_dtype=BF16)
    (dqn, dkn, dv, dgate), landed = _sb_attn_bwd(sv["qn"], sv["kn"], sv["vb"], dog, sv["o"], sv["ltot"], sv["done"],
                                                 sv["proj"], tag + "_attn_bwd", comm)
    dproj, dqgp, dkgp = _sb_prep_bwd(sv["proj"], dqn, dkn, dv, dgate, sv["qg2"], sv["kg2"], tag + "_prep_bwd")
    dw_in = _matmul(sv["h"], dproj, "tn", tag + "_dwin", out_dtype=BF16, blocked_out=N_DEV)
    dh = _matmul(dproj, sv["w_in"], "nt", tag + "_dh", blocked_b=True)
    dqg = _fold_heads(dqgp, tag + "_dqg")
    dkg = _fold_heads(dkgp, tag + "_dkg")
    return dh, dict(sb_w_in=dw_in, sb_q_norm_g=dqg, sb_k_norm_g=dkg, sb_w_out=dw_out), landed


def _sc_layer_fwd(h, w_in, conv_w, w_out, x_res, tag, next_g=None):
    proj = _matmul(h, w_in, "nn", tag + "_proj", blocked_b=True)
    yg = _sc_fwd(proj, conv_w, tag + "_fwd")
    y = _out_proj(yg, w_out, x_res, next_g, tag + "_out")
    return y, dict(h=h, w_in=w_in, conv_w=conv_w, w_out=w_out, proj=proj, yg=yg)


def _sc_layer_bwd(dout, sv, tag):
    dyg = _matmul(dout, sv["w_out"], "nt", tag + "_dyg")
    dw_out = _matmul(sv["yg"], dout, "tn", tag + "_dwout", out_dtype=BF16)
    dproj, dconv = _sc_bwd(dyg, sv["proj"], sv["conv_w"], tag + "_bwd")
    dw_in = _matmul(sv["h"], dproj, "tn", tag + "_dwin", out_dtype=BF16, blocked_out=N_DEV)
    dh = _matmul(dproj, sv["w_in"], "nt", tag + "_dh", blocked_b=True)
    return dh, dict(sc_w_in=dw_in, sc_conv_w=dconv, sc_w_out=dw_out)


def _adamw(w, m, v, parts, name):
    L, R, C = w.shape
    tr = _tile(R, 128, SUBLANE)

    def body(*refs):
        w_ref, m_ref, v_ref = refs[:3]
        g_ref, d_ref, nm_ref, nv_ref = refs[3 + L:]

        def update(p_ref):
            g = p_ref[0].astype(F32)
            for s in range(1, N_DEV):
                g = g + p_ref[s].astype(F32)
            m2 = ADAM_B1 * m_ref[...] + (1.0 - ADAM_B1) * g
            v2 = ADAM_B2 * v_ref[...] + (1.0 - ADAM_B2) * (g * g)
            m_hat = m2 / (1.0 - ADAM_B1 ** ADAM_STEP)
            v_hat = v2 / (1.0 - ADAM_B2 ** ADAM_STEP)
            g_ref[...] = g
            d_ref[...] = -ADAM_LR * (m_hat / (jnp.sqrt(v_hat) + ADAM_EPS) + ADAM_WD * w_ref[...])
            nm_ref[...] = m2
            nv_ref[...] = v2

        for layer in range(L):
            pl.when(pl.program_id(0) == layer)(functools.partial(update, refs[3 + layer]))

    blk = pl.BlockSpec((None, tr, C), lambda l, i: (l, i, 0))
    landing = pl.BlockSpec((N_DEV, tr, C), lambda l, i: (0, i, 0))
    return pl.pallas_call(
        body, name=name, grid=(L, R // tr),
        in_specs=[blk, blk, blk] + [landing] * L,
        out_specs=[blk] * 4, out_shape=[jax.ShapeDtypeStruct((L, R, C), F32)] * 4,
        compiler_params=_params("parallel", "parallel"),
    )(w, m, v, *parts)


_HBM = pl.BlockSpec(memory_space=pltpu.HBM)
_MESH = pl.DeviceIdType.MESH


def _slot(x, y, c):
    return 4 * x + 2 * y + c


class _Gather:
    def __init__(self, shards):
        self.arrays = list(shards)
        n = len(self.arrays)
        self.out_shapes = [jax.ShapeDtypeStruct((N_DEV,) + s.shape, s.dtype) for s in self.arrays]
        self.scratch = [pltpu.SemaphoreType.DMA((n, N_DEV - 1)), pltpu.SemaphoreType.DMA((n, N_DEV - 1)),
                        pltpu.SemaphoreType.DMA((n,))]

    def _parts(self, ins, outs, sems):
        send_sems, recv_sems, local_sems = sems
        n = len(self.arrays)
        x, y, c = lax.axis_index("x"), lax.axis_index("y"), lax.axis_index("c")
        me, sibling = (x, y, c), (x, y, 1 - c)
        chips = [(1 - x, y), (x, 1 - y), (1 - x, 1 - y)]

        def copy(a, k, block, to, src=None):
            dst = outs[a].at[_slot(*block)]
            return pltpu.make_async_remote_copy(src_ref=dst if src is None else src, dst_ref=dst,
                                                send_sem=send_sems.at[a, k], recv_sem=recv_sems.at[a, k],
                                                device_id=to, device_id_type=_MESH)

        mine = [pltpu.make_async_copy(ins[a], outs[a].at[_slot(*me)], local_sems.at[a]) for a in range(n)]
        first = []
        for a in range(n):
            first.append(copy(a, 0, me, sibling, src=ins[a]))
            first += [copy(a, 1 + j, me, (*chip, c), src=ins[a]) for j, chip in enumerate(chips)]
        return n, c, me, sibling, chips, copy, mine, first

    def start(self, ins, outs, sems):
        _, _, _, _, _, _, mine, first = self._parts(ins, outs, sems)
        for cp in mine + first:
            cp.start()

    def finish(self, ins, outs, sems):
        n, c, me, sibling, chips, copy, mine, first = self._parts(ins, outs, sems)
        passed = []
        for j, chip in enumerate(chips):
            for a in range(n):
                copy(a, 1 + j, (*chip, c), me).wait_recv()
                fwd = copy(a, 4 + j, (*chip, c), sibling)
                fwd.start()
                passed.append(fwd)
        for a in range(n):
            copy(a, 0, sibling, me).wait_recv()
            for j, chip in enumerate(chips):
                copy(a, 4 + j, (*chip, 1 - c), me).wait_recv()
        for cp in first + passed:
            cp.wait_send()
        for cp in mine:
            cp.wait()


class _Exchange:
    def __init__(self, arrays, scatter):
        self.arrays, self.scatter = list(arrays), list(scatter)
        n = len(self.arrays)
        shapes = [a.shape[1:] if s else a.shape for a, s in zip(self.arrays, self.scatter)]
        self.out_shapes = [jax.ShapeDtypeStruct((N_DEV,) + tuple(s), a.dtype) for s, a in zip(shapes, self.arrays)]
        self.scratch = [pltpu.SemaphoreType.DMA((n, N_DEV - 1)), pltpu.SemaphoreType.DMA((n, N_DEV - 1)),
                        pltpu.SemaphoreType.DMA((n,))]

    def _copies(self, ins, outs, sems):
        send_sems, recv_sems, local_sems = sems
        n, scatter = len(self.arrays), self.scatter
        x, y, c = lax.axis_index("x"), lax.axis_index("y"), lax.axis_index("c")
        me = _slot(x, y, c)
        copies = [pltpu.make_async_copy(ins[a].at[me] if scatter[a] else ins[a], outs[a].at[me], local_sems.at[a])
                  for a in range(n)]
        for r in range(1, N_DEV):
            px = 1 - x if r & 4 else x
            py = 1 - y if r & 2 else y
            pc = 1 - c if r & 1 else c
            for a in range(n):
                copies.append(pltpu.make_async_remote_copy(
                    src_ref=ins[a].at[_slot(px, py, pc)] if scatter[a] else ins[a], dst_ref=outs[a].at[me],
                    send_sem=send_sems.at[a, r - 1], recv_sem=recv_sems.at[a, r - 1],
                    device_id=(px, py, pc), device_id_type=_MESH))
        return copies

    def start(self, ins, outs, sems):
        for cp in self._copies(ins, outs, sems):
            cp.start()

    def finish(self, ins, outs, sems):
        for cp in self._copies(ins, outs, sems):
            cp.wait()


def _comm_call(comm, name):
    n = len(comm.arrays)

    def body(*refs):
        ins, outs, sems = refs[:n], refs[n:2 * n], refs[2 * n:]
        comm.start(ins, outs, sems)
        comm.finish(ins, outs, sems)

    return pl.pallas_call(body, name=name, in_specs=[_HBM] * n, out_specs=[_HBM] * n, out_shape=comm.out_shapes,
                          scratch_shapes=comm.scratch)(*comm.arrays)


def _call(body, comm, *, name, grid, in_specs, out_specs, out_shape, scratch_shapes, semantics, args):
    if comm is None:
        outs = pl.pallas_call(body, name=name, grid=grid, in_specs=in_specs, out_specs=out_specs, out_shape=out_shape,
                              scratch_shapes=scratch_shapes, compiler_params=_params(*semantics))(*args)
        return outs, []
    n_in, n_out, n_scr, n_c = len(in_specs), len(out_specs), len(scratch_shapes), len(comm.arrays)

    def fused(*refs):
        ins, refs = refs[:n_in], refs[n_in:]
        c_ins, refs = refs[:n_c], refs[n_c:]
        outs, refs = refs[:n_out], refs[n_out:]
        c_outs, refs = refs[:n_c], refs[n_c:]
        scr, sems = refs[:n_scr], refs[n_scr:]
        ids = [pl.program_id(d) for d in range(len(grid))]
        first = functools.reduce(jnp.logical_and, [i == 0 for i in ids])
        last = functools.reduce(jnp.logical_and, [i == g - 1 for i, g in zip(ids, grid)])

        @pl.when(first)
        def _():
            comm.start(c_ins, c_outs, sems)

        body(*ins, *outs, *scr)

        @pl.when(last)
        def _():
            comm.finish(c_ins, c_outs, sems)

    outs = pl.pallas_call(
        fused, name=name, grid=grid, in_specs=list(in_specs) + [_HBM] * n_c, out_specs=list(out_specs) + [_HBM] * n_c,
        out_shape=list(out_shape) + comm.out_shapes, scratch_shapes=list(scratch_shapes) + comm.scratch,
        compiler_params=_params(*["arbitrary"] * len(grid)))(*args, *comm.arrays)
    return outs[:n_out], outs[n_out:]


_GATHER_0 = (("dn_w_in", 0), ("dn_conv_w", 0), ("dn_o_norm_g", 0))
_GATHER_1 = (("dn_w_out", 0), ("sb_w_in", 0), ("sb_w_out", 0))
_GATHER_2 = (("sc_w_in", 0), ("sc_conv_w", 0), ("sc_w_out", 0), ("dn_w_in", 1), ("dn_conv_w", 1), ("dn_o_norm_g", 1),
             ("dn_w_out", 1))
_EXCHANGE_A = _GATHER_2
_EXCHANGE_B = (("sb_w_in", 0), ("sb_w_out", 0), ("dn_w_out", 0))
_EXCHANGE_C = _GATHER_0
_MATMUL_WEIGHTS = ("dn_w_in", "dn_w_out", "sb_w_in", "sb_w_out", "sc_w_in", "sc_w_out")
_COLUMN_SHARDED = ("dn_w_in", "dn_conv_w", "dn_o_norm_g", "sb_w_in", "sc_w_in", "sc_conv_w")
_BLOCKED = ("sb_w_in", "sc_w_in")
_REPLICATED = ("norm_g", "dn_a_log", "dn_dt_bias", "sb_q_norm_g", "sb_k_norm_g")
_ORDER = ("norm_g", "dn_w_in", "dn_conv_w", "dn_a_log", "dn_dt_bias", "dn_o_norm_g", "dn_w_out", "sb_w_in", "sb_q_norm_g",
          "sb_k_norm_g", "sb_w_out", "sc_w_in", "sc_conv_w", "sc_w_out")
_PACK_COLS = D_MODEL


def _as_2d(a):
    return a.reshape(1, -1) if a.ndim == 1 else a


def _assemble(name, gathered):
    n, r, c = gathered.shape
    if name in _COLUMN_SHARDED:
        return jnp.moveaxis(gathered, 0, 1).reshape(r, n * c)
    return gathered.reshape(n * r, c)


def _disassemble(name, full):
    r, c = full.shape
    if name in _COLUMN_SHARDED:
        return jnp.moveaxis(full.reshape(r, N_DEV, c // N_DEV), 1, 0)
    return full.reshape(N_DEV, r // N_DEV, c)


def _pack_replicated(d):
    rows = [d["norm_g"]]
    for name in _REPLICATED[1:]:
        flat = d[name].reshape(1, -1)
        rows.append(jnp.pad(flat, ((0, 0), (0, _PACK_COLS - flat.shape[1]))))
    return jnp.concatenate(rows, axis=0)


def _unpack_replicated(p, like):
    out = {"norm_g": p[:4]}
    for r, name in enumerate(_REPLICATED[1:]):
        shape = like[name].shape
        out[name] = p[4 + r, :math.prod(shape)].reshape(shape)
    return out


def kernel(x, norm_g, dn_w_in, dn_conv_w, dn_a_log, dn_dt_bias, dn_o_norm_g, dn_w_out, sb_w_in, sb_q_norm_g, sb_k_norm_g, sb_w_out, sc_w_in, sc_conv_w, sc_w_out, loss_target, m_norm_g, m_dn_w_in, m_dn_conv_w, m_dn_a_log, m_dn_dt_bias, m_dn_o_norm_g, m_dn_w_out, m_sb_w_in, m_sb_q_norm_g, m_sb_k_norm_g, m_sb_w_out, m_sc_w_in, m_sc_conv_w, m_sc_w_out, v_norm_g, v_dn_w_in, v_dn_conv_w, v_dn_a_log, v_dn_dt_bias, v_dn_o_norm_g, v_dn_w_out, v_sb_w_in, v_sb_q_norm_g, v_sb_k_norm_g, v_sb_w_out, v_sc_w_in, v_sc_conv_w, v_sc_w_out):
    w = dict(norm_g=norm_g, dn_w_in=dn_w_in, dn_conv_w=dn_conv_w, dn_a_log=dn_a_log, dn_dt_bias=dn_dt_bias,
             dn_o_norm_g=dn_o_norm_g, dn_w_out=dn_w_out, sb_w_in=sb_w_in, sb_q_norm_g=sb_q_norm_g, sb_k_norm_g=sb_k_norm_g,
             sb_w_out=sb_w_out, sc_w_in=sc_w_in, sc_conv_w=sc_conv_w, sc_w_out=sc_w_out)
    m = dict(norm_g=m_norm_g, dn_w_in=m_dn_w_in, dn_conv_w=m_dn_conv_w, dn_a_log=m_dn_a_log, dn_dt_bias=m_dn_dt_bias,
             dn_o_norm_g=m_dn_o_norm_g, dn_w_out=m_dn_w_out, sb_w_in=m_sb_w_in, sb_q_norm_g=m_sb_q_norm_g,
             sb_k_norm_g=m_sb_k_norm_g, sb_w_out=m_sb_w_out, sc_w_in=m_sc_w_in, sc_conv_w=m_sc_conv_w, sc_w_out=m_sc_w_out)
    v = dict(norm_g=v_norm_g, dn_w_in=v_dn_w_in, dn_conv_w=v_dn_conv_w, dn_a_log=v_dn_a_log, dn_dt_bias=v_dn_dt_bias,
             dn_o_norm_g=v_dn_o_norm_g, dn_w_out=v_dn_w_out, sb_w_in=v_sb_w_in, sb_q_norm_g=v_sb_q_norm_g,
             sb_k_norm_g=v_sb_k_norm_g, sb_w_out=v_sb_w_out, sc_w_in=v_sc_w_in, sc_conv_w=v_sc_conv_w, sc_w_out=v_sc_w_out)

    def gather_of(keys):
        return _Gather([_as_2d(w[k][j]).astype(BF16) if k in _MATMUL_WEIGHTS else _as_2d(w[k][j]) for k, j in keys])

    def full_weights(keys, gathered):
        return {key: g if key[0] in _BLOCKED else _assemble(key[0], g) for key, g in zip(keys, gathered)}

    def exchange_of(keys, grads, extra=()):
        out = [grads[k, j] if k in _BLOCKED else
               _disassemble(k, grads[k, j].astype(BF16) if k in _MATMUL_WEIGHTS else grads[k, j]) for k, j in keys]
        return _Exchange(out + list(extra), [True] * len(out) + [False] * len(extra))

    xs, saves = [x[0]], []
    h, got = _rmsnorm_fwd(xs[0], norm_g[0:1], "norm0", gather_of(_GATHER_0))
    F = full_weights(_GATHER_0, got)

    def w_out_0(got):
        F.update(full_weights(_GATHER_1, got))
        return F["dn_w_out", 0]

    (y, h), sv, _ = _dn_layer_fwd(h, _dn_split_w_in(F["dn_w_in", 0]), F["dn_conv_w", 0], dn_a_log[0:1], dn_dt_bias[0:1],
                                  F["dn_o_norm_g", 0], w_out_0, xs[0], "dn0", gather_of(_GATHER_1), norm_g[1:2])
    xs.append(y)
    saves.append(sv)
    (y, h), sv, got = _sb_layer_fwd(h, F["sb_w_in", 0], sb_q_norm_g, sb_k_norm_g, F["sb_w_out", 0], xs[1], "sb",
                                    gather_of(_GATHER_2), norm_g[2:3])
    F.update(full_weights(_GATHER_2, got))
    xs.append(y)
    saves.append(sv)
    (y, h), sv = _sc_layer_fwd(h, F["sc_w_in", 0], F["sc_conv_w", 0], F["sc_w_out", 0], xs[2], "sc", norm_g[3:4])
    xs.append(y)
    saves.append(sv)
    (y, _), sv, _ = _dn_layer_fwd(h, _dn_split_w_in(F["dn_w_in", 1]), F["dn_conv_w", 1], dn_a_log[1:2], dn_dt_bias[1:2],
                                  F["dn_o_norm_g", 1], F["dn_w_out", 1], xs[3], "dn1")
    xs.append(y)
    saves.append(sv)
    dx, loss_part = _loss_head(xs[4], loss_target[0])

    G, dnorm, landed = {}, [None] * 4, {}

    def keep(grads, j):
        G.update({(k, j): g for k, g in grads.items()})

    dh, grads, _, _ = _dn_layer_bwd(dx, saves[3], "dn1", None)
    keep(grads, 1)
    dx, dnorm[3] = _rmsnorm_bwd(dh, xs[3], norm_g[3:4], dx, "norm3_bwd")
    dh, grads = _sc_layer_bwd(dx, saves[2], "sc")
    keep(grads, 0)
    dx, dnorm[2] = _rmsnorm_bwd(dh, xs[2], norm_g[2:3], dx, "norm2_bwd")
    dh, grads, got = _sb_layer_bwd(dx, saves[1], "sb", exchange_of(_EXCHANGE_A, G))
    keep(grads, 0)
    landed.update(zip(_EXCHANGE_A, got))
    dx, dnorm[1] = _rmsnorm_bwd(dh, xs[1], norm_g[1:2], dx, "norm1_bwd")

    def exchange_b(dw_out):
        G["dn_w_out", 0] = dw_out
        return exchange_of(_EXCHANGE_B, G)

    def exchange_c(grads):
        keep(grads, 0)
        return exchange_of(_EXCHANGE_C, G)

    (dx, dnorm[0]), grads, got, got_late = _dn_layer_bwd(dx, saves[0], "dn0", (xs[0], norm_g[0:1], dx), exchange_b, exchange_c)
    landed.update(zip(_EXCHANGE_B, got))
    landed.update(zip(_EXCHANGE_C, got_late))
    replicated = dict(norm_g=jnp.concatenate(dnorm, axis=0),
                      dn_a_log=jnp.concatenate([G["dn_a_log", 0], G["dn_a_log", 1]], axis=0),
                      dn_dt_bias=jnp.concatenate([G["dn_dt_bias", 0], G["dn_dt_bias", 1]], axis=0),
                      sb_q_norm_g=G["sb_q_norm_g", 0], sb_k_norm_g=G["sb_k_norm_g", 0])
    got = _comm_call(_Exchange([_pack_replicated(replicated)], [False]), "exchange_replicated")

    res = {}
    for k in _ORDER:
        if k in _REPLICATED:
            continue
        shape = w[k].shape
        as_3d = lambda a: a.reshape(shape[0], math.prod(shape[1:-1]), shape[-1])
        outs = _adamw(as_3d(w[k]), as_3d(m[k]), as_3d(v[k]), [landed[k, j] for j in range(shape[0])], "adamw_" + k)
        res[k] = [o.reshape(shape) for o in outs]
    outs = _adamw(_pack_replicated(w)[None], _pack_replicated(m)[None], _pack_replicated(v)[None], [got[-1]],
                  "adamw_replicated")
    unpacked = [_unpack_replicated(o[0], w) for o in outs]
    for k in _REPLICATED:
        res[k] = [u[k] for u in unpacked]

    loss = lax.psum(loss_part[0, 0], ("x", "y", "c"))
    return (loss, dx[None]) + tuple(res[k][0] for k in _ORDER) + tuple(res[k][1] for k in _ORDER) \
        + tuple(res[k][2] for k in _ORDER) + tuple(res[k][3] for k in _ORDER)
```

```python
import functools
import itertools
import math

import jax
import jax.numpy as jnp
from jax import lax
from jax.experimental import pallas as pl
from jax.experimental.pallas import tpu as pltpu

F32 = jnp.float32
BF16 = jnp.bfloat16
HIGHEST = lax.Precision.HIGHEST

N_DEV = 8
D_MODEL = 1024
RMS_EPS = 1e-6
L2_EPS = 1e-6

DN_HEADS = 8
DN_DK = 128
DN_DV = 256
DN_QK_W = DN_HEADS * DN_DK
DN_V_W = DN_HEADS * DN_DV
DN_CONV = 4
DN_CHUNK = 64
DN_CONV_W = 2 * DN_QK_W + DN_V_W
DN_IN = DN_CONV_W + DN_V_W + 2 * DN_HEADS
DN_AB_PAD = 128
DN_PREP_BLK = 512

SB_HEADS = 16
SB_DH = 64
SB_W = SB_HEADS * SB_DH
SB_PAIRS = SB_HEADS // 2
SB_TQ = 256
SB_TK = 128
SB_DEAD = -106.0

SC_W = 2 * D_MODEL
SC_CONV = 3
SC_BLK = 512
SC_NBLK = SC_W // SC_BLK

ADAM_LR = 0.001
ADAM_B1 = 0.9
ADAM_B2 = 0.999
ADAM_EPS = 1e-08
ADAM_WD = 0.01
ADAM_STEP = 10

LANE = 128
SUBLANE = 8
HALO = SUBLANE
LONG_ROW_TILE = 512
NORM_FUSED_TM = 512
DEEP_TK = 2048
WIDE_TN = 2048
WIDE_ROW_TILE = 128
VMEM_LIMIT = 48 * 2 ** 20

NN = ((1,), (0,))
NT = ((1,), (1,))
TN = ((0,), (0,))


def _dot(a, b, dims=NN, precision=None):
    return lax.dot_general(a, b, (dims, ((), ())), precision=precision, preferred_element_type=F32)


def _bdot(a, b, dims=NN):
    return _dot(a.astype(BF16), b.astype(BF16), dims)


def _hdot(a, b, dims=NN):
    return _dot(a, b, dims, precision=HIGHEST)


def _tile(dim, pref, align=LANE):
    t = (min(pref, dim) // align) * align
    while t >= align:
        if dim % t == 0:
            return t
        t -= align
    return dim


def _params(*sem):
    return pltpu.CompilerParams(dimension_semantics=sem, vmem_limit_bytes=VMEM_LIMIT)


def _sigmoid(x):
    return 0.5 * jnp.tanh(0.5 * x) + 0.5


def _softplus(x):
    return jnp.maximum(x, 0.0) + jnp.log(1.0 + jnp.exp(-jnp.abs(x)))


def _silu_and_grad(x):
    s = _sigmoid(x)
    return x * s, s * (1.0 + x * (1.0 - s))


def _iota2(shape, dim):
    return lax.broadcasted_iota(jnp.int32, shape, dim)


def _matmul(a, b, mode, name, out_dtype=F32, add=None, b_cols=None, blocked_b=False, blocked_out=0,
            norm_fwd=None, norm_bwd=None, tm=1024, tn=1024, tk=1024):
    b_rows, b_width = (b.shape[1], b.shape[0] * b.shape[2]) if blocked_b else b.shape
    c0, b_used = b_cols if b_cols is not None else (0, b_width)
    if mode == "nn":
        (M, K), (K2, N) = a.shape, (b_rows, b_used)
    elif mode == "nt":
        (M, K), (N, K2) = a.shape, (b_rows, b_used)
    else:
        (K, M), (K2, N) = a.shape, (b_rows, b_used)
    assert K == K2, (a.shape, b.shape, mode)
    if mode == "tn":
        tk = max(tk, DEEP_TK)
    elif norm_fwd is None and norm_bwd is None and add is None:
        tn = max(tn, WIDE_TN)
    tm, tn, tk = _tile(M, tm), _tile(N, tn), _tile(K, tk)
    if blocked_b and mode == "nt":
        tk = b.shape[2]
    elif blocked_b:
        tn = b.shape[2]
    if blocked_out:
        tn = N // blocked_out
    nk = K // tk
    dims = {"nn": NN, "nt": NT, "tn": TN}[mode]
    a_spec = pl.BlockSpec((tk, tm), lambda i, j, k: (k, i)) if mode == "tn" else pl.BlockSpec((tm, tk), lambda i, j, k: (i, k))
    if mode == "nt":
        cb0 = c0 // tk
        assert c0 % tk == 0
        b_spec = (pl.BlockSpec((None, tn, tk), lambda i, j, k: (k + cb0, j, 0)) if blocked_b
                  else pl.BlockSpec((tn, tk), lambda i, j, k: (j, k + cb0)))
    else:
        cb0 = c0 // tn
        assert c0 % tn == 0
        b_spec = (pl.BlockSpec((None, tk, tn), lambda i, j, k: (j + cb0, k, 0)) if blocked_b
                  else pl.BlockSpec((tk, tn), lambda i, j, k: (k, j + cb0)))
    o_spec = pl.BlockSpec((tm, tn), lambda i, j, k: (i, j))
    out_spec = pl.BlockSpec((None, tm, tn), lambda i, j, k: (j, i, 0)) if blocked_out else o_spec
    out_shape = (blocked_out, M, tn) if blocked_out else (M, N)
    has_add = add is not None
    vec_spec = pl.BlockSpec((1, tn), lambda i, j, k: (0, j))
    assert not (norm_fwd is not None or norm_bwd is not None) or tn == N
    extra_in, extra_specs = [], []
    if has_add:
        extra_in, extra_specs = [add], [o_spec]
    if norm_fwd is not None:
        extra_in, extra_specs = extra_in + [norm_fwd], extra_specs + [vec_spec]
        out_specs = [o_spec, o_spec]
        out_shapes = [jax.ShapeDtypeStruct((M, N), out_dtype), jax.ShapeDtypeStruct((M, N), BF16)]
    elif norm_bwd is not None:
        extra_in, extra_specs = extra_in + list(norm_bwd), extra_specs + [o_spec, vec_spec, o_spec]
        out_specs = [o_spec, vec_spec]
        out_shapes = [jax.ShapeDtypeStruct((M, N), F32), jax.ShapeDtypeStruct((1, N), F32)]
    else:
        out_specs, out_shapes = out_spec, jax.ShapeDtypeStruct(out_shape, out_dtype)

    def body(*refs):
        a_ref, b_ref = refs[0], refs[1]
        extra = list(refs[2:2 + len(extra_in)])
        outs = refs[2 + len(extra_in):]
        add_ref = extra.pop(0) if has_add else None
        p = _bdot(a_ref[...], b_ref[...], dims)

        def finish(acc):
            if has_add:
                acc = acc + add_ref[...]
            if norm_bwd is not None:
                _rmsnorm_bwd_tile(acc, *extra, outs[0], outs[1], first=pl.program_id(0) == 0)
                return
            outs[0][...] = acc.astype(out_dtype)
            if norm_fwd is not None:
                r = lax.rsqrt(jnp.mean(acc * acc, axis=-1, keepdims=True) + RMS_EPS)
                outs[1][...] = (acc * r * extra[0][...]).astype(BF16)

        if nk == 1:
            finish(p)
        else:
            acc_ref = refs[-1]
            k = pl.program_id(2)

            @pl.when(k == 0)
            def _():
                acc_ref[...] = p

            @pl.when(k > 0)
            def _():
                acc_ref[...] += p

            @pl.when(k == nk - 1)
            def _():
                finish(acc_ref[...])

    return pl.pallas_call(
        body, name=name, grid=(M // tm, N // tn, nk),
        in_specs=[a_spec, b_spec] + extra_specs, out_specs=out_specs, out_shape=out_shapes,
        scratch_shapes=[pltpu.VMEM((tm, tn), F32)] if nk > 1 else [],
        compiler_params=(_params("arbitrary", "arbitrary", "arbitrary") if norm_bwd is not None
                         else _params("parallel", "parallel", "arbitrary")),
    )(a, b, *extra_in)


def _rmsnorm_bwd_tile(dh, x_ref, g_ref, res_ref, dx_ref, dg_ref, first):
    xv = x_ref[...]
    r = lax.rsqrt(jnp.mean(xv * xv, axis=-1, keepdims=True) + RMS_EPS)
    xh = xv * r
    dxh = dh * g_ref[...]
    m = jnp.mean(dxh * xh, axis=-1, keepdims=True)
    dx_ref[...] = res_ref[...] + r * (dxh - xh * m)
    part = jnp.sum(dh * xh, axis=0, keepdims=True)

    @pl.when(first)
    def _():
        dg_ref[...] = part

    @pl.when(jnp.logical_not(first))
    def _():
        dg_ref[...] += part


def _matmul_nt_sum(pairs, name, comm=None, norm_bwd=None, tm=NORM_FUSED_TM, tk=1024):
    M, N = pairs[0][0].shape[0], pairs[0][1].shape[0]
    tm = _tile(M, tm)
    tks = [_tile(a.shape[1], tk) for a, _, _ in pairs]
    steps = [a.shape[1] // t for (a, _, _), t in zip(pairs, tks)]
    offs = [sum(steps[:p]) for p in range(len(pairs))]
    total = sum(steps)

    n_extra = 3 if norm_bwd is not None else 0

    def body(*refs):
        a_refs, b_refs = refs[0:2 * len(pairs):2], refs[1:2 * len(pairs):2]
        extra = refs[2 * len(pairs):2 * len(pairs) + n_extra]
        outs, acc_ref = refs[2 * len(pairs) + n_extra:-1], refs[-1]
        k = pl.program_id(1)
        for p in range(len(pairs)):
            @pl.when((k >= offs[p]) & (k < offs[p] + steps[p]))
            def _(p=p):
                prod = _bdot(a_refs[p][...], b_refs[p][...], NT)
                if p == 0:
                    @pl.when(k == 0)
                    def _():
                        acc_ref[...] = prod

                    @pl.when(k > 0)
                    def _():
                        acc_ref[...] += prod
                else:
                    acc_ref[...] += prod

        @pl.when(k == total - 1)
        def _():
            if norm_bwd is not None:
                _rmsnorm_bwd_tile(acc_ref[...], *extra, outs[0], outs[1], first=pl.program_id(0) == 0)
            else:
                outs[0][...] = acc_ref[...]

    in_specs, args = [], []
    for (a, b, c0), t, off, n in zip(pairs, tks, offs, steps):
        assert c0 % t == 0
        pick = lambda k, off=off, n=n: jnp.clip(k - off, 0, n - 1)
        in_specs += [pl.BlockSpec((tm, t), lambda i, k, pick=pick: (i, pick(k))),
                     pl.BlockSpec((N, t), lambda i, k, pick=pick, cb0=c0 // t: (0, pick(k) + cb0))]
        args += [a, b]
    row, vec = pl.BlockSpec((tm, N), lambda i, k: (i, 0)), pl.BlockSpec((1, N), lambda i, k: (0, 0))
    if norm_bwd is not None:
        in_specs += [row, vec, row]
        args += list(norm_bwd)
        out_specs, out_shape = [row, vec], [jax.ShapeDtypeStruct((M, N), F32), jax.ShapeDtypeStruct((1, N), F32)]
    else:
        out_specs, out_shape = [row], [jax.ShapeDtypeStruct((M, N), F32)]
    outs, landed = _call(body, comm, name=name, grid=(M // tm, total), in_specs=in_specs, out_specs=out_specs,
                         out_shape=out_shape, scratch_shapes=[pltpu.VMEM((tm, N), F32)],
                         semantics=("arbitrary", "arbitrary"), args=tuple(args))
    return (outs if norm_bwd is not None else outs[0]), landed


def _rmsnorm_fwd(x, g, name, comm=None):
    T, D = x.shape
    tt = _tile(T, LONG_ROW_TILE, SUBLANE)

    def body(x_ref, g_ref, o_ref):
        xv = x_ref[...]
        r = lax.rsqrt(jnp.mean(xv * xv, axis=-1, keepdims=True) + RMS_EPS)
        o_ref[...] = (xv * r * g_ref[...]).astype(BF16)

    outs, landed = _call(
        body, comm, name=name, grid=(T // tt,),
        in_specs=[pl.BlockSpec((tt, D), lambda i: (i, 0)), pl.BlockSpec((1, D), lambda i: (0, 0))],
        out_specs=[pl.BlockSpec((tt, D), lambda i: (i, 0))], out_shape=[jax.ShapeDtypeStruct((T, D), BF16)],
        scratch_shapes=[], semantics=("parallel",), args=(x, g))
    return outs[0], landed


def _rmsnorm_bwd(dh, x, g, dx_res, name):
    T, D = x.shape
    tt = _tile(T, LONG_ROW_TILE // 2, SUBLANE)

    def body(dh_ref, x_ref, g_ref, res_ref, dx_ref, dg_ref):
        _rmsnorm_bwd_tile(dh_ref[...], x_ref, g_ref, res_ref, dx_ref, dg_ref, first=pl.program_id(0) == 0)

    row = pl.BlockSpec((tt, D), lambda i: (i, 0))
    vec = pl.BlockSpec((1, D), lambda i: (0, 0))
    return pl.pallas_call(
        body, name=name, grid=(T // tt,),
        in_specs=[row, row, vec, row], out_specs=[row, vec],
        out_shape=[jax.ShapeDtypeStruct((T, D), F32), jax.ShapeDtypeStruct((1, D), F32)],
        compiler_params=_params("arbitrary"),
    )(dh, x, g, dx_res)


def _loss_head(y, target, name="loss_head"):
    T, D = y.shape
    tt = _tile(T, LONG_ROW_TILE, SUBLANE)

    def body(y_ref, t_ref, dy_ref, l_ref):
        e = y_ref[...] - t_ref[...]
        dy_ref[...] = e * (1.0 / D)
        s = jnp.sum(jnp.sum(e * e, axis=1, keepdims=True), axis=0, keepdims=True) * (0.5 / D)
        s = jnp.broadcast_to(s, (1, LANE))

        @pl.when(pl.program_id(0) == 0)
        def _():
            l_ref[...] = s

        @pl.when(pl.program_id(0) > 0)
        def _():
            l_ref[...] += s

    row = pl.BlockSpec((tt, D), lambda i: (i, 0))
    return pl.pallas_call(
        body, name=name, grid=(T // tt,),
        in_specs=[row, row], out_specs=[row, pl.BlockSpec((1, LANE), lambda i: (0, 0))],
        out_shape=[jax.ShapeDtypeStruct((T, D), F32), jax.ShapeDtypeStruct((1, LANE), F32)],
        compiler_params=_params("arbitrary"),
    )(y, target)


def _down(x, k):
    return pltpu.roll(x, k, 0) if k else x


def _up(x, k):
    return pltpu.roll(x, x.shape[0] - k, 0) if k else x


def _sc_fwd(proj, conv_w, name):
    T = proj.shape[0]
    tt = _tile(T, WIDE_ROW_TILE, SUBLANE)
    B = SC_BLK

    def body(p_ref, ph_ref, w_ref, o_ref):
        keep = (pl.program_id(0) > 0).astype(F32)
        for j in range(SC_NBLK):
            cb, cc, cu, cg = (slice(k * SC_W + j * B, k * SC_W + (j + 1) * B) for k in range(4))
            cw = slice(j * B, (j + 1) * B)
            z = jnp.concatenate([ph_ref[:, cc] * ph_ref[:, cu] * keep, p_ref[:, cc] * p_ref[:, cu]], axis=0)
            cz = (w_ref[2:3, cw] * z + w_ref[1:2, cw] * _down(z, 1) + w_ref[0:1, cw] * _down(z, 2))[HALO:]
            gate = p_ref[:, cg]
            o_ref[:, cw] = (p_ref[:, cb] * cz * (gate * _sigmoid(gate))).astype(BF16)

    return pl.pallas_call(
        body, name=name, grid=(T // tt,),
        in_specs=[pl.BlockSpec((tt, 4 * SC_W), lambda i: (i, 0)),
                  pl.BlockSpec((HALO, 4 * SC_W), lambda i: (jnp.maximum(i * (tt // HALO) - 1, 0), 0)),
                  pl.BlockSpec((SC_CONV, SC_W), lambda i: (0, 0))],
        out_specs=pl.BlockSpec((tt, SC_W), lambda i: (i, 0)),
        out_shape=jax.ShapeDtypeStruct((T, SC_W), BF16),
        compiler_params=_params("parallel"),
    )(proj, proj, conv_w)


def _sc_bwd(dyg, proj, conv_w, name):
    T = proj.shape[0]
    tt = _tile(T, WIDE_ROW_TILE, SUBLANE)
    nt = T // tt
    B = SC_BLK
    hb = tt // HALO

    def body(d_ref, dn_ref, p_ref, pp_ref, pn_ref, w_ref, o_ref, dw_ref):
        i = pl.program_id(0)
        keep_p = (i > 0).astype(F32)
        keep_n = (i < nt - 1).astype(F32)
        main = slice(HALO, HALO + tt)
        parts = []
        for j in range(SC_NBLK):
            cw = slice(j * B, (j + 1) * B)

            def ext(k):
                s = slice(k * SC_W + j * B, k * SC_W + (j + 1) * B)
                return s, jnp.concatenate([pp_ref[:, s] * keep_p, p_ref[:, s], pn_ref[:, s]], axis=0)

            (sb, b), (sc, c), (su, u), (sg_, gate) = ext(0), ext(1), ext(2), ext(3)
            dyg_e = jnp.concatenate([jnp.zeros((HALO, B), F32), d_ref[:, cw], dn_ref[:, cw] * keep_n], axis=0)
            w0, w1, w2 = w_ref[0:1, cw], w_ref[1:2, cw], w_ref[2:3, cw]
            z = c * u
            z1, z2 = _down(z, 1), _down(z, 2)
            cz = w2 * z + w1 * z1 + w0 * z2
            sg, dsg = _silu_and_grad(gate)
            dy = dyg_e * sg
            dcz = dy * b
            dz = w2 * dcz + w1 * _up(dcz, 1) + w0 * _up(dcz, 2)
            o_ref[:, sb] = (dy * cz)[main].astype(BF16)
            o_ref[:, sc] = (dz * u)[main].astype(BF16)
            o_ref[:, su] = (dz * c)[main].astype(BF16)
            o_ref[:, sg_] = (dyg_e * (b * cz) * dsg)[main].astype(BF16)
            dcm = dcz[main]
            parts.append(jnp.concatenate([jnp.sum(dcm * z2[main], axis=0, keepdims=True),
                                          jnp.sum(dcm * z1[main], axis=0, keepdims=True),
                                          jnp.sum(dcm * z[main], axis=0, keepdims=True)], axis=0))
        part = jnp.concatenate(parts, axis=1)

        @pl.when(i == 0)
        def _():
            dw_ref[...] = part

        @pl.when(i > 0)
        def _():
            dw_ref[...] += part

    nxt = lambda i: (jnp.minimum((i + 1) * hb, nt * hb - 1), 0)
    return pl.pallas_call(
        body, name=name, grid=(nt,),
        in_specs=[pl.BlockSpec((tt, SC_W), lambda i: (i, 0)),
                  pl.BlockSpec((HALO, SC_W), nxt),
                  pl.BlockSpec((tt, 4 * SC_W), lambda i: (i, 0)),
                  pl.BlockSpec((HALO, 4 * SC_W), lambda i: (jnp.maximum(i * hb - 1, 0), 0)),
                  pl.BlockSpec((HALO, 4 * SC_W), nxt),
                  pl.BlockSpec((SC_CONV, SC_W), lambda i: (0, 0))],
        out_specs=[pl.BlockSpec((tt, 4 * SC_W), lambda i: (i, 0)), pl.BlockSpec((SC_CONV, SC_W), lambda i: (0, 0))],
        out_shape=[jax.ShapeDtypeStruct((T, 4 * SC_W), BF16), jax.ShapeDtypeStruct((SC_CONV, SC_W), F32)],
        compiler_params=_params("arbitrary"),
    )(dyg, dyg, proj, proj, proj, conv_w)


def _split3_dot(x, m):
    hi = x.astype(BF16)
    r1 = x - hi.astype(F32)
    mid = r1.astype(BF16)
    lo = (r1 - mid.astype(F32)).astype(BF16)
    return _dot(hi, m) + _dot(mid, m) + _dot(lo, m)


def _split2_dot(x, m):
    hi = x.astype(BF16)
    lo = (x - hi.astype(F32)).astype(BF16)
    return _dot(hi, m) + _dot(lo, m)


def _head_mean_matrix():
    r, c = _iota2((LANE, LANE), 0), _iota2((LANE, LANE), 1)
    return jnp.where((r // SB_DH) == (c // SB_DH), 1.0 / SB_DH, 0.0).astype(BF16)


def _sb_prep(proj, qg2, kg2, name):
    T = proj.shape[0]
    tt = _tile(T, WIDE_ROW_TILE, SUBLANE)

    def body(p_ref, qg_ref, kg_ref, q_ref, k_ref, v_ref):
        bd = _head_mean_matrix()

        def norm(x, g, scale):
            r = lax.rsqrt(_split3_dot(x * x, bd) + RMS_EPS)
            return (x * r * g * scale).astype(BF16)

        v_ref[...] = p_ref[:, 2 * SB_W:3 * SB_W].astype(BF16)
        for p in range(SB_PAIRS):
            cols = slice(p * LANE, (p + 1) * LANE)
            q_ref[:, cols] = norm(p_ref[:, cols], qg_ref[...], SB_DH ** -0.5)
            k_ref[:, cols] = norm(p_ref[:, SB_W + p * LANE:SB_W + (p + 1) * LANE], kg_ref[...], 1.0)

    blk = pl.BlockSpec((tt, SB_W), lambda i: (i, 0))
    vec = pl.BlockSpec((1, LANE), lambda i: (0, 0))
    return pl.pallas_call(
        body, name=name, grid=(T // tt,),
        in_specs=[pl.BlockSpec((tt, 4 * SB_W), lambda i: (i, 0)), vec, vec],
        out_specs=[blk, blk, blk],
        out_shape=[jax.ShapeDtypeStruct((T, SB_W), BF16)] * 3,
        compiler_params=_params("parallel"),
    )(proj, qg2, kg2)


def _sb_prep_bwd(proj, dqn, dkn, dv, dgate, qg2, kg2, name):
    T = proj.shape[0]
    tt = _tile(T, WIDE_ROW_TILE, SUBLANE)

    def body(p_ref, dq_ref, dk_ref, dv_ref, dg_ref, qg_ref, kg_ref, o_ref, dqg_ref, dkg_ref):
        i = pl.program_id(0)
        bd = _head_mean_matrix()

        def norm_bwd(x, g, dy):
            r = lax.rsqrt(_split3_dot(x * x, bd) + RMS_EPS)
            xh = x * r
            dxh = dy * g
            m = _split3_dot(dxh * xh, bd)
            return r * (dxh - xh * m), jnp.sum(dy * xh, axis=0, keepdims=True)

        o_ref[:, 2 * SB_W:3 * SB_W] = dv_ref[...].astype(BF16)
        o_ref[:, 3 * SB_W:4 * SB_W] = dg_ref[...].astype(BF16)
        pq = pk = jnp.zeros((1, LANE), F32)
        for p in range(SB_PAIRS):
            cols, kcols = slice(p * LANE, (p + 1) * LANE), slice(SB_W + p * LANE, SB_W + (p + 1) * LANE)
            dxq, sq = norm_bwd(p_ref[:, cols], qg_ref[...], dq_ref[:, cols])
            dxk, sk = norm_bwd(p_ref[:, kcols], kg_ref[...], dk_ref[:, cols])
            o_ref[:, cols] = dxq.astype(BF16)
            o_ref[:, kcols] = dxk.astype(BF16)
            pq, pk = pq + sq, pk + sk

        @pl.when(i == 0)
        def _():
            dqg_ref[...] = pq
            dkg_ref[...] = pk

        @pl.when(i > 0)
        def _():
            dqg_ref[...] += pq
            dkg_ref[...] += pk

    blk = pl.BlockSpec((tt, SB_W), lambda i: (i, 0))
    vec = pl.BlockSpec((1, LANE), lambda i: (0, 0))
    wide = pl.BlockSpec((tt, 4 * SB_W), lambda i: (i, 0))
    return pl.pallas_call(
        body, name=name, grid=(T // tt,),
        in_specs=[wide, blk, blk, blk, blk, vec, vec],
        out_specs=[wide, vec, vec],
        out_shape=[jax.ShapeDtypeStruct((T, 4 * SB_W), BF16)] + [jax.ShapeDtypeStruct((1, LANE), F32)] * 2,
        compiler_params=_params("arbitrary"),
    )(proj, dqn, dkn, dv, dgate, qg2, kg2)


def _fold_heads(part, name):
    def body(p_ref, o_ref):
        r, c = _iota2((LANE, SB_DH), 0), _iota2((LANE, SB_DH), 1)
        fold = jnp.where((r % SB_DH) == c, 1.0, 0.0).astype(F32)
        o_ref[...] = jnp.sum(_hdot(p_ref[...], fold), axis=0, keepdims=True)

    return pl.pallas_call(body, name=name, out_shape=jax.ShapeDtypeStruct((1, SB_DH), F32))(part)


def _sb_masks():
    lane = _iota2((1, LANE), 1)
    return lane < SB_DH


def _sb_attn_fwd(qn, kn, vb, proj, name, comm=None):
    T = qn.shape[0]
    tq, tk = _tile(T, SB_TQ, SUBLANE), SB_TK
    assert tq % tk == 0

    def body(q_ref, k_ref, v_ref, g_ref, o_ref, og_ref, lt_ref, done_ref):
        i = pl.program_id(1)
        ma = _sb_masks()
        q2 = q_ref[...]
        zero = jnp.zeros_like(q2)
        qs = (jnp.where(ma, q2, zero), jnp.where(ma, zero, q2))
        upper = (_iota2((tk, tk), 0) > _iota2((tk, tk), 1)).astype(BF16)
        qpos = i * tq + _iota2((tq, tk), 0)
        nb = tq // tk

        def trip(kb_top, masked, carry):
            acc, la, lb = carry
            chains = [(b, h) for b in range(nb) for h in range(2)]
            k2s, vss, masks = [], [], []
            for b in range(nb):
                kb = kb_top - b
                rows = pl.ds(pl.multiple_of(kb * tk, tk), tk)
                k2s.append(k_ref[rows, :])
                v2 = v_ref[rows, :]
                zv = jnp.zeros_like(v2)
                vss.append((jnp.where(ma, v2, zv), jnp.where(ma, zv, v2)))
                masks.append((kb * tk + _iota2((tq, tk), 1)) < qpos if masked else None)
            zs = [_dot(qs[h], k2s[b], NT) for b, h in chains]
            ts = [jnp.log(1.0 + jnp.exp(-jnp.abs(z))) for z in zs]
            ls = [-(jnp.maximum(z, 0.0) + t) for z, t in zip(zs, ts)]
            if masked:
                ls = [jnp.where(masks[b], l, 0.0) for (b, h), l in zip(chains, ls)]
            cums = [_split2_dot(l, upper) for l in ls]
            sums = [jnp.sum(l, axis=1, keepdims=True) for l in ls]
            offs, tot = {}, [la, lb]
            for b in range(nb):
                for h in range(2):
                    offs[(b, h)] = tot[h]
                    tot[h] = tot[h] + sums[chains.index((b, h))]
            ws = [jnp.exp(jnp.minimum(z, 0.0) - t + c + offs[ch]) for ch, z, t, c in zip(chains, zs, ts, cums)]
            if masked:
                ws = [jnp.where(masks[b], w, 0.0) for (b, h), w in zip(chains, ws)]
            for (b, h), w in zip(chains, ws):
                acc = acc + _dot(w.astype(BF16), vss[b][h])
            return acc, tot[0], tot[1]

        def largest(la, lb):
            return jnp.max(jnp.maximum(la, lb))

        z1 = jnp.zeros((tq, 1), F32)
        acc, la, lb = trip((i + 1) * nb - 1, True, (jnp.zeros((tq, LANE), F32), z1, z1))

        def live(c):
            return (c[0] < i) & (c[4] > SB_DEAD)

        def more(c):
            j, acc, la, lb, _ = c
            acc, la, lb = trip((i - j) * nb - 1, False, (acc, la, lb))
            return j + 1, acc, la, lb, largest(la, lb)

        done, acc, la, lb, _ = lax.while_loop(live, more, (jnp.int32(0), acc, la, lb, largest(la, lb)))
        gate = g_ref[...]
        o_ref[...] = acc
        og_ref[...] = (acc * (gate * _sigmoid(gate))).astype(BF16)
        lt_ref[...] = jnp.where(_iota2((tq, 2), 1) == 0, la, lb)
        done_ref[...] = jnp.full((SUBLANE, LANE), done, F32)

    nq = T // tq
    qblk = pl.BlockSpec((tq, LANE), lambda p, i: (i, p))
    full = pl.BlockSpec((T, LANE), lambda p, i: (0, p))
    return _call(
        body, comm, name=name, grid=(SB_PAIRS, nq),
        in_specs=[qblk, full, full, pl.BlockSpec((tq, LANE), lambda p, i: (i, 3 * SB_PAIRS + p))],
        out_specs=[qblk, qblk, pl.BlockSpec((None, tq, 2), lambda p, i: (p, i, 0)),
                   pl.BlockSpec((None, None, SUBLANE, LANE), lambda p, i: (p, i, 0, 0))],
        out_shape=[jax.ShapeDtypeStruct((T, SB_W), F32), jax.ShapeDtypeStruct((T, SB_W), BF16),
                   jax.ShapeDtypeStruct((SB_PAIRS, T, 2), F32), jax.ShapeDtypeStruct((SB_PAIRS, nq, SUBLANE, LANE), F32)],
        scratch_shapes=[], semantics=("parallel", "parallel"), args=(qn, kn, vb, proj))


def _sb_attn_bwd(qn, kn, vb, dog, o, ltot, done, proj, name, comm=None):
    T = qn.shape[0]
    tq, tk = _tile(T, SB_TQ, SUBLANE), SB_TK

    def body(q_ref, k_ref, v_ref, dog_ref, o_ref, lt_ref, done_ref, g_ref, dq_ref, dk_ref, dv_ref, dgate_ref):
        i = pl.program_id(1)
        first_trip = i - jnp.max(done_ref[...]).astype(jnp.int32)

        @pl.when(i == 0)
        def _():
            dk_ref[...] = jnp.zeros_like(dk_ref)
            dv_ref[...] = jnp.zeros_like(dv_ref)

        ma = _sb_masks()
        gate, o2, dog2 = g_ref[...], o_ref[...], dog_ref[...]
        sg, dsg = _silu_and_grad(gate)
        do2 = dog2 * sg
        dgate_ref[...] = dog2 * o2 * dsg
        lt = lt_ref[...]
        first = _iota2((tq, 2), 1) == 0
        ltots = (jnp.sum(jnp.where(first, lt, 0.0), axis=1, keepdims=True),
                 jnp.sum(jnp.where(first, 0.0, lt), axis=1, keepdims=True))
        q2 = q_ref[...]
        zq = jnp.zeros_like(q2)
        qs = (jnp.where(ma, q2, zq), jnp.where(ma, zq, q2))
        dob = do2.astype(BF16)
        dos = (jnp.where(ma, dob, zq), jnp.where(ma, zq, dob))
        upto = (_iota2((tk, tk), 0) <= _iota2((tk, tk), 1)).astype(BF16)
        before = (_iota2((tk, tk), 0) < _iota2((tk, tk), 1)).astype(BF16)
        qpos = i * tq + _iota2((tq, tk), 0)
        nb = tq // tk

        def trip(kb_bot, masked, carry):
            dq, la, lb, ea, eb = carry
            chains = [(b, h) for b in range(nb) for h in range(2)]
            rows, k2s, v2s, kss, masks = [], [], [], [], []
            for b in range(nb):
                kb = kb_bot + b
                rows.append(pl.ds(pl.multiple_of(kb * tk, tk), tk))
                k2 = k_ref[rows[b], :]
                zk = jnp.zeros_like(k2)
                k2s.append(k2)
                v2s.append(v_ref[rows[b], :])
                kss.append((jnp.where(ma, k2, zk), jnp.where(ma, zk, k2)))
                masks.append((kb * tk + _iota2((tq, tk), 1)) < qpos if masked else None)

            def keep(vals):
                return [jnp.where(masks[b], x, 0.0) for (b, h), x in zip(chains, vals)] if masked else vals

            zs = [_dot(qs[h], k2s[b], NT) for b, h in chains]
            dws = [_dot(dos[h], v2s[b], NT) for b, h in chains]
            ts = [jnp.log(1.0 + jnp.exp(-jnp.abs(z))) for z in zs]
            ls = keep([-(jnp.maximum(z, 0.0) + t) for z, t in zip(zs, ts)])
            lps = [jnp.minimum(z, 0.0) - t for z, t in zip(zs, ts)]
            cums = [_split3_dot(l, upto) for l in ls]
            lsums = [jnp.sum(l, axis=1, keepdims=True) for l in ls]
            offs, tot = {}, [la, lb]
            for b in range(nb):
                for h in range(2):
                    offs[(b, h)] = tot[h]
                    tot[h] = tot[h] + lsums[chains.index((b, h))]
            ws = keep([jnp.exp(lp + (ltots[h] - (offs[(b, h)] + c))) for (b, h), lp, c in zip(chains, lps, cums)])
            es = [dw * w for dw, w in zip(dws, ws)]
            ecums = [_split2_dot(e, before) for e in es]
            esums = [jnp.sum(e, axis=1, keepdims=True) for e in es]
            eoffs, etot = {}, [ea, eb]
            for b in range(nb):
                for h in range(2):
                    eoffs[(b, h)] = etot[h]
                    etot[h] = etot[h] + esums[chains.index((b, h))]
            dzs = keep([e - jnp.exp(lp) * (e + eoffs[ch] + ec) for ch, e, lp, ec in zip(chains, es, lps, ecums)])
            dzs = [dz.astype(BF16) for dz in dzs]
            wbs = [w.astype(BF16) for w in ws]
            for (b, h), dz in zip(chains, dzs):
                dq = dq + _dot(dz, kss[b][h])
            for b in range(nb):
                ia, ib = chains.index((b, 0)), chains.index((b, 1))
                dk_ref[rows[b], :] += _dot(dzs[ia], qs[0], TN) + _dot(dzs[ib], qs[1], TN)
                dv_ref[rows[b], :] += _dot(wbs[ia], dos[0], TN) + _dot(wbs[ib], dos[1], TN)
            return dq, tot[0], tot[1], etot[0], etot[1]

        z1 = jnp.zeros((tq, 1), F32)
        carry = lax.fori_loop(first_trip, i, lambda j, c: trip(j * nb, False, c),
                              (jnp.zeros((tq, LANE), F32), z1, z1, z1, z1))
        dq = trip(i * nb, True, carry)[0]
        dq_ref[...] = dq * (SB_DH ** -0.5)

    qblk = pl.BlockSpec((tq, LANE), lambda p, i: (i, p))
    full = pl.BlockSpec((T, LANE), lambda p, i: (0, p))
    return _call(
        body, comm, name=name, grid=(SB_PAIRS, T // tq),
        in_specs=[qblk, full, full, qblk, qblk, pl.BlockSpec((None, tq, 2), lambda p, i: (p, i, 0)),
                  pl.BlockSpec((None, None, SUBLANE, LANE), lambda p, i: (p, i, 0, 0)),
                  pl.BlockSpec((tq, LANE), lambda p, i: (i, 3 * SB_PAIRS + p))],
        out_specs=[qblk, full, full, qblk],
        out_shape=[jax.ShapeDtypeStruct((T, SB_W), F32)] * 4,
        scratch_shapes=[], semantics=("parallel", "arbitrary"), args=(qn, kn, vb, dog, o, ltot, done, proj))


def _dn_conv(ext, w_ref, cw):
    return (w_ref[3:4, cw] * ext + w_ref[2:3, cw] * _down(ext, 1) + w_ref[1:2, cw] * _down(ext, 2)
            + w_ref[0:1, cw] * _down(ext, 3))


def _dn_gates(a_in, b_in, a_log, dt_bias, name):
    T, H = a_in.shape
    C = DN_CHUNK

    def body(a_ref, b_ref, al_ref, dt_ref, g_ref, beta_ref):
        beta_ref[...] = _sigmoid(b_ref[...])
        g_ref[...] = -jnp.exp(al_ref[...]) * _softplus(a_ref[...] + dt_ref[...])
        tri = (_iota2((C, C), 0) >= _iota2((C, C), 1)).astype(F32)

        def chunk(n, carry):
            rows = pl.ds(pl.multiple_of(n * C, C), C)
            g_ref[rows, :] = _hdot(tri, g_ref[rows, :])
            return carry

        lax.fori_loop(0, T // C, chunk, 0)

    return pl.pallas_call(body, name=name, out_shape=[jax.ShapeDtypeStruct((T, H), F32)] * 2)(a_in, b_in, a_log, dt_bias)


def _dn_gates_bwd(dg, dbeta, a_in, b_in, a_log, dt_bias, name):
    T, H = a_in.shape
    C = DN_CHUNK

    def body(dg_ref, db_ref, a_ref, b_ref, al_ref, dt_ref, da_ref, dbi_ref, dal_ref, ddt_ref):
        tri_t = (_iota2((C, C), 0) <= _iota2((C, C), 1)).astype(F32)

        def chunk(n, carry):
            rows = pl.ds(pl.multiple_of(n * C, C), C)
            da_ref[rows, :] = _hdot(tri_t, dg_ref[rows, :])
            return carry

        lax.fori_loop(0, T // C, chunk, 0)
        dla = da_ref[...]
        x = a_ref[...] + dt_ref[...]
        ea = jnp.exp(al_ref[...])
        da = dla * (-ea) * _sigmoid(x)
        da_ref[...] = da
        dal_ref[...] = jnp.sum(dla * (-ea * _softplus(x)), axis=0, keepdims=True)
        ddt_ref[...] = jnp.sum(da, axis=0, keepdims=True)
        beta = _sigmoid(b_ref[...])
        dbi_ref[...] = db_ref[...] * beta * (1.0 - beta)

    return pl.pallas_call(
        body, name=name,
        out_shape=[jax.ShapeDtypeStruct((T, H), F32)] * 2 + [jax.ShapeDtypeStruct((1, H), F32)] * 2,
    )(dg, dbeta, a_in, b_in, a_log, dt_bias)


def _dn_chunk_terms(q, k, gc, bc):
    C = DN_CHUNK
    r, c = _iota2((C, C), 0), _iota2((C, C), 1)
    lower, strict, eye = r >= c, r > c, r == c
    grow = jnp.sum(jnp.where(eye, gc, 0.0), axis=0, keepdims=True)
    decay = jnp.where(lower, jnp.exp(jnp.where(lower, gc - grow, 0.0)), 0.0)
    last = _iota2((C, 1), 0) == C - 1
    gl = jnp.sum(jnp.where(last, gc, 0.0), axis=0, keepdims=True)
    eg = jnp.exp(gc)
    egl = jnp.exp(gl - gc)
    kb = k * bc
    lmat = jnp.where(strict, _bdot(kb, k, NT) * decay, 0.0)
    aqk = jnp.where(lower, _bdot(q, k, NT) * decay, 0.0)
    return dict(lower=lower, strict=strict, eye=eye, last=last, decay=decay, gl=gl, eg=eg, egl=egl, kb=kb,
                lmat=lmat, aqk=aqk, qd=q * eg, kd=k * egl)


def _split(x):
    hi = x.astype(BF16)
    return hi, (x - hi.astype(F32)).astype(BF16)


def _x3dot(a, b, dims=NN):
    ah, al = a if isinstance(a, tuple) else _split(a)
    bh, bl = b if isinstance(b, tuple) else _split(b)
    return _dot(ah, bh, dims) + (_dot(ah, bl, dims) + _dot(al, bh, dims))


def _interleave(gens):
    for _ in itertools.zip_longest(*gens):
        pass


def _unit_lower_inverse_steps(lmat, eye, out):
    ident = jnp.where(eye, 1.0, 0.0).astype(F32)
    m = -lmat
    inv = ident + m
    for _ in range(int(math.log2(DN_CHUNK)) - 1):
        ms = _split(m)
        m = _x3dot(ms, ms)
        yield
        inv = inv + _x3dot(inv, m)
        yield
    out["tm"] = inv


def _dn_chunk_fwd(pqkv, conv_w, g, beta, pgate, gn, name, comm=None):
    T = pqkv.shape[0]
    C, H = DN_CHUNK, DN_HEADS
    N = T // C
    B = DN_PREP_BLK
    nq, nqk = DN_QK_W // B, 2 * DN_QK_W // B

    def step(p_ref, cw_ref, g_ref, b_ref, pg_ref, gn_ref, act_out, o_ref, og_ref, s_out, t_out, vn_out, u_out, w_out,
             s_scr, tail_scr, a_ref, a_next):
        head_lane = _iota2((C, H), 1)

        def prepare(cb):
            cw = slice(cb * B, (cb + 1) * B)
            ext = jnp.concatenate([tail_scr[:, cw], p_ref[:, cw]], axis=0)
            c = _dn_conv(ext, cw_ref, cw)[HALO:]
            yield
            a = c * _sigmoid(c)
            if cb >= nqk:
                a_next[:, cw] = a
                act_out[:, cw] = a
                return
            scale = DN_DK ** -0.5 if cb < nq else 1.0
            for hh in range(B // DN_DK):
                yield
                ah = a[:, hh * DN_DK:(hh + 1) * DN_DK]
                val = ah * (lax.rsqrt(jnp.sum(ah * ah, axis=-1, keepdims=True) + L2_EPS) * scale)
                cols = slice(cb * B + hh * DN_DK, cb * B + (hh + 1) * DN_DK)
                a_next[:, cols] = val
                act_out[:, cols] = val

        def head(hh):
            qs, vs = slice(hh * DN_DK, (hh + 1) * DN_DK), slice(hh * DN_DV, (hh + 1) * DN_DV)
            q, k, v = a_ref[:, qs], a_ref[:, DN_QK_W + hh * DN_DK:DN_QK_W + (hh + 1) * DN_DK], \
                a_ref[:, 2 * DN_QK_W + hh * DN_DV:2 * DN_QK_W + (hh + 1) * DN_DV]
            gc = jnp.sum(jnp.where(head_lane == hh, g_ref[...], 0.0), axis=1, keepdims=True)
            bc = jnp.sum(jnp.where(head_lane == hh, b_ref[...], 0.0), axis=1, keepdims=True)
            t = _dn_chunk_terms(q, k, gc, bc)
            yield
            res = {}
            yield from _unit_lower_inverse_steps(t["lmat"], t["eye"], res)
            tms = _split(res["tm"])
            u = _x3dot(tms, v * bc)
            yield
            w = _x3dot(tms, t["kb"] * t["eg"])
            yield
            s = s_scr[hh]
            s_out[hh] = s
            t_out[hh] = res["tm"]
            sb = s.astype(BF16)
            vn = u - _dot(w.astype(BF16), sb)
            yield
            o = _dot(t["qd"].astype(BF16), sb) + _bdot(t["aqk"], vn)
            yield
            s_scr[hh] = s * jnp.exp(t["gl"]) + _bdot(t["kd"], vn, TN)
            vn_out[:, vs] = vn
            u_out[:, vs] = u
            w_out[:, qs] = w
            o_ref[:, vs] = o
            gate = pg_ref[:, vs]
            r = lax.rsqrt(jnp.mean(o * o, axis=-1, keepdims=True) + RMS_EPS)
            og_ref[:, vs] = (o * r * gn_ref[...] * (gate * _sigmoid(gate))).astype(BF16)

        _interleave([prepare(cb) for cb in range(DN_CONV_W // B)] + [head(hh) for hh in range(H)])

        @pl.when(pl.program_id(0) < N - 1)
        def _():
            tail_scr[...] = p_ref[C - HALO:C, :]

    def body(*refs):
        s = pl.program_id(0)
        io, (s_scr, tail_scr, buf_a, buf_b) = refs[:-4], refs[-4:]

        @pl.when(s == 0)
        def _():
            tail_scr[...] = jnp.zeros_like(tail_scr)
            buf_b[...] = jnp.zeros_like(buf_b)

        @pl.when(s <= 1)
        def _():
            s_scr[...] = jnp.zeros_like(s_scr)

        @pl.when(s % 2 == 0)
        def _():
            step(*io, s_scr, tail_scr, buf_b, buf_a)

        @pl.when(s % 2 == 1)
        def _():
            step(*io, s_scr, tail_scr, buf_a, buf_b)

    nxt = lambda w: pl.BlockSpec((C, w), lambda s: (jnp.minimum(s, N - 1), 0))
    cur = lambda w: pl.BlockSpec((C, w), lambda s: (jnp.maximum(s - 1, 0), 0))
    per_chunk = lambda a, b: pl.BlockSpec((H, None, a, b), lambda s: (0, jnp.maximum(s - 1, 0), 0, 0))
    return _call(
        body, comm, name=name, grid=(N + 1,),
        in_specs=[nxt(DN_CONV_W), pl.BlockSpec((DN_CONV, DN_CONV_W), lambda s: (0, 0)), cur(H), cur(H), cur(DN_V_W),
                  pl.BlockSpec((1, DN_DV), lambda s: (0, 0))],
        out_specs=[nxt(DN_CONV_W), cur(DN_V_W), cur(DN_V_W), per_chunk(DN_DK, DN_DV), per_chunk(C, C),
                   cur(DN_V_W), cur(DN_V_W), cur(DN_QK_W)],
        out_shape=[jax.ShapeDtypeStruct((T, DN_CONV_W), F32),
                   jax.ShapeDtypeStruct((T, DN_V_W), F32), jax.ShapeDtypeStruct((T, DN_V_W), BF16),
                   jax.ShapeDtypeStruct((H, N, DN_DK, DN_DV), F32),
                   jax.ShapeDtypeStruct((H, N, C, C), F32),
                   jax.ShapeDtypeStruct((T, DN_V_W), F32),
                   jax.ShapeDtypeStruct((T, DN_V_W), F32),
                   jax.ShapeDtypeStruct((T, DN_QK_W), F32)],
        scratch_shapes=[pltpu.VMEM((H, DN_DK, DN_DV), F32), pltpu.VMEM((HALO, DN_CONV_W), F32),
                        pltpu.VMEM((C, DN_CONV_W), F32), pltpu.VMEM((C, DN_CONV_W), F32)],
        semantics=("arbitrary",), args=(pqkv, conv_w, g, beta, pgate, gn))


def _dn_chunk_bwd(pqkv, conv_w, act, g, beta, s_saved, tm_saved, vn_saved, u_saved, w_saved, dog, o_raw, pgate, gn,
                  name, comm=None):
    T = act.shape[0]
    C, H = DN_CHUNK, DN_HEADS
    N = T // C
    assert N % 2 == 0
    B = DN_PREP_BLK
    nq, nqk = DN_QK_W // B, 2 * DN_QK_W // B
    main = slice(HALO, HALO + C)

    def prepare_bwd(cb, p_ref, pp_ref, pn_ref, cw_ref, dread, dnext_scr, dp_ref, conv_parts):
        s = pl.program_id(0)
        keep_p = (N - s > 0).astype(F32)
        keep_n = (s > 1).astype(F32)
        cw = slice(cb * B, (cb + 1) * B)
        ext = jnp.concatenate([pp_ref[:, cw] * keep_p, p_ref[:, cw], pn_ref[:, cw]], axis=0)
        c = _dn_conv(ext, cw_ref, cw)
        yield
        sg = _sigmoid(c)
        da_dc = sg * (1.0 + c * (1.0 - sg))
        d_up = jnp.concatenate([jnp.zeros((HALO, B), F32), dread[:, cw], dnext_scr[:, cw] * keep_n], axis=0)
        if cb < nqk:
            a = c * sg
            scale = DN_DK ** -0.5 if cb < nq else 1.0
            normed = []
            for hh in range(B // DN_DK):
                yield
                cols = slice(hh * DN_DK, (hh + 1) * DN_DK)
                ah = a[:, cols]
                r = lax.rsqrt(jnp.sum(ah * ah, axis=-1, keepdims=True) + L2_EPS)
                y = ah * r
                dy = d_up[:, cols] * scale
                normed.append(r * (dy - y * jnp.sum(dy * y, axis=-1, keepdims=True)))
            d_up = jnp.concatenate(normed, axis=1)
        yield
        dc = d_up * da_dc
        dp = (cw_ref[3:4, cw] * dc + cw_ref[2:3, cw] * _up(dc, 1) + cw_ref[1:2, cw] * _up(dc, 2)
              + cw_ref[0:1, cw] * _up(dc, 3))
        dp_ref[:, cw] = dp[main].astype(BF16)
        yield
        dcm = dc[main]
        conv_parts[cb] = jnp.concatenate([jnp.sum(dcm * _down(ext, 3 - k)[main], axis=0, keepdims=True)
                                          for k in range(DN_CONV)], axis=0)

    def finish_prepare(conv_parts, dconv_ref, dread, dnext_scr):
        part = jnp.concatenate([conv_parts[cb] for cb in range(DN_CONV_W // B)], axis=1)

        @pl.when(pl.program_id(0) == 0)
        def _():
            dconv_ref[...] = part

        @pl.when(pl.program_id(0) > 0)
        def _():
            dconv_ref[...] += part

        dnext_scr[...] = dread[0:HALO, :]

    def step(a_ref, g_ref, b_ref, s_ref, t_ref, vn_ref, u_ref, w_ref, dog_ref, o_ref, pg_ref, gn_ref,
             p_ref, pp_ref, pn_ref, cw_ref, dp_ref, dconv_ref, dg_ref, db_ref, dgate_ref, dgn_ref,
             ds_scr, dnext_scr, dwrite, dread):
        head_lane = _iota2((C, H), 1)
        dg_cols, db_cols, dgn_parts, conv_parts = {}, {}, {}, {}

        def output_gate_bwd(hh, vs):
            d, o, gate, gn_v = dog_ref[:, vs], o_ref[:, vs], pg_ref[:, vs], gn_ref[...]
            sg, dsg = _silu_and_grad(gate)
            r = lax.rsqrt(jnp.mean(o * o, axis=-1, keepdims=True) + RMS_EPS)
            n = o * r
            dy = d * sg
            dgate_ref[:, vs] = (d * (n * gn_v) * dsg).astype(BF16)
            dn = dy * gn_v
            dgn_parts[hh] = jnp.sum(dy * n, axis=0, keepdims=True)
            return r * (dn - n * jnp.mean(dn * n, axis=-1, keepdims=True))

        def head(hh):
            qs, vs = slice(hh * DN_DK, (hh + 1) * DN_DK), slice(hh * DN_DV, (hh + 1) * DN_DV)
            ks = slice(DN_QK_W + hh * DN_DK, DN_QK_W + (hh + 1) * DN_DK)
            vas = slice(2 * DN_QK_W + hh * DN_DV, 2 * DN_QK_W + (hh + 1) * DN_DV)
            q, k, v = a_ref[:, qs], a_ref[:, ks], a_ref[:, vas]
            gc = jnp.sum(jnp.where(head_lane == hh, g_ref[...], 0.0), axis=1, keepdims=True)
            bc = jnp.sum(jnp.where(head_lane == hh, b_ref[...], 0.0), axis=1, keepdims=True)
            t = _dn_chunk_terms(q, k, gc, bc)
            yield
            lower, strict, eye = t["lower"], t["strict"], t["eye"]
            decay, eg, egl, kb, qd, kd = t["decay"], t["eg"], t["egl"], t["kb"], t["qd"], t["kd"]
            s, tm, vn, u, w = s_ref[hh], t_ref[hh], vn_ref[:, vs], u_ref[:, vs], w_ref[:, qs]
            d_o = output_gate_bwd(hh, vs)
            ds_next = ds_scr[hh]
            egl_tot = jnp.exp(t["gl"])
            dob, sb, dsb, vnb = d_o.astype(BF16), s.astype(BF16), ds_next.astype(BF16), vn.astype(BF16)

            dvn = _bdot(t["aqk"], dob, TN) + _bdot(kd, dsb)
            yield
            daqk = jnp.where(lower, _dot(dob, vnb, NT), 0.0)
            dqd = _dot(dob, sb, NT)
            dkd = _dot(vnb, dsb, NT)
            yield
            dvnb = dvn.astype(BF16)
            ds_scr[hh] = _bdot(qd, dob, TN) + egl_tot * ds_next - _bdot(w, dvnb, TN)
            dgl = egl_tot * jnp.sum(jnp.sum(s * ds_next, axis=1, keepdims=True), axis=0, keepdims=True)
            dw = -_dot(dvnb, sb, NT)
            yield
            tms = _split(tm)
            dru = _x3dot(tms, dvn, TN)
            drw = _x3dot(tms, dw, TN)
            yield
            dl = -jnp.where(strict, _x3dot(dru, u, NT) + _x3dot(drw, w, NT), 0.0)
            yield
            dkk = (dl * decay).astype(BF16)
            dqk = (daqk * decay).astype(BF16)
            dkb = _bdot(dkk, k) + drw * eg
            yield
            dwrite[:, ks] = _bdot(dkk, kb, TN) + _bdot(dqk, q, TN) + dkd * egl + dkb * bc
            dwrite[:, qs] = _bdot(dqk, k) + dqd * eg
            dwrite[:, vas] = dru * bc
            yield
            db_cols[hh] = jnp.sum(dru * v, axis=1, keepdims=True) + jnp.sum(dkb * k, axis=1, keepdims=True)
            pm = dl * t["lmat"] + daqk * t["aqk"]
            col_as_col = jnp.sum(jnp.where(eye, jnp.sum(pm, axis=0, keepdims=True), 0.0), axis=1, keepdims=True)
            kdsum = jnp.sum(dkd * kd, axis=1, keepdims=True)
            dgc = (jnp.sum(pm, axis=1, keepdims=True) - col_as_col + jnp.sum(dqd * qd, axis=1, keepdims=True)
                   - kdsum + jnp.sum(drw * (kb * eg), axis=1, keepdims=True))
            dgl = dgl + jnp.sum(kdsum, axis=0, keepdims=True)
            dg_cols[hh] = dgc + jnp.where(t["last"], dgl, 0.0)

        _interleave([head(hh) for hh in range(H)]
                    + [prepare_bwd(cb, p_ref, pp_ref, pn_ref, cw_ref, dread, dnext_scr, dp_ref, conv_parts)
                       for cb in range(DN_CONV_W // B)])
        dg_ref[...] = sum(jnp.where(head_lane == hh, dg_cols[hh], 0.0) for hh in range(H))
        db_ref[...] = sum(jnp.where(head_lane == hh, db_cols[hh], 0.0) for hh in range(H))
        dgn_part = sum(dgn_parts[hh] for hh in range(H))

        @pl.when(pl.program_id(0) == 0)
        def _():
            dgn_ref[...] = dgn_part

        @pl.when(pl.program_id(0) > 0)
        def _():
            dgn_ref[...] += dgn_part

        finish_prepare(conv_parts, dconv_ref, dread, dnext_scr)

    def body(*refs):
        s = pl.program_id(0)
        io, (ds_scr, dnext_scr, buf_a, buf_b) = refs[:-4], refs[-4:]
        p_ref, pp_ref, pn_ref, cw_ref, dp_ref, dconv_ref = refs[12:18]

        @pl.when(s == 0)
        def _():
            ds_scr[...] = jnp.zeros_like(ds_scr)
            dnext_scr[...] = jnp.zeros_like(dnext_scr)
            buf_b[...] = jnp.zeros_like(buf_b)

        @pl.when((s < N) & (s % 2 == 0))
        def _():
            step(*io, ds_scr, dnext_scr, buf_a, buf_b)

        @pl.when((s < N) & (s % 2 == 1))
        def _():
            step(*io, ds_scr, dnext_scr, buf_b, buf_a)

        @pl.when(s == N)
        def _():
            conv_parts = {}
            _interleave([prepare_bwd(cb, p_ref, pp_ref, pn_ref, cw_ref, buf_b, dnext_scr, dp_ref, conv_parts)
                         for cb in range(DN_CONV_W // B)])
            finish_prepare(conv_parts, dconv_ref, buf_b, dnext_scr)

    cc = lambda s: jnp.maximum(N - 1 - s, 0)
    pc = lambda s: jnp.clip(N - s, 0, N - 1)
    row = lambda w: pl.BlockSpec((C, w), lambda s: (cc(s), 0))
    per_chunk = lambda a, b: pl.BlockSpec((H, None, a, b), lambda s: (0, cc(s), 0, 0))
    vec = pl.BlockSpec((1, DN_DV), lambda s: (0, 0))
    per_c = C // HALO
    conv_spec = pl.BlockSpec((DN_CONV, DN_CONV_W), lambda s: (0, 0))
    return _call(
        body, comm, name=name, grid=(N + 1,),
        in_specs=[row(DN_CONV_W), row(H), row(H), per_chunk(DN_DK, DN_DV), per_chunk(C, C),
                  row(DN_V_W), row(DN_V_W), row(DN_QK_W), row(DN_V_W), row(DN_V_W), row(DN_V_W), vec,
                  pl.BlockSpec((C, DN_CONV_W), lambda s: (pc(s), 0)),
                  pl.BlockSpec((HALO, DN_CONV_W), lambda s: (jnp.maximum(pc(s) * per_c - 1, 0), 0)),
                  pl.BlockSpec((HALO, DN_CONV_W), lambda s: (jnp.minimum((pc(s) + 1) * per_c, N * per_c - 1), 0)),
                  conv_spec],
        out_specs=[pl.BlockSpec((C, DN_CONV_W), lambda s: (pc(s), 0)), conv_spec, row(H), row(H), row(DN_V_W), vec],
        out_shape=[jax.ShapeDtypeStruct((T, DN_CONV_W), BF16), jax.ShapeDtypeStruct((DN_CONV, DN_CONV_W), F32),
                   jax.ShapeDtypeStruct((T, H), F32), jax.ShapeDtypeStruct((T, H), F32),
                   jax.ShapeDtypeStruct((T, DN_V_W), BF16), jax.ShapeDtypeStruct((1, DN_DV), F32)],
        scratch_shapes=[pltpu.VMEM((H, DN_DK, DN_DV), F32), pltpu.VMEM((HALO, DN_CONV_W), F32),
                        pltpu.VMEM((C, DN_CONV_W), F32), pltpu.VMEM((C, DN_CONV_W), F32)],
        semantics=("arbitrary",),
        args=(act, g, beta, s_saved, tm_saved, vn_saved, u_saved, w_saved, dog, o_raw, pgate, gn,
              pqkv, pqkv, pqkv, conv_w))


def _dn_split_w_in(w):
    return w, jnp.pad(w[:, DN_CONV_W + DN_V_W:], ((0, 0), (0, DN_AB_PAD - 2 * DN_HEADS)))


def _out_proj(og, w_out, x_res, next_g, name):
    if next_g is None:
        return _matmul(og, w_out, "nn", name, add=x_res), None
    return tuple(_matmul(og, w_out, "nn", name, add=x_res, norm_fwd=next_g, tm=NORM_FUSED_TM))


def _dn_layer_fwd(h, wts, conv_w, a_log, dt_bias, gn, w_out, x_res, tag, comm=None, next_g=None):
    w_in, wab = wts
    H = DN_HEADS
    pqkv = _matmul(h, w_in, "nn", tag + "_pqkv", b_cols=(0, DN_CONV_W))
    pgate = _matmul(h, w_in, "nn", tag + "_pgate", b_cols=(DN_CONV_W, DN_V_W))
    pab = _matmul(h, wab, "nn", tag + "_pab")
    a_in, b_in = pab[:, :H], pab[:, H:2 * H]
    g, beta = _dn_gates(a_in, b_in, a_log, dt_bias, tag + "_gates")
    (act, o_raw, og, s_sv, tm_sv, vn_sv, u_sv, w_sv), landed = _dn_chunk_fwd(pqkv, conv_w, g, beta, pgate, gn,
                                                                             tag + "_chunk_fwd", comm)
    if callable(w_out):
        w_out = w_out(landed)
    y = _out_proj(og, w_out, x_res, next_g, tag + "_out")
    saved = dict(h=h, wts=wts, conv_w=conv_w, a_log=a_log, dt_bias=dt_bias, gn=gn, w_out=w_out, pqkv=pqkv, pgate=pgate,
                 a_in=a_in, b_in=b_in, g=g, beta=beta, act=act, o_raw=o_raw, chunk=(s_sv, tm_sv, vn_sv, u_sv, w_sv), og=og)
    return y, saved, landed


def _dn_layer_bwd(dout, sv, tag, norm, comm_of=None, late_comm_of=None):
    w_in, wab = sv["wts"]
    h = sv["h"]
    dog = _matmul(dout, sv["w_out"], "nt", tag + "_dog")
    dw_out = _matmul(sv["og"], dout, "tn", tag + "_dwout", out_dtype=BF16)
    comm = comm_of(dw_out) if comm_of is not None else None
    (dpqkv, dconv, dg, dbeta, dgate, dgn), landed = _dn_chunk_bwd(
        sv["pqkv"], sv["conv_w"], sv["act"], sv["g"], sv["beta"], *sv["chunk"], dog, sv["o_raw"], sv["pgate"], sv["gn"],
        tag + "_chunk_bwd", comm)
    da_in, db_in, da_log, ddt = _dn_gates_bwd(dg, dbeta, sv["a_in"], sv["b_in"], sv["a_log"], sv["dt_bias"],
                                              tag + "_gates_bwd")
    dpab = jnp.pad(jnp.concatenate([da_in, db_in], axis=1), ((0, 0), (0, DN_AB_PAD - 2 * DN_HEADS)))
    dwqkv = _matmul(h, dpqkv, "tn", tag + "_dwqkv", out_dtype=BF16)
    dwgate = _matmul(h, dgate, "tn", tag + "_dwgate", out_dtype=BF16)
    dwab = _matmul(h, dpab, "tn", tag + "_dwab", out_dtype=BF16)
    dw_in = jnp.concatenate([dwqkv, dwgate, dwab[:, :2 * DN_HEADS]], axis=1)
    grads = dict(dn_w_in=dw_in, dn_conv_w=dconv, dn_a_log=da_log, dn_dt_bias=ddt, dn_o_norm_g=dgn, dn_w_out=dw_out)
    dx, landed_late = _matmul_nt_sum([(dpqkv, w_in, 0), (dgate, w_in, DN_CONV_W), (dpab, wab, 0)], tag + "_dh",
                                     late_comm_of(grads) if late_comm_of is not None else None, norm_bwd=norm,
                                     tm=NORM_FUSED_TM if norm is not None else 1024)
    return dx, grads, landed, landed_late


def _sb_layer_fwd(h, w_in, qg, kg, w_out, x_res, tag, comm=None, next_g=None):
    qg2, kg2 = jnp.tile(qg, (1, 2)), jnp.tile(kg, (1, 2))
    proj = _matmul(h, w_in, "nn", tag + "_proj", blocked_b=True)
    qn, kn, vb = _sb_prep(proj, qg2, kg2, tag + "_prep")
    (o, og, ltot, done), landed = _sb_attn_fwd(qn, kn, vb, proj, tag + "_attn_fwd", comm)
    y = _out_proj(og, w_out, x_res, next_g, tag + "_out")
    saved = dict(h=h, w_in=w_in, qg2=qg2, kg2=kg2, w_out=w_out, proj=proj, qn=qn, kn=kn, vb=vb, o=o, og=og, ltot=ltot,
                 done=done)
    return y, saved, landed


def _sb_layer_bwd(dout, sv, tag, comm=None):
    dog = _matmul(dout, sv["w_out"], "nt", tag + "_dog")
    dw_out = _matmul(sv["og"], dout, "tn", tag + "_dwout", out_dtype=BF16)
    (dqn, dkn, dv, dgate), landed = _sb_attn_bwd(sv["qn"], sv["kn"], sv["vb"], dog, sv["o"], sv["ltot"], sv["done"],
                                                 sv["proj"], tag + "_attn_bwd", comm)
    dproj, dqgp, dkgp = _sb_prep_bwd(sv["proj"], dqn, dkn, dv, dgate, sv["qg2"], sv["kg2"], tag + "_prep_bwd")
    dw_in = _matmul(sv["h"], dproj, "tn", tag + "_dwin", out_dtype=BF16, blocked_out=N_DEV)
    dh = _matmul(dproj, sv["w_in"], "nt", tag + "_dh", blocked_b=True)
    dqg = _fold_heads(dqgp, tag + "_dqg")
    dkg = _fold_heads(dkgp, tag + "_dkg")
    return dh, dict(sb_w_in=dw_in, sb_q_norm_g=dqg, sb_k_norm_g=dkg, sb_w_out=dw_out), landed


def _sc_layer_fwd(h, w_in, conv_w, w_out, x_res, tag, next_g=None):
    proj = _matmul(h, w_in, "nn", tag + "_proj", blocked_b=True)
    yg = _sc_fwd(proj, conv_w, tag + "_fwd")
    y = _out_proj(yg, w_out, x_res, next_g, tag + "_out")
    return y, dict(h=h, w_in=w_in, conv_w=conv_w, w_out=w_out, proj=proj, yg=yg)


def _sc_layer_bwd(dout, sv, tag):
    dyg = _matmul(dout, sv["w_out"], "nt", tag + "_dyg")
    dw_out = _matmul(sv["yg"], dout, "tn", tag + "_dwout", out_dtype=BF16)
    dproj, dconv = _sc_bwd(dyg, sv["proj"], sv["conv_w"], tag + "_bwd")
    dw_in = _matmul(sv["h"], dproj, "tn", tag + "_dwin", out_dtype=BF16, blocked_out=N_DEV)
    dh = _matmul(dproj, sv["w_in"], "nt", tag + "_dh", blocked_b=True)
    return dh, dict(sc_w_in=dw_in, sc_conv_w=dconv, sc_w_out=dw_out)


def _adamw(w, m, v, parts, name):
    L, R, C = w.shape
    tr = _tile(R, 128, SUBLANE)

    def body(*refs):
        w_ref, m_ref, v_ref = refs[:3]
        g_ref, d_ref, nm_ref, nv_ref = refs[3 + L:]

        def update(p_ref):
            g = p_ref[0].astype(F32)
            for s in range(1, N_DEV):
                g = g + p_ref[s].astype(F32)
            m2 = ADAM_B1 * m_ref[...] + (1.0 - ADAM_B1) * g
            v2 = ADAM_B2 * v_ref[...] + (1.0 - ADAM_B2) * (g * g)
            m_hat = m2 / (1.0 - ADAM_B1 ** ADAM_STEP)
            v_hat = v2 / (1.0 - ADAM_B2 ** ADAM_STEP)
            g_ref[...] = g
            d_ref[...] = -ADAM_LR * (m_hat / (jnp.sqrt(v_hat) + ADAM_EPS) + ADAM_WD * w_ref[...])
            nm_ref[...] = m2
            nv_ref[...] = v2

        for layer in range(L):
            pl.when(pl.program_id(0) == layer)(functools.partial(update, refs[3 + layer]))

    blk = pl.BlockSpec((None, tr, C), lambda l, i: (l, i, 0))
    landing = pl.BlockSpec((N_DEV, tr, C), lambda l, i: (0, i, 0))
    return pl.pallas_call(
        body, name=name, grid=(L, R // tr),
        in_specs=[blk, blk, blk] + [landing] * L,
        out_specs=[blk] * 4, out_shape=[jax.ShapeDtypeStruct((L, R, C), F32)] * 4,
        compiler_params=_params("parallel", "parallel"),
    )(w, m, v, *parts)


_HBM = pl.BlockSpec(memory_space=pltpu.HBM)
_MESH = pl.DeviceIdType.MESH


def _slot(x, y, c):
    return 4 * x + 2 * y + c


class _Gather:
    def __init__(self, shards):
        self.arrays = list(shards)
        n = len(self.arrays)
        self.out_shapes = [jax.ShapeDtypeStruct((N_DEV,) + s.shape, s.dtype) for s in self.arrays]
        self.scratch = [pltpu.SemaphoreType.DMA((n, N_DEV - 1)), pltpu.SemaphoreType.DMA((n, N_DEV - 1)),
                        pltpu.SemaphoreType.DMA((n,))]

    def _parts(self, ins, outs, sems):
        send_sems, recv_sems, local_sems = sems
        n = len(self.arrays)
        x, y, c = lax.axis_index("x"), lax.axis_index("y"), lax.axis_index("c")
        me, sibling = (x, y, c), (x, y, 1 - c)
        chips = [(1 - x, y), (x, 1 - y), (1 - x, 1 - y)]

        def copy(a, k, block, to, src=None):
            dst = outs[a].at[_slot(*block)]
            return pltpu.make_async_remote_copy(src_ref=dst if src is None else src, dst_ref=dst,
                                                send_sem=send_sems.at[a, k], recv_sem=recv_sems.at[a, k],
                                                device_id=to, device_id_type=_MESH)

        mine = [pltpu.make_async_copy(ins[a], outs[a].at[_slot(*me)], local_sems.at[a]) for a in range(n)]
        first = []
        for a in range(n):
            first.append(copy(a, 0, me, sibling, src=ins[a]))
            first += [copy(a, 1 + j, me, (*chip, c), src=ins[a]) for j, chip in enumerate(chips)]
        return n, c, me, sibling, chips, copy, mine, first

    def start(self, ins, outs, sems):
        _, _, _, _, _, _, mine, first = self._parts(ins, outs, sems)
        for cp in mine + first:
            cp.start()

    def finish(self, ins, outs, sems):
        n, c, me, sibling, chips, copy, mine, first = self._parts(ins, outs, sems)
        passed = []
        for j, chip in enumerate(chips):
            for a in range(n):
                copy(a, 1 + j, (*chip, c), me).wait_recv()
                fwd = copy(a, 4 + j, (*chip, c), sibling)
                fwd.start()
                passed.append(fwd)
        for a in range(n):
            copy(a, 0, sibling, me).wait_recv()
            for j, chip in enumerate(chips):
                copy(a, 4 + j, (*chip, 1 - c), me).wait_recv()
        for cp in first + passed:
            cp.wait_send()
        for cp in mine:
            cp.wait()


class _Exchange:
    def __init__(self, arrays, scatter):
        self.arrays, self.scatter = list(arrays), list(scatter)
        n = len(self.arrays)
        shapes = [a.shape[1:] if s else a.shape for a, s in zip(self.arrays, self.scatter)]
        self.out_shapes = [jax.ShapeDtypeStruct((N_DEV,) + tuple(s), a.dtype) for s, a in zip(shapes, self.arrays)]
        self.scratch = [pltpu.SemaphoreType.DMA((n, N_DEV - 1)), pltpu.SemaphoreType.DMA((n, N_DEV - 1)),
                        pltpu.SemaphoreType.DMA((n,))]

    def _copies(self, ins, outs, sems):
        send_sems, recv_sems, local_sems = sems
        n, scatter = len(self.arrays), self.scatter
        x, y, c = lax.axis_index("x"), lax.axis_index("y"), lax.axis_index("c")
        me = _slot(x, y, c)
        copies = [pltpu.make_async_copy(ins[a].at[me] if scatter[a] else ins[a], outs[a].at[me], local_sems.at[a])
                  for a in range(n)]
        for r in range(1, N_DEV):
            px = 1 - x if r & 4 else x
            py = 1 - y if r & 2 else y
            pc = 1 - c if r & 1 else c
            for a in range(n):
                copies.append(pltpu.make_async_remote_copy(
                    src_ref=ins[a].at[_slot(px, py, pc)] if scatter[a] else ins[a], dst_ref=outs[a].at[me],
                    send_sem=send_sems.at[a, r - 1], recv_sem=recv_sems.at[a, r - 1],
                    device_id=(px, py, pc), device_id_type=_MESH))
        return copies

    def start(self, ins, outs, sems):
        for cp in self._copies(ins, outs, sems):
            cp.start()

    def finish(self, ins, outs, sems):
        for cp in self._copies(ins, outs, sems):
            cp.wait()


def _comm_call(comm, name):
    n = len(comm.arrays)

    def body(*refs):
        ins, outs, sems = refs[:n], refs[n:2 * n], refs[2 * n:]
        comm.start(ins, outs, sems)
        comm.finish(ins, outs, sems)

    return pl.pallas_call(body, name=name, in_specs=[_HBM] * n, out_specs=[_HBM] * n, out_shape=comm.out_shapes,
                          scratch_shapes=comm.scratch)(*comm.arrays)


def _call(body, comm, *, name, grid, in_specs, out_specs, out_shape, scratch_shapes, semantics, args):
    if comm is None:
        outs = pl.pallas_call(body, name=name, grid=grid, in_specs=in_specs, out_specs=out_specs, out_shape=out_shape,
                              scratch_shapes=scratch_shapes, compiler_params=_params(*semantics))(*args)
        return outs, []
    n_in, n_out, n_scr, n_c = len(in_specs), len(out_specs), len(scratch_shapes), len(comm.arrays)

    def fused(*refs):
        ins, refs = refs[:n_in], refs[n_in:]
        c_ins, refs = refs[:n_c], refs[n_c:]
        outs, refs = refs[:n_out], refs[n_out:]
        c_outs, refs = refs[:n_c], refs[n_c:]
        scr, sems = refs[:n_scr], refs[n_scr:]
        ids = [pl.program_id(d) for d in range(len(grid))]
        first = functools.reduce(jnp.logical_and, [i == 0 for i in ids])
        last = functools.reduce(jnp.logical_and, [i == g - 1 for i, g in zip(ids, grid)])

        @pl.when(first)
        def _():
            comm.start(c_ins, c_outs, sems)

        body(*ins, *outs, *scr)

        @pl.when(last)
        def _():
            comm.finish(c_ins, c_outs, sems)

    outs = pl.pallas_call(
        fused, name=name, grid=grid, in_specs=list(in_specs) + [_HBM] * n_c, out_specs=list(out_specs) + [_HBM] * n_c,
        out_shape=list(out_shape) + comm.out_shapes, scratch_shapes=list(scratch_shapes) + comm.scratch,
        compiler_params=_params(*["arbitrary"] * len(grid)))(*args, *comm.arrays)
    return outs[:n_out], outs[n_out:]


_GATHER_0 = (("dn_w_in", 0), ("dn_conv_w", 0), ("dn_o_norm_g", 0))
_GATHER_1 = (("dn_w_out", 0), ("sb_w_in", 0), ("sb_w_out", 0), ("sc_w_out", 0), ("dn_w_out", 1))
_GATHER_2 = (("sc_w_in", 0), ("sc_conv_w", 0), ("dn_w_in", 1), ("dn_conv_w", 1), ("dn_o_norm_g", 1))
_EXCHANGE_A = _GATHER_2
_EXCHANGE_B = (("sb_w_in", 0), ("sb_w_out", 0), ("dn_w_out", 0), ("sc_w_out", 0), ("dn_w_out", 1))
_EXCHANGE_C = _GATHER_0
_MATMUL_WEIGHTS = ("dn_w_in", "dn_w_out", "sb_w_in", "sb_w_out", "sc_w_in", "sc_w_out")
_COLUMN_SHARDED = ("dn_w_in", "dn_conv_w", "dn_o_norm_g", "sb_w_in", "sc_w_in", "sc_conv_w")
_BLOCKED = ("sb_w_in", "sc_w_in")
_REPLICATED = ("norm_g", "dn_a_log", "dn_dt_bias", "sb_q_norm_g", "sb_k_norm_g")
_ORDER = ("norm_g", "dn_w_in", "dn_conv_w", "dn_a_log", "dn_dt_bias", "dn_o_norm_g", "dn_w_out", "sb_w_in", "sb_q_norm_g",
          "sb_k_norm_g", "sb_w_out", "sc_w_in", "sc_conv_w", "sc_w_out")
_PACK_COLS = D_MODEL


def _as_2d(a):
    return a.reshape(1, -1) if a.ndim == 1 else a


def _assemble(name, gathered):
    n, r, c = gathered.shape
    if name in _COLUMN_SHARDED:
        return jnp.moveaxis(gathered, 0, 1).reshape(r, n * c)
    return gathered.reshape(n * r, c)


def _disassemble(name, full):
    r, c = full.shape
    if name in _COLUMN_SHARDED:
        return jnp.moveaxis(full.reshape(r, N_DEV, c // N_DEV), 1, 0)
    return full.reshape(N_DEV, r // N_DEV, c)


def _pack_replicated(d):
    rows = [d["norm_g"]]
    for name in _REPLICATED[1:]:
        flat = d[name].reshape(1, -1)
        rows.append(jnp.pad(flat, ((0, 0), (0, _PACK_COLS - flat.shape[1]))))
    return jnp.concatenate(rows, axis=0)


def _unpack_replicated(p, like):
    out = {"norm_g": p[:4]}
    for r, name in enumerate(_REPLICATED[1:]):
        shape = like[name].shape
        out[name] = p[4 + r, :math.prod(shape)].reshape(shape)
    return out


def kernel(x, norm_g, dn_w_in, dn_conv_w, dn_a_log, dn_dt_bias, dn_o_norm_g, dn_w_out, sb_w_in, sb_q_norm_g, sb_k_norm_g, sb_w_out, sc_w_in, sc_conv_w, sc_w_out, loss_target, m_norm_g, m_dn_w_in, m_dn_conv_w, m_dn_a_log, m_dn_dt_bias, m_dn_o_norm_g, m_dn_w_out, m_sb_w_in, m_sb_q_norm_g, m_sb_k_norm_g, m_sb_w_out, m_sc_w_in, m_sc_conv_w, m_sc_w_out, v_norm_g, v_dn_w_in, v_dn_conv_w, v_dn_a_log, v_dn_dt_bias, v_dn_o_norm_g, v_dn_w_out, v_sb_w_in, v_sb_q_norm_g, v_sb_k_norm_g, v_sb_w_out, v_sc_w_in, v_sc_conv_w, v_sc_w_out):
    w = dict(norm_g=norm_g, dn_w_in=dn_w_in, dn_conv_w=dn_conv_w, dn_a_log=dn_a_log, dn_dt_bias=dn_dt_bias,
             dn_o_norm_g=dn_o_norm_g, dn_w_out=dn_w_out, sb_w_in=sb_w_in, sb_q_norm_g=sb_q_norm_g, sb_k_norm_g=sb_k_norm_g,
             sb_w_out=sb_w_out, sc_w_in=sc_w_in, sc_conv_w=sc_conv_w, sc_w_out=sc_w_out)
    m = dict(norm_g=m_norm_g, dn_w_in=m_dn_w_in, dn_conv_w=m_dn_conv_w, dn_a_log=m_dn_a_log, dn_dt_bias=m_dn_dt_bias,
             dn_o_norm_g=m_dn_o_norm_g, dn_w_out=m_dn_w_out, sb_w_in=m_sb_w_in, sb_q_norm_g=m_sb_q_norm_g,
             sb_k_norm_g=m_sb_k_norm_g, sb_w_out=m_sb_w_out, sc_w_in=m_sc_w_in, sc_conv_w=m_sc_conv_w, sc_w_out=m_sc_w_out)
    v = dict(norm_g=v_norm_g, dn_w_in=v_dn_w_in, dn_conv_w=v_dn_conv_w, dn_a_log=v_dn_a_log, dn_dt_bias=v_dn_dt_bias,
             dn_o_norm_g=v_dn_o_norm_g, dn_w_out=v_dn_w_out, sb_w_in=v_sb_w_in, sb_q_norm_g=v_sb_q_norm_g,
             sb_k_norm_g=v_sb_k_norm_g, sb_w_out=v_sb_w_out, sc_w_in=v_sc_w_in, sc_conv_w=v_sc_conv_w, sc_w_out=v_sc_w_out)

    def gather_of(keys):
        return _Gather([_as_2d(w[k][j]).astype(BF16) if k in _MATMUL_WEIGHTS else _as_2d(w[k][j]) for k, j in keys])

    def full_weights(keys, gathered):
        return {key: g if key[0] in _BLOCKED else _assemble(key[0], g) for key, g in zip(keys, gathered)}

    def exchange_of(keys, grads, extra=()):
        out = [grads[k, j] if k in _BLOCKED else
               _disassemble(k, grads[k, j].astype(BF16) if k in _MATMUL_WEIGHTS else grads[k, j]) for k, j in keys]
        return _Exchange(out + list(extra), [True] * len(out) + [False] * len(extra))

    xs, saves = [x[0]], []
    h, got = _rmsnorm_fwd(xs[0], norm_g[0:1], "norm0", gather_of(_GATHER_0))
    F = full_weights(_GATHER_0, got)

    def w_out_0(got):
        F.update(full_weights(_GATHER_1, got))
        return F["dn_w_out", 0]

    (y, h), sv, _ = _dn_layer_fwd(h, _dn_split_w_in(F["dn_w_in", 0]), F["dn_conv_w", 0], dn_a_log[0:1], dn_dt_bias[0:1],
                                  F["dn_o_norm_g", 0], w_out_0, xs[0], "dn0", gather_of(_GATHER_1), norm_g[1:2])
    xs.append(y)
    saves.append(sv)
    (y, h), sv, got = _sb_layer_fwd(h, F["sb_w_in", 0], sb_q_norm_g, sb_k_norm_g, F["sb_w_out", 0], xs[1], "sb",
                                    gather_of(_GATHER_2), norm_g[2:3])
    F.update(full_weights(_GATHER_2, got))
    xs.append(y)
    saves.append(sv)
    (y, h), sv = _sc_layer_fwd(h, F["sc_w_in", 0], F["sc_conv_w", 0], F["sc_w_out", 0], xs[2], "sc", norm_g[3:4])
    xs.append(y)
    saves.append(sv)
    (y, _), sv, _ = _dn_layer_fwd(h, _dn_split_w_in(F["dn_w_in", 1]), F["dn_conv_w", 1], dn_a_log[1:2], dn_dt_bias[1:2],
                                  F["dn_o_norm_g", 1], F["dn_w_out", 1], xs[3], "dn1")
    xs.append(y)
    saves.append(sv)
    dx, loss_part = _loss_head(xs[4], loss_target[0])

    G, dnorm, landed = {}, [None] * 4, {}

    def keep(grads, j):
        G.update({(k, j): g for k, g in grads.items()})

    dh, grads, _, _ = _dn_layer_bwd(dx, saves[3], "dn1", None)
    keep(grads, 1)
    dx, dnorm[3] = _rmsnorm_bwd(dh, xs[3], norm_g[3:4], dx, "norm3_bwd")
    dh, grads = _sc_layer_bwd(dx, saves[2], "sc")
    keep(grads, 0)
    dx, dnorm[2] = _rmsnorm_bwd(dh, xs[2], norm_g[2:3], dx, "norm2_bwd")
    dh, grads, got = _sb_layer_bwd(dx, saves[1], "sb", exchange_of(_EXCHANGE_A, G))
    keep(grads, 0)
    landed.update(zip(_EXCHANGE_A, got))
    dx, dnorm[1] = _rmsnorm_bwd(dh, xs[1], norm_g[1:2], dx, "norm1_bwd")

    def exchange_b(dw_out):
        G["dn_w_out", 0] = dw_out
        return exchange_of(_EXCHANGE_B, G)

    def exchange_c(grads):
        keep(grads, 0)
        return exchange_of(_EXCHANGE_C, G)

    (dx, dnorm[0]), grads, got, got_late = _dn_layer_bwd(dx, saves[0], "dn0", (xs[0], norm_g[0:1], dx), exchange_b, exchange_c)
    landed.update(zip(_EXCHANGE_B, got))
    landed.update(zip(_EXCHANGE_C, got_late))
    replicated = dict(norm_g=jnp.concatenate(dnorm, axis=0),
                      dn_a_log=jnp.concatenate([G["dn_a_log", 0], G["dn_a_log", 1]], axis=0),
                      dn_dt_bias=jnp.concatenate([G["dn_dt_bias", 0], G["dn_dt_bias", 1]], axis=0),
                      sb_q_norm_g=G["sb_q_norm_g", 0], sb_k_norm_g=G["sb_k_norm_g", 0])
    got = _comm_call(_Exchange([_pack_replicated(replicated)], [False]), "exchange_replicated")

    res = {}
    for k in _ORDER:
        if k in _REPLICATED:
            continue
        shape = w[k].shape
        as_3d = lambda a: a.reshape(shape[0], math.prod(shape[1:-1]), shape[-1])
        outs = _adamw(as_3d(w[k]), as_3d(m[k]), as_3d(v[k]), [landed[k, j] for j in range(shape[0])], "adamw_" + k)
        res[k] = [o.reshape(shape) for o in outs]
    outs = _adamw(_pack_replicated(w)[None], _pack_replicated(m)[None], _pack_replicated(v)[None], [got[-1]],
                  "adamw_replicated")
    unpacked = [_unpack_replicated(o[0], w) for o in outs]
    for k in _REPLICATED:
        res[k] = [u[k] for u in unpacked]

    loss = lax.psum(loss_part[0, 0], ("x", "y", "c"))
    return (loss, dx[None]) + tuple(res[k][0] for k in _ORDER) + tuple(res[k][1] for k in _ORDER) \
        + tuple(res[k][2] for k in _ORDER) + tuple(res[k][3] for k in _ORDER)
```

```python
import functools
import itertools
import math

import jax
import jax.numpy as jnp
from jax import lax
from jax.experimental import pallas as pl
from jax.experimental.pallas import tpu as pltpu

F32 = jnp.float32
BF16 = jnp.bfloat16
HIGHEST = lax.Precision.HIGHEST

N_DEV = 8
D_MODEL = 1024
RMS_EPS = 1e-6
L2_EPS = 1e-6

DN_HEADS = 8
DN_DK = 128
DN_DV = 256
DN_QK_W = DN_HEADS * DN_DK
DN_V_W = DN_HEADS * DN_DV
DN_CONV = 4
DN_CHUNK = 64
DN_CONV_W = 2 * DN_QK_W + DN_V_W
DN_IN = DN_CONV_W + DN_V_W + 2 * DN_HEADS
DN_AB_PAD = 128
DN_PREP_BLK = 512

SB_HEADS = 16
SB_DH = 64
SB_W = SB_HEADS * SB_DH
SB_PAIRS = SB_HEADS // 2
SB_TQ = 256
SB_TK = 128
SB_DEAD = -106.0

SC_W = 2 * D_MODEL
SC_CONV = 3
SC_BLK = 512
SC_NBLK = SC_W // SC_BLK

ADAM_LR = 0.001
ADAM_B1 = 0.9
ADAM_B2 = 0.999
ADAM_EPS = 1e-08
ADAM_WD = 0.01
ADAM_STEP = 10

LANE = 128
SUBLANE = 8
HALO = SUBLANE
LONG_ROW_TILE = 512
NORM_FUSED_TM = 512
DEEP_TK = 2048
WIDE_TN = 2048
WIDE_ROW_TILE = 128
VMEM_LIMIT = 48 * 2 ** 20

NN = ((1,), (0,))
NT = ((1,), (1,))
TN = ((0,), (0,))


def _dot(a, b, dims=NN, precision=None):
    return lax.dot_general(a, b, (dims, ((), ())), precision=precision, preferred_element_type=F32)


def _bdot(a, b, dims=NN):
    return _dot(a.astype(BF16), b.astype(BF16), dims)


def _hdot(a, b, dims=NN):
    return _dot(a, b, dims, precision=HIGHEST)


def _tile(dim, pref, align=LANE):
    t = (min(pref, dim) // align) * align
    while t >= align:
        if dim % t == 0:
            return t
        t -= align
    return dim


def _params(*sem):
    return pltpu.CompilerParams(dimension_semantics=sem, vmem_limit_bytes=VMEM_LIMIT)


def _sigmoid(x):
    return 0.5 * jnp.tanh(0.5 * x) + 0.5


def _softplus(x):
    return jnp.maximum(x, 0.0) + jnp.log(1.0 + jnp.exp(-jnp.abs(x)))


def _silu_and_grad(x):
    s = _sigmoid(x)
    return x * s, s * (1.0 + x * (1.0 - s))


def _iota2(shape, dim):
    return lax.broadcasted_iota(jnp.int32, shape, dim)


def _matmul(a, b, mode, name, out_dtype=F32, add=None, b_cols=None, blocked_b=False, blocked_out=0,
            norm_fwd=None, norm_bwd=None, tm=1024, tn=1024, tk=1024):
    b_rows, b_width = (b.shape[1], b.shape[0] * b.shape[2]) if blocked_b else b.shape
    c0, b_used = b_cols if b_cols is not None else (0, b_width)
    if mode == "nn":
        (M, K), (K2, N) = a.shape, (b_rows, b_used)
    elif mode == "nt":
        (M, K), (N, K2) = a.shape, (b_rows, b_used)
    else:
        (K, M), (K2, N) = a.shape, (b_rows, b_used)
    assert K == K2, (a.shape, b.shape, mode)
    if mode == "tn":
        tk = max(tk, DEEP_TK)
    elif norm_fwd is None and norm_bwd is None and add is None:
        tn = max(tn, WIDE_TN)
    tm, tn, tk = _tile(M, tm), _tile(N, tn), _tile(K, tk)
    if blocked_b and mode == "nt":
        tk = b.shape[2]
    elif blocked_b:
        tn = b.shape[2]
    if blocked_out:
        tn = N // blocked_out
    nk = K // tk
    dims = {"nn": NN, "nt": NT, "tn": TN}[mode]
    a_spec = pl.BlockSpec((tk, tm), lambda i, j, k: (k, i)) if mode == "tn" else pl.BlockSpec((tm, tk), lambda i, j, k: (i, k))
    if mode == "nt":
        cb0 = c0 // tk
        assert c0 % tk == 0
        b_spec = (pl.BlockSpec((None, tn, tk), lambda i, j, k: (k + cb0, j, 0)) if blocked_b
                  else pl.BlockSpec((tn, tk), lambda i, j, k: (j, k + cb0)))
    else:
        cb0 = c0 // tn
        assert c0 % tn == 0
        b_spec = (pl.BlockSpec((None, tk, tn), lambda i, j, k: (j + cb0, k, 0)) if blocked_b
                  else pl.BlockSpec((tk, tn), lambda i, j, k: (k, j + cb0)))
    o_spec = pl.BlockSpec((tm, tn), lambda i, j, k: (i, j))
    out_spec = pl.BlockSpec((None, tm, tn), lambda i, j, k: (j, i, 0)) if blocked_out else o_spec
    out_shape = (blocked_out, M, tn) if blocked_out else (M, N)
    has_add = add is not None
    vec_spec = pl.BlockSpec((1, tn), lambda i, j, k: (0, j))
    assert not (norm_fwd is not None or norm_bwd is not None) or tn == N
    extra_in, extra_specs = [], []
    if has_add:
        extra_in, extra_specs = [add], [o_spec]
    if norm_fwd is not None:
        extra_in, extra_specs = extra_in + [norm_fwd], extra_specs + [vec_spec]
        out_specs = [o_spec, o_spec]
        out_shapes = [jax.ShapeDtypeStruct((M, N), out_dtype), jax.ShapeDtypeStruct((M, N), BF16)]
    elif norm_bwd is not None:
        extra_in, extra_specs = extra_in + list(norm_bwd), extra_specs + [o_spec, vec_spec, o_spec]
        out_specs = [o_spec, vec_spec]
        out_shapes = [jax.ShapeDtypeStruct((M, N), F32), jax.ShapeDtypeStruct((1, N), F32)]
    else:
        out_specs, out_shapes = out_spec, jax.ShapeDtypeStruct(out_shape, out_dtype)

    def body(*refs):
        a_ref, b_ref = refs[0], refs[1]
        extra = list(refs[2:2 + len(extra_in)])
        outs = refs[2 + len(extra_in):]
        add_ref = extra.pop(0) if has_add else None
        p = _bdot(a_ref[...], b_ref[...], dims)

        def finish(acc):
            if has_add:
                acc = acc + add_ref[...]
            if norm_bwd is not None:
                _rmsnorm_bwd_tile(acc, *extra, outs[0], outs[1], first=pl.program_id(0) == 0)
                return
            outs[0][...] = acc.astype(out_dtype)
            if norm_fwd is not None:
                r = lax.rsqrt(jnp.mean(acc * acc, axis=-1, keepdims=True) + RMS_EPS)
                outs[1][...] = (acc * r * extra[0][...]).astype(BF16)

        if nk == 1:
            finish(p)
        else:
            acc_ref = refs[-1]
            k = pl.program_id(2)

            @pl.when(k == 0)
            def _():
                acc_ref[...] = p

            @pl.when(k > 0)
            def _():
                acc_ref[...] += p

            @pl.when(k == nk - 1)
            def _():
                finish(acc_ref[...])

    return pl.pallas_call(
        body, name=name, grid=(M // tm, N // tn, nk),
        in_specs=[a_spec, b_spec] + extra_specs, out_specs=out_specs, out_shape=out_shapes,
        scratch_shapes=[pltpu.VMEM((tm, tn), F32)] if nk > 1 else [],
        compiler_params=(_params("arbitrary", "arbitrary", "arbitrary") if norm_bwd is not None
                         else _params("parallel", "parallel", "arbitrary")),
    )(a, b, *extra_in)


def _rmsnorm_bwd_tile(dh, x_ref, g_ref, res_ref, dx_ref, dg_ref, first):
    xv = x_ref[...]
    r = lax.rsqrt(jnp.mean(xv * xv, axis=-1, keepdims=True) + RMS_EPS)
    xh = xv * r
    dxh = dh * g_ref[...]
    m = jnp.mean(dxh * xh, axis=-1, keepdims=True)
    dx_ref[...] = res_ref[...] + r * (dxh - xh * m)
    part = jnp.sum(dh * xh, axis=0, keepdims=True)

    @pl.when(first)
    def _():
        dg_ref[...] = part

    @pl.when(jnp.logical_not(first))
    def _():
        dg_ref[...] += part


def _matmul_nt_sum(pairs, name, comm=None, norm_bwd=None, tm=NORM_FUSED_TM, tk=1024):
    M, N = pairs[0][0].shape[0], pairs[0][1].shape[0]
    tm = _tile(M, tm)
    tks = [_tile(a.shape[1], tk) for a, _, _ in pairs]
    steps = [a.shape[1] // t for (a, _, _), t in zip(pairs, tks)]
    offs = [sum(steps[:p]) for p in range(len(pairs))]
    total = sum(steps)

    n_extra = 3 if norm_bwd is not None else 0

    def body(*refs):
        a_refs, b_refs = refs[0:2 * len(pairs):2], refs[1:2 * len(pairs):2]
        extra = refs[2 * len(pairs):2 * len(pairs) + n_extra]
        outs, acc_ref = refs[2 * len(pairs) + n_extra:-1], refs[-1]
        k = pl.program_id(1)
        for p in range(len(pairs)):
            @pl.when((k >= offs[p]) & (k < offs[p] + steps[p]))
            def _(p=p):
                prod = _bdot(a_refs[p][...], b_refs[p][...], NT)
                if p == 0:
                    @pl.when(k == 0)
                    def _():
                        acc_ref[...] = prod

                    @pl.when(k > 0)
                    def _():
                        acc_ref[...] += prod
                else:
                    acc_ref[...] += prod

        @pl.when(k == total - 1)
        def _():
            if norm_bwd is not None:
                _rmsnorm_bwd_tile(acc_ref[...], *extra, outs[0], outs[1], first=pl.program_id(0) == 0)
            else:
                outs[0][...] = acc_ref[...]

    in_specs, args = [], []
    for (a, b, c0), t, off, n in zip(pairs, tks, offs, steps):
        assert c0 % t == 0
        pick = lambda k, off=off, n=n: jnp.clip(k - off, 0, n - 1)
        in_specs += [pl.BlockSpec((tm, t), lambda i, k, pick=pick: (i, pick(k))),
                     pl.BlockSpec((N, t), lambda i, k, pick=pick, cb0=c0 // t: (0, pick(k) + cb0))]
        args += [a, b]
    row, vec = pl.BlockSpec((tm, N), lambda i, k: (i, 0)), pl.BlockSpec((1, N), lambda i, k: (0, 0))
    if norm_bwd is not None:
        in_specs += [row, vec, row]
        args += list(norm_bwd)
        out_specs, out_shape = [row, vec], [jax.ShapeDtypeStruct((M, N), F32), jax.ShapeDtypeStruct((1, N), F32)]
    else:
        out_specs, out_shape = [row], [jax.ShapeDtypeStruct((M, N), F32)]
    outs, landed = _call(body, comm, name=name, grid=(M // tm, total), in_specs=in_specs, out_specs=out_specs,
                         out_shape=out_shape, scratch_shapes=[pltpu.VMEM((tm, N), F32)],
                         semantics=("arbitrary", "arbitrary"), args=tuple(args))
    return (outs if norm_bwd is not None else outs[0]), landed


def _rmsnorm_fwd(x, g, name, comm=None):
    T, D = x.shape
    tt = _tile(T, LONG_ROW_TILE, SUBLANE)

    def body(x_ref, g_ref, o_ref):
        xv = x_ref[...]
        r = lax.rsqrt(jnp.mean(xv * xv, axis=-1, keepdims=True) + RMS_EPS)
        o_ref[...] = (xv * r * g_ref[...]).astype(BF16)

    outs, landed = _call(
        body, comm, name=name, grid=(T // tt,),
        in_specs=[pl.BlockSpec((tt, D), lambda i: (i, 0)), pl.BlockSpec((1, D), lambda i: (0, 0))],
        out_specs=[pl.BlockSpec((tt, D), lambda i: (i, 0))], out_shape=[jax.ShapeDtypeStruct((T, D), BF16)],
        scratch_shapes=[], semantics=("parallel",), args=(x, g))
    return outs[0], landed


def _rmsnorm_bwd(dh, x, g, dx_res, name):
    T, D = x.shape
    tt = _tile(T, LONG_ROW_TILE // 2, SUBLANE)

    def body(dh_ref, x_ref, g_ref, res_ref, dx_ref, dg_ref):
        _rmsnorm_bwd_tile(dh_ref[...], x_ref, g_ref, res_ref, dx_ref, dg_ref, first=pl.program_id(0) == 0)

    row = pl.BlockSpec((tt, D), lambda i: (i, 0))
    vec = pl.BlockSpec((1, D), lambda i: (0, 0))
    return pl.pallas_call(
        body, name=name, grid=(T // tt,),
        in_specs=[row, row, vec, row], out_specs=[row, vec],
        out_shape=[jax.ShapeDtypeStruct((T, D), F32), jax.ShapeDtypeStruct((1, D), F32)],
        compiler_params=_params("arbitrary"),
    )(dh, x, g, dx_res)


def _loss_head(y, target, name="loss_head"):
    T, D = y.shape
    tt = _tile(T, LONG_ROW_TILE, SUBLANE)

    def body(y_ref, t_ref, dy_ref, l_ref):
        e = y_ref[...] - t_ref[...]
        dy_ref[...] = e * (1.0 / D)
        s = jnp.sum(jnp.sum(e * e, axis=1, keepdims=True), axis=0, keepdims=True) * (0.5 / D)
        s = jnp.broadcast_to(s, (1, LANE))

        @pl.when(pl.program_id(0) == 0)
        def _():
            l_ref[...] = s

        @pl.when(pl.program_id(0) > 0)
        def _():
            l_ref[...] += s

    row = pl.BlockSpec((tt, D), lambda i: (i, 0))
    return pl.pallas_call(
        body, name=name, grid=(T // tt,),
        in_specs=[row, row], out_specs=[row, pl.BlockSpec((1, LANE), lambda i: (0, 0))],
        out_shape=[jax.ShapeDtypeStruct((T, D), F32), jax.ShapeDtypeStruct((1, LANE), F32)],
        compiler_params=_params("arbitrary"),
    )(y, target)


def _down(x, k):
    return pltpu.roll(x, k, 0) if k else x


def _up(x, k):
    return pltpu.roll(x, x.shape[0] - k, 0) if k else x


def _sc_fwd(proj, conv_w, name):
    T = proj.shape[0]
    tt = _tile(T, WIDE_ROW_TILE, SUBLANE)
    B = SC_BLK

    def body(p_ref, ph_ref, w_ref, o_ref):
        keep = (pl.program_id(0) > 0).astype(F32)
        for j in range(SC_NBLK):
            cb, cc, cu, cg = (slice(k * SC_W + j * B, k * SC_W + (j + 1) * B) for k in range(4))
            cw = slice(j * B, (j + 1) * B)
            z = jnp.concatenate([ph_ref[:, cc] * ph_ref[:, cu] * keep, p_ref[:, cc] * p_ref[:, cu]], axis=0)
            cz = (w_ref[2:3, cw] * z + w_ref[1:2, cw] * _down(z, 1) + w_ref[0:1, cw] * _down(z, 2))[HALO:]
            gate = p_ref[:, cg]
            o_ref[:, cw] = (p_ref[:, cb] * cz * (gate * _sigmoid(gate))).astype(BF16)

    return pl.pallas_call(
        body, name=name, grid=(T // tt,),
        in_specs=[pl.BlockSpec((tt, 4 * SC_W), lambda i: (i, 0)),
                  pl.BlockSpec((HALO, 4 * SC_W), lambda i: (jnp.maximum(i * (tt // HALO) - 1, 0), 0)),
                  pl.BlockSpec((SC_CONV, SC_W), lambda i: (0, 0))],
        out_specs=pl.BlockSpec((tt, SC_W), lambda i: (i, 0)),
        out_shape=jax.ShapeDtypeStruct((T, SC_W), BF16),
        compiler_params=_params("parallel"),
    )(proj, proj, conv_w)


def _sc_bwd(dyg, proj, conv_w, name):
    T = proj.shape[0]
    tt = _tile(T, WIDE_ROW_TILE, SUBLANE)
    nt = T // tt
    B = SC_BLK
    hb = tt // HALO

    def body(d_ref, dn_ref, p_ref, pp_ref, pn_ref, w_ref, o_ref, dw_ref):
        i = pl.program_id(0)
        keep_p = (i > 0).astype(F32)
        keep_n = (i < nt - 1).astype(F32)
        main = slice(HALO, HALO + tt)
        parts = []
        for j in range(SC_NBLK):
            cw = slice(j * B, (j + 1) * B)

            def ext(k):
                s = slice(k * SC_W + j * B, k * SC_W + (j + 1) * B)
                return s, jnp.concatenate([pp_ref[:, s] * keep_p, p_ref[:, s], pn_ref[:, s]], axis=0)

            (sb, b), (sc, c), (su, u), (sg_, gate) = ext(0), ext(1), ext(2), ext(3)
            dyg_e = jnp.concatenate([jnp.zeros((HALO, B), F32), d_ref[:, cw], dn_ref[:, cw] * keep_n], axis=0)
            w0, w1, w2 = w_ref[0:1, cw], w_ref[1:2, cw], w_ref[2:3, cw]
            z = c * u
            z1, z2 = _down(z, 1), _down(z, 2)
            cz = w2 * z + w1 * z1 + w0 * z2
            sg, dsg = _silu_and_grad(gate)
            dy = dyg_e * sg
            dcz = dy * b
            dz = w2 * dcz + w1 * _up(dcz, 1) + w0 * _up(dcz, 2)
            o_ref[:, sb] = (dy * cz)[main].astype(BF16)
            o_ref[:, sc] = (dz * u)[main].astype(BF16)
            o_ref[:, su] = (dz * c)[main].astype(BF16)
            o_ref[:, sg_] = (dyg_e * (b * cz) * dsg)[main].astype(BF16)
            dcm = dcz[main]
            parts.append(jnp.concatenate([jnp.sum(dcm * z2[main], axis=0, keepdims=True),
                                          jnp.sum(dcm * z1[main], axis=0, keepdims=True),
                                          jnp.sum(dcm * z[main], axis=0, keepdims=True)], axis=0))
        part = jnp.concatenate(parts, axis=1)

        @pl.when(i == 0)
        def _():
            dw_ref[...] = part

        @pl.when(i > 0)
        def _():
            dw_ref[...] += part

    nxt = lambda i: (jnp.minimum((i + 1) * hb, nt * hb - 1), 0)
    return pl.pallas_call(
        body, name=name, grid=(nt,),
        in_specs=[pl.BlockSpec((tt, SC_W), lambda i: (i, 0)),
                  pl.BlockSpec((HALO, SC_W), nxt),
                  pl.BlockSpec((tt, 4 * SC_W), lambda i: (i, 0)),
                  pl.BlockSpec((HALO, 4 * SC_W), lambda i: (jnp.maximum(i * hb - 1, 0), 0)),
                  pl.BlockSpec((HALO, 4 * SC_W), nxt),
                  pl.BlockSpec((SC_CONV, SC_W), lambda i: (0, 0))],
        out_specs=[pl.BlockSpec((tt, 4 * SC_W), lambda i: (i, 0)), pl.BlockSpec((SC_CONV, SC_W), lambda i: (0, 0))],
        out_shape=[jax.ShapeDtypeStruct((T, 4 * SC_W), BF16), jax.ShapeDtypeStruct((SC_CONV, SC_W), F32)],
        compiler_params=_params("arbitrary"),
    )(dyg, dyg, proj, proj, proj, conv_w)


def _split3_dot(x, m):
    hi = x.astype(BF16)
    r1 = x - hi.astype(F32)
    mid = r1.astype(BF16)
    lo = (r1 - mid.astype(F32)).astype(BF16)
    return _dot(hi, m) + _dot(mid, m) + _dot(lo, m)


def _split2_dot(x, m):
    hi = x.astype(BF16)
    lo = (x - hi.astype(F32)).astype(BF16)
    return _dot(hi, m) + _dot(lo, m)


def _head_mean_matrix():
    r, c = _iota2((LANE, LANE), 0), _iota2((LANE, LANE), 1)
    return jnp.where((r // SB_DH) == (c // SB_DH), 1.0 / SB_DH, 0.0).astype(BF16)


def _sb_prep(proj, qg2, kg2, name):
    T = proj.shape[0]
    tt = _tile(T, WIDE_ROW_TILE, SUBLANE)

    def body(p_ref, qg_ref, kg_ref, q_ref, k_ref, v_ref):
        bd = _head_mean_matrix()

        def norm(x, g, scale):
            r = lax.rsqrt(_split3_dot(x * x, bd) + RMS_EPS)
            return (x * r * g * scale).astype(BF16)

        v_ref[...] = p_ref[:, 2 * SB_W:3 * SB_W].astype(BF16)
        for p in range(SB_PAIRS):
            cols = slice(p * LANE, (p + 1) * LANE)
            q_ref[:, cols] = norm(p_ref[:, cols], qg_ref[...], SB_DH ** -0.5)
            k_ref[:, cols] = norm(p_ref[:, SB_W + p * LANE:SB_W + (p + 1) * LANE], kg_ref[...], 1.0)

    blk = pl.BlockSpec((tt, SB_W), lambda i: (i, 0))
    vec = pl.BlockSpec((1, LANE), lambda i: (0, 0))
    return pl.pallas_call(
        body, name=name, grid=(T // tt,),
        in_specs=[pl.BlockSpec((tt, 4 * SB_W), lambda i: (i, 0)), vec, vec],
        out_specs=[blk, blk, blk],
        out_shape=[jax.ShapeDtypeStruct((T, SB_W), BF16)] * 3,
        compiler_params=_params("parallel"),
    )(proj, qg2, kg2)


def _sb_prep_bwd(proj, dqn, dkn, dv, dgate, qg2, kg2, name):
    T = proj.shape[0]
    tt = _tile(T, WIDE_ROW_TILE, SUBLANE)

    def body(p_ref, dq_ref, dk_ref, dv_ref, dg_ref, qg_ref, kg_ref, o_ref, dqg_ref, dkg_ref):
        i = pl.program_id(0)
        bd = _head_mean_matrix()

        def norm_bwd(x, g, dy):
            r = lax.rsqrt(_split3_dot(x * x, bd) + RMS_EPS)
            xh = x * r
            dxh = dy * g
            m = _split3_dot(dxh * xh, bd)
            return r * (dxh - xh * m), jnp.sum(dy * xh, axis=0, keepdims=True)

        o_ref[:, 2 * SB_W:3 * SB_W] = dv_ref[...].astype(BF16)
        o_ref[:, 3 * SB_W:4 * SB_W] = dg_ref[...].astype(BF16)
        pq = pk = jnp.zeros((1, LANE), F32)
        for p in range(SB_PAIRS):
            cols, kcols = slice(p * LANE, (p + 1) * LANE), slice(SB_W + p * LANE, SB_W + (p + 1) * LANE)
            dxq, sq = norm_bwd(p_ref[:, cols], qg_ref[...], dq_ref[:, cols])
            dxk, sk = norm_bwd(p_ref[:, kcols], kg_ref[...], dk_ref[:, cols])
            o_ref[:, cols] = dxq.astype(BF16)
            o_ref[:, kcols] = dxk.astype(BF16)
            pq, pk = pq + sq, pk + sk

        @pl.when(i == 0)
        def _():
            dqg_ref[...] = pq
            dkg_ref[...] = pk

        @pl.when(i > 0)
        def _():
            dqg_ref[...] += pq
            dkg_ref[...] += pk

    blk = pl.BlockSpec((tt, SB_W), lambda i: (i, 0))
    vec = pl.BlockSpec((1, LANE), lambda i: (0, 0))
    wide = pl.BlockSpec((tt, 4 * SB_W), lambda i: (i, 0))
    return pl.pallas_call(
        body, name=name, grid=(T // tt,),
        in_specs=[wide, blk, blk, blk, blk, vec, vec],
        out_specs=[wide, vec, vec],
        out_shape=[jax.ShapeDtypeStruct((T, 4 * SB_W), BF16)] + [jax.ShapeDtypeStruct((1, LANE), F32)] * 2,
        compiler_params=_params("arbitrary"),
    )(proj, dqn, dkn, dv, dgate, qg2, kg2)


def _fold_heads(part, name):
    def body(p_ref, o_ref):
        r, c = _iota2((LANE, SB_DH), 0), _iota2((LANE, SB_DH), 1)
        fold = jnp.where((r % SB_DH) == c, 1.0, 0.0).astype(F32)
        o_ref[...] = jnp.sum(_hdot(p_ref[...], fold), axis=0, keepdims=True)

    return pl.pallas_call(body, name=name, out_shape=jax.ShapeDtypeStruct((1, SB_DH), F32))(part)


def _sb_masks():
    lane = _iota2((1, LANE), 1)
    return lane < SB_DH


def _from_row(x, r0):
    return x if r0 == 0 else x[r0:]


def _pad_rows(x, r0):
    return x if r0 == 0 else jnp.concatenate([jnp.zeros((r0, x.shape[1]), x.dtype), x], axis=0)


def _sb_attn_fwd(qn, kn, vb, proj, name, comm=None):
    T = qn.shape[0]
    tq, tk = _tile(T, SB_TQ, SUBLANE), SB_TK
    assert tq % tk == 0

    def body(q_ref, k_ref, v_ref, g_ref, o_ref, og_ref, lt_ref, done_ref):
        i = pl.program_id(1)
        ma = _sb_masks()
        q2 = q_ref[...]
        zero = jnp.zeros_like(q2)
        qs = (jnp.where(ma, q2, zero), jnp.where(ma, zero, q2))
        upper = (_iota2((tk, tk), 0) > _iota2((tk, tk), 1)).astype(BF16)
        qpos = i * tq + _iota2((tq, tk), 0)
        nb = tq // tk

        def trip(kb_top, masked, carry):
            acc, la, lb = carry
            chains = [(b, h) for b in range(nb) for h in range(2)]
            r0 = [(nb - 1 - b) * tk if masked else 0 for b in range(nb)]
            k2s, vss, masks = [], [], []
            for b in range(nb):
                kb = kb_top - b
                rows = pl.ds(pl.multiple_of(kb * tk, tk), tk)
                k2s.append(k_ref[rows, :])
                v2 = v_ref[rows, :]
                zv = jnp.zeros_like(v2)
                vss.append((jnp.where(ma, v2, zv), jnp.where(ma, zv, v2)))
                masks.append(_from_row((kb * tk + _iota2((tq, tk), 1)) < qpos, r0[b]) if masked else None)
            zs = [_dot(_from_row(qs[h], r0[b]), k2s[b], NT) for b, h in chains]
            ts = [jnp.log(1.0 + jnp.exp(-jnp.abs(z))) for z in zs]
            ls = [-(jnp.maximum(z, 0.0) + t) for z, t in zip(zs, ts)]
            if masked:
                ls = [jnp.where(masks[b], l, 0.0) for (b, h), l in zip(chains, ls)]
            cums = [_split2_dot(l, upper) for l in ls]
            sums = [_pad_rows(jnp.sum(l, axis=1, keepdims=True), r0[b]) for (b, h), l in zip(chains, ls)]
            offs, tot = {}, [la, lb]
            for b in range(nb):
                for h in range(2):
                    offs[(b, h)] = _from_row(tot[h], r0[b])
                    tot[h] = tot[h] + sums[chains.index((b, h))]
            ws = [jnp.exp(jnp.minimum(z, 0.0) - t + c + offs[ch]) for ch, z, t, c in zip(chains, zs, ts, cums)]
            if masked:
                ws = [jnp.where(masks[b], w, 0.0) for (b, h), w in zip(chains, ws)]
            for (b, h), w in zip(chains, ws):
                acc = acc + _pad_rows(_dot(w.astype(BF16), vss[b][h]), r0[b])
            return acc, tot[0], tot[1]

        def largest(la, lb):
            return jnp.max(jnp.maximum(la, lb))

        z1 = jnp.zeros((tq, 1), F32)
        acc, la, lb = trip((i + 1) * nb - 1, True, (jnp.zeros((tq, LANE), F32), z1, z1))

        def live(c):
            return (c[0] < i) & (c[4] > SB_DEAD)

        def more(c):
            j, acc, la, lb, _ = c
            acc, la, lb = trip((i - j) * nb - 1, False, (acc, la, lb))
            return j + 1, acc, la, lb, largest(la, lb)

        done, acc, la, lb, _ = lax.while_loop(live, more, (jnp.int32(0), acc, la, lb, largest(la, lb)))
        gate = g_ref[...]
        o_ref[...] = acc
        og_ref[...] = (acc * (gate * _sigmoid(gate))).astype(BF16)
        lt_ref[...] = jnp.where(_iota2((tq, 2), 1) == 0, la, lb)
        done_ref[...] = jnp.full((SUBLANE, LANE), done, F32)

    nq = T // tq
    qblk = pl.BlockSpec((tq, LANE), lambda p, i: (i, p))
    full = pl.BlockSpec((T, LANE), lambda p, i: (0, p))
    return _call(
        body, comm, name=name, grid=(SB_PAIRS, nq),
        in_specs=[qblk, full, full, pl.BlockSpec((tq, LANE), lambda p, i: (i, 3 * SB_PAIRS + p))],
        out_specs=[qblk, qblk, pl.BlockSpec((None, tq, 2), lambda p, i: (p, i, 0)),
                   pl.BlockSpec((None, None, SUBLANE, LANE), lambda p, i: (p, i, 0, 0))],
        out_shape=[jax.ShapeDtypeStruct((T, SB_W), F32), jax.ShapeDtypeStruct((T, SB_W), BF16),
                   jax.ShapeDtypeStruct((SB_PAIRS, T, 2), F32), jax.ShapeDtypeStruct((SB_PAIRS, nq, SUBLANE, LANE), F32)],
        scratch_shapes=[], semantics=("parallel", "parallel"), args=(qn, kn, vb, proj))


def _sb_attn_bwd(qn, kn, vb, dog, o, ltot, done, proj, name, comm=None):
    T = qn.shape[0]
    tq, tk = _tile(T, SB_TQ, SUBLANE), SB_TK

    def body(q_ref, k_ref, v_ref, dog_ref, o_ref, lt_ref, done_ref, g_ref, dq_ref, dk_ref, dv_ref, dgate_ref):
        i = pl.program_id(1)
        first_trip = i - jnp.max(done_ref[...]).astype(jnp.int32)

        @pl.when(i == 0)
        def _():
            dk_ref[...] = jnp.zeros_like(dk_ref)
            dv_ref[...] = jnp.zeros_like(dv_ref)

        ma = _sb_masks()
        gate, o2, dog2 = g_ref[...], o_ref[...], dog_ref[...]
        sg, dsg = _silu_and_grad(gate)
        do2 = dog2 * sg
        dgate_ref[...] = dog2 * o2 * dsg
        lt = lt_ref[...]
        first = _iota2((tq, 2), 1) == 0
        ltots = (jnp.sum(jnp.where(first, lt, 0.0), axis=1, keepdims=True),
                 jnp.sum(jnp.where(first, 0.0, lt), axis=1, keepdims=True))
        q2 = q_ref[...]
        zq = jnp.zeros_like(q2)
        qs = (jnp.where(ma, q2, zq), jnp.where(ma, zq, q2))
        dob = do2.astype(BF16)
        dos = (jnp.where(ma, dob, zq), jnp.where(ma, zq, dob))
        upto = (_iota2((tk, tk), 0) <= _iota2((tk, tk), 1)).astype(BF16)
        before = (_iota2((tk, tk), 0) < _iota2((tk, tk), 1)).astype(BF16)
        qpos = i * tq + _iota2((tq, tk), 0)
        nb = tq // tk

        def trip(kb_bot, masked, carry):
            dq, la, lb, ea, eb = carry
            chains = [(b, h) for b in range(nb) for h in range(2)]
            r0 = [b * tk if masked else 0 for b in range(nb)]
            rows, k2s, v2s, kss, masks = [], [], [], [], []
            for b in range(nb):
                kb = kb_bot + b
                rows.append(pl.ds(pl.multiple_of(kb * tk, tk), tk))
                k2 = k_ref[rows[b], :]
                zk = jnp.zeros_like(k2)
                k2s.append(k2)
                v2s.append(v_ref[rows[b], :])
                kss.append((jnp.where(ma, k2, zk), jnp.where(ma, zk, k2)))
                masks.append(_from_row((kb * tk + _iota2((tq, tk), 1)) < qpos, r0[b]) if masked else None)

            def keep(vals):
                return [jnp.where(masks[b], x, 0.0) for (b, h), x in zip(chains, vals)] if masked else vals

            zs = [_dot(_from_row(qs[h], r0[b]), k2s[b], NT) for b, h in chains]
            dws = [_dot(_from_row(dos[h], r0[b]), v2s[b], NT) for b, h in chains]
            ts = [jnp.log(1.0 + jnp.exp(-jnp.abs(z))) for z in zs]
            ls = keep([-(jnp.maximum(z, 0.0) + t) for z, t in zip(zs, ts)])
            lps = [jnp.minimum(z, 0.0) - t for z, t in zip(zs, ts)]
            cums = [_split3_dot(l, upto) for l in ls]
            lsums = [_pad_rows(jnp.sum(l, axis=1, keepdims=True), r0[b]) for (b, h), l in zip(chains, ls)]
            offs, tot = {}, [la, lb]
            for b in range(nb):
                for h in range(2):
                    offs[(b, h)] = _from_row(tot[h], r0[b])
                    tot[h] = tot[h] + lsums[chains.index((b, h))]
            ws = keep([jnp.exp(lp + (_from_row(ltots[h], r0[b]) - (offs[(b, h)] + c)))
                       for (b, h), lp, c in zip(chains, lps, cums)])
            es = [dw * w for dw, w in zip(dws, ws)]
            ecums = [_split2_dot(e, before) for e in es]
            esums = [_pad_rows(jnp.sum(e, axis=1, keepdims=True), r0[b]) for (b, h), e in zip(chains, es)]
            eoffs, etot = {}, [ea, eb]
            for b in range(nb):
                for h in range(2):
                    eoffs[(b, h)] = _from_row(etot[h], r0[b])
                    etot[h] = etot[h] + esums[chains.index((b, h))]
            dzs = keep([e - jnp.exp(lp) * (e + eoffs[ch] + ec) for ch, e, lp, ec in zip(chains, es, lps, ecums)])
            dzs = [dz.astype(BF16) for dz in dzs]
            wbs = [w.astype(BF16) for w in ws]
            for (b, h), dz in zip(chains, dzs):
                dq = dq + _pad_rows(_dot(dz, kss[b][h]), r0[b])
            for b in range(nb):
                ia, ib = chains.index((b, 0)), chains.index((b, 1))
                q_a, q_b = _from_row(qs[0], r0[b]), _from_row(qs[1], r0[b])
                do_a, do_b = _from_row(dos[0], r0[b]), _from_row(dos[1], r0[b])
                dk_ref[rows[b], :] += _dot(dzs[ia], q_a, TN) + _dot(dzs[ib], q_b, TN)
                dv_ref[rows[b], :] += _dot(wbs[ia], do_a, TN) + _dot(wbs[ib], do_b, TN)
            return dq, tot[0], tot[1], etot[0], etot[1]

        z1 = jnp.zeros((tq, 1), F32)
        carry = lax.fori_loop(first_trip, i, lambda j, c: trip(j * nb, False, c),
                              (jnp.zeros((tq, LANE), F32), z1, z1, z1, z1))
        dq = trip(i * nb, True, carry)[0]
        dq_ref[...] = dq * (SB_DH ** -0.5)

    qblk = pl.BlockSpec((tq, LANE), lambda p, i: (i, p))
    full = pl.BlockSpec((T, LANE), lambda p, i: (0, p))
    return _call(
        body, comm, name=name, grid=(SB_PAIRS, T // tq),
        in_specs=[qblk, full, full, qblk, qblk, pl.BlockSpec((None, tq, 2), lambda p, i: (p, i, 0)),
                  pl.BlockSpec((None, None, SUBLANE, LANE), lambda p, i: (p, i, 0, 0)),
                  pl.BlockSpec((tq, LANE), lambda p, i: (i, 3 * SB_PAIRS + p))],
        out_specs=[qblk, full, full, qblk],
        out_shape=[jax.ShapeDtypeStruct((T, SB_W), F32)] * 4,
        scratch_shapes=[], semantics=("parallel", "arbitrary"), args=(qn, kn, vb, dog, o, ltot, done, proj))


def _dn_conv(ext, w_ref, cw):
    return (w_ref[3:4, cw] * ext + w_ref[2:3, cw] * _down(ext, 1) + w_ref[1:2, cw] * _down(ext, 2)
            + w_ref[0:1, cw] * _down(ext, 3))


def _dn_gates(a_in, b_in, a_log, dt_bias, name):
    T, H = a_in.shape
    C = DN_CHUNK

    def body(a_ref, b_ref, al_ref, dt_ref, g_ref, beta_ref):
        beta_ref[...] = _sigmoid(b_ref[...])
        g_ref[...] = -jnp.exp(al_ref[...]) * _softplus(a_ref[...] + dt_ref[...])
        tri = (_iota2((C, C), 0) >= _iota2((C, C), 1)).astype(F32)

        def chunk(n, carry):
            rows = pl.ds(pl.multiple_of(n * C, C), C)
            g_ref[rows, :] = _hdot(tri, g_ref[rows, :])
            return carry

        lax.fori_loop(0, T // C, chunk, 0)

    return pl.pallas_call(body, name=name, out_shape=[jax.ShapeDtypeStruct((T, H), F32)] * 2)(a_in, b_in, a_log, dt_bias)


def _dn_gates_bwd(dg, dbeta, a_in, b_in, a_log, dt_bias, name):
    T, H = a_in.shape
    C = DN_CHUNK

    def body(dg_ref, db_ref, a_ref, b_ref, al_ref, dt_ref, da_ref, dbi_ref, dal_ref, ddt_ref):
        tri_t = (_iota2((C, C), 0) <= _iota2((C, C), 1)).astype(F32)

        def chunk(n, carry):
            rows = pl.ds(pl.multiple_of(n * C, C), C)
            da_ref[rows, :] = _hdot(tri_t, dg_ref[rows, :])
            return carry

        lax.fori_loop(0, T // C, chunk, 0)
        dla = da_ref[...]
        x = a_ref[...] + dt_ref[...]
        ea = jnp.exp(al_ref[...])
        da = dla * (-ea) * _sigmoid(x)
        da_ref[...] = da
        dal_ref[...] = jnp.sum(dla * (-ea * _softplus(x)), axis=0, keepdims=True)
        ddt_ref[...] = jnp.sum(da, axis=0, keepdims=True)
        beta = _sigmoid(b_ref[...])
        dbi_ref[...] = db_ref[...] * beta * (1.0 - beta)

    return pl.pallas_call(
        body, name=name,
        out_shape=[jax.ShapeDtypeStruct((T, H), F32)] * 2 + [jax.ShapeDtypeStruct((1, H), F32)] * 2,
    )(dg, dbeta, a_in, b_in, a_log, dt_bias)


def _dn_chunk_terms(q, k, gc, bc):
    C = DN_CHUNK
    r, c = _iota2((C, C), 0), _iota2((C, C), 1)
    lower, strict, eye = r >= c, r > c, r == c
    grow = jnp.sum(jnp.where(eye, gc, 0.0), axis=0, keepdims=True)
    decay = jnp.where(lower, jnp.exp(jnp.where(lower, gc - grow, 0.0)), 0.0)
    last = _iota2((C, 1), 0) == C - 1
    gl = jnp.sum(jnp.where(last, gc, 0.0), axis=0, keepdims=True)
    eg = jnp.exp(gc)
    egl = jnp.exp(gl - gc)
    kb = k * bc
    lmat = jnp.where(strict, _bdot(kb, k, NT) * decay, 0.0)
    aqk = jnp.where(lower, _bdot(q, k, NT) * decay, 0.0)
    return dict(lower=lower, strict=strict, eye=eye, last=last, decay=decay, gl=gl, eg=eg, egl=egl, kb=kb,
                lmat=lmat, aqk=aqk, qd=q * eg, kd=k * egl)


def _split(x):
    hi = x.astype(BF16)
    return hi, (x - hi.astype(F32)).astype(BF16)


def _x3dot(a, b, dims=NN):
    ah, al = a if isinstance(a, tuple) else _split(a)
    bh, bl = b if isinstance(b, tuple) else _split(b)
    return _dot(ah, bh, dims) + (_dot(ah, bl, dims) + _dot(al, bh, dims))


def _interleave(gens):
    for _ in itertools.zip_longest(*gens):
        pass


def _unit_lower_inverse_steps(lmat, eye, out):
    ident = jnp.where(eye, 1.0, 0.0).astype(F32)
    m = -lmat
    inv = ident + m
    for _ in range(int(math.log2(DN_CHUNK)) - 1):
        ms = _split(m)
        m = _x3dot(ms, ms)
        yield
        inv = inv + _x3dot(inv, m)
        yield
    out["tm"] = inv


def _dn_chunk_fwd(pqkv, conv_w, g, beta, pgate, gn, name, comm=None):
    T = pqkv.shape[0]
    C, H = DN_CHUNK, DN_HEADS
    N = T // C
    B = DN_PREP_BLK
    nq, nqk = DN_QK_W // B, 2 * DN_QK_W // B

    def step(p_ref, cw_ref, g_ref, b_ref, pg_ref, gn_ref, act_out, o_ref, og_ref, s_out, t_out, vn_out, u_out, w_out,
             s_scr, tail_scr, a_ref, a_next):
        head_lane = _iota2((C, H), 1)

        def prepare(cb):
            cw = slice(cb * B, (cb + 1) * B)
            ext = jnp.concatenate([tail_scr[:, cw], p_ref[:, cw]], axis=0)
            c = _dn_conv(ext, cw_ref, cw)[HALO:]
            yield
            a = c * _sigmoid(c)
            if cb >= nqk:
                a_next[:, cw] = a
                act_out[:, cw] = a
                return
            scale = DN_DK ** -0.5 if cb < nq else 1.0
            for hh in range(B // DN_DK):
                yield
                ah = a[:, hh * DN_DK:(hh + 1) * DN_DK]
                val = ah * (lax.rsqrt(jnp.sum(ah * ah, axis=-1, keepdims=True) + L2_EPS) * scale)
                cols = slice(cb * B + hh * DN_DK, cb * B + (hh + 1) * DN_DK)
                a_next[:, cols] = val
                act_out[:, cols] = val

        def head(hh):
            qs, vs = slice(hh * DN_DK, (hh + 1) * DN_DK), slice(hh * DN_DV, (hh + 1) * DN_DV)
            q, k, v = a_ref[:, qs], a_ref[:, DN_QK_W + hh * DN_DK:DN_QK_W + (hh + 1) * DN_DK], \
                a_ref[:, 2 * DN_QK_W + hh * DN_DV:2 * DN_QK_W + (hh + 1) * DN_DV]
            gc = jnp.sum(jnp.where(head_lane == hh, g_ref[...], 0.0), axis=1, keepdims=True)
            bc = jnp.sum(jnp.where(head_lane == hh, b_ref[...], 0.0), axis=1, keepdims=True)
            t = _dn_chunk_terms(q, k, gc, bc)
            yield
            res = {}
            yield from _unit_lower_inverse_steps(t["lmat"], t["eye"], res)
            tms = _split(res["tm"])
            u = _x3dot(tms, v * bc)
            yield
            w = _x3dot(tms, t["kb"] * t["eg"])
            yield
            s = s_scr[hh]
            s_out[hh] = s
            t_out[hh] = res["tm"]
            sb = s.astype(BF16)
            vn = u - _dot(w.astype(BF16), sb)
            yield
            o = _dot(t["qd"].astype(BF16), sb) + _bdot(t["aqk"], vn)
            yield
            s_scr[hh] = s * jnp.exp(t["gl"]) + _bdot(t["kd"], vn, TN)
            vn_out[:, vs] = vn
            u_out[:, vs] = u
            w_out[:, qs] = w
            o_ref[:, vs] = o
            gate = pg_ref[:, vs]
            r = lax.rsqrt(jnp.mean(o * o, axis=-1, keepdims=True) + RMS_EPS)
            og_ref[:, vs] = (o * r * gn_ref[...] * (gate * _sigmoid(gate))).astype(BF16)

        _interleave([prepare(cb) for cb in range(DN_CONV_W // B)] + [head(hh) for hh in range(H)])

        @pl.when(pl.program_id(0) < N - 1)
        def _():
            tail_scr[...] = p_ref[C - HALO:C, :]

    def body(*refs):
        s = pl.program_id(0)
        io, (s_scr, tail_scr, buf_a, buf_b) = refs[:-4], refs[-4:]

        @pl.when(s == 0)
        def _():
            tail_scr[...] = jnp.zeros_like(tail_scr)
            buf_b[...] = jnp.zeros_like(buf_b)

        @pl.when(s <= 1)
        def _():
            s_scr[...] = jnp.zeros_like(s_scr)

        @pl.when(s % 2 == 0)
        def _():
            step(*io, s_scr, tail_scr, buf_b, buf_a)

        @pl.when(s % 2 == 1)
        def _():
            step(*io, s_scr, tail_scr, buf_a, buf_b)

    nxt = lambda w: pl.BlockSpec((C, w), lambda s: (jnp.minimum(s, N - 1), 0))
    cur = lambda w: pl.BlockSpec((C, w), lambda s: (jnp.maximum(s - 1, 0), 0))
    per_chunk = lambda a, b: pl.BlockSpec((H, None, a, b), lambda s: (0, jnp.maximum(s - 1, 0), 0, 0))
    return _call(
        body, comm, name=name, grid=(N + 1,),
        in_specs=[nxt(DN_CONV_W), pl.BlockSpec((DN_CONV, DN_CONV_W), lambda s: (0, 0)), cur(H), cur(H), cur(DN_V_W),
                  pl.BlockSpec((1, DN_DV), lambda s: (0, 0))],
        out_specs=[nxt(DN_CONV_W), cur(DN_V_W), cur(DN_V_W), per_chunk(DN_DK, DN_DV), per_chunk(C, C),
                   cur(DN_V_W), cur(DN_V_W), cur(DN_QK_W)],
        out_shape=[jax.ShapeDtypeStruct((T, DN_CONV_W), F32),
                   jax.ShapeDtypeStruct((T, DN_V_W), F32), jax.ShapeDtypeStruct((T, DN_V_W), BF16),
                   jax.ShapeDtypeStruct((H, N, DN_DK, DN_DV), F32),
                   jax.ShapeDtypeStruct((H, N, C, C), F32),
                   jax.ShapeDtypeStruct((T, DN_V_W), F32),
                   jax.ShapeDtypeStruct((T, DN_V_W), F32),
                   jax.ShapeDtypeStruct((T, DN_QK_W), F32)],
        scratch_shapes=[pltpu.VMEM((H, DN_DK, DN_DV), F32), pltpu.VMEM((HALO, DN_CONV_W), F32),
                        pltpu.VMEM((C, DN_CONV_W), F32), pltpu.VMEM((C, DN_CONV_W), F32)],
        semantics=("arbitrary",), args=(pqkv, conv_w, g, beta, pgate, gn))


def _dn_chunk_bwd(pqkv, conv_w, act, g, beta, s_saved, tm_saved, vn_saved, u_saved, w_saved, dog, o_raw, pgate, gn,
                  name, comm=None):
    T = act.shape[0]
    C, H = DN_CHUNK, DN_HEADS
    N = T // C
    assert N % 2 == 0
    B = DN_PREP_BLK
    nq, nqk = DN_QK_W // B, 2 * DN_QK_W // B
    main = slice(HALO, HALO + C)

    def prepare_bwd(cb, p_ref, pp_ref, pn_ref, cw_ref, dread, dnext_scr, dp_ref, conv_parts):
        s = pl.program_id(0)
        keep_p = (N - s > 0).astype(F32)
        keep_n = (s > 1).astype(F32)
        cw = slice(cb * B, (cb + 1) * B)
        ext = jnp.concatenate([pp_ref[:, cw] * keep_p, p_ref[:, cw], pn_ref[:, cw]], axis=0)
        c = _dn_conv(ext, cw_ref, cw)
        yield
        sg = _sigmoid(c)
        da_dc = sg * (1.0 + c * (1.0 - sg))
        d_up = jnp.concatenate([jnp.zeros((HALO, B), F32), dread[:, cw], dnext_scr[:, cw] * keep_n], axis=0)
        if cb < nqk:
            a = c * sg
            scale = DN_DK ** -0.5 if cb < nq else 1.0
            normed = []
            for hh in range(B // DN_DK):
                yield
                cols = slice(hh * DN_DK, (hh + 1) * DN_DK)
                ah = a[:, cols]
                r = lax.rsqrt(jnp.sum(ah * ah, axis=-1, keepdims=True) + L2_EPS)
                y = ah * r
                dy = d_up[:, cols] * scale
                normed.append(r * (dy - y * jnp.sum(dy * y, axis=-1, keepdims=True)))
            d_up = jnp.concatenate(normed, axis=1)
        yield
        dc = d_up * da_dc
        dp = (cw_ref[3:4, cw] * dc + cw_ref[2:3, cw] * _up(dc, 1) + cw_ref[1:2, cw] * _up(dc, 2)
              + cw_ref[0:1, cw] * _up(dc, 3))
        dp_ref[:, cw] = dp[main].astype(BF16)
        yield
        dcm = dc[main]
        conv_parts[cb] = jnp.concatenate([jnp.sum(dcm * _down(ext, 3 - k)[main], axis=0, keepdims=True)
                                          for k in range(DN_CONV)], axis=0)

    def finish_prepare(conv_parts, dconv_ref, dread, dnext_scr):
        part = jnp.concatenate([conv_parts[cb] for cb in range(DN_CONV_W // B)], axis=1)

        @pl.when(pl.program_id(0) == 0)
        def _():
            dconv_ref[...] = part

        @pl.when(pl.program_id(0) > 0)
        def _():
            dconv_ref[...] += part

        dnext_scr[...] = dread[0:HALO, :]

    def step(a_ref, g_ref, b_ref, s_ref, t_ref, vn_ref, u_ref, w_ref, dog_ref, o_ref, pg_ref, gn_ref,
             p_ref, pp_ref, pn_ref, cw_ref, dp_ref, dconv_ref, dg_ref, db_ref, dgate_ref, dgn_ref,
             ds_scr, dnext_scr, dwrite, dread):
        head_lane = _iota2((C, H), 1)
        dg_cols, db_cols, dgn_parts, conv_parts = {}, {}, {}, {}

        def output_gate_bwd(hh, vs):
            d, o, gate, gn_v = dog_ref[:, vs], o_ref[:, vs], pg_ref[:, vs], gn_ref[...]
            sg, dsg = _silu_and_grad(gate)
            r = lax.rsqrt(jnp.mean(o * o, axis=-1, keepdims=True) + RMS_EPS)
            n = o * r
            dy = d * sg
            dgate_ref[:, vs] = (d * (n * gn_v) * dsg).astype(BF16)
            dn = dy * gn_v
            dgn_parts[hh] = jnp.sum(dy * n, axis=0, keepdims=True)
            return r * (dn - n * jnp.mean(dn * n, axis=-1, keepdims=True))

        def head(hh):
            qs, vs = slice(hh * DN_DK, (hh + 1) * DN_DK), slice(hh * DN_DV, (hh + 1) * DN_DV)
            ks = slice(DN_QK_W + hh * DN_DK, DN_QK_W + (hh + 1) * DN_DK)
            vas = slice(2 * DN_QK_W + hh * DN_DV, 2 * DN_QK_W + (hh + 1) * DN_DV)
            q, k, v = a_ref[:, qs], a_ref[:, ks], a_ref[:, vas]
            gc = jnp.sum(jnp.where(head_lane == hh, g_ref[...], 0.0), axis=1, keepdims=True)
            bc = jnp.sum(jnp.where(head_lane == hh, b_ref[...], 0.0), axis=1, keepdims=True)
            t = _dn_chunk_terms(q, k, gc, bc)
            yield
            lower, strict, eye = t["lower"], t["strict"], t["eye"]
            decay, eg, egl, kb, qd, kd = t["decay"], t["eg"], t["egl"], t["kb"], t["qd"], t["kd"]
            s, tm, vn, u, w = s_ref[hh], t_ref[hh], vn_ref[:, vs], u_ref[:, vs], w_ref[:, qs]
            d_o = output_gate_bwd(hh, vs)
            ds_next = ds_scr[hh]
            egl_tot = jnp.exp(t["gl"])
            dob, sb, dsb, vnb = d_o.astype(BF16), s.astype(BF16), ds_next.astype(BF16), vn.astype(BF16)

            dvn = _bdot(t["aqk"], dob, TN) + _bdot(kd, dsb)
            yield
            daqk = jnp.where(lower, _dot(dob, vnb, NT), 0.0)
            dqd = _dot(dob, sb, NT)
            dkd = _dot(vnb, dsb, NT)
            yield
            dvnb = dvn.astype(BF16)
            ds_scr[hh] = _bdot(qd, dob, TN) + egl_tot * ds_next - _bdot(w, dvnb, TN)
            dgl = egl_tot * jnp.sum(jnp.sum(s * ds_next, axis=1, keepdims=True), axis=0, keepdims=True)
            dw = -_dot(dvnb, sb, NT)
            yield
            tms = _split(tm)
            dru = _x3dot(tms, dvn, TN)
            drw = _x3dot(tms, dw, TN)
            yield
            dl = -jnp.where(strict, _x3dot(dru, u, NT) + _x3dot(drw, w, NT), 0.0)
            yield
            dkk = (dl * decay).astype(BF16)
            dqk = (daqk * decay).astype(BF16)
            dkb = _bdot(dkk, k) + drw * eg
            yield
            dwrite[:, ks] = _bdot(dkk, kb, TN) + _bdot(dqk, q, TN) + dkd * egl + dkb * bc
            dwrite[:, qs] = _bdot(dqk, k) + dqd * eg
            dwrite[:, vas] = dru * bc
            yield
            db_cols[hh] = jnp.sum(dru * v, axis=1, keepdims=True) + jnp.sum(dkb * k, axis=1, keepdims=True)
            pm = dl * t["lmat"] + daqk * t["aqk"]
            col_as_col = jnp.sum(jnp.where(eye, jnp.sum(pm, axis=0, keepdims=True), 0.0), axis=1, keepdims=True)
            kdsum = jnp.sum(dkd * kd, axis=1, keepdims=True)
            dgc = (jnp.sum(pm, axis=1, keepdims=True) - col_as_col + jnp.sum(dqd * qd, axis=1, keepdims=True)
                   - kdsum + jnp.sum(drw * (kb * eg), axis=1, keepdims=True))
            dgl = dgl + jnp.sum(kdsum, axis=0, keepdims=True)
            dg_cols[hh] = dgc + jnp.where(t["last"], dgl, 0.0)

        _interleave([head(hh) for hh in range(H)]
                    + [prepare_bwd(cb, p_ref, pp_ref, pn_ref, cw_ref, dread, dnext_scr, dp_ref, conv_parts)
                       for cb in range(DN_CONV_W // B)])
        dg_ref[...] = sum(jnp.where(head_lane == hh, dg_cols[hh], 0.0) for hh in range(H))
        db_ref[...] = sum(jnp.where(head_lane == hh, db_cols[hh], 0.0) for hh in range(H))
        dgn_part = sum(dgn_parts[hh] for hh in range(H))

        @pl.when(pl.program_id(0) == 0)
        def _():
            dgn_ref[...] = dgn_part

        @pl.when(pl.program_id(0) > 0)
        def _():
            dgn_ref[...] += dgn_part

        finish_prepare(conv_parts, dconv_ref, dread, dnext_scr)

    def body(*refs):
        s = pl.program_id(0)
        io, (ds_scr, dnext_scr, buf_a, buf_b) = refs[:-4], refs[-4:]
        p_ref, pp_ref, pn_ref, cw_ref, dp_ref, dconv_ref = refs[12:18]

        @pl.when(s == 0)
        def _():
            ds_scr[...] = jnp.zeros_like(ds_scr)
            dnext_scr[...] = jnp.zeros_like(dnext_scr)
            buf_b[...] = jnp.zeros_like(buf_b)

        @pl.when((s < N) & (s % 2 == 0))
        def _():
            step(*io, ds_scr, dnext_scr, buf_a, buf_b)

        @pl.when((s < N) & (s % 2 == 1))
        def _():
            step(*io, ds_scr, dnext_scr, buf_b, buf_a)

        @pl.when(s == N)
        def _():
            conv_parts = {}
            _interleave([prepare_bwd(cb, p_ref, pp_ref, pn_ref, cw_ref, buf_b, dnext_scr, dp_ref, conv_parts)
                         for cb in range(DN_CONV_W // B)])
            finish_prepare(conv_parts, dconv_ref, buf_b, dnext_scr)

    cc = lambda s: jnp.maximum(N - 1 - s, 0)
    pc = lambda s: jnp.clip(N - s, 0, N - 1)
    row = lambda w: pl.BlockSpec((C, w), lambda s: (cc(s), 0))
    per_chunk = lambda a, b: pl.BlockSpec((H, None, a, b), lambda s: (0, cc(s), 0, 0))
    vec = pl.BlockSpec((1, DN_DV), lambda s: (0, 0))
    per_c = C // HALO
    conv_spec = pl.BlockSpec((DN_CONV, DN_CONV_W), lambda s: (0, 0))
    return _call(
        body, comm, name=name, grid=(N + 1,),
        in_specs=[row(DN_CONV_W), row(H), row(H), per_chunk(DN_DK, DN_DV), per_chunk(C, C),
                  row(DN_V_W), row(DN_V_W), row(DN_QK_W), row(DN_V_W), row(DN_V_W), row(DN_V_W), vec,
                  pl.BlockSpec((C, DN_CONV_W), lambda s: (pc(s), 0)),
                  pl.BlockSpec((HALO, DN_CONV_W), lambda s: (jnp.maximum(pc(s) * per_c - 1, 0), 0)),
                  pl.BlockSpec((HALO, DN_CONV_W), lambda s: (jnp.minimum((pc(s) + 1) * per_c, N * per_c - 1), 0)),
                  conv_spec],
        out_specs=[pl.BlockSpec((C, DN_CONV_W), lambda s: (pc(s), 0)), conv_spec, row(H), row(H), row(DN_V_W), vec],
        out_shape=[jax.ShapeDtypeStruct((T, DN_CONV_W), BF16), jax.ShapeDtypeStruct((DN_CONV, DN_CONV_W), F32),
                   jax.ShapeDtypeStruct((T, H), F32), jax.ShapeDtypeStruct((T, H), F32),
                   jax.ShapeDtypeStruct((T, DN_V_W), BF16), jax.ShapeDtypeStruct((1, DN_DV), F32)],
        scratch_shapes=[pltpu.VMEM((H, DN_DK, DN_DV), F32), pltpu.VMEM((HALO, DN_CONV_W), F32),
                        pltpu.VMEM((C, DN_CONV_W), F32), pltpu.VMEM((C, DN_CONV_W), F32)],
        semantics=("arbitrary",),
        args=(act, g, beta, s_saved, tm_saved, vn_saved, u_saved, w_saved, dog, o_raw, pgate, gn,
              pqkv, pqkv, pqkv, conv_w))


def _dn_split_w_in(w):
    return w, jnp.pad(w[:, DN_CONV_W + DN_V_W:], ((0, 0), (0, DN_AB_PAD - 2 * DN_HEADS)))


def _out_proj(og, w_out, x_res, next_g, name):
    if next_g is None:
        return _matmul(og, w_out, "nn", name, add=x_res), None
    return tuple(_matmul(og, w_out, "nn", name, add=x_res, norm_fwd=next_g, tm=NORM_FUSED_TM))


def _dn_layer_fwd(h, wts, conv_w, a_log, dt_bias, gn, w_out, x_res, tag, comm=None, next_g=None):
    w_in, wab = wts
    H = DN_HEADS
    pqkv = _matmul(h, w_in, "nn", tag + "_pqkv", b_cols=(0, DN_CONV_W))
    pgate = _matmul(h, w_in, "nn", tag + "_pgate", b_cols=(DN_CONV_W, DN_V_W))
    pab = _matmul(h, wab, "nn", tag + "_pab")
    a_in, b_in = pab[:, :H], pab[:, H:2 * H]
    g, beta = _dn_gates(a_in, b_in, a_log, dt_bias, tag + "_gates")
    (act, o_raw, og, s_sv, tm_sv, vn_sv, u_sv, w_sv), landed = _dn_chunk_fwd(pqkv, conv_w, g, beta, pgate, gn,
                                                                             tag + "_chunk_fwd", comm)
    if callable(w_out):
        w_out = w_out(landed)
    y = _out_proj(og, w_out, x_res, next_g, tag + "_out")
    saved = dict(h=h, wts=wts, conv_w=conv_w, a_log=a_log, dt_bias=dt_bias, gn=gn, w_out=w_out, pqkv=pqkv, pgate=pgate,
                 a_in=a_in, b_in=b_in, g=g, beta=beta, act=act, o_raw=o_raw, chunk=(s_sv, tm_sv, vn_sv, u_sv, w_sv), og=og)
    return y, saved, landed


def _dn_layer_bwd(dout, sv, tag, norm, comm_of=None, late_comm_of=None):
    w_in, wab = sv["wts"]
    h = sv["h"]
    dog = _matmul(dout, sv["w_out"], "nt", tag + "_dog")
    dw_out = _matmul(sv["og"], dout, "tn", tag + "_dwout", out_dtype=BF16)
    comm = comm_of(dw_out) if comm_of is not None else None
    (dpqkv, dconv, dg, dbeta, dgate, dgn), landed = _dn_chunk_bwd(
        sv["pqkv"], sv["conv_w"], sv["act"], sv["g"], sv["beta"], *sv["chunk"], dog, sv["o_raw"], sv["pgate"], sv["gn"],
        tag + "_chunk_bwd", comm)
    da_in, db_in, da_log, ddt = _dn_gates_bwd(dg, dbeta, sv["a_in"], sv["b_in"], sv["a_log"], sv["dt_bias"],
                                              tag + "_gates_bwd")
    dpab = jnp.pad(jnp.concatenate([da_in, db_in], axis=1), ((0, 0), (0, DN_AB_PAD - 2 * DN_HEADS)))
    dwqkv = _matmul(h, dpqkv, "tn", tag + "_dwqkv", out_dtype=BF16)
    dwgate = _matmul(h, dgate, "tn", tag + "_dwgate", out_dtype=BF16)
    dwab = _matmul(h, dpab, "tn", tag + "_dwab", out_dtype=BF16)
    dw_in = jnp.concatenate([dwqkv, dwgate, dwab[:, :2 * DN_HEADS]], axis=1)
    grads = dict(dn_w_in=dw_in, dn_conv_w=dconv, dn_a_log=da_log, dn_dt_bias=ddt, dn_o_norm_g=dgn, dn_w_out=dw_out)
    dx, landed_late = _matmul_nt_sum([(dpqkv, w_in, 0), (dgate, w_in, DN_CONV_W), (dpab, wab, 0)], tag + "_dh",
                                     late_comm_of(grads) if late_comm_of is not None else None, norm_bwd=norm,
                                     tm=NORM_FUSED_TM if norm is not None else 1024)
    return dx, grads, landed, landed_late


def _sb_layer_fwd(h, w_in, qg, kg, w_out, x_res, tag, comm=None, next_g=None):
    qg2, kg2 = jnp.tile(qg, (1, 2)), jnp.tile(kg, (1, 2))
    proj = _matmul(h, w_in, "nn", tag + "_proj", blocked_b=True)
    qn, kn, vb = _sb_prep(proj, qg2, kg2, tag + "_prep")
    (o, og, ltot, done), landed = _sb_attn_fwd(qn, kn, vb, proj, tag + "_attn_fwd", comm)
    y = _out_proj(og, w_out, x_res, next_g, tag + "_out")
    saved = dict(h=h, w_in=w_in, qg2=qg2, kg2=kg2, w_out=w_out, proj=proj, qn=qn, kn=kn, vb=vb, o=o, og=og, ltot=ltot,
                 done=done)
    return y, saved, landed


def _sb_layer_bwd(dout, sv, tag, comm=None):
    dog = _matmul(dout, sv["w_out"], "nt", tag + "_dog")
    dw_out = _matmul(sv["og"], dout, "tn", tag + "_dwout", out_dtype=BF16)
    (dqn, dkn, dv, dgate), landed = _sb_attn_bwd(sv["qn"], sv["kn"], sv["vb"], dog, sv["o"], sv["ltot"], sv["done"],
                                                 sv["proj"], tag + "_attn_bwd", comm)
    dproj, dqgp, dkgp = _sb_prep_bwd(sv["proj"], dqn, dkn, dv, dgate, sv["qg2"], sv["kg2"], tag + "_prep_bwd")
    dw_in = _matmul(sv["h"], dproj, "tn", tag + "_dwin", out_dtype=BF16, blocked_out=N_DEV)
    dh = _matmul(dproj, sv["w_in"], "nt", tag + "_dh", blocked_b=True)
    dqg = _fold_heads(dqgp, tag + "_dqg")
    dkg = _fold_heads(dkgp, tag + "_dkg")
    return dh, dict(sb_w_in=dw_in, sb_q_norm_g=dqg, sb_k_norm_g=dkg, sb_w_out=dw_out), landed


def _sc_layer_fwd(h, w_in, conv_w, w_out, x_res, tag, next_g=None):
    proj = _matmul(h, w_in, "nn", tag + "_proj", blocked_b=True)
    yg = _sc_fwd(proj, conv_w, tag + "_fwd")
    y = _out_proj(yg, w_out, x_res, next_g, tag + "_out")
    return y, dict(h=h, w_in=w_in, conv_w=conv_w, w_out=w_out, proj=proj, yg=yg)


def _sc_layer_bwd(dout, sv, tag):
    dyg = _matmul(dout, sv["w_out"], "nt", tag + "_dyg")
    dw_out = _matmul(sv["yg"], dout, "tn", tag + "_dwout", out_dtype=BF16)
    dproj, dconv = _sc_bwd(dyg, sv["proj"], sv["conv_w"], tag + "_bwd")
    dw_in = _matmul(sv["h"], dproj, "tn", tag + "_dwin", out_dtype=BF16, blocked_out=N_DEV)
    dh = _matmul(dproj, sv["w_in"], "nt", tag + "_dh", blocked_b=True)
    return dh, dict(sc_w_in=dw_in, sc_conv_w=dconv, sc_w_out=dw_out)


def _adamw(w, m, v, parts, name):
    L, R, C = w.shape
    tr = _tile(R, 128, SUBLANE)

    def body(*refs):
        w_ref, m_ref, v_ref = refs[:3]
        g_ref, d_ref, nm_ref, nv_ref = refs[3 + L:]

        def update(p_ref):
            g = p_ref[0].astype(F32)
            for s in range(1, N_DEV):
                g = g + p_ref[s].astype(F32)
            m2 = ADAM_B1 * m_ref[...] + (1.0 - ADAM_B1) * g
            v2 = ADAM_B2 * v_ref[...] + (1.0 - ADAM_B2) * (g * g)
            m_hat = m2 / (1.0 - ADAM_B1 ** ADAM_STEP)
            v_hat = v2 / (1.0 - ADAM_B2 ** ADAM_STEP)
            g_ref[...] = g
            d_ref[...] = -ADAM_LR * (m_hat / (jnp.sqrt(v_hat) + ADAM_EPS) + ADAM_WD * w_ref[...])
            nm_ref[...] = m2
            nv_ref[...] = v2

        for layer in range(L):
            pl.when(pl.program_id(0) == layer)(functools.partial(update, refs[3 + layer]))

    blk = pl.BlockSpec((None, tr, C), lambda l, i: (l, i, 0))
    landing = pl.BlockSpec((N_DEV, tr, C), lambda l, i: (0, i, 0))
    return pl.pallas_call(
        body, name=name, grid=(L, R // tr),
        in_specs=[blk, blk, blk] + [landing] * L,
        out_specs=[blk] * 4, out_shape=[jax.ShapeDtypeStruct((L, R, C), F32)] * 4,
        compiler_params=_params("parallel", "parallel"),
    )(w, m, v, *parts)


_HBM = pl.BlockSpec(memory_space=pltpu.HBM)
_MESH = pl.DeviceIdType.MESH


def _slot(x, y, c):
    return 4 * x + 2 * y + c


class _Gather:
    def __init__(self, shards):
        self.arrays = list(shards)
        n = len(self.arrays)
        self.out_shapes = [jax.ShapeDtypeStruct((N_DEV,) + s.shape, s.dtype) for s in self.arrays]
        self.scratch = [pltpu.SemaphoreType.DMA((n, N_DEV - 1)), pltpu.SemaphoreType.DMA((n, N_DEV - 1)),
                        pltpu.SemaphoreType.DMA((n,))]

    def _parts(self, ins, outs, sems):
        send_sems, recv_sems, local_sems = sems
        n = len(self.arrays)
        x, y, c = lax.axis_index("x"), lax.axis_index("y"), lax.axis_index("c")
        me, sibling = (x, y, c), (x, y, 1 - c)
        chips = [(1 - x, y), (x, 1 - y), (1 - x, 1 - y)]

        def copy(a, k, block, to, src=None):
            dst = outs[a].at[_slot(*block)]
            return pltpu.make_async_remote_copy(src_ref=dst if src is None else src, dst_ref=dst,
                                                send_sem=send_sems.at[a, k], recv_sem=recv_sems.at[a, k],
                                                device_id=to, device_id_type=_MESH)

        mine = [pltpu.make_async_copy(ins[a], outs[a].at[_slot(*me)], local_sems.at[a]) for a in range(n)]
        first = []
        for a in range(n):
            first.append(copy(a, 0, me, sibling, src=ins[a]))
            first += [copy(a, 1 + j, me, (*chip, c), src=ins[a]) for j, chip in enumerate(chips)]
        return n, c, me, sibling, chips, copy, mine, first

    def start(self, ins, outs, sems):
        _, _, _, _, _, _, mine, first = self._parts(ins, outs, sems)
        for cp in mine + first:
            cp.start()

    def finish(self, ins, outs, sems):
        n, c, me, sibling, chips, copy, mine, first = self._parts(ins, outs, sems)
        passed = []
        for j, chip in enumerate(chips):
            for a in range(n):
                copy(a, 1 + j, (*chip, c), me).wait_recv()
                fwd = copy(a, 4 + j, (*chip, c), sibling)
                fwd.start()
                passed.append(fwd)
        for a in range(n):
            copy(a, 0, sibling, me).wait_recv()
            for j, chip in enumerate(chips):
                copy(a, 4 + j, (*chip, 1 - c), me).wait_recv()
        for cp in first + passed:
            cp.wait_send()
        for cp in mine:
            cp.wait()


class _Exchange:
    def __init__(self, arrays, scatter):
        self.arrays, self.scatter = list(arrays), list(scatter)
        n = len(self.arrays)
        shapes = [a.shape[1:] if s else a.shape for a, s in zip(self.arrays, self.scatter)]
        self.out_shapes = [jax.ShapeDtypeStruct((N_DEV,) + tuple(s), a.dtype) for s, a in zip(shapes, self.arrays)]
        self.scratch = [pltpu.SemaphoreType.DMA((n, N_DEV - 1)), pltpu.SemaphoreType.DMA((n, N_DEV - 1)),
                        pltpu.SemaphoreType.DMA((n,))]

    def _copies(self, ins, outs, sems):
        send_sems, recv_sems, local_sems = sems
        n, scatter = len(self.arrays), self.scatter
        x, y, c = lax.axis_index("x"), lax.axis_index("y"), lax.axis_index("c")
        me = _slot(x, y, c)
        copies = [pltpu.make_async_copy(ins[a].at[me] if scatter[a] else ins[a], outs[a].at[me], local_sems.at[a])
                  for a in range(n)]
        for r in range(1, N_DEV):
            px = 1 - x if r & 4 else x
            py = 1 - y if r & 2 else y
            pc = 1 - c if r & 1 else c
            for a in range(n):
                copies.append(pltpu.make_async_remote_copy(
                    src_ref=ins[a].at[_slot(px, py, pc)] if scatter[a] else ins[a], dst_ref=outs[a].at[me],
                    send_sem=send_sems.at[a, r - 1], recv_sem=recv_sems.at[a, r - 1],
                    device_id=(px, py, pc), device_id_type=_MESH))
        return copies

    def start(self, ins, outs, sems):
        for cp in self._copies(ins, outs, sems):
            cp.start()

    def finish(self, ins, outs, sems):
        for cp in self._copies(ins, outs, sems):
            cp.wait()


def _comm_call(comm, name):
    n = len(comm.arrays)

    def body(*refs):
        ins, outs, sems = refs[:n], refs[n:2 * n], refs[2 * n:]
        comm.start(ins, outs, sems)
        comm.finish(ins, outs, sems)

    return pl.pallas_call(body, name=name, in_specs=[_HBM] * n, out_specs=[_HBM] * n, out_shape=comm.out_shapes,
                          scratch_shapes=comm.scratch)(*comm.arrays)


def _call(body, comm, *, name, grid, in_specs, out_specs, out_shape, scratch_shapes, semantics, args):
    if comm is None:
        outs = pl.pallas_call(body, name=name, grid=grid, in_specs=in_specs, out_specs=out_specs, out_shape=out_shape,
                              scratch_shapes=scratch_shapes, compiler_params=_params(*semantics))(*args)
        return outs, []
    n_in, n_out, n_scr, n_c = len(in_specs), len(out_specs), len(scratch_shapes), len(comm.arrays)

    def fused(*refs):
        ins, refs = refs[:n_in], refs[n_in:]
        c_ins, refs = refs[:n_c], refs[n_c:]
        outs, refs = refs[:n_out], refs[n_out:]
        c_outs, refs = refs[:n_c], refs[n_c:]
        scr, sems = refs[:n_scr], refs[n_scr:]
        ids = [pl.program_id(d) for d in range(len(grid))]
        first = functools.reduce(jnp.logical_and, [i == 0 for i in ids])
        last = functools.reduce(jnp.logical_and, [i == g - 1 for i, g in zip(ids, grid)])

        @pl.when(first)
        def _():
            comm.start(c_ins, c_outs, sems)

        body(*ins, *outs, *scr)

        @pl.when(last)
        def _():
            comm.finish(c_ins, c_outs, sems)

    outs = pl.pallas_call(
        fused, name=name, grid=grid, in_specs=list(in_specs) + [_HBM] * n_c, out_specs=list(out_specs) + [_HBM] * n_c,
        out_shape=list(out_shape) + comm.out_shapes, scratch_shapes=list(scratch_shapes) + comm.scratch,
        compiler_params=_params(*["arbitrary"] * len(grid)))(*args, *comm.arrays)
    return outs[:n_out], outs[n_out:]


_GATHER_0 = (("dn_w_in", 0), ("dn_conv_w", 0), ("dn_o_norm_g", 0))
_GATHER_1 = (("dn_w_out", 0), ("sb_w_in", 0), ("sb_w_out", 0), ("sc_w_out", 0), ("dn_w_out", 1))
_GATHER_2 = (("sc_w_in", 0), ("sc_conv_w", 0), ("dn_w_in", 1), ("dn_conv_w", 1), ("dn_o_norm_g", 1))
_EXCHANGE_A = _GATHER_2
_EXCHANGE_B = (("sb_w_in", 0), ("sb_w_out", 0), ("dn_w_out", 0), ("sc_w_out", 0), ("dn_w_out", 1))
_EXCHANGE_C = _GATHER_0
_MATMUL_WEIGHTS = ("dn_w_in", "dn_w_out", "sb_w_in", "sb_w_out", "sc_w_in", "sc_w_out")
_COLUMN_SHARDED = ("dn_w_in", "dn_conv_w", "dn_o_norm_g", "sb_w_in", "sc_w_in", "sc_conv_w")
_BLOCKED = ("sb_w_in", "sc_w_in")
_REPLICATED = ("norm_g", "dn_a_log", "dn_dt_bias", "sb_q_norm_g", "sb_k_norm_g")
_ORDER = ("norm_g", "dn_w_in", "dn_conv_w", "dn_a_log", "dn_dt_bias", "dn_o_norm_g", "dn_w_out", "sb_w_in", "sb_q_norm_g",
          "sb_k_norm_g", "sb_w_out", "sc_w_in", "sc_conv_w", "sc_w_out")
_PACK_COLS = D_MODEL


def _as_2d(a):
    return a.reshape(1, -1) if a.ndim == 1 else a


def _assemble(name, gathered):
    n, r, c = gathered.shape
    if name in _COLUMN_SHARDED:
        return jnp.moveaxis(gathered, 0, 1).reshape(r, n * c)
    return gathered.reshape(n * r, c)


def _disassemble(name, full):
    r, c = full.shape
    if name in _COLUMN_SHARDED:
        return jnp.moveaxis(full.reshape(r, N_DEV, c // N_DEV), 1, 0)
    return full.reshape(N_DEV, r // N_DEV, c)


def _pack_replicated(d):
    rows = [d["norm_g"]]
    for name in _REPLICATED[1:]:
        flat = d[name].reshape(1, -1)
        rows.append(jnp.pad(flat, ((0, 0), (0, _PACK_COLS - flat.shape[1]))))
    return jnp.concatenate(rows, axis=0)


def _unpack_replicated(p, like):
    out = {"norm_g": p[:4]}
    for r, name in enumerate(_REPLICATED[1:]):
        shape = like[name].shape
        out[name] = p[4 + r, :math.prod(shape)].reshape(shape)
    return out


def kernel(x, norm_g, dn_w_in, dn_conv_w, dn_a_log, dn_dt_bias, dn_o_norm_g, dn_w_out, sb_w_in, sb_q_norm_g, sb_k_norm_g, sb_w_out, sc_w_in, sc_conv_w, sc_w_out, loss_target, m_norm_g, m_dn_w_in, m_dn_conv_w, m_dn_a_log, m_dn_dt_bias, m_dn_o_norm_g, m_dn_w_out, m_sb_w_in, m_sb_q_norm_g, m_sb_k_norm_g, m_sb_w_out, m_sc_w_in, m_sc_conv_w, m_sc_w_out, v_norm_g, v_dn_w_in, v_dn_conv_w, v_dn_a_log, v_dn_dt_bias, v_dn_o_norm_g, v_dn_w_out, v_sb_w_in, v_sb_q_norm_g, v_sb_k_norm_g, v_sb_w_out, v_sc_w_in, v_sc_conv_w, v_sc_w_out):
    w = dict(norm_g=norm_g, dn_w_in=dn_w_in, dn_conv_w=dn_conv_w, dn_a_log=dn_a_log, dn_dt_bias=dn_dt_bias,
             dn_o_norm_g=dn_o_norm_g, dn_w_out=dn_w_out, sb_w_in=sb_w_in, sb_q_norm_g=sb_q_norm_g, sb_k_norm_g=sb_k_norm_g,
             sb_w_out=sb_w_out, sc_w_in=sc_w_in, sc_conv_w=sc_conv_w, sc_w_out=sc_w_out)
    m = dict(norm_g=m_norm_g, dn_w_in=m_dn_w_in, dn_conv_w=m_dn_conv_w, dn_a_log=m_dn_a_log, dn_dt_bias=m_dn_dt_bias,
             dn_o_norm_g=m_dn_o_norm_g, dn_w_out=m_dn_w_out, sb_w_in=m_sb_w_in, sb_q_norm_g=m_sb_q_norm_g,
             sb_k_norm_g=m_sb_k_norm_g, sb_w_out=m_sb_w_out, sc_w_in=m_sc_w_in, sc_conv_w=m_sc_conv_w, sc_w_out=m_sc_w_out)
    v = dict(norm_g=v_norm_g, dn_w_in=v_dn_w_in, dn_conv_w=v_dn_conv_w, dn_a_log=v_dn_a_log, dn_dt_bias=v_dn_dt_bias,
             dn_o_norm_g=v_dn_o_norm_g, dn_w_out=v_dn_w_out, sb_w_in=v_sb_w_in, sb_q_norm_g=v_sb_q_norm_g,
             sb_k_norm_g=v_sb_k_norm_g, sb_w_out=v_sb_w_out, sc_w_in=v_sc_w_in, sc_conv_w=v_sc_conv_w, sc_w_out=v_sc_w_out)

    def gather_of(keys):
        return _Gather([_as_2d(w[k][j]).astype(BF16) if k in _MATMUL_WEIGHTS else _as_2d(w[k][j]) for k, j in keys])

    def full_weights(keys, gathered):
        return {key: g if key[0] in _BLOCKED else _assemble(key[0], g) for key, g in zip(keys, gathered)}

    def exchange_of(keys, grads, extra=()):
        out = [grads[k, j] if k in _BLOCKED else
               _disassemble(k, grads[k, j].astype(BF16) if k in _MATMUL_WEIGHTS else grads[k, j]) for k, j in keys]
        return _Exchange(out + list(extra), [True] * len(out) + [False] * len(extra))

    xs, saves = [x[0]], []
    h, got = _rmsnorm_fwd(xs[0], norm_g[0:1], "norm0", gather_of(_GATHER_0))
    F = full_weights(_GATHER_0, got)

    def w_out_0(got):
        F.update(full_weights(_GATHER_1, got))
        return F["dn_w_out", 0]

    (y, h), sv, _ = _dn_layer_fwd(h, _dn_split_w_in(F["dn_w_in", 0]), F["dn_conv_w", 0], dn_a_log[0:1], dn_dt_bias[0:1],
                                  F["dn_o_norm_g", 0], w_out_0, xs[0], "dn0", gather_of(_GATHER_1), norm_g[1:2])
    xs.append(y)
    saves.append(sv)
    (y, h), sv, got = _sb_layer_fwd(h, F["sb_w_in", 0], sb_q_norm_g, sb_k_norm_g, F["sb_w_out", 0], xs[1], "sb",
                                    gather_of(_GATHER_2), norm_g[2:3])
    F.update(full_weights(_GATHER_2, got))
    xs.append(y)
    saves.append(sv)
    (y, h), sv = _sc_layer_fwd(h, F["sc_w_in", 0], F["sc_conv_w", 0], F["sc_w_out", 0], xs[2], "sc", norm_g[3:4])
    xs.append(y)
    saves.append(sv)
    (y, _), sv, _ = _dn_layer_fwd(h, _dn_split_w_in(F["dn_w_in", 1]), F["dn_conv_w", 1], dn_a_log[1:2], dn_dt_bias[1:2],
                                  F["dn_o_norm_g", 1], F["dn_w_out", 1], xs[3], "dn1")
    xs.append(y)
    saves.append(sv)
    dx, loss_part = _loss_head(xs[4], loss_target[0])

    G, dnorm, landed = {}, [None] * 4, {}

    def keep(grads, j):
        G.update({(k, j): g for k, g in grads.items()})

    dh, grads, _, _ = _dn_layer_bwd(dx, saves[3], "dn1", None)
    keep(grads, 1)
    dx, dnorm[3] = _rmsnorm_bwd(dh, xs[3], norm_g[3:4], dx, "norm3_bwd")
    dh, grads = _sc_layer_bwd(dx, saves[2], "sc")
    keep(grads, 0)
    dx, dnorm[2] = _rmsnorm_bwd(dh, xs[2], norm_g[2:3], dx, "norm2_bwd")
    dh, grads, got = _sb_layer_bwd(dx, saves[1], "sb", exchange_of(_EXCHANGE_A, G))
    keep(grads, 0)
    landed.update(zip(_EXCHANGE_A, got))
    dx, dnorm[1] = _rmsnorm_bwd(dh, xs[1], norm_g[1:2], dx, "norm1_bwd")

    def exchange_b(dw_out):
        G["dn_w_out", 0] = dw_out
        return exchange_of(_EXCHANGE_B, G)

    def exchange_c(grads):
        keep(grads, 0)
        return exchange_of(_EXCHANGE_C, G)

    (dx, dnorm[0]), grads, got, got_late = _dn_layer_bwd(dx, saves[0], "dn0", (xs[0], norm_g[0:1], dx), exchange_b, exchange_c)
    landed.update(zip(_EXCHANGE_B, got))
    landed.update(zip(_EXCHANGE_C, got_late))
    replicated = dict(norm_g=jnp.concatenate(dnorm, axis=0),
                      dn_a_log=jnp.concatenate([G["dn_a_log", 0], G["dn_a_log", 1]], axis=0),
                      dn_dt_bias=jnp.concatenate([G["dn_dt_bias", 0], G["dn_dt_bias", 1]], axis=0),
                      sb_q_norm_g=G["sb_q_norm_g", 0], sb_k_norm_g=G["sb_k_norm_g", 0])
    got = _comm_call(_Exchange([_pack_replicated(replicated)], [False]), "exchange_replicated")

    res = {}
    for k in _ORDER:
        if k in _REPLICATED:
            continue
        shape = w[k].shape
        as_3d = lambda a: a.reshape(shape[0], math.prod(shape[1:-1]), shape[-1])
        outs = _adamw(as_3d(w[k]), as_3d(m[k]), as_3d(v[k]), [landed[k, j] for j in range(shape[0])], "adamw_" + k)
        res[k] = [o.reshape(shape) for o in outs]
    outs = _adamw(_pack_replicated(w)[None], _pack_replicated(m)[None], _pack_replicated(v)[None], [got[-1]],
                  "adamw_replicated")
    unpacked = [_unpack_replicated(o[0], w) for o in outs]
    for k in _REPLICATED:
        res[k] = [u[k] for u in unpacked]

    loss = lax.psum(loss_part[0, 0], ("x", "y", "c"))
    return (loss, dx[None]) + tuple(res[k][0] for k in _ORDER) + tuple(res[k][1] for k in _ORDER) \
        + tuple(res[k][2] for k in _ORDER) + tuple(res[k][3] for k in _ORDER)
```

```python
import functools
import itertools
import math

import jax
import jax.numpy as jnp
from jax import lax
from jax.experimental import pallas as pl
from jax.experimental.pallas import tpu as pltpu

F32 = jnp.float32
BF16 = jnp.bfloat16
HIGHEST = lax.Precision.HIGHEST

N_DEV = 8
D_MODEL = 1024
RMS_EPS = 1e-6
L2_EPS = 1e-6

DN_HEADS = 8
DN_DK = 128
DN_DV = 256
DN_QK_W = DN_HEADS * DN_DK
DN_V_W = DN_HEADS * DN_DV
DN_CONV = 4
DN_CHUNK = 64
DN_CONV_W = 2 * DN_QK_W + DN_V_W
DN_IN = DN_CONV_W + DN_V_W + 2 * DN_HEADS
DN_AB_PAD = 128
DN_PREP_BLK = 512

SB_HEADS = 16
SB_DH = 64
SB_W = SB_HEADS * SB_DH
SB_PAIRS = SB_HEADS // 2
SB_TQ = 256
SB_TK = 128
SB_DEAD = -106.0

SC_W = 2 * D_MODEL
SC_CONV = 3
SC_BLK = 512
SC_NBLK = SC_W // SC_BLK

ADAM_LR = 0.001
ADAM_B1 = 0.9
ADAM_B2 = 0.999
ADAM_EPS = 1e-08
ADAM_WD = 0.01
ADAM_STEP = 10

LANE = 128
SUBLANE = 8
HALO = SUBLANE
LONG_ROW_TILE = 512
NORM_FUSED_TM = 512
DEEP_TK = 2048
WIDE_TN = 2048
WIDE_ROW_TILE = 128
VMEM_LIMIT = 48 * 2 ** 20

NN = ((1,), (0,))
NT = ((1,), (1,))
TN = ((0,), (0,))


def _dot(a, b, dims=NN, precision=None):
    return lax.dot_general(a, b, (dims, ((), ())), precision=precision, preferred_element_type=F32)


def _bdot(a, b, dims=NN):
    return _dot(a.astype(BF16), b.astype(BF16), dims)


def _hdot(a, b, dims=NN):
    return _dot(a, b, dims, precision=HIGHEST)


def _tile(dim, pref, align=LANE):
    t = (min(pref, dim) // align) * align
    while t >= align:
        if dim % t == 0:
            return t
        t -= align
    return dim


def _params(*sem):
    return pltpu.CompilerParams(dimension_semantics=sem, vmem_limit_bytes=VMEM_LIMIT)


def _sigmoid(x):
    return 0.5 * jnp.tanh(0.5 * x) + 0.5


def _softplus(x):
    return jnp.maximum(x, 0.0) + jnp.log(1.0 + jnp.exp(-jnp.abs(x)))


def _silu_and_grad(x):
    s = _sigmoid(x)
    return x * s, s * (1.0 + x * (1.0 - s))


def _iota2(shape, dim):
    return lax.broadcasted_iota(jnp.int32, shape, dim)


def _matmul(a, b, mode, name, out_dtype=F32, add=None, b_cols=None, blocked_b=False, blocked_out=0,
            norm_fwd=None, norm_bwd=None, tm=1024, tn=1024, tk=1024):
    b_rows, b_width = (b.shape[1], b.shape[0] * b.shape[2]) if blocked_b else b.shape
    c0, b_used = b_cols if b_cols is not None else (0, b_width)
    if mode == "nn":
        (M, K), (K2, N) = a.shape, (b_rows, b_used)
    elif mode == "nt":
        (M, K), (N, K2) = a.shape, (b_rows, b_used)
    else:
        (K, M), (K2, N) = a.shape, (b_rows, b_used)
    assert K == K2, (a.shape, b.shape, mode)
    if mode == "tn":
        tk = max(tk, DEEP_TK)
    elif norm_fwd is None and norm_bwd is None and add is None:
        tn = max(tn, WIDE_TN)
    tm, tn, tk = _tile(M, tm), _tile(N, tn), _tile(K, tk)
    if blocked_b and mode == "nt":
        tk = b.shape[2]
    elif blocked_b:
        tn = b.shape[2]
    if blocked_out:
        tn = N // blocked_out
    nk = K // tk
    dims = {"nn": NN, "nt": NT, "tn": TN}[mode]
    a_spec = pl.BlockSpec((tk, tm), lambda i, j, k: (k, i)) if mode == "tn" else pl.BlockSpec((tm, tk), lambda i, j, k: (i, k))
    if mode == "nt":
        cb0 = c0 // tk
        assert c0 % tk == 0
        b_spec = (pl.BlockSpec((None, tn, tk), lambda i, j, k: (k + cb0, j, 0)) if blocked_b
                  else pl.BlockSpec((tn, tk), lambda i, j, k: (j, k + cb0)))
    else:
        cb0 = c0 // tn
        assert c0 % tn == 0
        b_spec = (pl.BlockSpec((None, tk, tn), lambda i, j, k: (j + cb0, k, 0)) if blocked_b
                  else pl.BlockSpec((tk, tn), lambda i, j, k: (k, j + cb0)))
    o_spec = pl.BlockSpec((tm, tn), lambda i, j, k: (i, j))
    out_spec = pl.BlockSpec((None, tm, tn), lambda i, j, k: (j, i, 0)) if blocked_out else o_spec
    out_shape = (blocked_out, M, tn) if blocked_out else (M, N)
    has_add = add is not None
    vec_spec = pl.BlockSpec((1, tn), lambda i, j, k: (0, j))
    assert not (norm_fwd is not None or norm_bwd is not None) or tn == N
    extra_in, extra_specs = [], []
    if has_add:
        extra_in, extra_specs = [add], [o_spec]
    if norm_fwd is not None:
        extra_in, extra_specs = extra_in + [norm_fwd], extra_specs + [vec_spec]
        out_specs = [o_spec, o_spec]
        out_shapes = [jax.ShapeDtypeStruct((M, N), out_dtype), jax.ShapeDtypeStruct((M, N), BF16)]
    elif norm_bwd is not None:
        extra_in, extra_specs = extra_in + list(norm_bwd), extra_specs + [o_spec, vec_spec, o_spec]
        out_specs = [o_spec, vec_spec]
        out_shapes = [jax.ShapeDtypeStruct((M, N), F32), jax.ShapeDtypeStruct((1, N), F32)]
    else:
        out_specs, out_shapes = out_spec, jax.ShapeDtypeStruct(out_shape, out_dtype)

    def body(*refs):
        a_ref, b_ref = refs[0], refs[1]
        extra = list(refs[2:2 + len(extra_in)])
        outs = refs[2 + len(extra_in):]
        add_ref = extra.pop(0) if has_add else None
        p = _bdot(a_ref[...], b_ref[...], dims)

        def finish(acc):
            if has_add:
                acc = acc + add_ref[...]
            if norm_bwd is not None:
                _rmsnorm_bwd_tile(acc, *extra, outs[0], outs[1], first=pl.program_id(0) == 0)
                return
            outs[0][...] = acc.astype(out_dtype)
            if norm_fwd is not None:
                r = lax.rsqrt(jnp.mean(acc * acc, axis=-1, keepdims=True) + RMS_EPS)
                outs[1][...] = (acc * r * extra[0][...]).astype(BF16)

        if nk == 1:
            finish(p)
        else:
            acc_ref = refs[-1]
            k = pl.program_id(2)

            @pl.when(k == 0)
            def _():
                acc_ref[...] = p

            @pl.when(k > 0)
            def _():
                acc_ref[...] += p

            @pl.when(k == nk - 1)
            def _():
                finish(acc_ref[...])

    return pl.pallas_call(
        body, name=name, grid=(M // tm, N // tn, nk),
        in_specs=[a_spec, b_spec] + extra_specs, out_specs=out_specs, out_shape=out_shapes,
        scratch_shapes=[pltpu.VMEM((tm, tn), F32)] if nk > 1 else [],
        compiler_params=(_params("arbitrary", "arbitrary", "arbitrary") if norm_bwd is not None
                         else _params("parallel", "parallel", "arbitrary")),
    )(a, b, *extra_in)


def _rmsnorm_bwd_tile(dh, x_ref, g_ref, res_ref, dx_ref, dg_ref, first):
    xv = x_ref[...]
    r = lax.rsqrt(jnp.mean(xv * xv, axis=-1, keepdims=True) + RMS_EPS)
    xh = xv * r
    dxh = dh * g_ref[...]
    m = jnp.mean(dxh * xh, axis=-1, keepdims=True)
    dx_ref[...] = res_ref[...] + r * (dxh - xh * m)
    part = jnp.sum(dh * xh, axis=0, keepdims=True)

    @pl.when(first)
    def _():
        dg_ref[...] = part

    @pl.when(jnp.logical_not(first))
    def _():
        dg_ref[...] += part


def _matmul_nt_sum(pairs, name, comm=None, norm_bwd=None, tm=NORM_FUSED_TM, tk=1024):
    M, N = pairs[0][0].shape[0], pairs[0][1].shape[0]
    tm = _tile(M, tm)
    tks = [_tile(a.shape[1], tk) for a, _, _ in pairs]
    steps = [a.shape[1] // t for (a, _, _), t in zip(pairs, tks)]
    offs = [sum(steps[:p]) for p in range(len(pairs))]
    total = sum(steps)

    n_extra = 3 if norm_bwd is not None else 0

    def body(*refs):
        a_refs, b_refs = refs[0:2 * len(pairs):2], refs[1:2 * len(pairs):2]
        extra = refs[2 * len(pairs):2 * len(pairs) + n_extra]
        outs, acc_ref = refs[2 * len(pairs) + n_extra:-1], refs[-1]
        k = pl.program_id(1)
        for p in range(len(pairs)):
            @pl.when((k >= offs[p]) & (k < offs[p] + steps[p]))
            def _(p=p):
                prod = _bdot(a_refs[p][...], b_refs[p][...], NT)
                if p == 0:
                    @pl.when(k == 0)
                    def _():
                        acc_ref[...] = prod

                    @pl.when(k > 0)
                    def _():
                        acc_ref[...] += prod
                else:
                    acc_ref[...] += prod

        @pl.when(k == total - 1)
        def _():
            if norm_bwd is not None:
                _rmsnorm_bwd_tile(acc_ref[...], *extra, outs[0], outs[1], first=pl.program_id(0) == 0)
            else:
                outs[0][...] = acc_ref[...]

    in_specs, args = [], []
    for (a, b, c0), t, off, n in zip(pairs, tks, offs, steps):
        assert c0 % t == 0
        pick = lambda k, off=off, n=n: jnp.clip(k - off, 0, n - 1)
        in_specs += [pl.BlockSpec((tm, t), lambda i, k, pick=pick: (i, pick(k))),
                     pl.BlockSpec((N, t), lambda i, k, pick=pick, cb0=c0 // t: (0, pick(k) + cb0))]
        args += [a, b]
    row, vec = pl.BlockSpec((tm, N), lambda i, k: (i, 0)), pl.BlockSpec((1, N), lambda i, k: (0, 0))
    if norm_bwd is not None:
        in_specs += [row, vec, row]
        args += list(norm_bwd)
        out_specs, out_shape = [row, vec], [jax.ShapeDtypeStruct((M, N), F32), jax.ShapeDtypeStruct((1, N), F32)]
    else:
        out_specs, out_shape = [row], [jax.ShapeDtypeStruct((M, N), F32)]
    outs, landed = _call(body, comm, name=name, grid=(M // tm, total), in_specs=in_specs, out_specs=out_specs,
                         out_shape=out_shape, scratch_shapes=[pltpu.VMEM((tm, N), F32)],
                         semantics=("arbitrary", "arbitrary"), args=tuple(args))
    return (outs if norm_bwd is not None else outs[0]), landed


def _rmsnorm_fwd(x, g, name, comm=None):
    T, D = x.shape
    tt = _tile(T, LONG_ROW_TILE, SUBLANE)

    def body(x_ref, g_ref, o_ref):
        xv = x_ref[...]
        r = lax.rsqrt(jnp.mean(xv * xv, axis=-1, keepdims=True) + RMS_EPS)
        o_ref[...] = (xv * r * g_ref[...]).astype(BF16)

    outs, landed = _call(
        body, comm, name=name, grid=(T // tt,),
        in_specs=[pl.BlockSpec((tt, D), lambda i: (i, 0)), pl.BlockSpec((1, D), lambda i: (0, 0))],
        out_specs=[pl.BlockSpec((tt, D), lambda i: (i, 0))], out_shape=[jax.ShapeDtypeStruct((T, D), BF16)],
        scratch_shapes=[], semantics=("parallel",), args=(x, g))
    return outs[0], landed


def _rmsnorm_bwd(dh, x, g, dx_res, name):
    T, D = x.shape
    tt = _tile(T, LONG_ROW_TILE, SUBLANE)

    def body(dh_ref, x_ref, g_ref, res_ref, dx_ref, dg_ref):
        _rmsnorm_bwd_tile(dh_ref[...], x_ref, g_ref, res_ref, dx_ref, dg_ref, first=pl.program_id(0) == 0)

    row = pl.BlockSpec((tt, D), lambda i: (i, 0))
    vec = pl.BlockSpec((1, D), lambda i: (0, 0))
    return pl.pallas_call(
        body, name=name, grid=(T // tt,),
        in_specs=[row, row, vec, row], out_specs=[row, vec],
        out_shape=[jax.ShapeDtypeStruct((T, D), F32), jax.ShapeDtypeStruct((1, D), F32)],
        compiler_params=_params("arbitrary"),
    )(dh, x, g, dx_res)


def _loss_head(y, target, name="loss_head"):
    T, D = y.shape
    tt = _tile(T, LONG_ROW_TILE, SUBLANE)

    def body(y_ref, t_ref, dy_ref, l_ref):
        e = y_ref[...] - t_ref[...]
        dy_ref[...] = e * (1.0 / D)
        s = jnp.sum(jnp.sum(e * e, axis=1, keepdims=True), axis=0, keepdims=True) * (0.5 / D)
        s = jnp.broadcast_to(s, (1, LANE))

        @pl.when(pl.program_id(0) == 0)
        def _():
            l_ref[...] = s

        @pl.when(pl.program_id(0) > 0)
        def _():
            l_ref[...] += s

    row = pl.BlockSpec((tt, D), lambda i: (i, 0))
    return pl.pallas_call(
        body, name=name, grid=(T // tt,),
        in_specs=[row, row], out_specs=[row, pl.BlockSpec((1, LANE), lambda i: (0, 0))],
        out_shape=[jax.ShapeDtypeStruct((T, D), F32), jax.ShapeDtypeStruct((1, LANE), F32)],
        compiler_params=_params("arbitrary"),
    )(y, target)


def _down(x, k):
    return pltpu.roll(x, k, 0) if k else x


def _up(x, k):
    return pltpu.roll(x, x.shape[0] - k, 0) if k else x


def _sc_fwd(proj, conv_w, name):
    T = proj.shape[0]
    tt = _tile(T, WIDE_ROW_TILE, SUBLANE)
    B = SC_BLK

    def body(p_ref, ph_ref, w_ref, o_ref):
        keep = (pl.program_id(0) > 0).astype(F32)
        for j in range(SC_NBLK):
            cb, cc, cu, cg = (slice(k * SC_W + j * B, k * SC_W + (j + 1) * B) for k in range(4))
            cw = slice(j * B, (j + 1) * B)
            z = jnp.concatenate([ph_ref[:, cc] * ph_ref[:, cu] * keep, p_ref[:, cc] * p_ref[:, cu]], axis=0)
            cz = (w_ref[2:3, cw] * z + w_ref[1:2, cw] * _down(z, 1) + w_ref[0:1, cw] * _down(z, 2))[HALO:]
            gate = p_ref[:, cg]
            o_ref[:, cw] = (p_ref[:, cb] * cz * (gate * _sigmoid(gate))).astype(BF16)

    return pl.pallas_call(
        body, name=name, grid=(T // tt,),
        in_specs=[pl.BlockSpec((tt, 4 * SC_W), lambda i: (i, 0)),
                  pl.BlockSpec((HALO, 4 * SC_W), lambda i: (jnp.maximum(i * (tt // HALO) - 1, 0), 0)),
                  pl.BlockSpec((SC_CONV, SC_W), lambda i: (0, 0))],
        out_specs=pl.BlockSpec((tt, SC_W), lambda i: (i, 0)),
        out_shape=jax.ShapeDtypeStruct((T, SC_W), BF16),
        compiler_params=_params("parallel"),
    )(proj, proj, conv_w)


def _sc_bwd(dyg, proj, conv_w, name):
    T = proj.shape[0]
    tt = _tile(T, WIDE_ROW_TILE, SUBLANE)
    nt = T // tt
    B = SC_BLK
    hb = tt // HALO

    def body(d_ref, dn_ref, p_ref, pp_ref, pn_ref, w_ref, o_ref, dw_ref):
        i = pl.program_id(0)
        keep_p = (i > 0).astype(F32)
        keep_n = (i < nt - 1).astype(F32)
        main = slice(HALO, HALO + tt)
        parts = []
        for j in range(SC_NBLK):
            cw = slice(j * B, (j + 1) * B)

            def ext(k):
                s = slice(k * SC_W + j * B, k * SC_W + (j + 1) * B)
                return s, jnp.concatenate([pp_ref[:, s] * keep_p, p_ref[:, s], pn_ref[:, s]], axis=0)

            (sb, b), (sc, c), (su, u), (sg_, gate) = ext(0), ext(1), ext(2), ext(3)
            dyg_e = jnp.concatenate([jnp.zeros((HALO, B), F32), d_ref[:, cw], dn_ref[:, cw] * keep_n], axis=0)
            w0, w1, w2 = w_ref[0:1, cw], w_ref[1:2, cw], w_ref[2:3, cw]
            z = c * u
            z1, z2 = _down(z, 1), _down(z, 2)
            cz = w2 * z + w1 * z1 + w0 * z2
            sg, dsg = _silu_and_grad(gate)
            dy = dyg_e * sg
            dcz = dy * b
            dz = w2 * dcz + w1 * _up(dcz, 1) + w0 * _up(dcz, 2)
            o_ref[:, sb] = (dy * cz)[main].astype(BF16)
            o_ref[:, sc] = (dz * u)[main].astype(BF16)
            o_ref[:, su] = (dz * c)[main].astype(BF16)
            o_ref[:, sg_] = (dyg_e * (b * cz) * dsg)[main].astype(BF16)
            dcm = dcz[main]
            parts.append(jnp.concatenate([jnp.sum(dcm * z2[main], axis=0, keepdims=True),
                                          jnp.sum(dcm * z1[main], axis=0, keepdims=True),
                                          jnp.sum(dcm * z[main], axis=0, keepdims=True)], axis=0))
        part = jnp.concatenate(parts, axis=1)

        @pl.when(i == 0)
        def _():
            dw_ref[...] = part

        @pl.when(i > 0)
        def _():
            dw_ref[...] += part

    nxt = lambda i: (jnp.minimum((i + 1) * hb, nt * hb - 1), 0)
    return pl.pallas_call(
        body, name=name, grid=(nt,),
        in_specs=[pl.BlockSpec((tt, SC_W), lambda i: (i, 0)),
                  pl.BlockSpec((HALO, SC_W), nxt),
                  pl.BlockSpec((tt, 4 * SC_W), lambda i: (i, 0)),
                  pl.BlockSpec((HALO, 4 * SC_W), lambda i: (jnp.maximum(i * hb - 1, 0), 0)),
                  pl.BlockSpec((HALO, 4 * SC_W), nxt),
                  pl.BlockSpec((SC_CONV, SC_W), lambda i: (0, 0))],
        out_specs=[pl.BlockSpec((tt, 4 * SC_W), lambda i: (i, 0)), pl.BlockSpec((SC_CONV, SC_W), lambda i: (0, 0))],
        out_shape=[jax.ShapeDtypeStruct((T, 4 * SC_W), BF16), jax.ShapeDtypeStruct((SC_CONV, SC_W), F32)],
        compiler_params=_params("arbitrary"),
    )(dyg, dyg, proj, proj, proj, conv_w)


def _split3_dot(x, m):
    hi = x.astype(BF16)
    r1 = x - hi.astype(F32)
    mid = r1.astype(BF16)
    lo = (r1 - mid.astype(F32)).astype(BF16)
    return _dot(hi, m) + _dot(mid, m) + _dot(lo, m)


def _split2_dot(x, m):
    hi = x.astype(BF16)
    lo = (x - hi.astype(F32)).astype(BF16)
    return _dot(hi, m) + _dot(lo, m)


def _head_mean_matrix():
    r, c = _iota2((LANE, LANE), 0), _iota2((LANE, LANE), 1)
    return jnp.where((r // SB_DH) == (c // SB_DH), 1.0 / SB_DH, 0.0).astype(BF16)


def _sb_prep(proj, qg2, kg2, name):
    T = proj.shape[0]
    tt = _tile(T, WIDE_ROW_TILE, SUBLANE)

    def body(p_ref, qg_ref, kg_ref, q_ref, k_ref, v_ref):
        bd = _head_mean_matrix()

        def norm(x, g, scale):
            r = lax.rsqrt(_split3_dot(x * x, bd) + RMS_EPS)
            return (x * r * g * scale).astype(BF16)

        v_ref[...] = p_ref[:, 2 * SB_W:3 * SB_W].astype(BF16)
        for p in range(SB_PAIRS):
            cols = slice(p * LANE, (p + 1) * LANE)
            q_ref[:, cols] = norm(p_ref[:, cols], qg_ref[...], SB_DH ** -0.5)
            k_ref[:, cols] = norm(p_ref[:, SB_W + p * LANE:SB_W + (p + 1) * LANE], kg_ref[...], 1.0)

    blk = pl.BlockSpec((tt, SB_W), lambda i: (i, 0))
    vec = pl.BlockSpec((1, LANE), lambda i: (0, 0))
    return pl.pallas_call(
        body, name=name, grid=(T // tt,),
        in_specs=[pl.BlockSpec((tt, 4 * SB_W), lambda i: (i, 0)), vec, vec],
        out_specs=[blk, blk, blk],
        out_shape=[jax.ShapeDtypeStruct((T, SB_W), BF16)] * 3,
        compiler_params=_params("parallel"),
    )(proj, qg2, kg2)


def _sb_prep_bwd(proj, dqn, dkn, dv, dgate, qg2, kg2, name):
    T = proj.shape[0]
    tt = _tile(T, WIDE_ROW_TILE, SUBLANE)

    def body(p_ref, dq_ref, dk_ref, dv_ref, dg_ref, qg_ref, kg_ref, o_ref, dqg_ref, dkg_ref):
        i = pl.program_id(0)
        bd = _head_mean_matrix()

        def norm_bwd(x, g, dy):
            r = lax.rsqrt(_split3_dot(x * x, bd) + RMS_EPS)
            xh = x * r
            dxh = dy * g
            m = _split3_dot(dxh * xh, bd)
            return r * (dxh - xh * m), jnp.sum(dy * xh, axis=0, keepdims=True)

        o_ref[:, 2 * SB_W:3 * SB_W] = dv_ref[...].astype(BF16)
        o_ref[:, 3 * SB_W:4 * SB_W] = dg_ref[...].astype(BF16)
        pq = pk = jnp.zeros((1, LANE), F32)
        for p in range(SB_PAIRS):
            cols, kcols = slice(p * LANE, (p + 1) * LANE), slice(SB_W + p * LANE, SB_W + (p + 1) * LANE)
            dxq, sq = norm_bwd(p_ref[:, cols], qg_ref[...], dq_ref[:, cols])
            dxk, sk = norm_bwd(p_ref[:, kcols], kg_ref[...], dk_ref[:, cols])
            o_ref[:, cols] = dxq.astype(BF16)
            o_ref[:, kcols] = dxk.astype(BF16)
            pq, pk = pq + sq, pk + sk

        @pl.when(i == 0)
        def _():
            dqg_ref[...] = pq
            dkg_ref[...] = pk

        @pl.when(i > 0)
        def _():
            dqg_ref[...] += pq
            dkg_ref[...] += pk

    blk = pl.BlockSpec((tt, SB_W), lambda i: (i, 0))
    vec = pl.BlockSpec((1, LANE), lambda i: (0, 0))
    wide = pl.BlockSpec((tt, 4 * SB_W), lambda i: (i, 0))
    return pl.pallas_call(
        body, name=name, grid=(T // tt,),
        in_specs=[wide, blk, blk, blk, blk, vec, vec],
        out_specs=[wide, vec, vec],
        out_shape=[jax.ShapeDtypeStruct((T, 4 * SB_W), BF16)] + [jax.ShapeDtypeStruct((1, LANE), F32)] * 2,
        compiler_params=_params("arbitrary"),
    )(proj, dqn, dkn, dv, dgate, qg2, kg2)


def _fold_heads(part, name):
    def body(p_ref, o_ref):
        r, c = _iota2((LANE, SB_DH), 0), _iota2((LANE, SB_DH), 1)
        fold = jnp.where((r % SB_DH) == c, 1.0, 0.0).astype(F32)
        o_ref[...] = jnp.sum(_hdot(p_ref[...], fold), axis=0, keepdims=True)

    return pl.pallas_call(body, name=name, out_shape=jax.ShapeDtypeStruct((1, SB_DH), F32))(part)


def _sb_masks():
    lane = _iota2((1, LANE), 1)
    return lane < SB_DH


def _sb_attn_fwd(qn, kn, vb, proj, name, comm=None):
    T = qn.shape[0]
    tq, tk = _tile(T, SB_TQ, SUBLANE), SB_TK
    assert tq % tk == 0

    def body(q_ref, k_ref, v_ref, g_ref, o_ref, og_ref, lt_ref, done_ref):
        i = pl.program_id(1)
        ma = _sb_masks()
        q2 = q_ref[...]
        zero = jnp.zeros_like(q2)
        qs = (jnp.where(ma, q2, zero), jnp.where(ma, zero, q2))
        upper = (_iota2((tk, tk), 0) > _iota2((tk, tk), 1)).astype(BF16)
        qpos = i * tq + _iota2((tq, tk), 0)
        nb = tq // tk

        def trip(kb_top, masked, carry):
            acc, la, lb = carry
            chains = [(b, h) for b in range(nb) for h in range(2)]
            k2s, vss, masks = [], [], []
            for b in range(nb):
                kb = kb_top - b
                rows = pl.ds(pl.multiple_of(kb * tk, tk), tk)
                k2s.append(k_ref[rows, :])
                v2 = v_ref[rows, :]
                zv = jnp.zeros_like(v2)
                vss.append((jnp.where(ma, v2, zv), jnp.where(ma, zv, v2)))
                masks.append((kb * tk + _iota2((tq, tk), 1)) < qpos if masked else None)
            zs = [_dot(qs[h], k2s[b], NT) for b, h in chains]
            ts = [jnp.log(1.0 + jnp.exp(-jnp.abs(z))) for z in zs]
            ls = [-(jnp.maximum(z, 0.0) + t) for z, t in zip(zs, ts)]
            if masked:
                ls = [jnp.where(masks[b], l, 0.0) for (b, h), l in zip(chains, ls)]
            cums = [_split2_dot(l, upper) for l in ls]
            sums = [jnp.sum(l, axis=1, keepdims=True) for l in ls]
            offs, tot = {}, [la, lb]
            for b in range(nb):
                for h in range(2):
                    offs[(b, h)] = tot[h]
                    tot[h] = tot[h] + sums[chains.index((b, h))]
            ws = [jnp.exp(jnp.minimum(z, 0.0) - t + c + offs[ch]) for ch, z, t, c in zip(chains, zs, ts, cums)]
            if masked:
                ws = [jnp.where(masks[b], w, 0.0) for (b, h), w in zip(chains, ws)]
            for (b, h), w in zip(chains, ws):
                acc = acc + _dot(w.astype(BF16), vss[b][h])
            return acc, tot[0], tot[1]

        def largest(la, lb):
            return jnp.max(jnp.maximum(la, lb))

        z1 = jnp.zeros((tq, 1), F32)
        acc, la, lb = trip((i + 1) * nb - 1, True, (jnp.zeros((tq, LANE), F32), z1, z1))

        def live(c):
            return (c[0] < i) & (c[4] > SB_DEAD)

        def more(c):
            j, acc, la, lb, _ = c
            acc, la, lb = trip((i - j) * nb - 1, False, (acc, la, lb))
            return j + 1, acc, la, lb, largest(la, lb)

        done, acc, la, lb, _ = lax.while_loop(live, more, (jnp.int32(0), acc, la, lb, largest(la, lb)))
        gate = g_ref[...]
        o_ref[...] = acc
        og_ref[...] = (acc * (gate * _sigmoid(gate))).astype(BF16)
        lt_ref[...] = jnp.where(_iota2((tq, 2), 1) == 0, la, lb)
        done_ref[...] = jnp.full((SUBLANE, LANE), done, F32)

    nq = T // tq
    qblk = pl.BlockSpec((tq, LANE), lambda p, i: (i, p))
    full = pl.BlockSpec((T, LANE), lambda p, i: (0, p))
    return _call(
        body, comm, name=name, grid=(SB_PAIRS, nq),
        in_specs=[qblk, full, full, pl.BlockSpec((tq, LANE), lambda p, i: (i, 3 * SB_PAIRS + p))],
        out_specs=[qblk, qblk, pl.BlockSpec((None, tq, 2), lambda p, i: (p, i, 0)),
                   pl.BlockSpec((None, None, SUBLANE, LANE), lambda p, i: (p, i, 0, 0))],
        out_shape=[jax.ShapeDtypeStruct((T, SB_W), F32), jax.ShapeDtypeStruct((T, SB_W), BF16),
                   jax.ShapeDtypeStruct((SB_PAIRS, T, 2), F32), jax.ShapeDtypeStruct((SB_PAIRS, nq, SUBLANE, LANE), F32)],
        scratch_shapes=[], semantics=("parallel", "parallel"), args=(qn, kn, vb, proj))


def _sb_attn_bwd(qn, kn, vb, dog, o, ltot, done, proj, name, comm=None):
    T = qn.shape[0]
    tq, tk = _tile(T, SB_TQ, SUBLANE), SB_TK

    def body(q_ref, k_ref, v_ref, dog_ref, o_ref, lt_ref, done_ref, g_ref, dq_ref, dk_ref, dv_ref, dgate_ref):
        i = pl.program_id(1)
        first_trip = i - jnp.max(done_ref[...]).astype(jnp.int32)

        @pl.when(i == 0)
        def _():
            dk_ref[...] = jnp.zeros_like(dk_ref)
            dv_ref[...] = jnp.zeros_like(dv_ref)

        ma = _sb_masks()
        gate, o2, dog2 = g_ref[...], o_ref[...], dog_ref[...]
        sg, dsg = _silu_and_grad(gate)
        do2 = dog2 * sg
        dgate_ref[...] = dog2 * o2 * dsg
        lt = lt_ref[...]
        first = _iota2((tq, 2), 1) == 0
        ltots = (jnp.sum(jnp.where(first, lt, 0.0), axis=1, keepdims=True),
                 jnp.sum(jnp.where(first, 0.0, lt), axis=1, keepdims=True))
        q2 = q_ref[...]
        zq = jnp.zeros_like(q2)
        qs = (jnp.where(ma, q2, zq), jnp.where(ma, zq, q2))
        dob = do2.astype(BF16)
        dos = (jnp.where(ma, dob, zq), jnp.where(ma, zq, dob))
        upto = (_iota2((tk, tk), 0) <= _iota2((tk, tk), 1)).astype(BF16)
        before = (_iota2((tk, tk), 0) < _iota2((tk, tk), 1)).astype(BF16)
        qpos = i * tq + _iota2((tq, tk), 0)
        nb = tq // tk

        def trip(kb_bot, masked, carry):
            dq, la, lb, ea, eb = carry
            chains = [(b, h) for b in range(nb) for h in range(2)]
            rows, k2s, v2s, kss, masks = [], [], [], [], []
            for b in range(nb):
                kb = kb_bot + b
                rows.append(pl.ds(pl.multiple_of(kb * tk, tk), tk))
                k2 = k_ref[rows[b], :]
                zk = jnp.zeros_like(k2)
                k2s.append(k2)
                v2s.append(v_ref[rows[b], :])
                kss.append((jnp.where(ma, k2, zk), jnp.where(ma, zk, k2)))
                masks.append((kb * tk + _iota2((tq, tk), 1)) < qpos if masked else None)

            def keep(vals):
                return [jnp.where(masks[b], x, 0.0) for (b, h), x in zip(chains, vals)] if masked else vals

            zs = [_dot(qs[h], k2s[b], NT) for b, h in chains]
            dws = [_dot(dos[h], v2s[b], NT) for b, h in chains]
            ts = [jnp.log(1.0 + jnp.exp(-jnp.abs(z))) for z in zs]
            ls = keep([-(jnp.maximum(z, 0.0) + t) for z, t in zip(zs, ts)])
            lps = [jnp.minimum(z, 0.0) - t for z, t in zip(zs, ts)]
            cums = [_split3_dot(l, upto) for l in ls]
            lsums = [jnp.sum(l, axis=1, keepdims=True) for l in ls]
            offs, tot = {}, [la, lb]
            for b in range(nb):
                for h in range(2):
                    offs[(b, h)] = tot[h]
                    tot[h] = tot[h] + lsums[chains.index((b, h))]
            ws = keep([jnp.exp(lp + (ltots[h] - (offs[(b, h)] + c))) for (b, h), lp, c in zip(chains, lps, cums)])
            es = [dw * w for dw, w in zip(dws, ws)]
            ecums = [_split2_dot(e, before) for e in es]
            esums = [jnp.sum(e, axis=1, keepdims=True) for e in es]
            eoffs, etot = {}, [ea, eb]
            for b in range(nb):
                for h in range(2):
                    eoffs[(b, h)] = etot[h]
                    etot[h] = etot[h] + esums[chains.index((b, h))]
            dzs = keep([e - jnp.exp(lp) * (e + eoffs[ch] + ec) for ch, e, lp, ec in zip(chains, es, lps, ecums)])
            dzs = [dz.astype(BF16) for dz in dzs]
            wbs = [w.astype(BF16) for w in ws]
            for (b, h), dz in zip(chains, dzs):
                dq = dq + _dot(dz, kss[b][h])
            for b in range(nb):
                ia, ib = chains.index((b, 0)), chains.index((b, 1))
                dk_ref[rows[b], :] += _dot(dzs[ia], qs[0], TN) + _dot(dzs[ib], qs[1], TN)
                dv_ref[rows[b], :] += _dot(wbs[ia], dos[0], TN) + _dot(wbs[ib], dos[1], TN)
            return dq, tot[0], tot[1], etot[0], etot[1]

        z1 = jnp.zeros((tq, 1), F32)
        carry = lax.fori_loop(first_trip, i, lambda j, c: trip(j * nb, False, c),
                              (jnp.zeros((tq, LANE), F32), z1, z1, z1, z1))
        dq = trip(i * nb, True, carry)[0]
        dq_ref[...] = dq * (SB_DH ** -0.5)

    qblk = pl.BlockSpec((tq, LANE), lambda p, i: (i, p))
    full = pl.BlockSpec((T, LANE), lambda p, i: (0, p))
    return _call(
        body, comm, name=name, grid=(SB_PAIRS, T // tq),
        in_specs=[qblk, full, full, qblk, qblk, pl.BlockSpec((None, tq, 2), lambda p, i: (p, i, 0)),
                  pl.BlockSpec((None, None, SUBLANE, LANE), lambda p, i: (p, i, 0, 0)),
                  pl.BlockSpec((tq, LANE), lambda p, i: (i, 3 * SB_PAIRS + p))],
        out_specs=[qblk, full, full, qblk],
        out_shape=[jax.ShapeDtypeStruct((T, SB_W), F32)] * 4,
        scratch_shapes=[], semantics=("parallel", "arbitrary"), args=(qn, kn, vb, dog, o, ltot, done, proj))


def _dn_conv(ext, w_ref, cw):
    return (w_ref[3:4, cw] * ext + w_ref[2:3, cw] * _down(ext, 1) + w_ref[1:2, cw] * _down(ext, 2)
            + w_ref[0:1, cw] * _down(ext, 3))


def _dn_gates(a_in, b_in, a_log, dt_bias, name):
    T, H = a_in.shape
    C = DN_CHUNK

    def body(a_ref, b_ref, al_ref, dt_ref, g_ref, beta_ref):
        beta_ref[...] = _sigmoid(b_ref[...])
        g_ref[...] = -jnp.exp(al_ref[...]) * _softplus(a_ref[...] + dt_ref[...])
        tri = (_iota2((C, C), 0) >= _iota2((C, C), 1)).astype(F32)

        def chunk(n, carry):
            rows = pl.ds(pl.multiple_of(n * C, C), C)
            g_ref[rows, :] = _hdot(tri, g_ref[rows, :])
            return carry

        lax.fori_loop(0, T // C, chunk, 0)

    return pl.pallas_call(body, name=name, out_shape=[jax.ShapeDtypeStruct((T, H), F32)] * 2)(a_in, b_in, a_log, dt_bias)


def _dn_gates_bwd(dg, dbeta, a_in, b_in, a_log, dt_bias, name):
    T, H = a_in.shape
    C = DN_CHUNK

    def body(dg_ref, db_ref, a_ref, b_ref, al_ref, dt_ref, da_ref, dbi_ref, dal_ref, ddt_ref):
        tri_t = (_iota2((C, C), 0) <= _iota2((C, C), 1)).astype(F32)

        def chunk(n, carry):
            rows = pl.ds(pl.multiple_of(n * C, C), C)
            da_ref[rows, :] = _hdot(tri_t, dg_ref[rows, :])
            return carry

        lax.fori_loop(0, T // C, chunk, 0)
        dla = da_ref[...]
        x = a_ref[...] + dt_ref[...]
        ea = jnp.exp(al_ref[...])
        da = dla * (-ea) * _sigmoid(x)
        da_ref[...] = da
        dal_ref[...] = jnp.sum(dla * (-ea * _softplus(x)), axis=0, keepdims=True)
        ddt_ref[...] = jnp.sum(da, axis=0, keepdims=True)
        beta = _sigmoid(b_ref[...])
        dbi_ref[...] = db_ref[...] * beta * (1.0 - beta)

    return pl.pallas_call(
        body, name=name,
        out_shape=[jax.ShapeDtypeStruct((T, H), F32)] * 2 + [jax.ShapeDtypeStruct((1, H), F32)] * 2,
    )(dg, dbeta, a_in, b_in, a_log, dt_bias)


def _dn_chunk_terms(q, k, gc, bc):
    C = DN_CHUNK
    r, c = _iota2((C, C), 0), _iota2((C, C), 1)
    lower, strict, eye = r >= c, r > c, r == c
    grow = jnp.sum(jnp.where(eye, gc, 0.0), axis=0, keepdims=True)
    decay = jnp.where(lower, jnp.exp(jnp.where(lower, gc - grow, 0.0)), 0.0)
    last = _iota2((C, 1), 0) == C - 1
    gl = jnp.sum(jnp.where(last, gc, 0.0), axis=0, keepdims=True)
    eg = jnp.exp(gc)
    egl = jnp.exp(gl - gc)
    kb = k * bc
    lmat = jnp.where(strict, _bdot(kb, k, NT) * decay, 0.0)
    aqk = jnp.where(lower, _bdot(q, k, NT) * decay, 0.0)
    return dict(lower=lower, strict=strict, eye=eye, last=last, decay=decay, gl=gl, eg=eg, egl=egl, kb=kb,
                lmat=lmat, aqk=aqk, qd=q * eg, kd=k * egl)


def _split(x):
    hi = x.astype(BF16)
    return hi, (x - hi.astype(F32)).astype(BF16)


def _x3dot(a, b, dims=NN):
    ah, al = a if isinstance(a, tuple) else _split(a)
    bh, bl = b if isinstance(b, tuple) else _split(b)
    return _dot(ah, bh, dims) + (_dot(ah, bl, dims) + _dot(al, bh, dims))


def _interleave(gens):
    for _ in itertools.zip_longest(*gens):
        pass


def _unit_lower_inverse_steps(lmat, eye, out):
    ident = jnp.where(eye, 1.0, 0.0).astype(F32)
    m = -lmat
    inv = ident + m
    for _ in range(int(math.log2(DN_CHUNK)) - 1):
        ms = _split(m)
        m = _x3dot(ms, ms)
        yield
        inv = inv + _x3dot(inv, m)
        yield
    out["tm"] = inv


def _dn_chunk_fwd(pqkv, conv_w, g, beta, pgate, gn, name, comm=None):
    T = pqkv.shape[0]
    C, H = DN_CHUNK, DN_HEADS
    N = T // C
    B = DN_PREP_BLK
    nq, nqk = DN_QK_W // B, 2 * DN_QK_W // B

    def step(p_ref, cw_ref, g_ref, b_ref, pg_ref, gn_ref, act_out, o_ref, og_ref, s_out, t_out, vn_out, u_out, w_out,
             s_scr, tail_scr, a_ref, a_next):
        head_lane = _iota2((C, H), 1)

        def prepare(cb):
            cw = slice(cb * B, (cb + 1) * B)
            ext = jnp.concatenate([tail_scr[:, cw], p_ref[:, cw]], axis=0)
            c = _dn_conv(ext, cw_ref, cw)[HALO:]
            yield
            a = c * _sigmoid(c)
            if cb >= nqk:
                a_next[:, cw] = a
                act_out[:, cw] = a
                return
            scale = DN_DK ** -0.5 if cb < nq else 1.0
            for hh in range(B // DN_DK):
                yield
                ah = a[:, hh * DN_DK:(hh + 1) * DN_DK]
                val = ah * (lax.rsqrt(jnp.sum(ah * ah, axis=-1, keepdims=True) + L2_EPS) * scale)
                cols = slice(cb * B + hh * DN_DK, cb * B + (hh + 1) * DN_DK)
                a_next[:, cols] = val
                act_out[:, cols] = val

        def head(hh):
            qs, vs = slice(hh * DN_DK, (hh + 1) * DN_DK), slice(hh * DN_DV, (hh + 1) * DN_DV)
            q, k, v = a_ref[:, qs], a_ref[:, DN_QK_W + hh * DN_DK:DN_QK_W + (hh + 1) * DN_DK], \
                a_ref[:, 2 * DN_QK_W + hh * DN_DV:2 * DN_QK_W + (hh + 1) * DN_DV]
            gc = jnp.sum(jnp.where(head_lane == hh, g_ref[...], 0.0), axis=1, keepdims=True)
            bc = jnp.sum(jnp.where(head_lane == hh, b_ref[...], 0.0), axis=1, keepdims=True)
            t = _dn_chunk_terms(q, k, gc, bc)
            yield
            res = {}
            yield from _unit_lower_inverse_steps(t["lmat"], t["eye"], res)
            tms = _split(res["tm"])
            u = _x3dot(tms, v * bc)
            yield
            w = _x3dot(tms, t["kb"] * t["eg"])
            yield
            s = s_scr[hh]
            s_out[hh] = s
            t_out[hh] = res["tm"]
            sb = s.astype(BF16)
            vn = u - _dot(w.astype(BF16), sb)
            yield
            o = _dot(t["qd"].astype(BF16), sb) + _bdot(t["aqk"], vn)
            yield
            s_scr[hh] = s * jnp.exp(t["gl"]) + _bdot(t["kd"], vn, TN)
            vn_out[:, vs] = vn
            u_out[:, vs] = u
            w_out[:, qs] = w
            o_ref[:, vs] = o
            gate = pg_ref[:, vs]
            r = lax.rsqrt(jnp.mean(o * o, axis=-1, keepdims=True) + RMS_EPS)
            og_ref[:, vs] = (o * r * gn_ref[...] * (gate * _sigmoid(gate))).astype(BF16)

        _interleave([prepare(cb) for cb in range(DN_CONV_W // B)] + [head(hh) for hh in range(H)])

        @pl.when(pl.program_id(0) < N - 1)
        def _():
            tail_scr[...] = p_ref[C - HALO:C, :]

    def body(*refs):
        s = pl.program_id(0)
        io, (s_scr, tail_scr, buf_a, buf_b) = refs[:-4], refs[-4:]

        @pl.when(s == 0)
        def _():
            tail_scr[...] = jnp.zeros_like(tail_scr)
            buf_b[...] = jnp.zeros_like(buf_b)

        @pl.when(s <= 1)
        def _():
            s_scr[...] = jnp.zeros_like(s_scr)

        @pl.when(s % 2 == 0)
        def _():
            step(*io, s_scr, tail_scr, buf_b, buf_a)

        @pl.when(s % 2 == 1)
        def _():
            step(*io, s_scr, tail_scr, buf_a, buf_b)

    nxt = lambda w: pl.BlockSpec((C, w), lambda s: (jnp.minimum(s, N - 1), 0))
    cur = lambda w: pl.BlockSpec((C, w), lambda s: (jnp.maximum(s - 1, 0), 0))
    per_chunk = lambda a, b: pl.BlockSpec((H, None, a, b), lambda s: (0, jnp.maximum(s - 1, 0), 0, 0))
    return _call(
        body, comm, name=name, grid=(N + 1,),
        in_specs=[nxt(DN_CONV_W), pl.BlockSpec((DN_CONV, DN_CONV_W), lambda s: (0, 0)), cur(H), cur(H), cur(DN_V_W),
                  pl.BlockSpec((1, DN_DV), lambda s: (0, 0))],
        out_specs=[nxt(DN_CONV_W), cur(DN_V_W), cur(DN_V_W), per_chunk(DN_DK, DN_DV), per_chunk(C, C),
                   cur(DN_V_W), cur(DN_V_W), cur(DN_QK_W)],
        out_shape=[jax.ShapeDtypeStruct((T, DN_CONV_W), F32),
                   jax.ShapeDtypeStruct((T, DN_V_W), F32), jax.ShapeDtypeStruct((T, DN_V_W), BF16),
                   jax.ShapeDtypeStruct((H, N, DN_DK, DN_DV), F32),
                   jax.ShapeDtypeStruct((H, N, C, C), F32),
                   jax.ShapeDtypeStruct((T, DN_V_W), F32),
                   jax.ShapeDtypeStruct((T, DN_V_W), F32),
                   jax.ShapeDtypeStruct((T, DN_QK_W), F32)],
        scratch_shapes=[pltpu.VMEM((H, DN_DK, DN_DV), F32), pltpu.VMEM((HALO, DN_CONV_W), F32),
                        pltpu.VMEM((C, DN_CONV_W), F32), pltpu.VMEM((C, DN_CONV_W), F32)],
        semantics=("arbitrary",), args=(pqkv, conv_w, g, beta, pgate, gn))


def _dn_chunk_bwd(pqkv, conv_w, act, g, beta, s_saved, tm_saved, vn_saved, u_saved, w_saved, dog, o_raw, pgate, gn,
                  name, comm=None):
    T = act.shape[0]
    C, H = DN_CHUNK, DN_HEADS
    N = T // C
    assert N % 2 == 0
    B = DN_PREP_BLK
    nq, nqk = DN_QK_W // B, 2 * DN_QK_W // B
    main = slice(HALO, HALO + C)

    def prepare_bwd(cb, p_ref, pp_ref, pn_ref, cw_ref, dread, dnext_scr, dp_ref, conv_parts):
        s = pl.program_id(0)
        keep_p = (N - s > 0).astype(F32)
        keep_n = (s > 1).astype(F32)
        cw = slice(cb * B, (cb + 1) * B)
        ext = jnp.concatenate([pp_ref[:, cw] * keep_p, p_ref[:, cw], pn_ref[:, cw]], axis=0)
        c = _dn_conv(ext, cw_ref, cw)
        yield
        sg = _sigmoid(c)
        da_dc = sg * (1.0 + c * (1.0 - sg))
        d_up = jnp.concatenate([jnp.zeros((HALO, B), F32), dread[:, cw], dnext_scr[:, cw] * keep_n], axis=0)
        if cb < nqk:
            a = c * sg
            scale = DN_DK ** -0.5 if cb < nq else 1.0
            normed = []
            for hh in range(B // DN_DK):
                yield
                cols = slice(hh * DN_DK, (hh + 1) * DN_DK)
                ah = a[:, cols]
                r = lax.rsqrt(jnp.sum(ah * ah, axis=-1, keepdims=True) + L2_EPS)
                y = ah * r
                dy = d_up[:, cols] * scale
                normed.append(r * (dy - y * jnp.sum(dy * y, axis=-1, keepdims=True)))
            d_up = jnp.concatenate(normed, axis=1)
        yield
        dc = d_up * da_dc
        dp = (cw_ref[3:4, cw] * dc + cw_ref[2:3, cw] * _up(dc, 1) + cw_ref[1:2, cw] * _up(dc, 2)
              + cw_ref[0:1, cw] * _up(dc, 3))
        dp_ref[:, cw] = dp[main].astype(BF16)
        yield
        dcm = dc[main]
        conv_parts[cb] = jnp.concatenate([jnp.sum(dcm * _down(ext, 3 - k)[main], axis=0, keepdims=True)
                                          for k in range(DN_CONV)], axis=0)

    def finish_prepare(conv_parts, dconv_ref, dread, dnext_scr):
        part = jnp.concatenate([conv_parts[cb] for cb in range(DN_CONV_W // B)], axis=1)

        @pl.when(pl.program_id(0) == 0)
        def _():
            dconv_ref[...] = part

        @pl.when(pl.program_id(0) > 0)
        def _():
            dconv_ref[...] += part

        dnext_scr[...] = dread[0:HALO, :]

    def step(a_ref, g_ref, b_ref, s_ref, t_ref, vn_ref, u_ref, w_ref, dog_ref, o_ref, pg_ref, gn_ref,
             p_ref, pp_ref, pn_ref, cw_ref, dp_ref, dconv_ref, dg_ref, db_ref, dgate_ref, dgn_ref,
             ds_scr, dnext_scr, dwrite, dread):
        head_lane = _iota2((C, H), 1)
        dg_cols, db_cols, dgn_parts, conv_parts = {}, {}, {}, {}

        def output_gate_bwd(hh, vs):
            d, o, gate, gn_v = dog_ref[:, vs], o_ref[:, vs], pg_ref[:, vs], gn_ref[...]
            sg, dsg = _silu_and_grad(gate)
            r = lax.rsqrt(jnp.mean(o * o, axis=-1, keepdims=True) + RMS_EPS)
            n = o * r
            dy = d * sg
            dgate_ref[:, vs] = (d * (n * gn_v) * dsg).astype(BF16)
            dn = dy * gn_v
            dgn_parts[hh] = jnp.sum(dy * n, axis=0, keepdims=True)
            return r * (dn - n * jnp.mean(dn * n, axis=-1, keepdims=True))

        def head(hh):
            qs, vs = slice(hh * DN_DK, (hh + 1) * DN_DK), slice(hh * DN_DV, (hh + 1) * DN_DV)
            ks = slice(DN_QK_W + hh * DN_DK, DN_QK_W + (hh + 1) * DN_DK)
            vas = slice(2 * DN_QK_W + hh * DN_DV, 2 * DN_QK_W + (hh + 1) * DN_DV)
            q, k, v = a_ref[:, qs], a_ref[:, ks], a_ref[:, vas]
            gc = jnp.sum(jnp.where(head_lane == hh, g_ref[...], 0.0), axis=1, keepdims=True)
            bc = jnp.sum(jnp.where(head_lane == hh, b_ref[...], 0.0), axis=1, keepdims=True)
            t = _dn_chunk_terms(q, k, gc, bc)
            yield
            lower, strict, eye = t["lower"], t["strict"], t["eye"]
            decay, eg, egl, kb, qd, kd = t["decay"], t["eg"], t["egl"], t["kb"], t["qd"], t["kd"]
            s, tm, vn, u, w = s_ref[hh], t_ref[hh], vn_ref[:, vs], u_ref[:, vs], w_ref[:, qs]
            d_o = output_gate_bwd(hh, vs)
            ds_next = ds_scr[hh]
            egl_tot = jnp.exp(t["gl"])
            dob, sb, dsb, vnb = d_o.astype(BF16), s.astype(BF16), ds_next.astype(BF16), vn.astype(BF16)

            dvn = _bdot(t["aqk"], dob, TN) + _bdot(kd, dsb)
            yield
            daqk = jnp.where(lower, _dot(dob, vnb, NT), 0.0)
            dqd = _dot(dob, sb, NT)
            dkd = _dot(vnb, dsb, NT)
            yield
            dvnb = dvn.astype(BF16)
            ds_scr[hh] = _bdot(qd, dob, TN) + egl_tot * ds_next - _bdot(w, dvnb, TN)
            dgl = egl_tot * jnp.sum(jnp.sum(s * ds_next, axis=1, keepdims=True), axis=0, keepdims=True)
            dw = -_dot(dvnb, sb, NT)
            yield
            tms = _split(tm)
            dru = _x3dot(tms, dvn, TN)
            drw = _x3dot(tms, dw, TN)
            yield
            dl = -jnp.where(strict, _x3dot(dru, u, NT) + _x3dot(drw, w, NT), 0.0)
            yield
            dkk = (dl * decay).astype(BF16)
            dqk = (daqk * decay).astype(BF16)
            dkb = _bdot(dkk, k) + drw * eg
            yield
            dwrite[:, ks] = _bdot(dkk, kb, TN) + _bdot(dqk, q, TN) + dkd * egl + dkb * bc
            dwrite[:, qs] = _bdot(dqk, k) + dqd * eg
            dwrite[:, vas] = dru * bc
            yield
            db_cols[hh] = jnp.sum(dru * v, axis=1, keepdims=True) + jnp.sum(dkb * k, axis=1, keepdims=True)
            pm = dl * t["lmat"] + daqk * t["aqk"]
            col_as_col = jnp.sum(jnp.where(eye, jnp.sum(pm, axis=0, keepdims=True), 0.0), axis=1, keepdims=True)
            kdsum = jnp.sum(dkd * kd, axis=1, keepdims=True)
            dgc = (jnp.sum(pm, axis=1, keepdims=True) - col_as_col + jnp.sum(dqd * qd, axis=1, keepdims=True)
                   - kdsum + jnp.sum(drw * (kb * eg), axis=1, keepdims=True))
            dgl = dgl + jnp.sum(kdsum, axis=0, keepdims=True)
            dg_cols[hh] = dgc + jnp.where(t["last"], dgl, 0.0)

        _interleave([head(hh) for hh in range(H)]
                    + [prepare_bwd(cb, p_ref, pp_ref, pn_ref, cw_ref, dread, dnext_scr, dp_ref, conv_parts)
                       for cb in range(DN_CONV_W // B)])
        dg_ref[...] = sum(jnp.where(head_lane == hh, dg_cols[hh], 0.0) for hh in range(H))
        db_ref[...] = sum(jnp.where(head_lane == hh, db_cols[hh], 0.0) for hh in range(H))
        dgn_part = sum(dgn_parts[hh] for hh in range(H))

        @pl.when(pl.program_id(0) == 0)
        def _():
            dgn_ref[...] = dgn_part

        @pl.when(pl.program_id(0) > 0)
        def _():
            dgn_ref[...] += dgn_part

        finish_prepare(conv_parts, dconv_ref, dread, dnext_scr)

    def body(*refs):
        s = pl.program_id(0)
        io, (ds_scr, dnext_scr, buf_a, buf_b) = refs[:-4], refs[-4:]
        p_ref, pp_ref, pn_ref, cw_ref, dp_ref, dconv_ref = refs[12:18]

        @pl.when(s == 0)
        def _():
            ds_scr[...] = jnp.zeros_like(ds_scr)
            dnext_scr[...] = jnp.zeros_like(dnext_scr)
            buf_b[...] = jnp.zeros_like(buf_b)

        @pl.when((s < N) & (s % 2 == 0))
        def _():
            step(*io, ds_scr, dnext_scr, buf_a, buf_b)

        @pl.when((s < N) & (s % 2 == 1))
        def _():
            step(*io, ds_scr, dnext_scr, buf_b, buf_a)

        @pl.when(s == N)
        def _():
            conv_parts = {}
            _interleave([prepare_bwd(cb, p_ref, pp_ref, pn_ref, cw_ref, buf_b, dnext_scr, dp_ref, conv_parts)
                         for cb in range(DN_CONV_W // B)])
            finish_prepare(conv_parts, dconv_ref, buf_b, dnext_scr)

    cc = lambda s: jnp.maximum(N - 1 - s, 0)
    pc = lambda s: jnp.clip(N - s, 0, N - 1)
    row = lambda w: pl.BlockSpec((C, w), lambda s: (cc(s), 0))
    per_chunk = lambda a, b: pl.BlockSpec((H, None, a, b), lambda s: (0, cc(s), 0, 0))
    vec = pl.BlockSpec((1, DN_DV), lambda s: (0, 0))
    per_c = C // HALO
    conv_spec = pl.BlockSpec((DN_CONV, DN_CONV_W), lambda s: (0, 0))
    return _call(
        body, comm, name=name, grid=(N + 1,),
        in_specs=[row(DN_CONV_W), row(H), row(H), per_chunk(DN_DK, DN_DV), per_chunk(C, C),
                  row(DN_V_W), row(DN_V_W), row(DN_QK_W), row(DN_V_W), row(DN_V_W), row(DN_V_W), vec,
                  pl.BlockSpec((C, DN_CONV_W), lambda s: (pc(s), 0)),
                  pl.BlockSpec((HALO, DN_CONV_W), lambda s: (jnp.maximum(pc(s) * per_c - 1, 0), 0)),
                  pl.BlockSpec((HALO, DN_CONV_W), lambda s: (jnp.minimum((pc(s) + 1) * per_c, N * per_c - 1), 0)),
                  conv_spec],
        out_specs=[pl.BlockSpec((C, DN_CONV_W), lambda s: (pc(s), 0)), conv_spec, row(H), row(H), row(DN_V_W), vec],
        out_shape=[jax.ShapeDtypeStruct((T, DN_CONV_W), BF16), jax.ShapeDtypeStruct((DN_CONV, DN_CONV_W), F32),
                   jax.ShapeDtypeStruct((T, H), F32), jax.ShapeDtypeStruct((T, H), F32),
                   jax.ShapeDtypeStruct((T, DN_V_W), BF16), jax.ShapeDtypeStruct((1, DN_DV), F32)],
        scratch_shapes=[pltpu.VMEM((H, DN_DK, DN_DV), F32), pltpu.VMEM((HALO, DN_CONV_W), F32),
                        pltpu.VMEM((C, DN_CONV_W), F32), pltpu.VMEM((C, DN_CONV_W), F32)],
        semantics=("arbitrary",),
        args=(act, g, beta, s_saved, tm_saved, vn_saved, u_saved, w_saved, dog, o_raw, pgate, gn,
              pqkv, pqkv, pqkv, conv_w))


def _dn_split_w_in(w):
    return w, jnp.pad(w[:, DN_CONV_W + DN_V_W:], ((0, 0), (0, DN_AB_PAD - 2 * DN_HEADS)))


def _out_proj(og, w_out, x_res, next_g, name):
    if next_g is None:
        return _matmul(og, w_out, "nn", name, add=x_res), None
    return tuple(_matmul(og, w_out, "nn", name, add=x_res, norm_fwd=next_g, tm=NORM_FUSED_TM))


def _dn_layer_fwd(h, wts, conv_w, a_log, dt_bias, gn, w_out, x_res, tag, comm=None, next_g=None):
    w_in, wab = wts
    H = DN_HEADS
    pqkv = _matmul(h, w_in, "nn", tag + "_pqkv", b_cols=(0, DN_CONV_W))
    pgate = _matmul(h, w_in, "nn", tag + "_pgate", b_cols=(DN_CONV_W, DN_V_W))
    pab = _matmul(h, wab, "nn", tag + "_pab")
    a_in, b_in = pab[:, :H], pab[:, H:2 * H]
    g, beta = _dn_gates(a_in, b_in, a_log, dt_bias, tag + "_gates")
    (act, o_raw, og, s_sv, tm_sv, vn_sv, u_sv, w_sv), landed = _dn_chunk_fwd(pqkv, conv_w, g, beta, pgate, gn,
                                                                             tag + "_chunk_fwd", comm)
    if callable(w_out):
        w_out = w_out(landed)
    y = _out_proj(og, w_out, x_res, next_g, tag + "_out")
    saved = dict(h=h, wts=wts, conv_w=conv_w, a_log=a_log, dt_bias=dt_bias, gn=gn, w_out=w_out, pqkv=pqkv, pgate=pgate,
                 a_in=a_in, b_in=b_in, g=g, beta=beta, act=act, o_raw=o_raw, chunk=(s_sv, tm_sv, vn_sv, u_sv, w_sv), og=og)
    return y, saved, landed


def _dn_layer_bwd(dout, sv, tag, norm, comm_of=None, late_comm_of=None):
    w_in, wab = sv["wts"]
    h = sv["h"]
    dog = _matmul(dout, sv["w_out"], "nt", tag + "_dog")
    dw_out = _matmul(sv["og"], dout, "tn", tag + "_dwout", out_dtype=BF16)
    comm = comm_of(dw_out) if comm_of is not None else None
    (dpqkv, dconv, dg, dbeta, dgate, dgn), landed = _dn_chunk_bwd(
        sv["pqkv"], sv["conv_w"], sv["act"], sv["g"], sv["beta"], *sv["chunk"], dog, sv["o_raw"], sv["pgate"], sv["gn"],
        tag + "_chunk_bwd", comm)
    da_in, db_in, da_log, ddt = _dn_gates_bwd(dg, dbeta, sv["a_in"], sv["b_in"], sv["a_log"], sv["dt_bias"],
                                              tag + "_gates_bwd")
    dpab = jnp.pad(jnp.concatenate([da_in, db_in], axis=1), ((0, 0), (0, DN_AB_PAD - 2 * DN_HEADS)))
    dwqkv = _matmul(h, dpqkv, "tn", tag + "_dwqkv", out_dtype=BF16)
    dwgate = _matmul(h, dgate, "tn", tag + "_dwgate", out_dtype=BF16)
    dwab = _matmul(h, dpab, "tn", tag + "_dwab", out_dtype=BF16)
    dw_in = jnp.concatenate([dwqkv, dwgate, dwab[:, :2 * DN_HEADS]], axis=1)
    grads = dict(dn_w_in=dw_in, dn_conv_w=dconv, dn_a_log=da_log, dn_dt_bias=ddt, dn_o_norm_g=dgn, dn_w_out=dw_out)
    dx, landed_late = _matmul_nt_sum([(dpqkv, w_in, 0), (dgate, w_in, DN_CONV_W), (dpab, wab, 0)], tag + "_dh",
                                     late_comm_of(grads) if late_comm_of is not None else None, norm_bwd=norm,
                                     tm=NORM_FUSED_TM if norm is not None else 1024)
    return dx, grads, landed, landed_late


def _sb_layer_fwd(h, w_in, qg, kg, w_out, x_res, tag, comm=None, next_g=None):
    qg2, kg2 = jnp.tile(qg, (1, 2)), jnp.tile(kg, (1, 2))
    proj = _matmul(h, w_in, "nn", tag + "_proj", blocked_b=True)
    qn, kn, vb = _sb_prep(proj, qg2, kg2, tag + "_prep")
    (o, og, ltot, done), landed = _sb_attn_fwd(qn, kn, vb, proj, tag + "_attn_fwd", comm)
    y = _out_proj(og, w_out, x_res, next_g, tag + "_out")
    saved = dict(h=h, w_in=w_in, qg2=qg2, kg2=kg2, w_out=w_out, proj=proj, qn=qn, kn=kn, vb=vb, o=o, og=og, ltot=ltot,
                 done=done)
    return y, saved, landed


def _sb_layer_bwd(dout, sv, tag, comm=None):
    dog = _matmul(dout, sv["w_out"], "nt", tag + "_dog")
    dw_out = _matmul(sv["og"], dout, "tn", tag + "_dwout", out_dtype=BF16)
    (dqn, dkn, dv, dgate), landed = _sb_attn_bwd(sv["qn"], sv["kn"], sv["vb"], dog, sv["o"], sv["ltot"], sv["done"],
                                                 sv["proj"], tag + "_attn_bwd", comm)
    dproj, dqgp, dkgp = _sb_prep_bwd(sv["proj"], dqn, dkn, dv, dgate, sv["qg2"], sv["kg2"], tag + "_prep_bwd")
    dw_in = _matmul(sv["h"], dproj, "tn", tag + "_dwin", out_dtype=BF16, blocked_out=N_DEV)
    dh = _matmul(dproj, sv["w_in"], "nt", tag + "_dh", blocked_b=True)
    dqg = _fold_heads(dqgp, tag + "_dqg")
    dkg = _fold_heads(dkgp, tag + "_dkg")
    return dh, dict(sb_w_in=dw_in, sb_q_norm_g=dqg, sb_k_norm_g=dkg, sb_w_out=dw_out), landed


def _sc_layer_fwd(h, w_in, conv_w, w_out, x_res, tag, next_g=None):
    proj = _matmul(h, w_in, "nn", tag + "_proj", blocked_b=True)
    yg = _sc_fwd(proj, conv_w, tag + "_fwd")
    y = _out_proj(yg, w_out, x_res, next_g, tag + "_out")
    return y, dict(h=h, w_in=w_in, conv_w=conv_w, w_out=w_out, proj=proj, yg=yg)


def _sc_layer_bwd(dout, sv, tag):
    dyg = _matmul(dout, sv["w_out"], "nt", tag + "_dyg")
    dw_out = _matmul(sv["yg"], dout, "tn", tag + "_dwout", out_dtype=BF16)
    dproj, dconv = _sc_bwd(dyg, sv["proj"], sv["conv_w"], tag + "_bwd")
    dw_in = _matmul(sv["h"], dproj, "tn", tag + "_dwin", out_dtype=BF16, blocked_out=N_DEV)
    dh = _matmul(dproj, sv["w_in"], "nt", tag + "_dh", blocked_b=True)
    return dh, dict(sc_w_in=dw_in, sc_conv_w=dconv, sc_w_out=dw_out)


def _adamw(w, m, v, parts, name):
    L, R, C = w.shape
    tr = _tile(R, 128, SUBLANE)

    def body(*refs):
        w_ref, m_ref, v_ref = refs[:3]
        g_ref, d_ref, nm_ref, nv_ref = refs[3 + L:]

        def update(p_ref):
            g = p_ref[0].astype(F32)
            for s in range(1, N_DEV):
                g = g + p_ref[s].astype(F32)
            m2 = ADAM_B1 * m_ref[...] + (1.0 - ADAM_B1) * g
            v2 = ADAM_B2 * v_ref[...] + (1.0 - ADAM_B2) * (g * g)
            m_hat = m2 / (1.0 - ADAM_B1 ** ADAM_STEP)
            v_hat = v2 / (1.0 - ADAM_B2 ** ADAM_STEP)
            g_ref[...] = g
            d_ref[...] = -ADAM_LR * (m_hat / (jnp.sqrt(v_hat) + ADAM_EPS) + ADAM_WD * w_ref[...])
            nm_ref[...] = m2
            nv_ref[...] = v2

        for layer in range(L):
            pl.when(pl.program_id(0) == layer)(functools.partial(update, refs[3 + layer]))

    blk = pl.BlockSpec((None, tr, C), lambda l, i: (l, i, 0))
    landing = pl.BlockSpec((N_DEV, tr, C), lambda l, i: (0, i, 0))
    return pl.pallas_call(
        body, name=name, grid=(L, R // tr),
        in_specs=[blk, blk, blk] + [landing] * L,
        out_specs=[blk] * 4, out_shape=[jax.ShapeDtypeStruct((L, R, C), F32)] * 4,
        compiler_params=_params("parallel", "parallel"),
    )(w, m, v, *parts)


_HBM = pl.BlockSpec(memory_space=pltpu.HBM)
_MESH = pl.DeviceIdType.MESH


def _slot(x, y, c):
    return 4 * x + 2 * y + c


class _Gather:
    def __init__(self, shards):
        self.arrays = list(shards)
        n = len(self.arrays)
        self.out_shapes = [jax.ShapeDtypeStruct((N_DEV,) + s.shape, s.dtype) for s in self.arrays]
        self.scratch = [pltpu.SemaphoreType.DMA((n, N_DEV - 1)), pltpu.SemaphoreType.DMA((n, N_DEV - 1)),
                        pltpu.SemaphoreType.DMA((n,))]

    def _parts(self, ins, outs, sems):
        send_sems, recv_sems, local_sems = sems
        n = len(self.arrays)
        x, y, c = lax.axis_index("x"), lax.axis_index("y"), lax.axis_index("c")
        me, sibling = (x, y, c), (x, y, 1 - c)
        chips = [(1 - x, y), (x, 1 - y), (1 - x, 1 - y)]

        def copy(a, k, block, to, src=None):
            dst = outs[a].at[_slot(*block)]
            return pltpu.make_async_remote_copy(src_ref=dst if src is None else src, dst_ref=dst,
                                                send_sem=send_sems.at[a, k], recv_sem=recv_sems.at[a, k],
                                                device_id=to, device_id_type=_MESH)

        mine = [pltpu.make_async_copy(ins[a], outs[a].at[_slot(*me)], local_sems.at[a]) for a in range(n)]
        first = []
        for a in range(n):
            first.append(copy(a, 0, me, sibling, src=ins[a]))
            first += [copy(a, 1 + j, me, (*chip, c), src=ins[a]) for j, chip in enumerate(chips)]
        return n, c, me, sibling, chips, copy, mine, first

    def start(self, ins, outs, sems):
        _, _, _, _, _, _, mine, first = self._parts(ins, outs, sems)
        for cp in mine + first:
            cp.start()

    def finish(self, ins, outs, sems):
        n, c, me, sibling, chips, copy, mine, first = self._parts(ins, outs, sems)
        passed = []
        for j, chip in enumerate(chips):
            for a in range(n):
                copy(a, 1 + j, (*chip, c), me).wait_recv()
                fwd = copy(a, 4 + j, (*chip, c), sibling)
                fwd.start()
                passed.append(fwd)
        for a in range(n):
            copy(a, 0, sibling, me).wait_recv()
            for j, chip in enumerate(chips):
                copy(a, 4 + j, (*chip, 1 - c), me).wait_recv()
        for cp in first + passed:
            cp.wait_send()
        for cp in mine:
            cp.wait()


class _Exchange:
    def __init__(self, arrays, scatter):
        self.arrays, self.scatter = list(arrays), list(scatter)
        n = len(self.arrays)
        shapes = [a.shape[1:] if s else a.shape for a, s in zip(self.arrays, self.scatter)]
        self.out_shapes = [jax.ShapeDtypeStruct((N_DEV,) + tuple(s), a.dtype) for s, a in zip(shapes, self.arrays)]
        self.scratch = [pltpu.SemaphoreType.DMA((n, N_DEV - 1)), pltpu.SemaphoreType.DMA((n, N_DEV - 1)),
                        pltpu.SemaphoreType.DMA((n,))]

    def _copies(self, ins, outs, sems):
        send_sems, recv_sems, local_sems = sems
        n, scatter = len(self.arrays), self.scatter
        x, y, c = lax.axis_index("x"), lax.axis_index("y"), lax.axis_index("c")
        me = _slot(x, y, c)
        copies = [pltpu.make_async_copy(ins[a].at[me] if scatter[a] else ins[a], outs[a].at[me], local_sems.at[a])
                  for a in range(n)]
        for r in range(1, N_DEV):
            px = 1 - x if r & 4 else x
            py = 1 - y if r & 2 else y
            pc = 1 - c if r & 1 else c
            for a in range(n):
                copies.append(pltpu.make_async_remote_copy(
                    src_ref=ins[a].at[_slot(px, py, pc)] if scatter[a] else ins[a], dst_ref=outs[a].at[me],
                    send_sem=send_sems.at[a, r - 1], recv_sem=recv_sems.at[a, r - 1],
                    device_id=(px, py, pc), device_id_type=_MESH))
        return copies

    def start(self, ins, outs, sems):
        for cp in self._copies(ins, outs, sems):
            cp.start()

    def finish(self, ins, outs, sems):
        for cp in self._copies(ins, outs, sems):
            cp.wait()


def _comm_call(comm, name):
    n = len(comm.arrays)

    def body(*refs):
        ins, outs, sems = refs[:n], refs[n:2 * n], refs[2 * n:]
        comm.start(ins, outs, sems)
        comm.finish(ins, outs, sems)

    return pl.pallas_call(body, name=name, in_specs=[_HBM] * n, out_specs=[_HBM] * n, out_shape=comm.out_shapes,
                          scratch_shapes=comm.scratch)(*comm.arrays)


def _call(body, comm, *, name, grid, in_specs, out_specs, out_shape, scratch_shapes, semantics, args):
    if comm is None:
        outs = pl.pallas_call(body, name=name, grid=grid, in_specs=in_specs, out_specs=out_specs, out_shape=out_shape,
                              scratch_shapes=scratch_shapes, compiler_params=_params(*semantics))(*args)
        return outs, []
    n_in, n_out, n_scr, n_c = len(in_specs), len(out_specs), len(scratch_shapes), len(comm.arrays)

    def fused(*refs):
        ins, refs = refs[:n_in], refs[n_in:]
        c_ins, refs = refs[:n_c], refs[n_c:]
        outs, refs = refs[:n_out], refs[n_out:]
        c_outs, refs = refs[:n_c], refs[n_c:]
        scr, sems = refs[:n_scr], refs[n_scr:]
        ids = [pl.program_id(d) for d in range(len(grid))]
        first = functools.reduce(jnp.logical_and, [i == 0 for i in ids])
        last = functools.reduce(jnp.logical_and, [i == g - 1 for i, g in zip(ids, grid)])

        @pl.when(first)
        def _():
            comm.start(c_ins, c_outs, sems)

        body(*ins, *outs, *scr)

        @pl.when(last)
        def _():
            comm.finish(c_ins, c_outs, sems)

    outs = pl.pallas_call(
        fused, name=name, grid=grid, in_specs=list(in_specs) + [_HBM] * n_c, out_specs=list(out_specs) + [_HBM] * n_c,
        out_shape=list(out_shape) + comm.out_shapes, scratch_shapes=list(scratch_shapes) + comm.scratch,
        compiler_params=_params(*["arbitrary"] * len(grid)))(*args, *comm.arrays)
    return outs[:n_out], outs[n_out:]


_GATHER_0 = (("dn_w_in", 0), ("dn_conv_w", 0), ("dn_o_norm_g", 0))
_GATHER_1 = (("dn_w_out", 0), ("sb_w_in", 0), ("sb_w_out", 0), ("sc_w_out", 0), ("dn_w_out", 1))
_GATHER_2 = (("sc_w_in", 0), ("sc_conv_w", 0), ("dn_w_in", 1), ("dn_conv_w", 1), ("dn_o_norm_g", 1))
_EXCHANGE_A = _GATHER_2
_EXCHANGE_B = (("sb_w_in", 0), ("sb_w_out", 0), ("dn_w_out", 0), ("sc_w_out", 0), ("dn_w_out", 1))
_EXCHANGE_C = _GATHER_0
_MATMUL_WEIGHTS = ("dn_w_in", "dn_w_out", "sb_w_in", "sb_w_out", "sc_w_in", "sc_w_out")
_COLUMN_SHARDED = ("dn_w_in", "dn_conv_w", "dn_o_norm_g", "sb_w_in", "sc_w_in", "sc_conv_w")
_BLOCKED = ("sb_w_in", "sc_w_in")
_REPLICATED = ("norm_g", "dn_a_log", "dn_dt_bias", "sb_q_norm_g", "sb_k_norm_g")
_ORDER = ("norm_g", "dn_w_in", "dn_conv_w", "dn_a_log", "dn_dt_bias", "dn_o_norm_g", "dn_w_out", "sb_w_in", "sb_q_norm_g",
          "sb_k_norm_g", "sb_w_out", "sc_w_in", "sc_conv_w", "sc_w_out")
_PACK_COLS = D_MODEL
_LOSS_SLOT = (4, 2 * DN_HEADS)


def _as_2d(a):
    return a.reshape(1, -1) if a.ndim == 1 else a


def _assemble(name, gathered):
    n, r, c = gathered.shape
    if name in _COLUMN_SHARDED:
        return jnp.moveaxis(gathered, 0, 1).reshape(r, n * c)
    return gathered.reshape(n * r, c)


def _disassemble(name, full):
    r, c = full.shape
    if name in _COLUMN_SHARDED:
        return jnp.moveaxis(full.reshape(r, N_DEV, c // N_DEV), 1, 0)
    return full.reshape(N_DEV, r // N_DEV, c)


def _pack_replicated(d):
    rows = [d["norm_g"]]
    for name in _REPLICATED[1:]:
        flat = d[name].reshape(1, -1)
        rows.append(jnp.pad(flat, ((0, 0), (0, _PACK_COLS - flat.shape[1]))))
    return jnp.concatenate(rows, axis=0)


def _unpack_replicated(p, like):
    out = {"norm_g": p[:4]}
    for r, name in enumerate(_REPLICATED[1:]):
        shape = like[name].shape
        out[name] = p[4 + r, :math.prod(shape)].reshape(shape)
    return out


def kernel(x, norm_g, dn_w_in, dn_conv_w, dn_a_log, dn_dt_bias, dn_o_norm_g, dn_w_out, sb_w_in, sb_q_norm_g, sb_k_norm_g, sb_w_out, sc_w_in, sc_conv_w, sc_w_out, loss_target, m_norm_g, m_dn_w_in, m_dn_conv_w, m_dn_a_log, m_dn_dt_bias, m_dn_o_norm_g, m_dn_w_out, m_sb_w_in, m_sb_q_norm_g, m_sb_k_norm_g, m_sb_w_out, m_sc_w_in, m_sc_conv_w, m_sc_w_out, v_norm_g, v_dn_w_in, v_dn_conv_w, v_dn_a_log, v_dn_dt_bias, v_dn_o_norm_g, v_dn_w_out, v_sb_w_in, v_sb_q_norm_g, v_sb_k_norm_g, v_sb_w_out, v_sc_w_in, v_sc_conv_w, v_sc_w_out):
    w = dict(norm_g=norm_g, dn_w_in=dn_w_in, dn_conv_w=dn_conv_w, dn_a_log=dn_a_log, dn_dt_bias=dn_dt_bias,
             dn_o_norm_g=dn_o_norm_g, dn_w_out=dn_w_out, sb_w_in=sb_w_in, sb_q_norm_g=sb_q_norm_g, sb_k_norm_g=sb_k_norm_g,
             sb_w_out=sb_w_out, sc_w_in=sc_w_in, sc_conv_w=sc_conv_w, sc_w_out=sc_w_out)
    m = dict(norm_g=m_norm_g, dn_w_in=m_dn_w_in, dn_conv_w=m_dn_conv_w, dn_a_log=m_dn_a_log, dn_dt_bias=m_dn_dt_bias,
             dn_o_norm_g=m_dn_o_norm_g, dn_w_out=m_dn_w_out, sb_w_in=m_sb_w_in, sb_q_norm_g=m_sb_q_norm_g,
             sb_k_norm_g=m_sb_k_norm_g, sb_w_out=m_sb_w_out, sc_w_in=m_sc_w_in, sc_conv_w=m_sc_conv_w, sc_w_out=m_sc_w_out)
    v = dict(norm_g=v_norm_g, dn_w_in=v_dn_w_in, dn_conv_w=v_dn_conv_w, dn_a_log=v_dn_a_log, dn_dt_bias=v_dn_dt_bias,
             dn_o_norm_g=v_dn_o_norm_g, dn_w_out=v_dn_w_out, sb_w_in=v_sb_w_in, sb_q_norm_g=v_sb_q_norm_g,
             sb_k_norm_g=v_sb_k_norm_g, sb_w_out=v_sb_w_out, sc_w_in=v_sc_w_in, sc_conv_w=v_sc_conv_w, sc_w_out=v_sc_w_out)

    def gather_of(keys):
        return _Gather([_as_2d(w[k][j]).astype(BF16) if k in _MATMUL_WEIGHTS else _as_2d(w[k][j]) for k, j in keys])

    def full_weights(keys, gathered):
        return {key: g if key[0] in _BLOCKED else _assemble(key[0], g) for key, g in zip(keys, gathered)}

    def exchange_of(keys, grads, extra=()):
        out = [grads[k, j] if k in _BLOCKED else
               _disassemble(k, grads[k, j].astype(BF16) if k in _MATMUL_WEIGHTS else grads[k, j]) for k, j in keys]
        return _Exchange(out + list(extra), [True] * len(out) + [False] * len(extra))

    xs, saves = [x[0]], []
    h, got = _rmsnorm_fwd(xs[0], norm_g[0:1], "norm0", gather_of(_GATHER_0))
    F = full_weights(_GATHER_0, got)

    def w_out_0(got):
        F.update(full_weights(_GATHER_1, got))
        return F["dn_w_out", 0]

    (y, h), sv, _ = _dn_layer_fwd(h, _dn_split_w_in(F["dn_w_in", 0]), F["dn_conv_w", 0], dn_a_log[0:1], dn_dt_bias[0:1],
                                  F["dn_o_norm_g", 0], w_out_0, xs[0], "dn0", gather_of(_GATHER_1), norm_g[1:2])
    xs.append(y)
    saves.append(sv)
    (y, h), sv, got = _sb_layer_fwd(h, F["sb_w_in", 0], sb_q_norm_g, sb_k_norm_g, F["sb_w_out", 0], xs[1], "sb",
                                    gather_of(_GATHER_2), norm_g[2:3])
    F.update(full_weights(_GATHER_2, got))
    xs.append(y)
    saves.append(sv)
    (y, h), sv = _sc_layer_fwd(h, F["sc_w_in", 0], F["sc_conv_w", 0], F["sc_w_out", 0], xs[2], "sc", norm_g[3:4])
    xs.append(y)
    saves.append(sv)
    (y, _), sv, _ = _dn_layer_fwd(h, _dn_split_w_in(F["dn_w_in", 1]), F["dn_conv_w", 1], dn_a_log[1:2], dn_dt_bias[1:2],
                                  F["dn_o_norm_g", 1], F["dn_w_out", 1], xs[3], "dn1")
    xs.append(y)
    saves.append(sv)
    dx, loss_part = _loss_head(xs[4], loss_target[0])

    G, dnorm, landed = {}, [None] * 4, {}

    def keep(grads, j):
        G.update({(k, j): g for k, g in grads.items()})

    dh, grads, _, _ = _dn_layer_bwd(dx, saves[3], "dn1", None)
    keep(grads, 1)
    dx, dnorm[3] = _rmsnorm_bwd(dh, xs[3], norm_g[3:4], dx, "norm3_bwd")
    dh, grads = _sc_layer_bwd(dx, saves[2], "sc")
    keep(grads, 0)
    dx, dnorm[2] = _rmsnorm_bwd(dh, xs[2], norm_g[2:3], dx, "norm2_bwd")
    dh, grads, got = _sb_layer_bwd(dx, saves[1], "sb", exchange_of(_EXCHANGE_A, G))
    keep(grads, 0)
    landed.update(zip(_EXCHANGE_A, got))
    dx, dnorm[1] = _rmsnorm_bwd(dh, xs[1], norm_g[1:2], dx, "norm1_bwd")

    def exchange_b(dw_out):
        G["dn_w_out", 0] = dw_out
        return exchange_of(_EXCHANGE_B, G)

    def exchange_c(grads):
        keep(grads, 0)
        return exchange_of(_EXCHANGE_C, G)

    (dx, dnorm[0]), grads, got, got_late = _dn_layer_bwd(dx, saves[0], "dn0", (xs[0], norm_g[0:1], dx), exchange_b, exchange_c)
    landed.update(zip(_EXCHANGE_B, got))
    landed.update(zip(_EXCHANGE_C, got_late))
    replicated = dict(norm_g=jnp.concatenate(dnorm, axis=0),
                      dn_a_log=jnp.concatenate([G["dn_a_log", 0], G["dn_a_log", 1]], axis=0),
                      dn_dt_bias=jnp.concatenate([G["dn_dt_bias", 0], G["dn_dt_bias", 1]], axis=0),
                      sb_q_norm_g=G["sb_q_norm_g", 0], sb_k_norm_g=G["sb_k_norm_g", 0])
    pack = _pack_replicated(replicated).at[_LOSS_SLOT].set(loss_part[0, 0])
    got = _comm_call(_Exchange([pack], [False]), "exchange_replicated")

    res = {}
    for k in _ORDER:
        if k in _REPLICATED:
            continue
        shape = w[k].shape
        as_3d = lambda a: a.reshape(shape[0], math.prod(shape[1:-1]), shape[-1])
        outs = _adamw(as_3d(w[k]), as_3d(m[k]), as_3d(v[k]), [landed[k, j] for j in range(shape[0])], "adamw_" + k)
        res[k] = [o.reshape(shape) for o in outs]
    outs = _adamw(_pack_replicated(w)[None], _pack_replicated(m)[None], _pack_replicated(v)[None], [got[-1]],
                  "adamw_replicated")
    unpacked = [_unpack_replicated(o[0], w) for o in outs]
    for k in _REPLICATED:
        res[k] = [u[k] for u in unpacked]

    loss = outs[0][0][_LOSS_SLOT]
    return (loss, dx[None]) + tuple(res[k][0] for k in _ORDER) + tuple(res[k][1] for k in _ORDER) \
        + tuple(res[k][2] for k in _ORDER) + tuple(res[k][3] for k in _ORDER)
```

```python
import functools
import itertools
import math

import jax
import jax.numpy as jnp
from jax import lax
from jax.experimental import pallas as pl
from jax.experimental.pallas import tpu as pltpu

F32 = jnp.float32
BF16 = jnp.bfloat16
HIGHEST = lax.Precision.HIGHEST

N_DEV = 8
D_MODEL = 1024
RMS_EPS = 1e-6
L2_EPS = 1e-6

DN_HEADS = 8
DN_DK = 128
DN_DV = 256
DN_QK_W = DN_HEADS * DN_DK
DN_V_W = DN_HEADS * DN_DV
DN_CONV = 4
DN_CHUNK = 64
DN_CONV_W = 2 * DN_QK_W + DN_V_W
DN_IN = DN_CONV_W + DN_V_W + 2 * DN_HEADS
DN_AB_PAD = 128
DN_PREP_BLK = 512

SB_HEADS = 16
SB_DH = 64
SB_W = SB_HEADS * SB_DH
SB_PAIRS = SB_HEADS // 2
SB_TQ = 256
SB_TK = 128
SB_DEAD = -106.0

SC_W = 2 * D_MODEL
SC_CONV = 3
SC_BLK = 512
SC_NBLK = SC_W // SC_BLK

ADAM_LR = 0.001
ADAM_B1 = 0.9
ADAM_B2 = 0.999
ADAM_EPS = 1e-08
ADAM_WD = 0.01
ADAM_STEP = 10

LANE = 128
SUBLANE = 8
HALO = SUBLANE
LONG_ROW_TILE = 512
NORM_FUSED_TM = 512
DEEP_TK = 2048
WIDE_TN = 2048
WIDE_ROW_TILE = 128
VMEM_LIMIT = 48 * 2 ** 20

NN = ((1,), (0,))
NT = ((1,), (1,))
TN = ((0,), (0,))


def _dot(a, b, dims=NN, precision=None):
    return lax.dot_general(a, b, (dims, ((), ())), precision=precision, preferred_element_type=F32)


def _bdot(a, b, dims=NN):
    return _dot(a.astype(BF16), b.astype(BF16), dims)


def _hdot(a, b, dims=NN):
    return _dot(a, b, dims, precision=HIGHEST)


def _tile(dim, pref, align=LANE):
    t = (min(pref, dim) // align) * align
    while t >= align:
        if dim % t == 0:
            return t
        t -= align
    return dim


def _params(*sem):
    return pltpu.CompilerParams(dimension_semantics=sem, vmem_limit_bytes=VMEM_LIMIT)


def _sigmoid(x):
    return 0.5 * jnp.tanh(0.5 * x) + 0.5


def _softplus(x):
    return jnp.maximum(x, 0.0) + jnp.log(1.0 + jnp.exp(-jnp.abs(x)))


def _silu_and_grad(x):
    s = _sigmoid(x)
    return x * s, s * (1.0 + x * (1.0 - s))


def _iota2(shape, dim):
    return lax.broadcasted_iota(jnp.int32, shape, dim)


def _matmul(a, b, mode, name, out_dtype=F32, add=None, b_cols=None, blocked_b=False, blocked_out=0,
            norm_fwd=None, norm_bwd=None, loss_target=None, tm=1024, tn=1024, tk=1024):
    b_rows, b_width = (b.shape[1], b.shape[0] * b.shape[2]) if blocked_b else b.shape
    c0, b_used = b_cols if b_cols is not None else (0, b_width)
    if mode == "nn":
        (M, K), (K2, N) = a.shape, (b_rows, b_used)
    elif mode == "nt":
        (M, K), (N, K2) = a.shape, (b_rows, b_used)
    else:
        (K, M), (K2, N) = a.shape, (b_rows, b_used)
    assert K == K2, (a.shape, b.shape, mode)
    if mode == "tn":
        tk = max(tk, DEEP_TK)
    elif norm_fwd is None and norm_bwd is None and loss_target is None and add is None:
        tn = max(tn, WIDE_TN)
    tm, tn, tk = _tile(M, tm), _tile(N, tn), _tile(K, tk)
    if blocked_b and mode == "nt":
        tk = b.shape[2]
    elif blocked_b:
        tn = b.shape[2]
    if blocked_out:
        tn = N // blocked_out
    nk = K // tk
    dims = {"nn": NN, "nt": NT, "tn": TN}[mode]
    a_spec = pl.BlockSpec((tk, tm), lambda i, j, k: (k, i)) if mode == "tn" else pl.BlockSpec((tm, tk), lambda i, j, k: (i, k))
    if mode == "nt":
        cb0 = c0 // tk
        assert c0 % tk == 0
        b_spec = (pl.BlockSpec((None, tn, tk), lambda i, j, k: (k + cb0, j, 0)) if blocked_b
                  else pl.BlockSpec((tn, tk), lambda i, j, k: (j, k + cb0)))
    else:
        cb0 = c0 // tn
        assert c0 % tn == 0
        b_spec = (pl.BlockSpec((None, tk, tn), lambda i, j, k: (j + cb0, k, 0)) if blocked_b
                  else pl.BlockSpec((tk, tn), lambda i, j, k: (k, j + cb0)))
    o_spec = pl.BlockSpec((tm, tn), lambda i, j, k: (i, j))
    out_spec = pl.BlockSpec((None, tm, tn), lambda i, j, k: (j, i, 0)) if blocked_out else o_spec
    out_shape = (blocked_out, M, tn) if blocked_out else (M, N)
    has_add = add is not None
    vec_spec = pl.BlockSpec((1, tn), lambda i, j, k: (0, j))
    assert not (norm_fwd is not None or norm_bwd is not None or loss_target is not None) or tn == N
    sequential = norm_bwd is not None or loss_target is not None
    extra_in, extra_specs = [], []
    if has_add:
        extra_in, extra_specs = [add], [o_spec]
    if loss_target is not None:
        extra_in, extra_specs = extra_in + [loss_target], extra_specs + [o_spec]
        out_specs = [o_spec, pl.BlockSpec((1, LANE), lambda i, j, k: (0, 0))]
        out_shapes = [jax.ShapeDtypeStruct((M, N), F32), jax.ShapeDtypeStruct((1, LANE), F32)]
    elif norm_fwd is not None:
        extra_in, extra_specs = extra_in + [norm_fwd], extra_specs + [vec_spec]
        out_specs = [o_spec, o_spec]
        out_shapes = [jax.ShapeDtypeStruct((M, N), out_dtype), jax.ShapeDtypeStruct((M, N), BF16)]
    elif norm_bwd is not None:
        extra_in, extra_specs = extra_in + list(norm_bwd), extra_specs + [o_spec, vec_spec, o_spec]
        out_specs = [o_spec, vec_spec]
        out_shapes = [jax.ShapeDtypeStruct((M, N), F32), jax.ShapeDtypeStruct((1, N), F32)]
    else:
        out_specs, out_shapes = out_spec, jax.ShapeDtypeStruct(out_shape, out_dtype)

    def body(*refs):
        a_ref, b_ref = refs[0], refs[1]
        extra = list(refs[2:2 + len(extra_in)])
        outs = refs[2 + len(extra_in):]
        add_ref = extra.pop(0) if has_add else None
        p = _bdot(a_ref[...], b_ref[...], dims)

        def finish(acc):
            if has_add:
                acc = acc + add_ref[...]
            if norm_bwd is not None:
                _rmsnorm_bwd_tile(acc, *extra, outs[0], outs[1], first=pl.program_id(0) == 0)
                return
            if loss_target is not None:
                _loss_tile(acc, extra[0], outs[0], outs[1], first=pl.program_id(0) == 0)
                return
            outs[0][...] = acc.astype(out_dtype)
            if norm_fwd is not None:
                r = lax.rsqrt(jnp.mean(acc * acc, axis=-1, keepdims=True) + RMS_EPS)
                outs[1][...] = (acc * r * extra[0][...]).astype(BF16)

        if nk == 1:
            finish(p)
        else:
            acc_ref = refs[-1]
            k = pl.program_id(2)

            @pl.when(k == 0)
            def _():
                acc_ref[...] = p

            @pl.when(k > 0)
            def _():
                acc_ref[...] += p

            @pl.when(k == nk - 1)
            def _():
                finish(acc_ref[...])

    return pl.pallas_call(
        body, name=name, grid=(M // tm, N // tn, nk),
        in_specs=[a_spec, b_spec] + extra_specs, out_specs=out_specs, out_shape=out_shapes,
        scratch_shapes=[pltpu.VMEM((tm, tn), F32)] if nk > 1 else [],
        compiler_params=(_params("arbitrary", "arbitrary", "arbitrary") if sequential
                         else _params("parallel", "parallel", "arbitrary")),
    )(a, b, *extra_in)


def _loss_tile(y, t_ref, dy_ref, l_ref, first):
    D = y.shape[1]
    e = y - t_ref[...]
    dy_ref[...] = e * (1.0 / D)
    s = jnp.sum(jnp.sum(e * e, axis=1, keepdims=True), axis=0, keepdims=True) * (0.5 / D)
    s = jnp.broadcast_to(s, (1, LANE))

    @pl.when(first)
    def _():
        l_ref[...] = s

    @pl.when(jnp.logical_not(first))
    def _():
        l_ref[...] += s


def _rmsnorm_bwd_tile(dh, x_ref, g_ref, res_ref, dx_ref, dg_ref, first):
    xv = x_ref[...]
    r = lax.rsqrt(jnp.mean(xv * xv, axis=-1, keepdims=True) + RMS_EPS)
    xh = xv * r
    dxh = dh * g_ref[...]
    m = jnp.mean(dxh * xh, axis=-1, keepdims=True)
    dx_ref[...] = res_ref[...] + r * (dxh - xh * m)
    part = jnp.sum(dh * xh, axis=0, keepdims=True)

    @pl.when(first)
    def _():
        dg_ref[...] = part

    @pl.when(jnp.logical_not(first))
    def _():
        dg_ref[...] += part


def _matmul_nt_sum(pairs, name, comm=None, norm_bwd=None, tm=NORM_FUSED_TM, tk=1024):
    M, N = pairs[0][0].shape[0], pairs[0][1].shape[0]
    tm = _tile(M, tm)
    tks = [_tile(a.shape[1], tk) for a, _, _ in pairs]
    steps = [a.shape[1] // t for (a, _, _), t in zip(pairs, tks)]
    offs = [sum(steps[:p]) for p in range(len(pairs))]
    total = sum(steps)

    n_extra = 3 if norm_bwd is not None else 0

    def body(*refs):
        a_refs, b_refs = refs[0:2 * len(pairs):2], refs[1:2 * len(pairs):2]
        extra = refs[2 * len(pairs):2 * len(pairs) + n_extra]
        outs, acc_ref = refs[2 * len(pairs) + n_extra:-1], refs[-1]
        k = pl.program_id(1)
        for p in range(len(pairs)):
            @pl.when((k >= offs[p]) & (k < offs[p] + steps[p]))
            def _(p=p):
                prod = _bdot(a_refs[p][...], b_refs[p][...], NT)
                if p == 0:
                    @pl.when(k == 0)
                    def _():
                        acc_ref[...] = prod

                    @pl.when(k > 0)
                    def _():
                        acc_ref[...] += prod
                else:
                    acc_ref[...] += prod

        @pl.when(k == total - 1)
        def _():
            if norm_bwd is not None:
                _rmsnorm_bwd_tile(acc_ref[...], *extra, outs[0], outs[1], first=pl.program_id(0) == 0)
            else:
                outs[0][...] = acc_ref[...]

    in_specs, args = [], []
    for (a, b, c0), t, off, n in zip(pairs, tks, offs, steps):
        assert c0 % t == 0
        pick = lambda k, off=off, n=n: jnp.clip(k - off, 0, n - 1)
        in_specs += [pl.BlockSpec((tm, t), lambda i, k, pick=pick: (i, pick(k))),
                     pl.BlockSpec((N, t), lambda i, k, pick=pick, cb0=c0 // t: (0, pick(k) + cb0))]
        args += [a, b]
    row, vec = pl.BlockSpec((tm, N), lambda i, k: (i, 0)), pl.BlockSpec((1, N), lambda i, k: (0, 0))
    if norm_bwd is not None:
        in_specs += [row, vec, row]
        args += list(norm_bwd)
        out_specs, out_shape = [row, vec], [jax.ShapeDtypeStruct((M, N), F32), jax.ShapeDtypeStruct((1, N), F32)]
    else:
        out_specs, out_shape = [row], [jax.ShapeDtypeStruct((M, N), F32)]
    outs, landed = _call(body, comm, name=name, grid=(M // tm, total), in_specs=in_specs, out_specs=out_specs,
                         out_shape=out_shape, scratch_shapes=[pltpu.VMEM((tm, N), F32)],
                         semantics=("arbitrary", "arbitrary"), args=tuple(args))
    return (outs if norm_bwd is not None else outs[0]), landed


def _rmsnorm_fwd(x, g, name, comm=None):
    T, D = x.shape
    tt = _tile(T, LONG_ROW_TILE, SUBLANE)

    def body(x_ref, g_ref, o_ref):
        xv = x_ref[...]
        r = lax.rsqrt(jnp.mean(xv * xv, axis=-1, keepdims=True) + RMS_EPS)
        o_ref[...] = (xv * r * g_ref[...]).astype(BF16)

    outs, landed = _call(
        body, comm, name=name, grid=(T // tt,),
        in_specs=[pl.BlockSpec((tt, D), lambda i: (i, 0)), pl.BlockSpec((1, D), lambda i: (0, 0))],
        out_specs=[pl.BlockSpec((tt, D), lambda i: (i, 0))], out_shape=[jax.ShapeDtypeStruct((T, D), BF16)],
        scratch_shapes=[], semantics=("parallel",), args=(x, g))
    return outs[0], landed


def _rmsnorm_bwd(dh, x, g, dx_res, name):
    T, D = x.shape
    tt = _tile(T, LONG_ROW_TILE, SUBLANE)

    def body(dh_ref, x_ref, g_ref, res_ref, dx_ref, dg_ref):
        _rmsnorm_bwd_tile(dh_ref[...], x_ref, g_ref, res_ref, dx_ref, dg_ref, first=pl.program_id(0) == 0)

    row = pl.BlockSpec((tt, D), lambda i: (i, 0))
    vec = pl.BlockSpec((1, D), lambda i: (0, 0))
    return pl.pallas_call(
        body, name=name, grid=(T // tt,),
        in_specs=[row, row, vec, row], out_specs=[row, vec],
        out_shape=[jax.ShapeDtypeStruct((T, D), F32), jax.ShapeDtypeStruct((1, D), F32)],
        compiler_params=_params("arbitrary"),
    )(dh, x, g, dx_res)


def _down(x, k):
    return pltpu.roll(x, k, 0) if k else x


def _up(x, k):
    return pltpu.roll(x, x.shape[0] - k, 0) if k else x


def _sc_fwd(proj, conv_w, name):
    T = proj.shape[0]
    tt = _tile(T, WIDE_ROW_TILE, SUBLANE)
    B = SC_BLK

    def body(p_ref, ph_ref, w_ref, o_ref):
        keep = (pl.program_id(0) > 0).astype(F32)
        for j in range(SC_NBLK):
            cb, cc, cu, cg = (slice(k * SC_W + j * B, k * SC_W + (j + 1) * B) for k in range(4))
            cw = slice(j * B, (j + 1) * B)
            z = jnp.concatenate([ph_ref[:, cc] * ph_ref[:, cu] * keep, p_ref[:, cc] * p_ref[:, cu]], axis=0)
            cz = (w_ref[2:3, cw] * z + w_ref[1:2, cw] * _down(z, 1) + w_ref[0:1, cw] * _down(z, 2))[HALO:]
            gate = p_ref[:, cg]
            o_ref[:, cw] = (p_ref[:, cb] * cz * (gate * _sigmoid(gate))).astype(BF16)

    return pl.pallas_call(
        body, name=name, grid=(T // tt,),
        in_specs=[pl.BlockSpec((tt, 4 * SC_W), lambda i: (i, 0)),
                  pl.BlockSpec((HALO, 4 * SC_W), lambda i: (jnp.maximum(i * (tt // HALO) - 1, 0), 0)),
                  pl.BlockSpec((SC_CONV, SC_W), lambda i: (0, 0))],
        out_specs=pl.BlockSpec((tt, SC_W), lambda i: (i, 0)),
        out_shape=jax.ShapeDtypeStruct((T, SC_W), BF16),
        compiler_params=_params("parallel"),
    )(proj, proj, conv_w)


def _sc_bwd(dyg, proj, conv_w, name):
    T = proj.shape[0]
    tt = _tile(T, WIDE_ROW_TILE, SUBLANE)
    nt = T // tt
    B = SC_BLK
    hb = tt // HALO

    def body(d_ref, dn_ref, p_ref, pp_ref, pn_ref, w_ref, o_ref, dw_ref):
        i = pl.program_id(0)
        keep_p = (i > 0).astype(F32)
        keep_n = (i < nt - 1).astype(F32)
        main = slice(HALO, HALO + tt)
        parts = []
        for j in range(SC_NBLK):
            cw = slice(j * B, (j + 1) * B)

            def ext(k):
                s = slice(k * SC_W + j * B, k * SC_W + (j + 1) * B)
                return s, jnp.concatenate([pp_ref[:, s] * keep_p, p_ref[:, s], pn_ref[:, s]], axis=0)

            (sb, b), (sc, c), (su, u), (sg_, gate) = ext(0), ext(1), ext(2), ext(3)
            dyg_e = jnp.concatenate([jnp.zeros((HALO, B), F32), d_ref[:, cw], dn_ref[:, cw] * keep_n], axis=0)
            w0, w1, w2 = w_ref[0:1, cw], w_ref[1:2, cw], w_ref[2:3, cw]
            z = c * u
            z1, z2 = _down(z, 1), _down(z, 2)
            cz = w2 * z + w1 * z1 + w0 * z2
            sg, dsg = _silu_and_grad(gate)
            dy = dyg_e * sg
            dcz = dy * b
            dz = w2 * dcz + w1 * _up(dcz, 1) + w0 * _up(dcz, 2)
            o_ref[:, sb] = (dy * cz)[main].astype(BF16)
            o_ref[:, sc] = (dz * u)[main].astype(BF16)
            o_ref[:, su] = (dz * c)[main].astype(BF16)
            o_ref[:, sg_] = (dyg_e * (b * cz) * dsg)[main].astype(BF16)
            dcm = dcz[main]
            parts.append(jnp.concatenate([jnp.sum(dcm * z2[main], axis=0, keepdims=True),
                                          jnp.sum(dcm * z1[main], axis=0, keepdims=True),
                                          jnp.sum(dcm * z[main], axis=0, keepdims=True)], axis=0))
        part = jnp.concatenate(parts, axis=1)

        @pl.when(i == 0)
        def _():
            dw_ref[...] = part

        @pl.when(i > 0)
        def _():
            dw_ref[...] += part

    nxt = lambda i: (jnp.minimum((i + 1) * hb, nt * hb - 1), 0)
    return pl.pallas_call(
        body, name=name, grid=(nt,),
        in_specs=[pl.BlockSpec((tt, SC_W), lambda i: (i, 0)),
                  pl.BlockSpec((HALO, SC_W), nxt),
                  pl.BlockSpec((tt, 4 * SC_W), lambda i: (i, 0)),
                  pl.BlockSpec((HALO, 4 * SC_W), lambda i: (jnp.maximum(i * hb - 1, 0), 0)),
                  pl.BlockSpec((HALO, 4 * SC_W), nxt),
                  pl.BlockSpec((SC_CONV, SC_W), lambda i: (0, 0))],
        out_specs=[pl.BlockSpec((tt, 4 * SC_W), lambda i: (i, 0)), pl.BlockSpec((SC_CONV, SC_W), lambda i: (0, 0))],
        out_shape=[jax.ShapeDtypeStruct((T, 4 * SC_W), BF16), jax.ShapeDtypeStruct((SC_CONV, SC_W), F32)],
        compiler_params=_params("arbitrary"),
    )(dyg, dyg, proj, proj, proj, conv_w)


def _split3_dot(x, m):
    hi = x.astype(BF16)
    r1 = x - hi.astype(F32)
    mid = r1.astype(BF16)
    lo = (r1 - mid.astype(F32)).astype(BF16)
    return _dot(hi, m) + _dot(mid, m) + _dot(lo, m)


def _split2_dot(x, m):
    hi = x.astype(BF16)
    lo = (x - hi.astype(F32)).astype(BF16)
    return _dot(hi, m) + _dot(lo, m)


def _head_mean_matrix():
    r, c = _iota2((LANE, LANE), 0), _iota2((LANE, LANE), 1)
    return jnp.where((r // SB_DH) == (c // SB_DH), 1.0 / SB_DH, 0.0).astype(BF16)


def _sb_prep(proj, qg2, kg2, name):
    T = proj.shape[0]
    tt = _tile(T, WIDE_ROW_TILE, SUBLANE)

    def body(p_ref, qg_ref, kg_ref, q_ref, k_ref, v_ref):
        bd = _head_mean_matrix()

        def norm(x, g, scale):
            r = lax.rsqrt(_split3_dot(x * x, bd) + RMS_EPS)
            return (x * r * g * scale).astype(BF16)

        v_ref[...] = p_ref[:, 2 * SB_W:3 * SB_W].astype(BF16)
        for p in range(SB_PAIRS):
            cols = slice(p * LANE, (p + 1) * LANE)
            q_ref[:, cols] = norm(p_ref[:, cols], qg_ref[...], SB_DH ** -0.5)
            k_ref[:, cols] = norm(p_ref[:, SB_W + p * LANE:SB_W + (p + 1) * LANE], kg_ref[...], 1.0)

    blk = pl.BlockSpec((tt, SB_W), lambda i: (i, 0))
    vec = pl.BlockSpec((1, LANE), lambda i: (0, 0))
    return pl.pallas_call(
        body, name=name, grid=(T // tt,),
        in_specs=[pl.BlockSpec((tt, 4 * SB_W), lambda i: (i, 0)), vec, vec],
        out_specs=[blk, blk, blk],
        out_shape=[jax.ShapeDtypeStruct((T, SB_W), BF16)] * 3,
        compiler_params=_params("parallel"),
    )(proj, qg2, kg2)


def _sb_prep_bwd(proj, dqn, dkn, dv, dgate, qg2, kg2, name):
    T = proj.shape[0]
    tt = _tile(T, WIDE_ROW_TILE, SUBLANE)

    def body(p_ref, dq_ref, dk_ref, dv_ref, dg_ref, qg_ref, kg_ref, o_ref, dqg_ref, dkg_ref):
        i = pl.program_id(0)
        bd = _head_mean_matrix()

        def norm_bwd(x, g, dy):
            r = lax.rsqrt(_split3_dot(x * x, bd) + RMS_EPS)
            xh = x * r
            dxh = dy * g
            m = _split3_dot(dxh * xh, bd)
            return r * (dxh - xh * m), jnp.sum(dy * xh, axis=0, keepdims=True)

        o_ref[:, 2 * SB_W:3 * SB_W] = dv_ref[...].astype(BF16)
        o_ref[:, 3 * SB_W:4 * SB_W] = dg_ref[...].astype(BF16)
        pq = pk = jnp.zeros((1, LANE), F32)
        for p in range(SB_PAIRS):
            cols, kcols = slice(p * LANE, (p + 1) * LANE), slice(SB_W + p * LANE, SB_W + (p + 1) * LANE)
            dxq, sq = norm_bwd(p_ref[:, cols], qg_ref[...], dq_ref[:, cols])
            dxk, sk = norm_bwd(p_ref[:, kcols], kg_ref[...], dk_ref[:, cols])
            o_ref[:, cols] = dxq.astype(BF16)
            o_ref[:, kcols] = dxk.astype(BF16)
            pq, pk = pq + sq, pk + sk

        @pl.when(i == 0)
        def _():
            dqg_ref[...] = pq
            dkg_ref[...] = pk

        @pl.when(i > 0)
        def _():
            dqg_ref[...] += pq
            dkg_ref[...] += pk

    blk = pl.BlockSpec((tt, SB_W), lambda i: (i, 0))
    vec = pl.BlockSpec((1, LANE), lambda i: (0, 0))
    wide = pl.BlockSpec((tt, 4 * SB_W), lambda i: (i, 0))
    return pl.pallas_call(
        body, name=name, grid=(T // tt,),
        in_specs=[wide, blk, blk, blk, blk, vec, vec],
        out_specs=[wide, vec, vec],
        out_shape=[jax.ShapeDtypeStruct((T, 4 * SB_W), BF16)] + [jax.ShapeDtypeStruct((1, LANE), F32)] * 2,
        compiler_params=_params("arbitrary"),
    )(proj, dqn, dkn, dv, dgate, qg2, kg2)


def _fold_heads(part, name):
    def body(p_ref, o_ref):
        r, c = _iota2((LANE, SB_DH), 0), _iota2((LANE, SB_DH), 1)
        fold = jnp.where((r % SB_DH) == c, 1.0, 0.0).astype(F32)
        o_ref[...] = jnp.sum(_hdot(p_ref[...], fold), axis=0, keepdims=True)

    return pl.pallas_call(body, name=name, out_shape=jax.ShapeDtypeStruct((1, SB_DH), F32))(part)


def _sb_masks():
    lane = _iota2((1, LANE), 1)
    return lane < SB_DH


def _sb_attn_fwd(qn, kn, vb, proj, name, comm=None):
    T = qn.shape[0]
    tq, tk = _tile(T, SB_TQ, SUBLANE), SB_TK
    assert tq % tk == 0

    def body(q_ref, k_ref, v_ref, g_ref, o_ref, og_ref, lt_ref, done_ref):
        i = pl.program_id(1)
        ma = _sb_masks()
        q2 = q_ref[...]
        zero = jnp.zeros_like(q2)
        qs = (jnp.where(ma, q2, zero), jnp.where(ma, zero, q2))
        upper = (_iota2((tk, tk), 0) > _iota2((tk, tk), 1)).astype(BF16)
        qpos = i * tq + _iota2((tq, tk), 0)
        nb = tq // tk

        def trip(kb_top, masked, carry):
            acc, la, lb = carry
            chains = [(b, h) for b in range(nb) for h in range(2)]
            k2s, vss, masks = [], [], []
            for b in range(nb):
                kb = kb_top - b
                rows = pl.ds(pl.multiple_of(kb * tk, tk), tk)
                k2s.append(k_ref[rows, :])
                v2 = v_ref[rows, :]
                zv = jnp.zeros_like(v2)
                vss.append((jnp.where(ma, v2, zv), jnp.where(ma, zv, v2)))
                masks.append((kb * tk + _iota2((tq, tk), 1)) < qpos if masked else None)
            zs = [_dot(qs[h], k2s[b], NT) for b, h in chains]
            ts = [jnp.log(1.0 + jnp.exp(-jnp.abs(z))) for z in zs]
            ls = [-(jnp.maximum(z, 0.0) + t) for z, t in zip(zs, ts)]
            if masked:
                ls = [jnp.where(masks[b], l, 0.0) for (b, h), l in zip(chains, ls)]
            cums = [_split2_dot(l, upper) for l in ls]
            sums = [jnp.sum(l, axis=1, keepdims=True) for l in ls]
            offs, tot = {}, [la, lb]
            for b in range(nb):
                for h in range(2):
                    offs[(b, h)] = tot[h]
                    tot[h] = tot[h] + sums[chains.index((b, h))]
            ws = [jnp.exp(jnp.minimum(z, 0.0) - t + c + offs[ch]) for ch, z, t, c in zip(chains, zs, ts, cums)]
            if masked:
                ws = [jnp.where(masks[b], w, 0.0) for (b, h), w in zip(chains, ws)]
            for (b, h), w in zip(chains, ws):
                acc = acc + _dot(w.astype(BF16), vss[b][h])
            return acc, tot[0], tot[1]

        def largest(la, lb):
            return jnp.max(jnp.maximum(la, lb))

        z1 = jnp.zeros((tq, 1), F32)
        acc, la, lb = trip((i + 1) * nb - 1, True, (jnp.zeros((tq, LANE), F32), z1, z1))

        def live(c):
            return (c[0] < i) & (c[4] > SB_DEAD)

        def more(c):
            j, acc, la, lb, _ = c
            acc, la, lb = trip((i - j) * nb - 1, False, (acc, la, lb))
            return j + 1, acc, la, lb, largest(la, lb)

        done, acc, la, lb, _ = lax.while_loop(live, more, (jnp.int32(0), acc, la, lb, largest(la, lb)))
        gate = g_ref[...]
        o_ref[...] = acc
        og_ref[...] = (acc * (gate * _sigmoid(gate))).astype(BF16)
        lt_ref[...] = jnp.where(_iota2((tq, 2), 1) == 0, la, lb)
        done_ref[...] = jnp.full((SUBLANE, LANE), done, F32)

    nq = T // tq
    qblk = pl.BlockSpec((tq, LANE), lambda p, i: (i, p))
    full = pl.BlockSpec((T, LANE), lambda p, i: (0, p))
    return _call(
        body, comm, name=name, grid=(SB_PAIRS, nq),
        in_specs=[qblk, full, full, pl.BlockSpec((tq, LANE), lambda p, i: (i, 3 * SB_PAIRS + p))],
        out_specs=[qblk, qblk, pl.BlockSpec((None, tq, 2), lambda p, i: (p, i, 0)),
                   pl.BlockSpec((None, None, SUBLANE, LANE), lambda p, i: (p, i, 0, 0))],
        out_shape=[jax.ShapeDtypeStruct((T, SB_W), F32), jax.ShapeDtypeStruct((T, SB_W), BF16),
                   jax.ShapeDtypeStruct((SB_PAIRS, T, 2), F32), jax.ShapeDtypeStruct((SB_PAIRS, nq, SUBLANE, LANE), F32)],
        scratch_shapes=[], semantics=("parallel", "parallel"), args=(qn, kn, vb, proj))


def _sb_attn_bwd(qn, kn, vb, dog, o, ltot, done, proj, name, comm=None):
    T = qn.shape[0]
    tq, tk = _tile(T, SB_TQ, SUBLANE), SB_TK

    def body(q_ref, k_ref, v_ref, dog_ref, o_ref, lt_ref, done_ref, g_ref, dq_ref, dk_ref, dv_ref, dgate_ref):
        i = pl.program_id(1)
        first_trip = i - jnp.max(done_ref[...]).astype(jnp.int32)

        @pl.when(i == 0)
        def _():
            dk_ref[...] = jnp.zeros_like(dk_ref)
            dv_ref[...] = jnp.zeros_like(dv_ref)

        ma = _sb_masks()
        gate, o2, dog2 = g_ref[...], o_ref[...], dog_ref[...]
        sg, dsg = _silu_and_grad(gate)
        do2 = dog2 * sg
        dgate_ref[...] = dog2 * o2 * dsg
        lt = lt_ref[...]
        first = _iota2((tq, 2), 1) == 0
        ltots = (jnp.sum(jnp.where(first, lt, 0.0), axis=1, keepdims=True),
                 jnp.sum(jnp.where(first, 0.0, lt), axis=1, keepdims=True))
        q2 = q_ref[...]
        zq = jnp.zeros_like(q2)
        qs = (jnp.where(ma, q2, zq), jnp.where(ma, zq, q2))
        dob = do2.astype(BF16)
        dos = (jnp.where(ma, dob, zq), jnp.where(ma, zq, dob))
        upto = (_iota2((tk, tk), 0) <= _iota2((tk, tk), 1)).astype(BF16)
        before = (_iota2((tk, tk), 0) < _iota2((tk, tk), 1)).astype(BF16)
        qpos = i * tq + _iota2((tq, tk), 0)
        nb = tq // tk

        def trip(kb_bot, masked, carry):
            dq, la, lb, ea, eb = carry
            chains = [(b, h) for b in range(nb) for h in range(2)]
            rows, k2s, v2s, kss, masks = [], [], [], [], []
            for b in range(nb):
                kb = kb_bot + b
                rows.append(pl.ds(pl.multiple_of(kb * tk, tk), tk))
                k2 = k_ref[rows[b], :]
                zk = jnp.zeros_like(k2)
                k2s.append(k2)
                v2s.append(v_ref[rows[b], :])
                kss.append((jnp.where(ma, k2, zk), jnp.where(ma, zk, k2)))
                masks.append((kb * tk + _iota2((tq, tk), 1)) < qpos if masked else None)

            def keep(vals):
                return [jnp.where(masks[b], x, 0.0) for (b, h), x in zip(chains, vals)] if masked else vals

            zs = [_dot(qs[h], k2s[b], NT) for b, h in chains]
            dws = [_dot(dos[h], v2s[b], NT) for b, h in chains]
            ts = [jnp.log(1.0 + jnp.exp(-jnp.abs(z))) for z in zs]
            ls = keep([-(jnp.maximum(z, 0.0) + t) for z, t in zip(zs, ts)])
            lps = [jnp.minimum(z, 0.0) - t for z, t in zip(zs, ts)]
            cums = [_split3_dot(l, upto) for l in ls]
            lsums = [jnp.sum(l, axis=1, keepdims=True) for l in ls]
            offs, tot = {}, [la, lb]
            for b in range(nb):
                for h in range(2):
                    offs[(b, h)] = tot[h]
                    tot[h] = tot[h] + lsums[chains.index((b, h))]
            ws = keep([jnp.exp(lp + (ltots[h] - (offs[(b, h)] + c))) for (b, h), lp, c in zip(chains, lps, cums)])
            es = [dw * w for dw, w in zip(dws, ws)]
            ecums = [_split2_dot(e, before) for e in es]
            esums = [jnp.sum(e, axis=1, keepdims=True) for e in es]
            eoffs, etot = {}, [ea, eb]
            for b in range(nb):
                for h in range(2):
                    eoffs[(b, h)] = etot[h]
                    etot[h] = etot[h] + esums[chains.index((b, h))]
            dzs = keep([e - jnp.exp(lp) * (e + eoffs[ch] + ec) for ch, e, lp, ec in zip(chains, es, lps, ecums)])
            dzs = [dz.astype(BF16) for dz in dzs]
            wbs = [w.astype(BF16) for w in ws]
            for (b, h), dz in zip(chains, dzs):
                dq = dq + _dot(dz, kss[b][h])
            for b in range(nb):
                ia, ib = chains.index((b, 0)), chains.index((b, 1))
                dk_ref[rows[b], :] += _dot(dzs[ia], qs[0], TN) + _dot(dzs[ib], qs[1], TN)
                dv_ref[rows[b], :] += _dot(wbs[ia], dos[0], TN) + _dot(wbs[ib], dos[1], TN)
            return dq, tot[0], tot[1], etot[0], etot[1]

        z1 = jnp.zeros((tq, 1), F32)
        carry = lax.fori_loop(first_trip, i, lambda j, c: trip(j * nb, False, c),
                              (jnp.zeros((tq, LANE), F32), z1, z1, z1, z1))
        dq = trip(i * nb, True, carry)[0]
        dq_ref[...] = dq * (SB_DH ** -0.5)

    qblk = pl.BlockSpec((tq, LANE), lambda p, i: (i, p))
    full = pl.BlockSpec((T, LANE), lambda p, i: (0, p))
    return _call(
        body, comm, name=name, grid=(SB_PAIRS, T // tq),
        in_specs=[qblk, full, full, qblk, qblk, pl.BlockSpec((None, tq, 2), lambda p, i: (p, i, 0)),
                  pl.BlockSpec((None, None, SUBLANE, LANE), lambda p, i: (p, i, 0, 0)),
                  pl.BlockSpec((tq, LANE), lambda p, i: (i, 3 * SB_PAIRS + p))],
        out_specs=[qblk, full, full, qblk],
        out_shape=[jax.ShapeDtypeStruct((T, SB_W), F32)] * 4,
        scratch_shapes=[], semantics=("parallel", "arbitrary"), args=(qn, kn, vb, dog, o, ltot, done, proj))


def _dn_conv(ext, w_ref, cw):
    return (w_ref[3:4, cw] * ext + w_ref[2:3, cw] * _down(ext, 1) + w_ref[1:2, cw] * _down(ext, 2)
            + w_ref[0:1, cw] * _down(ext, 3))


def _dn_gates(a_in, b_in, a_log, dt_bias, name):
    T, H = a_in.shape
    C = DN_CHUNK

    def body(a_ref, b_ref, al_ref, dt_ref, g_ref, beta_ref):
        beta_ref[...] = _sigmoid(b_ref[...])
        g_ref[...] = -jnp.exp(al_ref[...]) * _softplus(a_ref[...] + dt_ref[...])
        tri = (_iota2((C, C), 0) >= _iota2((C, C), 1)).astype(F32)

        def chunk(n, carry):
            rows = pl.ds(pl.multiple_of(n * C, C), C)
            g_ref[rows, :] = _hdot(tri, g_ref[rows, :])
            return carry

        lax.fori_loop(0, T // C, chunk, 0)

    return pl.pallas_call(body, name=name, out_shape=[jax.ShapeDtypeStruct((T, H), F32)] * 2)(a_in, b_in, a_log, dt_bias)


def _dn_gates_bwd(dg, dbeta, a_in, b_in, a_log, dt_bias, name):
    T, H = a_in.shape
    C = DN_CHUNK

    def body(dg_ref, db_ref, a_ref, b_ref, al_ref, dt_ref, da_ref, dbi_ref, dal_ref, ddt_ref):
        tri_t = (_iota2((C, C), 0) <= _iota2((C, C), 1)).astype(F32)

        def chunk(n, carry):
            rows = pl.ds(pl.multiple_of(n * C, C), C)
            da_ref[rows, :] = _hdot(tri_t, dg_ref[rows, :])
            return carry

        lax.fori_loop(0, T // C, chunk, 0)
        dla = da_ref[...]
        x = a_ref[...] + dt_ref[...]
        ea = jnp.exp(al_ref[...])
        da = dla * (-ea) * _sigmoid(x)
        da_ref[...] = da
        dal_ref[...] = jnp.sum(dla * (-ea * _softplus(x)), axis=0, keepdims=True)
        ddt_ref[...] = jnp.sum(da, axis=0, keepdims=True)
        beta = _sigmoid(b_ref[...])
        dbi_ref[...] = db_ref[...] * beta * (1.0 - beta)

    return pl.pallas_call(
        body, name=name,
        out_shape=[jax.ShapeDtypeStruct((T, H), F32)] * 2 + [jax.ShapeDtypeStruct((1, H), F32)] * 2,
    )(dg, dbeta, a_in, b_in, a_log, dt_bias)


def _dn_chunk_terms(q, k, gc, bc):
    C = DN_CHUNK
    r, c = _iota2((C, C), 0), _iota2((C, C), 1)
    lower, strict, eye = r >= c, r > c, r == c
    grow = jnp.sum(jnp.where(eye, gc, 0.0), axis=0, keepdims=True)
    decay = jnp.where(lower, jnp.exp(jnp.where(lower, gc - grow, 0.0)), 0.0)
    last = _iota2((C, 1), 0) == C - 1
    gl = jnp.sum(jnp.where(last, gc, 0.0), axis=0, keepdims=True)
    eg = jnp.exp(gc)
    egl = jnp.exp(gl - gc)
    kb = k * bc
    lmat = jnp.where(strict, _bdot(kb, k, NT) * decay, 0.0)
    aqk = jnp.where(lower, _bdot(q, k, NT) * decay, 0.0)
    return dict(lower=lower, strict=strict, eye=eye, last=last, decay=decay, gl=gl, eg=eg, egl=egl, kb=kb,
                lmat=lmat, aqk=aqk, qd=q * eg, kd=k * egl)


def _split(x):
    hi = x.astype(BF16)
    return hi, (x - hi.astype(F32)).astype(BF16)


def _x3dot(a, b, dims=NN):
    ah, al = a if isinstance(a, tuple) else _split(a)
    bh, bl = b if isinstance(b, tuple) else _split(b)
    return _dot(ah, bh, dims) + (_dot(ah, bl, dims) + _dot(al, bh, dims))


def _interleave(gens):
    for _ in itertools.zip_longest(*gens):
        pass


def _unit_lower_inverse_steps(lmat, eye, out):
    ident = jnp.where(eye, 1.0, 0.0).astype(F32)
    m = -lmat
    inv = ident + m
    for _ in range(int(math.log2(DN_CHUNK)) - 1):
        ms = _split(m)
        m = _x3dot(ms, ms)
        yield
        inv = inv + _x3dot(inv, m)
        yield
    out["tm"] = inv


def _dn_chunk_fwd(pqkv, conv_w, g, beta, pgate, gn, name, comm=None):
    T = pqkv.shape[0]
    C, H = DN_CHUNK, DN_HEADS
    N = T // C
    B = DN_PREP_BLK
    nq, nqk = DN_QK_W // B, 2 * DN_QK_W // B

    def step(p_ref, cw_ref, g_ref, b_ref, pg_ref, gn_ref, act_out, o_ref, og_ref, s_out, t_out, vn_out, u_out, w_out,
             s_scr, tail_scr, a_ref, a_next):
        head_lane = _iota2((C, H), 1)

        def prepare(cb):
            cw = slice(cb * B, (cb + 1) * B)
            ext = jnp.concatenate([tail_scr[:, cw], p_ref[:, cw]], axis=0)
            c = _dn_conv(ext, cw_ref, cw)[HALO:]
            yield
            a = c * _sigmoid(c)
            if cb >= nqk:
                a_next[:, cw] = a
                act_out[:, cw] = a
                return
            scale = DN_DK ** -0.5 if cb < nq else 1.0
            for hh in range(B // DN_DK):
                yield
                ah = a[:, hh * DN_DK:(hh + 1) * DN_DK]
                val = ah * (lax.rsqrt(jnp.sum(ah * ah, axis=-1, keepdims=True) + L2_EPS) * scale)
                cols = slice(cb * B + hh * DN_DK, cb * B + (hh + 1) * DN_DK)
                a_next[:, cols] = val
                act_out[:, cols] = val

        def head(hh):
            qs, vs = slice(hh * DN_DK, (hh + 1) * DN_DK), slice(hh * DN_DV, (hh + 1) * DN_DV)
            q, k, v = a_ref[:, qs], a_ref[:, DN_QK_W + hh * DN_DK:DN_QK_W + (hh + 1) * DN_DK], \
                a_ref[:, 2 * DN_QK_W + hh * DN_DV:2 * DN_QK_W + (hh + 1) * DN_DV]
            gc = jnp.sum(jnp.where(head_lane == hh, g_ref[...], 0.0), axis=1, keepdims=True)
            bc = jnp.sum(jnp.where(head_lane == hh, b_ref[...], 0.0), axis=1, keepdims=True)
            t = _dn_chunk_terms(q, k, gc, bc)
            yield
            res = {}
            yield from _unit_lower_inverse_steps(t["lmat"], t["eye"], res)
            tms = _split(res["tm"])
            u = _x3dot(tms, v * bc)
            yield
            w = _x3dot(tms, t["kb"] * t["eg"])
            yield
            s = s_scr[hh]
            s_out[hh] = s
            t_out[hh] = res["tm"]
            sb = s.astype(BF16)
            vn = u - _dot(w.astype(BF16), sb)
            yield
            o = _dot(t["qd"].astype(BF16), sb) + _bdot(t["aqk"], vn)
            yield
            s_scr[hh] = s * jnp.exp(t["gl"]) + _bdot(t["kd"], vn, TN)
            vn_out[:, vs] = vn
            u_out[:, vs] = u
            w_out[:, qs] = w
            o_ref[:, vs] = o
            gate = pg_ref[:, vs]
            r = lax.rsqrt(jnp.mean(o * o, axis=-1, keepdims=True) + RMS_EPS)
            og_ref[:, vs] = (o * r * gn_ref[...] * (gate * _sigmoid(gate))).astype(BF16)

        _interleave([prepare(cb) for cb in range(DN_CONV_W // B)] + [head(hh) for hh in range(H)])

        @pl.when(pl.program_id(0) < N - 1)
        def _():
            tail_scr[...] = p_ref[C - HALO:C, :]

    def body(*refs):
        s = pl.program_id(0)
        io, (s_scr, tail_scr, buf_a, buf_b) = refs[:-4], refs[-4:]

        @pl.when(s == 0)
        def _():
            tail_scr[...] = jnp.zeros_like(tail_scr)
            buf_b[...] = jnp.zeros_like(buf_b)

        @pl.when(s <= 1)
        def _():
            s_scr[...] = jnp.zeros_like(s_scr)

        @pl.when(s % 2 == 0)
        def _():
            step(*io, s_scr, tail_scr, buf_b, buf_a)

        @pl.when(s % 2 == 1)
        def _():
            step(*io, s_scr, tail_scr, buf_a, buf_b)

    nxt = lambda w: pl.BlockSpec((C, w), lambda s: (jnp.minimum(s, N - 1), 0))
    cur = lambda w: pl.BlockSpec((C, w), lambda s: (jnp.maximum(s - 1, 0), 0))
    per_chunk = lambda a, b: pl.BlockSpec((H, None, a, b), lambda s: (0, jnp.maximum(s - 1, 0), 0, 0))
    return _call(
        body, comm, name=name, grid=(N + 1,),
        in_specs=[nxt(DN_CONV_W), pl.BlockSpec((DN_CONV, DN_CONV_W), lambda s: (0, 0)), cur(H), cur(H), cur(DN_V_W),
                  pl.BlockSpec((1, DN_DV), lambda s: (0, 0))],
        out_specs=[nxt(DN_CONV_W), cur(DN_V_W), cur(DN_V_W), per_chunk(DN_DK, DN_DV), per_chunk(C, C),
                   cur(DN_V_W), cur(DN_V_W), cur(DN_QK_W)],
        out_shape=[jax.ShapeDtypeStruct((T, DN_CONV_W), F32),
                   jax.ShapeDtypeStruct((T, DN_V_W), F32), jax.ShapeDtypeStruct((T, DN_V_W), BF16),
                   jax.ShapeDtypeStruct((H, N, DN_DK, DN_DV), F32),
                   jax.ShapeDtypeStruct((H, N, C, C), F32),
                   jax.ShapeDtypeStruct((T, DN_V_W), F32),
                   jax.ShapeDtypeStruct((T, DN_V_W), F32),
                   jax.ShapeDtypeStruct((T, DN_QK_W), F32)],
        scratch_shapes=[pltpu.VMEM((H, DN_DK, DN_DV), F32), pltpu.VMEM((HALO, DN_CONV_W), F32),
                        pltpu.VMEM((C, DN_CONV_W), F32), pltpu.VMEM((C, DN_CONV_W), F32)],
        semantics=("arbitrary",), args=(pqkv, conv_w, g, beta, pgate, gn))


def _dn_chunk_bwd(pqkv, conv_w, act, g, beta, s_saved, tm_saved, vn_saved, u_saved, w_saved, dog, o_raw, pgate, gn,
                  name, comm=None):
    T = act.shape[0]
    C, H = DN_CHUNK, DN_HEADS
    N = T // C
    assert N % 2 == 0
    B = DN_PREP_BLK
    nq, nqk = DN_QK_W // B, 2 * DN_QK_W // B
    main = slice(HALO, HALO + C)

    def prepare_bwd(cb, p_ref, pp_ref, pn_ref, cw_ref, dread, dnext_scr, dp_ref, conv_parts):
        s = pl.program_id(0)
        keep_p = (N - s > 0).astype(F32)
        keep_n = (s > 1).astype(F32)
        cw = slice(cb * B, (cb + 1) * B)
        ext = jnp.concatenate([pp_ref[:, cw] * keep_p, p_ref[:, cw], pn_ref[:, cw]], axis=0)
        c = _dn_conv(ext, cw_ref, cw)
        yield
        sg = _sigmoid(c)
        da_dc = sg * (1.0 + c * (1.0 - sg))
        d_up = jnp.concatenate([jnp.zeros((HALO, B), F32), dread[:, cw], dnext_scr[:, cw] * keep_n], axis=0)
        if cb < nqk:
            a = c * sg
            scale = DN_DK ** -0.5 if cb < nq else 1.0
            normed = []
            for hh in range(B // DN_DK):
                yield
                cols = slice(hh * DN_DK, (hh + 1) * DN_DK)
                ah = a[:, cols]
                r = lax.rsqrt(jnp.sum(ah * ah, axis=-1, keepdims=True) + L2_EPS)
                y = ah * r
                dy = d_up[:, cols] * scale
                normed.append(r * (dy - y * jnp.sum(dy * y, axis=-1, keepdims=True)))
            d_up = jnp.concatenate(normed, axis=1)
        yield
        dc = d_up * da_dc
        dp = (cw_ref[3:4, cw] * dc + cw_ref[2:3, cw] * _up(dc, 1) + cw_ref[1:2, cw] * _up(dc, 2)
              + cw_ref[0:1, cw] * _up(dc, 3))
        dp_ref[:, cw] = dp[main].astype(BF16)
        yield
        dcm = dc[main]
        conv_parts[cb] = jnp.concatenate([jnp.sum(dcm * _down(ext, 3 - k)[main], axis=0, keepdims=True)
                                          for k in range(DN_CONV)], axis=0)

    def finish_prepare(conv_parts, dconv_ref, dread, dnext_scr):
        part = jnp.concatenate([conv_parts[cb] for cb in range(DN_CONV_W // B)], axis=1)

        @pl.when(pl.program_id(0) == 0)
        def _():
            dconv_ref[...] = part

        @pl.when(pl.program_id(0) > 0)
        def _():
            dconv_ref[...] += part

        dnext_scr[...] = dread[0:HALO, :]

    def step(a_ref, g_ref, b_ref, s_ref, t_ref, vn_ref, u_ref, w_ref, dog_ref, o_ref, pg_ref, gn_ref,
             p_ref, pp_ref, pn_ref, cw_ref, dp_ref, dconv_ref, dg_ref, db_ref, dgate_ref, dgn_ref,
             ds_scr, dnext_scr, dwrite, dread):
        head_lane = _iota2((C, H), 1)
        dg_cols, db_cols, dgn_parts, conv_parts = {}, {}, {}, {}

        def output_gate_bwd(hh, vs):
            d, o, gate, gn_v = dog_ref[:, vs], o_ref[:, vs], pg_ref[:, vs], gn_ref[...]
            sg, dsg = _silu_and_grad(gate)
            r = lax.rsqrt(jnp.mean(o * o, axis=-1, keepdims=True) + RMS_EPS)
            n = o * r
            dy = d * sg
            dgate_ref[:, vs] = (d * (n * gn_v) * dsg).astype(BF16)
            dn = dy * gn_v
            dgn_parts[hh] = jnp.sum(dy * n, axis=0, keepdims=True)
            return r * (dn - n * jnp.mean(dn * n, axis=-1, keepdims=True))

        def head(hh):
            qs, vs = slice(hh * DN_DK, (hh + 1) * DN_DK), slice(hh * DN_DV, (hh + 1) * DN_DV)
            ks = slice(DN_QK_W + hh * DN_DK, DN_QK_W + (hh + 1) * DN_DK)
            vas = slice(2 * DN_QK_W + hh * DN_DV, 2 * DN_QK_W + (hh + 1) * DN_DV)
            q, k, v = a_ref[:, qs], a_ref[:, ks], a_ref[:, vas]
            gc = jnp.sum(jnp.where(head_lane == hh, g_ref[...], 0.0), axis=1, keepdims=True)
            bc = jnp.sum(jnp.where(head_lane == hh, b_ref[...], 0.0), axis=1, keepdims=True)
            t = _dn_chunk_terms(q, k, gc, bc)
            yield
            lower, strict, eye = t["lower"], t["strict"], t["eye"]
            decay, eg, egl, kb, qd, kd = t["decay"], t["eg"], t["egl"], t["kb"], t["qd"], t["kd"]
            s, tm, vn, u, w = s_ref[hh], t_ref[hh], vn_ref[:, vs], u_ref[:, vs], w_ref[:, qs]
            d_o = output_gate_bwd(hh, vs)
            ds_next = ds_scr[hh]
            egl_tot = jnp.exp(t["gl"])
            dob, sb, dsb, vnb = d_o.astype(BF16), s.astype(BF16), ds_next.astype(BF16), vn.astype(BF16)

            dvn = _bdot(t["aqk"], dob, TN) + _bdot(kd, dsb)
            yield
            daqk = jnp.where(lower, _dot(dob, vnb, NT), 0.0)
            dqd = _dot(dob, sb, NT)
            dkd = _dot(vnb, dsb, NT)
            yield
            dvnb = dvn.astype(BF16)
            ds_scr[hh] = _bdot(qd, dob, TN) + egl_tot * ds_next - _bdot(w, dvnb, TN)
            dgl = egl_tot * jnp.sum(jnp.sum(s * ds_next, axis=1, keepdims=True), axis=0, keepdims=True)
            dw = -_dot(dvnb, sb, NT)
            yield
            tms = _split(tm)
            dru = _x3dot(tms, dvn, TN)
            drw = _x3dot(tms, dw, TN)
            yield
            dl = -jnp.where(strict, _x3dot(dru, u, NT) + _x3dot(drw, w, NT), 0.0)
            yield
            dkk = (dl * decay).astype(BF16)
            dqk = (daqk * decay).astype(BF16)
            dkb = _bdot(dkk, k) + drw * eg
            yield
            dwrite[:, ks] = _bdot(dkk, kb, TN) + _bdot(dqk, q, TN) + dkd * egl + dkb * bc
            dwrite[:, qs] = _bdot(dqk, k) + dqd * eg
            dwrite[:, vas] = dru * bc
            yield
            db_cols[hh] = jnp.sum(dru * v, axis=1, keepdims=True) + jnp.sum(dkb * k, axis=1, keepdims=True)
            pm = dl * t["lmat"] + daqk * t["aqk"]
            col_as_col = jnp.sum(jnp.where(eye, jnp.sum(pm, axis=0, keepdims=True), 0.0), axis=1, keepdims=True)
            kdsum = jnp.sum(dkd * kd, axis=1, keepdims=True)
            dgc = (jnp.sum(pm, axis=1, keepdims=True) - col_as_col + jnp.sum(dqd * qd, axis=1, keepdims=True)
                   - kdsum + jnp.sum(drw * (kb * eg), axis=1, keepdims=True))
            dgl = dgl + jnp.sum(kdsum, axis=0, keepdims=True)
            dg_cols[hh] = dgc + jnp.where(t["last"], dgl, 0.0)

        _interleave([head(hh) for hh in range(H)]
                    + [prepare_bwd(cb, p_ref, pp_ref, pn_ref, cw_ref, dread, dnext_scr, dp_ref, conv_parts)
                       for cb in range(DN_CONV_W // B)])
        dg_ref[...] = sum(jnp.where(head_lane == hh, dg_cols[hh], 0.0) for hh in range(H))
        db_ref[...] = sum(jnp.where(head_lane == hh, db_cols[hh], 0.0) for hh in range(H))
        dgn_part = sum(dgn_parts[hh] for hh in range(H))

        @pl.when(pl.program_id(0) == 0)
        def _():
            dgn_ref[...] = dgn_part

        @pl.when(pl.program_id(0) > 0)
        def _():
            dgn_ref[...] += dgn_part

        finish_prepare(conv_parts, dconv_ref, dread, dnext_scr)

    def body(*refs):
        s = pl.program_id(0)
        io, (ds_scr, dnext_scr, buf_a, buf_b) = refs[:-4], refs[-4:]
        p_ref, pp_ref, pn_ref, cw_ref, dp_ref, dconv_ref = refs[12:18]

        @pl.when(s == 0)
        def _():
            ds_scr[...] = jnp.zeros_like(ds_scr)
            dnext_scr[...] = jnp.zeros_like(dnext_scr)
            buf_b[...] = jnp.zeros_like(buf_b)

        @pl.when((s < N) & (s % 2 == 0))
        def _():
            step(*io, ds_scr, dnext_scr, buf_a, buf_b)

        @pl.when((s < N) & (s % 2 == 1))
        def _():
            step(*io, ds_scr, dnext_scr, buf_b, buf_a)

        @pl.when(s == N)
        def _():
            conv_parts = {}
            _interleave([prepare_bwd(cb, p_ref, pp_ref, pn_ref, cw_ref, buf_b, dnext_scr, dp_ref, conv_parts)
                         for cb in range(DN_CONV_W // B)])
            finish_prepare(conv_parts, dconv_ref, buf_b, dnext_scr)

    cc = lambda s: jnp.maximum(N - 1 - s, 0)
    pc = lambda s: jnp.clip(N - s, 0, N - 1)
    row = lambda w: pl.BlockSpec((C, w), lambda s: (cc(s), 0))
    per_chunk = lambda a, b: pl.BlockSpec((H, None, a, b), lambda s: (0, cc(s), 0, 0))
    vec = pl.BlockSpec((1, DN_DV), lambda s: (0, 0))
    per_c = C // HALO
    conv_spec = pl.BlockSpec((DN_CONV, DN_CONV_W), lambda s: (0, 0))
    return _call(
        body, comm, name=name, grid=(N + 1,),
        in_specs=[row(DN_CONV_W), row(H), row(H), per_chunk(DN_DK, DN_DV), per_chunk(C, C),
                  row(DN_V_W), row(DN_V_W), row(DN_QK_W), row(DN_V_W), row(DN_V_W), row(DN_V_W), vec,
                  pl.BlockSpec((C, DN_CONV_W), lambda s: (pc(s), 0)),
                  pl.BlockSpec((HALO, DN_CONV_W), lambda s: (jnp.maximum(pc(s) * per_c - 1, 0), 0)),
                  pl.BlockSpec((HALO, DN_CONV_W), lambda s: (jnp.minimum((pc(s) + 1) * per_c, N * per_c - 1), 0)),
                  conv_spec],
        out_specs=[pl.BlockSpec((C, DN_CONV_W), lambda s: (pc(s), 0)), conv_spec, row(H), row(H), row(DN_V_W), vec],
        out_shape=[jax.ShapeDtypeStruct((T, DN_CONV_W), BF16), jax.ShapeDtypeStruct((DN_CONV, DN_CONV_W), F32),
                   jax.ShapeDtypeStruct((T, H), F32), jax.ShapeDtypeStruct((T, H), F32),
                   jax.ShapeDtypeStruct((T, DN_V_W), BF16), jax.ShapeDtypeStruct((1, DN_DV), F32)],
        scratch_shapes=[pltpu.VMEM((H, DN_DK, DN_DV), F32), pltpu.VMEM((HALO, DN_CONV_W), F32),
                        pltpu.VMEM((C, DN_CONV_W), F32), pltpu.VMEM((C, DN_CONV_W), F32)],
        semantics=("arbitrary",),
        args=(act, g, beta, s_saved, tm_saved, vn_saved, u_saved, w_saved, dog, o_raw, pgate, gn,
              pqkv, pqkv, pqkv, conv_w))


def _dn_split_w_in(w):
    return w, jnp.pad(w[:, DN_CONV_W + DN_V_W:], ((0, 0), (0, DN_AB_PAD - 2 * DN_HEADS)))


def _out_proj(og, w_out, x_res, next_g, name, loss_target=None):
    if next_g is None:
        assert loss_target is not None
        return tuple(_matmul(og, w_out, "nn", name, add=x_res, loss_target=loss_target, tm=LONG_ROW_TILE))
    return tuple(_matmul(og, w_out, "nn", name, add=x_res, norm_fwd=next_g, tm=NORM_FUSED_TM))


def _dn_layer_fwd(h, wts, conv_w, a_log, dt_bias, gn, w_out, x_res, tag, comm=None, next_g=None, loss_target=None):
    w_in, wab = wts
    H = DN_HEADS
    pqkv = _matmul(h, w_in, "nn", tag + "_pqkv", b_cols=(0, DN_CONV_W))
    pgate = _matmul(h, w_in, "nn", tag + "_pgate", b_cols=(DN_CONV_W, DN_V_W))
    pab = _matmul(h, wab, "nn", tag + "_pab")
    a_in, b_in = pab[:, :H], pab[:, H:2 * H]
    g, beta = _dn_gates(a_in, b_in, a_log, dt_bias, tag + "_gates")
    (act, o_raw, og, s_sv, tm_sv, vn_sv, u_sv, w_sv), landed = _dn_chunk_fwd(pqkv, conv_w, g, beta, pgate, gn,
                                                                             tag + "_chunk_fwd", comm)
    if callable(w_out):
        w_out = w_out(landed)
    y = _out_proj(og, w_out, x_res, next_g, tag + "_out", loss_target)
    saved = dict(h=h, wts=wts, conv_w=conv_w, a_log=a_log, dt_bias=dt_bias, gn=gn, w_out=w_out, pqkv=pqkv, pgate=pgate,
                 a_in=a_in, b_in=b_in, g=g, beta=beta, act=act, o_raw=o_raw, chunk=(s_sv, tm_sv, vn_sv, u_sv, w_sv), og=og)
    return y, saved, landed


def _dn_layer_bwd(dout, sv, tag, norm, comm_of=None, late_comm_of=None):
    w_in, wab = sv["wts"]
    h = sv["h"]
    dog = _matmul(dout, sv["w_out"], "nt", tag + "_dog")
    dw_out = _matmul(sv["og"], dout, "tn", tag + "_dwout", out_dtype=BF16)
    comm = comm_of(dw_out) if comm_of is not None else None
    (dpqkv, dconv, dg, dbeta, dgate, dgn), landed = _dn_chunk_bwd(
        sv["pqkv"], sv["conv_w"], sv["act"], sv["g"], sv["beta"], *sv["chunk"], dog, sv["o_raw"], sv["pgate"], sv["gn"],
        tag + "_chunk_bwd", comm)
    da_in, db_in, da_log, ddt = _dn_gates_bwd(dg, dbeta, sv["a_in"], sv["b_in"], sv["a_log"], sv["dt_bias"],
                                              tag + "_gates_bwd")
    dpab = jnp.pad(jnp.concatenate([da_in, db_in], axis=1), ((0, 0), (0, DN_AB_PAD - 2 * DN_HEADS)))
    dwqkv = _matmul(h, dpqkv, "tn", tag + "_dwqkv", out_dtype=BF16)
    dwgate = _matmul(h, dgate, "tn", tag + "_dwgate", out_dtype=BF16)
    dwab = _matmul(h, dpab, "tn", tag + "_dwab", out_dtype=BF16)
    dw_in = jnp.concatenate([dwqkv, dwgate, dwab[:, :2 * DN_HEADS]], axis=1)
    grads = dict(dn_w_in=dw_in, dn_conv_w=dconv, dn_a_log=da_log, dn_dt_bias=ddt, dn_o_norm_g=dgn, dn_w_out=dw_out)
    dx, landed_late = _matmul_nt_sum([(dpqkv, w_in, 0), (dgate, w_in, DN_CONV_W), (dpab, wab, 0)], tag + "_dh",
                                     late_comm_of(grads) if late_comm_of is not None else None, norm_bwd=norm,
                                     tm=NORM_FUSED_TM if norm is not None else 1024)
    return dx, grads, landed, landed_late


def _sb_layer_fwd(h, w_in, qg, kg, w_out, x_res, tag, comm=None, next_g=None):
    qg2, kg2 = jnp.tile(qg, (1, 2)), jnp.tile(kg, (1, 2))
    proj = _matmul(h, w_in, "nn", tag + "_proj", blocked_b=True)
    qn, kn, vb = _sb_prep(proj, qg2, kg2, tag + "_prep")
    (o, og, ltot, done), landed = _sb_attn_fwd(qn, kn, vb, proj, tag + "_attn_fwd", comm)
    y = _out_proj(og, w_out, x_res, next_g, tag + "_out")
    saved = dict(h=h, w_in=w_in, qg2=qg2, kg2=kg2, w_out=w_out, proj=proj, qn=qn, kn=kn, vb=vb, o=o, og=og, ltot=ltot,
                 done=done)
    return y, saved, landed


def _sb_layer_bwd(dout, sv, tag, comm=None):
    dog = _matmul(dout, sv["w_out"], "nt", tag + "_dog")
    dw_out = _matmul(sv["og"], dout, "tn", tag + "_dwout", out_dtype=BF16)
    (dqn, dkn, dv, dgate), landed = _sb_attn_bwd(sv["qn"], sv["kn"], sv["vb"], dog, sv["o"], sv["ltot"], sv["done"],
                                                 sv["proj"], tag + "_attn_bwd", comm)
    dproj, dqgp, dkgp = _sb_prep_bwd(sv["proj"], dqn, dkn, dv, dgate, sv["qg2"], sv["kg2"], tag + "_prep_bwd")
    dw_in = _matmul(sv["h"], dproj, "tn", tag + "_dwin", out_dtype=BF16, blocked_out=N_DEV)
    dh = _matmul(dproj, sv["w_in"], "nt", tag + "_dh", blocked_b=True)
    dqg = _fold_heads(dqgp, tag + "_dqg")
    dkg = _fold_heads(dkgp, tag + "_dkg")
    return dh, dict(sb_w_in=dw_in, sb_q_norm_g=dqg, sb_k_norm_g=dkg, sb_w_out=dw_out), landed


def _sc_layer_fwd(h, w_in, conv_w, w_out, x_res, tag, next_g=None):
    proj = _matmul(h, w_in, "nn", tag + "_proj", blocked_b=True)
    yg = _sc_fwd(proj, conv_w, tag + "_fwd")
    y = _out_proj(yg, w_out, x_res, next_g, tag + "_out")
    return y, dict(h=h, w_in=w_in, conv_w=conv_w, w_out=w_out, proj=proj, yg=yg)


def _sc_layer_bwd(dout, sv, tag):
    dyg = _matmul(dout, sv["w_out"], "nt", tag + "_dyg")
    dw_out = _matmul(sv["yg"], dout, "tn", tag + "_dwout", out_dtype=BF16)
    dproj, dconv = _sc_bwd(dyg, sv["proj"], sv["conv_w"], tag + "_bwd")
    dw_in = _matmul(sv["h"], dproj, "tn", tag + "_dwin", out_dtype=BF16, blocked_out=N_DEV)
    dh = _matmul(dproj, sv["w_in"], "nt", tag + "_dh", blocked_b=True)
    return dh, dict(sc_w_in=dw_in, sc_conv_w=dconv, sc_w_out=dw_out)


def _adamw(w, m, v, parts, name):
    L, R, C = w.shape
    tr = _tile(R, 128, SUBLANE)

    def body(*refs):
        w_ref, m_ref, v_ref = refs[:3]
        g_ref, d_ref, nm_ref, nv_ref = refs[3 + L:]

        def update(p_ref):
            g = p_ref[0].astype(F32)
            for s in range(1, N_DEV):
                g = g + p_ref[s].astype(F32)
            m2 = ADAM_B1 * m_ref[...] + (1.0 - ADAM_B1) * g
            v2 = ADAM_B2 * v_ref[...] + (1.0 - ADAM_B2) * (g * g)
            m_hat = m2 / (1.0 - ADAM_B1 ** ADAM_STEP)
            v_hat = v2 / (1.0 - ADAM_B2 ** ADAM_STEP)
            g_ref[...] = g
            d_ref[...] = -ADAM_LR * (m_hat / (jnp.sqrt(v_hat) + ADAM_EPS) + ADAM_WD * w_ref[...])
            nm_ref[...] = m2
            nv_ref[...] = v2

        for layer in range(L):
            pl.when(pl.program_id(0) == layer)(functools.partial(update, refs[3 + layer]))

    blk = pl.BlockSpec((None, tr, C), lambda l, i: (l, i, 0))
    landing = pl.BlockSpec((N_DEV, tr, C), lambda l, i: (0, i, 0))
    return pl.pallas_call(
        body, name=name, grid=(L, R // tr),
        in_specs=[blk, blk, blk] + [landing] * L,
        out_specs=[blk] * 4, out_shape=[jax.ShapeDtypeStruct((L, R, C), F32)] * 4,
        compiler_params=_params("parallel", "parallel"),
    )(w, m, v, *parts)


_HBM = pl.BlockSpec(memory_space=pltpu.HBM)
_MESH = pl.DeviceIdType.MESH


def _slot(x, y, c):
    return 4 * x + 2 * y + c


class _Gather:
    def __init__(self, shards):
        self.arrays = list(shards)
        n = len(self.arrays)
        self.out_shapes = [jax.ShapeDtypeStruct((N_DEV,) + s.shape, s.dtype) for s in self.arrays]
        self.scratch = [pltpu.SemaphoreType.DMA((n, N_DEV - 1)), pltpu.SemaphoreType.DMA((n, N_DEV - 1)),
                        pltpu.SemaphoreType.DMA((n,))]

    def _parts(self, ins, outs, sems):
        send_sems, recv_sems, local_sems = sems
        n = len(self.arrays)
        x, y, c = lax.axis_index("x"), lax.axis_index("y"), lax.axis_index("c")
        me, sibling = (x, y, c), (x, y, 1 - c)
        chips = [(1 - x, y), (x, 1 - y), (1 - x, 1 - y)]

        def copy(a, k, block, to, src=None):
            dst = outs[a].at[_slot(*block)]
            return pltpu.make_async_remote_copy(src_ref=dst if src is None else src, dst_ref=dst,
                                                send_sem=send_sems.at[a, k], recv_sem=recv_sems.at[a, k],
                                                device_id=to, device_id_type=_MESH)

        mine = [pltpu.make_async_copy(ins[a], outs[a].at[_slot(*me)], local_sems.at[a]) for a in range(n)]
        first = []
        for a in range(n):
            first.append(copy(a, 0, me, sibling, src=ins[a]))
            first += [copy(a, 1 + j, me, (*chip, c), src=ins[a]) for j, chip in enumerate(chips)]
        return n, c, me, sibling, chips, copy, mine, first

    def start(self, ins, outs, sems):
        _, _, _, _, _, _, mine, first = self._parts(ins, outs, sems)
        for cp in mine + first:
            cp.start()

    def finish(self, ins, outs, sems):
        n, c, me, sibling, chips, copy, mine, first = self._parts(ins, outs, sems)
        passed = []
        for j, chip in enumerate(chips):
            for a in range(n):
                copy(a, 1 + j, (*chip, c), me).wait_recv()
                fwd = copy(a, 4 + j, (*chip, c), sibling)
                fwd.start()
                passed.append(fwd)
        for a in range(n):
            copy(a, 0, sibling, me).wait_recv()
            for j, chip in enumerate(chips):
                copy(a, 4 + j, (*chip, 1 - c), me).wait_recv()
        for cp in first + passed:
            cp.wait_send()
        for cp in mine:
            cp.wait()


class _Exchange:
    def __init__(self, arrays, scatter):
        self.arrays, self.scatter = list(arrays), list(scatter)
        n = len(self.arrays)
        shapes = [a.shape[1:] if s else a.shape for a, s in zip(self.arrays, self.scatter)]
        self.out_shapes = [jax.ShapeDtypeStruct((N_DEV,) + tuple(s), a.dtype) for s, a in zip(shapes, self.arrays)]
        self.scratch = [pltpu.SemaphoreType.DMA((n, N_DEV - 1)), pltpu.SemaphoreType.DMA((n, N_DEV - 1)),
                        pltpu.SemaphoreType.DMA((n,))]

    def _copies(self, ins, outs, sems):
        send_sems, recv_sems, local_sems = sems
        n, scatter = len(self.arrays), self.scatter
        x, y, c = lax.axis_index("x"), lax.axis_index("y"), lax.axis_index("c")
        me = _slot(x, y, c)
        copies = [pltpu.make_async_copy(ins[a].at[me] if scatter[a] else ins[a], outs[a].at[me], local_sems.at[a])
                  for a in range(n)]
        for r in range(1, N_DEV):
            px = 1 - x if r & 4 else x
            py = 1 - y if r & 2 else y
            pc = 1 - c if r & 1 else c
            for a in range(n):
                copies.append(pltpu.make_async_remote_copy(
                    src_ref=ins[a].at[_slot(px, py, pc)] if scatter[a] else ins[a], dst_ref=outs[a].at[me],
                    send_sem=send_sems.at[a, r - 1], recv_sem=recv_sems.at[a, r - 1],
                    device_id=(px, py, pc), device_id_type=_MESH))
        return copies

    def start(self, ins, outs, sems):
        for cp in self._copies(ins, outs, sems):
            cp.start()

    def finish(self, ins, outs, sems):
        for cp in self._copies(ins, outs, sems):
            cp.wait()


def _comm_call(comm, name):
    n = len(comm.arrays)

    def body(*refs):
        ins, outs, sems = refs[:n], refs[n:2 * n], refs[2 * n:]
        comm.start(ins, outs, sems)
        comm.finish(ins, outs, sems)

    return pl.pallas_call(body, name=name, in_specs=[_HBM] * n, out_specs=[_HBM] * n, out_shape=comm.out_shapes,
                          scratch_shapes=comm.scratch)(*comm.arrays)


def _call(body, comm, *, name, grid, in_specs, out_specs, out_shape, scratch_shapes, semantics, args):
    if comm is None:
        outs = pl.pallas_call(body, name=name, grid=grid, in_specs=in_specs, out_specs=out_specs, out_shape=out_shape,
                              scratch_shapes=scratch_shapes, compiler_params=_params(*semantics))(*args)
        return outs, []
    n_in, n_out, n_scr, n_c = len(in_specs), len(out_specs), len(scratch_shapes), len(comm.arrays)

    def fused(*refs):
        ins, refs = refs[:n_in], refs[n_in:]
        c_ins, refs = refs[:n_c], refs[n_c:]
        outs, refs = refs[:n_out], refs[n_out:]
        c_outs, refs = refs[:n_c], refs[n_c:]
        scr, sems = refs[:n_scr], refs[n_scr:]
        ids = [pl.program_id(d) for d in range(len(grid))]
        first = functools.reduce(jnp.logical_and, [i == 0 for i in ids])
        last = functools.reduce(jnp.logical_and, [i == g - 1 for i, g in zip(ids, grid)])

        @pl.when(first)
        def _():
            comm.start(c_ins, c_outs, sems)

        body(*ins, *outs, *scr)

        @pl.when(last)
        def _():
            comm.finish(c_ins, c_outs, sems)

    outs = pl.pallas_call(
        fused, name=name, grid=grid, in_specs=list(in_specs) + [_HBM] * n_c, out_specs=list(out_specs) + [_HBM] * n_c,
        out_shape=list(out_shape) + comm.out_shapes, scratch_shapes=list(scratch_shapes) + comm.scratch,
        compiler_params=_params(*["arbitrary"] * len(grid)))(*args, *comm.arrays)
    return outs[:n_out], outs[n_out:]


_GATHER_0 = (("dn_w_in", 0), ("dn_conv_w", 0), ("dn_o_norm_g", 0))
_GATHER_1 = (("dn_w_out", 0), ("sb_w_in", 0), ("sb_w_out", 0), ("sc_w_out", 0), ("dn_w_out", 1))
_GATHER_2 = (("sc_w_in", 0), ("sc_conv_w", 0), ("dn_w_in", 1), ("dn_conv_w", 1), ("dn_o_norm_g", 1))
_EXCHANGE_A = _GATHER_2
_EXCHANGE_B = (("sb_w_in", 0), ("sb_w_out", 0), ("dn_w_out", 0), ("sc_w_out", 0), ("dn_w_out", 1))
_EXCHANGE_C = _GATHER_0
_MATMUL_WEIGHTS = ("dn_w_in", "dn_w_out", "sb_w_in", "sb_w_out", "sc_w_in", "sc_w_out")
_COLUMN_SHARDED = ("dn_w_in", "dn_conv_w", "dn_o_norm_g", "sb_w_in", "sc_w_in", "sc_conv_w")
_BLOCKED = ("sb_w_in", "sc_w_in")
_REPLICATED = ("norm_g", "dn_a_log", "dn_dt_bias", "sb_q_norm_g", "sb_k_norm_g")
_ORDER = ("norm_g", "dn_w_in", "dn_conv_w", "dn_a_log", "dn_dt_bias", "dn_o_norm_g", "dn_w_out", "sb_w_in", "sb_q_norm_g",
          "sb_k_norm_g", "sb_w_out", "sc_w_in", "sc_conv_w", "sc_w_out")
_PACK_COLS = D_MODEL
_LOSS_SLOT = (4, 2 * DN_HEADS)


def _as_2d(a):
    return a.reshape(1, -1) if a.ndim == 1 else a


def _assemble(name, gathered):
    n, r, c = gathered.shape
    if name in _COLUMN_SHARDED:
        return jnp.moveaxis(gathered, 0, 1).reshape(r, n * c)
    return gathered.reshape(n * r, c)


def _disassemble(name, full):
    r, c = full.shape
    if name in _COLUMN_SHARDED:
        return jnp.moveaxis(full.reshape(r, N_DEV, c // N_DEV), 1, 0)
    return full.reshape(N_DEV, r // N_DEV, c)


def _pack_replicated(d):
    rows = [d["norm_g"]]
    for name in _REPLICATED[1:]:
        flat = d[name].reshape(1, -1)
        rows.append(jnp.pad(flat, ((0, 0), (0, _PACK_COLS - flat.shape[1]))))
    return jnp.concatenate(rows, axis=0)


def _unpack_replicated(p, like):
    out = {"norm_g": p[:4]}
    for r, name in enumerate(_REPLICATED[1:]):
        shape = like[name].shape
        out[name] = p[4 + r, :math.prod(shape)].reshape(shape)
    return out


def kernel(x, norm_g, dn_w_in, dn_conv_w, dn_a_log, dn_dt_bias, dn_o_norm_g, dn_w_out, sb_w_in, sb_q_norm_g, sb_k_norm_g, sb_w_out, sc_w_in, sc_conv_w, sc_w_out, loss_target, m_norm_g, m_dn_w_in, m_dn_conv_w, m_dn_a_log, m_dn_dt_bias, m_dn_o_norm_g, m_dn_w_out, m_sb_w_in, m_sb_q_norm_g, m_sb_k_norm_g, m_sb_w_out, m_sc_w_in, m_sc_conv_w, m_sc_w_out, v_norm_g, v_dn_w_in, v_dn_conv_w, v_dn_a_log, v_dn_dt_bias, v_dn_o_norm_g, v_dn_w_out, v_sb_w_in, v_sb_q_norm_g, v_sb_k_norm_g, v_sb_w_out, v_sc_w_in, v_sc_conv_w, v_sc_w_out):
    w = dict(norm_g=norm_g, dn_w_in=dn_w_in, dn_conv_w=dn_conv_w, dn_a_log=dn_a_log, dn_dt_bias=dn_dt_bias,
             dn_o_norm_g=dn_o_norm_g, dn_w_out=dn_w_out, sb_w_in=sb_w_in, sb_q_norm_g=sb_q_norm_g, sb_k_norm_g=sb_k_norm_g,
             sb_w_out=sb_w_out, sc_w_in=sc_w_in, sc_conv_w=sc_conv_w, sc_w_out=sc_w_out)
    m = dict(norm_g=m_norm_g, dn_w_in=m_dn_w_in, dn_conv_w=m_dn_conv_w, dn_a_log=m_dn_a_log, dn_dt_bias=m_dn_dt_bias,
             dn_o_norm_g=m_dn_o_norm_g, dn_w_out=m_dn_w_out, sb_w_in=m_sb_w_in, sb_q_norm_g=m_sb_q_norm_g,
             sb_k_norm_g=m_sb_k_norm_g, sb_w_out=m_sb_w_out, sc_w_in=m_sc_w_in, sc_conv_w=m_sc_conv_w, sc_w_out=m_sc_w_out)
    v = dict(norm_g=v_norm_g, dn_w_in=v_dn_w_in, dn_conv_w=v_dn_conv_w, dn_a_log=v_dn_a_log, dn_dt_bias=v_dn_dt_bias,
             dn_o_norm_g=v_dn_o_norm_g, dn_w_out=v_dn_w_out, sb_w_in=v_sb_w_in, sb_q_norm_g=v_sb_q_norm_g,
             sb_k_norm_g=v_sb_k_norm_g, sb_w_out=v_sb_w_out, sc_w_in=v_sc_w_in, sc_conv_w=v_sc_conv_w, sc_w_out=v_sc_w_out)

    def gather_of(keys):
        return _Gather([_as_2d(w[k][j]).astype(BF16) if k in _MATMUL_WEIGHTS else _as_2d(w[k][j]) for k, j in keys])

    def full_weights(keys, gathered):
        return {key: g if key[0] in _BLOCKED else _assemble(key[0], g) for key, g in zip(keys, gathered)}

    def exchange_of(keys, grads, extra=()):
        out = [grads[k, j] if k in _BLOCKED else
               _disassemble(k, grads[k, j].astype(BF16) if k in _MATMUL_WEIGHTS else grads[k, j]) for k, j in keys]
        return _Exchange(out + list(extra), [True] * len(out) + [False] * len(extra))

    xs, saves = [x[0]], []
    h, got = _rmsnorm_fwd(xs[0], norm_g[0:1], "norm0", gather_of(_GATHER_0))
    F = full_weights(_GATHER_0, got)

    def w_out_0(got):
        F.update(full_weights(_GATHER_1, got))
        return F["dn_w_out", 0]

    (y, h), sv, _ = _dn_layer_fwd(h, _dn_split_w_in(F["dn_w_in", 0]), F["dn_conv_w", 0], dn_a_log[0:1], dn_dt_bias[0:1],
                                  F["dn_o_norm_g", 0], w_out_0, xs[0], "dn0", gather_of(_GATHER_1), norm_g[1:2])
    xs.append(y)
    saves.append(sv)
    (y, h), sv, got = _sb_layer_fwd(h, F["sb_w_in", 0], sb_q_norm_g, sb_k_norm_g, F["sb_w_out", 0], xs[1], "sb",
                                    gather_of(_GATHER_2), norm_g[2:3])
    F.update(full_weights(_GATHER_2, got))
    xs.append(y)
    saves.append(sv)
    (y, h), sv = _sc_layer_fwd(h, F["sc_w_in", 0], F["sc_conv_w", 0], F["sc_w_out", 0], xs[2], "sc", norm_g[3:4])
    xs.append(y)
    saves.append(sv)
    (dx, loss_part), sv, _ = _dn_layer_fwd(h, _dn_split_w_in(F["dn_w_in", 1]), F["dn_conv_w", 1], dn_a_log[1:2],
                                           dn_dt_bias[1:2], F["dn_o_norm_g", 1], F["dn_w_out", 1], xs[3], "dn1",
                                           loss_target=loss_target[0])
    saves.append(sv)

    G, dnorm, landed = {}, [None] * 4, {}

    def keep(grads, j):
        G.update({(k, j): g for k, g in grads.items()})

    dh, grads, _, _ = _dn_layer_bwd(dx, saves[3], "dn1", None)
    keep(grads, 1)
    dx, dnorm[3] = _rmsnorm_bwd(dh, xs[3], norm_g[3:4], dx, "norm3_bwd")
    dh, grads = _sc_layer_bwd(dx, saves[2], "sc")
    keep(grads, 0)
    dx, dnorm[2] = _rmsnorm_bwd(dh, xs[2], norm_g[2:3], dx, "norm2_bwd")
    dh, grads, got = _sb_layer_bwd(dx, saves[1], "sb", exchange_of(_EXCHANGE_A, G))
    keep(grads, 0)
    landed.update(zip(_EXCHANGE_A, got))
    dx, dnorm[1] = _rmsnorm_bwd(dh, xs[1], norm_g[1:2], dx, "norm1_bwd")

    def exchange_b(dw_out):
        G["dn_w_out", 0] = dw_out
        return exchange_of(_EXCHANGE_B, G)

    def exchange_c(grads):
        keep(grads, 0)
        return exchange_of(_EXCHANGE_C, G)

    (dx, dnorm[0]), grads, got, got_late = _dn_layer_bwd(dx, saves[0], "dn0", (xs[0], norm_g[0:1], dx), exchange_b, exchange_c)
    landed.update(zip(_EXCHANGE_B, got))
    landed.update(zip(_EXCHANGE_C, got_late))
    replicated = dict(norm_g=jnp.concatenate(dnorm, axis=0),
                      dn_a_log=jnp.concatenate([G["dn_a_log", 0], G["dn_a_log", 1]], axis=0),
                      dn_dt_bias=jnp.concatenate([G["dn_dt_bias", 0], G["dn_dt_bias", 1]], axis=0),
                      sb_q_norm_g=G["sb_q_norm_g", 0], sb_k_norm_g=G["sb_k_norm_g", 0])
    pack = _pack_replicated(replicated).at[_LOSS_SLOT].set(loss_part[0, 0])
    got = _comm_call(_Exchange([pack], [False]), "exchange_replicated")

    res = {}
    for k in _ORDER:
        if k in _REPLICATED:
            continue
        shape = w[k].shape
        as_3d = lambda a: a.reshape(shape[0], math.prod(shape[1:-1]), shape[-1])
        outs = _adamw(as_3d(w[k]), as_3d(m[k]), as_3d(v[k]), [landed[k, j] for j in range(shape[0])], "adamw_" + k)
        res[k] = [o.reshape(shape) for o in outs]
    outs = _adamw(_pack_replicated(w)[None], _pack_replicated(m)[None], _pack_replicated(v)[None], [got[-1]],
                  "adamw_replicated")
    unpacked = [_unpack_replicated(o[0], w) for o in outs]
    for k in _REPLICATED:
        res[k] = [u[k] for u in unpacked]

    loss = outs[0][0][_LOSS_SLOT]
    return (loss, dx[None]) + tuple(res[k][0] for k in _ORDER) + tuple(res[k][1] for k in _ORDER) \
        + tuple(res[k][2] for k in _ORDER) + tuple(res[k][3] for k in _ORDER)
```

```python
import functools
import itertools
import math

import jax
import jax.numpy as jnp
from jax import lax
from jax.experimental import pallas as pl
from jax.experimental.pallas import tpu as pltpu

F32 = jnp.float32
BF16 = jnp.bfloat16
HIGHEST = lax.Precision.HIGHEST

N_DEV = 8
D_MODEL = 1024
RMS_EPS = 1e-6
L2_EPS = 1e-6

DN_HEADS = 8
DN_DK = 128
DN_DV = 256
DN_QK_W = DN_HEADS * DN_DK
DN_V_W = DN_HEADS * DN_DV
DN_CONV = 4
DN_CHUNK = 64
DN_CONV_W = 2 * DN_QK_W + DN_V_W
DN_IN = DN_CONV_W + DN_V_W + 2 * DN_HEADS
DN_AB_PAD = 128
DN_PREP_BLK = 512

SB_HEADS = 16
SB_DH = 64
SB_W = SB_HEADS * SB_DH
SB_PAIRS = SB_HEADS // 2
SB_TQ = 256
SB_TK = 128
SB_DEAD = -106.0

SC_W = 2 * D_MODEL
SC_CONV = 3
SC_BLK = 512
SC_NBLK = SC_W // SC_BLK

ADAM_LR = 0.001
ADAM_B1 = 0.9
ADAM_B2 = 0.999
ADAM_EPS = 1e-08
ADAM_WD = 0.01
ADAM_STEP = 10

LANE = 128
SUBLANE = 8
HALO = SUBLANE
LONG_ROW_TILE = 512
NORM_FUSED_TM = 512
DEEP_TK = 2048
WIDE_TN = 2048
WIDE_ROW_TILE = 128
VMEM_LIMIT = 48 * 2 ** 20

NN = ((1,), (0,))
NT = ((1,), (1,))
TN = ((0,), (0,))


def _dot(a, b, dims=NN, precision=None):
    return lax.dot_general(a, b, (dims, ((), ())), precision=precision, preferred_element_type=F32)


def _bdot(a, b, dims=NN):
    return _dot(a.astype(BF16), b.astype(BF16), dims)


def _hdot(a, b, dims=NN):
    return _dot(a, b, dims, precision=HIGHEST)


def _tile(dim, pref, align=LANE):
    t = (min(pref, dim) // align) * align
    while t >= align:
        if dim % t == 0:
            return t
        t -= align
    return dim


def _params(*sem):
    return pltpu.CompilerParams(dimension_semantics=sem, vmem_limit_bytes=VMEM_LIMIT)


def _sigmoid(x):
    return 0.5 * jnp.tanh(0.5 * x) + 0.5


def _softplus(x):
    return jnp.maximum(x, 0.0) + jnp.log(1.0 + jnp.exp(-jnp.abs(x)))


def _silu_and_grad(x):
    s = _sigmoid(x)
    return x * s, s * (1.0 + x * (1.0 - s))


def _iota2(shape, dim):
    return lax.broadcasted_iota(jnp.int32, shape, dim)


def _matmul(a, b, mode, name, out_dtype=F32, add=None, b_cols=None, blocked_b=False, blocked_out=0,
            norm_fwd=None, norm_bwd=None, loss_target=None, tm=1024, tn=1024, tk=1024):
    b_rows, b_width = (b.shape[1], b.shape[0] * b.shape[2]) if blocked_b else b.shape
    c0, b_used = b_cols if b_cols is not None else (0, b_width)
    if mode == "nn":
        (M, K), (K2, N) = a.shape, (b_rows, b_used)
    elif mode == "nt":
        (M, K), (N, K2) = a.shape, (b_rows, b_used)
    else:
        (K, M), (K2, N) = a.shape, (b_rows, b_used)
    assert K == K2, (a.shape, b.shape, mode)
    if mode == "tn":
        tk = max(tk, DEEP_TK)
    elif norm_fwd is None and norm_bwd is None and loss_target is None and add is None:
        tn = max(tn, WIDE_TN)
    tm, tn, tk = _tile(M, tm), _tile(N, tn), _tile(K, tk)
    if blocked_b and mode == "nt":
        tk = b.shape[2]
    elif blocked_b:
        tn = b.shape[2]
    if blocked_out:
        tn = N // blocked_out
    nk = K // tk
    dims = {"nn": NN, "nt": NT, "tn": TN}[mode]
    a_spec = pl.BlockSpec((tk, tm), lambda i, j, k: (k, i)) if mode == "tn" else pl.BlockSpec((tm, tk), lambda i, j, k: (i, k))
    if mode == "nt":
        cb0 = c0 // tk
        assert c0 % tk == 0
        b_spec = (pl.BlockSpec((None, tn, tk), lambda i, j, k: (k + cb0, j, 0)) if blocked_b
                  else pl.BlockSpec((tn, tk), lambda i, j, k: (j, k + cb0)))
    else:
        cb0 = c0 // tn
        assert c0 % tn == 0
        b_spec = (pl.BlockSpec((None, tk, tn), lambda i, j, k: (j + cb0, k, 0)) if blocked_b
                  else pl.BlockSpec((tk, tn), lambda i, j, k: (k, j + cb0)))
    o_spec = pl.BlockSpec((tm, tn), lambda i, j, k: (i, j))
    out_spec = pl.BlockSpec((None, tm, tn), lambda i, j, k: (j, i, 0)) if blocked_out else o_spec
    out_shape = (blocked_out, M, tn) if blocked_out else (M, N)
    has_add = add is not None
    vec_spec = pl.BlockSpec((1, tn), lambda i, j, k: (0, j))
    assert not (norm_fwd is not None or norm_bwd is not None or loss_target is not None) or tn == N
    sequential = norm_bwd is not None or loss_target is not None
    extra_in, extra_specs = [], []
    if has_add:
        extra_in, extra_specs = [add], [o_spec]
    if loss_target is not None:
        extra_in, extra_specs = extra_in + [loss_target], extra_specs + [o_spec]
        out_specs = [o_spec, pl.BlockSpec((1, LANE), lambda i, j, k: (0, 0))]
        out_shapes = [jax.ShapeDtypeStruct((M, N), F32), jax.ShapeDtypeStruct((1, LANE), F32)]
    elif norm_fwd is not None:
        extra_in, extra_specs = extra_in + [norm_fwd], extra_specs + [vec_spec]
        out_specs = [o_spec, o_spec]
        out_shapes = [jax.ShapeDtypeStruct((M, N), out_dtype), jax.ShapeDtypeStruct((M, N), BF16)]
    elif norm_bwd is not None:
        extra_in, extra_specs = extra_in + list(norm_bwd), extra_specs + [o_spec, vec_spec, o_spec]
        out_specs = [o_spec, vec_spec]
        out_shapes = [jax.ShapeDtypeStruct((M, N), F32), jax.ShapeDtypeStruct((1, N), F32)]
    else:
        out_specs, out_shapes = out_spec, jax.ShapeDtypeStruct(out_shape, out_dtype)

    def body(*refs):
        a_ref, b_ref = refs[0], refs[1]
        extra = list(refs[2:2 + len(extra_in)])
        outs = refs[2 + len(extra_in):]
        add_ref = extra.pop(0) if has_add else None
        p = _bdot(a_ref[...], b_ref[...], dims)

        def finish(acc):
            if has_add:
                acc = acc + add_ref[...]
            if norm_bwd is not None:
                _rmsnorm_bwd_tile(acc, *extra, outs[0], outs[1], first=pl.program_id(0) == 0)
                return
            if loss_target is not None:
                _loss_tile(acc, extra[0], outs[0], outs[1], first=pl.program_id(0) == 0)
                return
            outs[0][...] = acc.astype(out_dtype)
            if norm_fwd is not None:
                r = lax.rsqrt(jnp.mean(acc * acc, axis=-1, keepdims=True) + RMS_EPS)
                outs[1][...] = (acc * r * extra[0][...]).astype(BF16)

        if nk == 1:
            finish(p)
        else:
            acc_ref = refs[-1]
            k = pl.program_id(2)

            @pl.when(k == 0)
            def _():
                acc_ref[...] = p

            @pl.when(k > 0)
            def _():
                acc_ref[...] += p

            @pl.when(k == nk - 1)
            def _():
                finish(acc_ref[...])

    return pl.pallas_call(
        body, name=name, grid=(M // tm, N // tn, nk),
        in_specs=[a_spec, b_spec] + extra_specs, out_specs=out_specs, out_shape=out_shapes,
        scratch_shapes=[pltpu.VMEM((tm, tn), F32)] if nk > 1 else [],
        compiler_params=(_params("arbitrary", "arbitrary", "arbitrary") if sequential
                         else _params("parallel", "parallel", "arbitrary")),
    )(a, b, *extra_in)


def _loss_tile(y, t_ref, dy_ref, l_ref, first):
    D = y.shape[1]
    e = y - t_ref[...]
    dy_ref[...] = e * (1.0 / D)
    s = jnp.sum(jnp.sum(e * e, axis=1, keepdims=True), axis=0, keepdims=True) * (0.5 / D)
    s = jnp.broadcast_to(s, (1, LANE))

    @pl.when(first)
    def _():
        l_ref[...] = s

    @pl.when(jnp.logical_not(first))
    def _():
        l_ref[...] += s


def _rmsnorm_bwd_tile(dh, x_ref, g_ref, res_ref, dx_ref, dg_ref, first):
    xv = x_ref[...]
    r = lax.rsqrt(jnp.mean(xv * xv, axis=-1, keepdims=True) + RMS_EPS)
    xh = xv * r
    dxh = dh * g_ref[...]
    m = jnp.mean(dxh * xh, axis=-1, keepdims=True)
    dx_ref[...] = res_ref[...] + r * (dxh - xh * m)
    part = jnp.sum(dh * xh, axis=0, keepdims=True)

    @pl.when(first)
    def _():
        dg_ref[...] = part

    @pl.when(jnp.logical_not(first))
    def _():
        dg_ref[...] += part


def _matmul_nt_sum(pairs, name, comm=None, norm_bwd=None, tm=NORM_FUSED_TM, tk=1024):
    M, N = pairs[0][0].shape[0], pairs[0][1].shape[0]
    tm = _tile(M, tm)
    tks = [_tile(a.shape[1], tk) for a, _, _ in pairs]
    steps = [a.shape[1] // t for (a, _, _), t in zip(pairs, tks)]
    offs = [sum(steps[:p]) for p in range(len(pairs))]
    total = sum(steps)

    n_extra = 3 if norm_bwd is not None else 0

    def body(*refs):
        a_refs, b_refs = refs[0:2 * len(pairs):2], refs[1:2 * len(pairs):2]
        extra = refs[2 * len(pairs):2 * len(pairs) + n_extra]
        outs, acc_ref = refs[2 * len(pairs) + n_extra:-1], refs[-1]
        k = pl.program_id(1)
        for p in range(len(pairs)):
            @pl.when((k >= offs[p]) & (k < offs[p] + steps[p]))
            def _(p=p):
                prod = _bdot(a_refs[p][...], b_refs[p][...], NT)
                if p == 0:
                    @pl.when(k == 0)
                    def _():
                        acc_ref[...] = prod

                    @pl.when(k > 0)
                    def _():
                        acc_ref[...] += prod
                else:
                    acc_ref[...] += prod

        @pl.when(k == total - 1)
        def _():
            if norm_bwd is not None:
                _rmsnorm_bwd_tile(acc_ref[...], *extra, outs[0], outs[1], first=pl.program_id(0) == 0)
            else:
                outs[0][...] = acc_ref[...]

    in_specs, args = [], []
    for (a, b, c0), t, off, n in zip(pairs, tks, offs, steps):
        assert c0 % t == 0
        pick = lambda k, off=off, n=n: jnp.clip(k - off, 0, n - 1)
        in_specs += [pl.BlockSpec((tm, t), lambda i, k, pick=pick: (i, pick(k))),
                     pl.BlockSpec((N, t), lambda i, k, pick=pick, cb0=c0 // t: (0, pick(k) + cb0))]
        args += [a, b]
    row, vec = pl.BlockSpec((tm, N), lambda i, k: (i, 0)), pl.BlockSpec((1, N), lambda i, k: (0, 0))
    if norm_bwd is not None:
        in_specs += [row, vec, row]
        args += list(norm_bwd)
        out_specs, out_shape = [row, vec], [jax.ShapeDtypeStruct((M, N), F32), jax.ShapeDtypeStruct((1, N), F32)]
    else:
        out_specs, out_shape = [row], [jax.ShapeDtypeStruct((M, N), F32)]
    outs, landed = _call(body, comm, name=name, grid=(M // tm, total), in_specs=in_specs, out_specs=out_specs,
                         out_shape=out_shape, scratch_shapes=[pltpu.VMEM((tm, N), F32)],
                         semantics=("arbitrary", "arbitrary"), args=tuple(args))
    return (outs if norm_bwd is not None else outs[0]), landed


def _rmsnorm_fwd(x, g, name, comm=None):
    T, D = x.shape
    tt = _tile(T, LONG_ROW_TILE, SUBLANE)

    def body(x_ref, g_ref, o_ref):
        xv = x_ref[...]
        r = lax.rsqrt(jnp.mean(xv * xv, axis=-1, keepdims=True) + RMS_EPS)
        o_ref[...] = (xv * r * g_ref[...]).astype(BF16)

    outs, landed = _call(
        body, comm, name=name, grid=(T // tt,),
        in_specs=[pl.BlockSpec((tt, D), lambda i: (i, 0)), pl.BlockSpec((1, D), lambda i: (0, 0))],
        out_specs=[pl.BlockSpec((tt, D), lambda i: (i, 0))], out_shape=[jax.ShapeDtypeStruct((T, D), BF16)],
        scratch_shapes=[], semantics=("parallel",), args=(x, g))
    return outs[0], landed


def _rmsnorm_bwd(dh, x, g, dx_res, name):
    T, D = x.shape
    tt = _tile(T, LONG_ROW_TILE, SUBLANE)

    def body(dh_ref, x_ref, g_ref, res_ref, dx_ref, dg_ref):
        _rmsnorm_bwd_tile(dh_ref[...], x_ref, g_ref, res_ref, dx_ref, dg_ref, first=pl.program_id(0) == 0)

    row = pl.BlockSpec((tt, D), lambda i: (i, 0))
    vec = pl.BlockSpec((1, D), lambda i: (0, 0))
    return pl.pallas_call(
        body, name=name, grid=(T // tt,),
        in_specs=[row, row, vec, row], out_specs=[row, vec],
        out_shape=[jax.ShapeDtypeStruct((T, D), F32), jax.ShapeDtypeStruct((1, D), F32)],
        compiler_params=_params("arbitrary"),
    )(dh, x, g, dx_res)


def _down(x, k):
    return pltpu.roll(x, k, 0) if k else x


def _up(x, k):
    return pltpu.roll(x, x.shape[0] - k, 0) if k else x


def _sc_fwd(proj, conv_w, name):
    T = proj.shape[0]
    tt = _tile(T, WIDE_ROW_TILE, SUBLANE)
    B = SC_BLK

    def body(p_ref, ph_ref, w_ref, o_ref):
        keep = (pl.program_id(0) > 0).astype(F32)
        for j in range(SC_NBLK):
            cb, cc, cu, cg = (slice(k * SC_W + j * B, k * SC_W + (j + 1) * B) for k in range(4))
            cw = slice(j * B, (j + 1) * B)
            z = jnp.concatenate([ph_ref[:, cc] * ph_ref[:, cu] * keep, p_ref[:, cc] * p_ref[:, cu]], axis=0)
            cz = (w_ref[2:3, cw] * z + w_ref[1:2, cw] * _down(z, 1) + w_ref[0:1, cw] * _down(z, 2))[HALO:]
            gate = p_ref[:, cg]
            o_ref[:, cw] = (p_ref[:, cb] * cz * (gate * _sigmoid(gate))).astype(BF16)

    return pl.pallas_call(
        body, name=name, grid=(T // tt,),
        in_specs=[pl.BlockSpec((tt, 4 * SC_W), lambda i: (i, 0)),
                  pl.BlockSpec((HALO, 4 * SC_W), lambda i: (jnp.maximum(i * (tt // HALO) - 1, 0), 0)),
                  pl.BlockSpec((SC_CONV, SC_W), lambda i: (0, 0))],
        out_specs=pl.BlockSpec((tt, SC_W), lambda i: (i, 0)),
        out_shape=jax.ShapeDtypeStruct((T, SC_W), BF16),
        compiler_params=_params("parallel"),
    )(proj, proj, conv_w)


def _sc_bwd(dyg, proj, conv_w, name):
    T = proj.shape[0]
    tt = _tile(T, WIDE_ROW_TILE, SUBLANE)
    nt = T // tt
    B = SC_BLK
    hb = tt // HALO

    def body(d_ref, dn_ref, p_ref, pp_ref, pn_ref, w_ref, o_ref, dw_ref):
        i = pl.program_id(0)
        keep_p = (i > 0).astype(F32)
        keep_n = (i < nt - 1).astype(F32)
        main = slice(HALO, HALO + tt)
        parts = []
        for j in range(SC_NBLK):
            cw = slice(j * B, (j + 1) * B)

            def ext(k):
                s = slice(k * SC_W + j * B, k * SC_W + (j + 1) * B)
                return s, jnp.concatenate([pp_ref[:, s] * keep_p, p_ref[:, s], pn_ref[:, s]], axis=0)

            (sb, b), (sc, c), (su, u), (sg_, gate) = ext(0), ext(1), ext(2), ext(3)
            dyg_e = jnp.concatenate([jnp.zeros((HALO, B), F32), d_ref[:, cw], dn_ref[:, cw] * keep_n], axis=0)
            w0, w1, w2 = w_ref[0:1, cw], w_ref[1:2, cw], w_ref[2:3, cw]
            z = c * u
            z1, z2 = _down(z, 1), _down(z, 2)
            cz = w2 * z + w1 * z1 + w0 * z2
            sg, dsg = _silu_and_grad(gate)
            dy = dyg_e * sg
            dcz = dy * b
            dz = w2 * dcz + w1 * _up(dcz, 1) + w0 * _up(dcz, 2)
            o_ref[:, sb] = (dy * cz)[main].astype(BF16)
            o_ref[:, sc] = (dz * u)[main].astype(BF16)
            o_ref[:, su] = (dz * c)[main].astype(BF16)
            o_ref[:, sg_] = (dyg_e * (b * cz) * dsg)[main].astype(BF16)
            dcm = dcz[main]
            parts.append(jnp.concatenate([jnp.sum(dcm * z2[main], axis=0, keepdims=True),
                                          jnp.sum(dcm * z1[main], axis=0, keepdims=True),
                                          jnp.sum(dcm * z[main], axis=0, keepdims=True)], axis=0))
        part = jnp.concatenate(parts, axis=1)

        @pl.when(i == 0)
        def _():
            dw_ref[...] = part

        @pl.when(i > 0)
        def _():
            dw_ref[...] += part

    nxt = lambda i: (jnp.minimum((i + 1) * hb, nt * hb - 1), 0)
    return pl.pallas_call(
        body, name=name, grid=(nt,),
        in_specs=[pl.BlockSpec((tt, SC_W), lambda i: (i, 0)),
                  pl.BlockSpec((HALO, SC_W), nxt),
                  pl.BlockSpec((tt, 4 * SC_W), lambda i: (i, 0)),
                  pl.BlockSpec((HALO, 4 * SC_W), lambda i: (jnp.maximum(i * hb - 1, 0), 0)),
                  pl.BlockSpec((HALO, 4 * SC_W), nxt),
                  pl.BlockSpec((SC_CONV, SC_W), lambda i: (0, 0))],
        out_specs=[pl.BlockSpec((tt, 4 * SC_W), lambda i: (i, 0)), pl.BlockSpec((SC_CONV, SC_W), lambda i: (0, 0))],
        out_shape=[jax.ShapeDtypeStruct((T, 4 * SC_W), BF16), jax.ShapeDtypeStruct((SC_CONV, SC_W), F32)],
        compiler_params=_params("arbitrary"),
    )(dyg, dyg, proj, proj, proj, conv_w)


def _split3_dot(x, m):
    hi = x.astype(BF16)
    r1 = x - hi.astype(F32)
    mid = r1.astype(BF16)
    lo = (r1 - mid.astype(F32)).astype(BF16)
    return _dot(hi, m) + _dot(mid, m) + _dot(lo, m)


def _split2_dot(x, m):
    hi = x.astype(BF16)
    lo = (x - hi.astype(F32)).astype(BF16)
    return _dot(hi, m) + _dot(lo, m)


def _head_mean_matrix():
    r, c = _iota2((LANE, LANE), 0), _iota2((LANE, LANE), 1)
    return jnp.where((r // SB_DH) == (c // SB_DH), 1.0 / SB_DH, 0.0).astype(BF16)


def _sb_prep(proj, qg2, kg2, name):
    T = proj.shape[0]
    tt = _tile(T, WIDE_ROW_TILE, SUBLANE)

    def body(p_ref, qg_ref, kg_ref, q_ref, k_ref, v_ref):
        bd = _head_mean_matrix()

        def norm(x, g, scale):
            r = lax.rsqrt(_split3_dot(x * x, bd) + RMS_EPS)
            return (x * r * g * scale).astype(BF16)

        v_ref[...] = p_ref[:, 2 * SB_W:3 * SB_W].astype(BF16)
        for p in range(SB_PAIRS):
            cols = slice(p * LANE, (p + 1) * LANE)
            q_ref[:, cols] = norm(p_ref[:, cols], qg_ref[...], SB_DH ** -0.5)
            k_ref[:, cols] = norm(p_ref[:, SB_W + p * LANE:SB_W + (p + 1) * LANE], kg_ref[...], 1.0)

    blk = pl.BlockSpec((tt, SB_W), lambda i: (i, 0))
    vec = pl.BlockSpec((1, LANE), lambda i: (0, 0))
    return pl.pallas_call(
        body, name=name, grid=(T // tt,),
        in_specs=[pl.BlockSpec((tt, 4 * SB_W), lambda i: (i, 0)), vec, vec],
        out_specs=[blk, blk, blk],
        out_shape=[jax.ShapeDtypeStruct((T, SB_W), BF16)] * 3,
        compiler_params=_params("parallel"),
    )(proj, qg2, kg2)


def _sb_prep_bwd(proj, dqn, dkn, dv, dgate, qg2, kg2, name):
    T = proj.shape[0]
    tt = _tile(T, WIDE_ROW_TILE, SUBLANE)

    def body(p_ref, dq_ref, dk_ref, dv_ref, dg_ref, qg_ref, kg_ref, o_ref, dqg_ref, dkg_ref):
        i = pl.program_id(0)
        bd = _head_mean_matrix()

        def norm_bwd(x, g, dy):
            r = lax.rsqrt(_split3_dot(x * x, bd) + RMS_EPS)
            xh = x * r
            dxh = dy * g
            m = _split3_dot(dxh * xh, bd)
            return r * (dxh - xh * m), jnp.sum(dy * xh, axis=0, keepdims=True)

        o_ref[:, 2 * SB_W:3 * SB_W] = dv_ref[...].astype(BF16)
        o_ref[:, 3 * SB_W:4 * SB_W] = dg_ref[...].astype(BF16)
        pq = pk = jnp.zeros((1, LANE), F32)
        for p in range(SB_PAIRS):
            cols, kcols = slice(p * LANE, (p + 1) * LANE), slice(SB_W + p * LANE, SB_W + (p + 1) * LANE)
            dxq, sq = norm_bwd(p_ref[:, cols], qg_ref[...], dq_ref[:, cols])
            dxk, sk = norm_bwd(p_ref[:, kcols], kg_ref[...], dk_ref[:, cols])
            o_ref[:, cols] = dxq.astype(BF16)
            o_ref[:, kcols] = dxk.astype(BF16)
            pq, pk = pq + sq, pk + sk

        @pl.when(i == 0)
        def _():
            dqg_ref[...] = pq
            dkg_ref[...] = pk

        @pl.when(i > 0)
        def _():
            dqg_ref[...] += pq
            dkg_ref[...] += pk

    blk = pl.BlockSpec((tt, SB_W), lambda i: (i, 0))
    vec = pl.BlockSpec((1, LANE), lambda i: (0, 0))
    wide = pl.BlockSpec((tt, 4 * SB_W), lambda i: (i, 0))
    return pl.pallas_call(
        body, name=name, grid=(T // tt,),
        in_specs=[wide, blk, blk, blk, blk, vec, vec],
        out_specs=[wide, vec, vec],
        out_shape=[jax.ShapeDtypeStruct((T, 4 * SB_W), BF16)] + [jax.ShapeDtypeStruct((1, LANE), F32)] * 2,
        compiler_params=_params("arbitrary"),
    )(proj, dqn, dkn, dv, dgate, qg2, kg2)


def _fold_heads(part, name):
    def body(p_ref, o_ref):
        r, c = _iota2((LANE, SB_DH), 0), _iota2((LANE, SB_DH), 1)
        fold = jnp.where((r % SB_DH) == c, 1.0, 0.0).astype(F32)
        o_ref[...] = jnp.sum(_hdot(p_ref[...], fold), axis=0, keepdims=True)

    return pl.pallas_call(body, name=name, out_shape=jax.ShapeDtypeStruct((1, SB_DH), F32))(part)


def _sb_masks():
    lane = _iota2((1, LANE), 1)
    return lane < SB_DH


def _sb_attn_fwd(qn, kn, vb, proj, name, comm=None):
    T = qn.shape[0]
    tq, tk = _tile(T, SB_TQ, SUBLANE), SB_TK
    assert tq % tk == 0

    def body(q_ref, k_ref, v_ref, g_ref, o_ref, og_ref, lt_ref, done_ref):
        i = pl.program_id(1)
        ma = _sb_masks()
        q2 = q_ref[...]
        zero = jnp.zeros_like(q2)
        qs = (jnp.where(ma, q2, zero), jnp.where(ma, zero, q2))
        upper = (_iota2((tk, tk), 0) > _iota2((tk, tk), 1)).astype(BF16)
        qpos = i * tq + _iota2((tq, tk), 0)
        nb = tq // tk

        def trip(kb_top, masked, carry):
            acc, la, lb = carry
            chains = [(b, h) for b in range(nb) for h in range(2)]
            k2s, vss, masks = [], [], []
            for b in range(nb):
                kb = kb_top - b
                rows = pl.ds(pl.multiple_of(kb * tk, tk), tk)
                k2s.append(k_ref[rows, :])
                v2 = v_ref[rows, :]
                zv = jnp.zeros_like(v2)
                vss.append((jnp.where(ma, v2, zv), jnp.where(ma, zv, v2)))
                masks.append((kb * tk + _iota2((tq, tk), 1)) < qpos if masked else None)
            zs = [_dot(qs[h], k2s[b], NT) for b, h in chains]
            ts = [jnp.log(1.0 + jnp.exp(-jnp.abs(z))) for z in zs]
            ls = [-(jnp.maximum(z, 0.0) + t) for z, t in zip(zs, ts)]
            if masked:
                ls = [jnp.where(masks[b], l, 0.0) for (b, h), l in zip(chains, ls)]
            cums = [_split2_dot(l, upper) for l in ls]
            sums = [jnp.sum(l, axis=1, keepdims=True) for l in ls]
            offs, tot = {}, [la, lb]
            for b in range(nb):
                for h in range(2):
                    offs[(b, h)] = tot[h]
                    tot[h] = tot[h] + sums[chains.index((b, h))]
            ws = [jnp.exp(jnp.minimum(z, 0.0) - t + c + offs[ch]) for ch, z, t, c in zip(chains, zs, ts, cums)]
            if masked:
                ws = [jnp.where(masks[b], w, 0.0) for (b, h), w in zip(chains, ws)]
            for (b, h), w in zip(chains, ws):
                acc = acc + _dot(w.astype(BF16), vss[b][h])
            return acc, tot[0], tot[1]

        def largest(la, lb):
            return jnp.max(jnp.maximum(la, lb))

        z1 = jnp.zeros((tq, 1), F32)
        acc, la, lb = trip((i + 1) * nb - 1, True, (jnp.zeros((tq, LANE), F32), z1, z1))

        def live(c):
            return (c[0] < i) & (c[4] > SB_DEAD)

        def more(c):
            j, acc, la, lb, _ = c
            acc, la, lb = trip((i - j) * nb - 1, False, (acc, la, lb))
            return j + 1, acc, la, lb, largest(la, lb)

        done, acc, la, lb, _ = lax.while_loop(live, more, (jnp.int32(0), acc, la, lb, largest(la, lb)))
        gate = g_ref[...]
        o_ref[...] = acc
        og_ref[...] = (acc * (gate * _sigmoid(gate))).astype(BF16)
        lt_ref[...] = jnp.where(_iota2((tq, 2), 1) == 0, la, lb)
        done_ref[...] = jnp.full((SUBLANE, LANE), done, F32)

    nq = T // tq
    qblk = pl.BlockSpec((tq, LANE), lambda p, i: (i, p))
    full = pl.BlockSpec((T, LANE), lambda p, i: (0, p))
    return _call(
        body, comm, name=name, grid=(SB_PAIRS, nq),
        in_specs=[qblk, full, full, pl.BlockSpec((tq, LANE), lambda p, i: (i, 3 * SB_PAIRS + p))],
        out_specs=[qblk, qblk, pl.BlockSpec((None, tq, 2), lambda p, i: (p, i, 0)),
                   pl.BlockSpec((None, None, SUBLANE, LANE), lambda p, i: (p, i, 0, 0))],
        out_shape=[jax.ShapeDtypeStruct((T, SB_W), F32), jax.ShapeDtypeStruct((T, SB_W), BF16),
                   jax.ShapeDtypeStruct((SB_PAIRS, T, 2), F32), jax.ShapeDtypeStruct((SB_PAIRS, nq, SUBLANE, LANE), F32)],
        scratch_shapes=[], semantics=("parallel", "parallel"), args=(qn, kn, vb, proj))


def _sb_attn_bwd(qn, kn, vb, dog, o, ltot, done, proj, name, comm=None):
    T = qn.shape[0]
    tq, tk = _tile(T, SB_TQ, SUBLANE), SB_TK

    def body(q_ref, k_ref, v_ref, dog_ref, o_ref, lt_ref, done_ref, g_ref, dq_ref, dk_ref, dv_ref, dgate_ref):
        i = pl.program_id(1)
        first_trip = i - jnp.max(done_ref[...]).astype(jnp.int32)

        @pl.when(i == 0)
        def _():
            dk_ref[...] = jnp.zeros_like(dk_ref)
            dv_ref[...] = jnp.zeros_like(dv_ref)

        ma = _sb_masks()
        gate, o2, dog2 = g_ref[...], o_ref[...], dog_ref[...]
        sg, dsg = _silu_and_grad(gate)
        do2 = dog2 * sg
        dgate_ref[...] = dog2 * o2 * dsg
        lt = lt_ref[...]
        first = _iota2((tq, 2), 1) == 0
        ltots = (jnp.sum(jnp.where(first, lt, 0.0), axis=1, keepdims=True),
                 jnp.sum(jnp.where(first, 0.0, lt), axis=1, keepdims=True))
        q2 = q_ref[...]
        zq = jnp.zeros_like(q2)
        qs = (jnp.where(ma, q2, zq), jnp.where(ma, zq, q2))
        dob = do2.astype(BF16)
        dos = (jnp.where(ma, dob, zq), jnp.where(ma, zq, dob))
        upto = (_iota2((tk, tk), 0) <= _iota2((tk, tk), 1)).astype(BF16)
        before = (_iota2((tk, tk), 0) < _iota2((tk, tk), 1)).astype(BF16)
        qpos = i * tq + _iota2((tq, tk), 0)
        nb = tq // tk

        def trip(kb_bot, masked, carry):
            dq, la, lb, ea, eb = carry
            chains = [(b, h) for b in range(nb) for h in range(2)]
            rows, k2s, v2s, kss, masks = [], [], [], [], []
            for b in range(nb):
                kb = kb_bot + b
                rows.append(pl.ds(pl.multiple_of(kb * tk, tk), tk))
                k2 = k_ref[rows[b], :]
                zk = jnp.zeros_like(k2)
                k2s.append(k2)
                v2s.append(v_ref[rows[b], :])
                kss.append((jnp.where(ma, k2, zk), jnp.where(ma, zk, k2)))
                masks.append((kb * tk + _iota2((tq, tk), 1)) < qpos if masked else None)

            def keep(vals):
                return [jnp.where(masks[b], x, 0.0) for (b, h), x in zip(chains, vals)] if masked else vals

            zs = [_dot(qs[h], k2s[b], NT) for b, h in chains]
            dws = [_dot(dos[h], v2s[b], NT) for b, h in chains]
            ts = [jnp.log(1.0 + jnp.exp(-jnp.abs(z))) for z in zs]
            ls = keep([-(jnp.maximum(z, 0.0) + t) for z, t in zip(zs, ts)])
            lps = [jnp.minimum(z, 0.0) - t for z, t in zip(zs, ts)]
            cums = [_split3_dot(l, upto) for l in ls]
            lsums = [jnp.sum(l, axis=1, keepdims=True) for l in ls]
            offs, tot = {}, [la, lb]
            for b in range(nb):
                for h in range(2):
                    offs[(b, h)] = tot[h]
                    tot[h] = tot[h] + lsums[chains.index((b, h))]
            ws = keep([jnp.exp(lp + (ltots[h] - (offs[(b, h)] + c))) for (b, h), lp, c in zip(chains, lps, cums)])
            es = [dw * w for dw, w in zip(dws, ws)]
            ecums = [_split2_dot(e, before) for e in es]
            esums = [jnp.sum(e, axis=1, keepdims=True) for e in es]
            eoffs, etot = {}, [ea, eb]
            for b in range(nb):
                for h in range(2):
                    eoffs[(b, h)] = etot[h]
                    etot[h] = etot[h] + esums[chains.index((b, h))]
            dzs = keep([e - jnp.exp(lp) * (e + eoffs[ch] + ec) for ch, e, lp, ec in zip(chains, es, lps, ecums)])
            dzs = [dz.astype(BF16) for dz in dzs]
            wbs = [w.astype(BF16) for w in ws]
            for (b, h), dz in zip(chains, dzs):
                dq = dq + _dot(dz, kss[b][h])
            for b in range(nb):
                ia, ib = chains.index((b, 0)), chains.index((b, 1))
                dk_ref[rows[b], :] += _dot(dzs[ia], qs[0], TN) + _dot(dzs[ib], qs[1], TN)
                dv_ref[rows[b], :] += _dot(wbs[ia], dos[0], TN) + _dot(wbs[ib], dos[1], TN)
            return dq, tot[0], tot[1], etot[0], etot[1]

        z1 = jnp.zeros((tq, 1), F32)
        carry = lax.fori_loop(first_trip, i, lambda j, c: trip(j * nb, False, c),
                              (jnp.zeros((tq, LANE), F32), z1, z1, z1, z1))
        dq = trip(i * nb, True, carry)[0]
        dq_ref[...] = dq * (SB_DH ** -0.5)

    qblk = pl.BlockSpec((tq, LANE), lambda p, i: (i, p))
    full = pl.BlockSpec((T, LANE), lambda p, i: (0, p))
    return _call(
        body, comm, name=name, grid=(SB_PAIRS, T // tq),
        in_specs=[qblk, full, full, qblk, qblk, pl.BlockSpec((None, tq, 2), lambda p, i: (p, i, 0)),
                  pl.BlockSpec((None, None, SUBLANE, LANE), lambda p, i: (p, i, 0, 0)),
                  pl.BlockSpec((tq, LANE), lambda p, i: (i, 3 * SB_PAIRS + p))],
        out_specs=[qblk, full, full, qblk],
        out_shape=[jax.ShapeDtypeStruct((T, SB_W), F32)] * 4,
        scratch_shapes=[], semantics=("parallel", "arbitrary"), args=(qn, kn, vb, dog, o, ltot, done, proj))


def _dn_conv(ext, w_ref, cw):
    return (w_ref[3:4, cw] * ext + w_ref[2:3, cw] * _down(ext, 1) + w_ref[1:2, cw] * _down(ext, 2)
            + w_ref[0:1, cw] * _down(ext, 3))


DN_GATE_BLK = 4 * DN_CHUNK
DN_GATE_UNROLL = 4


def _chunk_block_diagonal(keep):
    r, c = _iota2((DN_GATE_BLK, DN_GATE_BLK), 0), _iota2((DN_GATE_BLK, DN_GATE_BLK), 1)
    return (keep(r, c) & (r // DN_CHUNK == c // DN_CHUNK)).astype(F32)


def _dn_gates(a_in, b_in, a_log, dt_bias, name):
    T, H = a_in.shape
    C = DN_CHUNK

    def body(a_ref, b_ref, al_ref, dt_ref, g_ref, beta_ref):
        beta_ref[...] = _sigmoid(b_ref[...])
        g_ref[...] = -jnp.exp(al_ref[...]) * _softplus(a_ref[...] + dt_ref[...])
        tri = _chunk_block_diagonal(lambda r, c: r >= c)

        def block(n, carry):
            rows = pl.ds(pl.multiple_of(n * DN_GATE_BLK, DN_GATE_BLK), DN_GATE_BLK)
            g_ref[rows, :] = _hdot(tri, g_ref[rows, :])
            return carry

        lax.fori_loop(0, T // DN_GATE_BLK, block, 0, unroll=DN_GATE_UNROLL)

    assert T % (DN_GATE_BLK * DN_GATE_UNROLL) == 0
    return pl.pallas_call(body, name=name, out_shape=[jax.ShapeDtypeStruct((T, H), F32)] * 2)(a_in, b_in, a_log, dt_bias)


def _dn_gates_bwd(dg, dbeta, a_in, b_in, a_log, dt_bias, name):
    T, H = a_in.shape
    C = DN_CHUNK

    def body(dg_ref, db_ref, a_ref, b_ref, al_ref, dt_ref, da_ref, dbi_ref, dal_ref, ddt_ref):
        tri_t = _chunk_block_diagonal(lambda r, c: r <= c)

        def block(n, carry):
            rows = pl.ds(pl.multiple_of(n * DN_GATE_BLK, DN_GATE_BLK), DN_GATE_BLK)
            da_ref[rows, :] = _hdot(tri_t, dg_ref[rows, :])
            return carry

        lax.fori_loop(0, T // DN_GATE_BLK, block, 0, unroll=DN_GATE_UNROLL)
        dla = da_ref[...]
        x = a_ref[...] + dt_ref[...]
        ea = jnp.exp(al_ref[...])
        da = dla * (-ea) * _sigmoid(x)
        da_ref[...] = da
        dal_ref[...] = jnp.sum(dla * (-ea * _softplus(x)), axis=0, keepdims=True)
        ddt_ref[...] = jnp.sum(da, axis=0, keepdims=True)
        beta = _sigmoid(b_ref[...])
        dbi_ref[...] = db_ref[...] * beta * (1.0 - beta)

    return pl.pallas_call(
        body, name=name,
        out_shape=[jax.ShapeDtypeStruct((T, H), F32)] * 2 + [jax.ShapeDtypeStruct((1, H), F32)] * 2,
    )(dg, dbeta, a_in, b_in, a_log, dt_bias)


def _dn_chunk_terms(q, k, gc, bc):
    C = DN_CHUNK
    r, c = _iota2((C, C), 0), _iota2((C, C), 1)
    lower, strict, eye = r >= c, r > c, r == c
    grow = jnp.sum(jnp.where(eye, gc, 0.0), axis=0, keepdims=True)
    decay = jnp.where(lower, jnp.exp(jnp.where(lower, gc - grow, 0.0)), 0.0)
    last = _iota2((C, 1), 0) == C - 1
    gl = jnp.sum(jnp.where(last, gc, 0.0), axis=0, keepdims=True)
    eg = jnp.exp(gc)
    egl = jnp.exp(gl - gc)
    kb = k * bc
    lmat = jnp.where(strict, _bdot(kb, k, NT) * decay, 0.0)
    aqk = jnp.where(lower, _bdot(q, k, NT) * decay, 0.0)
    return dict(lower=lower, strict=strict, eye=eye, last=last, decay=decay, gl=gl, eg=eg, egl=egl, kb=kb,
                lmat=lmat, aqk=aqk, qd=q * eg, kd=k * egl)


def _split(x):
    hi = x.astype(BF16)
    return hi, (x - hi.astype(F32)).astype(BF16)


def _x3dot(a, b, dims=NN):
    ah, al = a if isinstance(a, tuple) else _split(a)
    bh, bl = b if isinstance(b, tuple) else _split(b)
    return _dot(ah, bh, dims) + (_dot(ah, bl, dims) + _dot(al, bh, dims))


def _interleave(gens):
    for _ in itertools.zip_longest(*gens):
        pass


def _unit_lower_inverse_steps(lmat, eye, out):
    ident = jnp.where(eye, 1.0, 0.0).astype(F32)
    m = -lmat
    inv = ident + m
    for _ in range(int(math.log2(DN_CHUNK)) - 1):
        ms = _split(m)
        m = _x3dot(ms, ms)
        yield
        inv = inv + _x3dot(inv, m)
        yield
    out["tm"] = inv


def _dn_chunk_fwd(pqkv, conv_w, g, beta, pgate, gn, name, comm=None):
    T = pqkv.shape[0]
    C, H = DN_CHUNK, DN_HEADS
    N = T // C
    B = DN_PREP_BLK
    nq, nqk = DN_QK_W // B, 2 * DN_QK_W // B

    def step(p_ref, cw_ref, g_ref, b_ref, pg_ref, gn_ref, act_out, o_ref, og_ref, s_out, t_out, vn_out, u_out, w_out,
             s_scr, tail_scr, a_ref, a_next):
        head_lane = _iota2((C, H), 1)

        def prepare(cb):
            cw = slice(cb * B, (cb + 1) * B)
            ext = jnp.concatenate([tail_scr[:, cw], p_ref[:, cw]], axis=0)
            c = _dn_conv(ext, cw_ref, cw)[HALO:]
            yield
            a = c * _sigmoid(c)
            if cb >= nqk:
                a_next[:, cw] = a
                act_out[:, cw] = a
                return
            scale = DN_DK ** -0.5 if cb < nq else 1.0
            for hh in range(B // DN_DK):
                yield
                ah = a[:, hh * DN_DK:(hh + 1) * DN_DK]
                val = ah * (lax.rsqrt(jnp.sum(ah * ah, axis=-1, keepdims=True) + L2_EPS) * scale)
                cols = slice(cb * B + hh * DN_DK, cb * B + (hh + 1) * DN_DK)
                a_next[:, cols] = val
                act_out[:, cols] = val

        def head(hh):
            qs, vs = slice(hh * DN_DK, (hh + 1) * DN_DK), slice(hh * DN_DV, (hh + 1) * DN_DV)
            q, k, v = a_ref[:, qs], a_ref[:, DN_QK_W + hh * DN_DK:DN_QK_W + (hh + 1) * DN_DK], \
                a_ref[:, 2 * DN_QK_W + hh * DN_DV:2 * DN_QK_W + (hh + 1) * DN_DV]
            gc = jnp.sum(jnp.where(head_lane == hh, g_ref[...], 0.0), axis=1, keepdims=True)
            bc = jnp.sum(jnp.where(head_lane == hh, b_ref[...], 0.0), axis=1, keepdims=True)
            t = _dn_chunk_terms(q, k, gc, bc)
            yield
            res = {}
            yield from _unit_lower_inverse_steps(t["lmat"], t["eye"], res)
            tms = _split(res["tm"])
            u = _x3dot(tms, v * bc)
            yield
            w = _x3dot(tms, t["kb"] * t["eg"])
            yield
            s = s_scr[hh]
            s_out[hh] = s
            t_out[hh] = res["tm"]
            sb = s.astype(BF16)
            vn = u - _dot(w.astype(BF16), sb)
            yield
            o = _dot(t["qd"].astype(BF16), sb) + _bdot(t["aqk"], vn)
            yield
            s_scr[hh] = s * jnp.exp(t["gl"]) + _bdot(t["kd"], vn, TN)
            vn_out[:, vs] = vn
            u_out[:, vs] = u
            w_out[:, qs] = w
            o_ref[:, vs] = o
            gate = pg_ref[:, vs]
            r = lax.rsqrt(jnp.mean(o * o, axis=-1, keepdims=True) + RMS_EPS)
            og_ref[:, vs] = (o * r * gn_ref[...] * (gate * _sigmoid(gate))).astype(BF16)

        _interleave([prepare(cb) for cb in range(DN_CONV_W // B)] + [head(hh) for hh in range(H)])

        @pl.when(pl.program_id(0) < N - 1)
        def _():
            tail_scr[...] = p_ref[C - HALO:C, :]

    def body(*refs):
        s = pl.program_id(0)
        io, (s_scr, tail_scr, buf_a, buf_b) = refs[:-4], refs[-4:]

        @pl.when(s == 0)
        def _():
            tail_scr[...] = jnp.zeros_like(tail_scr)
            buf_b[...] = jnp.zeros_like(buf_b)

        @pl.when(s <= 1)
        def _():
            s_scr[...] = jnp.zeros_like(s_scr)

        @pl.when(s % 2 == 0)
        def _():
            step(*io, s_scr, tail_scr, buf_b, buf_a)

        @pl.when(s % 2 == 1)
        def _():
            step(*io, s_scr, tail_scr, buf_a, buf_b)

    nxt = lambda w: pl.BlockSpec((C, w), lambda s: (jnp.minimum(s, N - 1), 0))
    cur = lambda w: pl.BlockSpec((C, w), lambda s: (jnp.maximum(s - 1, 0), 0))
    per_chunk = lambda a, b: pl.BlockSpec((H, None, a, b), lambda s: (0, jnp.maximum(s - 1, 0), 0, 0))
    return _call(
        body, comm, name=name, grid=(N + 1,),
        in_specs=[nxt(DN_CONV_W), pl.BlockSpec((DN_CONV, DN_CONV_W), lambda s: (0, 0)), cur(H), cur(H), cur(DN_V_W),
                  pl.BlockSpec((1, DN_DV), lambda s: (0, 0))],
        out_specs=[nxt(DN_CONV_W), cur(DN_V_W), cur(DN_V_W), per_chunk(DN_DK, DN_DV), per_chunk(C, C),
                   cur(DN_V_W), cur(DN_V_W), cur(DN_QK_W)],
        out_shape=[jax.ShapeDtypeStruct((T, DN_CONV_W), F32),
                   jax.ShapeDtypeStruct((T, DN_V_W), F32), jax.ShapeDtypeStruct((T, DN_V_W), BF16),
                   jax.ShapeDtypeStruct((H, N, DN_DK, DN_DV), F32),
                   jax.ShapeDtypeStruct((H, N, C, C), F32),
                   jax.ShapeDtypeStruct((T, DN_V_W), F32),
                   jax.ShapeDtypeStruct((T, DN_V_W), F32),
                   jax.ShapeDtypeStruct((T, DN_QK_W), F32)],
        scratch_shapes=[pltpu.VMEM((H, DN_DK, DN_DV), F32), pltpu.VMEM((HALO, DN_CONV_W), F32),
                        pltpu.VMEM((C, DN_CONV_W), F32), pltpu.VMEM((C, DN_CONV_W), F32)],
        semantics=("arbitrary",), args=(pqkv, conv_w, g, beta, pgate, gn))


def _dn_chunk_bwd(pqkv, conv_w, act, g, beta, s_saved, tm_saved, vn_saved, u_saved, w_saved, dog, o_raw, pgate, gn,
                  name, comm=None):
    T = act.shape[0]
    C, H = DN_CHUNK, DN_HEADS
    N = T // C
    assert N % 2 == 0
    B = DN_PREP_BLK
    nq, nqk = DN_QK_W // B, 2 * DN_QK_W // B
    main = slice(HALO, HALO + C)

    def prepare_bwd(cb, p_ref, pp_ref, pn_ref, cw_ref, dread, dnext_scr, dp_ref, conv_parts):
        s = pl.program_id(0)
        keep_p = (N - s > 0).astype(F32)
        keep_n = (s > 1).astype(F32)
        cw = slice(cb * B, (cb + 1) * B)
        ext = jnp.concatenate([pp_ref[:, cw] * keep_p, p_ref[:, cw], pn_ref[:, cw]], axis=0)
        c = _dn_conv(ext, cw_ref, cw)
        yield
        sg = _sigmoid(c)
        da_dc = sg * (1.0 + c * (1.0 - sg))
        d_up = jnp.concatenate([jnp.zeros((HALO, B), F32), dread[:, cw], dnext_scr[:, cw] * keep_n], axis=0)
        if cb < nqk:
            a = c * sg
            scale = DN_DK ** -0.5 if cb < nq else 1.0
            normed = []
            for hh in range(B // DN_DK):
                yield
                cols = slice(hh * DN_DK, (hh + 1) * DN_DK)
                ah = a[:, cols]
                r = lax.rsqrt(jnp.sum(ah * ah, axis=-1, keepdims=True) + L2_EPS)
                y = ah * r
                dy = d_up[:, cols] * scale
                normed.append(r * (dy - y * jnp.sum(dy * y, axis=-1, keepdims=True)))
            d_up = jnp.concatenate(normed, axis=1)
        yield
        dc = d_up * da_dc
        dp = (cw_ref[3:4, cw] * dc + cw_ref[2:3, cw] * _up(dc, 1) + cw_ref[1:2, cw] * _up(dc, 2)
              + cw_ref[0:1, cw] * _up(dc, 3))
        dp_ref[:, cw] = dp[main].astype(BF16)
        yield
        dcm = dc[main]
        conv_parts[cb] = jnp.concatenate([jnp.sum(dcm * _down(ext, 3 - k)[main], axis=0, keepdims=True)
                                          for k in range(DN_CONV)], axis=0)

    def finish_prepare(conv_parts, dconv_ref, dread, dnext_scr):
        part = jnp.concatenate([conv_parts[cb] for cb in range(DN_CONV_W // B)], axis=1)

        @pl.when(pl.program_id(0) == 0)
        def _():
            dconv_ref[...] = part

        @pl.when(pl.program_id(0) > 0)
        def _():
            dconv_ref[...] += part

        dnext_scr[...] = dread[0:HALO, :]

    def step(a_ref, g_ref, b_ref, s_ref, t_ref, vn_ref, u_ref, w_ref, dog_ref, o_ref, pg_ref, gn_ref,
             p_ref, pp_ref, pn_ref, cw_ref, dp_ref, dconv_ref, dg_ref, db_ref, dgate_ref, dgn_ref,
             ds_scr, dnext_scr, dwrite, dread):
        head_lane = _iota2((C, H), 1)
        dg_cols, db_cols, dgn_parts, conv_parts = {}, {}, {}, {}

        def output_gate_bwd(hh, vs):
            d, o, gate, gn_v = dog_ref[:, vs], o_ref[:, vs], pg_ref[:, vs], gn_ref[...]
            sg, dsg = _silu_and_grad(gate)
            r = lax.rsqrt(jnp.mean(o * o, axis=-1, keepdims=True) + RMS_EPS)
            n = o * r
            dy = d * sg
            dgate_ref[:, vs] = (d * (n * gn_v) * dsg).astype(BF16)
            dn = dy * gn_v
            dgn_parts[hh] = jnp.sum(dy * n, axis=0, keepdims=True)
            return r * (dn - n * jnp.mean(dn * n, axis=-1, keepdims=True))

        def head(hh):
            qs, vs = slice(hh * DN_DK, (hh + 1) * DN_DK), slice(hh * DN_DV, (hh + 1) * DN_DV)
            ks = slice(DN_QK_W + hh * DN_DK, DN_QK_W + (hh + 1) * DN_DK)
            vas = slice(2 * DN_QK_W + hh * DN_DV, 2 * DN_QK_W + (hh + 1) * DN_DV)
            q, k, v = a_ref[:, qs], a_ref[:, ks], a_ref[:, vas]
            gc = jnp.sum(jnp.where(head_lane == hh, g_ref[...], 0.0), axis=1, keepdims=True)
            bc = jnp.sum(jnp.where(head_lane == hh, b_ref[...], 0.0), axis=1, keepdims=True)
            t = _dn_chunk_terms(q, k, gc, bc)
            yield
            lower, strict, eye = t["lower"], t["strict"], t["eye"]
            decay, eg, egl, kb, qd, kd = t["decay"], t["eg"], t["egl"], t["kb"], t["qd"], t["kd"]
            s, tm, vn, u, w = s_ref[hh], t_ref[hh], vn_ref[:, vs], u_ref[:, vs], w_ref[:, qs]
            d_o = output_gate_bwd(hh, vs)
            ds_next = ds_scr[hh]
            egl_tot = jnp.exp(t["gl"])
            dob, sb, dsb, vnb = d_o.astype(BF16), s.astype(BF16), ds_next.astype(BF16), vn.astype(BF16)

            dvn = _bdot(t["aqk"], dob, TN) + _bdot(kd, dsb)
            yield
            daqk = jnp.where(lower, _dot(dob, vnb, NT), 0.0)
            dqd = _dot(dob, sb, NT)
            dkd = _dot(vnb, dsb, NT)
            yield
            dvnb = dvn.astype(BF16)
            ds_scr[hh] = _bdot(qd, dob, TN) + egl_tot * ds_next - _bdot(w, dvnb, TN)
            dgl = egl_tot * jnp.sum(jnp.sum(s * ds_next, axis=1, keepdims=True), axis=0, keepdims=True)
            dw = -_dot(dvnb, sb, NT)
            yield
            tms = _split(tm)
            dru = _x3dot(tms, dvn, TN)
            drw = _x3dot(tms, dw, TN)
            yield
            dl = -jnp.where(strict, _x3dot(dru, u, NT) + _x3dot(drw, w, NT), 0.0)
            yield
            dkk = (dl * decay).astype(BF16)
            dqk = (daqk * decay).astype(BF16)
            dkb = _bdot(dkk, k) + drw * eg
            yield
            dwrite[:, ks] = _bdot(dkk, kb, TN) + _bdot(dqk, q, TN) + dkd * egl + dkb * bc
            dwrite[:, qs] = _bdot(dqk, k) + dqd * eg
            dwrite[:, vas] = dru * bc
            yield
            db_cols[hh] = jnp.sum(dru * v, axis=1, keepdims=True) + jnp.sum(dkb * k, axis=1, keepdims=True)
            pm = dl * t["lmat"] + daqk * t["aqk"]
            col_as_col = jnp.sum(jnp.where(eye, jnp.sum(pm, axis=0, keepdims=True), 0.0), axis=1, keepdims=True)
            kdsum = jnp.sum(dkd * kd, axis=1, keepdims=True)
            dgc = (jnp.sum(pm, axis=1, keepdims=True) - col_as_col + jnp.sum(dqd * qd, axis=1, keepdims=True)
                   - kdsum + jnp.sum(drw * (kb * eg), axis=1, keepdims=True))
            dgl = dgl + jnp.sum(kdsum, axis=0, keepdims=True)
            dg_cols[hh] = dgc + jnp.where(t["last"], dgl, 0.0)

        _interleave([head(hh) for hh in range(H)]
                    + [prepare_bwd(cb, p_ref, pp_ref, pn_ref, cw_ref, dread, dnext_scr, dp_ref, conv_parts)
                       for cb in range(DN_CONV_W // B)])
        dg_ref[...] = sum(jnp.where(head_lane == hh, dg_cols[hh], 0.0) for hh in range(H))
        db_ref[...] = sum(jnp.where(head_lane == hh, db_cols[hh], 0.0) for hh in range(H))
        dgn_part = sum(dgn_parts[hh] for hh in range(H))

        @pl.when(pl.program_id(0) == 0)
        def _():
            dgn_ref[...] = dgn_part

        @pl.when(pl.program_id(0) > 0)
        def _():
            dgn_ref[...] += dgn_part

        finish_prepare(conv_parts, dconv_ref, dread, dnext_scr)

    def body(*refs):
        s = pl.program_id(0)
        io, (ds_scr, dnext_scr, buf_a, buf_b) = refs[:-4], refs[-4:]
        p_ref, pp_ref, pn_ref, cw_ref, dp_ref, dconv_ref = refs[12:18]

        @pl.when(s == 0)
        def _():
            ds_scr[...] = jnp.zeros_like(ds_scr)
            dnext_scr[...] = jnp.zeros_like(dnext_scr)
            buf_b[...] = jnp.zeros_like(buf_b)

        @pl.when((s < N) & (s % 2 == 0))
        def _():
            step(*io, ds_scr, dnext_scr, buf_a, buf_b)

        @pl.when((s < N) & (s % 2 == 1))
        def _():
            step(*io, ds_scr, dnext_scr, buf_b, buf_a)

        @pl.when(s == N)
        def _():
            conv_parts = {}
            _interleave([prepare_bwd(cb, p_ref, pp_ref, pn_ref, cw_ref, buf_b, dnext_scr, dp_ref, conv_parts)
                         for cb in range(DN_CONV_W // B)])
            finish_prepare(conv_parts, dconv_ref, buf_b, dnext_scr)

    cc = lambda s: jnp.maximum(N - 1 - s, 0)
    pc = lambda s: jnp.clip(N - s, 0, N - 1)
    row = lambda w: pl.BlockSpec((C, w), lambda s: (cc(s), 0))
    per_chunk = lambda a, b: pl.BlockSpec((H, None, a, b), lambda s: (0, cc(s), 0, 0))
    vec = pl.BlockSpec((1, DN_DV), lambda s: (0, 0))
    per_c = C // HALO
    conv_spec = pl.BlockSpec((DN_CONV, DN_CONV_W), lambda s: (0, 0))
    return _call(
        body, comm, name=name, grid=(N + 1,),
        in_specs=[row(DN_CONV_W), row(H), row(H), per_chunk(DN_DK, DN_DV), per_chunk(C, C),
                  row(DN_V_W), row(DN_V_W), row(DN_QK_W), row(DN_V_W), row(DN_V_W), row(DN_V_W), vec,
                  pl.BlockSpec((C, DN_CONV_W), lambda s: (pc(s), 0)),
                  pl.BlockSpec((HALO, DN_CONV_W), lambda s: (jnp.maximum(pc(s) * per_c - 1, 0), 0)),
                  pl.BlockSpec((HALO, DN_CONV_W), lambda s: (jnp.minimum((pc(s) + 1) * per_c, N * per_c - 1), 0)),
                  conv_spec],
        out_specs=[pl.BlockSpec((C, DN_CONV_W), lambda s: (pc(s), 0)), conv_spec, row(H), row(H), row(DN_V_W), vec],
        out_shape=[jax.ShapeDtypeStruct((T, DN_CONV_W), BF16), jax.ShapeDtypeStruct((DN_CONV, DN_CONV_W), F32),
                   jax.ShapeDtypeStruct((T, H), F32), jax.ShapeDtypeStruct((T, H), F32),
                   jax.ShapeDtypeStruct((T, DN_V_W), BF16), jax.ShapeDtypeStruct((1, DN_DV), F32)],
        scratch_shapes=[pltpu.VMEM((H, DN_DK, DN_DV), F32), pltpu.VMEM((HALO, DN_CONV_W), F32),
                        pltpu.VMEM((C, DN_CONV_W), F32), pltpu.VMEM((C, DN_CONV_W), F32)],
        semantics=("arbitrary",),
        args=(act, g, beta, s_saved, tm_saved, vn_saved, u_saved, w_saved, dog, o_raw, pgate, gn,
              pqkv, pqkv, pqkv, conv_w))


def _dn_split_w_in(w):
    return w, jnp.pad(w[:, DN_CONV_W + DN_V_W:], ((0, 0), (0, DN_AB_PAD - 2 * DN_HEADS)))


def _out_proj(og, w_out, x_res, next_g, name, loss_target=None):
    if next_g is None:
        assert loss_target is not None
        return tuple(_matmul(og, w_out, "nn", name, add=x_res, loss_target=loss_target, tm=LONG_ROW_TILE))
    return tuple(_matmul(og, w_out, "nn", name, add=x_res, norm_fwd=next_g, tm=NORM_FUSED_TM))


def _dn_layer_fwd(h, wts, conv_w, a_log, dt_bias, gn, w_out, x_res, tag, comm=None, next_g=None, loss_target=None):
    w_in, wab = wts
    H = DN_HEADS
    pqkv = _matmul(h, w_in, "nn", tag + "_pqkv", b_cols=(0, DN_CONV_W))
    pgate = _matmul(h, w_in, "nn", tag + "_pgate", b_cols=(DN_CONV_W, DN_V_W))
    pab = _matmul(h, wab, "nn", tag + "_pab")
    a_in, b_in = pab[:, :H], pab[:, H:2 * H]
    g, beta = _dn_gates(a_in, b_in, a_log, dt_bias, tag + "_gates")
    (act, o_raw, og, s_sv, tm_sv, vn_sv, u_sv, w_sv), landed = _dn_chunk_fwd(pqkv, conv_w, g, beta, pgate, gn,
                                                                             tag + "_chunk_fwd", comm)
    if callable(w_out):
        w_out = w_out(landed)
    y = _out_proj(og, w_out, x_res, next_g, tag + "_out", loss_target)
    saved = dict(h=h, wts=wts, conv_w=conv_w, a_log=a_log, dt_bias=dt_bias, gn=gn, w_out=w_out, pqkv=pqkv, pgate=pgate,
                 a_in=a_in, b_in=b_in, g=g, beta=beta, act=act, o_raw=o_raw, chunk=(s_sv, tm_sv, vn_sv, u_sv, w_sv), og=og)
    return y, saved, landed


def _dn_layer_bwd(dout, sv, tag, norm, comm_of=None, late_comm_of=None):
    w_in, wab = sv["wts"]
    h = sv["h"]
    dog = _matmul(dout, sv["w_out"], "nt", tag + "_dog")
    dw_out = _matmul(sv["og"], dout, "tn", tag + "_dwout", out_dtype=BF16)
    comm = comm_of(dw_out) if comm_of is not None else None
    (dpqkv, dconv, dg, dbeta, dgate, dgn), landed = _dn_chunk_bwd(
        sv["pqkv"], sv["conv_w"], sv["act"], sv["g"], sv["beta"], *sv["chunk"], dog, sv["o_raw"], sv["pgate"], sv["gn"],
        tag + "_chunk_bwd", comm)
    da_in, db_in, da_log, ddt = _dn_gates_bwd(dg, dbeta, sv["a_in"], sv["b_in"], sv["a_log"], sv["dt_bias"],
                                              tag + "_gates_bwd")
    dpab = jnp.pad(jnp.concatenate([da_in, db_in], axis=1), ((0, 0), (0, DN_AB_PAD - 2 * DN_HEADS)))
    dwqkv = _matmul(h, dpqkv, "tn", tag + "_dwqkv", out_dtype=BF16)
    dwgate = _matmul(h, dgate, "tn", tag + "_dwgate", out_dtype=BF16)
    dwab = _matmul(h, dpab, "tn", tag + "_dwab", out_dtype=BF16)
    dw_in = jnp.concatenate([dwqkv, dwgate, dwab[:, :2 * DN_HEADS]], axis=1)
    grads = dict(dn_w_in=dw_in, dn_conv_w=dconv, dn_a_log=da_log, dn_dt_bias=ddt, dn_o_norm_g=dgn, dn_w_out=dw_out)
    dx, landed_late = _matmul_nt_sum([(dpqkv, w_in, 0), (dgate, w_in, DN_CONV_W), (dpab, wab, 0)], tag + "_dh",
                                     late_comm_of(grads) if late_comm_of is not None else None, norm_bwd=norm,
                                     tm=NORM_FUSED_TM if norm is not None else 1024)
    return dx, grads, landed, landed_late


def _sb_layer_fwd(h, w_in, qg, kg, w_out, x_res, tag, comm=None, next_g=None):
    qg2, kg2 = jnp.tile(qg, (1, 2)), jnp.tile(kg, (1, 2))
    proj = _matmul(h, w_in, "nn", tag + "_proj", blocked_b=True)
    qn, kn, vb = _sb_prep(proj, qg2, kg2, tag + "_prep")
    (o, og, ltot, done), landed = _sb_attn_fwd(qn, kn, vb, proj, tag + "_attn_fwd", comm)
    y = _out_proj(og, w_out, x_res, next_g, tag + "_out")
    saved = dict(h=h, w_in=w_in, qg2=qg2, kg2=kg2, w_out=w_out, proj=proj, qn=qn, kn=kn, vb=vb, o=o, og=og, ltot=ltot,
                 done=done)
    return y, saved, landed


def _sb_layer_bwd(dout, sv, tag, comm=None):
    dog = _matmul(dout, sv["w_out"], "nt", tag + "_dog")
    dw_out = _matmul(sv["og"], dout, "tn", tag + "_dwout", out_dtype=BF16)
    (dqn, dkn, dv, dgate), landed = _sb_attn_bwd(sv["qn"], sv["kn"], sv["vb"], dog, sv["o"], sv["ltot"], sv["done"],
                                                 sv["proj"], tag + "_attn_bwd", comm)
    dproj, dqgp, dkgp = _sb_prep_bwd(sv["proj"], dqn, dkn, dv, dgate, sv["qg2"], sv["kg2"], tag + "_prep_bwd")
    dw_in = _matmul(sv["h"], dproj, "tn", tag + "_dwin", out_dtype=BF16, blocked_out=N_DEV)
    dh = _matmul(dproj, sv["w_in"], "nt", tag + "_dh", blocked_b=True)
    dqg = _fold_heads(dqgp, tag + "_dqg")
    dkg = _fold_heads(dkgp, tag + "_dkg")
    return dh, dict(sb_w_in=dw_in, sb_q_norm_g=dqg, sb_k_norm_g=dkg, sb_w_out=dw_out), landed


def _sc_layer_fwd(h, w_in, conv_w, w_out, x_res, tag, next_g=None):
    proj = _matmul(h, w_in, "nn", tag + "_proj", blocked_b=True)
    yg = _sc_fwd(proj, conv_w, tag + "_fwd")
    y = _out_proj(yg, w_out, x_res, next_g, tag + "_out")
    return y, dict(h=h, w_in=w_in, conv_w=conv_w, w_out=w_out, proj=proj, yg=yg)


def _sc_layer_bwd(dout, sv, tag):
    dyg = _matmul(dout, sv["w_out"], "nt", tag + "_dyg")
    dw_out = _matmul(sv["yg"], dout, "tn", tag + "_dwout", out_dtype=BF16)
    dproj, dconv = _sc_bwd(dyg, sv["proj"], sv["conv_w"], tag + "_bwd")
    dw_in = _matmul(sv["h"], dproj, "tn", tag + "_dwin", out_dtype=BF16, blocked_out=N_DEV)
    dh = _matmul(dproj, sv["w_in"], "nt", tag + "_dh", blocked_b=True)
    return dh, dict(sc_w_in=dw_in, sc_conv_w=dconv, sc_w_out=dw_out)


def _adamw(w, m, v, parts, name):
    L, R, C = w.shape
    tr = _tile(R, 128, SUBLANE)

    def body(*refs):
        w_ref, m_ref, v_ref = refs[:3]
        g_ref, d_ref, nm_ref, nv_ref = refs[3 + L:]

        def update(p_ref):
            g = p_ref[0].astype(F32)
            for s in range(1, N_DEV):
                g = g + p_ref[s].astype(F32)
            m2 = ADAM_B1 * m_ref[...] + (1.0 - ADAM_B1) * g
            v2 = ADAM_B2 * v_ref[...] + (1.0 - ADAM_B2) * (g * g)
            m_hat = m2 / (1.0 - ADAM_B1 ** ADAM_STEP)
            v_hat = v2 / (1.0 - ADAM_B2 ** ADAM_STEP)
            g_ref[...] = g
            d_ref[...] = -ADAM_LR * (m_hat / (jnp.sqrt(v_hat) + ADAM_EPS) + ADAM_WD * w_ref[...])
            nm_ref[...] = m2
            nv_ref[...] = v2

        for layer in range(L):
            pl.when(pl.program_id(0) == layer)(functools.partial(update, refs[3 + layer]))

    blk = pl.BlockSpec((None, tr, C), lambda l, i: (l, i, 0))
    landing = pl.BlockSpec((N_DEV, tr, C), lambda l, i: (0, i, 0))
    return pl.pallas_call(
        body, name=name, grid=(L, R // tr),
        in_specs=[blk, blk, blk] + [landing] * L,
        out_specs=[blk] * 4, out_shape=[jax.ShapeDtypeStruct((L, R, C), F32)] * 4,
        compiler_params=_params("parallel", "parallel"),
    )(w, m, v, *parts)


_HBM = pl.BlockSpec(memory_space=pltpu.HBM)
_MESH = pl.DeviceIdType.MESH


def _slot(x, y, c):
    return 4 * x + 2 * y + c


class _Gather:
    def __init__(self, shards):
        self.arrays = list(shards)
        n = len(self.arrays)
        self.out_shapes = [jax.ShapeDtypeStruct((N_DEV,) + s.shape, s.dtype) for s in self.arrays]
        self.scratch = [pltpu.SemaphoreType.DMA((n, N_DEV - 1)), pltpu.SemaphoreType.DMA((n, N_DEV - 1)),
                        pltpu.SemaphoreType.DMA((n,))]

    def _parts(self, ins, outs, sems):
        send_sems, recv_sems, local_sems = sems
        n = len(self.arrays)
        x, y, c = lax.axis_index("x"), lax.axis_index("y"), lax.axis_index("c")
        me, sibling = (x, y, c), (x, y, 1 - c)
        chips = [(1 - x, y), (x, 1 - y), (1 - x, 1 - y)]

        def copy(a, k, block, to, src=None):
            dst = outs[a].at[_slot(*block)]
            return pltpu.make_async_remote_copy(src_ref=dst if src is None else src, dst_ref=dst,
                                                send_sem=send_sems.at[a, k], recv_sem=recv_sems.at[a, k],
                                                device_id=to, device_id_type=_MESH)

        mine = [pltpu.make_async_copy(ins[a], outs[a].at[_slot(*me)], local_sems.at[a]) for a in range(n)]
        first = []
        for a in range(n):
            first.append(copy(a, 0, me, sibling, src=ins[a]))
            first += [copy(a, 1 + j, me, (*chip, c), src=ins[a]) for j, chip in enumerate(chips)]
        return n, c, me, sibling, chips, copy, mine, first

    def start(self, ins, outs, sems):
        _, _, _, _, _, _, mine, first = self._parts(ins, outs, sems)
        for cp in mine + first:
            cp.start()

    def finish(self, ins, outs, sems):
        n, c, me, sibling, chips, copy, mine, first = self._parts(ins, outs, sems)
        passed = []
        for j, chip in enumerate(chips):
            for a in range(n):
                copy(a, 1 + j, (*chip, c), me).wait_recv()
                fwd = copy(a, 4 + j, (*chip, c), sibling)
                fwd.start()
                passed.append(fwd)
        for a in range(n):
            copy(a, 0, sibling, me).wait_recv()
            for j, chip in enumerate(chips):
                copy(a, 4 + j, (*chip, 1 - c), me).wait_recv()
        for cp in first + passed:
            cp.wait_send()
        for cp in mine:
            cp.wait()


class _Exchange:
    def __init__(self, arrays, scatter):
        self.arrays, self.scatter = list(arrays), list(scatter)
        n = len(self.arrays)
        shapes = [a.shape[1:] if s else a.shape for a, s in zip(self.arrays, self.scatter)]
        self.out_shapes = [jax.ShapeDtypeStruct((N_DEV,) + tuple(s), a.dtype) for s, a in zip(shapes, self.arrays)]
        self.scratch = [pltpu.SemaphoreType.DMA((n, N_DEV - 1)), pltpu.SemaphoreType.DMA((n, N_DEV - 1)),
                        pltpu.SemaphoreType.DMA((n,))]

    def _copies(self, ins, outs, sems):
        send_sems, recv_sems, local_sems = sems
        n, scatter = len(self.arrays), self.scatter
        x, y, c = lax.axis_index("x"), lax.axis_index("y"), lax.axis_index("c")
        me = _slot(x, y, c)
        copies = [pltpu.make_async_copy(ins[a].at[me] if scatter[a] else ins[a], outs[a].at[me], local_sems.at[a])
                  for a in range(n)]
        for r in range(1, N_DEV):
            px = 1 - x if r & 4 else x
            py = 1 - y if r & 2 else y
            pc = 1 - c if r & 1 else c
            for a in range(n):
                copies.append(pltpu.make_async_remote_copy(
                    src_ref=ins[a].at[_slot(px, py, pc)] if scatter[a] else ins[a], dst_ref=outs[a].at[me],
                    send_sem=send_sems.at[a, r - 1], recv_sem=recv_sems.at[a, r - 1],
                    device_id=(px, py, pc), device_id_type=_MESH))
        return copies

    def start(self, ins, outs, sems):
        for cp in self._copies(ins, outs, sems):
            cp.start()

    def finish(self, ins, outs, sems):
        for cp in self._copies(ins, outs, sems):
            cp.wait()


def _comm_call(comm, name):
    n = len(comm.arrays)

    def body(*refs):
        ins, outs, sems = refs[:n], refs[n:2 * n], refs[2 * n:]
        comm.start(ins, outs, sems)
        comm.finish(ins, outs, sems)

    return pl.pallas_call(body, name=name, in_specs=[_HBM] * n, out_specs=[_HBM] * n, out_shape=comm.out_shapes,
                          scratch_shapes=comm.scratch)(*comm.arrays)


def _call(body, comm, *, name, grid, in_specs, out_specs, out_shape, scratch_shapes, semantics, args):
    if comm is None:
        outs = pl.pallas_call(body, name=name, grid=grid, in_specs=in_specs, out_specs=out_specs, out_shape=out_shape,
                              scratch_shapes=scratch_shapes, compiler_params=_params(*semantics))(*args)
        return outs, []
    n_in, n_out, n_scr, n_c = len(in_specs), len(out_specs), len(scratch_shapes), len(comm.arrays)

    def fused(*refs):
        ins, refs = refs[:n_in], refs[n_in:]
        c_ins, refs = refs[:n_c], refs[n_c:]
        outs, refs = refs[:n_out], refs[n_out:]
        c_outs, refs = refs[:n_c], refs[n_c:]
        scr, sems = refs[:n_scr], refs[n_scr:]
        ids = [pl.program_id(d) for d in range(len(grid))]
        first = functools.reduce(jnp.logical_and, [i == 0 for i in ids])
        last = functools.reduce(jnp.logical_and, [i == g - 1 for i, g in zip(ids, grid)])

        @pl.when(first)
        def _():
            comm.start(c_ins, c_outs, sems)

        body(*ins, *outs, *scr)

        @pl.when(last)
        def _():
            comm.finish(c_ins, c_outs, sems)

    outs = pl.pallas_call(
        fused, name=name, grid=grid, in_specs=list(in_specs) + [_HBM] * n_c, out_specs=list(out_specs) + [_HBM] * n_c,
        out_shape=list(out_shape) + comm.out_shapes, scratch_shapes=list(scratch_shapes) + comm.scratch,
        compiler_params=_params(*["arbitrary"] * len(grid)))(*args, *comm.arrays)
    return outs[:n_out], outs[n_out:]


_GATHER_0 = (("dn_w_in", 0), ("dn_conv_w", 0), ("dn_o_norm_g", 0))
_GATHER_1 = (("dn_w_out", 0), ("sb_w_in", 0), ("sb_w_out", 0), ("sc_w_out", 0), ("dn_w_out", 1))
_GATHER_2 = (("sc_w_in", 0), ("sc_conv_w", 0), ("dn_w_in", 1), ("dn_conv_w", 1), ("dn_o_norm_g", 1))
_EXCHANGE_A = _GATHER_2
_EXCHANGE_B = (("sb_w_in", 0), ("sb_w_out", 0), ("dn_w_out", 0), ("sc_w_out", 0), ("dn_w_out", 1))
_EXCHANGE_C = _GATHER_0
_MATMUL_WEIGHTS = ("dn_w_in", "dn_w_out", "sb_w_in", "sb_w_out", "sc_w_in", "sc_w_out")
_COLUMN_SHARDED = ("dn_w_in", "dn_conv_w", "dn_o_norm_g", "sb_w_in", "sc_w_in", "sc_conv_w")
_BLOCKED = ("sb_w_in", "sc_w_in")
_REPLICATED = ("norm_g", "dn_a_log", "dn_dt_bias", "sb_q_norm_g", "sb_k_norm_g")
_ORDER = ("norm_g", "dn_w_in", "dn_conv_w", "dn_a_log", "dn_dt_bias", "dn_o_norm_g", "dn_w_out", "sb_w_in", "sb_q_norm_g",
          "sb_k_norm_g", "sb_w_out", "sc_w_in", "sc_conv_w", "sc_w_out")
_PACK_COLS = D_MODEL
_LOSS_SLOT = (4, 2 * DN_HEADS)


def _as_2d(a):
    return a.reshape(1, -1) if a.ndim == 1 else a


def _assemble(name, gathered):
    n, r, c = gathered.shape
    if name in _COLUMN_SHARDED:
        return jnp.moveaxis(gathered, 0, 1).reshape(r, n * c)
    return gathered.reshape(n * r, c)


def _disassemble(name, full):
    r, c = full.shape
    if name in _COLUMN_SHARDED:
        return jnp.moveaxis(full.reshape(r, N_DEV, c // N_DEV), 1, 0)
    return full.reshape(N_DEV, r // N_DEV, c)


def _pack_replicated(d):
    rows = [d["norm_g"]]
    for name in _REPLICATED[1:]:
        flat = d[name].reshape(1, -1)
        rows.append(jnp.pad(flat, ((0, 0), (0, _PACK_COLS - flat.shape[1]))))
    return jnp.concatenate(rows, axis=0)


def _unpack_replicated(p, like):
    out = {"norm_g": p[:4]}
    for r, name in enumerate(_REPLICATED[1:]):
        shape = like[name].shape
        out[name] = p[4 + r, :math.prod(shape)].reshape(shape)
    return out


def kernel(x, norm_g, dn_w_in, dn_conv_w, dn_a_log, dn_dt_bias, dn_o_norm_g, dn_w_out, sb_w_in, sb_q_norm_g, sb_k_norm_g, sb_w_out, sc_w_in, sc_conv_w, sc_w_out, loss_target, m_norm_g, m_dn_w_in, m_dn_conv_w, m_dn_a_log, m_dn_dt_bias, m_dn_o_norm_g, m_dn_w_out, m_sb_w_in, m_sb_q_norm_g, m_sb_k_norm_g, m_sb_w_out, m_sc_w_in, m_sc_conv_w, m_sc_w_out, v_norm_g, v_dn_w_in, v_dn_conv_w, v_dn_a_log, v_dn_dt_bias, v_dn_o_norm_g, v_dn_w_out, v_sb_w_in, v_sb_q_norm_g, v_sb_k_norm_g, v_sb_w_out, v_sc_w_in, v_sc_conv_w, v_sc_w_out):
    w = dict(norm_g=norm_g, dn_w_in=dn_w_in, dn_conv_w=dn_conv_w, dn_a_log=dn_a_log, dn_dt_bias=dn_dt_bias,
             dn_o_norm_g=dn_o_norm_g, dn_w_out=dn_w_out, sb_w_in=sb_w_in, sb_q_norm_g=sb_q_norm_g, sb_k_norm_g=sb_k_norm_g,
             sb_w_out=sb_w_out, sc_w_in=sc_w_in, sc_conv_w=sc_conv_w, sc_w_out=sc_w_out)
    m = dict(norm_g=m_norm_g, dn_w_in=m_dn_w_in, dn_conv_w=m_dn_conv_w, dn_a_log=m_dn_a_log, dn_dt_bias=m_dn_dt_bias,
             dn_o_norm_g=m_dn_o_norm_g, dn_w_out=m_dn_w_out, sb_w_in=m_sb_w_in, sb_q_norm_g=m_sb_q_norm_g,
             sb_k_norm_g=m_sb_k_norm_g, sb_w_out=m_sb_w_out, sc_w_in=m_sc_w_in, sc_conv_w=m_sc_conv_w, sc_w_out=m_sc_w_out)
    v = dict(norm_g=v_norm_g, dn_w_in=v_dn_w_in, dn_conv_w=v_dn_conv_w, dn_a_log=v_dn_a_log, dn_dt_bias=v_dn_dt_bias,
             dn_o_norm_g=v_dn_o_norm_g, dn_w_out=v_dn_w_out, sb_w_in=v_sb_w_in, sb_q_norm_g=v_sb_q_norm_g,
             sb_k_norm_g=v_sb_k_norm_g, sb_w_out=v_sb_w_out, sc_w_in=v_sc_w_in, sc_conv_w=v_sc_conv_w, sc_w_out=v_sc_w_out)

    def gather_of(keys):
        return _Gather([_as_2d(w[k][j]).astype(BF16) if k in _MATMUL_WEIGHTS else _as_2d(w[k][j]) for k, j in keys])

    def full_weights(keys, gathered):
        return {key: g if key[0] in _BLOCKED else _assemble(key[0], g) for key, g in zip(keys, gathered)}

    def exchange_of(keys, grads, extra=()):
        out = [grads[k, j] if k in _BLOCKED else
               _disassemble(k, grads[k, j].astype(BF16) if k in _MATMUL_WEIGHTS else grads[k, j]) for k, j in keys]
        return _Exchange(out + list(extra), [True] * len(out) + [False] * len(extra))

    xs, saves = [x[0]], []
    h, got = _rmsnorm_fwd(xs[0], norm_g[0:1], "norm0", gather_of(_GATHER_0))
    F = full_weights(_GATHER_0, got)

    def w_out_0(got):
        F.update(full_weights(_GATHER_1, got))
        return F["dn_w_out", 0]

    (y, h), sv, _ = _dn_layer_fwd(h, _dn_split_w_in(F["dn_w_in", 0]), F["dn_conv_w", 0], dn_a_log[0:1], dn_dt_bias[0:1],
                                  F["dn_o_norm_g", 0], w_out_0, xs[0], "dn0", gather_of(_GATHER_1), norm_g[1:2])
    xs.append(y)
    saves.append(sv)
    (y, h), sv, got = _sb_layer_fwd(h, F["sb_w_in", 0], sb_q_norm_g, sb_k_norm_g, F["sb_w_out", 0], xs[1], "sb",
                                    gather_of(_GATHER_2), norm_g[2:3])
    F.update(full_weights(_GATHER_2, got))
    xs.append(y)
    saves.append(sv)
    (y, h), sv = _sc_layer_fwd(h, F["sc_w_in", 0], F["sc_conv_w", 0], F["sc_w_out", 0], xs[2], "sc", norm_g[3:4])
    xs.append(y)
    saves.append(sv)
    (dx, loss_part), sv, _ = _dn_layer_fwd(h, _dn_split_w_in(F["dn_w_in", 1]), F["dn_conv_w", 1], dn_a_log[1:2],
                                           dn_dt_bias[1:2], F["dn_o_norm_g", 1], F["dn_w_out", 1], xs[3], "dn1",
                                           loss_target=loss_target[0])
    saves.append(sv)

    G, dnorm, landed = {}, [None] * 4, {}

    def keep(grads, j):
        G.update({(k, j): g for k, g in grads.items()})

    dh, grads, _, _ = _dn_layer_bwd(dx, saves[3], "dn1", None)
    keep(grads, 1)
    dx, dnorm[3] = _rmsnorm_bwd(dh, xs[3], norm_g[3:4], dx, "norm3_bwd")
    dh, grads = _sc_layer_bwd(dx, saves[2], "sc")
    keep(grads, 0)
    dx, dnorm[2] = _rmsnorm_bwd(dh, xs[2], norm_g[2:3], dx, "norm2_bwd")
    dh, grads, got = _sb_layer_bwd(dx, saves[1], "sb", exchange_of(_EXCHANGE_A, G))
    keep(grads, 0)
    landed.update(zip(_EXCHANGE_A, got))
    dx, dnorm[1] = _rmsnorm_bwd(dh, xs[1], norm_g[1:2], dx, "norm1_bwd")

    def exchange_b(dw_out):
        G["dn_w_out", 0] = dw_out
        return exchange_of(_EXCHANGE_B, G)

    def exchange_c(grads):
        keep(grads, 0)
        return exchange_of(_EXCHANGE_C, G)

    (dx, dnorm[0]), grads, got, got_late = _dn_layer_bwd(dx, saves[0], "dn0", (xs[0], norm_g[0:1], dx), exchange_b, exchange_c)
    landed.update(zip(_EXCHANGE_B, got))
    landed.update(zip(_EXCHANGE_C, got_late))
    replicated = dict(norm_g=jnp.concatenate(dnorm, axis=0),
                      dn_a_log=jnp.concatenate([G["dn_a_log", 0], G["dn_a_log", 1]], axis=0),
                      dn_dt_bias=jnp.concatenate([G["dn_dt_bias", 0], G["dn_dt_bias", 1]], axis=0),
                      sb_q_norm_g=G["sb_q_norm_g", 0], sb_k_norm_g=G["sb_k_norm_g", 0])
    pack = _pack_replicated(replicated).at[_LOSS_SLOT].set(loss_part[0, 0])
    got = _comm_call(_Exchange([pack], [False]), "exchange_replicated")

    res = {}
    for k in _ORDER:
        if k in _REPLICATED:
            continue
        shape = w[k].shape
        as_3d = lambda a: a.reshape(shape[0], math.prod(shape[1:-1]), shape[-1])
        outs = _adamw(as_3d(w[k]), as_3d(m[k]), as_3d(v[k]), [landed[k, j] for j in range(shape[0])], "adamw_" + k)
        res[k] = [o.reshape(shape) for o in outs]
    outs = _adamw(_pack_replicated(w)[None], _pack_replicated(m)[None], _pack_replicated(v)[None], [got[-1]],
                  "adamw_replicated")
    unpacked = [_unpack_replicated(o[0], w) for o in outs]
    for k in _REPLICATED:
        res[k] = [u[k] for u in unpacked]

    loss = outs[0][0][_LOSS_SLOT]
    return (loss, dx[None]) + tuple(res[k][0] for k in _ORDER) + tuple(res[k][1] for k in _ORDER) \
        + tuple(res[k][2] for k in _ORDER) + tuple(res[k][3] for k in _ORDER)
```
